```python
import jax, jax.numpy as jnp
from jax import lax
import numpy as np

D_MODEL = 2048
BATCH = 8
SEQ = 4096
DEPTH = 2

MIX_WIDTH = D_MODEL
MLA_HEADS = D_MODEL // 256
QK_NOPE_DIM = 128
QK_ROPE_DIM = 64
QK_HEAD_DIM = QK_NOPE_DIM + QK_ROPE_DIM
V_HEAD_DIM = 128
Q_LORA_RANK = 512
KV_LORA_RANK = 256
ATTN_WIDTH = MLA_HEADS * V_HEAD_DIM
GM_WIDTH = MIX_WIDTH - ATTN_WIDTH
GM_GROUPS = D_MODEL // 256
GM_GROUP_DIM = GM_WIDTH // GM_GROUPS
CHUNK = 128
D_FF = 128 * ((8 * D_MODEL // 3 + 127) // 128)
PLE_DIM = 256
ROPE_BASE = 10000.0
EPS = 1e-6
Q_BLOCK = 128
IN_COLS = Q_LORA_RANK + KV_LORA_RANK + QK_ROPE_DIM + 2 * GM_WIDTH
SPLITS = (Q_LORA_RANK,
          Q_LORA_RANK + KV_LORA_RANK,
          Q_LORA_RANK + KV_LORA_RANK + QK_ROPE_DIM,
          Q_LORA_RANK + KV_LORA_RANK + QK_ROPE_DIM + GM_WIDTH)

kernel_name = "hybrid_mla_gmlp_macaron_ple"


def rms_norm(x, g):
    xf = x.astype(jnp.float32)
    y = xf * lax.rsqrt(jnp.mean(xf * xf, axis=-1, keepdims=True) + EPS)
    return (y * g.astype(jnp.float32)).astype(x.dtype)


def swiglu(x, w1, w3, w2):
    return (jax.nn.silu(x @ w1) * (x @ w3)) @ w2


def rope_tables(positions):
    inv_freq = ROPE_BASE ** (-jnp.arange(0, QK_ROPE_DIM, 2, dtype=jnp.float32) / QK_ROPE_DIM)
    ang = positions.astype(jnp.float32)[..., None] * inv_freq
    return jnp.cos(ang)[:, :, None, :], jnp.sin(ang)[:, :, None, :]


def apply_rope(x, cos, sin):
    x1, x2 = jnp.split(x.astype(jnp.float32), 2, axis=-1)
    out = jnp.concatenate([x1 * cos - x2 * sin, x2 * cos + x1 * sin], axis=-1)
    return out.astype(x.dtype)


def causal_block_attention(q, k, v):
    b, s, h, dqk = q.shape
    nb = s // Q_BLOCK
    scale = dqk ** -0.5
    qb = q.reshape(b, nb, Q_BLOCK, h, dqk).transpose(1, 0, 2, 3, 4)
    key_pos = jnp.arange(s)
    neg = jnp.finfo(jnp.float32).min

    def one_block(args):
        q_blk, blk = args
        q_pos = blk * Q_BLOCK + jnp.arange(Q_BLOCK)
        scores = jnp.einsum('bqhd,bkhd->bhqk', q_blk, k,
                            preferred_element_type=jnp.float32) * scale
        mask = key_pos[None, :] <= q_pos[:, None]
        scores = jnp.where(mask[None, None], scores, neg)
        probs = jax.nn.softmax(scores, axis=-1).astype(v.dtype)
        return jnp.einsum('bhqk,bkhd->bqhd', probs, v)

    out = lax.map(one_block, (qb, jnp.arange(nb)))
    return out.transpose(1, 0, 2, 3, 4).reshape(b, s, h, v.shape[-1])


def mla_mixer(c_q, c_kv, k_rope_raw, cos, sin, q_a_norm, w_uq, kv_a_norm, w_ukv, q_norm, k_norm):
    b, s, _ = c_q.shape
    q = (rms_norm(c_q, q_a_norm) @ w_uq).reshape(b, s, MLA_HEADS, QK_HEAD_DIM)
    kv = (rms_norm(c_kv, kv_a_norm) @ w_ukv).reshape(b, s, MLA_HEADS, QK_NOPE_DIM + V_HEAD_DIM)
    k_nope, v = jnp.split(kv, [QK_NOPE_DIM], axis=-1)
    k_rope = jnp.broadcast_to(k_rope_raw[:, :, None, :], (b, s, MLA_HEADS, QK_ROPE_DIM))
    k = jnp.concatenate([k_nope, k_rope], axis=-1)
    q = rms_norm(q, q_norm)
    k = rms_norm(k, k_norm)
    q = jnp.concatenate([q[..., :QK_NOPE_DIM], apply_rope(q[..., QK_NOPE_DIM:], cos, sin)], axis=-1)
    k = jnp.concatenate([k[..., :QK_NOPE_DIM], apply_rope(k[..., QK_NOPE_DIM:], cos, sin)], axis=-1)
    return causal_block_attention(q, k, v).reshape(b, s, ATTN_WIDTH)


def gmlp_mixer(u, v, v_norm, w_s, b_s):
    b, s, _ = u.shape
    u = jax.nn.gelu(u)
    v = rms_norm(jax.nn.gelu(v), v_norm)
    vc = v.reshape(b, s // CHUNK, CHUNK, GM_GROUPS, GM_GROUP_DIM)
    tril = jnp.tril(jnp.ones((CHUNK, CHUNK), dtype=bool))
    w_causal = jnp.where(tril[None], w_s, jnp.zeros_like(w_s))
    gate = jnp.einsum('gts,bcsgd->bctgd', w_causal, vc) + b_s.T[None, None, :, :, None]
    return u * gate.reshape(b, s, GM_WIDTH)


def _fwd_setup_inputs(seed: int = 0) -> dict:
    key = jax.random.key(seed)
    ks = jax.random.split(key, 32)
    f32 = jnp.float32

    def w(k, shape, fan_in):
        return jax.random.normal(k, shape, f32) * fan_in ** -0.5

    def g(k, shape):
        return 1.0 + 0.05 * jax.random.normal(k, shape, f32)

    L = DEPTH
    offsets = jax.random.randint(ks[2], (BATCH, 1), 0, 1024, dtype=jnp.int32)
    positions = (offsets + jnp.arange(SEQ, dtype=jnp.int32)[None, :]).astype(jnp.int32)
    return {
        "x": jax.random.normal(ks[0], (BATCH, SEQ, D_MODEL), f32),
        "p": jax.random.normal(ks[1], (DEPTH, BATCH, SEQ, PLE_DIM), f32),
        "positions": positions,
        "ffn_a_norm": g(ks[3], (L, D_MODEL)),
        "ffn_a_w1": w(ks[4], (L, D_MODEL, D_FF), D_MODEL),
        "ffn_a_w3": w(ks[5], (L, D_MODEL, D_FF), D_MODEL),
        "ffn_a_w2": w(ks[6], (L, D_FF, D_MODEL), D_FF),
        "mix_norm": g(ks[7], (L, D_MODEL)),
        "w_in": w(ks[8], (L, D_MODEL, IN_COLS), D_MODEL),
        "q_a_norm": g(ks[9], (L, Q_LORA_RANK)),
        "w_uq": w(ks[10], (L, Q_LORA_RANK, MLA_HEADS * QK_HEAD_DIM), Q_LORA_RANK),
        "kv_a_norm": g(ks[11], (L, KV_LORA_RANK)),
        "w_ukv": w(ks[12], (L, KV_LORA_RANK, MLA_HEADS * (QK_NOPE_DIM + V_HEAD_DIM)), KV_LORA_RANK),
        "q_norm": g(ks[13], (L, QK_HEAD_DIM)),
        "k_norm": g(ks[14], (L, QK_HEAD_DIM)),
        "gm_v_norm": g(ks[15], (L, GM_WIDTH)),
        "gm_ws": w(ks[16], (L, GM_GROUPS, CHUNK, CHUNK), CHUNK),
        "gm_bs": 1.0 + 0.1 * jax.random.normal(ks[17], (L, GM_GROUPS, CHUNK), f32),
        "attn_out_norm": g(ks[18], (L, ATTN_WIDTH)),
        "gm_out_norm": g(ks[19], (L, GM_WIDTH)),
        "w_out": w(ks[20], (L, MIX_WIDTH, D_MODEL), MIX_WIDTH),
        "ffn_b_norm": g(ks[21], (L, D_MODEL)),
        "ffn_b_w1": w(ks[22], (L, D_MODEL, D_FF), D_MODEL),
        "ffn_b_w3": w(ks[23], (L, D_MODEL, D_FF), D_MODEL),
        "ffn_b_w2": w(ks[24], (L, D_FF, D_MODEL), D_FF),
        "ple_gate_norm": g(ks[25], (L, D_MODEL)),
        "w_ple_gate": w(ks[26], (L, D_MODEL, D_MODEL), D_MODEL),
        "w_ple": w(ks[27], (L, PLE_DIM, D_MODEL), PLE_DIM),
        "ple_norm": g(ks[28], (L, D_MODEL)),
    }


def _fwd_reference(x, p, positions, ffn_a_norm, ffn_a_w1, ffn_a_w3, ffn_a_w2, mix_norm, w_in,
              q_a_norm, w_uq, kv_a_norm, w_ukv, q_norm, k_norm, gm_v_norm, gm_ws, gm_bs,
              attn_out_norm, gm_out_norm, w_out, ffn_b_norm, ffn_b_w1, ffn_b_w3, ffn_b_w2,
              ple_gate_norm, w_ple_gate, w_ple, ple_norm):
    cos, sin = rope_tables(positions)
    h = x
    for i in range(DEPTH):
        h = h + 0.5 * swiglu(rms_norm(h, ffn_a_norm[i]), ffn_a_w1[i], ffn_a_w3[i], ffn_a_w2[i])
        z = rms_norm(h, mix_norm[i]) @ w_in[i]
        c_q, c_kv, k_rope_raw, u, v = jnp.split(z, SPLITS, axis=-1)
        a_out = mla_mixer(c_q, c_kv, k_rope_raw, cos, sin, q_a_norm[i], w_uq[i],
                          kv_a_norm[i], w_ukv[i], q_norm[i], k_norm[i])
        g_out = gmlp_mixer(u, v, gm_v_norm[i], gm_ws[i], gm_bs[i])
        mixed = jnp.concatenate([rms_norm(a_out, attn_out_norm[i]),
                                 rms_norm(g_out, gm_out_norm[i])], axis=-1)
        h = h + mixed @ w_out[i]
        h = h + 0.5 * swiglu(rms_norm(h, ffn_b_norm[i]), ffn_b_w1[i], ffn_b_w3[i], ffn_b_w2[i])
        e = rms_norm(p[i] @ w_ple[i], ple_norm[i])
        gate = jax.nn.sigmoid(rms_norm(h, ple_gate_norm[i]) @ w_ple_gate[i])
        h = h + gate * e
    return h


import jax as _jax
import jax.numpy as _jnp

TWIN_FORMAT = 'train_step'
FWD_PARAMS = ['x', 'p', 'positions', 'ffn_a_norm', 'ffn_a_w1', 'ffn_a_w3', 'ffn_a_w2', 'mix_norm', 'w_in', 'q_a_norm', 'w_uq', 'kv_a_norm', 'w_ukv', 'q_norm', 'k_norm', 'gm_v_norm', 'gm_ws', 'gm_bs', 'attn_out_norm', 'gm_out_norm', 'w_out', 'ffn_b_norm', 'ffn_b_w1', 'ffn_b_w3', 'ffn_b_w2', 'ple_gate_norm', 'w_ple_gate', 'w_ple', 'ple_norm']
TWIN_WEIGHTS = ['ffn_a_norm', 'ffn_a_w1', 'ffn_a_w3', 'ffn_a_w2', 'mix_norm', 'w_in', 'q_a_norm', 'w_uq', 'kv_a_norm', 'w_ukv', 'q_norm', 'k_norm', 'gm_v_norm', 'gm_ws', 'gm_bs', 'attn_out_norm', 'gm_out_norm', 'w_out', 'ffn_b_norm', 'ffn_b_w1', 'ffn_b_w3', 'ffn_b_w2', 'ple_gate_norm', 'w_ple_gate', 'w_ple', 'ple_norm']
TWIN_DIFF_INPUT = 'x'
TWIN_INPUTS = ['x', 'p', 'positions', 'ffn_a_norm', 'ffn_a_w1', 'ffn_a_w3', 'ffn_a_w2', 'mix_norm', 'w_in', 'q_a_norm', 'w_uq', 'kv_a_norm', 'w_ukv', 'q_norm', 'k_norm', 'gm_v_norm', 'gm_ws', 'gm_bs', 'attn_out_norm', 'gm_out_norm', 'w_out', 'ffn_b_norm', 'ffn_b_w1', 'ffn_b_w3', 'ffn_b_w2', 'ple_gate_norm', 'w_ple_gate', 'w_ple', 'ple_norm', 'loss_target', 'm_ffn_a_norm', 'm_ffn_a_w1', 'm_ffn_a_w3', 'm_ffn_a_w2', 'm_mix_norm', 'm_w_in', 'm_q_a_norm', 'm_w_uq', 'm_kv_a_norm', 'm_w_ukv', 'm_q_norm', 'm_k_norm', 'm_gm_v_norm', 'm_gm_ws', 'm_gm_bs', 'm_attn_out_norm', 'm_gm_out_norm', 'm_w_out', 'm_ffn_b_norm', 'm_ffn_b_w1', 'm_ffn_b_w3', 'm_ffn_b_w2', 'm_ple_gate_norm', 'm_w_ple_gate', 'm_w_ple', 'm_ple_norm', 'v_ffn_a_norm', 'v_ffn_a_w1', 'v_ffn_a_w3', 'v_ffn_a_w2', 'v_mix_norm', 'v_w_in', 'v_q_a_norm', 'v_w_uq', 'v_kv_a_norm', 'v_w_ukv', 'v_q_norm', 'v_k_norm', 'v_gm_v_norm', 'v_gm_ws', 'v_gm_bs', 'v_attn_out_norm', 'v_gm_out_norm', 'v_w_out', 'v_ffn_b_norm', 'v_ffn_b_w1', 'v_ffn_b_w3', 'v_ffn_b_w2', 'v_ple_gate_norm', 'v_w_ple_gate', 'v_w_ple', 'v_ple_norm']
TWIN_OUTPUTS = ['loss', 'grad_x', 'grad_ffn_a_norm', 'grad_ffn_a_w1', 'grad_ffn_a_w3', 'grad_ffn_a_w2', 'grad_mix_norm', 'grad_w_in', 'grad_q_a_norm', 'grad_w_uq', 'grad_kv_a_norm', 'grad_w_ukv', 'grad_q_norm', 'grad_k_norm', 'grad_gm_v_norm', 'grad_gm_ws', 'grad_gm_bs', 'grad_attn_out_norm', 'grad_gm_out_norm', 'grad_w_out', 'grad_ffn_b_norm', 'grad_ffn_b_w1', 'grad_ffn_b_w3', 'grad_ffn_b_w2', 'grad_ple_gate_norm', 'grad_w_ple_gate', 'grad_w_ple', 'grad_ple_norm', 'delta_ffn_a_norm', 'delta_ffn_a_w1', 'delta_ffn_a_w3', 'delta_ffn_a_w2', 'delta_mix_norm', 'delta_w_in', 'delta_q_a_norm', 'delta_w_uq', 'delta_kv_a_norm', 'delta_w_ukv', 'delta_q_norm', 'delta_k_norm', 'delta_gm_v_norm', 'delta_gm_ws', 'delta_gm_bs', 'delta_attn_out_norm', 'delta_gm_out_norm', 'delta_w_out', 'delta_ffn_b_norm', 'delta_ffn_b_w1', 'delta_ffn_b_w3', 'delta_ffn_b_w2', 'delta_ple_gate_norm', 'delta_w_ple_gate', 'delta_w_ple', 'delta_ple_norm', 'new_m_ffn_a_norm', 'new_m_ffn_a_w1', 'new_m_ffn_a_w3', 'new_m_ffn_a_w2', 'new_m_mix_norm', 'new_m_w_in', 'new_m_q_a_norm', 'new_m_w_uq', 'new_m_kv_a_norm', 'new_m_w_ukv', 'new_m_q_norm', 'new_m_k_norm', 'new_m_gm_v_norm', 'new_m_gm_ws', 'new_m_gm_bs', 'new_m_attn_out_norm', 'new_m_gm_out_norm', 'new_m_w_out', 'new_m_ffn_b_norm', 'new_m_ffn_b_w1', 'new_m_ffn_b_w3', 'new_m_ffn_b_w2', 'new_m_ple_gate_norm', 'new_m_w_ple_gate', 'new_m_w_ple', 'new_m_ple_norm', 'new_v_ffn_a_norm', 'new_v_ffn_a_w1', 'new_v_ffn_a_w3', 'new_v_ffn_a_w2', 'new_v_mix_norm', 'new_v_w_in', 'new_v_q_a_norm', 'new_v_w_uq', 'new_v_kv_a_norm', 'new_v_w_ukv', 'new_v_q_norm', 'new_v_k_norm', 'new_v_gm_v_norm', 'new_v_gm_ws', 'new_v_gm_bs', 'new_v_attn_out_norm', 'new_v_gm_out_norm', 'new_v_w_out', 'new_v_ffn_b_norm', 'new_v_ffn_b_w1', 'new_v_ffn_b_w3', 'new_v_ffn_b_w2', 'new_v_ple_gate_norm', 'new_v_w_ple_gate', 'new_v_w_ple', 'new_v_ple_norm']
TWIN_LEAF_KINDS = {'loss': 'loss', 'grad_x': 'grad_x', 'grad_ffn_a_norm': 'grad_w', 'grad_ffn_a_w1': 'grad_w', 'grad_ffn_a_w3': 'grad_w', 'grad_ffn_a_w2': 'grad_w', 'grad_mix_norm': 'grad_w', 'grad_w_in': 'grad_w', 'grad_q_a_norm': 'grad_w', 'grad_w_uq': 'grad_w', 'grad_kv_a_norm': 'grad_w', 'grad_w_ukv': 'grad_w', 'grad_q_norm': 'grad_w', 'grad_k_norm': 'grad_w', 'grad_gm_v_norm': 'grad_w', 'grad_gm_ws': 'grad_w', 'grad_gm_bs': 'grad_w', 'grad_attn_out_norm': 'grad_w', 'grad_gm_out_norm': 'grad_w', 'grad_w_out': 'grad_w', 'grad_ffn_b_norm': 'grad_w', 'grad_ffn_b_w1': 'grad_w', 'grad_ffn_b_w3': 'grad_w', 'grad_ffn_b_w2': 'grad_w', 'grad_ple_gate_norm': 'grad_w', 'grad_w_ple_gate': 'grad_w', 'grad_w_ple': 'grad_w', 'grad_ple_norm': 'grad_w', 'delta_ffn_a_norm': 'delta_w', 'delta_ffn_a_w1': 'delta_w', 'delta_ffn_a_w3': 'delta_w', 'delta_ffn_a_w2': 'delta_w', 'delta_mix_norm': 'delta_w', 'delta_w_in': 'delta_w', 'delta_q_a_norm': 'delta_w', 'delta_w_uq': 'delta_w', 'delta_kv_a_norm': 'delta_w', 'delta_w_ukv': 'delta_w', 'delta_q_norm': 'delta_w', 'delta_k_norm': 'delta_w', 'delta_gm_v_norm': 'delta_w', 'delta_gm_ws': 'delta_w', 'delta_gm_bs': 'delta_w', 'delta_attn_out_norm': 'delta_w', 'delta_gm_out_norm': 'delta_w', 'delta_w_out': 'delta_w', 'delta_ffn_b_norm': 'delta_w', 'delta_ffn_b_w1': 'delta_w', 'delta_ffn_b_w3': 'delta_w', 'delta_ffn_b_w2': 'delta_w', 'delta_ple_gate_norm': 'delta_w', 'delta_w_ple_gate': 'delta_w', 'delta_w_ple': 'delta_w', 'delta_ple_norm': 'delta_w', 'new_m_ffn_a_norm': 'new_m', 'new_m_ffn_a_w1': 'new_m', 'new_m_ffn_a_w3': 'new_m', 'new_m_ffn_a_w2': 'new_m', 'new_m_mix_norm': 'new_m', 'new_m_w_in': 'new_m', 'new_m_q_a_norm': 'new_m', 'new_m_w_uq': 'new_m', 'new_m_kv_a_norm': 'new_m', 'new_m_w_ukv': 'new_m', 'new_m_q_norm': 'new_m', 'new_m_k_norm': 'new_m', 'new_m_gm_v_norm': 'new_m', 'new_m_gm_ws': 'new_m', 'new_m_gm_bs': 'new_m', 'new_m_attn_out_norm': 'new_m', 'new_m_gm_out_norm': 'new_m', 'new_m_w_out': 'new_m', 'new_m_ffn_b_norm': 'new_m', 'new_m_ffn_b_w1': 'new_m', 'new_m_ffn_b_w3': 'new_m', 'new_m_ffn_b_w2': 'new_m', 'new_m_ple_gate_norm': 'new_m', 'new_m_w_ple_gate': 'new_m', 'new_m_w_ple': 'new_m', 'new_m_ple_norm': 'new_m', 'new_v_ffn_a_norm': 'new_v', 'new_v_ffn_a_w1': 'new_v', 'new_v_ffn_a_w3': 'new_v', 'new_v_ffn_a_w2': 'new_v', 'new_v_mix_norm': 'new_v', 'new_v_w_in': 'new_v', 'new_v_q_a_norm': 'new_v', 'new_v_w_uq': 'new_v', 'new_v_kv_a_norm': 'new_v', 'new_v_w_ukv': 'new_v', 'new_v_q_norm': 'new_v', 'new_v_k_norm': 'new_v', 'new_v_gm_v_norm': 'new_v', 'new_v_gm_ws': 'new_v', 'new_v_gm_bs': 'new_v', 'new_v_attn_out_norm': 'new_v', 'new_v_gm_out_norm': 'new_v', 'new_v_w_out': 'new_v', 'new_v_ffn_b_norm': 'new_v', 'new_v_ffn_b_w1': 'new_v', 'new_v_ffn_b_w3': 'new_v', 'new_v_ffn_b_w2': 'new_v', 'new_v_ple_gate_norm': 'new_v', 'new_v_w_ple_gate': 'new_v', 'new_v_w_ple': 'new_v', 'new_v_ple_norm': 'new_v'}


def _forward(args):
    return _fwd_reference(*[args[k] for k in FWD_PARAMS])


def _output_shape():
    out = _jax.eval_shape(lambda: _forward(_fwd_setup_inputs(0)))
    return out.shape, out.dtype

N_MICROBATCH = 1
ADAM_LR = 0.001
ADAM_B1 = 0.9
ADAM_B2 = 0.999
ADAM_EPS = 1e-08
ADAM_WD = 0.01
ADAM_STEP = 10
PER_EXAMPLE_BATCH_AXIS = {'x': 0, 'p': 1, 'positions': 0, 'loss_target': 0}
SHARED_INPUTS = []
_WEIGHT_DTYPES = {'ffn_a_norm': _jnp.float32, 'ffn_a_w1': _jnp.float32, 'ffn_a_w3': _jnp.float32, 'ffn_a_w2': _jnp.float32, 'mix_norm': _jnp.float32, 'w_in': _jnp.float32, 'q_a_norm': _jnp.float32, 'w_uq': _jnp.float32, 'kv_a_norm': _jnp.float32, 'w_ukv': _jnp.float32, 'q_norm': _jnp.float32, 'k_norm': _jnp.float32, 'gm_v_norm': _jnp.float32, 'gm_ws': _jnp.float32, 'gm_bs': _jnp.float32, 'attn_out_norm': _jnp.float32, 'gm_out_norm': _jnp.float32, 'w_out': _jnp.float32, 'ffn_b_norm': _jnp.float32, 'ffn_b_w1': _jnp.float32, 'ffn_b_w3': _jnp.float32, 'ffn_b_w2': _jnp.float32, 'ple_gate_norm': _jnp.float32, 'w_ple_gate': _jnp.float32, 'w_ple': _jnp.float32, 'ple_norm': _jnp.float32}
MOMENT_SCALE = {'ffn_a_norm': 3.436602e+00, 'ffn_a_w1': 4.788921e-01, 'ffn_a_w3': 5.002466e-01, 'ffn_a_w2': 8.050034e-01, 'mix_norm': 4.234702e+00, 'w_in': 3.637660e+00, 'q_a_norm': 5.742858e-01, 'w_uq': 2.993803e-01, 'kv_a_norm': 1.259302e+01, 'w_ukv': 4.550384e+00, 'q_norm': 1.303791e+00, 'k_norm': 1.382350e+00, 'gm_v_norm': 4.008014e-01, 'gm_ws': 1.767952e-01, 'gm_bs': 4.349977e-01, 'attn_out_norm': 1.834291e+01, 'gm_out_norm': 1.722887e+01, 'w_out': 5.948643e+00, 'ffn_b_norm': 3.199023e+00, 'ffn_b_w1': 1.855242e-01, 'ffn_b_w3': 2.727686e-01, 'ffn_b_w2': 4.493331e-01, 'ple_gate_norm': 6.302049e-01, 'w_ple_gate': 3.876429e-01, 'w_ple': 1.543688e-01, 'ple_norm': 4.751459e+00}


def _to_microbatches(a, axis):
    t = _jnp.moveaxis(a, axis, 0)
    t = t.reshape((N_MICROBATCH, t.shape[0] // N_MICROBATCH) + t.shape[1:])
    return _jnp.moveaxis(t, 1, axis + 1)


def setup_inputs(seed: int = 0) -> dict:
    inp = _fwd_setup_inputs(seed)
    key = _jax.random.fold_in(_jax.random.key(seed), 7919)
    shape, _ = _output_shape()
    out = dict(inp)
    out["loss_target"] = _jax.random.normal(_jax.random.fold_in(key, 0), shape, _jnp.float32)
    for i, name in enumerate(TWIN_WEIGHTS):
        w = inp[name].astype(_jnp.float32)
        if MOMENT_SCALE is None:
            s = _jnp.sqrt(_jnp.mean(_jnp.square(w)) + 1e-30)
        else:
            s = MOMENT_SCALE[name]
        km, kv = _jax.random.split(_jax.random.fold_in(key, i + 1))
        out[name] = w
        out["m_" + name] = s * _jax.random.normal(km, w.shape, _jnp.float32)
        out["v_" + name] = (s * s) * _jax.random.uniform(kv, w.shape, _jnp.float32, 0.5, 1.5)
    if N_MICROBATCH > 1:
        for name, axis in PER_EXAMPLE_BATCH_AXIS.items():
            out[name] = _to_microbatches(out[name], axis)
    return {'x': out['x'], 'p': out['p'], 'positions': out['positions'], 'ffn_a_norm': out['ffn_a_norm'], 'ffn_a_w1': out['ffn_a_w1'], 'ffn_a_w3': out['ffn_a_w3'], 'ffn_a_w2': out['ffn_a_w2'], 'mix_norm': out['mix_norm'], 'w_in': out['w_in'], 'q_a_norm': out['q_a_norm'], 'w_uq': out['w_uq'], 'kv_a_norm': out['kv_a_norm'], 'w_ukv': out['w_ukv'], 'q_norm': out['q_norm'], 'k_norm': out['k_norm'], 'gm_v_norm': out['gm_v_norm'], 'gm_ws': out['gm_ws'], 'gm_bs': out['gm_bs'], 'attn_out_norm': out['attn_out_norm'], 'gm_out_norm': out['gm_out_norm'], 'w_out': out['w_out'], 'ffn_b_norm': out['ffn_b_norm'], 'ffn_b_w1': out['ffn_b_w1'], 'ffn_b_w3': out['ffn_b_w3'], 'ffn_b_w2': out['ffn_b_w2'], 'ple_gate_norm': out['ple_gate_norm'], 'w_ple_gate': out['w_ple_gate'], 'w_ple': out['w_ple'], 'ple_norm': out['ple_norm'], 'loss_target': out['loss_target'], 'm_ffn_a_norm': out['m_ffn_a_norm'], 'm_ffn_a_w1': out['m_ffn_a_w1'], 'm_ffn_a_w3': out['m_ffn_a_w3'], 'm_ffn_a_w2': out['m_ffn_a_w2'], 'm_mix_norm': out['m_mix_norm'], 'm_w_in': out['m_w_in'], 'm_q_a_norm': out['m_q_a_norm'], 'm_w_uq': out['m_w_uq'], 'm_kv_a_norm': out['m_kv_a_norm'], 'm_w_ukv': out['m_w_ukv'], 'm_q_norm': out['m_q_norm'], 'm_k_norm': out['m_k_norm'], 'm_gm_v_norm': out['m_gm_v_norm'], 'm_gm_ws': out['m_gm_ws'], 'm_gm_bs': out['m_gm_bs'], 'm_attn_out_norm': out['m_attn_out_norm'], 'm_gm_out_norm': out['m_gm_out_norm'], 'm_w_out': out['m_w_out'], 'm_ffn_b_norm': out['m_ffn_b_norm'], 'm_ffn_b_w1': out['m_ffn_b_w1'], 'm_ffn_b_w3': out['m_ffn_b_w3'], 'm_ffn_b_w2': out['m_ffn_b_w2'], 'm_ple_gate_norm': out['m_ple_gate_norm'], 'm_w_ple_gate': out['m_w_ple_gate'], 'm_w_ple': out['m_w_ple'], 'm_ple_norm': out['m_ple_norm'], 'v_ffn_a_norm': out['v_ffn_a_norm'], 'v_ffn_a_w1': out['v_ffn_a_w1'], 'v_ffn_a_w3': out['v_ffn_a_w3'], 'v_ffn_a_w2': out['v_ffn_a_w2'], 'v_mix_norm': out['v_mix_norm'], 'v_w_in': out['v_w_in'], 'v_q_a_norm': out['v_q_a_norm'], 'v_w_uq': out['v_w_uq'], 'v_kv_a_norm': out['v_kv_a_norm'], 'v_w_ukv': out['v_w_ukv'], 'v_q_norm': out['v_q_norm'], 'v_k_norm': out['v_k_norm'], 'v_gm_v_norm': out['v_gm_v_norm'], 'v_gm_ws': out['v_gm_ws'], 'v_gm_bs': out['v_gm_bs'], 'v_attn_out_norm': out['v_attn_out_norm'], 'v_gm_out_norm': out['v_gm_out_norm'], 'v_w_out': out['v_w_out'], 'v_ffn_b_norm': out['v_ffn_b_norm'], 'v_ffn_b_w1': out['v_ffn_b_w1'], 'v_ffn_b_w3': out['v_ffn_b_w3'], 'v_ffn_b_w2': out['v_ffn_b_w2'], 'v_ple_gate_norm': out['v_ple_gate_norm'], 'v_w_ple_gate': out['v_w_ple_gate'], 'v_w_ple': out['v_w_ple'], 'v_ple_norm': out['v_ple_norm']}


def _loss(weights, diff, rest, loss_target):
    with _jax.named_scope("forward"):
        args = {**rest, TWIN_DIFF_INPUT: diff, **{k: w.astype(_WEIGHT_DTYPES[k]) for k, w in weights.items()}}
        y = _forward(args)
    with _jax.named_scope("loss_head"):
        err = _jnp.square(y.astype(_jnp.float32) - loss_target)
        return 0.5 * _jnp.sum(_jnp.mean(err, axis=-1)) if err.ndim else 0.5 * err


def _adamw(w, g, m, v):
    m = ADAM_B1 * m + (1.0 - ADAM_B1) * g
    v = ADAM_B2 * v + (1.0 - ADAM_B2) * _jnp.square(g)
    m_hat = m / (1.0 - ADAM_B1 ** ADAM_STEP)
    v_hat = v / (1.0 - ADAM_B2 ** ADAM_STEP)
    delta = -ADAM_LR * (m_hat / (_jnp.sqrt(v_hat) + ADAM_EPS) + ADAM_WD * w)
    return delta, m, v


def reference(x, p, positions, ffn_a_norm, ffn_a_w1, ffn_a_w3, ffn_a_w2, mix_norm, w_in, q_a_norm, w_uq, kv_a_norm, w_ukv, q_norm, k_norm, gm_v_norm, gm_ws, gm_bs, attn_out_norm, gm_out_norm, w_out, ffn_b_norm, ffn_b_w1, ffn_b_w3, ffn_b_w2, ple_gate_norm, w_ple_gate, w_ple, ple_norm, loss_target, m_ffn_a_norm, m_ffn_a_w1, m_ffn_a_w3, m_ffn_a_w2, m_mix_norm, m_w_in, m_q_a_norm, m_w_uq, m_kv_a_norm, m_w_ukv, m_q_norm, m_k_norm, m_gm_v_norm, m_gm_ws, m_gm_bs, m_attn_out_norm, m_gm_out_norm, m_w_out, m_ffn_b_norm, m_ffn_b_w1, m_ffn_b_w3, m_ffn_b_w2, m_ple_gate_norm, m_w_ple_gate, m_w_ple, m_ple_norm, v_ffn_a_norm, v_ffn_a_w1, v_ffn_a_w3, v_ffn_a_w2, v_mix_norm, v_w_in, v_q_a_norm, v_w_uq, v_kv_a_norm, v_w_ukv, v_q_norm, v_k_norm, v_gm_v_norm, v_gm_ws, v_gm_bs, v_attn_out_norm, v_gm_out_norm, v_w_out, v_ffn_b_norm, v_ffn_b_w1, v_ffn_b_w3, v_ffn_b_w2, v_ple_gate_norm, v_w_ple_gate, v_w_ple, v_ple_norm):
    given = dict(x=x, p=p, positions=positions, ffn_a_norm=ffn_a_norm, ffn_a_w1=ffn_a_w1, ffn_a_w3=ffn_a_w3, ffn_a_w2=ffn_a_w2, mix_norm=mix_norm, w_in=w_in, q_a_norm=q_a_norm, w_uq=w_uq, kv_a_norm=kv_a_norm, w_ukv=w_ukv, q_norm=q_norm, k_norm=k_norm, gm_v_norm=gm_v_norm, gm_ws=gm_ws, gm_bs=gm_bs, attn_out_norm=attn_out_norm, gm_out_norm=gm_out_norm, w_out=w_out, ffn_b_norm=ffn_b_norm, ffn_b_w1=ffn_b_w1, ffn_b_w3=ffn_b_w3, ffn_b_w2=ffn_b_w2, ple_gate_norm=ple_gate_norm, w_ple_gate=w_ple_gate, w_ple=w_ple, ple_norm=ple_norm, loss_target=loss_target, m_ffn_a_norm=m_ffn_a_norm, m_ffn_a_w1=m_ffn_a_w1, m_ffn_a_w3=m_ffn_a_w3, m_ffn_a_w2=m_ffn_a_w2, m_mix_norm=m_mix_norm, m_w_in=m_w_in, m_q_a_norm=m_q_a_norm, m_w_uq=m_w_uq, m_kv_a_norm=m_kv_a_norm, m_w_ukv=m_w_ukv, m_q_norm=m_q_norm, m_k_norm=m_k_norm, m_gm_v_norm=m_gm_v_norm, m_gm_ws=m_gm_ws, m_gm_bs=m_gm_bs, m_attn_out_norm=m_attn_out_norm, m_gm_out_norm=m_gm_out_norm, m_w_out=m_w_out, m_ffn_b_norm=m_ffn_b_norm, m_ffn_b_w1=m_ffn_b_w1, m_ffn_b_w3=m_ffn_b_w3, m_ffn_b_w2=m_ffn_b_w2, m_ple_gate_norm=m_ple_gate_norm, m_w_ple_gate=m_w_ple_gate, m_w_ple=m_w_ple, m_ple_norm=m_ple_norm, v_ffn_a_norm=v_ffn_a_norm, v_ffn_a_w1=v_ffn_a_w1, v_ffn_a_w3=v_ffn_a_w3, v_ffn_a_w2=v_ffn_a_w2, v_mix_norm=v_mix_norm, v_w_in=v_w_in, v_q_a_norm=v_q_a_norm, v_w_uq=v_w_uq, v_kv_a_norm=v_kv_a_norm, v_w_ukv=v_w_ukv, v_q_norm=v_q_norm, v_k_norm=v_k_norm, v_gm_v_norm=v_gm_v_norm, v_gm_ws=v_gm_ws, v_gm_bs=v_gm_bs, v_attn_out_norm=v_attn_out_norm, v_gm_out_norm=v_gm_out_norm, v_w_out=v_w_out, v_ffn_b_norm=v_ffn_b_norm, v_ffn_b_w1=v_ffn_b_w1, v_ffn_b_w3=v_ffn_b_w3, v_ffn_b_w2=v_ffn_b_w2, v_ple_gate_norm=v_ple_gate_norm, v_w_ple_gate=v_w_ple_gate, v_w_ple=v_w_ple, v_ple_norm=v_ple_norm)
    weights = {n: given[n] for n in TWIN_WEIGHTS}
    shared = {n: given[n] for n in SHARED_INPUTS}
    per_example = {n: given[n] for n in ['x', 'p', 'positions']}
    grad_fn = _jax.value_and_grad(_loss, argnums=(0, 1))

    def one_microbatch(ex, loss_target):
        ex = dict(ex)
        diff = ex.pop(TWIN_DIFF_INPUT)
        return grad_fn(weights, diff, {**shared, **ex}, loss_target)

    if N_MICROBATCH == 1:
        loss, (grad_w, grad_x) = one_microbatch(per_example, given["loss_target"])
    else:
        def body(carry, xs):
            loss_sum, grad_sum = carry
            l_k, (gw_k, gx_k) = one_microbatch(xs[0], xs[1])
            with _jax.named_scope("update"):
                return (loss_sum + l_k, _jax.tree.map(_jnp.add, grad_sum, gw_k)), gx_k

        init = (_jnp.zeros((), _jnp.float32), _jax.tree.map(_jnp.zeros_like, weights))
        (loss, grad_w), grad_x = _jax.lax.scan(body, init, (per_example, given["loss_target"]))
    with _jax.named_scope("update"):
        delta_w, new_m, new_v = {}, {}, {}
        for n in TWIN_WEIGHTS:
            delta_w[n], new_m[n], new_v[n] = _adamw(weights[n], grad_w[n], given["m_" + n], given["v_" + n])
    return (loss, grad_x, *[grad_w[n] for n in TWIN_WEIGHTS], *[delta_w[n] for n in TWIN_WEIGHTS],
            *[new_m[n] for n in TWIN_WEIGHTS], *[new_v[n] for n in TWIN_WEIGHTS])
```

```python
import functools
import math

import jax
import jax.numpy as jnp
from jax import lax
from jax.experimental import pallas as pl
from jax.experimental.pallas import tpu as pltpu

F32 = jnp.float32
BF16 = jnp.bfloat16

D_MODEL = 2048
D_FF = 5504
N_CHIPS = 4
FF_SHARD = D_FF // N_CHIPS
FF_PAD = 1408
FF_P = N_CHIPS * FF_PAD
HEADS = 8
QK_NOPE = 128
QK_ROPE = 64
QK_DIM = 192
HEAD_PAD = 256
V_DIM = 128
Q_RANK = 512
KV_RANK = 256
ATTN_W = 1024
GM_W = 1024
GROUPS = 8
CHUNK = 128
PLE_DIM = 256
IN_P = 3072
EPS = 1e-6
ROPE_BASE = 10000.0
ATTN_SCALE = QK_DIM ** -0.5
VMEM_LIMIT_BYTES = 56 * 1024 * 1024


def _params(sem):
    return pltpu.CompilerParams(dimension_semantics=sem, vmem_limit_bytes=VMEM_LIMIT_BYTES)


def _bf(x):
    return x if x.dtype == BF16 else x.astype(BF16)


def _sigmoid(x):
    return 1.0 / (1.0 + jnp.exp(-x))


_GELU_C = math.sqrt(2.0 / math.pi)


def _gelu(x):
    t = jnp.tanh(_GELU_C * (x + 0.044715 * x * x * x))
    return 0.5 * x * (1.0 + t)


def _gelu_grad(x):
    t = jnp.tanh(_GELU_C * (x + 0.044715 * x * x * x))
    return 0.5 * (1.0 + t) + 0.5 * x * (1.0 - t * t) * _GELU_C * (1.0 + 3 * 0.044715 * x * x)


def op_a(a, tm, tk):
    return (a, (tm, tk), lambda i, j, k: (i, k), 1)


def op_at(a, tm, tk):
    return (a, (tk, tm), lambda i, j, k: (k, i), 0)


def op_b(b, tk, tn):
    return (b, (tk, tn), lambda i, j, k: (k, j), 0)


def op_bt(b, tk, tn):
    return (b, (tn, tk), lambda i, j, k: (j, k), 1)


def op_b_cols(g, pre, tk, tn):
    nb = g.shape[-1] // tn
    none = (None,) * (1 + len(pre))
    return (g, none + (tk, tn), lambda i, j, k: (j // nb,) + tuple(pre) + (k, j % nb), 0)


def op_b_cols_t(g, pre, tk, tn):
    nb = g.shape[-1] // tk
    none = (None,) * (1 + len(pre))
    return (g, none + (tn, tk), lambda i, j, k: (k // nb,) + tuple(pre) + (j, k % nb), 1)


def op_b_rows(g, pre, tk, tn, koff=0):
    nb = g.shape[-2] // tk
    none = (None,) * (1 + len(pre))
    return (g, none + (tk, tn), lambda i, j, k: ((k + koff) // nb,) + tuple(pre) + ((k + koff) % nb, j), 0)


def op_b_rows_t(g, pre, tk, tn):
    nb = g.shape[-2] // tn
    none = (None,) * (1 + len(pre))
    return (g, none + (tn, tk), lambda i, j, k: (j // nb,) + tuple(pre) + (j % nb, k), 1)


def tile_mn(x, tm, tn):
    return (x, (tm, tn), lambda i, j: (i, j))


def out_mn(M, N, tm, tn, dtype):
    return (jax.ShapeDtypeStruct((M, N), dtype), (tm, tn), lambda i, j: (i, j))


def out_cols(M, ns, tm, tn, dtype):
    nb = ns // tn
    return (jax.ShapeDtypeStruct((N_CHIPS, M, ns), dtype), (None, tm, tn), lambda i, j: (j // nb, i, j % nb))


def matmul(name, grid_mnk, a_ops, b_ops, terms, n_acc, extras, outs, epilogue, acc_tile, n_outer=False):
    gm, gn, gk = grid_mnk
    na, nb, nx, no = len(a_ops), len(b_ops), len(extras), len(outs)

    def body(*refs):
        a_refs, b_refs = refs[:na], refs[na:na + nb]
        x_refs = refs[na + nb:na + nb + nx]
        o_refs = refs[na + nb + nx:na + nb + nx + no]
        acc_refs = refs[na + nb + nx + no:]
        k = pl.program_id(2)

        @pl.when(k == 0)
        def _():
            for acc in acc_refs:
                acc[...] = jnp.zeros_like(acc)

        for ai, bi, ci in terms:
            dims = (((a_ops[ai][3],), (b_ops[bi][3],)), ((), ()))
            acc_refs[ci][...] += lax.dot_general(_bf(a_refs[ai][...]), _bf(b_refs[bi][...]), dims,
                                                 preferred_element_type=F32)

        @pl.when(k == gk - 1)
        def _():
            res = epilogue([acc[...] for acc in acc_refs], [x[...] for x in x_refs])
            for o, v in zip(o_refs, res):
                o[...] = v.astype(o.dtype)

    if n_outer:
        grid = (gn, gm, gk)

        def ix3(f):
            return lambda j, i, k: f(i, j, k)

        def ix2(f):
            return lambda j, i, k: f(i, j)
    else:
        grid = (gm, gn, gk)

        def ix3(f):
            return lambda i, j, k: f(i, j, k)

        def ix2(f):
            return lambda i, j, k: f(i, j)

    in_specs = [pl.BlockSpec(blk, ix3(f)) for (_, blk, f, _) in list(a_ops) + list(b_ops)]
    in_specs += [pl.BlockSpec(blk, ix2(f)) for (_, blk, f) in extras]
    out_specs = [pl.BlockSpec(blk, ix2(f)) for (_, blk, f) in outs]
    return pl.pallas_call(
        body,
        name=name,
        grid=grid,
        in_specs=in_specs,
        out_specs=out_specs,
        out_shape=[s for (s, _, _) in outs],
        scratch_shapes=[pltpu.VMEM(acc_tile, F32) for _ in range(n_acc)],
        compiler_params=_params(("parallel", "parallel", "arbitrary")),
    )(*[o[0] for o in a_ops], *[o[0] for o in b_ops], *[x[0] for x in extras])


def _acc0(accs, xs):
    return (accs[0],)


def mm_simple(name, a, b_op_fn, M, N, K, tm, tn, tk, out_dtype=F32, a_t=False, extras=(), epilogue=_acc0, outs=None):
    a_op = op_at(a, tm, tk) if a_t else op_a(a, tm, tk)
    outs = outs or [out_mn(M, N, tm, tn, out_dtype)]
    return matmul(name, (M // tm, N // tn, K // tk), [a_op], [b_op_fn(tk, tn)], [(0, 0, 0)], 1,
                  list(extras), outs, epilogue, (tm, tn))


def rms_fwd(name, x, g, width, col_blk=0, tm=256, out_dtype=BF16):
    T = x.shape[0]

    def body(x_ref, g_ref, o_ref):
        xv = x_ref[...].astype(F32)
        r = lax.rsqrt(jnp.mean(xv * xv, axis=-1, keepdims=True) + EPS)
        o_ref[...] = (xv * r * g_ref[...]).astype(o_ref.dtype)

    return pl.pallas_call(
        body, name=name, grid=(T // tm,),
        in_specs=[pl.BlockSpec((tm, width), lambda i: (i, col_blk)), pl.BlockSpec((1, width), lambda i: (0, 0))],
        out_specs=pl.BlockSpec((tm, width), lambda i: (i, 0)),
        out_shape=jax.ShapeDtypeStruct((T, width), out_dtype),
        compiler_params=_params(("parallel",)),
    )(x, g.reshape(1, width))


def rms_bwd(name, x, g, dn, width, col_blk=0, dres=None, tm=256, with_delta=False):
    T = x.shape[0]
    has_res = dres is not None

    def body(*refs):
        x_ref, g_ref, dn_ref = refs[:3]
        pos = 3
        res_ref = None
        if has_res:
            res_ref = refs[pos]
            pos += 1
        dx_ref, dg_ref = refs[pos], refs[pos + 1]
        delta_ref = refs[pos + 2] if with_delta else None
        i = pl.program_id(0)
        xv = x_ref[...].astype(F32)
        r = lax.rsqrt(jnp.mean(xv * xv, axis=-1, keepdims=True) + EPS)
        xh = xv * r
        d = dn_ref[...].astype(F32)
        gd = d * g_ref[...]
        dx = r * (gd - xh * jnp.mean(gd * xh, axis=-1, keepdims=True))
        if has_res:
            dx = dx + res_ref[...]
        dx_ref[...] = dx.astype(dx_ref.dtype)
        part = jnp.sum(d * xh, axis=0, keepdims=True)

        @pl.when(i == 0)
        def _():
            dg_ref[...] = part

        @pl.when(i > 0)
        def _():
            dg_ref[...] += part

        if with_delta:
            for h in range(width // 128):
                sl = slice(h * 128, (h + 1) * 128)
                s = jnp.sum(dx[:, sl] * xv[:, sl], axis=-1, keepdims=True)
                delta_ref[:, sl] = jnp.broadcast_to(s, (tm, 128))

    in_specs = [pl.BlockSpec((tm, width), lambda i: (i, col_blk)), pl.BlockSpec((1, width), lambda i: (0, 0)),
                pl.BlockSpec((tm, width), lambda i: (i, 0))]
    args = [x, g.reshape(1, width), dn]
    if has_res:
        in_specs.append(pl.BlockSpec((tm, width), lambda i: (i, 0)))
        args.append(dres)
    out_specs = [pl.BlockSpec((tm, width), lambda i: (i, 0)), pl.BlockSpec((1, width), lambda i: (0, 0))]
    out_shape = [jax.ShapeDtypeStruct((T, width), F32), jax.ShapeDtypeStruct((1, width), F32)]
    if with_delta:
        out_specs.append(pl.BlockSpec((tm, width), lambda i: (i, 0)))
        out_shape.append(jax.ShapeDtypeStruct((T, width), F32))
    return pl.pallas_call(
        body, name=name, grid=(T // tm,), in_specs=in_specs, out_specs=out_specs, out_shape=out_shape,
        compiler_params=_params(("arbitrary",)),
    )(*args)


def ffn_fwd(tag, h, g, w1g, w3g, w2g, pre):
    T = h.shape[0]
    n = rms_fwd(f"{tag}_rms", h, g, D_MODEL)
    tm, tn = 512, FF_PAD

    def up_epi(accs, xs):
        a1, a3 = accs
        return a1, a3, a1 * _sigmoid(a1) * a3

    a1, a3, s = matmul(
        f"{tag}_up", (T // tm, FF_P // tn, 1),
        [op_a(n, tm, D_MODEL)], [op_b_cols(w1g, pre, D_MODEL, tn), op_b_cols(w3g, pre, D_MODEL, tn)],
        [(0, 0, 0), (0, 1, 1)], 2, [],
        [out_mn(T, FF_P, tm, tn, BF16)] * 3, up_epi, (tm, tn), n_outer=True)

    tn2 = 1024
    (h_out,) = matmul(
        f"{tag}_down", (T // tm, D_MODEL // tn2, N_CHIPS),
        [op_a(s, tm, FF_PAD)], [op_b_rows(w2g, pre, FF_PAD, tn2)],
        [(0, 0, 0)], 1, [tile_mn(h, tm, tn2)],
        [out_mn(T, D_MODEL, tm, tn2, F32)], lambda accs, xs: (xs[0] + 0.5 * accs[0],), (tm, tn2))
    return h_out, (n, a1, a3, s)


def ffn_bwd(tag, dh_out, h, g, res, w1g, w3g, w2g, pre):
    n, a1, a3, s = res
    T = h.shape[0]
    tm, tn = 512, FF_PAD

    def act_epi(accs, xs):
        ds = 0.5 * accs[0]
        x1, x3 = xs[0].astype(F32), xs[1].astype(F32)
        sg = _sigmoid(x1)
        silu = x1 * sg
        return ds * x3 * (sg + silu * (1.0 - sg)), ds * silu

    da1, da3 = matmul(
        f"{tag}_dact", (T // tm, FF_P // tn, 1),
        [op_a(dh_out, tm, D_MODEL)], [op_b_rows_t(w2g, pre, D_MODEL, tn)],
        [(0, 0, 0)], 1, [tile_mn(a1, tm, tn), tile_mn(a3, tm, tn)],
        [out_mn(T, FF_P, tm, tn, BF16)] * 2, act_epi, (tm, tn), n_outer=True)

    tk = 512
    (dw2,) = matmul(
        f"{tag}_dw2", (FF_P // FF_PAD, D_MODEL // 1024, T // tk),
        [op_at(s, FF_PAD, tk)], [op_b(dh_out, tk, 1024)],
        [(0, 0, 0)], 1, [], [out_mn(FF_P, D_MODEL, FF_PAD, 1024, F32)],
        lambda accs, xs: (0.5 * accs[0],), (FF_PAD, 1024))

    def dw_up(nm, da):
        (dw,) = matmul(
            f"{tag}_{nm}", (D_MODEL // 1024, FF_P // FF_PAD, T // tk),
            [op_at(n, 1024, tk)], [op_b(da, tk, FF_PAD)],
            [(0, 0, 0)], 1, [], [out_cols(D_MODEL, FF_PAD, 1024, FF_PAD, F32)], _acc0, (1024, FF_PAD))
        return dw

    dw1 = dw_up("dw1", da1)
    dw3 = dw_up("dw3", da3)

    tn2 = 1024
    (dn,) = matmul(
        f"{tag}_dn", (T // tm, D_MODEL // tn2, N_CHIPS),
        [op_a(da1, tm, FF_PAD), op_a(da3, tm, FF_PAD)],
        [op_b_cols_t(w1g, pre, FF_PAD, tn2), op_b_cols_t(w3g, pre, FF_PAD, tn2)],
        [(0, 0, 0), (1, 1, 0)], 1, [], [out_mn(T, D_MODEL, tm, tn2, F32)], _acc0, (tm, tn2))
    dh, dg = rms_bwd(f"{tag}_rms_bwd", h, g, dn, D_MODEL, dres=dh_out)
    return dh, dg, dw1, dw3, dw2


def rope_tables(positions):
    inv_freq = ROPE_BASE ** (-jnp.arange(0, QK_ROPE, 2, dtype=F32) / QK_ROPE)
    ang = positions.astype(F32)[:, None] * inv_freq
    cos, sin = jnp.cos(ang), jnp.sin(ang)
    T = positions.shape[0]
    one, zero = jnp.ones((T, QK_NOPE), F32), jnp.zeros((T, 64), F32)
    z32, z128 = jnp.zeros((T, 32), F32), jnp.zeros((T, QK_NOPE), F32)
    c = jnp.concatenate([one, cos, cos, zero], axis=1)
    s1 = jnp.concatenate([z128, -sin, z32, zero], axis=1)
    s2 = jnp.concatenate([z128, z32, sin, zero], axis=1)
    return c, s1, s2


def _rope(y, c, s1, s2):
    return y * c + pltpu.roll(y, HEAD_PAD - 32, 1) * s1 + pltpu.roll(y, 32, 1) * s2


def _rope_t(d, c, s1, s2):
    return d * c + pltpu.roll(d * s1, 32, 1) + pltpu.roll(d * s2, HEAD_PAD - 32, 1)


def _head_norm(x):
    r = lax.rsqrt(jnp.sum(x * x, axis=-1, keepdims=True) * (1.0 / QK_DIM) + EPS)
    return x * r, r


def qk_prep_fwd(tag, q_raw, kk_raw, z_p, gq, gk, tabs, tm=256):
    T = q_raw.shape[0]
    c, s1, s2 = tabs

    def body(q_ref, k_ref, kr_ref, gq_ref, gk_ref, c_ref, s1_ref, s2_ref, qo_ref, ko_ref):
        cv, s1v, s2v = c_ref[...], s1_ref[...], s2_ref[...]
        kr = kr_ref[...]
        for h in range(HEADS):
            sl = slice(h * HEAD_PAD, (h + 1) * HEAD_PAD)
            xh, _ = _head_norm(q_ref[:, sl])
            qo_ref[:, sl] = (_rope(xh * gq_ref[...], cv, s1v, s2v) * ATTN_SCALE).astype(BF16)
            xh, _ = _head_norm(k_ref[:, sl] + kr)
            ko_ref[:, sl] = _rope(xh * gk_ref[...], cv, s1v, s2v).astype(BF16)

    row = lambda i: (i, 0)
    full = pl.BlockSpec((tm, HEADS * HEAD_PAD), row)
    tab = pl.BlockSpec((tm, HEAD_PAD), row)
    vec = pl.BlockSpec((1, HEAD_PAD), lambda i: (0, 0))
    return pl.pallas_call(
        body, name=f"{tag}_qk_prep", grid=(T // tm,),
        in_specs=[full, full, pl.BlockSpec((tm, HEAD_PAD), lambda i: (i, 3)), vec, vec, tab, tab, tab],
        out_specs=[full, full],
        out_shape=[jax.ShapeDtypeStruct((T, HEADS * HEAD_PAD), BF16)] * 2,
        compiler_params=_params(("parallel",)),
    )(q_raw, kk_raw, z_p, gq, gk, c, s1, s2)


def qk_prep_bwd(tag, dq_full, dk_full, q_raw, kk_raw, z_p, gq, gk, tabs, tm=256):
    T = q_raw.shape[0]
    c, s1, s2 = tabs

    def body(dq_ref, dk_ref, q_ref, k_ref, kr_ref, gq_ref, gk_ref, c_ref, s1_ref, s2_ref,
             dqr_ref, dkr_ref, dz_ref, dgq_ref, dgk_ref):
        i = pl.program_id(0)
        cv, s1v, s2v = c_ref[...], s1_ref[...], s2_ref[...]
        kr = kr_ref[...]
        lane = lax.broadcasted_iota(jnp.int32, (tm, HEAD_PAD), 1)
        slot = ((lane >= QK_NOPE) & (lane < QK_DIM)).astype(F32)

        def one(x, g, d):
            xh, r = _head_norm(x)
            dy = _rope_t(d, cv, s1v, s2v)
            gd = dy * g
            dx = r * (gd - xh * (jnp.sum(gd * xh, axis=-1, keepdims=True) * (1.0 / QK_DIM)))
            return dx, jnp.sum(dy * xh, axis=0, keepdims=True)

        dgq = jnp.zeros((1, HEAD_PAD), F32)
        dgk = jnp.zeros((1, HEAD_PAD), F32)
        dz = jnp.zeros((tm, HEAD_PAD), F32)
        for h in range(HEADS):
            sl = slice(h * HEAD_PAD, (h + 1) * HEAD_PAD)
            dx, dg = one(q_ref[:, sl], gq_ref[...], dq_ref[:, sl].astype(F32) * ATTN_SCALE)
            dqr_ref[:, sl] = dx
            dgq = dgq + dg
            dx, dg = one(k_ref[:, sl] + kr, gk_ref[...], dk_ref[:, sl].astype(F32))
            dkr_ref[:, sl] = dx
            dgk = dgk + dg
            dz = dz + dx
        dz_ref[...] = dz * slot

        @pl.when(i == 0)
        def _():
            dgq_ref[...] = dgq
            dgk_ref[...] = dgk

        @pl.when(i > 0)
        def _():
            dgq_ref[...] += dgq
            dgk_ref[...] += dgk

    row = lambda i: (i, 0)
    full = pl.BlockSpec((tm, HEADS * HEAD_PAD), row)
    tab = pl.BlockSpec((tm, HEAD_PAD), row)
    vec = pl.BlockSpec((1, HEAD_PAD), lambda i: (0, 0))
    return pl.pallas_call(
        body, name=f"{tag}_qk_prep_bwd", grid=(T // tm,),
        in_specs=[full, full, full, full, pl.BlockSpec((tm, HEAD_PAD), lambda i: (i, 3)), vec, vec, tab, tab, tab],
        out_specs=[full, full, tab, vec, vec],
        out_shape=[jax.ShapeDtypeStruct((T, HEADS * HEAD_PAD), F32)] * 2
        + [jax.ShapeDtypeStruct((T, HEAD_PAD), F32)] + [jax.ShapeDtypeStruct((1, HEAD_PAD), F32)] * 2,
        compiler_params=_params(("arbitrary",)),
    )(dq_full, dk_full, q_raw, kk_raw, z_p, gq, gk, c, s1, s2)


def attn_fwd(tag, q_full, k_full, vv, blk=512):
    T = q_full.shape[0]
    nb = T // blk
    neg = float(jnp.finfo(jnp.float32).min)

    def body(q_ref, k_ref, v_ref, o_ref, lse_ref, m_ref, l_ref, acc_ref):
        i, j = pl.program_id(1), pl.program_id(2)

        @pl.when(j == 0)
        def _():
            m_ref[...] = jnp.full_like(m_ref, neg)
            l_ref[...] = jnp.zeros_like(l_ref)
            acc_ref[...] = jnp.zeros_like(acc_ref)

        def step(masked):
            s = lax.dot_general(q_ref[...], k_ref[...], (((1,), (1,)), ((), ())), preferred_element_type=F32)
            if masked:
                row = lax.broadcasted_iota(jnp.int32, (blk, blk), 0)
                col = lax.broadcasted_iota(jnp.int32, (blk, blk), 1)
                s = jnp.where(col <= row, s, neg)
            m_prev = m_ref[...]
            m_new = jnp.maximum(m_prev, jnp.max(s, axis=-1, keepdims=True))
            alpha = jnp.exp(m_prev - m_new)
            p = jnp.exp(s - m_new[:, :1])
            l_ref[...] = alpha * l_ref[...] + jnp.sum(p, axis=-1, keepdims=True)
            acc_ref[...] = alpha * acc_ref[...] + jnp.dot(p.astype(BF16), v_ref[...], preferred_element_type=F32)
            m_ref[...] = m_new

        @pl.when(j < i)
        def _():
            step(False)

        @pl.when(j == i)
        def _():
            step(True)
            o_ref[...] = acc_ref[...] / l_ref[...]
            lse_ref[...] = m_ref[...] + jnp.log(l_ref[...])

    kv_ix = lambda h, i, j: (jnp.minimum(j, i), h)
    return pl.pallas_call(
        body, name=f"{tag}_attn_fwd", grid=(HEADS, nb, nb),
        in_specs=[pl.BlockSpec((blk, HEAD_PAD), lambda h, i, j: (i, h)),
                  pl.BlockSpec((blk, HEAD_PAD), kv_ix), pl.BlockSpec((blk, V_DIM), kv_ix)],
        out_specs=[pl.BlockSpec((blk, V_DIM), lambda h, i, j: (i, h))] * 2,
        out_shape=[jax.ShapeDtypeStruct((T, ATTN_W), F32)] * 2,
        scratch_shapes=[pltpu.VMEM((blk, V_DIM), F32)] * 3,
        compiler_params=_params(("parallel", "parallel", "arbitrary")),
    )(q_full, k_full, vv)


def attn_bwd(tag, q_full, k_full, vv, do, lse, delta, blk=512):
    T = q_full.shape[0]
    nb = T // blk
    neg = float(jnp.finfo(jnp.float32).min)

    def body(q_ref, k_ref, v_ref, do_ref, lse_ref, dl_ref, dq_ref, dk_ref, dv_ref, dk_acc, dv_acc):
        j, i = pl.program_id(1), pl.program_id(2)

        @pl.when((j == 0) & (i == 0))
        def _():
            dq_ref[...] = jnp.zeros_like(dq_ref)

        @pl.when(i == 0)
        def _():
            dk_acc[...] = jnp.zeros_like(dk_acc)
            dv_acc[...] = jnp.zeros_like(dv_acc)

        def step(masked):
            q, k = q_ref[...], k_ref[...]
            s = lax.dot_general(q, k, (((1,), (1,)), ((), ())), preferred_element_type=F32)
            if masked:
                row = lax.broadcasted_iota(jnp.int32, (blk, blk), 0)
                col = lax.broadcasted_iota(jnp.int32, (blk, blk), 1)
                s = jnp.where(col <= row, s, neg)
            p = jnp.exp(s - lse_ref[:, :1])
            dob = _bf(do_ref[...])
            dv_acc[...] += lax.dot_general(p.astype(BF16), dob, (((0,), (0,)), ((), ())), preferred_element_type=F32)
            dp = lax.dot_general(dob, v_ref[...], (((1,), (1,)), ((), ())), preferred_element_type=F32)
            ds = (p * (dp - dl_ref[:, :1])).astype(BF16)
            dk_acc[...] += lax.dot_general(ds, q, (((0,), (0,)), ((), ())), preferred_element_type=F32)
            rows = pl.ds(pl.multiple_of(i * blk, blk), blk)
            dq_ref[rows, :] += jnp.dot(ds, k, preferred_element_type=F32)

        @pl.when(i > j)
        def _():
            step(False)

        @pl.when(i == j)
        def _():
            step(True)

        @pl.when(i == nb - 1)
        def _():
            dk_ref[...] = dk_acc[...]
            dv_ref[...] = dv_acc[...]

    q_ix = lambda h, j, i: (jnp.maximum(i, j), h)
    kv_ix = lambda h, j, i: (j, h)
    return pl.pallas_call(
        body, name=f"{tag}_attn_bwd", grid=(HEADS, nb, nb),
        in_specs=[pl.BlockSpec((blk, HEAD_PAD), q_ix), pl.BlockSpec((blk, HEAD_PAD), kv_ix),
                  pl.BlockSpec((blk, V_DIM), kv_ix), pl.BlockSpec((blk, V_DIM), q_ix),
                  pl.BlockSpec((blk, V_DIM), q_ix), pl.BlockSpec((blk, V_DIM), q_ix)],
        out_specs=[pl.BlockSpec((T, HEAD_PAD), lambda h, j, i: (0, h)),
                   pl.BlockSpec((blk, HEAD_PAD), kv_ix), pl.BlockSpec((blk, V_DIM), kv_ix)],
        out_shape=[jax.ShapeDtypeStruct((T, HEADS * HEAD_PAD), F32)] * 2 + [jax.ShapeDtypeStruct((T, ATTN_W), F32)],
        scratch_shapes=[pltpu.VMEM((blk, HEAD_PAD), F32), pltpu.VMEM((blk, V_DIM), F32)],
        compiler_params=_params(("parallel", "arbitrary", "arbitrary")),
    )(q_full, k_full, vv, do, lse, delta)


def _gm_forward(u, v, gv, wc_ref, bb_ref, nchunk):
    ug = _gelu(u)
    vg = _gelu(v)
    rv = lax.rsqrt(jnp.mean(vg * vg, axis=-1, keepdims=True) + EPS)
    vhat = vg * rv
    vn = (vhat * gv).astype(BF16)
    gates = []
    for cidx in range(nchunk):
        rows = slice(cidx * CHUNK, (cidx + 1) * CHUNK)
        gates.append(jnp.concatenate(
            [jnp.dot(wc_ref[gidx], vn[rows, gidx * 128:(gidx + 1) * 128], preferred_element_type=F32) + bb_ref[gidx]
             for gidx in range(GROUPS)], axis=1))
    gate = jnp.concatenate(gates, axis=0)
    return ug, vhat, rv, vn, gate


def gmlp_fwd(tag, z_p, gv, gout, wc, bb, tm=256):
    T = z_p.shape[0]
    nchunk = tm // CHUNK

    def body(u_ref, v_ref, gv_ref, go_ref, wc_ref, bb_ref, o_ref):
        ug, _, _, _, gate = _gm_forward(u_ref[...], v_ref[...], gv_ref[...], wc_ref, bb_ref, nchunk)
        go = ug * gate
        ro = lax.rsqrt(jnp.mean(go * go, axis=-1, keepdims=True) + EPS)
        o_ref[...] = (go * ro * go_ref[...]).astype(BF16)

    vec = pl.BlockSpec((1, GM_W), lambda i: (0, 0))
    w3 = pl.BlockSpec((GROUPS, CHUNK, CHUNK), lambda i: (0, 0, 0))
    return pl.pallas_call(
        body, name=f"{tag}_gmlp_fwd", grid=(T // tm,),
        in_specs=[pl.BlockSpec((tm, GM_W), lambda i: (i, 1)), pl.BlockSpec((tm, GM_W), lambda i: (i, 2)), vec, vec, w3, w3],
        out_specs=pl.BlockSpec((tm, GM_W), lambda i: (i, 0)),
        out_shape=jax.ShapeDtypeStruct((T, GM_W), BF16),
        compiler_params=_params(("parallel",)),
    )(z_p, z_p, gv.reshape(1, GM_W), gout.reshape(1, GM_W), wc, bb)


def gmlp_bwd(tag, z_p, dmixed, gv, gout, wc, bb, tm=256):
    T = z_p.shape[0]
    nchunk = tm // CHUNK

    def body(u_ref, v_ref, dm_ref, gv_ref, go_ref, wc_ref, bb_ref, du_ref, dv_ref, dwc_ref, dbb_ref, dgv_ref, dgo_ref):
        i = pl.program_id(0)
        u, v = u_ref[...], v_ref[...]
        ug, vhat, rv, vn, gate = _gm_forward(u, v, gv_ref[...], wc_ref, bb_ref, nchunk)
        go = ug * gate
        ro = lax.rsqrt(jnp.mean(go * go, axis=-1, keepdims=True) + EPS)
        ohat = go * ro
        dm = dm_ref[...].astype(F32)
        dgo_part = jnp.sum(dm * ohat, axis=0, keepdims=True)
        doh = dm * go_ref[...]
        dgo = ro * (doh - ohat * jnp.mean(doh * ohat, axis=-1, keepdims=True))
        du_ref[...] = dgo * gate * _gelu_grad(u)
        dgate = dgo * ug
        dgb = dgate.astype(BF16)
        dvn_rows = []
        dwc_parts = []
        dbb_parts = []
        for gidx in range(GROUPS):
            cols = slice(gidx * 128, (gidx + 1) * 128)
            dw = jnp.zeros((CHUNK, CHUNK), F32)
            db = jnp.zeros((CHUNK, 128), F32)
            for cidx in range(nchunk):
                rows = slice(cidx * CHUNK, (cidx + 1) * CHUNK)
                dw = dw + lax.dot_general(dgb[rows, cols], vn[rows, cols], (((1,), (1,)), ((), ())),
                                          preferred_element_type=F32)
                db = db + dgate[rows, cols]
            dwc_parts.append(dw)
            dbb_parts.append(db)
        for cidx in range(nchunk):
            rows = slice(cidx * CHUNK, (cidx + 1) * CHUNK)
            dvn_rows.append(jnp.concatenate(
                [lax.dot_general(wc_ref[gidx], dgb[rows, gidx * 128:(gidx + 1) * 128], (((0,), (0,)), ((), ())),
                                 preferred_element_type=F32) for gidx in range(GROUPS)], axis=1))
        dvn = jnp.concatenate(dvn_rows, axis=0)
        dgv_part = jnp.sum(dvn * vhat, axis=0, keepdims=True)
        dvh = dvn * gv_ref[...]
        dvg = rv * (dvh - vhat * jnp.mean(dvh * vhat, axis=-1, keepdims=True))
        dv_ref[...] = dvg * _gelu_grad(v)

        @pl.when(i == 0)
        def _():
            for gidx in range(GROUPS):
                dwc_ref[gidx] = dwc_parts[gidx]
                dbb_ref[gidx] = dbb_parts[gidx]
            dgv_ref[...] = dgv_part
            dgo_ref[...] = dgo_part

        @pl.when(i > 0)
        def _():
            for gidx in range(GROUPS):
                dwc_ref[gidx] += dwc_parts[gidx]
                dbb_ref[gidx] += dbb_parts[gidx]
            dgv_ref[...] += dgv_part
            dgo_ref[...] += dgo_part

    vec = pl.BlockSpec((1, GM_W), lambda i: (0, 0))
    w3 = pl.BlockSpec((GROUPS, CHUNK, CHUNK), lambda i: (0, 0, 0))
    blk = pl.BlockSpec((tm, GM_W), lambda i: (i, 0))
    return pl.pallas_call(
        body, name=f"{tag}_gmlp_bwd", grid=(T // tm,),
        in_specs=[pl.BlockSpec((tm, GM_W), lambda i: (i, 1)), pl.BlockSpec((tm, GM_W), lambda i: (i, 2)),
                  pl.BlockSpec((tm, GM_W), lambda i: (i, 1)), vec, vec, w3, w3],
        out_specs=[blk, blk, w3, w3, vec, vec],
        out_shape=[jax.ShapeDtypeStruct((T, GM_W), F32)] * 2 + [jax.ShapeDtypeStruct((GROUPS, CHUNK, CHUNK), F32)] * 2
        + [jax.ShapeDtypeStruct((1, GM_W), F32)] * 2,
        compiler_params=_params(("arbitrary",)),
    )(z_p, z_p, dmixed, gv.reshape(1, GM_W), gout.reshape(1, GM_W), wc, bb)


def mixer_fwd(tag, h, w, tabs, wout_g, pre):
    T = h.shape[0]
    n2 = rms_fwd(f"{tag}_mix_rms", h, w["mix_norm"], D_MODEL)
    (z_p,) = mm_simple(f"{tag}_win", n2, lambda tk, tn: op_b(w["w_in_p"], tk, tn), T, IN_P, D_MODEL, 512, 1024, D_MODEL)
    cqn = rms_fwd(f"{tag}_cq_rms", z_p, w["q_a_norm"], Q_RANK, col_blk=0)
    ckvn = rms_fwd(f"{tag}_ckv_rms", z_p, w["kv_a_norm"], KV_RANK, col_blk=2)
    (q_raw,) = mm_simple(f"{tag}_wq", cqn, lambda tk, tn: op_b(w["wq_p"], tk, tn), T, 2048, Q_RANK, 512, 1024, Q_RANK)
    (kk_raw,) = mm_simple(f"{tag}_wk", ckvn, lambda tk, tn: op_b(w["wk_p"], tk, tn), T, 2048, KV_RANK, 512, 1024, KV_RANK)
    (vv,) = mm_simple(f"{tag}_wv", ckvn, lambda tk, tn: op_b(w["wv"], tk, tn), T, ATTN_W, KV_RANK, 512, 1024, KV_RANK,
                      out_dtype=BF16)
    q_full, k_full = qk_prep_fwd(tag, q_raw, kk_raw, z_p, w["gq_p"], w["gk_p"], tabs)
    a_out, lse = attn_fwd(tag, q_full, k_full, vv)
    mixed_a = rms_fwd(f"{tag}_ao_rms", a_out, w["attn_out_norm"], ATTN_W)
    mixed_g = gmlp_fwd(tag, z_p, w["gm_v_norm"], w["gm_out_norm"], w["wc"], w["bb"])
    tm, tn, tk = 512, 1024, 512
    (h2,) = matmul(
        f"{tag}_wout", (T // tm, D_MODEL // tn, ATTN_W // tk),
        [op_a(mixed_a, tm, tk), op_a(mixed_g, tm, tk)],
        [op_b_rows(wout_g, pre, tk, tn), op_b_rows(wout_g, pre, tk, tn, koff=ATTN_W // tk)],
        [(0, 0, 0), (1, 1, 0)], 1, [tile_mn(h, tm, tn)], [out_mn(T, D_MODEL, tm, tn, F32)],
        lambda accs, xs: (xs[0] + accs[0],), (tm, tn))
    res = dict(n2=n2, z_p=z_p, cqn=cqn, ckvn=ckvn, q_raw=q_raw, kk_raw=kk_raw, vv=vv, q_full=q_full, k_full=k_full,
               a_out=a_out, lse=lse, mixed_a=mixed_a, mixed_g=mixed_g)
    return h2, res


def mixer_bwd(tag, dh2, h, w, tabs, wout_g, pre, r):
    T = h.shape[0]
    g = {}
    (dmixed,) = mm_simple(f"{tag}_dmixed", dh2, lambda tk, tn: op_b_rows_t(wout_g, pre, tk, tn), T, D_MODEL, D_MODEL,
                          512, 512, D_MODEL)
    (dwo_a,) = mm_simple(f"{tag}_dwout_a", r["mixed_a"], lambda tk, tn: op_b(dh2, tk, tn), ATTN_W, D_MODEL, T,
                         1024, 1024, 512, a_t=True)
    (dwo_g,) = mm_simple(f"{tag}_dwout_g", r["mixed_g"], lambda tk, tn: op_b(dh2, tk, tn), GM_W, D_MODEL, T,
                         1024, 1024, 512, a_t=True)
    g["w_out"] = jnp.concatenate([dwo_a, dwo_g], axis=0)
    da_out, g["attn_out_norm"], delta = rms_bwd(f"{tag}_ao_rms_bwd", r["a_out"], w["attn_out_norm"], dmixed, ATTN_W,
                                                with_delta=True)
    dq_full, dk_full, dvv = attn_bwd(tag, r["q_full"], r["k_full"], r["vv"], da_out, r["lse"], delta)
    dq_raw, dkk_raw, dzkr, g["gq_p"], g["gk_p"] = qk_prep_bwd(tag, dq_full, dk_full, r["q_raw"], r["kk_raw"], r["z_p"],
                                                            w["gq_p"], w["gk_p"], tabs)
    (g["wq_p"],) = mm_simple(f"{tag}_dwq", r["cqn"], lambda tk, tn: op_b(dq_raw, tk, tn), Q_RANK, 2048, T, Q_RANK, 1024, 512,
                             a_t=True)
    (g["wk_p"],) = mm_simple(f"{tag}_dwk", r["ckvn"], lambda tk, tn: op_b(dkk_raw, tk, tn), KV_RANK, 2048, T, KV_RANK, 1024,
                             512, a_t=True)
    (g["wv"],) = mm_simple(f"{tag}_dwv", r["ckvn"], lambda tk, tn: op_b(dvv, tk, tn), KV_RANK, ATTN_W, T, KV_RANK, 1024, 512,
                           a_t=True)
    (dcqn,) = mm_simple(f"{tag}_dcqn", dq_raw, lambda tk, tn: op_bt(w["wq_p"], tk, tn), T, Q_RANK, 2048, 512, Q_RANK, 2048)
    (dck1,) = mm_simple(f"{tag}_dckvn_k", dkk_raw, lambda tk, tn: op_bt(w["wk_p"], tk, tn), T, KV_RANK, 2048, 512, KV_RANK,
                        2048)
    (dckvn,) = mm_simple(f"{tag}_dckvn_v", dvv, lambda tk, tn: op_bt(w["wv"], tk, tn), T, KV_RANK, ATTN_W, 512, KV_RANK,
                         ATTN_W, extras=[tile_mn(dck1, 512, KV_RANK)], epilogue=lambda accs, xs: (accs[0] + xs[0],))
    dc_q, g["q_a_norm"] = rms_bwd(f"{tag}_cq_rms_bwd", r["z_p"], w["q_a_norm"], dcqn, Q_RANK, col_blk=0)
    dc_kv, g["kv_a_norm"] = rms_bwd(f"{tag}_ckv_rms_bwd", r["z_p"], w["kv_a_norm"], dckvn, KV_RANK, col_blk=2)
    du, dv, g["wc"], g["bb"], g["gm_v_norm"], g["gm_out_norm"] = gmlp_bwd(
        tag, r["z_p"], dmixed, w["gm_v_norm"], w["gm_out_norm"], w["wc"], w["bb"])
    dz_p = jnp.concatenate([dc_q, dc_kv, dzkr, du, dv], axis=1).astype(BF16)
    (g["w_in_p"],) = mm_simple(f"{tag}_dwin", r["n2"], lambda tk, tn: op_b(dz_p, tk, tn), D_MODEL, IN_P, T, 1024, 1024, 512,
                               a_t=True)
    (dn2,) = mm_simple(f"{tag}_dn2", dz_p, lambda tk, tn: op_bt(w["w_in_p"], tk, tn), T, D_MODEL, IN_P, 512, 1024, IN_P)
    dh1, g["mix_norm"] = rms_bwd(f"{tag}_mix_rms_bwd", h, w["mix_norm"], dn2, D_MODEL, dres=dh2)
    return dh1, g


def ple_fwd(tag, h3, p_l, w, wpg_g, wple_g, pre):
    T = h3.shape[0]
    (pw,) = mm_simple(f"{tag}_wple", p_l, lambda tk, tn: op_b_cols(wple_g, pre, tk, tn), T, D_MODEL, PLE_DIM, 512, 512,
                      PLE_DIM)
    e = rms_fwd(f"{tag}_ple_rms", pw, w["ple_norm"], D_MODEL, out_dtype=F32)
    n4 = rms_fwd(f"{tag}_pg_rms", h3, w["ple_gate_norm"], D_MODEL)

    def epi(accs, xs):
        gt = _sigmoid(accs[0])
        return xs[0] + gt * xs[1], gt

    tm, tn, tk = 512, 1024, 512
    h4, gate = matmul(
        f"{tag}_wpg", (T // tm, D_MODEL // tn, D_MODEL // tk),
        [op_a(n4, tm, tk)], [op_b_rows(wpg_g, pre, tk, tn)], [(0, 0, 0)], 1,
        [tile_mn(h3, tm, tn), tile_mn(e, tm, tn)],
        [out_mn(T, D_MODEL, tm, tn, F32), out_mn(T, D_MODEL, tm, tn, BF16)], epi, (tm, tn))
    return h4, dict(pw=pw, e=e, n4=n4, gate=gate)


def ple_bwd(tag, dh4, h3, p_l, w, wpg_g, wple_g, pre, r, tm=256):
    T = h3.shape[0]

    def act_body(d_ref, g_ref, e_ref, dpre_ref, de_ref):
        d, gt = d_ref[...], g_ref[...].astype(F32)
        dpre_ref[...] = (d * e_ref[...] * gt * (1.0 - gt)).astype(BF16)
        de_ref[...] = d * gt

    blk = pl.BlockSpec((tm, D_MODEL), lambda i: (i, 0))
    dpre, de = pl.pallas_call(
        act_body, name=f"{tag}_ple_act_bwd", grid=(T // tm,), in_specs=[blk, blk, blk], out_specs=[blk, blk],
        out_shape=[jax.ShapeDtypeStruct((T, D_MODEL), BF16), jax.ShapeDtypeStruct((T, D_MODEL), F32)],
        compiler_params=_params(("parallel",)),
    )(dh4, r["gate"], r["e"])
    g = {}
    (g["w_ple_gate"],) = mm_simple(f"{tag}_dwpg", r["n4"], lambda tk, tn: op_b(dpre, tk, tn), D_MODEL, D_MODEL, T,
                                   1024, 1024, 512, a_t=True)
    (dn4,) = mm_simple(f"{tag}_dn4", dpre, lambda tk, tn: op_b_rows_t(wpg_g, pre, tk, tn), T, D_MODEL, D_MODEL, 512, 512,
                       D_MODEL)
    dh3, g["ple_gate_norm"] = rms_bwd(f"{tag}_pg_rms_bwd", h3, w["ple_gate_norm"], dn4, D_MODEL, dres=dh4)
    dpw, g["ple_norm"] = rms_bwd(f"{tag}_ple_rms_bwd", r["pw"], w["ple_norm"], de, D_MODEL)
    (g["w_ple"],) = mm_simple(f"{tag}_dwple", p_l, lambda tk, tn: op_b(dpw, tk, tn), PLE_DIM, D_MODEL, T, PLE_DIM, 512, 512,
                              a_t=True, outs=[out_cols(PLE_DIM, 512, PLE_DIM, 512, F32)])
    return dh3, g


def loss_grad(y, target, tm=256):
    T = y.shape[0]

    def body(y_ref, t_ref, dy_ref, l_ref):
        i = pl.program_id(0)
        d = y_ref[...] - t_ref[...]
        dy_ref[...] = d * (1.0 / D_MODEL)
        part = jnp.sum((d * d).reshape(tm // 8, 8, D_MODEL), axis=0)

        @pl.when(i == 0)
        def _():
            l_ref[...] = part

        @pl.when(i > 0)
        def _():
            l_ref[...] += part

    blk = pl.BlockSpec((tm, D_MODEL), lambda i: (i, 0))
    dy, part = pl.pallas_call(
        body, name="loss_grad", grid=(T // tm,), in_specs=[blk, blk],
        out_specs=[blk, pl.BlockSpec((8, D_MODEL), lambda i: (0, 0))],
        out_shape=[jax.ShapeDtypeStruct((T, D_MODEL), F32), jax.ShapeDtypeStruct((8, D_MODEL), F32)],
        compiler_params=_params(("arbitrary",)),
    )(y, target)
    return dy, 0.5 * jnp.sum(part) / D_MODEL


def _unshard_cols(g_l):
    return g_l.transpose(1, 0, 2).reshape(g_l.shape[1], -1)


def _shard_cols(w):
    return w.reshape(w.shape[0], N_CHIPS, -1).transpose(1, 0, 2)


def layer_weights(l, G, small):
    w = {k: small[k][l] for k in ("mix_norm", "q_a_norm", "kv_a_norm", "gm_v_norm", "attn_out_norm", "gm_out_norm",
                                  "ple_gate_norm", "ple_norm")}
    win = _unshard_cols(G["w_in"][:, l])
    z = lambda n: jnp.zeros((D_MODEL, n), BF16)
    w["w_in_p"] = jnp.concatenate([win[:, :768], z(128), win[:, 768:832], z(64), win[:, 832:]], axis=1)
    wuq = _unshard_cols(G["w_uq"][:, l]).reshape(Q_RANK, HEADS, QK_DIM)
    w["wq_p"] = jnp.pad(wuq, ((0, 0), (0, 0), (0, HEAD_PAD - QK_DIM))).reshape(Q_RANK, HEADS * HEAD_PAD)
    wukv = _unshard_cols(G["w_ukv"][:, l]).reshape(KV_RANK, HEADS, QK_NOPE + V_DIM)
    w["wk_p"] = jnp.pad(wukv[:, :, :QK_NOPE], ((0, 0), (0, 0), (0, HEAD_PAD - QK_NOPE))).reshape(KV_RANK, HEADS * HEAD_PAD)
    w["wv"] = wukv[:, :, QK_NOPE:].reshape(KV_RANK, ATTN_W)
    w["gq_p"] = jnp.pad(small["q_norm"][l], (0, HEAD_PAD - QK_DIM)).reshape(1, HEAD_PAD)
    w["gk_p"] = jnp.pad(small["k_norm"][l], (0, HEAD_PAD - QK_DIM)).reshape(1, HEAD_PAD)
    tril = jnp.tril(jnp.ones((CHUNK, CHUNK), dtype=bool))
    w["wc"] = jnp.where(tril[None], small["gm_ws"][l], 0.0).astype(BF16)
    w["bb"] = jnp.broadcast_to(small["gm_bs"][l][:, :, None], (GROUPS, CHUNK, 128)).astype(F32)
    return w


def mixer_grads_to_shards(g):
    out = {}
    dwin = g["w_in_p"]
    dwin = jnp.concatenate([dwin[:, :768], dwin[:, 896:960], dwin[:, 1024:]], axis=1)
    out["w_in"] = _shard_cols(dwin)
    dwuq = g["wq_p"].reshape(Q_RANK, HEADS, HEAD_PAD)[:, :, :QK_DIM].reshape(Q_RANK, HEADS * QK_DIM)
    out["w_uq"] = _shard_cols(dwuq)
    dwukv = jnp.concatenate([g["wk_p"].reshape(KV_RANK, HEADS, HEAD_PAD)[:, :, :QK_NOPE],
                             g["wv"].reshape(KV_RANK, HEADS, V_DIM)], axis=-1).reshape(KV_RANK, HEADS * (QK_NOPE + V_DIM))
    out["w_ukv"] = _shard_cols(dwukv)
    out["w_out"] = g["w_out"].reshape(N_CHIPS, D_MODEL // N_CHIPS, D_MODEL)
    out["q_norm"] = g["gq_p"][0, :QK_DIM]
    out["k_norm"] = g["gk_p"][0, :QK_DIM]
    tril = jnp.tril(jnp.ones((CHUNK, CHUNK), dtype=bool))
    out["gm_ws"] = jnp.where(tril[None], g["wc"], 0.0)
    out["gm_bs"] = jnp.sum(g["bb"], axis=-1)
    for k in ("mix_norm", "q_a_norm", "kv_a_norm", "gm_v_norm", "attn_out_norm", "gm_out_norm"):
        out[k] = g[k][0]
    return out


def device_grads(x, p, positions, target, G, small):
    tabs = rope_tables(positions)
    h = x
    saved = []
    for l in range(2):
        w = layer_weights(l, G, small)
        pre = (l,)
        h1, r_a = ffn_fwd(f"l{l}a", h, small["ffn_a_norm"][l], G["ffn_a_w1"], G["ffn_a_w3"], G["ffn_a_w2"], pre)
        h2, r_m = mixer_fwd(f"l{l}", h1, w, tabs, G["w_out"], pre)
        h3, r_b = ffn_fwd(f"l{l}b", h2, small["ffn_b_norm"][l], G["ffn_b_w1"], G["ffn_b_w3"], G["ffn_b_w2"], pre)
        h4, r_p = ple_fwd(f"l{l}", h3, p[l], w, G["w_ple_gate"], G["w_ple"], pre)
        saved.append((w, h, h1, h2, h3, r_a, r_m, r_b, r_p))
        h = h4
    dh, loss = loss_grad(h, target)
    grads = [None, None]
    for l in (1, 0):
        w, h0, h1, h2, h3, r_a, r_m, r_b, r_p = saved[l]
        pre = (l,)
        gl = {}
        dh, g_p = ple_bwd(f"l{l}", dh, h3, p[l], w, G["w_ple_gate"], G["w_ple"], pre, r_p)
        gl["w_ple_gate"] = g_p["w_ple_gate"].reshape(N_CHIPS, D_MODEL // N_CHIPS, D_MODEL)
        gl["w_ple"] = g_p["w_ple"]
        gl["ple_gate_norm"], gl["ple_norm"] = g_p["ple_gate_norm"][0], g_p["ple_norm"][0]
        dh, dg, dw1, dw3, dw2 = ffn_bwd(f"l{l}b", dh, h2, small["ffn_b_norm"][l], r_b,
                                        G["ffn_b_w1"], G["ffn_b_w3"], G["ffn_b_w2"], pre)
        gl["ffn_b_norm"], gl["ffn_b_w1"], gl["ffn_b_w3"] = dg[0], dw1, dw3
        gl["ffn_b_w2"] = dw2.reshape(N_CHIPS, FF_PAD, D_MODEL)
        dh, g_m = mixer_bwd(f"l{l}", dh, h1, w, tabs, G["w_out"], pre, r_m)
        gl.update(mixer_grads_to_shards(g_m))
        dh, dg, dw1, dw3, dw2 = ffn_bwd(f"l{l}a", dh, h0, small["ffn_a_norm"][l], r_a,
                                        G["ffn_a_w1"], G["ffn_a_w3"], G["ffn_a_w2"], pre)
        gl["ffn_a_norm"], gl["ffn_a_w1"], gl["ffn_a_w3"] = dg[0], dw1, dw3
        gl["ffn_a_w2"] = dw2.reshape(N_CHIPS, FF_PAD, D_MODEL)
        grads[l] = gl
    return loss, dh, grads


MESH = pl.DeviceIdType.MESH
HBM_SPEC = pl.BlockSpec(memory_space=pltpu.HBM)


def _place():
    x, y, c = lax.axis_index("x"), lax.axis_index("y"), lax.axis_index("c")
    others = [(1 - x, y), (x, 1 - y), (1 - x, 1 - y)]
    return x, y, c, 2 * x + y, others


def gather_weights(shards):
    n = len(shards)

    def body(*refs):
        s_refs, g_refs = refs[:n], refs[n:2 * n]
        ici_send, ici_recv, d2d_send, d2d_recv, loc = refs[2 * n:]
        x, y, c, jme, others = _place()
        sib = (x, y, 1 - c)
        local = [pltpu.make_async_copy(s_refs[w], g_refs[w].at[jme], loc.at[w]) for w in range(n)]
        for cp in local:
            cp.start()
        for w in range(n):
            for (px, py) in others:
                pltpu.make_async_remote_copy(
                    src_ref=s_refs[w].at[c], dst_ref=g_refs[w].at[jme, c], send_sem=ici_send.at[w], recv_sem=ici_recv.at[w],
                    device_id=(px, py, c), device_id_type=MESH).start()
        for w in range(n):
            three = g_refs[w].at[pl.ds(0, 3), c]
            pltpu.make_async_remote_copy(src_ref=three, dst_ref=three, send_sem=ici_send.at[w], recv_sem=ici_recv.at[w],
                                         device_id=sib, device_id_type=MESH).wait_recv()
            for (px, py) in others:
                blk = g_refs[w].at[2 * px + py, c]
                pltpu.make_async_remote_copy(src_ref=blk, dst_ref=blk, send_sem=d2d_send.at[w], recv_sem=d2d_recv.at[w],
                                             device_id=sib, device_id_type=MESH).start()
        for w in range(n):
            three = g_refs[w].at[pl.ds(0, 3), c]
            wait3 = pltpu.make_async_remote_copy(src_ref=three, dst_ref=three, send_sem=d2d_send.at[w],
                                                 recv_sem=d2d_recv.at[w], device_id=sib, device_id_type=MESH)
            wait3.wait_recv()
            wait3.wait_send()
            pltpu.make_async_remote_copy(src_ref=three, dst_ref=three, send_sem=ici_send.at[w], recv_sem=ici_recv.at[w],
                                         device_id=sib, device_id_type=MESH).wait_send()
            local[w].wait()

    return pl.pallas_call(
        body, name="gather_weights",
        in_specs=[HBM_SPEC] * n, out_specs=[HBM_SPEC] * n,
        out_shape=[jax.ShapeDtypeStruct((N_CHIPS,) + s.shape, s.dtype) for s in shards],
        scratch_shapes=[pltpu.SemaphoreType.DMA((n,))] * 5,
    )(*shards)


def exchange_halves(grads):
    n = len(grads)

    def body(*refs):
        d_refs, r_refs = refs[:n], refs[n:2 * n]
        send, recv = refs[2 * n:]
        x, y, c, _, _ = _place()
        cps = []
        for w in range(n):
            half = grads[w].shape[1] // 2
            cps.append(pltpu.make_async_remote_copy(
                src_ref=d_refs[w].at[pl.ds(0, N_CHIPS), pl.ds((1 - c) * half, half)], dst_ref=r_refs[w],
                send_sem=send.at[w], recv_sem=recv.at[w], device_id=(x, y, 1 - c), device_id_type=MESH))
        for cp in cps:
            cp.start()
        for cp in cps:
            cp.wait()

    return pl.pallas_call(
        body, name="exchange_halves", in_specs=[HBM_SPEC] * n, out_specs=[HBM_SPEC] * n,
        out_shape=[jax.ShapeDtypeStruct((N_CHIPS, g.shape[1] // 2, g.shape[2]), g.dtype) for g in grads],
        scratch_shapes=[pltpu.SemaphoreType.DMA((n,))] * 2,
    )(*grads)


def scatter_slabs(parts):
    n = len(parts)

    def body(*refs):
        p_refs, q_refs = refs[:n], refs[n:2 * n]
        send, recv, loc = refs[2 * n:]
        x, y, c, jme, others = _place()
        local = [pltpu.make_async_copy(p_refs[w].at[jme], q_refs[w].at[jme], loc.at[w]) for w in range(n)]
        for cp in local:
            cp.start()
        for w in range(n):
            for (px, py) in others:
                pltpu.make_async_remote_copy(
                    src_ref=p_refs[w].at[2 * px + py], dst_ref=q_refs[w].at[jme], send_sem=send.at[w], recv_sem=recv.at[w],
                    device_id=(px, py, c), device_id_type=MESH).start()
        for w in range(n):
            three = q_refs[w].at[pl.ds(0, 3)]
            wait3 = pltpu.make_async_remote_copy(src_ref=three, dst_ref=three, send_sem=send.at[w], recv_sem=recv.at[w],
                                                 device_id=(x, y, c), device_id_type=MESH)
            wait3.wait_recv()
            wait3.wait_send()
            local[w].wait()

    return pl.pallas_call(
        body, name="scatter_slabs", in_specs=[HBM_SPEC] * n, out_specs=[HBM_SPEC] * n,
        out_shape=[jax.ShapeDtypeStruct(p.shape, p.dtype) for p in parts],
        scratch_shapes=[pltpu.SemaphoreType.DMA((n,))] * 3,
    )(*parts)


def share_halves(halves):
    n = len(halves)

    def body(*refs):
        f_refs, o_refs = refs[:2 * n], refs[2 * n:3 * n]
        send, recv, loc = refs[3 * n:]
        x, y, c, _, _ = _place()
        cps = []
        for w in range(n):
            kh = halves[w][0].shape[0]
            for l in range(2):
                src = f_refs[2 * w + l]
                dst = o_refs[w].at[l, pl.ds(c * kh, kh)]
                k = 2 * w + l
                cps.append(pltpu.make_async_copy(src, dst, loc.at[k]))
                cps.append(pltpu.make_async_remote_copy(src_ref=src, dst_ref=dst, send_sem=send.at[k], recv_sem=recv.at[k],
                                                        device_id=(x, y, 1 - c), device_id_type=MESH))
        for cp in cps:
            cp.start()
        for cp in cps:
            cp.wait()

    flat = [f for pair in halves for f in pair]
    return pl.pallas_call(
        body, name="share_halves", in_specs=[HBM_SPEC] * (2 * n), out_specs=[HBM_SPEC] * n,
        out_shape=[jax.ShapeDtypeStruct((2, 2 * p[0].shape[0], p[0].shape[1]), F32) for p in halves],
        scratch_shapes=[pltpu.SemaphoreType.DMA((2 * n,))] * 3,
    )(*flat)


def allreduce_small(v):
    R = v.shape[0]

    def body(v_ref, o_ref, sib_ref, mine_ref, all_ref, d_send, d_recv, i_send, i_recv):
        x, y, c, jme, others = _place()
        swap = pltpu.make_async_remote_copy(src_ref=v_ref, dst_ref=sib_ref, send_sem=d_send, recv_sem=d_recv,
                                            device_id=(x, y, 1 - c), device_id_type=MESH)
        swap.start()
        swap.wait()
        mine_ref[...] = v_ref[...] + sib_ref[...]
        for (px, py) in others:
            pltpu.make_async_remote_copy(src_ref=mine_ref, dst_ref=all_ref.at[jme], send_sem=i_send, recv_sem=i_recv,
                                         device_id=(px, py, c), device_id_type=MESH).start()
        three = all_ref.at[pl.ds(0, 3)]
        wait3 = pltpu.make_async_remote_copy(src_ref=three, dst_ref=three, send_sem=i_send, recv_sem=i_recv,
                                             device_id=(x, y, c), device_id_type=MESH)
        wait3.wait_recv()
        wait3.wait_send()
        all_ref[jme] = mine_ref[...]
        o_ref[...] = ((all_ref[0] + all_ref[1]) + all_ref[2]) + all_ref[3]

    vm = pl.BlockSpec(memory_space=pltpu.VMEM)
    return pl.pallas_call(
        body, name="allreduce_small", in_specs=[vm], out_specs=vm,
        out_shape=jax.ShapeDtypeStruct(v.shape, F32),
        scratch_shapes=[pltpu.VMEM((R, 128), F32), pltpu.VMEM((R, 128), F32), pltpu.VMEM((N_CHIPS, R, 128), F32),
                        pltpu.SemaphoreType.DMA, pltpu.SemaphoreType.DMA, pltpu.SemaphoreType.DMA, pltpu.SemaphoreType.DMA],
        compiler_params=pltpu.CompilerParams(vmem_limit_bytes=VMEM_LIMIT_BYTES),
    )(v)


def _row_tile(rows, width, mult=16, cap=3 << 20):
    best = mult
    for t in range(mult, rows + 1, mult):
        if rows % t == 0 and t * width * 4 <= cap:
            best = t
    assert rows % best == 0
    return best


def add_sibling(name, mine, theirs, c_arr):
    _, kh, ns = theirs.shape
    tr = _row_tile(kh, ns)
    nblk = kh // tr

    def body(c_ref, a_ref, b_ref, o_ref):
        o_ref[...] = (a_ref[...] + b_ref[...]).astype(BF16)

    return pl.pallas_call(
        body, name=name,
        grid_spec=pltpu.PrefetchScalarGridSpec(
            num_scalar_prefetch=1, grid=(N_CHIPS, nblk),
            in_specs=[pl.BlockSpec((None, tr, ns), lambda j, i, c: (j, c[0] * nblk + i, 0)),
                      pl.BlockSpec((None, tr, ns), lambda j, i, c: (j, i, 0))],
            out_specs=pl.BlockSpec((None, tr, ns), lambda j, i, c: (j, i, 0))),
        out_shape=jax.ShapeDtypeStruct(theirs.shape, BF16),
        compiler_params=_params(("parallel", "parallel")),
    )(c_arr, mine, theirs)


def add_chips(name, q):
    _, kh, ns = q.shape
    tr = _row_tile(kh, ns)

    def body(a_ref, b_ref, c_ref, d_ref, o_ref):
        f = lambda r: r[...].astype(F32)
        o_ref[...] = ((f(a_ref) + f(b_ref)) + f(c_ref)) + f(d_ref)

    specs = [pl.BlockSpec((None, tr, ns), functools.partial(lambda j, i: (j, i, 0), j)) for j in range(N_CHIPS)]
    return pl.pallas_call(
        body, name=name, grid=(kh // tr,), in_specs=specs, out_specs=pl.BlockSpec((tr, ns), lambda i: (i, 0)),
        out_shape=jax.ShapeDtypeStruct((kh, ns), F32), compiler_params=_params(("parallel",)),
    )(q, q, q, q)


ADAM_LR, ADAM_B1, ADAM_B2, ADAM_EPS, ADAM_WD, ADAM_STEP = 0.001, 0.9, 0.999, 1e-08, 0.01, 10


def adamw(name, w, g, m, v):
    _, k, ns = w.shape
    nsp = g.shape[2]
    tr = _row_tile(k, nsp, mult=8, cap=1 << 20)

    def body(w_ref, g_ref, m_ref, v_ref, go_ref, d_ref, mo_ref, vo_ref):
        gv = g_ref[:, :ns] if nsp != ns else g_ref[...]
        mn = ADAM_B1 * m_ref[...] + (1.0 - ADAM_B1) * gv
        vn = ADAM_B2 * v_ref[...] + (1.0 - ADAM_B2) * (gv * gv)
        m_hat = mn / (1.0 - ADAM_B1 ** ADAM_STEP)
        v_hat = vn / (1.0 - ADAM_B2 ** ADAM_STEP)
        go_ref[...] = gv
        d_ref[...] = -ADAM_LR * (m_hat / (jnp.sqrt(v_hat) + ADAM_EPS) + ADAM_WD * w_ref[...])
        mo_ref[...] = mn
        vo_ref[...] = vn

    blk = pl.BlockSpec((None, tr, ns), lambda l, i: (l, i, 0))
    gblk = pl.BlockSpec((None, tr, nsp), lambda l, i: (l, i, 0))
    return pl.pallas_call(
        body, name=name, grid=(2, k // tr), in_specs=[blk, gblk, blk, blk], out_specs=[blk] * 4,
        out_shape=[jax.ShapeDtypeStruct(w.shape, F32)] * 4, compiler_params=_params(("parallel", "parallel")),
    )(w, g, m, v)


WEIGHTS = ("ffn_a_norm", "ffn_a_w1", "ffn_a_w3", "ffn_a_w2", "mix_norm", "w_in", "q_a_norm", "w_uq", "kv_a_norm", "w_ukv",
           "q_norm", "k_norm", "gm_v_norm", "gm_ws", "gm_bs", "attn_out_norm", "gm_out_norm", "w_out", "ffn_b_norm",
           "ffn_b_w1", "ffn_b_w3", "ffn_b_w2", "ple_gate_norm", "w_ple_gate", "w_ple", "ple_norm")
BIG = {"ffn_a_w1": (0, FF_PAD - FF_SHARD), "ffn_a_w3": (0, FF_PAD - FF_SHARD), "ffn_a_w2": (FF_PAD - FF_SHARD, 0),
       "ffn_b_w1": (0, FF_PAD - FF_SHARD), "ffn_b_w3": (0, FF_PAD - FF_SHARD), "ffn_b_w2": (FF_PAD - FF_SHARD, 0),
       "w_in": (0, 0), "w_uq": (0, 0), "w_ukv": (0, 0), "w_ple": (0, 0), "w_out": (0, 0), "w_ple_gate": (0, 0)}
SMALL = tuple(n for n in WEIGHTS if n not in BIG)
PACK = 1024


def _pack_small(d):
    parts = []
    for n in SMALL:
        flat = d[n].reshape(-1)
        parts.append(jnp.pad(flat, (0, (-flat.shape[0]) % PACK)))
    return jnp.concatenate(parts).reshape(-1, 128)


def _unpack_small(buf, like):
    flat = buf.reshape(-1)
    out, pos = {}, 0
    for n in SMALL:
        size = math.prod(like[n].shape)
        out[n] = flat[pos:pos + size].reshape(like[n].shape)
        pos += size + (-size) % PACK
    return out


def kernel(*args):
    names = (("x", "p", "positions") + WEIGHTS + ("loss_target",) + tuple("m_" + n for n in WEIGHTS)
             + tuple("v_" + n for n in WEIGHTS))
    a = dict(zip(names, args, strict=True))
    x, p, positions, target = a["x"][0], a["p"][:, 0], a["positions"][0], a["loss_target"][0]

    shards = [jnp.pad(a[n], ((0, 0), (0, BIG[n][0]), (0, BIG[n][1]))).astype(BF16) for n in BIG]
    G = dict(zip(BIG, gather_weights(shards)))
    small = {n: a[n] for n in SMALL}

    loss, gx, grads = device_grads(x, p, positions, target, G, small)
    loss = lax.psum(loss, ("x", "y", "c"))

    c_arr = lax.axis_index("c").astype(jnp.int32).reshape(1)
    keys = [(n, l) for n in BIG for l in range(2)]
    mine = [grads[l][n] for (n, l) in keys]
    theirs = exchange_halves(mine)
    parts = [add_sibling(f"add_sibling_{n}_{l}", d, r, c_arr) for (n, l), d, r in zip(keys, mine, theirs)]
    slabs = scatter_slabs(parts)
    halves = [add_chips(f"add_chips_{n}_{l}", q) for (n, l), q in zip(keys, slabs)]
    full = dict(zip(BIG, share_halves([(halves[2 * i], halves[2 * i + 1]) for i in range(len(BIG))])))

    out_g, out_d, out_m, out_v = {}, {}, {}, {}
    for n in BIG:
        out_g[n], out_d[n], out_m[n], out_v[n] = adamw(f"adamw_{n}", a[n], full[n], a["m_" + n], a["v_" + n])

    gs = allreduce_small(_pack_small({n: jnp.stack([grads[0][n], grads[1][n]]) for n in SMALL}))
    rows = gs.shape[0] // 2
    sm = adamw("adamw_small", _pack_small(small).reshape(2, rows, 128), gs.reshape(2, rows, 128),
               _pack_small({n: a["m_" + n] for n in SMALL}).reshape(2, rows, 128),
               _pack_small({n: a["v_" + n] for n in SMALL}).reshape(2, rows, 128))
    for dst, buf in zip((out_g, out_d, out_m, out_v), sm):
        dst.update(_unpack_small(buf, small))

    return (loss, gx[None], *[out_g[n] for n in WEIGHTS], *[out_d[n] for n in WEIGHTS],
            *[out_m[n] for n in WEIGHTS], *[out_v[n] for n in WEIGHTS])
```

```python
import math

import jax
import jax.numpy as jnp
from jax import lax
from jax.experimental import pallas as pl
from jax.experimental.pallas import tpu as pltpu

F32 = jnp.float32
BF16 = jnp.bfloat16

D_MODEL = 2048
D_FF = 5504
N_CHIPS = 4
FF_SHARD = D_FF // N_CHIPS
FF_PAD = 1408
FF_P = N_CHIPS * FF_PAD
HEADS = 8
QK_NOPE = 128
QK_ROPE = 64
QK_DIM = 192
HEAD_PAD = 256
V_DIM = 128
Q_RANK = 512
KV_RANK = 256
ATTN_W = 1024
GM_W = 1024
GROUPS = 8
CHUNK = 128
PLE_DIM = 256
IN_P = 3072
EPS = 1e-6
ROPE_BASE = 10000.0
ATTN_SCALE = QK_DIM ** -0.5
VMEM_LIMIT_BYTES = 56 * 1024 * 1024


def _params(sem):
    return pltpu.CompilerParams(dimension_semantics=sem, vmem_limit_bytes=VMEM_LIMIT_BYTES)


def _bf(x):
    return x if x.dtype == BF16 else x.astype(BF16)


def _sigmoid(x):
    return 1.0 / (1.0 + jnp.exp(-x))


_GELU_C = math.sqrt(2.0 / math.pi)


def _gelu(x):
    t = jnp.tanh(_GELU_C * (x + 0.044715 * x * x * x))
    return 0.5 * x * (1.0 + t)


def _gelu_grad(x):
    t = jnp.tanh(_GELU_C * (x + 0.044715 * x * x * x))
    return 0.5 * (1.0 + t) + 0.5 * x * (1.0 - t * t) * _GELU_C * (1.0 + 3 * 0.044715 * x * x)


def op_a(a, tm, tk):
    return (a, (tm, tk), lambda i, j, k: (i, k), 1)


def op_at(a, tm, tk):
    return (a, (tk, tm), lambda i, j, k: (k, i), 0)


def op_b(b, tk, tn):
    return (b, (tk, tn), lambda i, j, k: (k, j), 0)


def op_bt(b, tk, tn):
    return (b, (tn, tk), lambda i, j, k: (j, k), 1)


def op_b_cols(g, pre, tk, tn):
    nb = g.shape[-1] // tn
    none = (None,) * (1 + len(pre))
    return (g, none + (tk, tn), lambda i, j, k: (j // nb,) + tuple(pre) + (k, j % nb), 0)


def op_b_rows(g, pre, tk, tn, koff=0):
    nb = g.shape[-2] // tk
    none = (None,) * (1 + len(pre))
    return (g, none + (tk, tn), lambda i, j, k: ((k + koff) // nb,) + tuple(pre) + ((k + koff) % nb, j), 0)


def op_b_rows_t(g, pre, tk, tn):
    nb = g.shape[-2] // tn
    none = (None,) * (1 + len(pre))
    return (g, none + (tn, tk), lambda i, j, k: (j // nb,) + tuple(pre) + (j % nb, k), 1)


def tile_mn(x, tm, tn):
    return (x, (tm, tn), lambda i, j: (i, j))


def out_mn(M, N, tm, tn, dtype):
    return (jax.ShapeDtypeStruct((M, N), dtype), (tm, tn), lambda i, j: (i, j))


def out_cols(M, ns, tm, tn, dtype):
    nb = ns // tn
    return (jax.ShapeDtypeStruct((N_CHIPS, M, ns), dtype), (None, tm, tn), lambda i, j: (j // nb, i, j % nb))


def matmul(name, grid_mnk, a_ops, b_ops, terms, n_acc, extras, outs, epilogue, acc_tile, n_outer=False):
    gm, gn, gk = grid_mnk
    na, nb, nx, no = len(a_ops), len(b_ops), len(extras), len(outs)

    def body(*refs):
        a_refs, b_refs = refs[:na], refs[na:na + nb]
        x_refs = refs[na + nb:na + nb + nx]
        o_refs = refs[na + nb + nx:na + nb + nx + no]
        acc_refs = refs[na + nb + nx + no:]
        k = pl.program_id(2)

        @pl.when(k == 0)
        def _():
            for acc in acc_refs:
                acc[...] = jnp.zeros_like(acc)

        for ai, bi, ci in terms:
            dims = (((a_ops[ai][3],), (b_ops[bi][3],)), ((), ()))
            acc_refs[ci][...] += lax.dot_general(_bf(a_refs[ai][...]), _bf(b_refs[bi][...]), dims,
                                                 preferred_element_type=F32)

        @pl.when(k == gk - 1)
        def _():
            res = epilogue([acc[...] for acc in acc_refs], [x[...] for x in x_refs])
            for o, v in zip(o_refs, res):
                o[...] = v.astype(o.dtype)

    if n_outer:
        grid = (gn, gm, gk)

        def ix3(f):
            return lambda j, i, k: f(i, j, k)

        def ix2(f):
            return lambda j, i, k: f(i, j)
    else:
        grid = (gm, gn, gk)

        def ix3(f):
            return lambda i, j, k: f(i, j, k)

        def ix2(f):
            return lambda i, j, k: f(i, j)

    in_specs = [pl.BlockSpec(blk, ix3(f)) for (_, blk, f, _) in list(a_ops) + list(b_ops)]
    in_specs += [pl.BlockSpec(blk, ix2(f)) for (_, blk, f) in extras]
    out_specs = [pl.BlockSpec(blk, ix2(f)) for (_, blk, f) in outs]
    return pl.pallas_call(
        body,
        name=name,
        grid=grid,
        in_specs=in_specs,
        out_specs=out_specs,
        out_shape=[s for (s, _, _) in outs],
        scratch_shapes=[pltpu.VMEM(acc_tile, F32) for _ in range(n_acc)],
        compiler_params=_params(("parallel", "parallel", "arbitrary")),
    )(*[o[0] for o in a_ops], *[o[0] for o in b_ops], *[x[0] for x in extras])


def _acc0(accs, xs):
    return (accs[0],)


def mm_simple(name, a, b_op_fn, M, N, K, tm, tn, tk, out_dtype=F32, a_t=False, extras=(), epilogue=_acc0, outs=None):
    a_op = op_at(a, tm, tk) if a_t else op_a(a, tm, tk)
    outs = outs or [out_mn(M, N, tm, tn, out_dtype)]
    return matmul(name, (M // tm, N // tn, K // tk), [a_op], [b_op_fn(tk, tn)], [(0, 0, 0)], 1,
                  list(extras), outs, epilogue, (tm, tn))


def rms_fwd(name, x, g, width, col_blk=0, tm=256, out_dtype=BF16):
    T = x.shape[0]

    def body(x_ref, g_ref, o_ref):
        xv = x_ref[...].astype(F32)
        r = lax.rsqrt(jnp.mean(xv * xv, axis=-1, keepdims=True) + EPS)
        o_ref[...] = (xv * r * g_ref[...]).astype(o_ref.dtype)

    return pl.pallas_call(
        body, name=name, grid=(T // tm,),
        in_specs=[pl.BlockSpec((tm, width), lambda i: (i, col_blk)), pl.BlockSpec((1, width), lambda i: (0, 0))],
        out_specs=pl.BlockSpec((tm, width), lambda i: (i, 0)),
        out_shape=jax.ShapeDtypeStruct((T, width), out_dtype),
        compiler_params=_params(("parallel",)),
    )(x, g.reshape(1, width))


def rms_bwd(name, x, g, dn, width, col_blk=0, dres=None, tm=256, with_delta=False):
    T = x.shape[0]
    has_res = dres is not None

    def body(*refs):
        x_ref, g_ref, dn_ref = refs[:3]
        pos = 3
        res_ref = None
        if has_res:
            res_ref = refs[pos]
            pos += 1
        dx_ref, dg_ref = refs[pos], refs[pos + 1]
        delta_ref = refs[pos + 2] if with_delta else None
        i = pl.program_id(0)
        xv = x_ref[...].astype(F32)
        r = lax.rsqrt(jnp.mean(xv * xv, axis=-1, keepdims=True) + EPS)
        xh = xv * r
        d = dn_ref[...].astype(F32)
        gd = d * g_ref[...]
        dx = r * (gd - xh * jnp.mean(gd * xh, axis=-1, keepdims=True))
        if has_res:
            dx = dx + res_ref[...]
        dx_ref[...] = dx.astype(dx_ref.dtype)
        part = jnp.sum(d * xh, axis=0, keepdims=True)

        @pl.when(i == 0)
        def _():
            dg_ref[...] = part

        @pl.when(i > 0)
        def _():
            dg_ref[...] += part

        if with_delta:
            for h in range(width // 128):
                sl = slice(h * 128, (h + 1) * 128)
                s = jnp.sum(dx[:, sl] * xv[:, sl], axis=-1, keepdims=True)
                delta_ref[:, sl] = jnp.broadcast_to(s, (tm, 128))

    in_specs = [pl.BlockSpec((tm, width), lambda i: (i, col_blk)), pl.BlockSpec((1, width), lambda i: (0, 0)),
                pl.BlockSpec((tm, width), lambda i: (i, 0))]
    args = [x, g.reshape(1, width), dn]
    if has_res:
        in_specs.append(pl.BlockSpec((tm, width), lambda i: (i, 0)))
        args.append(dres)
    out_specs = [pl.BlockSpec((tm, width), lambda i: (i, 0)), pl.BlockSpec((1, width), lambda i: (0, 0))]
    out_shape = [jax.ShapeDtypeStruct((T, width), F32), jax.ShapeDtypeStruct((1, width), F32)]
    if with_delta:
        out_specs.append(pl.BlockSpec((tm, width), lambda i: (i, 0)))
        out_shape.append(jax.ShapeDtypeStruct((T, width), F32))
    return pl.pallas_call(
        body, name=name, grid=(T // tm,), in_specs=in_specs, out_specs=out_specs, out_shape=out_shape,
        compiler_params=_params(("arbitrary",)),
    )(*args)


def ffn_fwd(tag, h, g, w1g, w3g, w2g, pre):
    T = h.shape[0]
    n = rms_fwd(f"{tag}_rms", h, g, D_MODEL)
    tm, tn = 512, FF_PAD

    def up_epi(accs, xs):
        a1, a3 = accs
        return a1, a3, a1 * _sigmoid(a1) * a3

    a1, a3, s = matmul(
        f"{tag}_up", (T // tm, FF_P // tn, 1),
        [op_a(n, tm, D_MODEL)], [op_b_rows_t(w1g, pre, D_MODEL, tn), op_b_rows_t(w3g, pre, D_MODEL, tn)],
        [(0, 0, 0), (0, 1, 1)], 2, [],
        [out_mn(T, FF_P, tm, tn, BF16)] * 3, up_epi, (tm, tn), n_outer=True)

    tn2 = 1024
    (h_out,) = matmul(
        f"{tag}_down", (T // tm, D_MODEL // tn2, N_CHIPS),
        [op_a(s, tm, FF_PAD)], [op_b_rows(w2g, pre, FF_PAD, tn2)],
        [(0, 0, 0)], 1, [tile_mn(h, tm, tn2)],
        [out_mn(T, D_MODEL, tm, tn2, F32)], lambda accs, xs: (xs[0] + 0.5 * accs[0],), (tm, tn2))
    return h_out, (n, a1, a3, s)


def ffn_bwd(tag, dh_out, h, g, res, w1g, w3g, w2g, pre):
    n, a1, a3, s = res
    T = h.shape[0]
    tm, tn = 512, FF_PAD

    def act_epi(accs, xs):
        ds = 0.5 * accs[0]
        x1, x3 = xs[0].astype(F32), xs[1].astype(F32)
        sg = _sigmoid(x1)
        silu = x1 * sg
        return ds * x3 * (sg + silu * (1.0 - sg)), ds * silu

    da1, da3 = matmul(
        f"{tag}_dact", (T // tm, FF_P // tn, 1),
        [op_a(dh_out, tm, D_MODEL)], [op_b_rows_t(w2g, pre, D_MODEL, tn)],
        [(0, 0, 0)], 1, [tile_mn(a1, tm, tn), tile_mn(a3, tm, tn)],
        [out_mn(T, FF_P, tm, tn, BF16)] * 2, act_epi, (tm, tn), n_outer=True)

    tk = 512

    def dw_t(nm, left, right, scale):
        (dw,) = matmul(
            f"{tag}_{nm}", (FF_P // FF_PAD, D_MODEL // 1024, T // tk),
            [op_at(left, FF_PAD, tk)], [op_b(right, tk, 1024)],
            [(0, 0, 0)], 1, [], [out_mn(FF_P, D_MODEL, FF_PAD, 1024, F32)],
            lambda accs, xs: (scale * accs[0],), (FF_PAD, 1024))
        return dw

    dw2 = dw_t("dw2", s, dh_out, 0.5)
    dw1 = dw_t("dw1", da1, n, 1.0)
    dw3 = dw_t("dw3", da3, n, 1.0)

    tn2 = 1024
    (dn,) = matmul(
        f"{tag}_dn", (T // tm, D_MODEL // tn2, N_CHIPS),
        [op_a(da1, tm, FF_PAD), op_a(da3, tm, FF_PAD)],
        [op_b_rows(w1g, pre, FF_PAD, tn2), op_b_rows(w3g, pre, FF_PAD, tn2)],
        [(0, 0, 0), (1, 1, 0)], 1, [], [out_mn(T, D_MODEL, tm, tn2, F32)], _acc0, (tm, tn2))
    dh, dg = rms_bwd(f"{tag}_rms_bwd", h, g, dn, D_MODEL, dres=dh_out)
    return dh, dg, dw1, dw3, dw2


def rope_tables(positions):
    inv_freq = ROPE_BASE ** (-jnp.arange(0, QK_ROPE, 2, dtype=F32) / QK_ROPE)
    ang = positions.astype(F32)[:, None] * inv_freq
    cos, sin = jnp.cos(ang), jnp.sin(ang)
    T = positions.shape[0]
    one, zero = jnp.ones((T, QK_NOPE), F32), jnp.zeros((T, 64), F32)
    z32, z128 = jnp.zeros((T, 32), F32), jnp.zeros((T, QK_NOPE), F32)
    c = jnp.concatenate([one, cos, cos, zero], axis=1)
    s1 = jnp.concatenate([z128, -sin, z32, zero], axis=1)
    s2 = jnp.concatenate([z128, z32, sin, zero], axis=1)
    return c, s1, s2


def _rope(y, c, s1, s2):
    return y * c + pltpu.roll(y, HEAD_PAD - 32, 1) * s1 + pltpu.roll(y, 32, 1) * s2


def _rope_t(d, c, s1, s2):
    return d * c + pltpu.roll(d * s1, 32, 1) + pltpu.roll(d * s2, HEAD_PAD - 32, 1)


def _head_norm(x):
    r = lax.rsqrt(jnp.sum(x * x, axis=-1, keepdims=True) * (1.0 / QK_DIM) + EPS)
    return x * r, r


def qk_prep_fwd(tag, q_raw, kk_raw, z_p, gq, gk, tabs, tm=256):
    T = q_raw.shape[0]
    c, s1, s2 = tabs

    def body(q_ref, k_ref, kr_ref, gq_ref, gk_ref, c_ref, s1_ref, s2_ref, qo_ref, ko_ref):
        cv, s1v, s2v = c_ref[...], s1_ref[...], s2_ref[...]
        kr = kr_ref[...]
        for h in range(HEADS):
            sl = slice(h * HEAD_PAD, (h + 1) * HEAD_PAD)
            xh, _ = _head_norm(q_ref[:, sl])
            qo_ref[:, sl] = (_rope(xh * gq_ref[...], cv, s1v, s2v) * ATTN_SCALE).astype(BF16)
            xh, _ = _head_norm(k_ref[:, sl] + kr)
            ko_ref[:, sl] = _rope(xh * gk_ref[...], cv, s1v, s2v).astype(BF16)

    row = lambda i: (i, 0)
    full = pl.BlockSpec((tm, HEADS * HEAD_PAD), row)
    tab = pl.BlockSpec((tm, HEAD_PAD), row)
    vec = pl.BlockSpec((1, HEAD_PAD), lambda i: (0, 0))
    return pl.pallas_call(
        body, name=f"{tag}_qk_prep", grid=(T // tm,),
        in_specs=[full, full, pl.BlockSpec((tm, HEAD_PAD), lambda i: (i, 3)), vec, vec, tab, tab, tab],
        out_specs=[full, full],
        out_shape=[jax.ShapeDtypeStruct((T, HEADS * HEAD_PAD), BF16)] * 2,
        compiler_params=_params(("parallel",)),
    )(q_raw, kk_raw, z_p, gq, gk, c, s1, s2)


def qk_prep_bwd(tag, dq_full, dk_full, q_raw, kk_raw, z_p, gq, gk, tabs, tm=256):
    T = q_raw.shape[0]
    c, s1, s2 = tabs

    def body(dq_ref, dk_ref, q_ref, k_ref, kr_ref, gq_ref, gk_ref, c_ref, s1_ref, s2_ref,
             dqr_ref, dkr_ref, dz_ref, dgq_ref, dgk_ref):
        i = pl.program_id(0)
        cv, s1v, s2v = c_ref[...], s1_ref[...], s2_ref[...]
        kr = kr_ref[...]
        lane = lax.broadcasted_iota(jnp.int32, (tm, HEAD_PAD), 1)
        slot = ((lane >= QK_NOPE) & (lane < QK_DIM)).astype(F32)

        def one(x, g, d):
            xh, r = _head_norm(x)
            dy = _rope_t(d, cv, s1v, s2v)
            gd = dy * g
            dx = r * (gd - xh * (jnp.sum(gd * xh, axis=-1, keepdims=True) * (1.0 / QK_DIM)))
            return dx, jnp.sum(dy * xh, axis=0, keepdims=True)

        dgq = jnp.zeros((1, HEAD_PAD), F32)
        dgk = jnp.zeros((1, HEAD_PAD), F32)
        dz = jnp.zeros((tm, HEAD_PAD), F32)
        for h in range(HEADS):
            sl = slice(h * HEAD_PAD, (h + 1) * HEAD_PAD)
            dx, dg = one(q_ref[:, sl], gq_ref[...], dq_ref[:, sl].astype(F32) * ATTN_SCALE)
            dqr_ref[:, sl] = dx
            dgq = dgq + dg
            dx, dg = one(k_ref[:, sl] + kr, gk_ref[...], dk_ref[:, sl].astype(F32))
            dkr_ref[:, sl] = dx
            dgk = dgk + dg
            dz = dz + dx
        dz_ref[...] = dz * slot

        @pl.when(i == 0)
        def _():
            dgq_ref[...] = dgq
            dgk_ref[...] = dgk

        @pl.when(i > 0)
        def _():
            dgq_ref[...] += dgq
            dgk_ref[...] += dgk

    row = lambda i: (i, 0)
    full = pl.BlockSpec((tm, HEADS * HEAD_PAD), row)
    tab = pl.BlockSpec((tm, HEAD_PAD), row)
    vec = pl.BlockSpec((1, HEAD_PAD), lambda i: (0, 0))
    return pl.pallas_call(
        body, name=f"{tag}_qk_prep_bwd", grid=(T // tm,),
        in_specs=[full, full, full, full, pl.BlockSpec((tm, HEAD_PAD), lambda i: (i, 3)), vec, vec, tab, tab, tab],
        out_specs=[full, full, tab, vec, vec],
        out_shape=[jax.ShapeDtypeStruct((T, HEADS * HEAD_PAD), F32)] * 2
        + [jax.ShapeDtypeStruct((T, HEAD_PAD), F32)] + [jax.ShapeDtypeStruct((1, HEAD_PAD), F32)] * 2,
        compiler_params=_params(("arbitrary",)),
    )(dq_full, dk_full, q_raw, kk_raw, z_p, gq, gk, c, s1, s2)


def attn_fwd(tag, q_full, k_full, vv, blk=512):
    T = q_full.shape[0]
    nb = T // blk
    neg = float(jnp.finfo(jnp.float32).min)

    def body(q_ref, k_ref, v_ref, o_ref, lse_ref, m_ref, l_ref, acc_ref):
        i, j = pl.program_id(1), pl.program_id(2)

        @pl.when(j == 0)
        def _():
            m_ref[...] = jnp.full_like(m_ref, neg)
            l_ref[...] = jnp.zeros_like(l_ref)
            acc_ref[...] = jnp.zeros_like(acc_ref)

        def step(masked):
            s = lax.dot_general(q_ref[...], k_ref[...], (((1,), (1,)), ((), ())), preferred_element_type=F32)
            if masked:
                row = lax.broadcasted_iota(jnp.int32, (blk, blk), 0)
                col = lax.broadcasted_iota(jnp.int32, (blk, blk), 1)
                s = jnp.where(col <= row, s, neg)
            m_prev = m_ref[...]
            m_new = jnp.maximum(m_prev, jnp.max(s, axis=-1, keepdims=True))
            alpha = jnp.exp(m_prev - m_new)
            p = jnp.exp(s - m_new[:, :1])
            l_ref[...] = alpha * l_ref[...] + jnp.sum(p, axis=-1, keepdims=True)
            acc_ref[...] = alpha * acc_ref[...] + jnp.dot(p.astype(BF16), v_ref[...], preferred_element_type=F32)
            m_ref[...] = m_new

        @pl.when(j < i)
        def _():
            step(False)

        @pl.when(j == i)
        def _():
            step(True)
            o_ref[...] = acc_ref[...] / l_ref[...]
            lse_ref[...] = m_ref[...] + jnp.log(l_ref[...])

    kv_ix = lambda h, i, j: (jnp.minimum(j, i), h)
    return pl.pallas_call(
        body, name=f"{tag}_attn_fwd", grid=(HEADS, nb, nb),
        in_specs=[pl.BlockSpec((blk, HEAD_PAD), lambda h, i, j: (i, h)),
                  pl.BlockSpec((blk, HEAD_PAD), kv_ix), pl.BlockSpec((blk, V_DIM), kv_ix)],
        out_specs=[pl.BlockSpec((blk, V_DIM), lambda h, i, j: (i, h))] * 2,
        out_shape=[jax.ShapeDtypeStruct((T, ATTN_W), F32)] * 2,
        scratch_shapes=[pltpu.VMEM((blk, V_DIM), F32)] * 3,
        compiler_params=_params(("parallel", "parallel", "arbitrary")),
    )(q_full, k_full, vv)


def attn_bwd(tag, q_full, k_full, vv, do, lse, delta, blk=512):
    T = q_full.shape[0]
    nb = T // blk
    neg = float(jnp.finfo(jnp.float32).min)

    def body(q_ref, k_ref, v_ref, do_ref, lse_ref, dl_ref, dq_ref, dk_ref, dv_ref, dk_acc, dv_acc):
        j, i = pl.program_id(1), pl.program_id(2)

        @pl.when((j == 0) & (i == 0))
        def _():
            dq_ref[...] = jnp.zeros_like(dq_ref)

        @pl.when(i == 0)
        def _():
            dk_acc[...] = jnp.zeros_like(dk_acc)
            dv_acc[...] = jnp.zeros_like(dv_acc)

        def step(masked):
            q, k = q_ref[...], k_ref[...]
            s = lax.dot_general(q, k, (((1,), (1,)), ((), ())), preferred_element_type=F32)
            if masked:
                row = lax.broadcasted_iota(jnp.int32, (blk, blk), 0)
                col = lax.broadcasted_iota(jnp.int32, (blk, blk), 1)
                s = jnp.where(col <= row, s, neg)
            p = jnp.exp(s - lse_ref[:, :1])
            dob = _bf(do_ref[...])
            dv_acc[...] += lax.dot_general(p.astype(BF16), dob, (((0,), (0,)), ((), ())), preferred_element_type=F32)
            dp = lax.dot_general(dob, v_ref[...], (((1,), (1,)), ((), ())), preferred_element_type=F32)
            ds = (p * (dp - dl_ref[:, :1])).astype(BF16)
            dk_acc[...] += lax.dot_general(ds, q, (((0,), (0,)), ((), ())), preferred_element_type=F32)
            rows = pl.ds(pl.multiple_of(i * blk, blk), blk)
            dq_ref[rows, :] += jnp.dot(ds, k, preferred_element_type=F32)

        @pl.when(i > j)
        def _():
            step(False)

        @pl.when(i == j)
        def _():
            step(True)

        @pl.when(i == nb - 1)
        def _():
            dk_ref[...] = dk_acc[...]
            dv_ref[...] = dv_acc[...]

    q_ix = lambda h, j, i: (jnp.maximum(i, j), h)
    kv_ix = lambda h, j, i: (j, h)
    return pl.pallas_call(
        body, name=f"{tag}_attn_bwd", grid=(HEADS, nb, nb),
        in_specs=[pl.BlockSpec((blk, HEAD_PAD), q_ix), pl.BlockSpec((blk, HEAD_PAD), kv_ix),
                  pl.BlockSpec((blk, V_DIM), kv_ix), pl.BlockSpec((blk, V_DIM), q_ix),
                  pl.BlockSpec((blk, V_DIM), q_ix), pl.BlockSpec((blk, V_DIM), q_ix)],
        out_specs=[pl.BlockSpec((T, HEAD_PAD), lambda h, j, i: (0, h)),
                   pl.BlockSpec((blk, HEAD_PAD), kv_ix), pl.BlockSpec((blk, V_DIM), kv_ix)],
        out_shape=[jax.ShapeDtypeStruct((T, HEADS * HEAD_PAD), F32)] * 2 + [jax.ShapeDtypeStruct((T, ATTN_W), F32)],
        scratch_shapes=[pltpu.VMEM((blk, HEAD_PAD), F32), pltpu.VMEM((blk, V_DIM), F32)],
        compiler_params=_params(("parallel", "arbitrary", "arbitrary")),
    )(q_full, k_full, vv, do, lse, delta)


def _gm_forward(u, v, gv, wc_ref, bb_ref, nchunk):
    ug = _gelu(u)
    vg = _gelu(v)
    rv = lax.rsqrt(jnp.mean(vg * vg, axis=-1, keepdims=True) + EPS)
    vhat = vg * rv
    vn = (vhat * gv).astype(BF16)
    gates = []
    for cidx in range(nchunk):
        rows = slice(cidx * CHUNK, (cidx + 1) * CHUNK)
        gates.append(jnp.concatenate(
            [jnp.dot(wc_ref[gidx], vn[rows, gidx * 128:(gidx + 1) * 128], preferred_element_type=F32) + bb_ref[gidx]
             for gidx in range(GROUPS)], axis=1))
    gate = jnp.concatenate(gates, axis=0)
    return ug, vhat, rv, vn, gate


def gmlp_fwd(tag, z_p, gv, gout, wc, bb, tm=256):
    T = z_p.shape[0]
    nchunk = tm // CHUNK

    def body(u_ref, v_ref, gv_ref, go_ref, wc_ref, bb_ref, o_ref):
        ug, _, _, _, gate = _gm_forward(u_ref[...], v_ref[...], gv_ref[...], wc_ref, bb_ref, nchunk)
        go = ug * gate
        ro = lax.rsqrt(jnp.mean(go * go, axis=-1, keepdims=True) + EPS)
        o_ref[...] = (go * ro * go_ref[...]).astype(BF16)

    vec = pl.BlockSpec((1, GM_W), lambda i: (0, 0))
    w3 = pl.BlockSpec((GROUPS, CHUNK, CHUNK), lambda i: (0, 0, 0))
    return pl.pallas_call(
        body, name=f"{tag}_gmlp_fwd", grid=(T // tm,),
        in_specs=[pl.BlockSpec((tm, GM_W), lambda i: (i, 1)), pl.BlockSpec((tm, GM_W), lambda i: (i, 2)), vec, vec, w3, w3],
        out_specs=pl.BlockSpec((tm, GM_W), lambda i: (i, 0)),
        out_shape=jax.ShapeDtypeStruct((T, GM_W), BF16),
        compiler_params=_params(("parallel",)),
    )(z_p, z_p, gv.reshape(1, GM_W), gout.reshape(1, GM_W), wc, bb)


def gmlp_bwd(tag, z_p, dmixed, gv, gout, wc, bb, tm=256):
    T = z_p.shape[0]
    nchunk = tm // CHUNK

    def body(u_ref, v_ref, dm_ref, gv_ref, go_ref, wc_ref, bb_ref, du_ref, dv_ref, dwc_ref, dbb_ref, dgv_ref, dgo_ref):
        i = pl.program_id(0)
        u, v = u_ref[...], v_ref[...]
        ug, vhat, rv, vn, gate = _gm_forward(u, v, gv_ref[...], wc_ref, bb_ref, nchunk)
        go = ug * gate
        ro = lax.rsqrt(jnp.mean(go * go, axis=-1, keepdims=True) + EPS)
        ohat = go * ro
        dm = dm_ref[...].astype(F32)
        dgo_part = jnp.sum(dm * ohat, axis=0, keepdims=True)
        doh = dm * go_ref[...]
        dgo = ro * (doh - ohat * jnp.mean(doh * ohat, axis=-1, keepdims=True))
        du_ref[...] = dgo * gate * _gelu_grad(u)
        dgate = dgo * ug
        dgb = dgate.astype(BF16)
        dvn_rows = []
        dwc_parts = []
        dbb_parts = []
        for gidx in range(GROUPS):
            cols = slice(gidx * 128, (gidx + 1) * 128)
            dw = jnp.zeros((CHUNK, CHUNK), F32)
            db = jnp.zeros((CHUNK, 128), F32)
            for cidx in range(nchunk):
                rows = slice(cidx * CHUNK, (cidx + 1) * CHUNK)
                dw = dw + lax.dot_general(dgb[rows, cols], vn[rows, cols], (((1,), (1,)), ((), ())),
                                          preferred_element_type=F32)
                db = db + dgate[rows, cols]
            dwc_parts.append(dw)
            dbb_parts.append(db)
        for cidx in range(nchunk):
            rows = slice(cidx * CHUNK, (cidx + 1) * CHUNK)
            dvn_rows.append(jnp.concatenate(
                [lax.dot_general(wc_ref[gidx], dgb[rows, gidx * 128:(gidx + 1) * 128], (((0,), (0,)), ((), ())),
                                 preferred_element_type=F32) for gidx in range(GROUPS)], axis=1))
        dvn = jnp.concatenate(dvn_rows, axis=0)
        dgv_part = jnp.sum(dvn * vhat, axis=0, keepdims=True)
        dvh = dvn * gv_ref[...]
        dvg = rv * (dvh - vhat * jnp.mean(dvh * vhat, axis=-1, keepdims=True))
        dv_ref[...] = dvg * _gelu_grad(v)

        @pl.when(i == 0)
        def _():
            for gidx in range(GROUPS):
                dwc_ref[gidx] = dwc_parts[gidx]
                dbb_ref[gidx] = dbb_parts[gidx]
            dgv_ref[...] = dgv_part
            dgo_ref[...] = dgo_part

        @pl.when(i > 0)
        def _():
            for gidx in range(GROUPS):
                dwc_ref[gidx] += dwc_parts[gidx]
                dbb_ref[gidx] += dbb_parts[gidx]
            dgv_ref[...] += dgv_part
            dgo_ref[...] += dgo_part

    vec = pl.BlockSpec((1, GM_W), lambda i: (0, 0))
    w3 = pl.BlockSpec((GROUPS, CHUNK, CHUNK), lambda i: (0, 0, 0))
    blk = pl.BlockSpec((tm, GM_W), lambda i: (i, 0))
    return pl.pallas_call(
        body, name=f"{tag}_gmlp_bwd", grid=(T // tm,),
        in_specs=[pl.BlockSpec((tm, GM_W), lambda i: (i, 1)), pl.BlockSpec((tm, GM_W), lambda i: (i, 2)),
                  pl.BlockSpec((tm, GM_W), lambda i: (i, 1)), vec, vec, w3, w3],
        out_specs=[blk, blk, w3, w3, vec, vec],
        out_shape=[jax.ShapeDtypeStruct((T, GM_W), F32)] * 2 + [jax.ShapeDtypeStruct((GROUPS, CHUNK, CHUNK), F32)] * 2
        + [jax.ShapeDtypeStruct((1, GM_W), F32)] * 2,
        compiler_params=_params(("arbitrary",)),
    )(z_p, z_p, dmixed, gv.reshape(1, GM_W), gout.reshape(1, GM_W), wc, bb)


def mixer_fwd(tag, h, w, tabs, wout_g, pre):
    T = h.shape[0]
    n2 = rms_fwd(f"{tag}_mix_rms", h, w["mix_norm"], D_MODEL)
    (z_p,) = mm_simple(f"{tag}_win", n2, lambda tk, tn: op_bt(w["w_in_pt"], tk, tn), T, IN_P, D_MODEL, 512, 1024, D_MODEL)
    cqn = rms_fwd(f"{tag}_cq_rms", z_p, w["q_a_norm"], Q_RANK, col_blk=0)
    ckvn = rms_fwd(f"{tag}_ckv_rms", z_p, w["kv_a_norm"], KV_RANK, col_blk=2)
    (q_raw,) = mm_simple(f"{tag}_wq", cqn, lambda tk, tn: op_b(w["wq_p"], tk, tn), T, 2048, Q_RANK, 512, 1024, Q_RANK)
    (kk_raw,) = mm_simple(f"{tag}_wk", ckvn, lambda tk, tn: op_b(w["wk_p"], tk, tn), T, 2048, KV_RANK, 512, 1024, KV_RANK)
    (vv,) = mm_simple(f"{tag}_wv", ckvn, lambda tk, tn: op_b(w["wv"], tk, tn), T, ATTN_W, KV_RANK, 512, 1024, KV_RANK,
                      out_dtype=BF16)
    q_full, k_full = qk_prep_fwd(tag, q_raw, kk_raw, z_p, w["gq_p"], w["gk_p"], tabs)
    a_out, lse = attn_fwd(tag, q_full, k_full, vv)
    mixed_a = rms_fwd(f"{tag}_ao_rms", a_out, w["attn_out_norm"], ATTN_W)
    mixed_g = gmlp_fwd(tag, z_p, w["gm_v_norm"], w["gm_out_norm"], w["wc"], w["bb"])
    tm, tn, tk = 512, 1024, 512
    (h2,) = matmul(
        f"{tag}_wout", (T // tm, D_MODEL // tn, ATTN_W // tk),
        [op_a(mixed_a, tm, tk), op_a(mixed_g, tm, tk)],
        [op_b_rows(wout_g, pre, tk, tn), op_b_rows(wout_g, pre, tk, tn, koff=ATTN_W // tk)],
        [(0, 0, 0), (1, 1, 0)], 1, [tile_mn(h, tm, tn)], [out_mn(T, D_MODEL, tm, tn, F32)],
        lambda accs, xs: (xs[0] + accs[0],), (tm, tn))
    res = dict(n2=n2, z_p=z_p, cqn=cqn, ckvn=ckvn, q_raw=q_raw, kk_raw=kk_raw, vv=vv, q_full=q_full, k_full=k_full,
               a_out=a_out, lse=lse, mixed_a=mixed_a, mixed_g=mixed_g)
    return h2, res


def mixer_bwd(tag, dh2, h, w, tabs, wout_g, pre, r):
    T = h.shape[0]
    g = {}
    (dmixed,) = mm_simple(f"{tag}_dmixed", dh2, lambda tk, tn: op_b_rows_t(wout_g, pre, tk, tn), T, D_MODEL, D_MODEL,
                          512, 512, D_MODEL)
    (dwo_a,) = mm_simple(f"{tag}_dwout_a", r["mixed_a"], lambda tk, tn: op_b(dh2, tk, tn), ATTN_W, D_MODEL, T,
                         1024, 1024, 512, a_t=True)
    (dwo_g,) = mm_simple(f"{tag}_dwout_g", r["mixed_g"], lambda tk, tn: op_b(dh2, tk, tn), GM_W, D_MODEL, T,
                         1024, 1024, 512, a_t=True)
    g["w_out"] = jnp.concatenate([dwo_a, dwo_g], axis=0)
    da_out, g["attn_out_norm"], delta = rms_bwd(f"{tag}_ao_rms_bwd", r["a_out"], w["attn_out_norm"], dmixed, ATTN_W,
                                                with_delta=True)
    dq_full, dk_full, dvv = attn_bwd(tag, r["q_full"], r["k_full"], r["vv"], da_out, r["lse"], delta)
    dq_raw, dkk_raw, dzkr, g["gq_p"], g["gk_p"] = qk_prep_bwd(tag, dq_full, dk_full, r["q_raw"], r["kk_raw"], r["z_p"],
                                                            w["gq_p"], w["gk_p"], tabs)
    (g["wq_p"],) = mm_simple(f"{tag}_dwq", r["cqn"], lambda tk, tn: op_b(dq_raw, tk, tn), Q_RANK, 2048, T, Q_RANK, 1024, 512,
                             a_t=True)
    (g["wk_p"],) = mm_simple(f"{tag}_dwk", r["ckvn"], lambda tk, tn: op_b(dkk_raw, tk, tn), KV_RANK, 2048, T, KV_RANK, 1024,
                             512, a_t=True)
    (g["wv"],) = mm_simple(f"{tag}_dwv", r["ckvn"], lambda tk, tn: op_b(dvv, tk, tn), KV_RANK, ATTN_W, T, KV_RANK, 1024, 512,
                           a_t=True)
    (dcqn,) = mm_simple(f"{tag}_dcqn", dq_raw, lambda tk, tn: op_bt(w["wq_p"], tk, tn), T, Q_RANK, 2048, 512, Q_RANK, 2048)
    (dck1,) = mm_simple(f"{tag}_dckvn_k", dkk_raw, lambda tk, tn: op_bt(w["wk_p"], tk, tn), T, KV_RANK, 2048, 512, KV_RANK,
                        2048)
    (dckvn,) = mm_simple(f"{tag}_dckvn_v", dvv, lambda tk, tn: op_bt(w["wv"], tk, tn), T, KV_RANK, ATTN_W, 512, KV_RANK,
                         ATTN_W, extras=[tile_mn(dck1, 512, KV_RANK)], epilogue=lambda accs, xs: (accs[0] + xs[0],))
    dc_q, g["q_a_norm"] = rms_bwd(f"{tag}_cq_rms_bwd", r["z_p"], w["q_a_norm"], dcqn, Q_RANK, col_blk=0)
    dc_kv, g["kv_a_norm"] = rms_bwd(f"{tag}_ckv_rms_bwd", r["z_p"], w["kv_a_norm"], dckvn, KV_RANK, col_blk=2)
    du, dv, g["wc"], g["bb"], g["gm_v_norm"], g["gm_out_norm"] = gmlp_bwd(
        tag, r["z_p"], dmixed, w["gm_v_norm"], w["gm_out_norm"], w["wc"], w["bb"])
    dz_p = jnp.concatenate([dc_q, dc_kv, dzkr, du, dv], axis=1).astype(BF16)
    (g["w_in_pt"],) = mm_simple(f"{tag}_dwin", dz_p, lambda tk, tn: op_b(r["n2"], tk, tn), IN_P, D_MODEL, T, 1024, 1024, 512,
                                a_t=True)
    (dn2,) = mm_simple(f"{tag}_dn2", dz_p, lambda tk, tn: op_b(w["w_in_pt"], tk, tn), T, D_MODEL, IN_P, 512, 1024, IN_P)
    dh1, g["mix_norm"] = rms_bwd(f"{tag}_mix_rms_bwd", h, w["mix_norm"], dn2, D_MODEL, dres=dh2)
    return dh1, g


def ple_fwd(tag, h3, p_l, w, wpg_g, wple_g, pre):
    T = h3.shape[0]
    (pw,) = mm_simple(f"{tag}_wple", p_l, lambda tk, tn: op_b_cols(wple_g, pre, tk, tn), T, D_MODEL, PLE_DIM, 512, 512,
                      PLE_DIM)
    e = rms_fwd(f"{tag}_ple_rms", pw, w["ple_norm"], D_MODEL, out_dtype=F32)
    n4 = rms_fwd(f"{tag}_pg_rms", h3, w["ple_gate_norm"], D_MODEL)

    def epi(accs, xs):
        gt = _sigmoid(accs[0])
        return xs[0] + gt * xs[1], gt

    tm, tn, tk = 512, 1024, 512
    h4, gate = matmul(
        f"{tag}_wpg", (T // tm, D_MODEL // tn, D_MODEL // tk),
        [op_a(n4, tm, tk)], [op_b_rows(wpg_g, pre, tk, tn)], [(0, 0, 0)], 1,
        [tile_mn(h3, tm, tn), tile_mn(e, tm, tn)],
        [out_mn(T, D_MODEL, tm, tn, F32), out_mn(T, D_MODEL, tm, tn, BF16)], epi, (tm, tn))
    return h4, dict(pw=pw, e=e, n4=n4, gate=gate)


def ple_bwd(tag, dh4, h3, p_l, w, wpg_g, wple_g, pre, r, tm=256):
    T = h3.shape[0]

    def act_body(d_ref, g_ref, e_ref, dpre_ref, de_ref):
        d, gt = d_ref[...], g_ref[...].astype(F32)
        dpre_ref[...] = (d * e_ref[...] * gt * (1.0 - gt)).astype(BF16)
        de_ref[...] = d * gt

    blk = pl.BlockSpec((tm, D_MODEL), lambda i: (i, 0))
    dpre, de = pl.pallas_call(
        act_body, name=f"{tag}_ple_act_bwd", grid=(T // tm,), in_specs=[blk, blk, blk], out_specs=[blk, blk],
        out_shape=[jax.ShapeDtypeStruct((T, D_MODEL), BF16), jax.ShapeDtypeStruct((T, D_MODEL), F32)],
        compiler_params=_params(("parallel",)),
    )(dh4, r["gate"], r["e"])
    g = {}
    (g["w_ple_gate"],) = mm_simple(f"{tag}_dwpg", r["n4"], lambda tk, tn: op_b(dpre, tk, tn), D_MODEL, D_MODEL, T,
                                   1024, 1024, 512, a_t=True)
    (dn4,) = mm_simple(f"{tag}_dn4", dpre, lambda tk, tn: op_b_rows_t(wpg_g, pre, tk, tn), T, D_MODEL, D_MODEL, 512, 512,
                       D_MODEL)
    dh3, g["ple_gate_norm"] = rms_bwd(f"{tag}_pg_rms_bwd", h3, w["ple_gate_norm"], dn4, D_MODEL, dres=dh4)
    dpw, g["ple_norm"] = rms_bwd(f"{tag}_ple_rms_bwd", r["pw"], w["ple_norm"], de, D_MODEL)
    (g["w_ple"],) = mm_simple(f"{tag}_dwple", p_l, lambda tk, tn: op_b(dpw, tk, tn), PLE_DIM, D_MODEL, T, PLE_DIM, 512, 512,
                              a_t=True, outs=[out_cols(PLE_DIM, 512, PLE_DIM, 512, F32)])
    return dh3, g


def loss_grad(y, target, tm=256):
    T = y.shape[0]

    def body(y_ref, t_ref, dy_ref, l_ref):
        i = pl.program_id(0)
        d = y_ref[...] - t_ref[...]
        dy_ref[...] = d * (1.0 / D_MODEL)
        part = jnp.sum((d * d).reshape(tm // 8, 8, D_MODEL), axis=0)

        @pl.when(i == 0)
        def _():
            l_ref[...] = part

        @pl.when(i > 0)
        def _():
            l_ref[...] += part

    blk = pl.BlockSpec((tm, D_MODEL), lambda i: (i, 0))
    dy, part = pl.pallas_call(
        body, name="loss_grad", grid=(T // tm,), in_specs=[blk, blk],
        out_specs=[blk, pl.BlockSpec((8, D_MODEL), lambda i: (0, 0))],
        out_shape=[jax.ShapeDtypeStruct((T, D_MODEL), F32), jax.ShapeDtypeStruct((8, D_MODEL), F32)],
        compiler_params=_params(("arbitrary",)),
    )(y, target)
    return dy, 0.5 * jnp.sum(part) / D_MODEL


def _unshard_cols(g_l):
    return g_l.transpose(1, 0, 2).reshape(g_l.shape[1], -1)


def _shard_cols(w):
    return w.reshape(w.shape[0], N_CHIPS, -1).transpose(1, 0, 2)


def layer_weights(l, G, small):
    w = {k: small[k][l] for k in ("mix_norm", "q_a_norm", "kv_a_norm", "gm_v_norm", "attn_out_norm", "gm_out_norm",
                                  "ple_gate_norm", "ple_norm")}
    wint = G["w_in"][:, l].reshape(-1, D_MODEL)
    z = lambda n: jnp.zeros((n, D_MODEL), BF16)
    w["w_in_pt"] = jnp.concatenate([wint[:768], z(128), wint[768:832], z(64), wint[832:]], axis=0)
    wuq = _unshard_cols(G["w_uq"][:, l]).reshape(Q_RANK, HEADS, QK_DIM)
    w["wq_p"] = jnp.pad(wuq, ((0, 0), (0, 0), (0, HEAD_PAD - QK_DIM))).reshape(Q_RANK, HEADS * HEAD_PAD)
    wukv = _unshard_cols(G["w_ukv"][:, l]).reshape(KV_RANK, HEADS, QK_NOPE + V_DIM)
    w["wk_p"] = jnp.pad(wukv[:, :, :QK_NOPE], ((0, 0), (0, 0), (0, HEAD_PAD - QK_NOPE))).reshape(KV_RANK, HEADS * HEAD_PAD)
    w["wv"] = wukv[:, :, QK_NOPE:].reshape(KV_RANK, ATTN_W)
    w["gq_p"] = jnp.pad(small["q_norm"][l], (0, HEAD_PAD - QK_DIM)).reshape(1, HEAD_PAD)
    w["gk_p"] = jnp.pad(small["k_norm"][l], (0, HEAD_PAD - QK_DIM)).reshape(1, HEAD_PAD)
    tril = jnp.tril(jnp.ones((CHUNK, CHUNK), dtype=bool))
    w["wc"] = jnp.where(tril[None], small["gm_ws"][l], 0.0).astype(BF16)
    w["bb"] = jnp.broadcast_to(small["gm_bs"][l][:, :, None], (GROUPS, CHUNK, 128)).astype(F32)
    return w


def mixer_grads_to_shards(g):
    out = {}
    dwint = g["w_in_pt"]
    out["w_in"] = jnp.concatenate([dwint[:768], dwint[896:960], dwint[1024:]], axis=0).reshape(N_CHIPS, -1, D_MODEL)
    dwuq = g["wq_p"].reshape(Q_RANK, HEADS, HEAD_PAD)[:, :, :QK_DIM].reshape(Q_RANK, HEADS * QK_DIM)
    out["w_uq"] = _shard_cols(dwuq)
    dwukv = jnp.concatenate([g["wk_p"].reshape(KV_RANK, HEADS, HEAD_PAD)[:, :, :QK_NOPE],
                             g["wv"].reshape(KV_RANK, HEADS, V_DIM)], axis=-1).reshape(KV_RANK, HEADS * (QK_NOPE + V_DIM))
    out["w_ukv"] = _shard_cols(dwukv)
    out["w_out"] = g["w_out"].reshape(N_CHIPS, D_MODEL // N_CHIPS, D_MODEL)
    out["q_norm"] = g["gq_p"][0, :QK_DIM]
    out["k_norm"] = g["gk_p"][0, :QK_DIM]
    tril = jnp.tril(jnp.ones((CHUNK, CHUNK), dtype=bool))
    out["gm_ws"] = jnp.where(tril[None], g["wc"], 0.0)
    out["gm_bs"] = jnp.sum(g["bb"], axis=-1)
    for k in ("mix_norm", "q_a_norm", "kv_a_norm", "gm_v_norm", "attn_out_norm", "gm_out_norm"):
        out[k] = g[k][0]
    return out


def device_grads(x, p, positions, target, G, small):
    tabs = rope_tables(positions)
    h = x
    saved = []
    for l in range(2):
        w = layer_weights(l, G, small)
        pre = (l,)
        h1, r_a = ffn_fwd(f"l{l}a", h, small["ffn_a_norm"][l], G["ffn_a_w1"], G["ffn_a_w3"], G["ffn_a_w2"], pre)
        h2, r_m = mixer_fwd(f"l{l}", h1, w, tabs, G["w_out"], pre)
        h3, r_b = ffn_fwd(f"l{l}b", h2, small["ffn_b_norm"][l], G["ffn_b_w1"], G["ffn_b_w3"], G["ffn_b_w2"], pre)
        h4, r_p = ple_fwd(f"l{l}", h3, p[l], w, G["w_ple_gate"], G["w_ple"], pre)
        saved.append((w, h, h1, h2, h3, r_a, r_m, r_b, r_p))
        h = h4
    dh, loss = loss_grad(h, target)
    grads = [None, None]
    for l in (1, 0):
        w, h0, h1, h2, h3, r_a, r_m, r_b, r_p = saved[l]
        pre = (l,)
        gl = {}
        dh, g_p = ple_bwd(f"l{l}", dh, h3, p[l], w, G["w_ple_gate"], G["w_ple"], pre, r_p)
        gl["w_ple_gate"] = g_p["w_ple_gate"].reshape(N_CHIPS, D_MODEL // N_CHIPS, D_MODEL)
        gl["w_ple"] = g_p["w_ple"]
        gl["ple_gate_norm"], gl["ple_norm"] = g_p["ple_gate_norm"][0], g_p["ple_norm"][0]
        dh, dg, dw1, dw3, dw2 = ffn_bwd(f"l{l}b", dh, h2, small["ffn_b_norm"][l], r_b,
                                        G["ffn_b_w1"], G["ffn_b_w3"], G["ffn_b_w2"], pre)
        gl["ffn_b_norm"] = dg[0]
        gl["ffn_b_w1"], gl["ffn_b_w3"], gl["ffn_b_w2"] = (d.reshape(N_CHIPS, FF_PAD, D_MODEL) for d in (dw1, dw3, dw2))
        dh, g_m = mixer_bwd(f"l{l}", dh, h1, w, tabs, G["w_out"], pre, r_m)
        gl.update(mixer_grads_to_shards(g_m))
        dh, dg, dw1, dw3, dw2 = ffn_bwd(f"l{l}a", dh, h0, small["ffn_a_norm"][l], r_a,
                                        G["ffn_a_w1"], G["ffn_a_w3"], G["ffn_a_w2"], pre)
        gl["ffn_a_norm"] = dg[0]
        gl["ffn_a_w1"], gl["ffn_a_w3"], gl["ffn_a_w2"] = (d.reshape(N_CHIPS, FF_PAD, D_MODEL) for d in (dw1, dw3, dw2))
        grads[l] = gl
    return loss, dh, grads


MESH = pl.DeviceIdType.MESH
HBM_SPEC = pl.BlockSpec(memory_space=pltpu.HBM)


def _place():
    x, y, c = lax.axis_index("x"), lax.axis_index("y"), lax.axis_index("c")
    others = [(1 - x, y), (x, 1 - y), (1 - x, 1 - y)]
    return x, y, c, 2 * x + y, others


def prep_shard(name, w, rows_pad, place):
    _, ks, n = w.shape
    ksp = ks + rows_pad
    tr = _row_tile(math.gcd(ks, ksp), n, cap=2 << 20)
    nin = ks // tr

    def body(place_ref, x_ref, o_ref):
        v = x_ref[...].astype(BF16)
        if rows_pad:
            v = jnp.where(pl.program_id(1) < nin, v, jnp.zeros_like(v))
        o_ref[...] = v

    return pl.pallas_call(
        body, name=name,
        grid_spec=pltpu.PrefetchScalarGridSpec(
            num_scalar_prefetch=1, grid=(2, ksp // tr),
            in_specs=[pl.BlockSpec((None, tr, n), lambda l, i, s: (l, jnp.minimum(i, nin - 1), 0))],
            out_specs=pl.BlockSpec((None, None, tr, n), lambda l, i, s: (s[0], l, i, 0))),
        out_shape=jax.ShapeDtypeStruct((N_CHIPS, 2, ksp, n), BF16),
        compiler_params=_params(("parallel", "parallel")),
    )(place, w)


def gather_weights(slots):
    n = len(slots)

    def body(*refs):
        g_refs = refs[n:2 * n]
        ici_send, ici_recv, d2d_send, d2d_recv = refs[2 * n:]
        x, y, c, jme, others = _place()
        sib = (x, y, 1 - c)
        for w in range(n):
            for (px, py) in others:
                mine = g_refs[w].at[jme, c]
                pltpu.make_async_remote_copy(
                    src_ref=mine, dst_ref=mine, send_sem=ici_send.at[w], recv_sem=ici_recv.at[w],
                    device_id=(px, py, c), device_id_type=MESH).start()
        for w in range(n):
            three = g_refs[w].at[pl.ds(0, 3), c]
            pltpu.make_async_remote_copy(src_ref=three, dst_ref=three, send_sem=ici_send.at[w], recv_sem=ici_recv.at[w],
                                         device_id=sib, device_id_type=MESH).wait_recv()
            for (px, py) in others:
                blk = g_refs[w].at[2 * px + py, c]
                pltpu.make_async_remote_copy(src_ref=blk, dst_ref=blk, send_sem=d2d_send.at[w], recv_sem=d2d_recv.at[w],
                                             device_id=sib, device_id_type=MESH).start()
        for w in range(n):
            three = g_refs[w].at[pl.ds(0, 3), c]
            wait3 = pltpu.make_async_remote_copy(src_ref=three, dst_ref=three, send_sem=d2d_send.at[w],
                                                 recv_sem=d2d_recv.at[w], device_id=sib, device_id_type=MESH)
            wait3.wait_recv()
            wait3.wait_send()
            pltpu.make_async_remote_copy(src_ref=three, dst_ref=three, send_sem=ici_send.at[w], recv_sem=ici_recv.at[w],
                                         device_id=sib, device_id_type=MESH).wait_send()

    return pl.pallas_call(
        body, name="gather_weights",
        in_specs=[HBM_SPEC] * n, out_specs=[HBM_SPEC] * n,
        out_shape=[jax.ShapeDtypeStruct(s.shape, s.dtype) for s in slots],
        input_output_aliases={w: w for w in range(n)},
        scratch_shapes=[pltpu.SemaphoreType.DMA((n,))] * 4,
    )(*slots)


def exchange_halves(grads):
    n = len(grads)

    def body(*refs):
        d_refs, r_refs = refs[:n], refs[n:2 * n]
        send, recv = refs[2 * n:]
        x, y, c, _, _ = _place()
        cps = []
        for w in range(n):
            half = grads[w].shape[1] // 2
            cps.append(pltpu.make_async_remote_copy(
                src_ref=d_refs[w].at[pl.ds(0, N_CHIPS), pl.ds((1 - c) * half, half)], dst_ref=r_refs[w],
                send_sem=send.at[w], recv_sem=recv.at[w], device_id=(x, y, 1 - c), device_id_type=MESH))
        for cp in cps:
            cp.start()
        for cp in cps:
            cp.wait()

    return pl.pallas_call(
        body, name="exchange_halves", in_specs=[HBM_SPEC] * n, out_specs=[HBM_SPEC] * n,
        out_shape=[jax.ShapeDtypeStruct((N_CHIPS, g.shape[1] // 2, g.shape[2]), g.dtype) for g in grads],
        scratch_shapes=[pltpu.SemaphoreType.DMA((n,))] * 2,
    )(*grads)


def scatter_slabs(parts):
    n = len(parts)

    def body(*refs):
        p_refs, q_refs = refs[:n], refs[n:2 * n]
        send, recv = refs[2 * n:]
        x, y, c, jme, others = _place()
        for w in range(n):
            for (px, py) in others:
                pltpu.make_async_remote_copy(
                    src_ref=p_refs[w].at[2 * px + py], dst_ref=q_refs[w].at[jme], send_sem=send.at[w], recv_sem=recv.at[w],
                    device_id=(px, py, c), device_id_type=MESH).start()
        for w in range(n):
            three = q_refs[w].at[pl.ds(0, 3)]
            wait3 = pltpu.make_async_remote_copy(src_ref=three, dst_ref=three, send_sem=send.at[w], recv_sem=recv.at[w],
                                                 device_id=(x, y, c), device_id_type=MESH)
            wait3.wait_recv()
            wait3.wait_send()

    return pl.pallas_call(
        body, name="scatter_slabs", in_specs=[HBM_SPEC] * n, out_specs=[HBM_SPEC] * n,
        out_shape=[jax.ShapeDtypeStruct(p.shape, p.dtype) for p in parts],
        scratch_shapes=[pltpu.SemaphoreType.DMA((n,))] * 2,
    )(*parts)


def share_halves(fulls):
    n = len(fulls)

    def body(*refs):
        o_refs = refs[n:2 * n]
        send, recv = refs[2 * n:]
        x, y, c, _, _ = _place()
        cps = []
        for w in range(n):
            kh = fulls[w].shape[1] // 2
            for l in range(2):
                half = o_refs[w].at[l, pl.ds(c * kh, kh)]
                k = 2 * w + l
                cps.append(pltpu.make_async_remote_copy(src_ref=half, dst_ref=half, send_sem=send.at[k], recv_sem=recv.at[k],
                                                        device_id=(x, y, 1 - c), device_id_type=MESH))
        for cp in cps:
            cp.start()
        for cp in cps:
            cp.wait()

    return pl.pallas_call(
        body, name="share_halves", in_specs=[HBM_SPEC] * n, out_specs=[HBM_SPEC] * n,
        out_shape=[jax.ShapeDtypeStruct(f.shape, f.dtype) for f in fulls],
        input_output_aliases={w: w for w in range(n)},
        scratch_shapes=[pltpu.SemaphoreType.DMA((2 * n,))] * 2,
    )(*fulls)


def allreduce_small(v):
    R = v.shape[0]

    def body(v_ref, o_ref, sib_ref, mine_ref, all_ref, d_send, d_recv, i_send, i_recv):
        x, y, c, jme, others = _place()
        swap = pltpu.make_async_remote_copy(src_ref=v_ref, dst_ref=sib_ref, send_sem=d_send, recv_sem=d_recv,
                                            device_id=(x, y, 1 - c), device_id_type=MESH)
        swap.start()
        swap.wait()
        mine_ref[...] = v_ref[...] + sib_ref[...]
        for (px, py) in others:
            pltpu.make_async_remote_copy(src_ref=mine_ref, dst_ref=all_ref.at[jme], send_sem=i_send, recv_sem=i_recv,
                                         device_id=(px, py, c), device_id_type=MESH).start()
        three = all_ref.at[pl.ds(0, 3)]
        wait3 = pltpu.make_async_remote_copy(src_ref=three, dst_ref=three, send_sem=i_send, recv_sem=i_recv,
                                             device_id=(x, y, c), device_id_type=MESH)
        wait3.wait_recv()
        wait3.wait_send()
        all_ref[jme] = mine_ref[...]
        o_ref[...] = ((all_ref[0] + all_ref[1]) + all_ref[2]) + all_ref[3]

    vm = pl.BlockSpec(memory_space=pltpu.VMEM)
    return pl.pallas_call(
        body, name="allreduce_small", in_specs=[vm], out_specs=vm,
        out_shape=jax.ShapeDtypeStruct(v.shape, F32),
        scratch_shapes=[pltpu.VMEM((R, 128), F32), pltpu.VMEM((R, 128), F32), pltpu.VMEM((N_CHIPS, R, 128), F32),
                        pltpu.SemaphoreType.DMA, pltpu.SemaphoreType.DMA, pltpu.SemaphoreType.DMA, pltpu.SemaphoreType.DMA],
        compiler_params=pltpu.CompilerParams(vmem_limit_bytes=VMEM_LIMIT_BYTES),
    )(v)


def _row_tile(rows, width, mult=16, cap=3 << 20):
    best = rows
    for t in range(mult, rows + 1, mult):
        if rows % t == 0 and t * width * 4 <= cap:
            best = t
    return best


def add_sibling(name, mine, theirs, place):
    _, kh, ns = theirs.shape
    tr = _row_tile(kh, ns)
    nblk = kh // tr

    def body(place_ref, a_ref, b_ref, o_ref):
        o_ref[...] = (a_ref[...] + b_ref[...]).astype(BF16)

    return pl.pallas_call(
        body, name=name,
        grid_spec=pltpu.PrefetchScalarGridSpec(
            num_scalar_prefetch=1, grid=(N_CHIPS, nblk),
            in_specs=[pl.BlockSpec((None, tr, ns), lambda j, i, s: (j, s[1] * nblk + i, 0)),
                      pl.BlockSpec((None, tr, ns), lambda j, i, s: (j, i, 0))],
            out_specs=pl.BlockSpec((None, tr, ns), lambda j, i, s: (j, i, 0))),
        out_shape=jax.ShapeDtypeStruct(theirs.shape, BF16),
        compiler_params=_params(("parallel", "parallel")),
    )(place, mine, theirs)


def add_chips(name, q, p, place, layer, full=None):
    _, kh, ns = q.shape
    tr = _row_tile(kh, ns)
    nblk = kh // tr

    def body(place_ref, *refs):
        q_refs, own_ref, o_ref = refs[:N_CHIPS], refs[N_CHIPS], refs[-1]
        jme = place_ref[0]
        tot = None
        for j in range(N_CHIPS):
            v = jnp.where(jme == j, own_ref[...], q_refs[j][...]).astype(F32)
            tot = v if tot is None else tot + v
        o_ref[...] = tot

    def q_ix(j):
        return lambda i, s: (jnp.where(s[0] == j, (j + 1) % N_CHIPS, j), i, 0)

    in_specs = [pl.BlockSpec((None, tr, ns), q_ix(j)) for j in range(N_CHIPS)]
    in_specs.append(pl.BlockSpec((None, tr, ns), lambda i, s: (s[0], i, 0)))
    args = [place, q, q, q, q, p]
    aliases = {}
    if full is not None:
        in_specs.append(pl.BlockSpec(memory_space=pl.ANY))
        args.append(full)
        aliases = {len(args) - 1: 0}
    return pl.pallas_call(
        body, name=name,
        grid_spec=pltpu.PrefetchScalarGridSpec(
            num_scalar_prefetch=1, grid=(nblk,), in_specs=in_specs,
            out_specs=pl.BlockSpec((None, tr, ns), lambda i, s: (layer, s[1] * nblk + i, 0))),
        out_shape=jax.ShapeDtypeStruct((2, 2 * kh, ns), F32),
        input_output_aliases=aliases,
        compiler_params=_params(("parallel",)),
    )(*args)


ADAM_LR, ADAM_B1, ADAM_B2, ADAM_EPS, ADAM_WD, ADAM_STEP = 0.001, 0.9, 0.999, 1e-08, 0.01, 10


def adamw(name, w, g, m, v):
    _, k, ns = w.shape
    nsp = g.shape[2]
    tr = _row_tile(k, nsp, mult=8, cap=1 << 20)

    def body(w_ref, g_ref, m_ref, v_ref, go_ref, d_ref, mo_ref, vo_ref):
        gv = g_ref[:, :ns] if nsp != ns else g_ref[...]
        mn = ADAM_B1 * m_ref[...] + (1.0 - ADAM_B1) * gv
        vn = ADAM_B2 * v_ref[...] + (1.0 - ADAM_B2) * (gv * gv)
        m_hat = mn / (1.0 - ADAM_B1 ** ADAM_STEP)
        v_hat = vn / (1.0 - ADAM_B2 ** ADAM_STEP)
        go_ref[...] = gv
        d_ref[...] = -ADAM_LR * (m_hat / (jnp.sqrt(v_hat) + ADAM_EPS) + ADAM_WD * w_ref[...])
        mo_ref[...] = mn
        vo_ref[...] = vn

    blk = pl.BlockSpec((None, tr, ns), lambda l, i: (l, i, 0))
    gblk = pl.BlockSpec((None, tr, nsp), lambda l, i: (l, i, 0))
    return pl.pallas_call(
        body, name=name, grid=(2, k // tr), in_specs=[blk, gblk, blk, blk], out_specs=[blk] * 4,
        out_shape=[jax.ShapeDtypeStruct(w.shape, F32)] * 4, compiler_params=_params(("parallel", "parallel")),
    )(w, g, m, v)


WEIGHTS = ("ffn_a_norm", "ffn_a_w1", "ffn_a_w3", "ffn_a_w2", "mix_norm", "w_in", "q_a_norm", "w_uq", "kv_a_norm", "w_ukv",
           "q_norm", "k_norm", "gm_v_norm", "gm_ws", "gm_bs", "attn_out_norm", "gm_out_norm", "w_out", "ffn_b_norm",
           "ffn_b_w1", "ffn_b_w3", "ffn_b_w2", "ple_gate_norm", "w_ple_gate", "w_ple", "ple_norm")
_FF = FF_PAD - FF_SHARD
BIG = {"ffn_a_w1": _FF, "ffn_a_w3": _FF, "ffn_a_w2": _FF, "ffn_b_w1": _FF, "ffn_b_w3": _FF, "ffn_b_w2": _FF,
       "w_in": 0, "w_uq": 0, "w_ukv": 0, "w_ple": 0, "w_out": 0, "w_ple_gate": 0}
TRANSPOSED = ("ffn_a_w1", "ffn_a_w3", "ffn_b_w1", "ffn_b_w3", "w_in")
SMALL = tuple(n for n in WEIGHTS if n not in BIG)
PACK = 1024


def _pack_small(d):
    parts = []
    for n in SMALL:
        flat = d[n].reshape(-1)
        parts.append(jnp.pad(flat, (0, (-flat.shape[0]) % PACK)))
    return jnp.concatenate(parts).reshape(-1, 128)


def _unpack_small(buf, like):
    flat = buf.reshape(-1)
    out, pos = {}, 0
    for n in SMALL:
        size = math.prod(like[n].shape)
        out[n] = flat[pos:pos + size].reshape(like[n].shape)
        pos += size + (-size) % PACK
    return out


def kernel(*args):
    names = (("x", "p", "positions") + WEIGHTS + ("loss_target",) + tuple("m_" + n for n in WEIGHTS)
             + tuple("v_" + n for n in WEIGHTS))
    a = dict(zip(names, args, strict=True))
    x, p, positions, target = a["x"][0], a["p"][:, 0], a["positions"][0], a["loss_target"][0]
    for n in TRANSPOSED:
        for pre in ("", "m_", "v_"):
            a[pre + n] = jnp.swapaxes(a[pre + n], 1, 2)

    place = jnp.stack([2 * lax.axis_index("x") + lax.axis_index("y"), lax.axis_index("c")]).astype(jnp.int32)
    G = dict(zip(BIG, gather_weights([prep_shard(f"prep_{n}", a[n], BIG[n], place) for n in BIG])))
    small = {n: a[n] for n in SMALL}

    loss, gx, grads = device_grads(x, p, positions, target, G, small)
    loss = lax.psum(loss, ("x", "y", "c"))

    keys = [(n, l) for n in BIG for l in range(2)]
    mine = [grads[l][n] for (n, l) in keys]
    theirs = exchange_halves(mine)
    parts = [add_sibling(f"add_sibling_{n}_{l}", d, r, place) for (n, l), d, r in zip(keys, mine, theirs)]
    slabs = scatter_slabs(parts)
    fulls = []
    for i, n in enumerate(BIG):
        f0 = add_chips(f"add_chips_{n}_0", slabs[2 * i], parts[2 * i], place, 0)
        fulls.append(add_chips(f"add_chips_{n}_1", slabs[2 * i + 1], parts[2 * i + 1], place, 1, full=f0))
    full = dict(zip(BIG, share_halves(fulls)))

    out_g, out_d, out_m, out_v = {}, {}, {}, {}
    for n in BIG:
        outs = adamw(f"adamw_{n}", a[n], full[n], a["m_" + n], a["v_" + n])
        if n in TRANSPOSED:
            outs = [jnp.swapaxes(o, 1, 2) for o in outs]
        out_g[n], out_d[n], out_m[n], out_v[n] = outs

    gs = allreduce_small(_pack_small({n: jnp.stack([grads[0][n], grads[1][n]]) for n in SMALL}))
    rows = gs.shape[0] // 2
    sm = adamw("adamw_small", _pack_small(small).reshape(2, rows, 128), gs.reshape(2, rows, 128),
               _pack_small({n: a["m_" + n] for n in SMALL}).reshape(2, rows, 128),
               _pack_small({n: a["v_" + n] for n in SMALL}).reshape(2, rows, 128))
    for dst, buf in zip((out_g, out_d, out_m, out_v), sm):
        dst.update(_unpack_small(buf, small))

    return (loss, gx[None], *[out_g[n] for n in WEIGHTS], *[out_d[n] for n in WEIGHTS],
            *[out_m[n] for n in WEIGHTS], *[out_v[n] for n in WEIGHTS])
```

```python
import math

import jax
import jax.numpy as jnp
from jax import lax
from jax.experimental import pallas as pl
from jax.experimental.pallas import tpu as pltpu

F32 = jnp.float32
BF16 = jnp.bfloat16

D_MODEL = 2048
D_FF = 5504
N_CHIPS = 4
FF_SHARD = D_FF // N_CHIPS
FF_PAD = 1408
FF_P = N_CHIPS * FF_PAD
HEADS = 8
QK_NOPE = 128
QK_ROPE = 64
QK_DIM = 192
HEAD_PAD = 256
V_DIM = 128
Q_RANK = 512
KV_RANK = 256
ATTN_W = 1024
GM_W = 1024
GROUPS = 8
CHUNK = 128
PLE_DIM = 256
IN_P = 3072
IN_SHARD = 720
IN_SHARD_PAD = 736
EPS = 1e-6
ROPE_BASE = 10000.0
ATTN_SCALE = QK_DIM ** -0.5
VMEM_LIMIT_BYTES = 56 * 1024 * 1024


def _params(sem):
    return pltpu.CompilerParams(dimension_semantics=sem, vmem_limit_bytes=VMEM_LIMIT_BYTES)


def _bf(x):
    return x if x.dtype == BF16 else x.astype(BF16)


def _sigmoid(x):
    return 1.0 / (1.0 + jnp.exp(-x))


_GELU_C = math.sqrt(2.0 / math.pi)


def _gelu(x):
    t = jnp.tanh(_GELU_C * (x + 0.044715 * x * x * x))
    return 0.5 * x * (1.0 + t)


def _gelu_grad(x):
    t = jnp.tanh(_GELU_C * (x + 0.044715 * x * x * x))
    return 0.5 * (1.0 + t) + 0.5 * x * (1.0 - t * t) * _GELU_C * (1.0 + 3 * 0.044715 * x * x)


def op_a(a, tm, tk):
    return (a, (tm, tk), lambda i, j, k: (i, k), 1)


def op_at(a, tm, tk):
    return (a, (tk, tm), lambda i, j, k: (k, i), 0)


def op_b(b, tk, tn):
    return (b, (tk, tn), lambda i, j, k: (k, j), 0)


def op_bt(b, tk, tn):
    return (b, (tn, tk), lambda i, j, k: (j, k), 1)


def op_b_cols(g, pre, tk, tn):
    nb = g.shape[-1] // tn
    none = (None,) * (1 + len(pre))
    return (g, none + (tk, tn), lambda i, j, k: (j // nb,) + tuple(pre) + (k, j % nb), 0)


def op_b_rows(g, pre, tk, tn, koff=0):
    nb = g.shape[-2] // tk
    none = (None,) * (1 + len(pre))
    return (g, none + (tk, tn), lambda i, j, k: ((k + koff) // nb,) + tuple(pre) + ((k + koff) % nb, j), 0)


def op_b_rows_t(g, pre, tk, tn):
    nb = g.shape[-2] // tn
    none = (None,) * (1 + len(pre))
    return (g, none + (tn, tk), lambda i, j, k: (j // nb,) + tuple(pre) + (j % nb, k), 1)


def tile_mn(x, tm, tn):
    return (x, (tm, tn), lambda i, j: (i, j))


def out_mn(M, N, tm, tn, dtype):
    return (jax.ShapeDtypeStruct((M, N), dtype), (tm, tn), lambda i, j: (i, j))


def out_cols(M, ns, tm, tn, dtype):
    nb = ns // tn
    return (jax.ShapeDtypeStruct((N_CHIPS, M, ns), dtype), (None, tm, tn), lambda i, j: (j // nb, i, j % nb))


def matmul(name, grid_mnk, a_ops, b_ops, terms, n_acc, extras, outs, epilogue, acc_tile, n_outer=False):
    gm, gn, gk = grid_mnk
    na, nb, nx, no = len(a_ops), len(b_ops), len(extras), len(outs)

    def body(*refs):
        a_refs, b_refs = refs[:na], refs[na:na + nb]
        x_refs = refs[na + nb:na + nb + nx]
        o_refs = refs[na + nb + nx:na + nb + nx + no]
        acc_refs = refs[na + nb + nx + no:]
        k = pl.program_id(2)

        @pl.when(k == 0)
        def _():
            for acc in acc_refs:
                acc[...] = jnp.zeros_like(acc)

        for ai, bi, ci in terms:
            dims = (((a_ops[ai][3],), (b_ops[bi][3],)), ((), ()))
            acc_refs[ci][...] += lax.dot_general(_bf(a_refs[ai][...]), _bf(b_refs[bi][...]), dims,
                                                 preferred_element_type=F32)

        @pl.when(k == gk - 1)
        def _():
            res = epilogue([acc[...] for acc in acc_refs], [x[...] for x in x_refs])
            for o, v in zip(o_refs, res):
                o[...] = v.astype(o.dtype)

    if n_outer:
        grid = (gn, gm, gk)

        def ix3(f):
            return lambda j, i, k: f(i, j, k)

        def ix2(f):
            return lambda j, i, k: f(i, j)
    else:
        grid = (gm, gn, gk)

        def ix3(f):
            return lambda i, j, k: f(i, j, k)

        def ix2(f):
            return lambda i, j, k: f(i, j)

    in_specs = [pl.BlockSpec(blk, ix3(f)) for (_, blk, f, _) in list(a_ops) + list(b_ops)]
    in_specs += [pl.BlockSpec(blk, ix2(f)) for (_, blk, f) in extras]
    out_specs = [pl.BlockSpec(blk, ix2(f)) for (_, blk, f) in outs]
    return pl.pallas_call(
        body,
        name=name,
        grid=grid,
        in_specs=in_specs,
        out_specs=out_specs,
        out_shape=[s for (s, _, _) in outs],
        scratch_shapes=[pltpu.VMEM(acc_tile, F32) for _ in range(n_acc)],
        compiler_params=_params(("parallel", "parallel", "arbitrary")),
    )(*[o[0] for o in a_ops], *[o[0] for o in b_ops], *[x[0] for x in extras])


def _acc0(accs, xs):
    return (accs[0],)


def mm_simple(name, a, b_op_fn, M, N, K, tm, tn, tk, out_dtype=F32, a_t=False, extras=(), epilogue=_acc0, outs=None):
    a_op = op_at(a, tm, tk) if a_t else op_a(a, tm, tk)
    outs = outs or [out_mn(M, N, tm, tn, out_dtype)]
    return matmul(name, (M // tm, N // tn, K // tk), [a_op], [b_op_fn(tk, tn)], [(0, 0, 0)], 1,
                  list(extras), outs, epilogue, (tm, tn))


def rms_fwd(name, x, g, width, col_blk=0, tm=256, out_dtype=BF16):
    T = x.shape[0]

    def body(x_ref, g_ref, o_ref):
        xv = x_ref[...].astype(F32)
        r = lax.rsqrt(jnp.mean(xv * xv, axis=-1, keepdims=True) + EPS)
        o_ref[...] = (xv * r * g_ref[...]).astype(o_ref.dtype)

    return pl.pallas_call(
        body, name=name, grid=(T // tm,),
        in_specs=[pl.BlockSpec((tm, width), lambda i: (i, col_blk)), pl.BlockSpec((1, width), lambda i: (0, 0))],
        out_specs=pl.BlockSpec((tm, width), lambda i: (i, 0)),
        out_shape=jax.ShapeDtypeStruct((T, width), out_dtype),
        compiler_params=_params(("parallel",)),
    )(x, g.reshape(1, width))


def rms_bwd(name, x, g, dn, width, col_blk=0, dres=None, tm=256, with_delta=False):
    T = x.shape[0]
    has_res = dres is not None

    def body(*refs):
        x_ref, g_ref, dn_ref = refs[:3]
        pos = 3
        res_ref = None
        if has_res:
            res_ref = refs[pos]
            pos += 1
        dx_ref, dg_ref = refs[pos], refs[pos + 1]
        delta_ref = refs[pos + 2] if with_delta else None
        i = pl.program_id(0)
        xv = x_ref[...].astype(F32)
        r = lax.rsqrt(jnp.mean(xv * xv, axis=-1, keepdims=True) + EPS)
        xh = xv * r
        d = dn_ref[...].astype(F32)
        gd = d * g_ref[...]
        dx = r * (gd - xh * jnp.mean(gd * xh, axis=-1, keepdims=True))
        if has_res:
            dx = dx + res_ref[...]
        dx_ref[...] = dx.astype(dx_ref.dtype)
        part = jnp.sum(d * xh, axis=0, keepdims=True)

        @pl.when(i == 0)
        def _():
            dg_ref[...] = part

        @pl.when(i > 0)
        def _():
            dg_ref[...] += part

        if with_delta:
            for h in range(width // 128):
                sl = slice(h * 128, (h + 1) * 128)
                s = jnp.sum(dx[:, sl] * xv[:, sl], axis=-1, keepdims=True)
                delta_ref[:, sl] = jnp.broadcast_to(s, (tm, 128))

    in_specs = [pl.BlockSpec((tm, width), lambda i: (i, col_blk)), pl.BlockSpec((1, width), lambda i: (0, 0)),
                pl.BlockSpec((tm, width), lambda i: (i, 0))]
    args = [x, g.reshape(1, width), dn]
    if has_res:
        in_specs.append(pl.BlockSpec((tm, width), lambda i: (i, 0)))
        args.append(dres)
    out_specs = [pl.BlockSpec((tm, width), lambda i: (i, 0)), pl.BlockSpec((1, width), lambda i: (0, 0))]
    out_shape = [jax.ShapeDtypeStruct((T, width), F32), jax.ShapeDtypeStruct((1, width), F32)]
    if with_delta:
        out_specs.append(pl.BlockSpec((tm, width), lambda i: (i, 0)))
        out_shape.append(jax.ShapeDtypeStruct((T, width), F32))
    return pl.pallas_call(
        body, name=name, grid=(T // tm,), in_specs=in_specs, out_specs=out_specs, out_shape=out_shape,
        compiler_params=_params(("arbitrary",)),
    )(*args)


def ffn_fwd(tag, h, g, w1g, w3g, w2g, pre):
    T = h.shape[0]
    n = rms_fwd(f"{tag}_rms", h, g, D_MODEL)
    tm, tn = 512, FF_PAD

    def up_epi(accs, xs):
        a1, a3 = accs
        return a1, a3, a1 * _sigmoid(a1) * a3

    a1, a3, s = matmul(
        f"{tag}_up", (T // tm, FF_P // tn, 1),
        [op_a(n, tm, D_MODEL)], [op_b_rows_t(w1g, pre, D_MODEL, tn), op_b_rows_t(w3g, pre, D_MODEL, tn)],
        [(0, 0, 0), (0, 1, 1)], 2, [],
        [out_mn(T, FF_P, tm, tn, BF16)] * 3, up_epi, (tm, tn), n_outer=True)

    tn2 = 1024
    (h_out,) = matmul(
        f"{tag}_down", (T // tm, D_MODEL // tn2, N_CHIPS),
        [op_a(s, tm, FF_PAD)], [op_b_rows(w2g, pre, FF_PAD, tn2)],
        [(0, 0, 0)], 1, [tile_mn(h, tm, tn2)],
        [out_mn(T, D_MODEL, tm, tn2, F32)], lambda accs, xs: (xs[0] + 0.5 * accs[0],), (tm, tn2))
    return h_out, (n, a1, a3, s)


def ffn_bwd(tag, dh_out, h, g, res, w1g, w3g, w2g, pre):
    n, a1, a3, s = res
    T = h.shape[0]
    tm, tn = 512, FF_PAD

    def act_epi(accs, xs):
        ds = 0.5 * accs[0]
        x1, x3 = xs[0].astype(F32), xs[1].astype(F32)
        sg = _sigmoid(x1)
        silu = x1 * sg
        return ds * x3 * (sg + silu * (1.0 - sg)), ds * silu

    da1, da3 = matmul(
        f"{tag}_dact", (T // tm, FF_P // tn, 1),
        [op_a(dh_out, tm, D_MODEL)], [op_b_rows_t(w2g, pre, D_MODEL, tn)],
        [(0, 0, 0)], 1, [tile_mn(a1, tm, tn), tile_mn(a3, tm, tn)],
        [out_mn(T, FF_P, tm, tn, BF16)] * 2, act_epi, (tm, tn), n_outer=True)

    tk = 512

    def dw_t(nm, left, right, scale):
        (dw,) = matmul(
            f"{tag}_{nm}", (FF_P // FF_PAD, D_MODEL // 1024, T // tk),
            [op_at(left, FF_PAD, tk)], [op_b(right, tk, 1024)],
            [(0, 0, 0)], 1, [], [out_mn(FF_P, D_MODEL, FF_PAD, 1024, BF16)],
            lambda accs, xs: (scale * accs[0],), (FF_PAD, 1024))
        return dw

    dw2 = dw_t("dw2", s, dh_out, 0.5)
    dw1 = dw_t("dw1", da1, n, 1.0)
    dw3 = dw_t("dw3", da3, n, 1.0)

    tn2 = 1024
    (dn,) = matmul(
        f"{tag}_dn", (T // tm, D_MODEL // tn2, N_CHIPS),
        [op_a(da1, tm, FF_PAD), op_a(da3, tm, FF_PAD)],
        [op_b_rows(w1g, pre, FF_PAD, tn2), op_b_rows(w3g, pre, FF_PAD, tn2)],
        [(0, 0, 0), (1, 1, 0)], 1, [], [out_mn(T, D_MODEL, tm, tn2, F32)], _acc0, (tm, tn2))
    dh, dg = rms_bwd(f"{tag}_rms_bwd", h, g, dn, D_MODEL, dres=dh_out)
    return dh, dg, dw1, dw3, dw2


def rope_tables(positions):
    inv_freq = ROPE_BASE ** (-jnp.arange(0, QK_ROPE, 2, dtype=F32) / QK_ROPE)
    ang = positions.astype(F32)[:, None] * inv_freq
    cos, sin = jnp.cos(ang), jnp.sin(ang)
    T = positions.shape[0]
    one, zero = jnp.ones((T, QK_NOPE), F32), jnp.zeros((T, 64), F32)
    z32, z128 = jnp.zeros((T, 32), F32), jnp.zeros((T, QK_NOPE), F32)
    c = jnp.concatenate([one, cos, cos, zero], axis=1)
    s1 = jnp.concatenate([z128, -sin, z32, zero], axis=1)
    s2 = jnp.concatenate([z128, z32, sin, zero], axis=1)
    return c, s1, s2


def _rope(y, c, s1, s2):
    return y * c + pltpu.roll(y, HEAD_PAD - 32, 1) * s1 + pltpu.roll(y, 32, 1) * s2


def _rope_t(d, c, s1, s2):
    return d * c + pltpu.roll(d * s1, 32, 1) + pltpu.roll(d * s2, HEAD_PAD - 32, 1)


def _head_norm(x):
    r = lax.rsqrt(jnp.sum(x * x, axis=-1, keepdims=True) * (1.0 / QK_DIM) + EPS)
    return x * r, r


def qk_prep_fwd(tag, q_raw, kk_raw, z_p, gq, gk, tabs, tm=256):
    T = q_raw.shape[0]
    c, s1, s2 = tabs

    def body(q_ref, k_ref, kr_ref, gq_ref, gk_ref, c_ref, s1_ref, s2_ref, qo_ref, ko_ref):
        cv, s1v, s2v = c_ref[...], s1_ref[...], s2_ref[...]
        kr = kr_ref[...]
        for h in range(HEADS):
            sl = slice(h * HEAD_PAD, (h + 1) * HEAD_PAD)
            xh, _ = _head_norm(q_ref[:, sl])
            qo_ref[:, sl] = (_rope(xh * gq_ref[...], cv, s1v, s2v) * ATTN_SCALE).astype(BF16)
            xh, _ = _head_norm(k_ref[:, sl] + kr)
            ko_ref[:, sl] = _rope(xh * gk_ref[...], cv, s1v, s2v).astype(BF16)

    row = lambda i: (i, 0)
    full = pl.BlockSpec((tm, HEADS * HEAD_PAD), row)
    tab = pl.BlockSpec((tm, HEAD_PAD), row)
    vec = pl.BlockSpec((1, HEAD_PAD), lambda i: (0, 0))
    return pl.pallas_call(
        body, name=f"{tag}_qk_prep", grid=(T // tm,),
        in_specs=[full, full, pl.BlockSpec((tm, HEAD_PAD), lambda i: (i, 3)), vec, vec, tab, tab, tab],
        out_specs=[full, full],
        out_shape=[jax.ShapeDtypeStruct((T, HEADS * HEAD_PAD), BF16)] * 2,
        compiler_params=_params(("parallel",)),
    )(q_raw, kk_raw, z_p, gq, gk, c, s1, s2)


def qk_prep_bwd(tag, dq_full, dk_full, q_raw, kk_raw, z_p, gq, gk, tabs, tm=256):
    T = q_raw.shape[0]
    c, s1, s2 = tabs

    def body(dq_ref, dk_ref, q_ref, k_ref, kr_ref, gq_ref, gk_ref, c_ref, s1_ref, s2_ref,
             dqr_ref, dkr_ref, dz_ref, dgq_ref, dgk_ref):
        i = pl.program_id(0)
        cv, s1v, s2v = c_ref[...], s1_ref[...], s2_ref[...]
        kr = kr_ref[...]
        lane = lax.broadcasted_iota(jnp.int32, (tm, HEAD_PAD), 1)
        slot = ((lane >= QK_NOPE) & (lane < QK_DIM)).astype(F32)

        def one(x, g, d):
            xh, r = _head_norm(x)
            dy = _rope_t(d, cv, s1v, s2v)
            gd = dy * g
            dx = r * (gd - xh * (jnp.sum(gd * xh, axis=-1, keepdims=True) * (1.0 / QK_DIM)))
            return dx, jnp.sum(dy * xh, axis=0, keepdims=True)

        dgq = jnp.zeros((1, HEAD_PAD), F32)
        dgk = jnp.zeros((1, HEAD_PAD), F32)
        dz = jnp.zeros((tm, HEAD_PAD), F32)
        for h in range(HEADS):
            sl = slice(h * HEAD_PAD, (h + 1) * HEAD_PAD)
            dx, dg = one(q_ref[:, sl], gq_ref[...], dq_ref[:, sl].astype(F32) * ATTN_SCALE)
            dqr_ref[:, sl] = dx
            dgq = dgq + dg
            dx, dg = one(k_ref[:, sl] + kr, gk_ref[...], dk_ref[:, sl].astype(F32))
            dkr_ref[:, sl] = dx
            dgk = dgk + dg
            dz = dz + dx
        dz_ref[...] = dz * slot

        @pl.when(i == 0)
        def _():
            dgq_ref[...] = dgq
            dgk_ref[...] = dgk

        @pl.when(i > 0)
        def _():
            dgq_ref[...] += dgq
            dgk_ref[...] += dgk

    row = lambda i: (i, 0)
    full = pl.BlockSpec((tm, HEADS * HEAD_PAD), row)
    tab = pl.BlockSpec((tm, HEAD_PAD), row)
    vec = pl.BlockSpec((1, HEAD_PAD), lambda i: (0, 0))
    return pl.pallas_call(
        body, name=f"{tag}_qk_prep_bwd", grid=(T // tm,),
        in_specs=[full, full, full, full, pl.BlockSpec((tm, HEAD_PAD), lambda i: (i, 3)), vec, vec, tab, tab, tab],
        out_specs=[full, full, tab, vec, vec],
        out_shape=[jax.ShapeDtypeStruct((T, HEADS * HEAD_PAD), F32)] * 2
        + [jax.ShapeDtypeStruct((T, HEAD_PAD), F32)] + [jax.ShapeDtypeStruct((1, HEAD_PAD), F32)] * 2,
        compiler_params=_params(("arbitrary",)),
    )(dq_full, dk_full, q_raw, kk_raw, z_p, gq, gk, c, s1, s2)


def attn_fwd(tag, q_full, k_full, vv, blk=512):
    T = q_full.shape[0]
    nb = T // blk
    neg = float(jnp.finfo(jnp.float32).min)

    def body(q_ref, k_ref, v_ref, o_ref, lse_ref, m_ref, l_ref, acc_ref):
        i, j = pl.program_id(1), pl.program_id(2)

        @pl.when(j == 0)
        def _():
            m_ref[...] = jnp.full_like(m_ref, neg)
            l_ref[...] = jnp.zeros_like(l_ref)
            acc_ref[...] = jnp.zeros_like(acc_ref)

        def step(masked):
            s = lax.dot_general(q_ref[...], k_ref[...], (((1,), (1,)), ((), ())), preferred_element_type=F32)
            if masked:
                row = lax.broadcasted_iota(jnp.int32, (blk, blk), 0)
                col = lax.broadcasted_iota(jnp.int32, (blk, blk), 1)
                s = jnp.where(col <= row, s, neg)
            m_prev = m_ref[...]
            m_new = jnp.maximum(m_prev, jnp.max(s, axis=-1, keepdims=True))
            alpha = jnp.exp(m_prev - m_new)
            p = jnp.exp(s - m_new[:, :1])
            l_ref[...] = alpha * l_ref[...] + jnp.sum(p, axis=-1, keepdims=True)
            acc_ref[...] = alpha * acc_ref[...] + jnp.dot(p.astype(BF16), v_ref[...], preferred_element_type=F32)
            m_ref[...] = m_new

        @pl.when(j < i)
        def _():
            step(False)

        @pl.when(j == i)
        def _():
            step(True)
            o_ref[...] = acc_ref[...] / l_ref[...]
            lse_ref[...] = m_ref[...] + jnp.log(l_ref[...])

    kv_ix = lambda h, i, j: (jnp.minimum(j, i), h)
    return pl.pallas_call(
        body, name=f"{tag}_attn_fwd", grid=(HEADS, nb, nb),
        in_specs=[pl.BlockSpec((blk, HEAD_PAD), lambda h, i, j: (i, h)),
                  pl.BlockSpec((blk, HEAD_PAD), kv_ix), pl.BlockSpec((blk, V_DIM), kv_ix)],
        out_specs=[pl.BlockSpec((blk, V_DIM), lambda h, i, j: (i, h))] * 2,
        out_shape=[jax.ShapeDtypeStruct((T, ATTN_W), F32)] * 2,
        scratch_shapes=[pltpu.VMEM((blk, V_DIM), F32)] * 3,
        compiler_params=_params(("parallel", "parallel", "arbitrary")),
    )(q_full, k_full, vv)


def attn_bwd(tag, q_full, k_full, vv, do, lse, delta, blk=512):
    T = q_full.shape[0]
    nb = T // blk
    neg = float(jnp.finfo(jnp.float32).min)

    def body(q_ref, k_ref, v_ref, do_ref, lse_ref, dl_ref, dq_ref, dk_ref, dv_ref, dk_acc, dv_acc):
        j, i = pl.program_id(1), pl.program_id(2)

        @pl.when((j == 0) & (i == 0))
        def _():
            dq_ref[...] = jnp.zeros_like(dq_ref)

        @pl.when(i == 0)
        def _():
            dk_acc[...] = jnp.zeros_like(dk_acc)
            dv_acc[...] = jnp.zeros_like(dv_acc)

        def step(masked):
            q, k = q_ref[...], k_ref[...]
            s = lax.dot_general(q, k, (((1,), (1,)), ((), ())), preferred_element_type=F32)
            if masked:
                row = lax.broadcasted_iota(jnp.int32, (blk, blk), 0)
                col = lax.broadcasted_iota(jnp.int32, (blk, blk), 1)
                s = jnp.where(col <= row, s, neg)
            p = jnp.exp(s - lse_ref[:, :1])
            dob = _bf(do_ref[...])
            dv_acc[...] += lax.dot_general(p.astype(BF16), dob, (((0,), (0,)), ((), ())), preferred_element_type=F32)
            dp = lax.dot_general(dob, v_ref[...], (((1,), (1,)), ((), ())), preferred_element_type=F32)
            ds = (p * (dp - dl_ref[:, :1])).astype(BF16)
            dk_acc[...] += lax.dot_general(ds, q, (((0,), (0,)), ((), ())), preferred_element_type=F32)
            rows = pl.ds(pl.multiple_of(i * blk, blk), blk)
            dq_ref[rows, :] += jnp.dot(ds, k, preferred_element_type=F32)

        @pl.when(i > j)
        def _():
            step(False)

        @pl.when(i == j)
        def _():
            step(True)

        @pl.when(i == nb - 1)
        def _():
            dk_ref[...] = dk_acc[...]
            dv_ref[...] = dv_acc[...]

    q_ix = lambda h, j, i: (jnp.maximum(i, j), h)
    kv_ix = lambda h, j, i: (j, h)
    return pl.pallas_call(
        body, name=f"{tag}_attn_bwd", grid=(HEADS, nb, nb),
        in_specs=[pl.BlockSpec((blk, HEAD_PAD), q_ix), pl.BlockSpec((blk, HEAD_PAD), kv_ix),
                  pl.BlockSpec((blk, V_DIM), kv_ix), pl.BlockSpec((blk, V_DIM), q_ix),
                  pl.BlockSpec((blk, V_DIM), q_ix), pl.BlockSpec((blk, V_DIM), q_ix)],
        out_specs=[pl.BlockSpec((T, HEAD_PAD), lambda h, j, i: (0, h)),
                   pl.BlockSpec((blk, HEAD_PAD), kv_ix), pl.BlockSpec((blk, V_DIM), kv_ix)],
        out_shape=[jax.ShapeDtypeStruct((T, HEADS * HEAD_PAD), F32)] * 2 + [jax.ShapeDtypeStruct((T, ATTN_W), F32)],
        scratch_shapes=[pltpu.VMEM((blk, HEAD_PAD), F32), pltpu.VMEM((blk, V_DIM), F32)],
        compiler_params=_params(("parallel", "arbitrary", "arbitrary")),
    )(q_full, k_full, vv, do, lse, delta)


def _gm_forward(u, v, gv, wc_ref, bb_ref, nchunk):
    ug = _gelu(u)
    vg = _gelu(v)
    rv = lax.rsqrt(jnp.mean(vg * vg, axis=-1, keepdims=True) + EPS)
    vhat = vg * rv
    vn = (vhat * gv).astype(BF16)
    gates = []
    for cidx in range(nchunk):
        rows = slice(cidx * CHUNK, (cidx + 1) * CHUNK)
        gates.append(jnp.concatenate(
            [jnp.dot(wc_ref[gidx], vn[rows, gidx * 128:(gidx + 1) * 128], preferred_element_type=F32) + bb_ref[gidx]
             for gidx in range(GROUPS)], axis=1))
    gate = jnp.concatenate(gates, axis=0)
    return ug, vhat, rv, vn, gate


def gmlp_fwd(tag, z_p, gv, gout, wc, bb, tm=256):
    T = z_p.shape[0]
    nchunk = tm // CHUNK

    def body(u_ref, v_ref, gv_ref, go_ref, wc_ref, bb_ref, o_ref):
        ug, _, _, _, gate = _gm_forward(u_ref[...], v_ref[...], gv_ref[...], wc_ref, bb_ref, nchunk)
        go = ug * gate
        ro = lax.rsqrt(jnp.mean(go * go, axis=-1, keepdims=True) + EPS)
        o_ref[...] = (go * ro * go_ref[...]).astype(BF16)

    vec = pl.BlockSpec((1, GM_W), lambda i: (0, 0))
    w3 = pl.BlockSpec((GROUPS, CHUNK, CHUNK), lambda i: (0, 0, 0))
    return pl.pallas_call(
        body, name=f"{tag}_gmlp_fwd", grid=(T // tm,),
        in_specs=[pl.BlockSpec((tm, GM_W), lambda i: (i, 1)), pl.BlockSpec((tm, GM_W), lambda i: (i, 2)), vec, vec, w3, w3],
        out_specs=pl.BlockSpec((tm, GM_W), lambda i: (i, 0)),
        out_shape=jax.ShapeDtypeStruct((T, GM_W), BF16),
        compiler_params=_params(("parallel",)),
    )(z_p, z_p, gv.reshape(1, GM_W), gout.reshape(1, GM_W), wc, bb)


def gmlp_bwd(tag, z_p, dmixed, gv, gout, wc, bb, tm=256):
    T = z_p.shape[0]
    nchunk = tm // CHUNK

    def body(u_ref, v_ref, dm_ref, gv_ref, go_ref, wc_ref, bb_ref, du_ref, dv_ref, dwc_ref, dbb_ref, dgv_ref, dgo_ref):
        i = pl.program_id(0)
        u, v = u_ref[...], v_ref[...]
        ug, vhat, rv, vn, gate = _gm_forward(u, v, gv_ref[...], wc_ref, bb_ref, nchunk)
        go = ug * gate
        ro = lax.rsqrt(jnp.mean(go * go, axis=-1, keepdims=True) + EPS)
        ohat = go * ro
        dm = dm_ref[...].astype(F32)
        dgo_part = jnp.sum(dm * ohat, axis=0, keepdims=True)
        doh = dm * go_ref[...]
        dgo = ro * (doh - ohat * jnp.mean(doh * ohat, axis=-1, keepdims=True))
        du_ref[...] = dgo * gate * _gelu_grad(u)
        dgate = dgo * ug
        dgb = dgate.astype(BF16)
        dvn_rows = []
        dwc_parts = []
        dbb_parts = []
        for gidx in range(GROUPS):
            cols = slice(gidx * 128, (gidx + 1) * 128)
            dw = jnp.zeros((CHUNK, CHUNK), F32)
            db = jnp.zeros((CHUNK, 128), F32)
            for cidx in range(nchunk):
                rows = slice(cidx * CHUNK, (cidx + 1) * CHUNK)
                dw = dw + lax.dot_general(dgb[rows, cols], vn[rows, cols], (((1,), (1,)), ((), ())),
                                          preferred_element_type=F32)
                db = db + dgate[rows, cols]
            dwc_parts.append(dw)
            dbb_parts.append(db)
        for cidx in range(nchunk):
            rows = slice(cidx * CHUNK, (cidx + 1) * CHUNK)
            dvn_rows.append(jnp.concatenate(
                [lax.dot_general(wc_ref[gidx], dgb[rows, gidx * 128:(gidx + 1) * 128], (((0,), (0,)), ((), ())),
                                 preferred_element_type=F32) for gidx in range(GROUPS)], axis=1))
        dvn = jnp.concatenate(dvn_rows, axis=0)
        dgv_part = jnp.sum(dvn * vhat, axis=0, keepdims=True)
        dvh = dvn * gv_ref[...]
        dvg = rv * (dvh - vhat * jnp.mean(dvh * vhat, axis=-1, keepdims=True))
        dv_ref[...] = dvg * _gelu_grad(v)

        @pl.when(i == 0)
        def _():
            for gidx in range(GROUPS):
                dwc_ref[gidx] = dwc_parts[gidx]
                dbb_ref[gidx] = dbb_parts[gidx]
            dgv_ref[...] = dgv_part
            dgo_ref[...] = dgo_part

        @pl.when(i > 0)
        def _():
            for gidx in range(GROUPS):
                dwc_ref[gidx] += dwc_parts[gidx]
                dbb_ref[gidx] += dbb_parts[gidx]
            dgv_ref[...] += dgv_part
            dgo_ref[...] += dgo_part

    vec = pl.BlockSpec((1, GM_W), lambda i: (0, 0))
    w3 = pl.BlockSpec((GROUPS, CHUNK, CHUNK), lambda i: (0, 0, 0))
    blk = pl.BlockSpec((tm, GM_W), lambda i: (i, 0))
    return pl.pallas_call(
        body, name=f"{tag}_gmlp_bwd", grid=(T // tm,),
        in_specs=[pl.BlockSpec((tm, GM_W), lambda i: (i, 1)), pl.BlockSpec((tm, GM_W), lambda i: (i, 2)),
                  pl.BlockSpec((tm, GM_W), lambda i: (i, 1)), vec, vec, w3, w3],
        out_specs=[blk, blk, w3, w3, vec, vec],
        out_shape=[jax.ShapeDtypeStruct((T, GM_W), F32)] * 2 + [jax.ShapeDtypeStruct((GROUPS, CHUNK, CHUNK), F32)] * 2
        + [jax.ShapeDtypeStruct((1, GM_W), F32)] * 2,
        compiler_params=_params(("arbitrary",)),
    )(z_p, z_p, dmixed, gv.reshape(1, GM_W), gout.reshape(1, GM_W), wc, bb)


def mixer_fwd(tag, h, w, tabs, wout_g, pre):
    T = h.shape[0]
    n2 = rms_fwd(f"{tag}_mix_rms", h, w["mix_norm"], D_MODEL)
    (z_p,) = mm_simple(f"{tag}_win", n2, lambda tk, tn: op_bt(w["w_in_pt"], tk, tn), T, IN_P, D_MODEL, 512, 1024, D_MODEL)
    cqn = rms_fwd(f"{tag}_cq_rms", z_p, w["q_a_norm"], Q_RANK, col_blk=0)
    ckvn = rms_fwd(f"{tag}_ckv_rms", z_p, w["kv_a_norm"], KV_RANK, col_blk=2)
    (q_raw,) = mm_simple(f"{tag}_wq", cqn, lambda tk, tn: op_b(w["wq_p"], tk, tn), T, 2048, Q_RANK, 512, 1024, Q_RANK)
    (kk_raw,) = mm_simple(f"{tag}_wk", ckvn, lambda tk, tn: op_b(w["wk_p"], tk, tn), T, 2048, KV_RANK, 512, 1024, KV_RANK)
    (vv,) = mm_simple(f"{tag}_wv", ckvn, lambda tk, tn: op_b(w["wv"], tk, tn), T, ATTN_W, KV_RANK, 512, 1024, KV_RANK,
                      out_dtype=BF16)
    q_full, k_full = qk_prep_fwd(tag, q_raw, kk_raw, z_p, w["gq_p"], w["gk_p"], tabs)
    a_out, lse = attn_fwd(tag, q_full, k_full, vv)
    mixed_a = rms_fwd(f"{tag}_ao_rms", a_out, w["attn_out_norm"], ATTN_W)
    mixed_g = gmlp_fwd(tag, z_p, w["gm_v_norm"], w["gm_out_norm"], w["wc"], w["bb"])
    tm, tn, tk = 512, 1024, 512
    (h2,) = matmul(
        f"{tag}_wout", (T // tm, D_MODEL // tn, ATTN_W // tk),
        [op_a(mixed_a, tm, tk), op_a(mixed_g, tm, tk)],
        [op_b_rows(wout_g, pre, tk, tn), op_b_rows(wout_g, pre, tk, tn, koff=ATTN_W // tk)],
        [(0, 0, 0), (1, 1, 0)], 1, [tile_mn(h, tm, tn)], [out_mn(T, D_MODEL, tm, tn, F32)],
        lambda accs, xs: (xs[0] + accs[0],), (tm, tn))
    res = dict(n2=n2, z_p=z_p, cqn=cqn, ckvn=ckvn, q_raw=q_raw, kk_raw=kk_raw, vv=vv, q_full=q_full, k_full=k_full,
               a_out=a_out, lse=lse, mixed_a=mixed_a, mixed_g=mixed_g)
    return h2, res


def mixer_bwd(tag, dh2, h, w, tabs, wout_g, pre, r):
    T = h.shape[0]
    g = {}
    (dmixed,) = mm_simple(f"{tag}_dmixed", dh2, lambda tk, tn: op_b_rows_t(wout_g, pre, tk, tn), T, D_MODEL, D_MODEL,
                          512, 512, D_MODEL)
    (dwo_a,) = mm_simple(f"{tag}_dwout_a", r["mixed_a"], lambda tk, tn: op_b(dh2, tk, tn), ATTN_W, D_MODEL, T,
                         1024, 1024, 512, a_t=True, out_dtype=BF16)
    (dwo_g,) = mm_simple(f"{tag}_dwout_g", r["mixed_g"], lambda tk, tn: op_b(dh2, tk, tn), GM_W, D_MODEL, T,
                         1024, 1024, 512, a_t=True, out_dtype=BF16)
    g["w_out"] = jnp.concatenate([dwo_a, dwo_g], axis=0)
    da_out, g["attn_out_norm"], delta = rms_bwd(f"{tag}_ao_rms_bwd", r["a_out"], w["attn_out_norm"], dmixed, ATTN_W,
                                                with_delta=True)
    dq_full, dk_full, dvv = attn_bwd(tag, r["q_full"], r["k_full"], r["vv"], da_out, r["lse"], delta)
    dq_raw, dkk_raw, dzkr, g["gq_p"], g["gk_p"] = qk_prep_bwd(tag, dq_full, dk_full, r["q_raw"], r["kk_raw"], r["z_p"],
                                                            w["gq_p"], w["gk_p"], tabs)
    (g["wq_p"],) = mm_simple(f"{tag}_dwq", r["cqn"], lambda tk, tn: op_b(dq_raw, tk, tn), Q_RANK, 2048, T, Q_RANK, 1024, 512,
                             a_t=True, out_dtype=BF16)
    (g["wk_p"],) = mm_simple(f"{tag}_dwk", r["ckvn"], lambda tk, tn: op_b(dkk_raw, tk, tn), KV_RANK, 2048, T, KV_RANK, 1024,
                             512, a_t=True, out_dtype=BF16)
    (g["wv"],) = mm_simple(f"{tag}_dwv", r["ckvn"], lambda tk, tn: op_b(dvv, tk, tn), KV_RANK, ATTN_W, T, KV_RANK, 1024, 512,
                           a_t=True, out_dtype=BF16)
    (dcqn,) = mm_simple(f"{tag}_dcqn", dq_raw, lambda tk, tn: op_bt(w["wq_p"], tk, tn), T, Q_RANK, 2048, 512, Q_RANK, 2048)
    (dck1,) = mm_simple(f"{tag}_dckvn_k", dkk_raw, lambda tk, tn: op_bt(w["wk_p"], tk, tn), T, KV_RANK, 2048, 512, KV_RANK,
                        2048)
    (dckvn,) = mm_simple(f"{tag}_dckvn_v", dvv, lambda tk, tn: op_bt(w["wv"], tk, tn), T, KV_RANK, ATTN_W, 512, KV_RANK,
                         ATTN_W, extras=[tile_mn(dck1, 512, KV_RANK)], epilogue=lambda accs, xs: (accs[0] + xs[0],))
    dc_q, g["q_a_norm"] = rms_bwd(f"{tag}_cq_rms_bwd", r["z_p"], w["q_a_norm"], dcqn, Q_RANK, col_blk=0)
    dc_kv, g["kv_a_norm"] = rms_bwd(f"{tag}_ckv_rms_bwd", r["z_p"], w["kv_a_norm"], dckvn, KV_RANK, col_blk=2)
    du, dv, g["wc"], g["bb"], g["gm_v_norm"], g["gm_out_norm"] = gmlp_bwd(
        tag, r["z_p"], dmixed, w["gm_v_norm"], w["gm_out_norm"], w["wc"], w["bb"])
    dz_p = jnp.concatenate([dc_q, dc_kv, dzkr, du, dv], axis=1).astype(BF16)
    (g["w_in_pt"],) = mm_simple(f"{tag}_dwin", dz_p, lambda tk, tn: op_b(r["n2"], tk, tn), IN_P, D_MODEL, T, 1024, 1024, 512,
                                a_t=True, out_dtype=BF16)
    (dn2,) = mm_simple(f"{tag}_dn2", dz_p, lambda tk, tn: op_b(w["w_in_pt"], tk, tn), T, D_MODEL, IN_P, 512, 1024, IN_P)
    dh1, g["mix_norm"] = rms_bwd(f"{tag}_mix_rms_bwd", h, w["mix_norm"], dn2, D_MODEL, dres=dh2)
    return dh1, g


def ple_fwd(tag, h3, p_l, w, wpg_g, wple_g, pre):
    T = h3.shape[0]
    (pw,) = mm_simple(f"{tag}_wple", p_l, lambda tk, tn: op_b_cols(wple_g, pre, tk, tn), T, D_MODEL, PLE_DIM, 512, 512,
                      PLE_DIM)
    e = rms_fwd(f"{tag}_ple_rms", pw, w["ple_norm"], D_MODEL, out_dtype=F32)
    n4 = rms_fwd(f"{tag}_pg_rms", h3, w["ple_gate_norm"], D_MODEL)

    def epi(accs, xs):
        gt = _sigmoid(accs[0])
        return xs[0] + gt * xs[1], gt

    tm, tn, tk = 512, 1024, 512
    h4, gate = matmul(
        f"{tag}_wpg", (T // tm, D_MODEL // tn, D_MODEL // tk),
        [op_a(n4, tm, tk)], [op_b_rows(wpg_g, pre, tk, tn)], [(0, 0, 0)], 1,
        [tile_mn(h3, tm, tn), tile_mn(e, tm, tn)],
        [out_mn(T, D_MODEL, tm, tn, F32), out_mn(T, D_MODEL, tm, tn, BF16)], epi, (tm, tn))
    return h4, dict(pw=pw, e=e, n4=n4, gate=gate)


def ple_bwd(tag, dh4, h3, p_l, w, wpg_g, wple_g, pre, r, tm=256):
    T = h3.shape[0]

    def act_body(d_ref, g_ref, e_ref, dpre_ref, de_ref):
        d, gt = d_ref[...], g_ref[...].astype(F32)
        dpre_ref[...] = (d * e_ref[...] * gt * (1.0 - gt)).astype(BF16)
        de_ref[...] = d * gt

    blk = pl.BlockSpec((tm, D_MODEL), lambda i: (i, 0))
    dpre, de = pl.pallas_call(
        act_body, name=f"{tag}_ple_act_bwd", grid=(T // tm,), in_specs=[blk, blk, blk], out_specs=[blk, blk],
        out_shape=[jax.ShapeDtypeStruct((T, D_MODEL), BF16), jax.ShapeDtypeStruct((T, D_MODEL), F32)],
        compiler_params=_params(("parallel",)),
    )(dh4, r["gate"], r["e"])
    g = {}
    (g["w_ple_gate"],) = mm_simple(f"{tag}_dwpg", r["n4"], lambda tk, tn: op_b(dpre, tk, tn), D_MODEL, D_MODEL, T,
                                   1024, 1024, 512, a_t=True, out_dtype=BF16)
    (dn4,) = mm_simple(f"{tag}_dn4", dpre, lambda tk, tn: op_b_rows_t(wpg_g, pre, tk, tn), T, D_MODEL, D_MODEL, 512, 512,
                       D_MODEL)
    dh3, g["ple_gate_norm"] = rms_bwd(f"{tag}_pg_rms_bwd", h3, w["ple_gate_norm"], dn4, D_MODEL, dres=dh4)
    dpw, g["ple_norm"] = rms_bwd(f"{tag}_ple_rms_bwd", r["pw"], w["ple_norm"], de, D_MODEL)
    (g["w_ple"],) = mm_simple(f"{tag}_dwple", p_l, lambda tk, tn: op_b(dpw, tk, tn), PLE_DIM, D_MODEL, T, PLE_DIM, 512, 512,
                              a_t=True, outs=[out_cols(PLE_DIM, 512, PLE_DIM, 512, BF16)])
    return dh3, g


def loss_grad(y, target, tm=256):
    T = y.shape[0]

    def body(y_ref, t_ref, dy_ref, l_ref):
        i = pl.program_id(0)
        d = y_ref[...] - t_ref[...]
        dy_ref[...] = d * (1.0 / D_MODEL)
        part = jnp.sum((d * d).reshape(tm // 8, 8, D_MODEL), axis=0)

        @pl.when(i == 0)
        def _():
            l_ref[...] = part

        @pl.when(i > 0)
        def _():
            l_ref[...] += part

    blk = pl.BlockSpec((tm, D_MODEL), lambda i: (i, 0))
    dy, part = pl.pallas_call(
        body, name="loss_grad", grid=(T // tm,), in_specs=[blk, blk],
        out_specs=[blk, pl.BlockSpec((8, D_MODEL), lambda i: (0, 0))],
        out_shape=[jax.ShapeDtypeStruct((T, D_MODEL), F32), jax.ShapeDtypeStruct((8, D_MODEL), F32)],
        compiler_params=_params(("arbitrary",)),
    )(y, target)
    return dy, 0.5 * jnp.sum(part) / D_MODEL


def _unshard_cols(g_l):
    return g_l.transpose(1, 0, 2).reshape(g_l.shape[1], -1)


def _shard_cols(w):
    return w.reshape(w.shape[0], N_CHIPS, -1).transpose(1, 0, 2)


def layer_weights(l, Gl, small):
    w = {k: small[k][l] for k in ("mix_norm", "q_a_norm", "kv_a_norm", "gm_v_norm", "attn_out_norm", "gm_out_norm",
                                  "ple_gate_norm", "ple_norm")}
    wint = Gl["w_in"][:, :IN_SHARD].reshape(-1, D_MODEL)
    z = lambda n: jnp.zeros((n, D_MODEL), BF16)
    w["w_in_pt"] = jnp.concatenate([wint[:768], z(128), wint[768:832], z(64), wint[832:]], axis=0)
    wuq = _unshard_cols(Gl["w_uq"]).reshape(Q_RANK, HEADS, QK_DIM)
    w["wq_p"] = jnp.pad(wuq, ((0, 0), (0, 0), (0, HEAD_PAD - QK_DIM))).reshape(Q_RANK, HEADS * HEAD_PAD)
    wukv = _unshard_cols(Gl["w_ukv"]).reshape(KV_RANK, HEADS, QK_NOPE + V_DIM)
    w["wk_p"] = jnp.pad(wukv[:, :, :QK_NOPE], ((0, 0), (0, 0), (0, HEAD_PAD - QK_NOPE))).reshape(KV_RANK, HEADS * HEAD_PAD)
    w["wv"] = wukv[:, :, QK_NOPE:].reshape(KV_RANK, ATTN_W)
    w["gq_p"] = jnp.pad(small["q_norm"][l], (0, HEAD_PAD - QK_DIM)).reshape(1, HEAD_PAD)
    w["gk_p"] = jnp.pad(small["k_norm"][l], (0, HEAD_PAD - QK_DIM)).reshape(1, HEAD_PAD)
    tril = jnp.tril(jnp.ones((CHUNK, CHUNK), dtype=bool))
    w["wc"] = jnp.where(tril[None], small["gm_ws"][l], 0.0).astype(BF16)
    w["bb"] = jnp.broadcast_to(small["gm_bs"][l][:, :, None], (GROUPS, CHUNK, 128)).astype(F32)
    return w


def mixer_grads_to_shards(g):
    out = {}
    dwint = g["w_in_pt"]
    dwint = jnp.concatenate([dwint[:768], dwint[896:960], dwint[1024:]], axis=0).reshape(N_CHIPS, IN_SHARD, D_MODEL)
    out["w_in"] = jnp.pad(dwint, ((0, 0), (0, IN_SHARD_PAD - IN_SHARD), (0, 0)))
    dwuq = g["wq_p"].reshape(Q_RANK, HEADS, HEAD_PAD)[:, :, :QK_DIM].reshape(Q_RANK, HEADS * QK_DIM)
    out["w_uq"] = _shard_cols(dwuq)
    dwukv = jnp.concatenate([g["wk_p"].reshape(KV_RANK, HEADS, HEAD_PAD)[:, :, :QK_NOPE],
                             g["wv"].reshape(KV_RANK, HEADS, V_DIM)], axis=-1).reshape(KV_RANK, HEADS * (QK_NOPE + V_DIM))
    out["w_ukv"] = _shard_cols(dwukv)
    out["w_out"] = g["w_out"].reshape(N_CHIPS, D_MODEL // N_CHIPS, D_MODEL)
    out["q_norm"] = g["gq_p"][0, :QK_DIM]
    out["k_norm"] = g["gk_p"][0, :QK_DIM]
    tril = jnp.tril(jnp.ones((CHUNK, CHUNK), dtype=bool))
    out["gm_ws"] = jnp.where(tril[None], g["wc"], 0.0)
    out["gm_bs"] = jnp.sum(g["bb"], axis=-1)
    for k in ("mix_norm", "q_a_norm", "kv_a_norm", "gm_v_norm", "attn_out_norm", "gm_out_norm"):
        out[k] = g[k][0]
    return out


def layer_fwd(l, h, p_l, Gl, small, tabs):
    w = layer_weights(l, Gl, small)
    h1, r_a = ffn_fwd(f"l{l}a", h, small["ffn_a_norm"][l], Gl["ffn_a_w1"], Gl["ffn_a_w3"], Gl["ffn_a_w2"], ())
    h2, r_m = mixer_fwd(f"l{l}", h1, w, tabs, Gl["w_out"], ())
    h3, r_b = ffn_fwd(f"l{l}b", h2, small["ffn_b_norm"][l], Gl["ffn_b_w1"], Gl["ffn_b_w3"], Gl["ffn_b_w2"], ())
    h4, r_p = ple_fwd(f"l{l}", h3, p_l, w, Gl["w_ple_gate"], Gl["w_ple"], ())
    return h4, (w, h, h1, h2, h3, r_a, r_m, r_b, r_p)


def layer_bwd(l, dh, p_l, Gl, small, tabs, saved):
    w, h0, h1, h2, h3, r_a, r_m, r_b, r_p = saved
    slabs = lambda d: d.reshape(N_CHIPS, FF_PAD, D_MODEL)
    gl = {}
    dh, g_p = ple_bwd(f"l{l}", dh, h3, p_l, w, Gl["w_ple_gate"], Gl["w_ple"], (), r_p)
    gl["w_ple_gate"] = g_p["w_ple_gate"].reshape(N_CHIPS, D_MODEL // N_CHIPS, D_MODEL)
    gl["w_ple"] = g_p["w_ple"]
    gl["ple_gate_norm"], gl["ple_norm"] = g_p["ple_gate_norm"][0], g_p["ple_norm"][0]
    dh, dg, dw1, dw3, dw2 = ffn_bwd(f"l{l}b", dh, h2, small["ffn_b_norm"][l], r_b,
                                    Gl["ffn_b_w1"], Gl["ffn_b_w3"], Gl["ffn_b_w2"], ())
    gl["ffn_b_norm"] = dg[0]
    gl["ffn_b_w1"], gl["ffn_b_w3"], gl["ffn_b_w2"] = slabs(dw1), slabs(dw3), slabs(dw2)
    dh, g_m = mixer_bwd(f"l{l}", dh, h1, w, tabs, Gl["w_out"], (), r_m)
    gl.update(mixer_grads_to_shards(g_m))
    dh, dg, dw1, dw3, dw2 = ffn_bwd(f"l{l}a", dh, h0, small["ffn_a_norm"][l], r_a,
                                    Gl["ffn_a_w1"], Gl["ffn_a_w3"], Gl["ffn_a_w2"], ())
    gl["ffn_a_norm"] = dg[0]
    gl["ffn_a_w1"], gl["ffn_a_w3"], gl["ffn_a_w2"] = slabs(dw1), slabs(dw3), slabs(dw2)
    return dh, gl


MESH = pl.DeviceIdType.MESH
HBM_SPEC = pl.BlockSpec(memory_space=pltpu.HBM)


def _place():
    x, y, c = lax.axis_index("x"), lax.axis_index("y"), lax.axis_index("c")
    others = [(1 - x, y), (x, 1 - y), (1 - x, 1 - y)]
    return x, y, c, 2 * x + y, others


def prep_shard(name, w, layer, rows_pad, place):
    _, ks, n = w.shape
    ksp = ks + rows_pad
    tc = 512 if n % 512 == 0 else n

    def body(place_ref, x_ref, o_ref):
        o_ref[:ks] = x_ref[...].astype(BF16)
        if rows_pad:
            o_ref[ks:] = jnp.zeros((rows_pad, tc), BF16)

    return pl.pallas_call(
        body, name=name,
        grid_spec=pltpu.PrefetchScalarGridSpec(
            num_scalar_prefetch=1, grid=(n // tc,),
            in_specs=[pl.BlockSpec((None, ks, tc), lambda i, s: (layer, 0, i))],
            out_specs=pl.BlockSpec((None, ksp, tc), lambda i, s: (s[0], 0, i))),
        out_shape=jax.ShapeDtypeStruct((N_CHIPS, ksp, n), BF16),
        compiler_params=_params(("parallel",)),
    )(place, w)


def gather_weights(name, slots):
    n = len(slots)

    def body(*refs):
        g_refs = refs[n:2 * n]
        ici_send, ici_recv, d2d_send, d2d_recv = refs[2 * n:]
        x, y, c, jme, others = _place()
        sib = (x, y, 1 - c)

        def half(w, j):
            kh = slots[w].shape[1] // 2
            return g_refs[w].at[j, pl.ds(c * kh, kh)]

        def three(w):
            return g_refs[w].at[pl.ds(0, 3), pl.ds(0, slots[w].shape[1] // 2)]

        for w in range(n):
            for (px, py) in others:
                pltpu.make_async_remote_copy(
                    src_ref=half(w, jme), dst_ref=half(w, jme), send_sem=ici_send.at[w], recv_sem=ici_recv.at[w],
                    device_id=(px, py, c), device_id_type=MESH).start()
        for w in range(n):
            pltpu.make_async_remote_copy(src_ref=three(w), dst_ref=three(w), send_sem=ici_send.at[w],
                                         recv_sem=ici_recv.at[w], device_id=sib, device_id_type=MESH).wait_recv()
            for (px, py) in others:
                blk = half(w, 2 * px + py)
                pltpu.make_async_remote_copy(src_ref=blk, dst_ref=blk, send_sem=d2d_send.at[w], recv_sem=d2d_recv.at[w],
                                             device_id=sib, device_id_type=MESH).start()
        for w in range(n):
            wait3 = pltpu.make_async_remote_copy(src_ref=three(w), dst_ref=three(w), send_sem=d2d_send.at[w],
                                                 recv_sem=d2d_recv.at[w], device_id=sib, device_id_type=MESH)
            wait3.wait_recv()
            wait3.wait_send()
            pltpu.make_async_remote_copy(src_ref=three(w), dst_ref=three(w), send_sem=ici_send.at[w],
                                         recv_sem=ici_recv.at[w], device_id=sib, device_id_type=MESH).wait_send()

    return pl.pallas_call(
        body, name=name,
        in_specs=[HBM_SPEC] * n, out_specs=[HBM_SPEC] * n,
        out_shape=[jax.ShapeDtypeStruct(s.shape, s.dtype) for s in slots],
        input_output_aliases={w: w for w in range(n)},
        scratch_shapes=[pltpu.SemaphoreType.DMA((n,))] * 4,
    )(*slots)


def exchange_halves(name, grads):
    n = len(grads)

    def body(*refs):
        d_refs, r_refs = refs[:n], refs[n:2 * n]
        send, recv = refs[2 * n:]
        x, y, c, _, _ = _place()
        cps = []
        for w in range(n):
            half = grads[w].shape[1] // 2
            cps.append(pltpu.make_async_remote_copy(
                src_ref=d_refs[w].at[pl.ds(0, N_CHIPS), pl.ds((1 - c) * half, half)], dst_ref=r_refs[w],
                send_sem=send.at[w], recv_sem=recv.at[w], device_id=(x, y, 1 - c), device_id_type=MESH))
        for cp in cps:
            cp.start()
        for cp in cps:
            cp.wait()

    return pl.pallas_call(
        body, name=name, in_specs=[HBM_SPEC] * n, out_specs=[HBM_SPEC] * n,
        out_shape=[jax.ShapeDtypeStruct((N_CHIPS, g.shape[1] // 2, g.shape[2]), g.dtype) for g in grads],
        scratch_shapes=[pltpu.SemaphoreType.DMA((n,))] * 2,
    )(*grads)


def scatter_slabs(name, parts):
    n = len(parts)

    def body(*refs):
        p_refs, q_refs = refs[:n], refs[n:2 * n]
        send, recv = refs[2 * n:]
        x, y, c, jme, others = _place()
        for w in range(n):
            for (px, py) in others:
                pltpu.make_async_remote_copy(
                    src_ref=p_refs[w].at[2 * px + py], dst_ref=q_refs[w].at[jme], send_sem=send.at[w], recv_sem=recv.at[w],
                    device_id=(px, py, c), device_id_type=MESH).start()
        for w in range(n):
            three = q_refs[w].at[pl.ds(0, 3)]
            wait3 = pltpu.make_async_remote_copy(src_ref=three, dst_ref=three, send_sem=send.at[w], recv_sem=recv.at[w],
                                                 device_id=(x, y, c), device_id_type=MESH)
            wait3.wait_recv()
            wait3.wait_send()

    return pl.pallas_call(
        body, name=name, in_specs=[HBM_SPEC] * n, out_specs=[HBM_SPEC] * n,
        out_shape=[jax.ShapeDtypeStruct(p.shape, p.dtype) for p in parts],
        scratch_shapes=[pltpu.SemaphoreType.DMA((n,))] * 2,
    )(*parts)


def share_halves(fulls):
    n = len(fulls)

    def body(*refs):
        o_refs = refs[n:2 * n]
        send, recv = refs[2 * n:]
        x, y, c, _, _ = _place()
        cps = []
        for w in range(n):
            kh = fulls[w].shape[1] // 2
            for l in range(2):
                half = o_refs[w].at[l, pl.ds(c * kh, kh)]
                k = 2 * w + l
                cps.append(pltpu.make_async_remote_copy(src_ref=half, dst_ref=half, send_sem=send.at[k], recv_sem=recv.at[k],
                                                        device_id=(x, y, 1 - c), device_id_type=MESH))
        for cp in cps:
            cp.start()
        for cp in cps:
            cp.wait()

    return pl.pallas_call(
        body, name="share_halves", in_specs=[HBM_SPEC] * n, out_specs=[HBM_SPEC] * n,
        out_shape=[jax.ShapeDtypeStruct(f.shape, f.dtype) for f in fulls],
        input_output_aliases={w: w for w in range(n)},
        scratch_shapes=[pltpu.SemaphoreType.DMA((2 * n,))] * 2,
    )(*fulls)


def allreduce_small(v):
    R = v.shape[0]

    def body(v_ref, o_ref, sib_ref, mine_ref, all_ref, d_send, d_recv, i_send, i_recv):
        x, y, c, jme, others = _place()
        swap = pltpu.make_async_remote_copy(src_ref=v_ref, dst_ref=sib_ref, send_sem=d_send, recv_sem=d_recv,
                                            device_id=(x, y, 1 - c), device_id_type=MESH)
        swap.start()
        swap.wait()
        mine_ref[...] = v_ref[...] + sib_ref[...]
        for (px, py) in others:
            pltpu.make_async_remote_copy(src_ref=mine_ref, dst_ref=all_ref.at[jme], send_sem=i_send, recv_sem=i_recv,
                                         device_id=(px, py, c), device_id_type=MESH).start()
        three = all_ref.at[pl.ds(0, 3)]
        wait3 = pltpu.make_async_remote_copy(src_ref=three, dst_ref=three, send_sem=i_send, recv_sem=i_recv,
                                             device_id=(x, y, c), device_id_type=MESH)
        wait3.wait_recv()
        wait3.wait_send()
        all_ref[jme] = mine_ref[...]
        o_ref[...] = ((all_ref[0] + all_ref[1]) + all_ref[2]) + all_ref[3]

    vm = pl.BlockSpec(memory_space=pltpu.VMEM)
    return pl.pallas_call(
        body, name="allreduce_small", in_specs=[vm], out_specs=vm,
        out_shape=jax.ShapeDtypeStruct(v.shape, F32),
        scratch_shapes=[pltpu.VMEM((R, 128), F32), pltpu.VMEM((R, 128), F32), pltpu.VMEM((N_CHIPS, R, 128), F32),
                        pltpu.SemaphoreType.DMA, pltpu.SemaphoreType.DMA, pltpu.SemaphoreType.DMA, pltpu.SemaphoreType.DMA],
        compiler_params=pltpu.CompilerParams(vmem_limit_bytes=VMEM_LIMIT_BYTES),
    )(v)


def _row_tile(rows, width, mult=16, cap=3 << 20):
    best = rows
    for t in range(mult, rows + 1, mult):
        if rows % t == 0 and t * width * 4 <= cap:
            best = t
    return best


def add_sibling(name, mine, theirs, place):
    _, kh, ns = theirs.shape
    tr = _row_tile(kh, ns)
    nblk = kh // tr

    def body(place_ref, a_ref, b_ref, o_ref):
        o_ref[...] = (a_ref[...].astype(F32) + b_ref[...].astype(F32)).astype(BF16)

    return pl.pallas_call(
        body, name=name,
        grid_spec=pltpu.PrefetchScalarGridSpec(
            num_scalar_prefetch=1, grid=(N_CHIPS, nblk),
            in_specs=[pl.BlockSpec((None, tr, ns), lambda j, i, s: (j, s[1] * nblk + i, 0)),
                      pl.BlockSpec((None, tr, ns), lambda j, i, s: (j, i, 0))],
            out_specs=pl.BlockSpec((None, tr, ns), lambda j, i, s: (j, i, 0))),
        out_shape=jax.ShapeDtypeStruct(theirs.shape, BF16),
        compiler_params=_params(("parallel", "parallel")),
    )(place, mine, theirs)


def add_chips(name, q, p, place, layer, full=None):
    _, kh, ns = q.shape
    tr = _row_tile(kh, ns)
    nblk = kh // tr

    def body(place_ref, *refs):
        q_refs, own_ref, o_ref = refs[:N_CHIPS], refs[N_CHIPS], refs[-1]
        jme = place_ref[0]
        tot = None
        for j in range(N_CHIPS):
            v = jnp.where(jme == j, own_ref[...], q_refs[j][...]).astype(F32)
            tot = v if tot is None else tot + v
        o_ref[...] = tot

    def q_ix(j):
        return lambda i, s: (jnp.where(s[0] == j, (j + 1) % N_CHIPS, j), i, 0)

    in_specs = [pl.BlockSpec((None, tr, ns), q_ix(j)) for j in range(N_CHIPS)]
    in_specs.append(pl.BlockSpec((None, tr, ns), lambda i, s: (s[0], i, 0)))
    args = [place, q, q, q, q, p]
    aliases = {}
    if full is not None:
        in_specs.append(pl.BlockSpec(memory_space=pl.ANY))
        args.append(full)
        aliases = {len(args) - 1: 0}
    return pl.pallas_call(
        body, name=name,
        grid_spec=pltpu.PrefetchScalarGridSpec(
            num_scalar_prefetch=1, grid=(nblk,), in_specs=in_specs,
            out_specs=pl.BlockSpec((None, tr, ns), lambda i, s: (layer, s[1] * nblk + i, 0))),
        out_shape=jax.ShapeDtypeStruct((2, 2 * kh, ns), F32),
        input_output_aliases=aliases,
        compiler_params=_params(("parallel",)),
    )(*args)


ADAM_LR, ADAM_B1, ADAM_B2, ADAM_EPS, ADAM_WD, ADAM_STEP = 0.001, 0.9, 0.999, 1e-08, 0.01, 10


def adamw(name, w, g, m, v):
    _, k, ns = w.shape
    nsp = g.shape[2]
    tr = _row_tile(k, nsp, mult=8, cap=2 << 20)

    def body(w_ref, g_ref, m_ref, v_ref, go_ref, d_ref, mo_ref, vo_ref):
        gv = g_ref[:, :ns] if nsp != ns else g_ref[...]
        mn = ADAM_B1 * m_ref[...] + (1.0 - ADAM_B1) * gv
        vn = ADAM_B2 * v_ref[...] + (1.0 - ADAM_B2) * (gv * gv)
        m_hat = mn / (1.0 - ADAM_B1 ** ADAM_STEP)
        v_hat = vn / (1.0 - ADAM_B2 ** ADAM_STEP)
        go_ref[...] = gv
        d_ref[...] = -ADAM_LR * (m_hat / (jnp.sqrt(v_hat) + ADAM_EPS) + ADAM_WD * w_ref[...])
        mo_ref[...] = mn
        vo_ref[...] = vn

    blk = pl.BlockSpec((None, tr, ns), lambda l, i: (l, i, 0))
    gblk = pl.BlockSpec((None, tr, nsp), lambda l, i: (l, i, 0))
    return pl.pallas_call(
        body, name=name, grid=(2, k // tr), in_specs=[blk, gblk, blk, blk], out_specs=[blk] * 4,
        out_shape=[jax.ShapeDtypeStruct(w.shape, F32)] * 4, compiler_params=_params(("parallel", "parallel")),
    )(w, g, m, v)


WEIGHTS = ("ffn_a_norm", "ffn_a_w1", "ffn_a_w3", "ffn_a_w2", "mix_norm", "w_in", "q_a_norm", "w_uq", "kv_a_norm", "w_ukv",
           "q_norm", "k_norm", "gm_v_norm", "gm_ws", "gm_bs", "attn_out_norm", "gm_out_norm", "w_out", "ffn_b_norm",
           "ffn_b_w1", "ffn_b_w3", "ffn_b_w2", "ple_gate_norm", "w_ple_gate", "w_ple", "ple_norm")
_FF = FF_PAD - FF_SHARD
BIG = {"ffn_a_w1": _FF, "ffn_a_w3": _FF, "ffn_a_w2": _FF, "ffn_b_w1": _FF, "ffn_b_w3": _FF, "ffn_b_w2": _FF,
       "w_in": IN_SHARD_PAD - IN_SHARD, "w_uq": 0, "w_ukv": 0, "w_ple": 0, "w_out": 0, "w_ple_gate": 0}
TRANSPOSED = ("ffn_a_w1", "ffn_a_w3", "ffn_b_w1", "ffn_b_w3", "w_in")
SMALL = tuple(n for n in WEIGHTS if n not in BIG)
PACK = 1024


def _pack_small(d):
    parts = []
    for n in SMALL:
        flat = d[n].reshape(-1)
        parts.append(jnp.pad(flat, (0, (-flat.shape[0]) % PACK)))
    return jnp.concatenate(parts).reshape(-1, 128)


def _unpack_small(buf, like):
    flat = buf.reshape(-1)
    out, pos = {}, 0
    for n in SMALL:
        size = math.prod(like[n].shape)
        out[n] = flat[pos:pos + size].reshape(like[n].shape)
        pos += size + (-size) % PACK
    return out


def kernel(*args):
    names = (("x", "p", "positions") + WEIGHTS + ("loss_target",) + tuple("m_" + n for n in WEIGHTS)
             + tuple("v_" + n for n in WEIGHTS))
    a = dict(zip(names, args, strict=True))
    x, p, positions, target = a["x"][0], a["p"][:, 0], a["positions"][0], a["loss_target"][0]
    for n in TRANSPOSED:
        for pre in ("", "m_", "v_"):
            a[pre + n] = jnp.swapaxes(a[pre + n], 1, 2)

    place = jnp.stack([2 * lax.axis_index("x") + lax.axis_index("y"), lax.axis_index("c")]).astype(jnp.int32)
    small = {n: a[n] for n in SMALL}
    tabs = rope_tables(positions)
    G = [dict(zip(BIG, gather_weights(f"gather_l{l}", [prep_shard(f"prep_{n}_{l}", a[n], l, BIG[n], place) for n in BIG])))
         for l in range(2)]

    h, saved0 = layer_fwd(0, x, p[0], G[0], small, tabs)
    h, saved1 = layer_fwd(1, h, p[1], G[1], small, tabs)
    dh, loss = loss_grad(h, target)
    loss = lax.psum(loss, ("x", "y", "c"))

    def reduce_layer(l, gl, fulls):
        mine = [gl[n] for n in BIG]
        theirs = exchange_halves(f"exchange_l{l}", mine)
        parts = [add_sibling(f"add_sibling_{n}_{l}", d, r, place) for n, d, r in zip(BIG, mine, theirs)]
        slabs = scatter_slabs(f"scatter_l{l}", parts)
        return [add_chips(f"add_chips_{n}_{l}", q, pt, place, l, full=f) for n, q, pt, f in zip(BIG, slabs, parts, fulls)]

    grads = [None, None]
    dh, grads[1] = layer_bwd(1, dh, p[1], G[1], small, tabs, saved1)
    fulls = reduce_layer(1, grads[1], [None] * len(BIG))
    gx, grads[0] = layer_bwd(0, dh, p[0], G[0], small, tabs, saved0)
    fulls = reduce_layer(0, grads[0], fulls)
    full = dict(zip(BIG, share_halves(fulls)))

    out_g, out_d, out_m, out_v = {}, {}, {}, {}
    for n in BIG:
        outs = adamw(f"adamw_{n}", a[n], full[n], a["m_" + n], a["v_" + n])
        if n in TRANSPOSED:
            outs = [jnp.swapaxes(o, 1, 2) for o in outs]
        out_g[n], out_d[n], out_m[n], out_v[n] = outs

    gs = allreduce_small(_pack_small({n: jnp.stack([grads[0][n], grads[1][n]]) for n in SMALL}))
    rows = gs.shape[0] // 2
    sm = adamw("adamw_small", _pack_small(small).reshape(2, rows, 128), gs.reshape(2, rows, 128),
               _pack_small({n: a["m_" + n] for n in SMALL}).reshape(2, rows, 128),
               _pack_small({n: a["v_" + n] for n in SMALL}).reshape(2, rows, 128))
    for dst, buf in zip((out_g, out_d, out_m, out_v), sm):
        dst.update(_unpack_small(buf, small))

    return (loss, gx[None], *[out_g[n] for n in WEIGHTS], *[out_d[n] for n in WEIGHTS],
            *[out_m[n] for n in WEIGHTS], *[out_v[n] for n in WEIGHTS])
```

```python
import math

import jax
import jax.numpy as jnp
from jax import lax
from jax.experimental import pallas as pl
from jax.experimental.pallas import tpu as pltpu

F32 = jnp.float32
BF16 = jnp.bfloat16

D_MODEL = 2048
D_FF = 5504
N_CHIPS = 4
FF_SHARD = D_FF // N_CHIPS
FF_PAD = 1408
FF_P = N_CHIPS * FF_PAD
HEADS = 8
QK_NOPE = 128
QK_ROPE = 64
QK_DIM = 192
HEAD_PAD = 256
V_DIM = 128
Q_RANK = 512
KV_RANK = 256
ATTN_W = 1024
GM_W = 1024
GROUPS = 8
CHUNK = 128
PLE_DIM = 256
IN_P = 3072
IN_SHARD = 720
IN_SHARD_PAD = 736
EPS = 1e-6
ROPE_BASE = 10000.0
ATTN_SCALE = QK_DIM ** -0.5
VMEM_LIMIT_BYTES = 56 * 1024 * 1024


def _params(sem):
    return pltpu.CompilerParams(dimension_semantics=sem, vmem_limit_bytes=VMEM_LIMIT_BYTES)


def _bf(x):
    return x if x.dtype == BF16 else x.astype(BF16)


def _sigmoid(x):
    return 1.0 / (1.0 + jnp.exp(-x))


_GELU_C = math.sqrt(2.0 / math.pi)


def _gelu(x):
    t = jnp.tanh(_GELU_C * (x + 0.044715 * x * x * x))
    return 0.5 * x * (1.0 + t)


def _gelu_grad(x):
    t = jnp.tanh(_GELU_C * (x + 0.044715 * x * x * x))
    return 0.5 * (1.0 + t) + 0.5 * x * (1.0 - t * t) * _GELU_C * (1.0 + 3 * 0.044715 * x * x)


def op_a(a, tm, tk):
    return (a, (tm, tk), lambda i, j, k: (i, k), 1)


def op_at(a, tm, tk):
    return (a, (tk, tm), lambda i, j, k: (k, i), 0)


def op_b(b, tk, tn):
    return (b, (tk, tn), lambda i, j, k: (k, j), 0)


def op_bt(b, tk, tn):
    return (b, (tn, tk), lambda i, j, k: (j, k), 1)


def op_b_cols(g, pre, tk, tn):
    nb = g.shape[-1] // tn
    none = (None,) * (1 + len(pre))
    return (g, none + (tk, tn), lambda i, j, k: (j // nb,) + tuple(pre) + (k, j % nb), 0)


def op_b_rows(g, pre, tk, tn, koff=0):
    nb = g.shape[-2] // tk
    none = (None,) * (1 + len(pre))
    return (g, none + (tk, tn), lambda i, j, k: ((k + koff) // nb,) + tuple(pre) + ((k + koff) % nb, j), 0)


def op_b_rows_t(g, pre, tk, tn):
    nb = g.shape[-2] // tn
    none = (None,) * (1 + len(pre))
    return (g, none + (tn, tk), lambda i, j, k: (j // nb,) + tuple(pre) + (j % nb, k), 1)


def tile_mn(x, tm, tn):
    return (x, (tm, tn), lambda i, j: (i, j))


def out_mn(M, N, tm, tn, dtype):
    return (jax.ShapeDtypeStruct((M, N), dtype), (tm, tn), lambda i, j: (i, j))


def out_cols(M, ns, tm, tn, dtype):
    nb = ns // tn
    return (jax.ShapeDtypeStruct((N_CHIPS, M, ns), dtype), (None, tm, tn), lambda i, j: (j // nb, i, j % nb))


def matmul(name, grid_mnk, a_ops, b_ops, terms, n_acc, extras, outs, epilogue, acc_tile, n_outer=False):
    gm, gn, gk = grid_mnk
    na, nb, nx, no = len(a_ops), len(b_ops), len(extras), len(outs)

    def body(*refs):
        a_refs, b_refs = refs[:na], refs[na:na + nb]
        x_refs = refs[na + nb:na + nb + nx]
        o_refs = refs[na + nb + nx:na + nb + nx + no]
        acc_refs = refs[na + nb + nx + no:]
        k = pl.program_id(2)

        @pl.when(k == 0)
        def _():
            for acc in acc_refs:
                acc[...] = jnp.zeros_like(acc)

        for ai, bi, ci in terms:
            dims = (((a_ops[ai][3],), (b_ops[bi][3],)), ((), ()))
            acc_refs[ci][...] += lax.dot_general(_bf(a_refs[ai][...]), _bf(b_refs[bi][...]), dims,
                                                 preferred_element_type=F32)

        @pl.when(k == gk - 1)
        def _():
            res = epilogue([acc[...] for acc in acc_refs], [x[...] for x in x_refs])
            for o, v in zip(o_refs, res):
                o[...] = v.astype(o.dtype)

    if n_outer:
        grid = (gn, gm, gk)

        def ix3(f):
            return lambda j, i, k: f(i, j, k)

        def ix2(f):
            return lambda j, i, k: f(i, j)
    else:
        grid = (gm, gn, gk)

        def ix3(f):
            return lambda i, j, k: f(i, j, k)

        def ix2(f):
            return lambda i, j, k: f(i, j)

    in_specs = [pl.BlockSpec(blk, ix3(f)) for (_, blk, f, _) in list(a_ops) + list(b_ops)]
    in_specs += [pl.BlockSpec(blk, ix2(f)) for (_, blk, f) in extras]
    out_specs = [pl.BlockSpec(blk, ix2(f)) for (_, blk, f) in outs]
    return pl.pallas_call(
        body,
        name=name,
        grid=grid,
        in_specs=in_specs,
        out_specs=out_specs,
        out_shape=[s for (s, _, _) in outs],
        scratch_shapes=[pltpu.VMEM(acc_tile, F32) for _ in range(n_acc)],
        compiler_params=_params(("parallel", "parallel", "arbitrary")),
    )(*[o[0] for o in a_ops], *[o[0] for o in b_ops], *[x[0] for x in extras])


def _acc0(accs, xs):
    return (accs[0],)


def mm_simple(name, a, b_op_fn, M, N, K, tm, tn, tk, out_dtype=F32, a_t=False, extras=(), epilogue=_acc0, outs=None):
    a_op = op_at(a, tm, tk) if a_t else op_a(a, tm, tk)
    outs = outs or [out_mn(M, N, tm, tn, out_dtype)]
    return matmul(name, (M // tm, N // tn, K // tk), [a_op], [b_op_fn(tk, tn)], [(0, 0, 0)], 1,
                  list(extras), outs, epilogue, (tm, tn))


def rms_fwd(name, x, g, width, col_blk=0, tm=256, out_dtype=BF16):
    T = x.shape[0]

    def body(x_ref, g_ref, o_ref):
        xv = x_ref[...].astype(F32)
        r = lax.rsqrt(jnp.mean(xv * xv, axis=-1, keepdims=True) + EPS)
        o_ref[...] = (xv * r * g_ref[...]).astype(o_ref.dtype)

    return pl.pallas_call(
        body, name=name, grid=(T // tm,),
        in_specs=[pl.BlockSpec((tm, width), lambda i: (i, col_blk)), pl.BlockSpec((1, width), lambda i: (0, 0))],
        out_specs=pl.BlockSpec((tm, width), lambda i: (i, 0)),
        out_shape=jax.ShapeDtypeStruct((T, width), out_dtype),
        compiler_params=_params(("parallel",)),
    )(x, g.reshape(1, width))


def rms_bwd(name, x, g, dn, width, col_blk=0, dres=None, tm=256, with_delta=False):
    T = x.shape[0]
    has_res = dres is not None

    def body(*refs):
        x_ref, g_ref, dn_ref = refs[:3]
        pos = 3
        res_ref = None
        if has_res:
            res_ref = refs[pos]
            pos += 1
        dx_ref, dg_ref = refs[pos], refs[pos + 1]
        delta_ref = refs[pos + 2] if with_delta else None
        i = pl.program_id(0)
        xv = x_ref[...].astype(F32)
        r = lax.rsqrt(jnp.mean(xv * xv, axis=-1, keepdims=True) + EPS)
        xh = xv * r
        d = dn_ref[...].astype(F32)
        gd = d * g_ref[...]
        dx = r * (gd - xh * jnp.mean(gd * xh, axis=-1, keepdims=True))
        if has_res:
            dx = dx + res_ref[...]
        dx_ref[...] = dx.astype(dx_ref.dtype)
        part = jnp.sum(d * xh, axis=0, keepdims=True)

        @pl.when(i == 0)
        def _():
            dg_ref[...] = part

        @pl.when(i > 0)
        def _():
            dg_ref[...] += part

        if with_delta:
            for h in range(width // 128):
                sl = slice(h * 128, (h + 1) * 128)
                s = jnp.sum(dx[:, sl] * xv[:, sl], axis=-1, keepdims=True)
                delta_ref[:, sl] = jnp.broadcast_to(s, (tm, 128))

    in_specs = [pl.BlockSpec((tm, width), lambda i: (i, col_blk)), pl.BlockSpec((1, width), lambda i: (0, 0)),
                pl.BlockSpec((tm, width), lambda i: (i, 0))]
    args = [x, g.reshape(1, width), dn]
    if has_res:
        in_specs.append(pl.BlockSpec((tm, width), lambda i: (i, 0)))
        args.append(dres)
    out_specs = [pl.BlockSpec((tm, width), lambda i: (i, 0)), pl.BlockSpec((1, width), lambda i: (0, 0))]
    out_shape = [jax.ShapeDtypeStruct((T, width), F32), jax.ShapeDtypeStruct((1, width), F32)]
    if with_delta:
        out_specs.append(pl.BlockSpec((tm, width), lambda i: (i, 0)))
        out_shape.append(jax.ShapeDtypeStruct((T, width), F32))
    return pl.pallas_call(
        body, name=name, grid=(T // tm,), in_specs=in_specs, out_specs=out_specs, out_shape=out_shape,
        compiler_params=_params(("arbitrary",)),
    )(*args)


def ffn_fwd(tag, h, g, w1g, w3g, w2g, pre):
    T = h.shape[0]
    n = rms_fwd(f"{tag}_rms", h, g, D_MODEL)
    tm, tn = 512, FF_PAD

    def up_epi(accs, xs):
        a1, a3 = accs
        return a1, a3, a1 * _sigmoid(a1) * a3

    a1, a3, s = matmul(
        f"{tag}_up", (T // tm, FF_P // tn, 1),
        [op_a(n, tm, D_MODEL)], [op_b_rows_t(w1g, pre, D_MODEL, tn), op_b_rows_t(w3g, pre, D_MODEL, tn)],
        [(0, 0, 0), (0, 1, 1)], 2, [],
        [out_mn(T, FF_P, tm, tn, BF16)] * 3, up_epi, (tm, tn), n_outer=True)

    tn2 = 1024
    (h_out,) = matmul(
        f"{tag}_down", (T // tm, D_MODEL // tn2, N_CHIPS),
        [op_a(s, tm, FF_PAD)], [op_b_rows(w2g, pre, FF_PAD, tn2)],
        [(0, 0, 0)], 1, [tile_mn(h, tm, tn2)],
        [out_mn(T, D_MODEL, tm, tn2, F32)], lambda accs, xs: (xs[0] + 0.5 * accs[0],), (tm, tn2))
    return h_out, (n, a1, a3, s)


def ffn_bwd(tag, dh_out, h, g, res, w1g, w3g, w2g, pre):
    n, a1, a3, s = res
    T = h.shape[0]
    tm, tn = 512, FF_PAD

    def act_epi(accs, xs):
        ds = 0.5 * accs[0]
        x1, x3 = xs[0].astype(F32), xs[1].astype(F32)
        sg = _sigmoid(x1)
        silu = x1 * sg
        return ds * x3 * (sg + silu * (1.0 - sg)), ds * silu

    da1, da3 = matmul(
        f"{tag}_dact", (T // tm, FF_P // tn, 1),
        [op_a(dh_out, tm, D_MODEL)], [op_b_rows_t(w2g, pre, D_MODEL, tn)],
        [(0, 0, 0)], 1, [tile_mn(a1, tm, tn), tile_mn(a3, tm, tn)],
        [out_mn(T, FF_P, tm, tn, BF16)] * 2, act_epi, (tm, tn), n_outer=True)

    tk = 512

    def dw_t(nm, left, right, scale):
        (dw,) = matmul(
            f"{tag}_{nm}", (FF_P // FF_PAD, D_MODEL // 1024, T // tk),
            [op_at(left, FF_PAD, tk)], [op_b(right, tk, 1024)],
            [(0, 0, 0)], 1, [], [out_mn(FF_P, D_MODEL, FF_PAD, 1024, BF16)],
            lambda accs, xs: (scale * accs[0],), (FF_PAD, 1024))
        return dw

    dw2 = dw_t("dw2", s, dh_out, 0.5)
    dw1 = dw_t("dw1", da1, n, 1.0)
    dw3 = dw_t("dw3", da3, n, 1.0)

    tn2 = 1024
    (dn,) = matmul(
        f"{tag}_dn", (T // tm, D_MODEL // tn2, N_CHIPS),
        [op_a(da1, tm, FF_PAD), op_a(da3, tm, FF_PAD)],
        [op_b_rows(w1g, pre, FF_PAD, tn2), op_b_rows(w3g, pre, FF_PAD, tn2)],
        [(0, 0, 0), (1, 1, 0)], 1, [], [out_mn(T, D_MODEL, tm, tn2, F32)], _acc0, (tm, tn2))
    dh, dg = rms_bwd(f"{tag}_rms_bwd", h, g, dn, D_MODEL, dres=dh_out)
    return dh, dg, dw1, dw3, dw2


def rope_tables(positions):
    inv_freq = ROPE_BASE ** (-jnp.arange(0, QK_ROPE, 2, dtype=F32) / QK_ROPE)
    ang = positions.astype(F32)[:, None] * inv_freq
    cos, sin = jnp.cos(ang), jnp.sin(ang)
    T = positions.shape[0]
    one, zero = jnp.ones((T, QK_NOPE), F32), jnp.zeros((T, 64), F32)
    z32, z128 = jnp.zeros((T, 32), F32), jnp.zeros((T, QK_NOPE), F32)
    c = jnp.concatenate([one, cos, cos, zero], axis=1)
    s1 = jnp.concatenate([z128, -sin, z32, zero], axis=1)
    s2 = jnp.concatenate([z128, z32, sin, zero], axis=1)
    return c, s1, s2


def _rope(y, c, s1, s2):
    return y * c + pltpu.roll(y, HEAD_PAD - 32, 1) * s1 + pltpu.roll(y, 32, 1) * s2


def _rope_t(d, c, s1, s2):
    return d * c + pltpu.roll(d * s1, 32, 1) + pltpu.roll(d * s2, HEAD_PAD - 32, 1)


def _head_norm(x):
    r = lax.rsqrt(jnp.sum(x * x, axis=-1, keepdims=True) * (1.0 / QK_DIM) + EPS)
    return x * r, r


def qk_prep_fwd(tag, q_raw, kk_raw, z_p, gq, gk, tabs, tm=256):
    T = q_raw.shape[0]
    c, s1, s2 = tabs

    def body(q_ref, k_ref, kr_ref, gq_ref, gk_ref, c_ref, s1_ref, s2_ref, qo_ref, ko_ref):
        cv, s1v, s2v = c_ref[...], s1_ref[...], s2_ref[...]
        kr = kr_ref[...]
        for h in range(HEADS):
            sl = slice(h * HEAD_PAD, (h + 1) * HEAD_PAD)
            xh, _ = _head_norm(q_ref[:, sl])
            qo_ref[:, sl] = (_rope(xh * gq_ref[...], cv, s1v, s2v) * ATTN_SCALE).astype(BF16)
            xh, _ = _head_norm(k_ref[:, sl] + kr)
            ko_ref[:, sl] = _rope(xh * gk_ref[...], cv, s1v, s2v).astype(BF16)

    row = lambda i: (i, 0)
    full = pl.BlockSpec((tm, HEADS * HEAD_PAD), row)
    tab = pl.BlockSpec((tm, HEAD_PAD), row)
    vec = pl.BlockSpec((1, HEAD_PAD), lambda i: (0, 0))
    return pl.pallas_call(
        body, name=f"{tag}_qk_prep", grid=(T // tm,),
        in_specs=[full, full, pl.BlockSpec((tm, HEAD_PAD), lambda i: (i, 3)), vec, vec, tab, tab, tab],
        out_specs=[full, full],
        out_shape=[jax.ShapeDtypeStruct((T, HEADS * HEAD_PAD), BF16)] * 2,
        compiler_params=_params(("parallel",)),
    )(q_raw, kk_raw, z_p, gq, gk, c, s1, s2)


def qk_prep_bwd(tag, dq_full, dk_full, q_raw, kk_raw, z_p, gq, gk, tabs, tm=256):
    T = q_raw.shape[0]
    c, s1, s2 = tabs

    def body(dq_ref, dk_ref, q_ref, k_ref, kr_ref, gq_ref, gk_ref, c_ref, s1_ref, s2_ref,
             dqr_ref, dkr_ref, dz_ref, dgq_ref, dgk_ref):
        i = pl.program_id(0)
        cv, s1v, s2v = c_ref[...], s1_ref[...], s2_ref[...]
        kr = kr_ref[...]
        lane = lax.broadcasted_iota(jnp.int32, (tm, HEAD_PAD), 1)
        slot = ((lane >= QK_NOPE) & (lane < QK_DIM)).astype(F32)

        def one(x, g, d):
            xh, r = _head_norm(x)
            dy = _rope_t(d, cv, s1v, s2v)
            gd = dy * g
            dx = r * (gd - xh * (jnp.sum(gd * xh, axis=-1, keepdims=True) * (1.0 / QK_DIM)))
            return dx, jnp.sum(dy * xh, axis=0, keepdims=True)

        dgq = jnp.zeros((1, HEAD_PAD), F32)
        dgk = jnp.zeros((1, HEAD_PAD), F32)
        dz = jnp.zeros((tm, HEAD_PAD), F32)
        for h in range(HEADS):
            sl = slice(h * HEAD_PAD, (h + 1) * HEAD_PAD)
            dx, dg = one(q_ref[:, sl], gq_ref[...], dq_ref[:, sl].astype(F32) * ATTN_SCALE)
            dqr_ref[:, sl] = dx
            dgq = dgq + dg
            dx, dg = one(k_ref[:, sl] + kr, gk_ref[...], dk_ref[:, sl].astype(F32))
            dkr_ref[:, sl] = dx
            dgk = dgk + dg
            dz = dz + dx
        dz_ref[...] = dz * slot

        @pl.when(i == 0)
        def _():
            dgq_ref[...] = dgq
            dgk_ref[...] = dgk

        @pl.when(i > 0)
        def _():
            dgq_ref[...] += dgq
            dgk_ref[...] += dgk

    row = lambda i: (i, 0)
    full = pl.BlockSpec((tm, HEADS * HEAD_PAD), row)
    tab = pl.BlockSpec((tm, HEAD_PAD), row)
    vec = pl.BlockSpec((1, HEAD_PAD), lambda i: (0, 0))
    return pl.pallas_call(
        body, name=f"{tag}_qk_prep_bwd", grid=(T // tm,),
        in_specs=[full, full, full, full, pl.BlockSpec((tm, HEAD_PAD), lambda i: (i, 3)), vec, vec, tab, tab, tab],
        out_specs=[full, full, tab, vec, vec],
        out_shape=[jax.ShapeDtypeStruct((T, HEADS * HEAD_PAD), F32)] * 2
        + [jax.ShapeDtypeStruct((T, HEAD_PAD), F32)] + [jax.ShapeDtypeStruct((1, HEAD_PAD), F32)] * 2,
        compiler_params=_params(("arbitrary",)),
    )(dq_full, dk_full, q_raw, kk_raw, z_p, gq, gk, c, s1, s2)


def attn_fwd(tag, q_full, k_full, vv, blk=512):
    T = q_full.shape[0]
    nb = T // blk
    neg = float(jnp.finfo(jnp.float32).min)

    def body(q_ref, k_ref, v_ref, o_ref, lse_ref, m_ref, l_ref, acc_ref):
        i, j = pl.program_id(1), pl.program_id(2)

        @pl.when(j == 0)
        def _():
            m_ref[...] = jnp.full_like(m_ref, neg)
            l_ref[...] = jnp.zeros_like(l_ref)
            acc_ref[...] = jnp.zeros_like(acc_ref)

        def step(masked):
            s = lax.dot_general(q_ref[...], k_ref[...], (((1,), (1,)), ((), ())), preferred_element_type=F32)
            if masked:
                row = lax.broadcasted_iota(jnp.int32, (blk, blk), 0)
                col = lax.broadcasted_iota(jnp.int32, (blk, blk), 1)
                s = jnp.where(col <= row, s, neg)
            m_prev = m_ref[...]
            m_new = jnp.maximum(m_prev, jnp.max(s, axis=-1, keepdims=True))
            alpha = jnp.exp(m_prev - m_new)
            p = jnp.exp(s - m_new[:, :1])
            l_ref[...] = alpha * l_ref[...] + jnp.sum(p, axis=-1, keepdims=True)
            acc_ref[...] = alpha * acc_ref[...] + jnp.dot(p.astype(BF16), v_ref[...], preferred_element_type=F32)
            m_ref[...] = m_new

        @pl.when(j < i)
        def _():
            step(False)

        @pl.when(j == i)
        def _():
            step(True)
            o_ref[...] = acc_ref[...] / l_ref[...]
            lse_ref[...] = m_ref[...] + jnp.log(l_ref[...])

    kv_ix = lambda h, i, j: (jnp.minimum(j, i), h)
    return pl.pallas_call(
        body, name=f"{tag}_attn_fwd", grid=(HEADS, nb, nb),
        in_specs=[pl.BlockSpec((blk, HEAD_PAD), lambda h, i, j: (i, h)),
                  pl.BlockSpec((blk, HEAD_PAD), kv_ix), pl.BlockSpec((blk, V_DIM), kv_ix)],
        out_specs=[pl.BlockSpec((blk, V_DIM), lambda h, i, j: (i, h))] * 2,
        out_shape=[jax.ShapeDtypeStruct((T, ATTN_W), F32)] * 2,
        scratch_shapes=[pltpu.VMEM((blk, V_DIM), F32)] * 3,
        compiler_params=_params(("parallel", "parallel", "arbitrary")),
    )(q_full, k_full, vv)


def attn_bwd(tag, q_full, k_full, vv, do, lse, delta, blk=512):
    T = q_full.shape[0]
    nb = T // blk
    neg = float(jnp.finfo(jnp.float32).min)

    def body(q_ref, k_ref, v_ref, do_ref, lse_ref, dl_ref, dq_ref, dk_ref, dv_ref, dk_acc, dv_acc):
        j, i = pl.program_id(1), pl.program_id(2)

        @pl.when((j == 0) & (i == 0))
        def _():
            dq_ref[...] = jnp.zeros_like(dq_ref)

        @pl.when(i == 0)
        def _():
            dk_acc[...] = jnp.zeros_like(dk_acc)
            dv_acc[...] = jnp.zeros_like(dv_acc)

        def step(masked):
            q, k = q_ref[...], k_ref[...]
            s = lax.dot_general(q, k, (((1,), (1,)), ((), ())), preferred_element_type=F32)
            if masked:
                row = lax.broadcasted_iota(jnp.int32, (blk, blk), 0)
                col = lax.broadcasted_iota(jnp.int32, (blk, blk), 1)
                s = jnp.where(col <= row, s, neg)
            p = jnp.exp(s - lse_ref[:, :1])
            dob = _bf(do_ref[...])
            dv_acc[...] += lax.dot_general(p.astype(BF16), dob, (((0,), (0,)), ((), ())), preferred_element_type=F32)
            dp = lax.dot_general(dob, v_ref[...], (((1,), (1,)), ((), ())), preferred_element_type=F32)
            ds = (p * (dp - dl_ref[:, :1])).astype(BF16)
            dk_acc[...] += lax.dot_general(ds, q, (((0,), (0,)), ((), ())), preferred_element_type=F32)
            rows = pl.ds(pl.multiple_of(i * blk, blk), blk)
            dq_ref[rows, :] += jnp.dot(ds, k, preferred_element_type=F32)

        @pl.when(i > j)
        def _():
            step(False)

        @pl.when(i == j)
        def _():
            step(True)

        @pl.when(i == nb - 1)
        def _():
            dk_ref[...] = dk_acc[...]
            dv_ref[...] = dv_acc[...]

    q_ix = lambda h, j, i: (jnp.maximum(i, j), h)
    kv_ix = lambda h, j, i: (j, h)
    return pl.pallas_call(
        body, name=f"{tag}_attn_bwd", grid=(HEADS, nb, nb),
        in_specs=[pl.BlockSpec((blk, HEAD_PAD), q_ix), pl.BlockSpec((blk, HEAD_PAD), kv_ix),
                  pl.BlockSpec((blk, V_DIM), kv_ix), pl.BlockSpec((blk, V_DIM), q_ix),
                  pl.BlockSpec((blk, V_DIM), q_ix), pl.BlockSpec((blk, V_DIM), q_ix)],
        out_specs=[pl.BlockSpec((T, HEAD_PAD), lambda h, j, i: (0, h)),
                   pl.BlockSpec((blk, HEAD_PAD), kv_ix), pl.BlockSpec((blk, V_DIM), kv_ix)],
        out_shape=[jax.ShapeDtypeStruct((T, HEADS * HEAD_PAD), F32)] * 2 + [jax.ShapeDtypeStruct((T, ATTN_W), F32)],
        scratch_shapes=[pltpu.VMEM((blk, HEAD_PAD), F32), pltpu.VMEM((blk, V_DIM), F32)],
        compiler_params=_params(("parallel", "arbitrary", "arbitrary")),
    )(q_full, k_full, vv, do, lse, delta)


def _gm_forward(u, v, gv, wc_ref, bb_ref, nchunk):
    ug = _gelu(u)
    vg = _gelu(v)
    rv = lax.rsqrt(jnp.mean(vg * vg, axis=-1, keepdims=True) + EPS)
    vhat = vg * rv
    vn = (vhat * gv).astype(BF16)
    gates = []
    for cidx in range(nchunk):
        rows = slice(cidx * CHUNK, (cidx + 1) * CHUNK)
        gates.append(jnp.concatenate(
            [jnp.dot(wc_ref[gidx], vn[rows, gidx * 128:(gidx + 1) * 128], preferred_element_type=F32) + bb_ref[gidx]
             for gidx in range(GROUPS)], axis=1))
    gate = jnp.concatenate(gates, axis=0)
    return ug, vhat, rv, vn, gate


def gmlp_fwd(tag, z_p, gv, gout, wc, bb, tm=256):
    T = z_p.shape[0]
    nchunk = tm // CHUNK

    def body(u_ref, v_ref, gv_ref, go_ref, wc_ref, bb_ref, o_ref):
        ug, _, _, _, gate = _gm_forward(u_ref[...], v_ref[...], gv_ref[...], wc_ref, bb_ref, nchunk)
        go = ug * gate
        ro = lax.rsqrt(jnp.mean(go * go, axis=-1, keepdims=True) + EPS)
        o_ref[...] = (go * ro * go_ref[...]).astype(BF16)

    vec = pl.BlockSpec((1, GM_W), lambda i: (0, 0))
    w3 = pl.BlockSpec((GROUPS, CHUNK, CHUNK), lambda i: (0, 0, 0))
    return pl.pallas_call(
        body, name=f"{tag}_gmlp_fwd", grid=(T // tm,),
        in_specs=[pl.BlockSpec((tm, GM_W), lambda i: (i, 1)), pl.BlockSpec((tm, GM_W), lambda i: (i, 2)), vec, vec, w3, w3],
        out_specs=pl.BlockSpec((tm, GM_W), lambda i: (i, 0)),
        out_shape=jax.ShapeDtypeStruct((T, GM_W), BF16),
        compiler_params=_params(("parallel",)),
    )(z_p, z_p, gv.reshape(1, GM_W), gout.reshape(1, GM_W), wc, bb)


def gmlp_bwd(tag, z_p, dmixed, gv, gout, wc, bb, tm=256):
    T = z_p.shape[0]
    nchunk = tm // CHUNK

    def body(u_ref, v_ref, dm_ref, gv_ref, go_ref, wc_ref, bb_ref, du_ref, dv_ref, dwc_ref, dbb_ref, dgv_ref, dgo_ref):
        i = pl.program_id(0)
        u, v = u_ref[...], v_ref[...]
        ug, vhat, rv, vn, gate = _gm_forward(u, v, gv_ref[...], wc_ref, bb_ref, nchunk)
        go = ug * gate
        ro = lax.rsqrt(jnp.mean(go * go, axis=-1, keepdims=True) + EPS)
        ohat = go * ro
        dm = dm_ref[...].astype(F32)
        dgo_part = jnp.sum(dm * ohat, axis=0, keepdims=True)
        doh = dm * go_ref[...]
        dgo = ro * (doh - ohat * jnp.mean(doh * ohat, axis=-1, keepdims=True))
        du_ref[...] = dgo * gate * _gelu_grad(u)
        dgate = dgo * ug
        dgb = dgate.astype(BF16)
        dvn_rows = []
        dwc_parts = []
        dbb_parts = []
        for gidx in range(GROUPS):
            cols = slice(gidx * 128, (gidx + 1) * 128)
            dw = jnp.zeros((CHUNK, CHUNK), F32)
            db = jnp.zeros((CHUNK, 128), F32)
            for cidx in range(nchunk):
                rows = slice(cidx * CHUNK, (cidx + 1) * CHUNK)
                dw = dw + lax.dot_general(dgb[rows, cols], vn[rows, cols], (((1,), (1,)), ((), ())),
                                          preferred_element_type=F32)
                db = db + dgate[rows, cols]
            dwc_parts.append(dw)
            dbb_parts.append(db)
        for cidx in range(nchunk):
            rows = slice(cidx * CHUNK, (cidx + 1) * CHUNK)
            dvn_rows.append(jnp.concatenate(
                [lax.dot_general(wc_ref[gidx], dgb[rows, gidx * 128:(gidx + 1) * 128], (((0,), (0,)), ((), ())),
                                 preferred_element_type=F32) for gidx in range(GROUPS)], axis=1))
        dvn = jnp.concatenate(dvn_rows, axis=0)
        dgv_part = jnp.sum(dvn * vhat, axis=0, keepdims=True)
        dvh = dvn * gv_ref[...]
        dvg = rv * (dvh - vhat * jnp.mean(dvh * vhat, axis=-1, keepdims=True))
        dv_ref[...] = dvg * _gelu_grad(v)

        @pl.when(i == 0)
        def _():
            for gidx in range(GROUPS):
                dwc_ref[gidx] = dwc_parts[gidx]
                dbb_ref[gidx] = dbb_parts[gidx]
            dgv_ref[...] = dgv_part
            dgo_ref[...] = dgo_part

        @pl.when(i > 0)
        def _():
            for gidx in range(GROUPS):
                dwc_ref[gidx] += dwc_parts[gidx]
                dbb_ref[gidx] += dbb_parts[gidx]
            dgv_ref[...] += dgv_part
            dgo_ref[...] += dgo_part

    vec = pl.BlockSpec((1, GM_W), lambda i: (0, 0))
    w3 = pl.BlockSpec((GROUPS, CHUNK, CHUNK), lambda i: (0, 0, 0))
    blk = pl.BlockSpec((tm, GM_W), lambda i: (i, 0))
    return pl.pallas_call(
        body, name=f"{tag}_gmlp_bwd", grid=(T // tm,),
        in_specs=[pl.BlockSpec((tm, GM_W), lambda i: (i, 1)), pl.BlockSpec((tm, GM_W), lambda i: (i, 2)),
                  pl.BlockSpec((tm, GM_W), lambda i: (i, 1)), vec, vec, w3, w3],
        out_specs=[blk, blk, w3, w3, vec, vec],
        out_shape=[jax.ShapeDtypeStruct((T, GM_W), F32)] * 2 + [jax.ShapeDtypeStruct((GROUPS, CHUNK, CHUNK), F32)] * 2
        + [jax.ShapeDtypeStruct((1, GM_W), F32)] * 2,
        compiler_params=_params(("arbitrary",)),
    )(z_p, z_p, dmixed, gv.reshape(1, GM_W), gout.reshape(1, GM_W), wc, bb)


def mixer_fwd(tag, h, w, tabs, wout_g, pre):
    T = h.shape[0]
    n2 = rms_fwd(f"{tag}_mix_rms", h, w["mix_norm"], D_MODEL)
    (z_p,) = mm_simple(f"{tag}_win", n2, lambda tk, tn: op_bt(w["w_in_pt"], tk, tn), T, IN_P, D_MODEL, 512, 1024, D_MODEL)
    cqn = rms_fwd(f"{tag}_cq_rms", z_p, w["q_a_norm"], Q_RANK, col_blk=0)
    ckvn = rms_fwd(f"{tag}_ckv_rms", z_p, w["kv_a_norm"], KV_RANK, col_blk=2)
    (q_raw,) = mm_simple(f"{tag}_wq", cqn, lambda tk, tn: op_b(w["wq_p"], tk, tn), T, 2048, Q_RANK, 512, 1024, Q_RANK)
    (kk_raw,) = mm_simple(f"{tag}_wk", ckvn, lambda tk, tn: op_b(w["wk_p"], tk, tn), T, 2048, KV_RANK, 512, 1024, KV_RANK)
    (vv,) = mm_simple(f"{tag}_wv", ckvn, lambda tk, tn: op_b(w["wv"], tk, tn), T, ATTN_W, KV_RANK, 512, 1024, KV_RANK,
                      out_dtype=BF16)
    q_full, k_full = qk_prep_fwd(tag, q_raw, kk_raw, z_p, w["gq_p"], w["gk_p"], tabs)
    a_out, lse = attn_fwd(tag, q_full, k_full, vv)
    mixed_a = rms_fwd(f"{tag}_ao_rms", a_out, w["attn_out_norm"], ATTN_W)
    mixed_g = gmlp_fwd(tag, z_p, w["gm_v_norm"], w["gm_out_norm"], w["wc"], w["bb"])
    tm, tn, tk = 512, 1024, 512
    (h2,) = matmul(
        f"{tag}_wout", (T // tm, D_MODEL // tn, ATTN_W // tk),
        [op_a(mixed_a, tm, tk), op_a(mixed_g, tm, tk)],
        [op_b_rows(wout_g, pre, tk, tn), op_b_rows(wout_g, pre, tk, tn, koff=ATTN_W // tk)],
        [(0, 0, 0), (1, 1, 0)], 1, [tile_mn(h, tm, tn)], [out_mn(T, D_MODEL, tm, tn, F32)],
        lambda accs, xs: (xs[0] + accs[0],), (tm, tn))
    res = dict(n2=n2, z_p=z_p, cqn=cqn, ckvn=ckvn, q_raw=q_raw, kk_raw=kk_raw, vv=vv, q_full=q_full, k_full=k_full,
               a_out=a_out, lse=lse, mixed_a=mixed_a, mixed_g=mixed_g)
    return h2, res


def mixer_bwd(tag, dh2, h, w, tabs, wout_g, pre, r):
    T = h.shape[0]
    g = {}
    (dmixed,) = mm_simple(f"{tag}_dmixed", dh2, lambda tk, tn: op_b_rows_t(wout_g, pre, tk, tn), T, D_MODEL, D_MODEL,
                          512, 512, D_MODEL)
    (dwo_a,) = mm_simple(f"{tag}_dwout_a", r["mixed_a"], lambda tk, tn: op_b(dh2, tk, tn), ATTN_W, D_MODEL, T,
                         1024, 1024, 512, a_t=True, out_dtype=BF16)
    (dwo_g,) = mm_simple(f"{tag}_dwout_g", r["mixed_g"], lambda tk, tn: op_b(dh2, tk, tn), GM_W, D_MODEL, T,
                         1024, 1024, 512, a_t=True, out_dtype=BF16)
    g["w_out"] = jnp.concatenate([dwo_a, dwo_g], axis=0)
    da_out, g["attn_out_norm"], delta = rms_bwd(f"{tag}_ao_rms_bwd", r["a_out"], w["attn_out_norm"], dmixed, ATTN_W,
                                                with_delta=True)
    dq_full, dk_full, dvv = attn_bwd(tag, r["q_full"], r["k_full"], r["vv"], da_out, r["lse"], delta)
    dq_raw, dkk_raw, dzkr, g["gq_p"], g["gk_p"] = qk_prep_bwd(tag, dq_full, dk_full, r["q_raw"], r["kk_raw"], r["z_p"],
                                                            w["gq_p"], w["gk_p"], tabs)
    (g["wq_p"],) = mm_simple(f"{tag}_dwq", r["cqn"], lambda tk, tn: op_b(dq_raw, tk, tn), Q_RANK, 2048, T, Q_RANK, 1024, 512,
                             a_t=True, out_dtype=BF16)
    (g["wk_p"],) = mm_simple(f"{tag}_dwk", r["ckvn"], lambda tk, tn: op_b(dkk_raw, tk, tn), KV_RANK, 2048, T, KV_RANK, 1024,
                             512, a_t=True, out_dtype=BF16)
    (g["wv"],) = mm_simple(f"{tag}_dwv", r["ckvn"], lambda tk, tn: op_b(dvv, tk, tn), KV_RANK, ATTN_W, T, KV_RANK, 1024, 512,
                           a_t=True, out_dtype=BF16)
    (dcqn,) = mm_simple(f"{tag}_dcqn", dq_raw, lambda tk, tn: op_bt(w["wq_p"], tk, tn), T, Q_RANK, 2048, 512, Q_RANK, 2048)
    (dck1,) = mm_simple(f"{tag}_dckvn_k", dkk_raw, lambda tk, tn: op_bt(w["wk_p"], tk, tn), T, KV_RANK, 2048, 512, KV_RANK,
                        2048)
    (dckvn,) = mm_simple(f"{tag}_dckvn_v", dvv, lambda tk, tn: op_bt(w["wv"], tk, tn), T, KV_RANK, ATTN_W, 512, KV_RANK,
                         ATTN_W, extras=[tile_mn(dck1, 512, KV_RANK)], epilogue=lambda accs, xs: (accs[0] + xs[0],))
    dc_q, g["q_a_norm"] = rms_bwd(f"{tag}_cq_rms_bwd", r["z_p"], w["q_a_norm"], dcqn, Q_RANK, col_blk=0)
    dc_kv, g["kv_a_norm"] = rms_bwd(f"{tag}_ckv_rms_bwd", r["z_p"], w["kv_a_norm"], dckvn, KV_RANK, col_blk=2)
    du, dv, g["wc"], g["bb"], g["gm_v_norm"], g["gm_out_norm"] = gmlp_bwd(
        tag, r["z_p"], dmixed, w["gm_v_norm"], w["gm_out_norm"], w["wc"], w["bb"])
    dz_p = jnp.concatenate([dc_q, dc_kv, dzkr, du, dv], axis=1).astype(BF16)
    (g["w_in_pt"],) = mm_simple(f"{tag}_dwin", dz_p, lambda tk, tn: op_b(r["n2"], tk, tn), IN_P, D_MODEL, T, 1024, 1024, 512,
                                a_t=True, out_dtype=BF16)
    (dn2,) = mm_simple(f"{tag}_dn2", dz_p, lambda tk, tn: op_b(w["w_in_pt"], tk, tn), T, D_MODEL, IN_P, 512, 1024, IN_P)
    dh1, g["mix_norm"] = rms_bwd(f"{tag}_mix_rms_bwd", h, w["mix_norm"], dn2, D_MODEL, dres=dh2)
    return dh1, g


def ple_fwd(tag, h3, p_l, w, wpg_g, wple_g, pre):
    T = h3.shape[0]
    (pw,) = mm_simple(f"{tag}_wple", p_l, lambda tk, tn: op_b_cols(wple_g, pre, tk, tn), T, D_MODEL, PLE_DIM, 512, 512,
                      PLE_DIM)
    e = rms_fwd(f"{tag}_ple_rms", pw, w["ple_norm"], D_MODEL, out_dtype=F32)
    n4 = rms_fwd(f"{tag}_pg_rms", h3, w["ple_gate_norm"], D_MODEL)

    def epi(accs, xs):
        gt = _sigmoid(accs[0])
        return xs[0] + gt * xs[1], gt

    tm, tn, tk = 512, 1024, 512
    h4, gate = matmul(
        f"{tag}_wpg", (T // tm, D_MODEL // tn, D_MODEL // tk),
        [op_a(n4, tm, tk)], [op_b_rows(wpg_g, pre, tk, tn)], [(0, 0, 0)], 1,
        [tile_mn(h3, tm, tn), tile_mn(e, tm, tn)],
        [out_mn(T, D_MODEL, tm, tn, F32), out_mn(T, D_MODEL, tm, tn, BF16)], epi, (tm, tn))
    return h4, dict(pw=pw, e=e, n4=n4, gate=gate)


def ple_bwd(tag, dh4, h3, p_l, w, wpg_g, wple_g, pre, r, tm=256):
    T = h3.shape[0]

    def act_body(d_ref, g_ref, e_ref, dpre_ref, de_ref):
        d, gt = d_ref[...], g_ref[...].astype(F32)
        dpre_ref[...] = (d * e_ref[...] * gt * (1.0 - gt)).astype(BF16)
        de_ref[...] = d * gt

    blk = pl.BlockSpec((tm, D_MODEL), lambda i: (i, 0))
    dpre, de = pl.pallas_call(
        act_body, name=f"{tag}_ple_act_bwd", grid=(T // tm,), in_specs=[blk, blk, blk], out_specs=[blk, blk],
        out_shape=[jax.ShapeDtypeStruct((T, D_MODEL), BF16), jax.ShapeDtypeStruct((T, D_MODEL), F32)],
        compiler_params=_params(("parallel",)),
    )(dh4, r["gate"], r["e"])
    g = {}
    (g["w_ple_gate"],) = mm_simple(f"{tag}_dwpg", r["n4"], lambda tk, tn: op_b(dpre, tk, tn), D_MODEL, D_MODEL, T,
                                   1024, 1024, 512, a_t=True, out_dtype=BF16)
    (dn4,) = mm_simple(f"{tag}_dn4", dpre, lambda tk, tn: op_b_rows_t(wpg_g, pre, tk, tn), T, D_MODEL, D_MODEL, 512, 512,
                       D_MODEL)
    dh3, g["ple_gate_norm"] = rms_bwd(f"{tag}_pg_rms_bwd", h3, w["ple_gate_norm"], dn4, D_MODEL, dres=dh4)
    dpw, g["ple_norm"] = rms_bwd(f"{tag}_ple_rms_bwd", r["pw"], w["ple_norm"], de, D_MODEL)
    (g["w_ple"],) = mm_simple(f"{tag}_dwple", p_l, lambda tk, tn: op_b(dpw, tk, tn), PLE_DIM, D_MODEL, T, PLE_DIM, 512, 512,
                              a_t=True, outs=[out_cols(PLE_DIM, 512, PLE_DIM, 512, BF16)])
    return dh3, g


def loss_grad(y, target, tm=256):
    T = y.shape[0]

    def body(y_ref, t_ref, dy_ref, l_ref):
        i = pl.program_id(0)
        d = y_ref[...] - t_ref[...]
        dy_ref[...] = d * (1.0 / D_MODEL)
        part = jnp.sum((d * d).reshape(tm // 8, 8, D_MODEL), axis=0)

        @pl.when(i == 0)
        def _():
            l_ref[...] = part

        @pl.when(i > 0)
        def _():
            l_ref[...] += part

    blk = pl.BlockSpec((tm, D_MODEL), lambda i: (i, 0))
    dy, part = pl.pallas_call(
        body, name="loss_grad", grid=(T // tm,), in_specs=[blk, blk],
        out_specs=[blk, pl.BlockSpec((8, D_MODEL), lambda i: (0, 0))],
        out_shape=[jax.ShapeDtypeStruct((T, D_MODEL), F32), jax.ShapeDtypeStruct((8, D_MODEL), F32)],
        compiler_params=_params(("arbitrary",)),
    )(y, target)
    return dy, 0.5 * jnp.sum(part) / D_MODEL


def _unshard_cols(g_l):
    return g_l.transpose(1, 0, 2).reshape(g_l.shape[1], -1)


def _shard_cols(w):
    return w.reshape(w.shape[0], N_CHIPS, -1).transpose(1, 0, 2)


def layer_weights(l, Gl, small):
    w = {k: small[k][l] for k in ("mix_norm", "q_a_norm", "kv_a_norm", "gm_v_norm", "attn_out_norm", "gm_out_norm",
                                  "ple_gate_norm", "ple_norm")}
    wint = Gl["w_in"][:, :IN_SHARD].reshape(-1, D_MODEL)
    z = lambda n: jnp.zeros((n, D_MODEL), BF16)
    w["w_in_pt"] = jnp.concatenate([wint[:768], z(128), wint[768:832], z(64), wint[832:]], axis=0)
    wuq = _unshard_cols(Gl["w_uq"]).reshape(Q_RANK, HEADS, QK_DIM)
    w["wq_p"] = jnp.pad(wuq, ((0, 0), (0, 0), (0, HEAD_PAD - QK_DIM))).reshape(Q_RANK, HEADS * HEAD_PAD)
    wukv = _unshard_cols(Gl["w_ukv"]).reshape(KV_RANK, HEADS, QK_NOPE + V_DIM)
    w["wk_p"] = jnp.pad(wukv[:, :, :QK_NOPE], ((0, 0), (0, 0), (0, HEAD_PAD - QK_NOPE))).reshape(KV_RANK, HEADS * HEAD_PAD)
    w["wv"] = wukv[:, :, QK_NOPE:].reshape(KV_RANK, ATTN_W)
    w["gq_p"] = jnp.pad(small["q_norm"][l], (0, HEAD_PAD - QK_DIM)).reshape(1, HEAD_PAD)
    w["gk_p"] = jnp.pad(small["k_norm"][l], (0, HEAD_PAD - QK_DIM)).reshape(1, HEAD_PAD)
    tril = jnp.tril(jnp.ones((CHUNK, CHUNK), dtype=bool))
    w["wc"] = jnp.where(tril[None], small["gm_ws"][l], 0.0).astype(BF16)
    w["bb"] = jnp.broadcast_to(small["gm_bs"][l][:, :, None], (GROUPS, CHUNK, 128)).astype(F32)
    return w


def mixer_grads_to_shards(g):
    out = {}
    dwint = g["w_in_pt"]
    dwint = jnp.concatenate([dwint[:768], dwint[896:960], dwint[1024:]], axis=0).reshape(N_CHIPS, IN_SHARD, D_MODEL)
    out["w_in"] = jnp.pad(dwint, ((0, 0), (0, IN_SHARD_PAD - IN_SHARD), (0, 0)))
    dwuq = g["wq_p"].reshape(Q_RANK, HEADS, HEAD_PAD)[:, :, :QK_DIM].reshape(Q_RANK, HEADS * QK_DIM)
    out["w_uq"] = _shard_cols(dwuq)
    dwukv = jnp.concatenate([g["wk_p"].reshape(KV_RANK, HEADS, HEAD_PAD)[:, :, :QK_NOPE],
                             g["wv"].reshape(KV_RANK, HEADS, V_DIM)], axis=-1).reshape(KV_RANK, HEADS * (QK_NOPE + V_DIM))
    out["w_ukv"] = _shard_cols(dwukv)
    out["w_out"] = g["w_out"].reshape(N_CHIPS, D_MODEL // N_CHIPS, D_MODEL)
    out["q_norm"] = g["gq_p"][0, :QK_DIM]
    out["k_norm"] = g["gk_p"][0, :QK_DIM]
    tril = jnp.tril(jnp.ones((CHUNK, CHUNK), dtype=bool))
    out["gm_ws"] = jnp.where(tril[None], g["wc"], 0.0)
    out["gm_bs"] = jnp.sum(g["bb"], axis=-1)
    for k in ("mix_norm", "q_a_norm", "kv_a_norm", "gm_v_norm", "attn_out_norm", "gm_out_norm"):
        out[k] = g[k][0]
    return out


def layer_fwd(l, h, p_l, Gl, small, tabs):
    w = layer_weights(l, Gl, small)
    h1, r_a = ffn_fwd(f"l{l}a", h, small["ffn_a_norm"][l], Gl["ffn_a_w1"], Gl["ffn_a_w3"], Gl["ffn_a_w2"], ())
    h2, r_m = mixer_fwd(f"l{l}", h1, w, tabs, Gl["w_out"], ())
    h3, r_b = ffn_fwd(f"l{l}b", h2, small["ffn_b_norm"][l], Gl["ffn_b_w1"], Gl["ffn_b_w3"], Gl["ffn_b_w2"], ())
    h4, r_p = ple_fwd(f"l{l}", h3, p_l, w, Gl["w_ple_gate"], Gl["w_ple"], ())
    return h4, (w, h, h1, h2, h3, r_a, r_m, r_b, r_p)


def layer_bwd(l, dh, p_l, Gl, small, tabs, saved):
    w, h0, h1, h2, h3, r_a, r_m, r_b, r_p = saved
    slabs = lambda d: d.reshape(N_CHIPS, FF_PAD, D_MODEL)
    gl = {}
    dh, g_p = ple_bwd(f"l{l}", dh, h3, p_l, w, Gl["w_ple_gate"], Gl["w_ple"], (), r_p)
    gl["w_ple_gate"] = g_p["w_ple_gate"].reshape(N_CHIPS, D_MODEL // N_CHIPS, D_MODEL)
    gl["w_ple"] = g_p["w_ple"]
    gl["ple_gate_norm"], gl["ple_norm"] = g_p["ple_gate_norm"][0], g_p["ple_norm"][0]
    dh, dg, dw1, dw3, dw2 = ffn_bwd(f"l{l}b", dh, h2, small["ffn_b_norm"][l], r_b,
                                    Gl["ffn_b_w1"], Gl["ffn_b_w3"], Gl["ffn_b_w2"], ())
    gl["ffn_b_norm"] = dg[0]
    gl["ffn_b_w1"], gl["ffn_b_w3"], gl["ffn_b_w2"] = slabs(dw1), slabs(dw3), slabs(dw2)
    dh, g_m = mixer_bwd(f"l{l}", dh, h1, w, tabs, Gl["w_out"], (), r_m)
    gl.update(mixer_grads_to_shards(g_m))
    dh, dg, dw1, dw3, dw2 = ffn_bwd(f"l{l}a", dh, h0, small["ffn_a_norm"][l], r_a,
                                    Gl["ffn_a_w1"], Gl["ffn_a_w3"], Gl["ffn_a_w2"], ())
    gl["ffn_a_norm"] = dg[0]
    gl["ffn_a_w1"], gl["ffn_a_w3"], gl["ffn_a_w2"] = slabs(dw1), slabs(dw3), slabs(dw2)
    return dh, gl


MESH = pl.DeviceIdType.MESH
HBM_SPEC = pl.BlockSpec(memory_space=pltpu.HBM)


def _place():
    x, y, c = lax.axis_index("x"), lax.axis_index("y"), lax.axis_index("c")
    others = [(1 - x, y), (x, 1 - y), (1 - x, 1 - y)]
    return x, y, c, 2 * x + y, others


def prep_shard(name, w, layer, rows_pad, place):
    _, ks, n = w.shape
    ksp = ks + rows_pad
    tc = 512 if n % 512 == 0 else n

    def body(place_ref, x_ref, o_ref):
        o_ref[:ks] = x_ref[...].astype(BF16)
        if rows_pad:
            o_ref[ks:] = jnp.zeros((rows_pad, tc), BF16)

    return pl.pallas_call(
        body, name=name,
        grid_spec=pltpu.PrefetchScalarGridSpec(
            num_scalar_prefetch=1, grid=(n // tc,),
            in_specs=[pl.BlockSpec((None, ks, tc), lambda i, s: (layer, 0, i))],
            out_specs=pl.BlockSpec((None, ksp, tc), lambda i, s: (s[0], 0, i))),
        out_shape=jax.ShapeDtypeStruct((N_CHIPS, ksp, n), BF16),
        compiler_params=_params(("parallel",)),
    )(place, w)


def gather_weights(name, slots):
    n = len(slots)

    def body(*refs):
        g_refs = refs[n:2 * n]
        ici_send, ici_recv, d2d_send, d2d_recv = refs[2 * n:]
        x, y, c, jme, others = _place()
        sib = (x, y, 1 - c)

        def half(w, j):
            kh = slots[w].shape[1] // 2
            return g_refs[w].at[j, pl.ds(c * kh, kh)]

        def three(w):
            return g_refs[w].at[pl.ds(0, 3), pl.ds(0, slots[w].shape[1] // 2)]

        for w in range(n):
            for (px, py) in others:
                pltpu.make_async_remote_copy(
                    src_ref=half(w, jme), dst_ref=half(w, jme), send_sem=ici_send.at[w], recv_sem=ici_recv.at[w],
                    device_id=(px, py, c), device_id_type=MESH).start()
        for w in range(n):
            pltpu.make_async_remote_copy(src_ref=three(w), dst_ref=three(w), send_sem=ici_send.at[w],
                                         recv_sem=ici_recv.at[w], device_id=sib, device_id_type=MESH).wait_recv()
            for (px, py) in others:
                blk = half(w, 2 * px + py)
                pltpu.make_async_remote_copy(src_ref=blk, dst_ref=blk, send_sem=d2d_send.at[w], recv_sem=d2d_recv.at[w],
                                             device_id=sib, device_id_type=MESH).start()
        for w in range(n):
            wait3 = pltpu.make_async_remote_copy(src_ref=three(w), dst_ref=three(w), send_sem=d2d_send.at[w],
                                                 recv_sem=d2d_recv.at[w], device_id=sib, device_id_type=MESH)
            wait3.wait_recv()
            wait3.wait_send()
            pltpu.make_async_remote_copy(src_ref=three(w), dst_ref=three(w), send_sem=ici_send.at[w],
                                         recv_sem=ici_recv.at[w], device_id=sib, device_id_type=MESH).wait_send()

    return pl.pallas_call(
        body, name=name,
        in_specs=[HBM_SPEC] * n, out_specs=[HBM_SPEC] * n,
        out_shape=[jax.ShapeDtypeStruct(s.shape, s.dtype) for s in slots],
        input_output_aliases={w: w for w in range(n)},
        scratch_shapes=[pltpu.SemaphoreType.DMA((n,))] * 4,
    )(*slots)


def exchange_halves(name, grads):
    n = len(grads)

    def body(*refs):
        d_refs, r_refs = refs[:n], refs[n:2 * n]
        send, recv = refs[2 * n:]
        x, y, c, _, _ = _place()
        cps = []
        for w in range(n):
            half = grads[w].shape[1] // 2
            cps.append(pltpu.make_async_remote_copy(
                src_ref=d_refs[w].at[pl.ds(0, N_CHIPS), pl.ds((1 - c) * half, half)], dst_ref=r_refs[w],
                send_sem=send.at[w], recv_sem=recv.at[w], device_id=(x, y, 1 - c), device_id_type=MESH))
        for cp in cps:
            cp.start()
        for cp in cps:
            cp.wait()

    return pl.pallas_call(
        body, name=name, in_specs=[HBM_SPEC] * n, out_specs=[HBM_SPEC] * n,
        out_shape=[jax.ShapeDtypeStruct((N_CHIPS, g.shape[1] // 2, g.shape[2]), g.dtype) for g in grads],
        scratch_shapes=[pltpu.SemaphoreType.DMA((n,))] * 2,
    )(*grads)


def scatter_slabs(name, parts):
    n = len(parts)

    def body(*refs):
        p_refs, q_refs = refs[:n], refs[n:2 * n]
        send, recv = refs[2 * n:]
        x, y, c, jme, others = _place()
        for w in range(n):
            for (px, py) in others:
                pltpu.make_async_remote_copy(
                    src_ref=p_refs[w].at[2 * px + py], dst_ref=q_refs[w].at[jme], send_sem=send.at[w], recv_sem=recv.at[w],
                    device_id=(px, py, c), device_id_type=MESH).start()
        for w in range(n):
            three = q_refs[w].at[pl.ds(0, 3)]
            wait3 = pltpu.make_async_remote_copy(src_ref=three, dst_ref=three, send_sem=send.at[w], recv_sem=recv.at[w],
                                                 device_id=(x, y, c), device_id_type=MESH)
            wait3.wait_recv()
            wait3.wait_send()

    return pl.pallas_call(
        body, name=name, in_specs=[HBM_SPEC] * n, out_specs=[HBM_SPEC] * n,
        out_shape=[jax.ShapeDtypeStruct(p.shape, p.dtype) for p in parts],
        scratch_shapes=[pltpu.SemaphoreType.DMA((n,))] * 2,
    )(*parts)


def share_halves(fulls):
    n = len(fulls)

    def body(*refs):
        o_refs = refs[n:2 * n]
        send, recv = refs[2 * n:]
        x, y, c, _, _ = _place()
        cps = []
        for w in range(n):
            kh = fulls[w].shape[1] // 2
            for l in range(2):
                half = o_refs[w].at[l, pl.ds(c * kh, kh)]
                k = 2 * w + l
                cps.append(pltpu.make_async_remote_copy(src_ref=half, dst_ref=half, send_sem=send.at[k], recv_sem=recv.at[k],
                                                        device_id=(x, y, 1 - c), device_id_type=MESH))
        for cp in cps:
            cp.start()
        for cp in cps:
            cp.wait()

    return pl.pallas_call(
        body, name="share_halves", in_specs=[HBM_SPEC] * n, out_specs=[HBM_SPEC] * n,
        out_shape=[jax.ShapeDtypeStruct(f.shape, f.dtype) for f in fulls],
        input_output_aliases={w: w for w in range(n)},
        scratch_shapes=[pltpu.SemaphoreType.DMA((2 * n,))] * 2,
    )(*fulls)


SEM_SPEC = pl.BlockSpec(memory_space=pltpu.SEMAPHORE)
ANY_SPEC = pl.BlockSpec(memory_space=pl.ANY)
DATAFLOW = pltpu.SideEffectType.DATAFLOW_SIDE_EFFECTING


def _hbm(x):
    return pltpu.with_memory_space_constraint(x, pltpu.HBM)


def gather_start(name, slots):
    n = len(slots)

    def body(*refs):
        g_refs = refs[n + 2:2 * n + 2]
        send, recv, token = refs[n], refs[n + 1], refs[2 * n + 2]
        x, y, c, jme, others = _place()
        for w in range(n):
            kh = slots[w].shape[1] // 2
            mine = g_refs[w].at[jme, pl.ds(c * kh, kh)]
            for (px, py) in others:
                for core in range(2):
                    pltpu.make_async_remote_copy(src_ref=mine, dst_ref=mine, send_sem=send.at[w], recv_sem=recv.at[w],
                                                 device_id=(px, py, core), device_id_type=MESH).start()
        token[...] = jnp.zeros_like(token)

    outs = pl.pallas_call(
        body, name=name,
        in_specs=[HBM_SPEC] * n,
        out_specs=(SEM_SPEC, SEM_SPEC, *([HBM_SPEC] * n), pl.BlockSpec(memory_space=pltpu.VMEM)),
        out_shape=(pltpu.SemaphoreType.DMA((n,)), pltpu.SemaphoreType.DMA((n,)),
                   *[pltpu.HBM(s.shape, s.dtype) for s in slots], jax.ShapeDtypeStruct((8, 128), F32)),
        input_output_aliases={w: w + 2 for w in range(n)},
        compiler_params=pltpu.CompilerParams(has_side_effects=DATAFLOW),
    )(*[_hbm(s) for s in slots])
    return outs[0], outs[1], list(outs[2:2 + n]), outs[2 + n]


def gather_wait(name, send, recv, flying, after):
    n = len(flying)

    def body(*refs):
        send_ref, recv_ref = refs[n], refs[n + 1]
        g_refs = refs[n + 3:]
        x, y, c, _, _ = _place()
        for w in range(n):
            six = g_refs[w].at[pl.ds(0, 3)]
            cp = pltpu.make_async_remote_copy(src_ref=six, dst_ref=six, send_sem=send_ref.at[w], recv_sem=recv_ref.at[w],
                                              device_id=(x, y, 1 - c), device_id_type=MESH)
            cp.wait_send()
            cp.wait_recv()

    return pl.pallas_call(
        body, name=name,
        in_specs=[HBM_SPEC] * n + [SEM_SPEC, SEM_SPEC, ANY_SPEC],
        out_specs=[HBM_SPEC] * n,
        out_shape=[pltpu.HBM(s.shape, s.dtype) for s in flying],
        input_output_aliases={w: w for w in range(n)},
        compiler_params=pltpu.CompilerParams(has_side_effects=DATAFLOW),
    )(*flying, send, recv, after)


def scatter_start(name, parts):
    n = len(parts)

    def body(*refs):
        p_refs, q_refs = refs[2 * n + 2:3 * n + 2], refs[3 * n + 2:4 * n + 2]
        send, recv, token = refs[2 * n], refs[2 * n + 1], refs[4 * n + 2]
        x, y, c, jme, others = _place()
        for w in range(n):
            for (px, py) in others:
                pltpu.make_async_remote_copy(
                    src_ref=p_refs[w].at[2 * px + py], dst_ref=q_refs[w].at[jme], send_sem=send.at[w], recv_sem=recv.at[w],
                    device_id=(px, py, c), device_id_type=MESH).start()
        token[...] = jnp.zeros_like(token)

    lands = [_hbm(lax.empty(p.shape, p.dtype)) for p in parts]
    outs = pl.pallas_call(
        body, name=name,
        in_specs=[HBM_SPEC] * (2 * n),
        out_specs=(SEM_SPEC, SEM_SPEC, *([HBM_SPEC] * (2 * n)), pl.BlockSpec(memory_space=pltpu.VMEM)),
        out_shape=(pltpu.SemaphoreType.DMA((n,)), pltpu.SemaphoreType.DMA((n,)),
                   *[pltpu.HBM(p.shape, p.dtype) for p in parts], *[pltpu.HBM(p.shape, p.dtype) for p in parts],
                   jax.ShapeDtypeStruct((8, 128), F32)),
        input_output_aliases={w: w + 2 for w in range(2 * n)},
        compiler_params=pltpu.CompilerParams(has_side_effects=DATAFLOW),
    )(*[_hbm(p) for p in parts], *lands)
    return outs[0], outs[1], list(outs[2:2 + n]), list(outs[2 + n:2 + 2 * n]), outs[2 + 2 * n]


def scatter_wait(name, send, recv, parts, lands, after):
    n = len(parts)

    def body(*refs):
        send_ref, recv_ref = refs[2 * n], refs[2 * n + 1]
        q_refs = refs[3 * n + 3:]
        x, y, c, _, _ = _place()
        for w in range(n):
            three = q_refs[w].at[pl.ds(0, 3)]
            cp = pltpu.make_async_remote_copy(src_ref=three, dst_ref=three, send_sem=send_ref.at[w], recv_sem=recv_ref.at[w],
                                              device_id=(x, y, 1 - c), device_id_type=MESH)
            cp.wait_send()
            cp.wait_recv()

    outs = pl.pallas_call(
        body, name=name,
        in_specs=[HBM_SPEC] * (2 * n) + [SEM_SPEC, SEM_SPEC, ANY_SPEC],
        out_specs=[HBM_SPEC] * (2 * n),
        out_shape=[pltpu.HBM(p.shape, p.dtype) for p in parts] * 2,
        input_output_aliases={w: w for w in range(2 * n)},
        compiler_params=pltpu.CompilerParams(has_side_effects=DATAFLOW),
    )(*parts, *lands, send, recv, after)
    return list(outs[:n]), list(outs[n:])


def allreduce_small(v):
    R = v.shape[0]

    def body(v_ref, o_ref, sib_ref, mine_ref, all_ref, d_send, d_recv, i_send, i_recv):
        x, y, c, jme, others = _place()
        swap = pltpu.make_async_remote_copy(src_ref=v_ref, dst_ref=sib_ref, send_sem=d_send, recv_sem=d_recv,
                                            device_id=(x, y, 1 - c), device_id_type=MESH)
        swap.start()
        swap.wait()
        mine_ref[...] = v_ref[...] + sib_ref[...]
        for (px, py) in others:
            pltpu.make_async_remote_copy(src_ref=mine_ref, dst_ref=all_ref.at[jme], send_sem=i_send, recv_sem=i_recv,
                                         device_id=(px, py, c), device_id_type=MESH).start()
        three = all_ref.at[pl.ds(0, 3)]
        wait3 = pltpu.make_async_remote_copy(src_ref=three, dst_ref=three, send_sem=i_send, recv_sem=i_recv,
                                             device_id=(x, y, c), device_id_type=MESH)
        wait3.wait_recv()
        wait3.wait_send()
        all_ref[jme] = mine_ref[...]
        o_ref[...] = ((all_ref[0] + all_ref[1]) + all_ref[2]) + all_ref[3]

    vm = pl.BlockSpec(memory_space=pltpu.VMEM)
    return pl.pallas_call(
        body, name="allreduce_small", in_specs=[vm], out_specs=vm,
        out_shape=jax.ShapeDtypeStruct(v.shape, F32),
        scratch_shapes=[pltpu.VMEM((R, 128), F32), pltpu.VMEM((R, 128), F32), pltpu.VMEM((N_CHIPS, R, 128), F32),
                        pltpu.SemaphoreType.DMA, pltpu.SemaphoreType.DMA, pltpu.SemaphoreType.DMA, pltpu.SemaphoreType.DMA],
        compiler_params=pltpu.CompilerParams(vmem_limit_bytes=VMEM_LIMIT_BYTES),
    )(v)


def _row_tile(rows, width, mult=16, cap=3 << 20):
    best = rows
    for t in range(mult, rows + 1, mult):
        if rows % t == 0 and t * width * 4 <= cap:
            best = t
    return best


def add_sibling(name, mine, theirs, place):
    _, kh, ns = theirs.shape
    tr = _row_tile(kh, ns)
    nblk = kh // tr

    def body(place_ref, a_ref, b_ref, o_ref):
        o_ref[...] = (a_ref[...].astype(F32) + b_ref[...].astype(F32)).astype(BF16)

    return pl.pallas_call(
        body, name=name,
        grid_spec=pltpu.PrefetchScalarGridSpec(
            num_scalar_prefetch=1, grid=(N_CHIPS, nblk),
            in_specs=[pl.BlockSpec((None, tr, ns), lambda j, i, s: (j, s[1] * nblk + i, 0)),
                      pl.BlockSpec((None, tr, ns), lambda j, i, s: (j, i, 0))],
            out_specs=pl.BlockSpec((None, tr, ns), lambda j, i, s: (j, i, 0))),
        out_shape=jax.ShapeDtypeStruct(theirs.shape, BF16),
        compiler_params=_params(("parallel", "parallel")),
    )(place, mine, theirs)


def add_chips(name, q, p, place, layer, full=None):
    _, kh, ns = q.shape
    tr = _row_tile(kh, ns)
    nblk = kh // tr

    def body(place_ref, *refs):
        q_refs, own_ref, o_ref = refs[:N_CHIPS], refs[N_CHIPS], refs[-1]
        jme = place_ref[0]
        tot = None
        for j in range(N_CHIPS):
            v = jnp.where(jme == j, own_ref[...], q_refs[j][...]).astype(F32)
            tot = v if tot is None else tot + v
        o_ref[...] = tot

    def q_ix(j):
        return lambda i, s: (jnp.where(s[0] == j, (j + 1) % N_CHIPS, j), i, 0)

    in_specs = [pl.BlockSpec((None, tr, ns), q_ix(j)) for j in range(N_CHIPS)]
    in_specs.append(pl.BlockSpec((None, tr, ns), lambda i, s: (s[0], i, 0)))
    args = [place, q, q, q, q, p]
    aliases = {}
    if full is not None:
        in_specs.append(pl.BlockSpec(memory_space=pl.ANY))
        args.append(full)
        aliases = {len(args) - 1: 0}
    return pl.pallas_call(
        body, name=name,
        grid_spec=pltpu.PrefetchScalarGridSpec(
            num_scalar_prefetch=1, grid=(nblk,), in_specs=in_specs,
            out_specs=pl.BlockSpec((None, tr, ns), lambda i, s: (layer, s[1] * nblk + i, 0))),
        out_shape=jax.ShapeDtypeStruct((2, 2 * kh, ns), F32),
        input_output_aliases=aliases,
        compiler_params=_params(("parallel",)),
    )(*args)


ADAM_LR, ADAM_B1, ADAM_B2, ADAM_EPS, ADAM_WD, ADAM_STEP = 0.001, 0.9, 0.999, 1e-08, 0.01, 10


def adamw(name, w, g, m, v):
    _, k, ns = w.shape
    nsp = g.shape[2]
    tr = _row_tile(k, nsp, mult=8, cap=2 << 20)

    def body(w_ref, g_ref, m_ref, v_ref, go_ref, d_ref, mo_ref, vo_ref):
        gv = g_ref[:, :ns] if nsp != ns else g_ref[...]
        mn = ADAM_B1 * m_ref[...] + (1.0 - ADAM_B1) * gv
        vn = ADAM_B2 * v_ref[...] + (1.0 - ADAM_B2) * (gv * gv)
        m_hat = mn / (1.0 - ADAM_B1 ** ADAM_STEP)
        v_hat = vn / (1.0 - ADAM_B2 ** ADAM_STEP)
        go_ref[...] = gv
        d_ref[...] = -ADAM_LR * (m_hat / (jnp.sqrt(v_hat) + ADAM_EPS) + ADAM_WD * w_ref[...])
        mo_ref[...] = mn
        vo_ref[...] = vn

    blk = pl.BlockSpec((None, tr, ns), lambda l, i: (l, i, 0))
    gblk = pl.BlockSpec((None, tr, nsp), lambda l, i: (l, i, 0))
    return pl.pallas_call(
        body, name=name, grid=(2, k // tr), in_specs=[blk, gblk, blk, blk], out_specs=[blk] * 4,
        out_shape=[jax.ShapeDtypeStruct(w.shape, F32)] * 4, compiler_params=_params(("parallel", "parallel")),
    )(w, g, m, v)


WEIGHTS = ("ffn_a_norm", "ffn_a_w1", "ffn_a_w3", "ffn_a_w2", "mix_norm", "w_in", "q_a_norm", "w_uq", "kv_a_norm", "w_ukv",
           "q_norm", "k_norm", "gm_v_norm", "gm_ws", "gm_bs", "attn_out_norm", "gm_out_norm", "w_out", "ffn_b_norm",
           "ffn_b_w1", "ffn_b_w3", "ffn_b_w2", "ple_gate_norm", "w_ple_gate", "w_ple", "ple_norm")
_FF = FF_PAD - FF_SHARD
BIG = {"ffn_a_w1": _FF, "ffn_a_w3": _FF, "ffn_a_w2": _FF, "ffn_b_w1": _FF, "ffn_b_w3": _FF, "ffn_b_w2": _FF,
       "w_in": IN_SHARD_PAD - IN_SHARD, "w_uq": 0, "w_ukv": 0, "w_ple": 0, "w_out": 0, "w_ple_gate": 0}
TRANSPOSED = ("ffn_a_w1", "ffn_a_w3", "ffn_b_w1", "ffn_b_w3", "w_in")
SMALL = tuple(n for n in WEIGHTS if n not in BIG)
PACK = 1024


def _pack_small(d):
    parts = []
    for n in SMALL:
        flat = d[n].reshape(-1)
        parts.append(jnp.pad(flat, (0, (-flat.shape[0]) % PACK)))
    return jnp.concatenate(parts).reshape(-1, 128)


def _unpack_small(buf, like):
    flat = buf.reshape(-1)
    out, pos = {}, 0
    for n in SMALL:
        size = math.prod(like[n].shape)
        out[n] = flat[pos:pos + size].reshape(like[n].shape)
        pos += size + (-size) % PACK
    return out


def kernel(*args):
    names = (("x", "p", "positions") + WEIGHTS + ("loss_target",) + tuple("m_" + n for n in WEIGHTS)
             + tuple("v_" + n for n in WEIGHTS))
    a = dict(zip(names, args, strict=True))
    x, p, positions, target = a["x"][0], a["p"][:, 0], a["positions"][0], a["loss_target"][0]
    for n in TRANSPOSED:
        for pre in ("", "m_", "v_"):
            a[pre + n] = jnp.swapaxes(a[pre + n], 1, 2)

    place = jnp.stack([2 * lax.axis_index("x") + lax.axis_index("y"), lax.axis_index("c")]).astype(jnp.int32)
    small = {n: a[n] for n in SMALL}
    tabs = rope_tables(positions)
    slots = [[prep_shard(f"prep_{n}_{l}", a[n], l, BIG[n], place) for n in BIG] for l in range(2)]
    G0 = dict(zip(BIG, gather_weights("gather_l0", slots[0])))
    g_send, g_recv, flying, token = gather_start("gather_l1_start", slots[1])
    small0 = {**small, "ffn_a_norm": small["ffn_a_norm"] + token[0, 0]}
    h, saved0 = layer_fwd(0, x, p[0], G0, small0, tabs)
    G1 = dict(zip(BIG, gather_wait("gather_l1_wait", g_send, g_recv, flying, h)))
    h, saved1 = layer_fwd(1, h, p[1], G1, small, tabs)
    dh, loss = loss_grad(h, target)
    loss = lax.psum(loss, ("x", "y", "c"))

    def chip_partials(l, gl):
        mine = [gl[n] for n in BIG]
        theirs = exchange_halves(f"exchange_l{l}", mine)
        return [add_sibling(f"add_sibling_{n}_{l}", d, r, place) for n, d, r in zip(BIG, mine, theirs)]

    def sum_chips(l, slabs, parts, fulls):
        return [add_chips(f"add_chips_{n}_{l}", q, pt, place, l, full=f) for n, q, pt, f in zip(BIG, slabs, parts, fulls)]

    grads = [None, None]
    dh, grads[1] = layer_bwd(1, dh, p[1], G1, small, tabs, saved1)
    s_send, s_recv, parts1, lands1, token = scatter_start("scatter_l1_start", chip_partials(1, grads[1]))
    w0 = {**saved0[0], "ple_gate_norm": saved0[0]["ple_gate_norm"] + token[0, 0]}
    gx, grads[0] = layer_bwd(0, dh, p[0], G0, small, tabs, (w0,) + saved0[1:])
    parts1, slabs1 = scatter_wait("scatter_l1_wait", s_send, s_recv, parts1, lands1, gx)
    fulls = sum_chips(1, slabs1, parts1, [None] * len(BIG))
    parts0 = chip_partials(0, grads[0])
    fulls = sum_chips(0, scatter_slabs("scatter_l0", parts0), parts0, fulls)
    full = dict(zip(BIG, share_halves(fulls)))

    out_g, out_d, out_m, out_v = {}, {}, {}, {}
    for n in BIG:
        outs = adamw(f"adamw_{n}", a[n], full[n], a["m_" + n], a["v_" + n])
        if n in TRANSPOSED:
            outs = [jnp.swapaxes(o, 1, 2) for o in outs]
        out_g[n], out_d[n], out_m[n], out_v[n] = outs

    gs = allreduce_small(_pack_small({n: jnp.stack([grads[0][n], grads[1][n]]) for n in SMALL}))
    rows = gs.shape[0] // 2
    sm = adamw("adamw_small", _pack_small(small).reshape(2, rows, 128), gs.reshape(2, rows, 128),
               _pack_small({n: a["m_" + n] for n in SMALL}).reshape(2, rows, 128),
               _pack_small({n: a["v_" + n] for n in SMALL}).reshape(2, rows, 128))
    for dst, buf in zip((out_g, out_d, out_m, out_v), sm):
        dst.update(_unpack_small(buf, small))

    return (loss, gx[None], *[out_g[n] for n in WEIGHTS], *[out_d[n] for n in WEIGHTS],
            *[out_m[n] for n in WEIGHTS], *[out_v[n] for n in WEIGHTS])
```

```python
import math

import jax
import jax.numpy as jnp
from jax import lax
from jax.experimental import pallas as pl
from jax.experimental.pallas import tpu as pltpu

F32 = jnp.float32
BF16 = jnp.bfloat16

D_MODEL = 2048
D_FF = 5504
N_CHIPS = 4
FF_SHARD = D_FF // N_CHIPS
FF_PAD = 1408
FF_P = N_CHIPS * FF_PAD
HEADS = 8
QK_NOPE = 128
QK_ROPE = 64
QK_DIM = 192
HEAD_PAD = 256
V_DIM = 128
Q_RANK = 512
KV_RANK = 256
ATTN_W = 1024
GM_W = 1024
GROUPS = 8
CHUNK = 128
PLE_DIM = 256
IN_P = 3072
IN_SHARD = 720
IN_SHARD_PAD = 736
EPS = 1e-6
ROPE_BASE = 10000.0
ATTN_SCALE = QK_DIM ** -0.5
VMEM_LIMIT_BYTES = 56 * 1024 * 1024


def _params(sem):
    return pltpu.CompilerParams(dimension_semantics=sem, vmem_limit_bytes=VMEM_LIMIT_BYTES)


def _bf(x):
    return x if x.dtype == BF16 else x.astype(BF16)


def _sigmoid(x):
    return 1.0 / (1.0 + jnp.exp(-x))


_GELU_C = math.sqrt(2.0 / math.pi)


def _gelu(x):
    t = jnp.tanh(_GELU_C * (x + 0.044715 * x * x * x))
    return 0.5 * x * (1.0 + t)


def _gelu_grad(x):
    t = jnp.tanh(_GELU_C * (x + 0.044715 * x * x * x))
    return 0.5 * (1.0 + t) + 0.5 * x * (1.0 - t * t) * _GELU_C * (1.0 + 3 * 0.044715 * x * x)


def op_a(a, tm, tk):
    return (a, (tm, tk), lambda i, j, k: (i, k), 1)


def op_at(a, tm, tk):
    return (a, (tk, tm), lambda i, j, k: (k, i), 0)


def op_b(b, tk, tn):
    return (b, (tk, tn), lambda i, j, k: (k, j), 0)


def op_bt(b, tk, tn):
    return (b, (tn, tk), lambda i, j, k: (j, k), 1)


def op_b_cols(g, pre, tk, tn):
    nb = g.shape[-1] // tn
    none = (None,) * (1 + len(pre))
    return (g, none + (tk, tn), lambda i, j, k: (j // nb,) + tuple(pre) + (k, j % nb), 0)


def op_b_rows(g, pre, tk, tn, koff=0):
    nb = g.shape[-2] // tk
    none = (None,) * (1 + len(pre))
    return (g, none + (tk, tn), lambda i, j, k: ((k + koff) // nb,) + tuple(pre) + ((k + koff) % nb, j), 0)


def op_b_rows_t(g, pre, tk, tn):
    nb = g.shape[-2] // tn
    none = (None,) * (1 + len(pre))
    return (g, none + (tn, tk), lambda i, j, k: (j // nb,) + tuple(pre) + (j % nb, k), 1)


def tile_mn(x, tm, tn):
    return (x, (tm, tn), lambda i, j: (i, j))


def out_mn(M, N, tm, tn, dtype):
    return (jax.ShapeDtypeStruct((M, N), dtype), (tm, tn), lambda i, j: (i, j))


def out_cols(M, ns, tm, tn, dtype):
    nb = ns // tn
    return (jax.ShapeDtypeStruct((N_CHIPS, M, ns), dtype), (None, tm, tn), lambda i, j: (j // nb, i, j % nb))


def matmul(name, grid_mnk, a_ops, b_ops, terms, n_acc, extras, outs, epilogue, acc_tile, n_outer=False, after=None):
    gm, gn, gk = grid_mnk
    na, nb, nx, no = len(a_ops), len(b_ops), len(extras), len(outs)
    nd = 0 if after is None else 1

    def body(*refs):
        a_refs, b_refs = refs[:na], refs[na:na + nb]
        x_refs = refs[na + nb:na + nb + nx]
        o_refs = refs[na + nb + nx + nd:na + nb + nx + nd + no]
        acc_refs = refs[na + nb + nx + nd + no:]
        k = pl.program_id(2)

        @pl.when(k == 0)
        def _():
            for acc in acc_refs:
                acc[...] = jnp.zeros_like(acc)

        for ai, bi, ci in terms:
            dims = (((a_ops[ai][3],), (b_ops[bi][3],)), ((), ()))
            acc_refs[ci][...] += lax.dot_general(_bf(a_refs[ai][...]), _bf(b_refs[bi][...]), dims,
                                                 preferred_element_type=F32)

        @pl.when(k == gk - 1)
        def _():
            res = epilogue([acc[...] for acc in acc_refs], [x[...] for x in x_refs])
            for o, v in zip(o_refs, res):
                o[...] = v.astype(o.dtype)

    if n_outer:
        grid = (gn, gm, gk)

        def ix3(f):
            return lambda j, i, k: f(i, j, k)

        def ix2(f):
            return lambda j, i, k: f(i, j)
    else:
        grid = (gm, gn, gk)

        def ix3(f):
            return lambda i, j, k: f(i, j, k)

        def ix2(f):
            return lambda i, j, k: f(i, j)

    in_specs = [pl.BlockSpec(blk, ix3(f)) for (_, blk, f, _) in list(a_ops) + list(b_ops)]
    in_specs += [pl.BlockSpec(blk, ix2(f)) for (_, blk, f) in extras]
    in_specs += [pl.BlockSpec(memory_space=pl.ANY)] * nd
    out_specs = [pl.BlockSpec(blk, ix2(f)) for (_, blk, f) in outs]
    return pl.pallas_call(
        body,
        name=name,
        grid=grid,
        in_specs=in_specs,
        out_specs=out_specs,
        out_shape=[s for (s, _, _) in outs],
        scratch_shapes=[pltpu.VMEM(acc_tile, F32) for _ in range(n_acc)],
        compiler_params=_params(("parallel", "parallel", "arbitrary")),
    )(*[o[0] for o in a_ops], *[o[0] for o in b_ops], *[x[0] for x in extras], *([after] * nd))


def _acc0(accs, xs):
    return (accs[0],)


def mm_simple(name, a, b_op_fn, M, N, K, tm, tn, tk, out_dtype=F32, a_t=False, extras=(), epilogue=_acc0, outs=None):
    a_op = op_at(a, tm, tk) if a_t else op_a(a, tm, tk)
    outs = outs or [out_mn(M, N, tm, tn, out_dtype)]
    return matmul(name, (M // tm, N // tn, K // tk), [a_op], [b_op_fn(tk, tn)], [(0, 0, 0)], 1,
                  list(extras), outs, epilogue, (tm, tn))


def rms_fwd(name, x, g, width, col_blk=0, tm=256, out_dtype=BF16):
    T = x.shape[0]

    def body(x_ref, g_ref, o_ref):
        xv = x_ref[...].astype(F32)
        r = lax.rsqrt(jnp.mean(xv * xv, axis=-1, keepdims=True) + EPS)
        o_ref[...] = (xv * r * g_ref[...]).astype(o_ref.dtype)

    return pl.pallas_call(
        body, name=name, grid=(T // tm,),
        in_specs=[pl.BlockSpec((tm, width), lambda i: (i, col_blk)), pl.BlockSpec((1, width), lambda i: (0, 0))],
        out_specs=pl.BlockSpec((tm, width), lambda i: (i, 0)),
        out_shape=jax.ShapeDtypeStruct((T, width), out_dtype),
        compiler_params=_params(("parallel",)),
    )(x, g.reshape(1, width))


def rms_bwd(name, x, g, dn, width, col_blk=0, dres=None, tm=256, with_delta=False):
    T = x.shape[0]
    has_res = dres is not None

    def body(*refs):
        x_ref, g_ref, dn_ref = refs[:3]
        pos = 3
        res_ref = None
        if has_res:
            res_ref = refs[pos]
            pos += 1
        dx_ref, dg_ref = refs[pos], refs[pos + 1]
        delta_ref = refs[pos + 2] if with_delta else None
        i = pl.program_id(0)
        xv = x_ref[...].astype(F32)
        r = lax.rsqrt(jnp.mean(xv * xv, axis=-1, keepdims=True) + EPS)
        xh = xv * r
        d = dn_ref[...].astype(F32)
        gd = d * g_ref[...]
        dx = r * (gd - xh * jnp.mean(gd * xh, axis=-1, keepdims=True))
        if has_res:
            dx = dx + res_ref[...]
        dx_ref[...] = dx.astype(dx_ref.dtype)
        part = jnp.sum(d * xh, axis=0, keepdims=True)

        @pl.when(i == 0)
        def _():
            dg_ref[...] = part

        @pl.when(i > 0)
        def _():
            dg_ref[...] += part

        if with_delta:
            for h in range(width // 128):
                sl = slice(h * 128, (h + 1) * 128)
                s = jnp.sum(dx[:, sl] * xv[:, sl], axis=-1, keepdims=True)
                delta_ref[:, sl] = jnp.broadcast_to(s, (tm, 128))

    in_specs = [pl.BlockSpec((tm, width), lambda i: (i, col_blk)), pl.BlockSpec((1, width), lambda i: (0, 0)),
                pl.BlockSpec((tm, width), lambda i: (i, 0))]
    args = [x, g.reshape(1, width), dn]
    if has_res:
        in_specs.append(pl.BlockSpec((tm, width), lambda i: (i, 0)))
        args.append(dres)
    out_specs = [pl.BlockSpec((tm, width), lambda i: (i, 0)), pl.BlockSpec((1, width), lambda i: (0, 0))]
    out_shape = [jax.ShapeDtypeStruct((T, width), F32), jax.ShapeDtypeStruct((1, width), F32)]
    if with_delta:
        out_specs.append(pl.BlockSpec((tm, width), lambda i: (i, 0)))
        out_shape.append(jax.ShapeDtypeStruct((T, width), F32))
    return pl.pallas_call(
        body, name=name, grid=(T // tm,), in_specs=in_specs, out_specs=out_specs, out_shape=out_shape,
        compiler_params=_params(("arbitrary",)),
    )(*args)


def ffn_fwd(tag, h, g, w1g, w3g, w2g, pre):
    T = h.shape[0]
    n = rms_fwd(f"{tag}_rms", h, g, D_MODEL)
    tm, tn = 512, FF_PAD

    def up_epi(accs, xs):
        a1, a3 = accs
        return a1, a3, a1 * _sigmoid(a1) * a3

    a1, a3, s = matmul(
        f"{tag}_up", (T // tm, FF_P // tn, 1),
        [op_a(n, tm, D_MODEL)], [op_b_rows_t(w1g, pre, D_MODEL, tn), op_b_rows_t(w3g, pre, D_MODEL, tn)],
        [(0, 0, 0), (0, 1, 1)], 2, [],
        [out_mn(T, FF_P, tm, tn, BF16)] * 3, up_epi, (tm, tn), n_outer=True)

    tn2 = 1024
    (h_out,) = matmul(
        f"{tag}_down", (T // tm, D_MODEL // tn2, N_CHIPS),
        [op_a(s, tm, FF_PAD)], [op_b_rows(w2g, pre, FF_PAD, tn2)],
        [(0, 0, 0)], 1, [tile_mn(h, tm, tn2)],
        [out_mn(T, D_MODEL, tm, tn2, F32)], lambda accs, xs: (xs[0] + 0.5 * accs[0],), (tm, tn2))
    return h_out, (n, a1, a3, s)


def ffn_bwd(tag, dh_out, h, g, res, w1g, w3g, w2g, pre, dw_after=None):
    n, a1, a3, s = res
    T = h.shape[0]
    tm, tn = 512, FF_PAD

    def act_epi(accs, xs):
        ds = 0.5 * accs[0]
        x1, x3 = xs[0].astype(F32), xs[1].astype(F32)
        sg = _sigmoid(x1)
        silu = x1 * sg
        return ds * x3 * (sg + silu * (1.0 - sg)), ds * silu

    da1, da3 = matmul(
        f"{tag}_dact", (T // tm, FF_P // tn, 1),
        [op_a(dh_out, tm, D_MODEL)], [op_b_rows_t(w2g, pre, D_MODEL, tn)],
        [(0, 0, 0)], 1, [tile_mn(a1, tm, tn), tile_mn(a3, tm, tn)],
        [out_mn(T, FF_P, tm, tn, BF16)] * 2, act_epi, (tm, tn), n_outer=True)

    tk = 512

    def dw_t(nm, left, right, scale):
        (dw,) = matmul(
            f"{tag}_{nm}", (FF_P // FF_PAD, D_MODEL // 1024, T // tk),
            [op_at(left, FF_PAD, tk)], [op_b(right, tk, 1024)],
            [(0, 0, 0)], 1, [], [out_mn(FF_P, D_MODEL, FF_PAD, 1024, BF16)],
            lambda accs, xs: (scale * accs[0],), (FF_PAD, 1024), after=dw_after)
        return dw

    dw2 = dw_t("dw2", s, dh_out, 0.5)
    dw1 = dw_t("dw1", da1, n, 1.0)
    dw3 = dw_t("dw3", da3, n, 1.0)

    tn2 = 1024
    (dn,) = matmul(
        f"{tag}_dn", (T // tm, D_MODEL // tn2, N_CHIPS),
        [op_a(da1, tm, FF_PAD), op_a(da3, tm, FF_PAD)],
        [op_b_rows(w1g, pre, FF_PAD, tn2), op_b_rows(w3g, pre, FF_PAD, tn2)],
        [(0, 0, 0), (1, 1, 0)], 1, [], [out_mn(T, D_MODEL, tm, tn2, F32)], _acc0, (tm, tn2))
    dh, dg = rms_bwd(f"{tag}_rms_bwd", h, g, dn, D_MODEL, dres=dh_out)
    return dh, dg, dw1, dw3, dw2


def rope_tables(positions):
    inv_freq = ROPE_BASE ** (-jnp.arange(0, QK_ROPE, 2, dtype=F32) / QK_ROPE)
    ang = positions.astype(F32)[:, None] * inv_freq
    cos, sin = jnp.cos(ang), jnp.sin(ang)
    T = positions.shape[0]
    one, zero = jnp.ones((T, QK_NOPE), F32), jnp.zeros((T, 64), F32)
    z32, z128 = jnp.zeros((T, 32), F32), jnp.zeros((T, QK_NOPE), F32)
    c = jnp.concatenate([one, cos, cos, zero], axis=1)
    s1 = jnp.concatenate([z128, -sin, z32, zero], axis=1)
    s2 = jnp.concatenate([z128, z32, sin, zero], axis=1)
    return c, s1, s2


def _rope(y, c, s1, s2):
    return y * c + pltpu.roll(y, HEAD_PAD - 32, 1) * s1 + pltpu.roll(y, 32, 1) * s2


def _rope_t(d, c, s1, s2):
    return d * c + pltpu.roll(d * s1, 32, 1) + pltpu.roll(d * s2, HEAD_PAD - 32, 1)


def _head_norm(x):
    r = lax.rsqrt(jnp.sum(x * x, axis=-1, keepdims=True) * (1.0 / QK_DIM) + EPS)
    return x * r, r


def qk_prep_fwd(tag, q_raw, kk_raw, z_p, gq, gk, tabs, tm=256):
    T = q_raw.shape[0]
    c, s1, s2 = tabs

    def body(q_ref, k_ref, kr_ref, gq_ref, gk_ref, c_ref, s1_ref, s2_ref, qo_ref, ko_ref):
        cv, s1v, s2v = c_ref[...], s1_ref[...], s2_ref[...]
        kr = kr_ref[...]
        for h in range(HEADS):
            sl = slice(h * HEAD_PAD, (h + 1) * HEAD_PAD)
            xh, _ = _head_norm(q_ref[:, sl])
            qo_ref[:, sl] = (_rope(xh * gq_ref[...], cv, s1v, s2v) * ATTN_SCALE).astype(BF16)
            xh, _ = _head_norm(k_ref[:, sl] + kr)
            ko_ref[:, sl] = _rope(xh * gk_ref[...], cv, s1v, s2v).astype(BF16)

    row = lambda i: (i, 0)
    full = pl.BlockSpec((tm, HEADS * HEAD_PAD), row)
    tab = pl.BlockSpec((tm, HEAD_PAD), row)
    vec = pl.BlockSpec((1, HEAD_PAD), lambda i: (0, 0))
    return pl.pallas_call(
        body, name=f"{tag}_qk_prep", grid=(T // tm,),
        in_specs=[full, full, pl.BlockSpec((tm, HEAD_PAD), lambda i: (i, 3)), vec, vec, tab, tab, tab],
        out_specs=[full, full],
        out_shape=[jax.ShapeDtypeStruct((T, HEADS * HEAD_PAD), BF16)] * 2,
        compiler_params=_params(("parallel",)),
    )(q_raw, kk_raw, z_p, gq, gk, c, s1, s2)


def qk_prep_bwd(tag, dq_full, dk_full, q_raw, kk_raw, z_p, gq, gk, tabs, tm=256):
    T = q_raw.shape[0]
    c, s1, s2 = tabs

    def body(dq_ref, dk_ref, q_ref, k_ref, kr_ref, gq_ref, gk_ref, c_ref, s1_ref, s2_ref,
             dqr_ref, dkr_ref, dz_ref, dgq_ref, dgk_ref):
        i = pl.program_id(0)
        cv, s1v, s2v = c_ref[...], s1_ref[...], s2_ref[...]
        kr = kr_ref[...]
        lane = lax.broadcasted_iota(jnp.int32, (tm, HEAD_PAD), 1)
        slot = ((lane >= QK_NOPE) & (lane < QK_DIM)).astype(F32)

        def one(x, g, d):
            xh, r = _head_norm(x)
            dy = _rope_t(d, cv, s1v, s2v)
            gd = dy * g
            dx = r * (gd - xh * (jnp.sum(gd * xh, axis=-1, keepdims=True) * (1.0 / QK_DIM)))
            return dx, jnp.sum(dy * xh, axis=0, keepdims=True)

        dgq = jnp.zeros((1, HEAD_PAD), F32)
        dgk = jnp.zeros((1, HEAD_PAD), F32)
        dz = jnp.zeros((tm, HEAD_PAD), F32)
        for h in range(HEADS):
            sl = slice(h * HEAD_PAD, (h + 1) * HEAD_PAD)
            dx, dg = one(q_ref[:, sl], gq_ref[...], dq_ref[:, sl].astype(F32) * ATTN_SCALE)
            dqr_ref[:, sl] = dx
            dgq = dgq + dg
            dx, dg = one(k_ref[:, sl] + kr, gk_ref[...], dk_ref[:, sl].astype(F32))
            dkr_ref[:, sl] = dx
            dgk = dgk + dg
            dz = dz + dx
        dz_ref[...] = dz * slot

        @pl.when(i == 0)
        def _():
            dgq_ref[...] = dgq
            dgk_ref[...] = dgk

        @pl.when(i > 0)
        def _():
            dgq_ref[...] += dgq
            dgk_ref[...] += dgk

    row = lambda i: (i, 0)
    full = pl.BlockSpec((tm, HEADS * HEAD_PAD), row)
    tab = pl.BlockSpec((tm, HEAD_PAD), row)
    vec = pl.BlockSpec((1, HEAD_PAD), lambda i: (0, 0))
    return pl.pallas_call(
        body, name=f"{tag}_qk_prep_bwd", grid=(T // tm,),
        in_specs=[full, full, full, full, pl.BlockSpec((tm, HEAD_PAD), lambda i: (i, 3)), vec, vec, tab, tab, tab],
        out_specs=[full, full, tab, vec, vec],
        out_shape=[jax.ShapeDtypeStruct((T, HEADS * HEAD_PAD), F32)] * 2
        + [jax.ShapeDtypeStruct((T, HEAD_PAD), F32)] + [jax.ShapeDtypeStruct((1, HEAD_PAD), F32)] * 2,
        compiler_params=_params(("arbitrary",)),
    )(dq_full, dk_full, q_raw, kk_raw, z_p, gq, gk, c, s1, s2)


def attn_fwd(tag, q_full, k_full, vv, blk=512):
    T = q_full.shape[0]
    nb = T // blk
    neg = float(jnp.finfo(jnp.float32).min)

    def body(q_ref, k_ref, v_ref, o_ref, lse_ref, m_ref, l_ref, acc_ref):
        i, j = pl.program_id(1), pl.program_id(2)

        @pl.when(j == 0)
        def _():
            m_ref[...] = jnp.full_like(m_ref, neg)
            l_ref[...] = jnp.zeros_like(l_ref)
            acc_ref[...] = jnp.zeros_like(acc_ref)

        def step(masked):
            s = lax.dot_general(q_ref[...], k_ref[...], (((1,), (1,)), ((), ())), preferred_element_type=F32)
            if masked:
                row = lax.broadcasted_iota(jnp.int32, (blk, blk), 0)
                col = lax.broadcasted_iota(jnp.int32, (blk, blk), 1)
                s = jnp.where(col <= row, s, neg)
            m_prev = m_ref[...]
            m_new = jnp.maximum(m_prev, jnp.max(s, axis=-1, keepdims=True))
            alpha = jnp.exp(m_prev - m_new)
            p = jnp.exp(s - m_new[:, :1])
            l_ref[...] = alpha * l_ref[...] + jnp.sum(p, axis=-1, keepdims=True)
            acc_ref[...] = alpha * acc_ref[...] + jnp.dot(p.astype(BF16), v_ref[...], preferred_element_type=F32)
            m_ref[...] = m_new

        @pl.when(j < i)
        def _():
            step(False)

        @pl.when(j == i)
        def _():
            step(True)
            o_ref[...] = acc_ref[...] / l_ref[...]
            lse_ref[...] = m_ref[...] + jnp.log(l_ref[...])

    kv_ix = lambda h, i, j: (jnp.minimum(j, i), h)
    return pl.pallas_call(
        body, name=f"{tag}_attn_fwd", grid=(HEADS, nb, nb),
        in_specs=[pl.BlockSpec((blk, HEAD_PAD), lambda h, i, j: (i, h)),
                  pl.BlockSpec((blk, HEAD_PAD), kv_ix), pl.BlockSpec((blk, V_DIM), kv_ix)],
        out_specs=[pl.BlockSpec((blk, V_DIM), lambda h, i, j: (i, h))] * 2,
        out_shape=[jax.ShapeDtypeStruct((T, ATTN_W), F32)] * 2,
        scratch_shapes=[pltpu.VMEM((blk, V_DIM), F32)] * 3,
        compiler_params=_params(("parallel", "parallel", "arbitrary")),
    )(q_full, k_full, vv)


def attn_bwd(tag, q_full, k_full, vv, do, lse, delta, blk=512):
    T = q_full.shape[0]
    nb = T // blk
    neg = float(jnp.finfo(jnp.float32).min)

    def body(q_ref, k_ref, v_ref, do_ref, lse_ref, dl_ref, dq_ref, dk_ref, dv_ref, dk_acc, dv_acc):
        j, i = pl.program_id(1), pl.program_id(2)

        @pl.when((j == 0) & (i == 0))
        def _():
            dq_ref[...] = jnp.zeros_like(dq_ref)

        @pl.when(i == 0)
        def _():
            dk_acc[...] = jnp.zeros_like(dk_acc)
            dv_acc[...] = jnp.zeros_like(dv_acc)

        def step(masked):
            q, k = q_ref[...], k_ref[...]
            s = lax.dot_general(q, k, (((1,), (1,)), ((), ())), preferred_element_type=F32)
            if masked:
                row = lax.broadcasted_iota(jnp.int32, (blk, blk), 0)
                col = lax.broadcasted_iota(jnp.int32, (blk, blk), 1)
                s = jnp.where(col <= row, s, neg)
            p = jnp.exp(s - lse_ref[:, :1])
            dob = _bf(do_ref[...])
            dv_acc[...] += lax.dot_general(p.astype(BF16), dob, (((0,), (0,)), ((), ())), preferred_element_type=F32)
            dp = lax.dot_general(dob, v_ref[...], (((1,), (1,)), ((), ())), preferred_element_type=F32)
            ds = (p * (dp - dl_ref[:, :1])).astype(BF16)
            dk_acc[...] += lax.dot_general(ds, q, (((0,), (0,)), ((), ())), preferred_element_type=F32)
            rows = pl.ds(pl.multiple_of(i * blk, blk), blk)
            dq_ref[rows, :] += jnp.dot(ds, k, preferred_element_type=F32)

        @pl.when(i > j)
        def _():
            step(False)

        @pl.when(i == j)
        def _():
            step(True)

        @pl.when(i == nb - 1)
        def _():
            dk_ref[...] = dk_acc[...]
            dv_ref[...] = dv_acc[...]

    q_ix = lambda h, j, i: (jnp.maximum(i, j), h)
    kv_ix = lambda h, j, i: (j, h)
    return pl.pallas_call(
        body, name=f"{tag}_attn_bwd", grid=(HEADS, nb, nb),
        in_specs=[pl.BlockSpec((blk, HEAD_PAD), q_ix), pl.BlockSpec((blk, HEAD_PAD), kv_ix),
                  pl.BlockSpec((blk, V_DIM), kv_ix), pl.BlockSpec((blk, V_DIM), q_ix),
                  pl.BlockSpec((blk, V_DIM), q_ix), pl.BlockSpec((blk, V_DIM), q_ix)],
        out_specs=[pl.BlockSpec((T, HEAD_PAD), lambda h, j, i: (0, h)),
                   pl.BlockSpec((blk, HEAD_PAD), kv_ix), pl.BlockSpec((blk, V_DIM), kv_ix)],
        out_shape=[jax.ShapeDtypeStruct((T, HEADS * HEAD_PAD), F32)] * 2 + [jax.ShapeDtypeStruct((T, ATTN_W), F32)],
        scratch_shapes=[pltpu.VMEM((blk, HEAD_PAD), F32), pltpu.VMEM((blk, V_DIM), F32)],
        compiler_params=_params(("parallel", "arbitrary", "arbitrary")),
    )(q_full, k_full, vv, do, lse, delta)


def _gm_forward(u, v, gv, wc_ref, bb_ref, nchunk):
    ug = _gelu(u)
    vg = _gelu(v)
    rv = lax.rsqrt(jnp.mean(vg * vg, axis=-1, keepdims=True) + EPS)
    vhat = vg * rv
    vn = (vhat * gv).astype(BF16)
    gates = []
    for cidx in range(nchunk):
        rows = slice(cidx * CHUNK, (cidx + 1) * CHUNK)
        gates.append(jnp.concatenate(
            [jnp.dot(wc_ref[gidx], vn[rows, gidx * 128:(gidx + 1) * 128], preferred_element_type=F32) + bb_ref[gidx]
             for gidx in range(GROUPS)], axis=1))
    gate = jnp.concatenate(gates, axis=0)
    return ug, vhat, rv, vn, gate


def gmlp_fwd(tag, z_p, gv, gout, wc, bb, tm=256):
    T = z_p.shape[0]
    nchunk = tm // CHUNK

    def body(u_ref, v_ref, gv_ref, go_ref, wc_ref, bb_ref, o_ref):
        ug, _, _, _, gate = _gm_forward(u_ref[...], v_ref[...], gv_ref[...], wc_ref, bb_ref, nchunk)
        go = ug * gate
        ro = lax.rsqrt(jnp.mean(go * go, axis=-1, keepdims=True) + EPS)
        o_ref[...] = (go * ro * go_ref[...]).astype(BF16)

    vec = pl.BlockSpec((1, GM_W), lambda i: (0, 0))
    w3 = pl.BlockSpec((GROUPS, CHUNK, CHUNK), lambda i: (0, 0, 0))
    return pl.pallas_call(
        body, name=f"{tag}_gmlp_fwd", grid=(T // tm,),
        in_specs=[pl.BlockSpec((tm, GM_W), lambda i: (i, 1)), pl.BlockSpec((tm, GM_W), lambda i: (i, 2)), vec, vec, w3, w3],
        out_specs=pl.BlockSpec((tm, GM_W), lambda i: (i, 0)),
        out_shape=jax.ShapeDtypeStruct((T, GM_W), BF16),
        compiler_params=_params(("parallel",)),
    )(z_p, z_p, gv.reshape(1, GM_W), gout.reshape(1, GM_W), wc, bb)


def gmlp_bwd(tag, z_p, dmixed, gv, gout, wc, bb, tm=256):
    T = z_p.shape[0]
    nchunk = tm // CHUNK

    def body(u_ref, v_ref, dm_ref, gv_ref, go_ref, wc_ref, bb_ref, du_ref, dv_ref, dwc_ref, dbb_ref, dgv_ref, dgo_ref):
        i = pl.program_id(0)
        u, v = u_ref[...], v_ref[...]
        ug, vhat, rv, vn, gate = _gm_forward(u, v, gv_ref[...], wc_ref, bb_ref, nchunk)
        go = ug * gate
        ro = lax.rsqrt(jnp.mean(go * go, axis=-1, keepdims=True) + EPS)
        ohat = go * ro
        dm = dm_ref[...].astype(F32)
        dgo_part = jnp.sum(dm * ohat, axis=0, keepdims=True)
        doh = dm * go_ref[...]
        dgo = ro * (doh - ohat * jnp.mean(doh * ohat, axis=-1, keepdims=True))
        du_ref[...] = dgo * gate * _gelu_grad(u)
        dgate = dgo * ug
        dgb = dgate.astype(BF16)
        dvn_rows = []
        dwc_parts = []
        dbb_parts = []
        for gidx in range(GROUPS):
            cols = slice(gidx * 128, (gidx + 1) * 128)
            dw = jnp.zeros((CHUNK, CHUNK), F32)
            db = jnp.zeros((CHUNK, 128), F32)
            for cidx in range(nchunk):
                rows = slice(cidx * CHUNK, (cidx + 1) * CHUNK)
                dw = dw + lax.dot_general(dgb[rows, cols], vn[rows, cols], (((1,), (1,)), ((), ())),
                                          preferred_element_type=F32)
                db = db + dgate[rows, cols]
            dwc_parts.append(dw)
            dbb_parts.append(db)
        for cidx in range(nchunk):
            rows = slice(cidx * CHUNK, (cidx + 1) * CHUNK)
            dvn_rows.append(jnp.concatenate(
                [lax.dot_general(wc_ref[gidx], dgb[rows, gidx * 128:(gidx + 1) * 128], (((0,), (0,)), ((), ())),
                                 preferred_element_type=F32) for gidx in range(GROUPS)], axis=1))
        dvn = jnp.concatenate(dvn_rows, axis=0)
        dgv_part = jnp.sum(dvn * vhat, axis=0, keepdims=True)
        dvh = dvn * gv_ref[...]
        dvg = rv * (dvh - vhat * jnp.mean(dvh * vhat, axis=-1, keepdims=True))
        dv_ref[...] = dvg * _gelu_grad(v)

        @pl.when(i == 0)
        def _():
            for gidx in range(GROUPS):
                dwc_ref[gidx] = dwc_parts[gidx]
                dbb_ref[gidx] = dbb_parts[gidx]
            dgv_ref[...] = dgv_part
            dgo_ref[...] = dgo_part

        @pl.when(i > 0)
        def _():
            for gidx in range(GROUPS):
                dwc_ref[gidx] += dwc_parts[gidx]
                dbb_ref[gidx] += dbb_parts[gidx]
            dgv_ref[...] += dgv_part
            dgo_ref[...] += dgo_part

    vec = pl.BlockSpec((1, GM_W), lambda i: (0, 0))
    w3 = pl.BlockSpec((GROUPS, CHUNK, CHUNK), lambda i: (0, 0, 0))
    blk = pl.BlockSpec((tm, GM_W), lambda i: (i, 0))
    return pl.pallas_call(
        body, name=f"{tag}_gmlp_bwd", grid=(T // tm,),
        in_specs=[pl.BlockSpec((tm, GM_W), lambda i: (i, 1)), pl.BlockSpec((tm, GM_W), lambda i: (i, 2)),
                  pl.BlockSpec((tm, GM_W), lambda i: (i, 1)), vec, vec, w3, w3],
        out_specs=[blk, blk, w3, w3, vec, vec],
        out_shape=[jax.ShapeDtypeStruct((T, GM_W), F32)] * 2 + [jax.ShapeDtypeStruct((GROUPS, CHUNK, CHUNK), F32)] * 2
        + [jax.ShapeDtypeStruct((1, GM_W), F32)] * 2,
        compiler_params=_params(("arbitrary",)),
    )(z_p, z_p, dmixed, gv.reshape(1, GM_W), gout.reshape(1, GM_W), wc, bb)


def mixer_fwd(tag, h, w, tabs, wout_g, pre):
    T = h.shape[0]
    n2 = rms_fwd(f"{tag}_mix_rms", h, w["mix_norm"], D_MODEL)
    (z_p,) = mm_simple(f"{tag}_win", n2, lambda tk, tn: op_bt(w["w_in_pt"], tk, tn), T, IN_P, D_MODEL, 512, 1024, D_MODEL)
    cqn = rms_fwd(f"{tag}_cq_rms", z_p, w["q_a_norm"], Q_RANK, col_blk=0)
    ckvn = rms_fwd(f"{tag}_ckv_rms", z_p, w["kv_a_norm"], KV_RANK, col_blk=2)
    (q_raw,) = mm_simple(f"{tag}_wq", cqn, lambda tk, tn: op_b(w["wq_p"], tk, tn), T, 2048, Q_RANK, 512, 1024, Q_RANK)
    (kk_raw,) = mm_simple(f"{tag}_wk", ckvn, lambda tk, tn: op_b(w["wk_p"], tk, tn), T, 2048, KV_RANK, 512, 1024, KV_RANK)
    (vv,) = mm_simple(f"{tag}_wv", ckvn, lambda tk, tn: op_b(w["wv"], tk, tn), T, ATTN_W, KV_RANK, 512, 1024, KV_RANK,
                      out_dtype=BF16)
    q_full, k_full = qk_prep_fwd(tag, q_raw, kk_raw, z_p, w["gq_p"], w["gk_p"], tabs)
    a_out, lse = attn_fwd(tag, q_full, k_full, vv)
    mixed_a = rms_fwd(f"{tag}_ao_rms", a_out, w["attn_out_norm"], ATTN_W)
    mixed_g = gmlp_fwd(tag, z_p, w["gm_v_norm"], w["gm_out_norm"], w["wc"], w["bb"])
    tm, tn, tk = 512, 1024, 512
    (h2,) = matmul(
        f"{tag}_wout", (T // tm, D_MODEL // tn, ATTN_W // tk),
        [op_a(mixed_a, tm, tk), op_a(mixed_g, tm, tk)],
        [op_b_rows(wout_g, pre, tk, tn), op_b_rows(wout_g, pre, tk, tn, koff=ATTN_W // tk)],
        [(0, 0, 0), (1, 1, 0)], 1, [tile_mn(h, tm, tn)], [out_mn(T, D_MODEL, tm, tn, F32)],
        lambda accs, xs: (xs[0] + accs[0],), (tm, tn))
    res = dict(n2=n2, z_p=z_p, cqn=cqn, ckvn=ckvn, q_raw=q_raw, kk_raw=kk_raw, vv=vv, q_full=q_full, k_full=k_full,
               a_out=a_out, lse=lse, mixed_a=mixed_a, mixed_g=mixed_g)
    return h2, res


def mixer_bwd(tag, dh2, h, w, tabs, wout_g, pre, r):
    T = h.shape[0]
    g = {}
    (dmixed,) = mm_simple(f"{tag}_dmixed", dh2, lambda tk, tn: op_b_rows_t(wout_g, pre, tk, tn), T, D_MODEL, D_MODEL,
                          512, 512, D_MODEL)
    (dwo_a,) = mm_simple(f"{tag}_dwout_a", r["mixed_a"], lambda tk, tn: op_b(dh2, tk, tn), ATTN_W, D_MODEL, T,
                         1024, 1024, 512, a_t=True, out_dtype=BF16)
    (dwo_g,) = mm_simple(f"{tag}_dwout_g", r["mixed_g"], lambda tk, tn: op_b(dh2, tk, tn), GM_W, D_MODEL, T,
                         1024, 1024, 512, a_t=True, out_dtype=BF16)
    g["w_out"] = jnp.concatenate([dwo_a, dwo_g], axis=0)
    da_out, g["attn_out_norm"], delta = rms_bwd(f"{tag}_ao_rms_bwd", r["a_out"], w["attn_out_norm"], dmixed, ATTN_W,
                                                with_delta=True)
    dq_full, dk_full, dvv = attn_bwd(tag, r["q_full"], r["k_full"], r["vv"], da_out, r["lse"], delta)
    dq_raw, dkk_raw, dzkr, g["gq_p"], g["gk_p"] = qk_prep_bwd(tag, dq_full, dk_full, r["q_raw"], r["kk_raw"], r["z_p"],
                                                            w["gq_p"], w["gk_p"], tabs)
    (g["wq_p"],) = mm_simple(f"{tag}_dwq", r["cqn"], lambda tk, tn: op_b(dq_raw, tk, tn), Q_RANK, 2048, T, Q_RANK, 1024, 512,
                             a_t=True, out_dtype=BF16)
    (g["wk_p"],) = mm_simple(f"{tag}_dwk", r["ckvn"], lambda tk, tn: op_b(dkk_raw, tk, tn), KV_RANK, 2048, T, KV_RANK, 1024,
                             512, a_t=True, out_dtype=BF16)
    (g["wv"],) = mm_simple(f"{tag}_dwv", r["ckvn"], lambda tk, tn: op_b(dvv, tk, tn), KV_RANK, ATTN_W, T, KV_RANK, 1024, 512,
                           a_t=True, out_dtype=BF16)
    (dcqn,) = mm_simple(f"{tag}_dcqn", dq_raw, lambda tk, tn: op_bt(w["wq_p"], tk, tn), T, Q_RANK, 2048, 512, Q_RANK, 2048)
    (dck1,) = mm_simple(f"{tag}_dckvn_k", dkk_raw, lambda tk, tn: op_bt(w["wk_p"], tk, tn), T, KV_RANK, 2048, 512, KV_RANK,
                        2048)
    (dckvn,) = mm_simple(f"{tag}_dckvn_v", dvv, lambda tk, tn: op_bt(w["wv"], tk, tn), T, KV_RANK, ATTN_W, 512, KV_RANK,
                         ATTN_W, extras=[tile_mn(dck1, 512, KV_RANK)], epilogue=lambda accs, xs: (accs[0] + xs[0],))
    dc_q, g["q_a_norm"] = rms_bwd(f"{tag}_cq_rms_bwd", r["z_p"], w["q_a_norm"], dcqn, Q_RANK, col_blk=0)
    dc_kv, g["kv_a_norm"] = rms_bwd(f"{tag}_ckv_rms_bwd", r["z_p"], w["kv_a_norm"], dckvn, KV_RANK, col_blk=2)
    du, dv, g["wc"], g["bb"], g["gm_v_norm"], g["gm_out_norm"] = gmlp_bwd(
        tag, r["z_p"], dmixed, w["gm_v_norm"], w["gm_out_norm"], w["wc"], w["bb"])
    dz_p = jnp.concatenate([dc_q, dc_kv, dzkr, du, dv], axis=1).astype(BF16)
    (g["w_in_pt"],) = mm_simple(f"{tag}_dwin", dz_p, lambda tk, tn: op_b(r["n2"], tk, tn), IN_P, D_MODEL, T, 1024, 1024, 512,
                                a_t=True, out_dtype=BF16)
    (dn2,) = mm_simple(f"{tag}_dn2", dz_p, lambda tk, tn: op_b(w["w_in_pt"], tk, tn), T, D_MODEL, IN_P, 512, 1024, IN_P)
    dh1, g["mix_norm"] = rms_bwd(f"{tag}_mix_rms_bwd", h, w["mix_norm"], dn2, D_MODEL, dres=dh2)
    return dh1, g


def ple_fwd(tag, h3, p_l, w, wpg_g, wple_g, pre):
    T = h3.shape[0]
    (pw,) = mm_simple(f"{tag}_wple", p_l, lambda tk, tn: op_b_cols(wple_g, pre, tk, tn), T, D_MODEL, PLE_DIM, 512, 512,
                      PLE_DIM)
    e = rms_fwd(f"{tag}_ple_rms", pw, w["ple_norm"], D_MODEL, out_dtype=F32)
    n4 = rms_fwd(f"{tag}_pg_rms", h3, w["ple_gate_norm"], D_MODEL)

    def epi(accs, xs):
        gt = _sigmoid(accs[0])
        return xs[0] + gt * xs[1], gt

    tm, tn, tk = 512, 1024, 512
    h4, gate = matmul(
        f"{tag}_wpg", (T // tm, D_MODEL // tn, D_MODEL // tk),
        [op_a(n4, tm, tk)], [op_b_rows(wpg_g, pre, tk, tn)], [(0, 0, 0)], 1,
        [tile_mn(h3, tm, tn), tile_mn(e, tm, tn)],
        [out_mn(T, D_MODEL, tm, tn, F32), out_mn(T, D_MODEL, tm, tn, BF16)], epi, (tm, tn))
    return h4, dict(pw=pw, e=e, n4=n4, gate=gate)


def ple_bwd(tag, dh4, h3, p_l, w, wpg_g, wple_g, pre, r, tm=256):
    T = h3.shape[0]

    def act_body(d_ref, g_ref, e_ref, dpre_ref, de_ref):
        d, gt = d_ref[...], g_ref[...].astype(F32)
        dpre_ref[...] = (d * e_ref[...] * gt * (1.0 - gt)).astype(BF16)
        de_ref[...] = d * gt

    blk = pl.BlockSpec((tm, D_MODEL), lambda i: (i, 0))
    dpre, de = pl.pallas_call(
        act_body, name=f"{tag}_ple_act_bwd", grid=(T // tm,), in_specs=[blk, blk, blk], out_specs=[blk, blk],
        out_shape=[jax.ShapeDtypeStruct((T, D_MODEL), BF16), jax.ShapeDtypeStruct((T, D_MODEL), F32)],
        compiler_params=_params(("parallel",)),
    )(dh4, r["gate"], r["e"])
    g = {}
    (g["w_ple_gate"],) = mm_simple(f"{tag}_dwpg", r["n4"], lambda tk, tn: op_b(dpre, tk, tn), D_MODEL, D_MODEL, T,
                                   1024, 1024, 512, a_t=True, out_dtype=BF16)
    (dn4,) = mm_simple(f"{tag}_dn4", dpre, lambda tk, tn: op_b_rows_t(wpg_g, pre, tk, tn), T, D_MODEL, D_MODEL, 512, 512,
                       D_MODEL)
    dh3, g["ple_gate_norm"] = rms_bwd(f"{tag}_pg_rms_bwd", h3, w["ple_gate_norm"], dn4, D_MODEL, dres=dh4)
    dpw, g["ple_norm"] = rms_bwd(f"{tag}_ple_rms_bwd", r["pw"], w["ple_norm"], de, D_MODEL)
    (g["w_ple"],) = mm_simple(f"{tag}_dwple", p_l, lambda tk, tn: op_b(dpw, tk, tn), PLE_DIM, D_MODEL, T, PLE_DIM, 512, 512,
                              a_t=True, outs=[out_cols(PLE_DIM, 512, PLE_DIM, 512, BF16)])
    return dh3, g


def loss_grad(y, target, tm=256):
    T = y.shape[0]

    def body(y_ref, t_ref, dy_ref, l_ref):
        i = pl.program_id(0)
        d = y_ref[...] - t_ref[...]
        dy_ref[...] = d * (1.0 / D_MODEL)
        part = jnp.sum((d * d).reshape(tm // 8, 8, D_MODEL), axis=0)

        @pl.when(i == 0)
        def _():
            l_ref[...] = part

        @pl.when(i > 0)
        def _():
            l_ref[...] += part

    blk = pl.BlockSpec((tm, D_MODEL), lambda i: (i, 0))
    dy, part = pl.pallas_call(
        body, name="loss_grad", grid=(T // tm,), in_specs=[blk, blk],
        out_specs=[blk, pl.BlockSpec((8, D_MODEL), lambda i: (0, 0))],
        out_shape=[jax.ShapeDtypeStruct((T, D_MODEL), F32), jax.ShapeDtypeStruct((8, D_MODEL), F32)],
        compiler_params=_params(("arbitrary",)),
    )(y, target)
    return dy, 0.5 * jnp.sum(part) / D_MODEL


def _unshard_cols(g_l):
    return g_l.transpose(1, 0, 2).reshape(g_l.shape[1], -1)


def _shard_cols(w):
    return w.reshape(w.shape[0], N_CHIPS, -1).transpose(1, 0, 2)


def layer_weights(l, Gl, small):
    w = {k: small[k][l] for k in ("mix_norm", "q_a_norm", "kv_a_norm", "gm_v_norm", "attn_out_norm", "gm_out_norm",
                                  "ple_gate_norm", "ple_norm")}
    wint = Gl["w_in"][:, :IN_SHARD].reshape(-1, D_MODEL)
    z = lambda n: jnp.zeros((n, D_MODEL), BF16)
    w["w_in_pt"] = jnp.concatenate([wint[:768], z(128), wint[768:832], z(64), wint[832:]], axis=0)
    wuq = _unshard_cols(Gl["w_uq"]).reshape(Q_RANK, HEADS, QK_DIM)
    w["wq_p"] = jnp.pad(wuq, ((0, 0), (0, 0), (0, HEAD_PAD - QK_DIM))).reshape(Q_RANK, HEADS * HEAD_PAD)
    wukv = _unshard_cols(Gl["w_ukv"]).reshape(KV_RANK, HEADS, QK_NOPE + V_DIM)
    w["wk_p"] = jnp.pad(wukv[:, :, :QK_NOPE], ((0, 0), (0, 0), (0, HEAD_PAD - QK_NOPE))).reshape(KV_RANK, HEADS * HEAD_PAD)
    w["wv"] = wukv[:, :, QK_NOPE:].reshape(KV_RANK, ATTN_W)
    w["gq_p"] = jnp.pad(small["q_norm"][l], (0, HEAD_PAD - QK_DIM)).reshape(1, HEAD_PAD)
    w["gk_p"] = jnp.pad(small["k_norm"][l], (0, HEAD_PAD - QK_DIM)).reshape(1, HEAD_PAD)
    tril = jnp.tril(jnp.ones((CHUNK, CHUNK), dtype=bool))
    w["wc"] = jnp.where(tril[None], small["gm_ws"][l], 0.0).astype(BF16)
    w["bb"] = jnp.broadcast_to(small["gm_bs"][l][:, :, None], (GROUPS, CHUNK, 128)).astype(F32)
    return w


def mixer_grads_to_shards(g):
    out = {}
    dwint = g["w_in_pt"]
    dwint = jnp.concatenate([dwint[:768], dwint[896:960], dwint[1024:]], axis=0).reshape(N_CHIPS, IN_SHARD, D_MODEL)
    out["w_in"] = jnp.pad(dwint, ((0, 0), (0, IN_SHARD_PAD - IN_SHARD), (0, 0)))
    dwuq = g["wq_p"].reshape(Q_RANK, HEADS, HEAD_PAD)[:, :, :QK_DIM].reshape(Q_RANK, HEADS * QK_DIM)
    out["w_uq"] = _shard_cols(dwuq)
    dwukv = jnp.concatenate([g["wk_p"].reshape(KV_RANK, HEADS, HEAD_PAD)[:, :, :QK_NOPE],
                             g["wv"].reshape(KV_RANK, HEADS, V_DIM)], axis=-1).reshape(KV_RANK, HEADS * (QK_NOPE + V_DIM))
    out["w_ukv"] = _shard_cols(dwukv)
    out["w_out"] = g["w_out"].reshape(N_CHIPS, D_MODEL // N_CHIPS, D_MODEL)
    out["q_norm"] = g["gq_p"][0, :QK_DIM]
    out["k_norm"] = g["gk_p"][0, :QK_DIM]
    tril = jnp.tril(jnp.ones((CHUNK, CHUNK), dtype=bool))
    out["gm_ws"] = jnp.where(tril[None], g["wc"], 0.0)
    out["gm_bs"] = jnp.sum(g["bb"], axis=-1)
    for k in ("mix_norm", "q_a_norm", "kv_a_norm", "gm_v_norm", "attn_out_norm", "gm_out_norm"):
        out[k] = g[k][0]
    return out


def layer_fwd(l, h, p_l, Gl, small, tabs):
    w = layer_weights(l, Gl, small)
    h1, r_a = ffn_fwd(f"l{l}a", h, small["ffn_a_norm"][l], Gl["ffn_a_w1"], Gl["ffn_a_w3"], Gl["ffn_a_w2"], ())
    h2, r_m = mixer_fwd(f"l{l}", h1, w, tabs, Gl["w_out"], ())
    h3, r_b = ffn_fwd(f"l{l}b", h2, small["ffn_b_norm"][l], Gl["ffn_b_w1"], Gl["ffn_b_w3"], Gl["ffn_b_w2"], ())
    h4, r_p = ple_fwd(f"l{l}", h3, p_l, w, Gl["w_ple_gate"], Gl["w_ple"], ())
    return h4, (w, h, h1, h2, h3, r_a, r_m, r_b, r_p)


def layer_bwd(l, dh, p_l, Gl, small, tabs, saved, before_ffn_a=None):
    w, h0, h1, h2, h3, r_a, r_m, r_b, r_p = saved
    slabs = lambda d: d.reshape(N_CHIPS, FF_PAD, D_MODEL)
    gl = {}
    dh, g_p = ple_bwd(f"l{l}", dh, h3, p_l, w, Gl["w_ple_gate"], Gl["w_ple"], (), r_p)
    gl["w_ple_gate"] = g_p["w_ple_gate"].reshape(N_CHIPS, D_MODEL // N_CHIPS, D_MODEL)
    gl["w_ple"] = g_p["w_ple"]
    gl["ple_gate_norm"], gl["ple_norm"] = g_p["ple_gate_norm"][0], g_p["ple_norm"][0]
    dh, dg, dw1, dw3, dw2 = ffn_bwd(f"l{l}b", dh, h2, small["ffn_b_norm"][l], r_b,
                                    Gl["ffn_b_w1"], Gl["ffn_b_w3"], Gl["ffn_b_w2"], ())
    gl["ffn_b_norm"] = dg[0]
    gl["ffn_b_w1"], gl["ffn_b_w3"], gl["ffn_b_w2"] = slabs(dw1), slabs(dw3), slabs(dw2)
    dh, g_m = mixer_bwd(f"l{l}", dh, h1, w, tabs, Gl["w_out"], (), r_m)
    gl.update(mixer_grads_to_shards(g_m))
    dw_after = None if before_ffn_a is None else before_ffn_a(gl)
    dh, dg, dw1, dw3, dw2 = ffn_bwd(f"l{l}a", dh, h0, small["ffn_a_norm"][l], r_a,
                                    Gl["ffn_a_w1"], Gl["ffn_a_w3"], Gl["ffn_a_w2"], (), dw_after=dw_after)
    gl["ffn_a_norm"] = dg[0]
    gl["ffn_a_w1"], gl["ffn_a_w3"], gl["ffn_a_w2"] = slabs(dw1), slabs(dw3), slabs(dw2)
    return dh, gl


MESH = pl.DeviceIdType.MESH
HBM_SPEC = pl.BlockSpec(memory_space=pltpu.HBM)


def _place():
    x, y, c = lax.axis_index("x"), lax.axis_index("y"), lax.axis_index("c")
    others = [(1 - x, y), (x, 1 - y), (1 - x, 1 - y)]
    return x, y, c, 2 * x + y, others


def prep_shard(name, w, layer, rows_pad, place):
    _, ks, n = w.shape
    ksp = ks + rows_pad
    tc = 512 if n % 512 == 0 else n

    def body(place_ref, x_ref, o_ref):
        o_ref[:ks] = x_ref[...].astype(BF16)
        if rows_pad:
            o_ref[ks:] = jnp.zeros((rows_pad, tc), BF16)

    return pl.pallas_call(
        body, name=name,
        grid_spec=pltpu.PrefetchScalarGridSpec(
            num_scalar_prefetch=1, grid=(n // tc,),
            in_specs=[pl.BlockSpec((None, ks, tc), lambda i, s: (layer, 0, i))],
            out_specs=pl.BlockSpec((None, ksp, tc), lambda i, s: (s[0], 0, i))),
        out_shape=jax.ShapeDtypeStruct((N_CHIPS, ksp, n), BF16),
        compiler_params=_params(("parallel",)),
    )(place, w)


def gather_weights(name, slots):
    n = len(slots)

    def body(*refs):
        g_refs = refs[n:2 * n]
        ici_send, ici_recv, d2d_send, d2d_recv = refs[2 * n:]
        x, y, c, jme, others = _place()
        sib = (x, y, 1 - c)

        def half(w, j):
            kh = slots[w].shape[1] // 2
            return g_refs[w].at[j, pl.ds(c * kh, kh)]

        def three(w):
            return g_refs[w].at[pl.ds(0, 3), pl.ds(0, slots[w].shape[1] // 2)]

        for w in range(n):
            for (px, py) in others:
                pltpu.make_async_remote_copy(
                    src_ref=half(w, jme), dst_ref=half(w, jme), send_sem=ici_send.at[w], recv_sem=ici_recv.at[w],
                    device_id=(px, py, c), device_id_type=MESH).start()
        for w in range(n):
            pltpu.make_async_remote_copy(src_ref=three(w), dst_ref=three(w), send_sem=ici_send.at[w],
                                         recv_sem=ici_recv.at[w], device_id=sib, device_id_type=MESH).wait_recv()
            for (px, py) in others:
                blk = half(w, 2 * px + py)
                pltpu.make_async_remote_copy(src_ref=blk, dst_ref=blk, send_sem=d2d_send.at[w], recv_sem=d2d_recv.at[w],
                                             device_id=sib, device_id_type=MESH).start()
        for w in range(n):
            wait3 = pltpu.make_async_remote_copy(src_ref=three(w), dst_ref=three(w), send_sem=d2d_send.at[w],
                                                 recv_sem=d2d_recv.at[w], device_id=sib, device_id_type=MESH)
            wait3.wait_recv()
            wait3.wait_send()
            pltpu.make_async_remote_copy(src_ref=three(w), dst_ref=three(w), send_sem=ici_send.at[w],
                                         recv_sem=ici_recv.at[w], device_id=sib, device_id_type=MESH).wait_send()

    return pl.pallas_call(
        body, name=name,
        in_specs=[HBM_SPEC] * n, out_specs=[HBM_SPEC] * n,
        out_shape=[jax.ShapeDtypeStruct(s.shape, s.dtype) for s in slots],
        input_output_aliases={w: w for w in range(n)},
        scratch_shapes=[pltpu.SemaphoreType.DMA((n,))] * 4,
    )(*slots)


def exchange_halves(name, grads):
    n = len(grads)

    def body(*refs):
        d_refs, r_refs = refs[:n], refs[n:2 * n]
        send, recv = refs[2 * n:]
        x, y, c, _, _ = _place()
        cps = []
        for w in range(n):
            half = grads[w].shape[1] // 2
            cps.append(pltpu.make_async_remote_copy(
                src_ref=d_refs[w].at[pl.ds(0, N_CHIPS), pl.ds((1 - c) * half, half)], dst_ref=r_refs[w],
                send_sem=send.at[w], recv_sem=recv.at[w], device_id=(x, y, 1 - c), device_id_type=MESH))
        for cp in cps:
            cp.start()
        for cp in cps:
            cp.wait()

    return pl.pallas_call(
        body, name=name, in_specs=[HBM_SPEC] * n, out_specs=[HBM_SPEC] * n,
        out_shape=[jax.ShapeDtypeStruct((N_CHIPS, g.shape[1] // 2, g.shape[2]), g.dtype) for g in grads],
        scratch_shapes=[pltpu.SemaphoreType.DMA((n,))] * 2,
    )(*grads)


def share_halves(name, fulls):
    n = len(fulls)

    def body(*refs):
        o_refs = refs[n:2 * n]
        send, recv = refs[2 * n:]
        x, y, c, _, _ = _place()
        cps = []
        for w in range(n):
            kh = fulls[w].shape[0] // 2
            half = o_refs[w].at[pl.ds(c * kh, kh)]
            cps.append(pltpu.make_async_remote_copy(src_ref=half, dst_ref=half, send_sem=send.at[w], recv_sem=recv.at[w],
                                                    device_id=(x, y, 1 - c), device_id_type=MESH))
        for cp in cps:
            cp.start()
        for cp in cps:
            cp.wait()

    return pl.pallas_call(
        body, name=name, in_specs=[HBM_SPEC] * n, out_specs=[HBM_SPEC] * n,
        out_shape=[jax.ShapeDtypeStruct(f.shape, f.dtype) for f in fulls],
        input_output_aliases={w: w for w in range(n)},
        scratch_shapes=[pltpu.SemaphoreType.DMA((n,))] * 2,
    )(*fulls)


SEM_SPEC = pl.BlockSpec(memory_space=pltpu.SEMAPHORE)
ANY_SPEC = pl.BlockSpec(memory_space=pl.ANY)
DATAFLOW = pltpu.SideEffectType.DATAFLOW_SIDE_EFFECTING


def _hbm(x):
    return pltpu.with_memory_space_constraint(x, pltpu.HBM)


def gather_start(name, slots, after):
    n = len(slots)

    def body(*refs):
        g_refs = refs[n + 3:2 * n + 3]
        send, recv, token = refs[n + 1], refs[n + 2], refs[2 * n + 3]
        x, y, c, jme, others = _place()
        for w in range(n):
            kh = slots[w].shape[1] // 2
            mine = g_refs[w].at[jme, pl.ds(c * kh, kh)]
            for (px, py) in others:
                for core in range(2):
                    pltpu.make_async_remote_copy(src_ref=mine, dst_ref=mine, send_sem=send.at[w], recv_sem=recv.at[w],
                                                 device_id=(px, py, core), device_id_type=MESH).start()
        token[...] = jnp.zeros_like(token)

    outs = pl.pallas_call(
        body, name=name,
        in_specs=[HBM_SPEC] * n + [ANY_SPEC],
        out_specs=(SEM_SPEC, SEM_SPEC, *([HBM_SPEC] * n), pl.BlockSpec(memory_space=pltpu.VMEM)),
        out_shape=(pltpu.SemaphoreType.DMA((n,)), pltpu.SemaphoreType.DMA((n,)),
                   *[pltpu.HBM(s.shape, s.dtype) for s in slots], jax.ShapeDtypeStruct((8, 128), F32)),
        input_output_aliases={w: w + 2 for w in range(n)},
        compiler_params=pltpu.CompilerParams(has_side_effects=DATAFLOW),
    )(*[_hbm(s) for s in slots], after)
    return outs[0], outs[1], list(outs[2:2 + n]), outs[2 + n]


def gather_wait(name, send, recv, flying, after):
    n = len(flying)

    def body(*refs):
        send_ref, recv_ref = refs[n], refs[n + 1]
        g_refs = refs[n + 3:]
        x, y, c, _, _ = _place()
        for w in range(n):
            six = g_refs[w].at[pl.ds(0, 3)]
            cp = pltpu.make_async_remote_copy(src_ref=six, dst_ref=six, send_sem=send_ref.at[w], recv_sem=recv_ref.at[w],
                                              device_id=(x, y, 1 - c), device_id_type=MESH)
            cp.wait_send()
            cp.wait_recv()

    return pl.pallas_call(
        body, name=name,
        in_specs=[HBM_SPEC] * n + [SEM_SPEC, SEM_SPEC, ANY_SPEC],
        out_specs=[HBM_SPEC] * n,
        out_shape=[pltpu.HBM(s.shape, s.dtype) for s in flying],
        input_output_aliases={w: w for w in range(n)},
        compiler_params=pltpu.CompilerParams(has_side_effects=DATAFLOW),
    )(*flying, send, recv, after)


def scatter_start(name, parts):
    n = len(parts)

    def body(*refs):
        p_refs, q_refs = refs[2 * n + 2:3 * n + 2], refs[3 * n + 2:4 * n + 2]
        send, recv, token = refs[2 * n], refs[2 * n + 1], refs[4 * n + 2]
        x, y, c, jme, others = _place()
        for w in range(n):
            for (px, py) in others:
                pltpu.make_async_remote_copy(
                    src_ref=p_refs[w].at[2 * px + py], dst_ref=q_refs[w].at[jme], send_sem=send.at[w], recv_sem=recv.at[w],
                    device_id=(px, py, c), device_id_type=MESH).start()
        token[...] = jnp.zeros_like(token)

    lands = [_hbm(lax.empty(p.shape, p.dtype)) for p in parts]
    outs = pl.pallas_call(
        body, name=name,
        in_specs=[HBM_SPEC] * (2 * n),
        out_specs=(SEM_SPEC, SEM_SPEC, *([HBM_SPEC] * (2 * n)), pl.BlockSpec(memory_space=pltpu.VMEM)),
        out_shape=(pltpu.SemaphoreType.DMA((n,)), pltpu.SemaphoreType.DMA((n,)),
                   *[pltpu.HBM(p.shape, p.dtype) for p in parts], *[pltpu.HBM(p.shape, p.dtype) for p in parts],
                   jax.ShapeDtypeStruct((8, 128), F32)),
        input_output_aliases={w: w + 2 for w in range(2 * n)},
        compiler_params=pltpu.CompilerParams(has_side_effects=DATAFLOW),
    )(*[_hbm(p) for p in parts], *lands)
    return outs[0], outs[1], list(outs[2:2 + n]), list(outs[2 + n:2 + 2 * n]), outs[2 + 2 * n]


def scatter_wait(name, send, recv, parts, lands, after):
    n = len(parts)

    def body(*refs):
        send_ref, recv_ref = refs[2 * n], refs[2 * n + 1]
        q_refs = refs[3 * n + 3:]
        x, y, c, _, _ = _place()
        for w in range(n):
            three = q_refs[w].at[pl.ds(0, 3)]
            cp = pltpu.make_async_remote_copy(src_ref=three, dst_ref=three, send_sem=send_ref.at[w], recv_sem=recv_ref.at[w],
                                              device_id=(x, y, 1 - c), device_id_type=MESH)
            cp.wait_send()
            cp.wait_recv()

    outs = pl.pallas_call(
        body, name=name,
        in_specs=[HBM_SPEC] * (2 * n) + [SEM_SPEC, SEM_SPEC, ANY_SPEC],
        out_specs=[HBM_SPEC] * (2 * n),
        out_shape=[pltpu.HBM(p.shape, p.dtype) for p in parts] * 2,
        input_output_aliases={w: w for w in range(2 * n)},
        compiler_params=pltpu.CompilerParams(has_side_effects=DATAFLOW),
    )(*parts, *lands, send, recv, after)
    return list(outs[:n]), list(outs[n:])


def allreduce_small(v):
    R = v.shape[0]

    def body(v_ref, o_ref, sib_ref, mine_ref, all_ref, d_send, d_recv, i_send, i_recv):
        x, y, c, jme, others = _place()
        swap = pltpu.make_async_remote_copy(src_ref=v_ref, dst_ref=sib_ref, send_sem=d_send, recv_sem=d_recv,
                                            device_id=(x, y, 1 - c), device_id_type=MESH)
        swap.start()
        swap.wait()
        mine_ref[...] = v_ref[...] + sib_ref[...]
        for (px, py) in others:
            pltpu.make_async_remote_copy(src_ref=mine_ref, dst_ref=all_ref.at[jme], send_sem=i_send, recv_sem=i_recv,
                                         device_id=(px, py, c), device_id_type=MESH).start()
        three = all_ref.at[pl.ds(0, 3)]
        wait3 = pltpu.make_async_remote_copy(src_ref=three, dst_ref=three, send_sem=i_send, recv_sem=i_recv,
                                             device_id=(x, y, c), device_id_type=MESH)
        wait3.wait_recv()
        wait3.wait_send()
        all_ref[jme] = mine_ref[...]
        o_ref[...] = ((all_ref[0] + all_ref[1]) + all_ref[2]) + all_ref[3]

    vm = pl.BlockSpec(memory_space=pltpu.VMEM)
    return pl.pallas_call(
        body, name="allreduce_small", in_specs=[vm], out_specs=vm,
        out_shape=jax.ShapeDtypeStruct(v.shape, F32),
        scratch_shapes=[pltpu.VMEM((R, 128), F32), pltpu.VMEM((R, 128), F32), pltpu.VMEM((N_CHIPS, R, 128), F32),
                        pltpu.SemaphoreType.DMA, pltpu.SemaphoreType.DMA, pltpu.SemaphoreType.DMA, pltpu.SemaphoreType.DMA],
        compiler_params=pltpu.CompilerParams(vmem_limit_bytes=VMEM_LIMIT_BYTES),
    )(v)


def _row_tile(rows, width, mult=16, cap=3 << 20):
    best = rows
    for t in range(mult, rows + 1, mult):
        if rows % t == 0 and t * width * 4 <= cap:
            best = t
    return best


def add_sibling(name, mine, theirs, place):
    _, kh, ns = theirs.shape
    tr = _row_tile(kh, ns)
    nblk = kh // tr

    def body(place_ref, a_ref, b_ref, o_ref):
        o_ref[...] = (a_ref[...].astype(F32) + b_ref[...].astype(F32)).astype(BF16)

    return pl.pallas_call(
        body, name=name,
        grid_spec=pltpu.PrefetchScalarGridSpec(
            num_scalar_prefetch=1, grid=(N_CHIPS, nblk),
            in_specs=[pl.BlockSpec((None, tr, ns), lambda j, i, s: (j, s[1] * nblk + i, 0)),
                      pl.BlockSpec((None, tr, ns), lambda j, i, s: (j, i, 0))],
            out_specs=pl.BlockSpec((None, tr, ns), lambda j, i, s: (j, i, 0))),
        out_shape=jax.ShapeDtypeStruct(theirs.shape, BF16),
        compiler_params=_params(("parallel", "parallel")),
    )(place, mine, theirs)


def add_chips(name, q, p, place):
    _, kh, ns = q.shape
    tr = _row_tile(kh, ns)
    nblk = kh // tr

    def body(place_ref, *refs):
        q_refs, own_ref, o_ref = refs[:N_CHIPS], refs[N_CHIPS], refs[-1]
        jme = place_ref[0]
        tot = None
        for j in range(N_CHIPS):
            v = jnp.where(jme == j, own_ref[...], q_refs[j][...]).astype(F32)
            tot = v if tot is None else tot + v
        o_ref[...] = tot

    def q_ix(j):
        return lambda i, s: (jnp.where(s[0] == j, (j + 1) % N_CHIPS, j), i, 0)

    in_specs = [pl.BlockSpec((None, tr, ns), q_ix(j)) for j in range(N_CHIPS)]
    in_specs.append(pl.BlockSpec((None, tr, ns), lambda i, s: (s[0], i, 0)))
    return pl.pallas_call(
        body, name=name,
        grid_spec=pltpu.PrefetchScalarGridSpec(
            num_scalar_prefetch=1, grid=(nblk,), in_specs=in_specs,
            out_specs=pl.BlockSpec((tr, ns), lambda i, s: (s[1] * nblk + i, 0))),
        out_shape=jax.ShapeDtypeStruct((2 * kh, ns), F32),
        compiler_params=_params(("parallel",)),
    )(place, q, q, q, q, p)


ADAM_LR, ADAM_B1, ADAM_B2, ADAM_EPS, ADAM_WD, ADAM_STEP = 0.001, 0.9, 0.999, 1e-08, 0.01, 10


def adamw(name, w, g, m, v, layer, prev=None):
    _, k, ns = w.shape
    nsp = g.shape[1]
    tr = _row_tile(k, nsp, mult=8, cap=2 << 20)

    def body(w_ref, g_ref, m_ref, v_ref, *rest):
        go_ref, d_ref, mo_ref, vo_ref = rest[-4:]
        gv = g_ref[:, :ns] if nsp != ns else g_ref[...]
        mn = ADAM_B1 * m_ref[...] + (1.0 - ADAM_B1) * gv
        vn = ADAM_B2 * v_ref[...] + (1.0 - ADAM_B2) * (gv * gv)
        m_hat = mn / (1.0 - ADAM_B1 ** ADAM_STEP)
        v_hat = vn / (1.0 - ADAM_B2 ** ADAM_STEP)
        go_ref[...] = gv
        d_ref[...] = -ADAM_LR * (m_hat / (jnp.sqrt(v_hat) + ADAM_EPS) + ADAM_WD * w_ref[...])
        mo_ref[...] = mn
        vo_ref[...] = vn

    blk = pl.BlockSpec((None, tr, ns), lambda i: (layer, i, 0))
    gblk = pl.BlockSpec((tr, nsp), lambda i: (i, 0))
    args, in_specs, aliases = [w, g, m, v], [blk, gblk, blk, blk], {}
    if prev is not None:
        args += list(prev)
        in_specs += [pl.BlockSpec(memory_space=pl.ANY)] * 4
        aliases = {4 + i: i for i in range(4)}
    return pl.pallas_call(
        body, name=name, grid=(k // tr,), in_specs=in_specs, out_specs=[blk] * 4,
        out_shape=[jax.ShapeDtypeStruct(w.shape, F32)] * 4, input_output_aliases=aliases,
        compiler_params=_params(("parallel",)),
    )(*args)


WEIGHTS = ("ffn_a_norm", "ffn_a_w1", "ffn_a_w3", "ffn_a_w2", "mix_norm", "w_in", "q_a_norm", "w_uq", "kv_a_norm", "w_ukv",
           "q_norm", "k_norm", "gm_v_norm", "gm_ws", "gm_bs", "attn_out_norm", "gm_out_norm", "w_out", "ffn_b_norm",
           "ffn_b_w1", "ffn_b_w3", "ffn_b_w2", "ple_gate_norm", "w_ple_gate", "w_ple", "ple_norm")
_FF = FF_PAD - FF_SHARD
BIG = {"ffn_a_w1": _FF, "ffn_a_w3": _FF, "ffn_a_w2": _FF, "ffn_b_w1": _FF, "ffn_b_w3": _FF, "ffn_b_w2": _FF,
       "w_in": IN_SHARD_PAD - IN_SHARD, "w_uq": 0, "w_ukv": 0, "w_ple": 0, "w_out": 0, "w_ple_gate": 0}
TRANSPOSED = ("ffn_a_w1", "ffn_a_w3", "ffn_b_w1", "ffn_b_w3", "w_in")
SMALL = tuple(n for n in WEIGHTS if n not in BIG)
PACK = 1024


def _pack_small(d):
    parts = []
    for n in SMALL:
        flat = d[n].reshape(-1)
        parts.append(jnp.pad(flat, (0, (-flat.shape[0]) % PACK)))
    return jnp.concatenate(parts).reshape(-1, 128)


def _unpack_small(buf, like):
    flat = buf.reshape(-1)
    out, pos = {}, 0
    for n in SMALL:
        size = math.prod(like[n].shape)
        out[n] = flat[pos:pos + size].reshape(like[n].shape)
        pos += size + (-size) % PACK
    return out


def kernel(*args):
    names = (("x", "p", "positions") + WEIGHTS + ("loss_target",) + tuple("m_" + n for n in WEIGHTS)
             + tuple("v_" + n for n in WEIGHTS))
    a = dict(zip(names, args, strict=True))
    x, p, positions, target = a["x"][0], a["p"][:, 0], a["positions"][0], a["loss_target"][0]
    for n in TRANSPOSED:
        for pre in ("", "m_", "v_"):
            a[pre + n] = jnp.swapaxes(a[pre + n], 1, 2)

    place = jnp.stack([2 * lax.axis_index("x") + lax.axis_index("y"), lax.axis_index("c")]).astype(jnp.int32)
    small = {n: a[n] for n in SMALL}
    tabs = rope_tables(positions)
    slots = [[prep_shard(f"prep_{n}_{l}", a[n], l, BIG[n], place) for n in BIG] for l in range(2)]
    G0 = dict(zip(BIG, gather_weights("gather_l0", slots[0])))
    g_send, g_recv, flying, token = gather_start("gather_l1_start", slots[1], G0["w_uq"])
    small0 = {**small, "ffn_a_norm": small["ffn_a_norm"] + token[0, 0]}
    h, saved0 = layer_fwd(0, x, p[0], G0, small0, tabs)
    G1 = dict(zip(BIG, gather_wait("gather_l1_wait", g_send, g_recv, flying, h)))
    h, saved1 = layer_fwd(1, h, p[1], G1, small, tabs)
    dh, loss = loss_grad(h, target)
    loss = lax.psum(loss, ("x", "y", "c"))

    def start_reduce(tag, names, gl):
        mine = [gl[n] for n in names]
        theirs = exchange_halves(f"exchange_{tag}", mine)
        parts = [add_sibling(f"add_sibling_{n}_{tag}", d, r, place) for n, d, r in zip(names, mine, theirs)]
        return scatter_start(f"scatter_{tag}_start", parts)

    def finish_reduce(tag, names, started, after):
        send, recv, parts, lands, _ = started
        parts, slabs = scatter_wait(f"scatter_{tag}_wait", send, recv, parts, lands, after)
        halves = [add_chips(f"add_chips_{n}_{tag}", q, pt, place) for n, q, pt in zip(names, slabs, parts)]
        return dict(zip(names, share_halves(f"share_{tag}", halves)))

    def update(names, full, layer, prev):
        return {n: adamw(f"adamw_{n}_{layer}", a[n], full[n], a["m_" + n], a["v_" + n], layer, prev and prev[n])
                for n in names}

    group_b = ("ffn_a_w1", "ffn_a_w3", "ffn_a_w2")
    group_a = tuple(n for n in BIG if n not in group_b)
    grads = [None, None]
    dh, grads[1] = layer_bwd(1, dh, p[1], G1, small, tabs, saved1)
    red1 = start_reduce("l1", tuple(BIG), grads[1])
    w0 = {**saved0[0], "ple_gate_norm": saved0[0]["ple_gate_norm"] + red1[4][0, 0]}
    started = {}

    def before_ffn_a(gl):
        started["a"] = start_reduce("l0a", group_a, gl)
        return started["a"][4]

    gx, grads[0] = layer_bwd(0, dh, p[0], G0, small, tabs, (w0,) + saved0[1:], before_ffn_a)
    started["b"] = start_reduce("l0b", group_b, grads[0])
    outs1 = update(BIG, finish_reduce("l1", tuple(BIG), red1, started["b"][4]), 1, None)
    behind = outs1[group_b[-1]][1]
    full0 = {**finish_reduce("l0a", group_a, started["a"], behind), **finish_reduce("l0b", group_b, started["b"], behind)}
    outs0 = update(BIG, full0, 0, outs1)

    out_g, out_d, out_m, out_v = {}, {}, {}, {}
    for n in BIG:
        outs = [jnp.swapaxes(o, 1, 2) for o in outs0[n]] if n in TRANSPOSED else outs0[n]
        out_g[n], out_d[n], out_m[n], out_v[n] = outs

    gs = allreduce_small(_pack_small({n: jnp.stack([grads[0][n], grads[1][n]]) for n in SMALL}))
    rows = gs.shape[0] // 2
    packed = [_pack_small(d).reshape(2, rows, 128) for d in
              (small, {n: a["m_" + n] for n in SMALL}, {n: a["v_" + n] for n in SMALL})]
    gs = gs.reshape(2, rows, 128)
    sm = adamw("adamw_small_0", packed[0], gs[0], packed[1], packed[2], 0)
    sm = adamw("adamw_small_1", packed[0], gs[1], packed[1], packed[2], 1, sm)
    for dst, buf in zip((out_g, out_d, out_m, out_v), sm):
        dst.update(_unpack_small(buf, small))

    return (loss, gx[None], *[out_g[n] for n in WEIGHTS], *[out_d[n] for n in WEIGHTS],
            *[out_m[n] for n in WEIGHTS], *[out_v[n] for n in WEIGHTS])
```

```python
import math

import jax
import jax.numpy as jnp
from jax import lax
from jax.experimental import pallas as pl
from jax.experimental.pallas import tpu as pltpu

F32 = jnp.float32
BF16 = jnp.bfloat16

D_MODEL = 2048
D_FF = 5504
N_CHIPS = 4
FF_SHARD = D_FF // N_CHIPS
FF_PAD = 1408
FF_P = N_CHIPS * FF_PAD
HEADS = 8
QK_NOPE = 128
QK_ROPE = 64
QK_DIM = 192
HEAD_PAD = 256
V_DIM = 128
Q_RANK = 512
KV_RANK = 256
ATTN_W = 1024
GM_W = 1024
GROUPS = 8
CHUNK = 128
PLE_DIM = 256
IN_P = 3072
IN_SHARD = 720
IN_SHARD_PAD = 736
EPS = 1e-6
ROPE_BASE = 10000.0
ATTN_SCALE = QK_DIM ** -0.5
VMEM_LIMIT_BYTES = 56 * 1024 * 1024


def _params(sem):
    return pltpu.CompilerParams(dimension_semantics=sem, vmem_limit_bytes=VMEM_LIMIT_BYTES)


def _bf(x):
    return x if x.dtype == BF16 else x.astype(BF16)


def _sigmoid(x):
    return 1.0 / (1.0 + jnp.exp(-x))


_GELU_C = math.sqrt(2.0 / math.pi)


def _gelu(x):
    t = jnp.tanh(_GELU_C * (x + 0.044715 * x * x * x))
    return 0.5 * x * (1.0 + t)


def _gelu_grad(x):
    t = jnp.tanh(_GELU_C * (x + 0.044715 * x * x * x))
    return 0.5 * (1.0 + t) + 0.5 * x * (1.0 - t * t) * _GELU_C * (1.0 + 3 * 0.044715 * x * x)


def op_a(a, tm, tk):
    return (a, (tm, tk), lambda i, j, k: (i, k), 1)


def op_at(a, tm, tk):
    return (a, (tk, tm), lambda i, j, k: (k, i), 0)


def op_b(b, tk, tn):
    return (b, (tk, tn), lambda i, j, k: (k, j), 0)


def op_bt(b, tk, tn):
    return (b, (tn, tk), lambda i, j, k: (j, k), 1)


def op_b_cols(g, pre, tk, tn):
    nb = g.shape[-1] // tn
    none = (None,) * (1 + len(pre))
    return (g, none + (tk, tn), lambda i, j, k: (j // nb,) + tuple(pre) + (k, j % nb), 0)


def op_b_rows(g, pre, tk, tn, koff=0):
    nb = g.shape[-2] // tk
    none = (None,) * (1 + len(pre))
    return (g, none + (tk, tn), lambda i, j, k: ((k + koff) // nb,) + tuple(pre) + ((k + koff) % nb, j), 0)


def op_b_rows_t(g, pre, tk, tn):
    nb = g.shape[-2] // tn
    none = (None,) * (1 + len(pre))
    return (g, none + (tn, tk), lambda i, j, k: (j // nb,) + tuple(pre) + (j % nb, k), 1)


def tile_mn(x, tm, tn):
    return (x, (tm, tn), lambda i, j: (i, j))


def out_mn(M, N, tm, tn, dtype):
    return (jax.ShapeDtypeStruct((M, N), dtype), (tm, tn), lambda i, j: (i, j))


def out_cols(M, ns, tm, tn, dtype):
    nb = ns // tn
    return (jax.ShapeDtypeStruct((N_CHIPS, M, ns), dtype), (None, tm, tn), lambda i, j: (j // nb, i, j % nb))


def matmul(name, grid_mnk, a_ops, b_ops, terms, n_acc, extras, outs, epilogue, acc_tile, n_outer=False, after=None):
    gm, gn, gk = grid_mnk
    na, nb, nx, no = len(a_ops), len(b_ops), len(extras), len(outs)
    nd = 0 if after is None else 1

    def body(*refs):
        a_refs, b_refs = refs[:na], refs[na:na + nb]
        x_refs = refs[na + nb:na + nb + nx]
        o_refs = refs[na + nb + nx + nd:na + nb + nx + nd + no]
        acc_refs = refs[na + nb + nx + nd + no:]
        k = pl.program_id(2)

        @pl.when(k == 0)
        def _():
            for acc in acc_refs:
                acc[...] = jnp.zeros_like(acc)

        for ai, bi, ci in terms:
            dims = (((a_ops[ai][3],), (b_ops[bi][3],)), ((), ()))
            acc_refs[ci][...] += lax.dot_general(_bf(a_refs[ai][...]), _bf(b_refs[bi][...]), dims,
                                                 preferred_element_type=F32)

        @pl.when(k == gk - 1)
        def _():
            res = epilogue([acc[...] for acc in acc_refs], [x[...] for x in x_refs])
            for o, v in zip(o_refs, res):
                o[...] = v.astype(o.dtype)

    if n_outer:
        grid = (gn, gm, gk)

        def ix3(f):
            return lambda j, i, k: f(i, j, k)

        def ix2(f):
            return lambda j, i, k: f(i, j)
    else:
        grid = (gm, gn, gk)

        def ix3(f):
            return lambda i, j, k: f(i, j, k)

        def ix2(f):
            return lambda i, j, k: f(i, j)

    in_specs = [pl.BlockSpec(blk, ix3(f)) for (_, blk, f, _) in list(a_ops) + list(b_ops)]
    in_specs += [pl.BlockSpec(blk, ix2(f)) for (_, blk, f) in extras]
    in_specs += [pl.BlockSpec(memory_space=pl.ANY)] * nd
    out_specs = [pl.BlockSpec(blk, ix2(f)) for (_, blk, f) in outs]
    return pl.pallas_call(
        body,
        name=name,
        grid=grid,
        in_specs=in_specs,
        out_specs=out_specs,
        out_shape=[s for (s, _, _) in outs],
        scratch_shapes=[pltpu.VMEM(acc_tile, F32) for _ in range(n_acc)],
        compiler_params=_params(("parallel", "parallel", "arbitrary")),
    )(*[o[0] for o in a_ops], *[o[0] for o in b_ops], *[x[0] for x in extras], *([after] * nd))


def _acc0(accs, xs):
    return (accs[0],)


def mm_simple(name, a, b_op_fn, M, N, K, tm, tn, tk, out_dtype=F32, a_t=False, extras=(), epilogue=_acc0, outs=None):
    a_op = op_at(a, tm, tk) if a_t else op_a(a, tm, tk)
    outs = outs or [out_mn(M, N, tm, tn, out_dtype)]
    return matmul(name, (M // tm, N // tn, K // tk), [a_op], [b_op_fn(tk, tn)], [(0, 0, 0)], 1,
                  list(extras), outs, epilogue, (tm, tn))


def rms_fwd(name, x, g, width, col_blk=0, tm=256, out_dtype=BF16):
    T = x.shape[0]

    def body(x_ref, g_ref, o_ref):
        xv = x_ref[...].astype(F32)
        r = lax.rsqrt(jnp.mean(xv * xv, axis=-1, keepdims=True) + EPS)
        o_ref[...] = (xv * r * g_ref[...]).astype(o_ref.dtype)

    return pl.pallas_call(
        body, name=name, grid=(T // tm,),
        in_specs=[pl.BlockSpec((tm, width), lambda i: (i, col_blk)), pl.BlockSpec((1, width), lambda i: (0, 0))],
        out_specs=pl.BlockSpec((tm, width), lambda i: (i, 0)),
        out_shape=jax.ShapeDtypeStruct((T, width), out_dtype),
        compiler_params=_params(("parallel",)),
    )(x, g.reshape(1, width))


def rms_bwd(name, x, g, dn, width, col_blk=0, dres=None, tm=256, with_delta=False):
    T = x.shape[0]
    has_res = dres is not None

    def body(*refs):
        x_ref, g_ref, dn_ref = refs[:3]
        pos = 3
        res_ref = None
        if has_res:
            res_ref = refs[pos]
            pos += 1
        dx_ref, dg_ref = refs[pos], refs[pos + 1]
        delta_ref = refs[pos + 2] if with_delta else None
        i = pl.program_id(0)
        xv = x_ref[...].astype(F32)
        r = lax.rsqrt(jnp.mean(xv * xv, axis=-1, keepdims=True) + EPS)
        xh = xv * r
        d = dn_ref[...].astype(F32)
        gd = d * g_ref[...]
        dx = r * (gd - xh * jnp.mean(gd * xh, axis=-1, keepdims=True))
        if has_res:
            dx = dx + res_ref[...]
        dx_ref[...] = dx.astype(dx_ref.dtype)
        part = jnp.sum(d * xh, axis=0, keepdims=True)

        @pl.when(i == 0)
        def _():
            dg_ref[...] = part

        @pl.when(i > 0)
        def _():
            dg_ref[...] += part

        if with_delta:
            for h in range(width // 128):
                sl = slice(h * 128, (h + 1) * 128)
                s = jnp.sum(dx[:, sl] * xv[:, sl], axis=-1, keepdims=True)
                delta_ref[:, sl] = jnp.broadcast_to(s, (tm, 128))

    in_specs = [pl.BlockSpec((tm, width), lambda i: (i, col_blk)), pl.BlockSpec((1, width), lambda i: (0, 0)),
                pl.BlockSpec((tm, width), lambda i: (i, 0))]
    args = [x, g.reshape(1, width), dn]
    if has_res:
        in_specs.append(pl.BlockSpec((tm, width), lambda i: (i, 0)))
        args.append(dres)
    out_specs = [pl.BlockSpec((tm, width), lambda i: (i, 0)), pl.BlockSpec((1, width), lambda i: (0, 0))]
    out_shape = [jax.ShapeDtypeStruct((T, width), F32), jax.ShapeDtypeStruct((1, width), F32)]
    if with_delta:
        out_specs.append(pl.BlockSpec((tm, width), lambda i: (i, 0)))
        out_shape.append(jax.ShapeDtypeStruct((T, width), F32))
    return pl.pallas_call(
        body, name=name, grid=(T // tm,), in_specs=in_specs, out_specs=out_specs, out_shape=out_shape,
        compiler_params=_params(("arbitrary",)),
    )(*args)


def ffn_fwd(tag, h, g, w1g, w3g, w2g, pre):
    T = h.shape[0]
    n = rms_fwd(f"{tag}_rms", h, g, D_MODEL)
    tm, tn = 512, FF_PAD

    def up_epi(accs, xs):
        a1, a3 = accs
        return a1, a3, a1 * _sigmoid(a1) * a3

    a1, a3, s = matmul(
        f"{tag}_up", (T // tm, FF_P // tn, 1),
        [op_a(n, tm, D_MODEL)], [op_b_rows_t(w1g, pre, D_MODEL, tn), op_b_rows_t(w3g, pre, D_MODEL, tn)],
        [(0, 0, 0), (0, 1, 1)], 2, [],
        [out_mn(T, FF_P, tm, tn, BF16)] * 3, up_epi, (tm, tn), n_outer=True)

    tn2 = 1024
    (h_out,) = matmul(
        f"{tag}_down", (T // tm, D_MODEL // tn2, N_CHIPS),
        [op_a(s, tm, FF_PAD)], [op_b_rows(w2g, pre, FF_PAD, tn2)],
        [(0, 0, 0)], 1, [tile_mn(h, tm, tn2)],
        [out_mn(T, D_MODEL, tm, tn2, F32)], lambda accs, xs: (xs[0] + 0.5 * accs[0],), (tm, tn2))
    return h_out, (n, a1, a3, s)


def ffn_bwd(tag, dh_out, h, g, res, w1g, w3g, w2g, pre, dw_after=None):
    n, a1, a3, s = res
    T = h.shape[0]
    tm, tn = 512, FF_PAD

    def act_epi(accs, xs):
        ds = 0.5 * accs[0]
        x1, x3 = xs[0].astype(F32), xs[1].astype(F32)
        sg = _sigmoid(x1)
        silu = x1 * sg
        return ds * x3 * (sg + silu * (1.0 - sg)), ds * silu

    da1, da3 = matmul(
        f"{tag}_dact", (T // tm, FF_P // tn, 1),
        [op_a(dh_out, tm, D_MODEL)], [op_b_rows_t(w2g, pre, D_MODEL, tn)],
        [(0, 0, 0)], 1, [tile_mn(a1, tm, tn), tile_mn(a3, tm, tn)],
        [out_mn(T, FF_P, tm, tn, BF16)] * 2, act_epi, (tm, tn), n_outer=True)

    tk = 512

    def dw_t(nm, left, right, scale):
        (dw,) = matmul(
            f"{tag}_{nm}", (FF_P // FF_PAD, D_MODEL // 1024, T // tk),
            [op_at(left, FF_PAD, tk)], [op_b(right, tk, 1024)],
            [(0, 0, 0)], 1, [], [out_mn(FF_P, D_MODEL, FF_PAD, 1024, BF16)],
            lambda accs, xs: (scale * accs[0],), (FF_PAD, 1024), after=dw_after)
        return dw

    dw2 = dw_t("dw2", s, dh_out, 0.5)
    dw1 = dw_t("dw1", da1, n, 1.0)
    dw3 = dw_t("dw3", da3, n, 1.0)

    tn2 = 1024
    (dn,) = matmul(
        f"{tag}_dn", (T // tm, D_MODEL // tn2, N_CHIPS),
        [op_a(da1, tm, FF_PAD), op_a(da3, tm, FF_PAD)],
        [op_b_rows(w1g, pre, FF_PAD, tn2), op_b_rows(w3g, pre, FF_PAD, tn2)],
        [(0, 0, 0), (1, 1, 0)], 1, [], [out_mn(T, D_MODEL, tm, tn2, F32)], _acc0, (tm, tn2))
    dh, dg = rms_bwd(f"{tag}_rms_bwd", h, g, dn, D_MODEL, dres=dh_out)
    return dh, dg, dw1, dw3, dw2


def rope_tables(positions):
    inv_freq = ROPE_BASE ** (-jnp.arange(0, QK_ROPE, 2, dtype=F32) / QK_ROPE)
    ang = positions.astype(F32)[:, None] * inv_freq
    cos, sin = jnp.cos(ang), jnp.sin(ang)
    T = positions.shape[0]
    one, zero = jnp.ones((T, QK_NOPE), F32), jnp.zeros((T, 64), F32)
    z32, z128 = jnp.zeros((T, 32), F32), jnp.zeros((T, QK_NOPE), F32)
    c = jnp.concatenate([one, cos, cos, zero], axis=1)
    s1 = jnp.concatenate([z128, -sin, z32, zero], axis=1)
    s2 = jnp.concatenate([z128, z32, sin, zero], axis=1)
    return c, s1, s2


def _rope(y, c, s1, s2):
    return y * c + pltpu.roll(y, HEAD_PAD - 32, 1) * s1 + pltpu.roll(y, 32, 1) * s2


def _rope_t(d, c, s1, s2):
    return d * c + pltpu.roll(d * s1, 32, 1) + pltpu.roll(d * s2, HEAD_PAD - 32, 1)


def _head_norm(x):
    r = lax.rsqrt(jnp.sum(x * x, axis=-1, keepdims=True) * (1.0 / QK_DIM) + EPS)
    return x * r, r


def qk_prep_fwd(tag, q_raw, kk_raw, z_p, gq, gk, tabs, tm=256):
    T = q_raw.shape[0]
    c, s1, s2 = tabs

    def body(q_ref, k_ref, kr_ref, gq_ref, gk_ref, c_ref, s1_ref, s2_ref, qo_ref, ko_ref):
        cv, s1v, s2v = c_ref[...], s1_ref[...], s2_ref[...]
        kr = kr_ref[...]
        for h in range(HEADS):
            sl = slice(h * HEAD_PAD, (h + 1) * HEAD_PAD)
            xh, _ = _head_norm(q_ref[:, sl])
            qo_ref[:, sl] = (_rope(xh * gq_ref[...], cv, s1v, s2v) * ATTN_SCALE).astype(BF16)
            xh, _ = _head_norm(k_ref[:, sl] + kr)
            ko_ref[:, sl] = _rope(xh * gk_ref[...], cv, s1v, s2v).astype(BF16)

    row = lambda i: (i, 0)
    full = pl.BlockSpec((tm, HEADS * HEAD_PAD), row)
    tab = pl.BlockSpec((tm, HEAD_PAD), row)
    vec = pl.BlockSpec((1, HEAD_PAD), lambda i: (0, 0))
    return pl.pallas_call(
        body, name=f"{tag}_qk_prep", grid=(T // tm,),
        in_specs=[full, full, pl.BlockSpec((tm, HEAD_PAD), lambda i: (i, 3)), vec, vec, tab, tab, tab],
        out_specs=[full, full],
        out_shape=[jax.ShapeDtypeStruct((T, HEADS * HEAD_PAD), BF16)] * 2,
        compiler_params=_params(("parallel",)),
    )(q_raw, kk_raw, z_p, gq, gk, c, s1, s2)


def qk_prep_bwd(tag, dq_full, dk_full, q_raw, kk_raw, z_p, gq, gk, tabs, tm=256):
    T = q_raw.shape[0]
    c, s1, s2 = tabs

    def body(dq_ref, dk_ref, q_ref, k_ref, kr_ref, gq_ref, gk_ref, c_ref, s1_ref, s2_ref,
             dqr_ref, dkr_ref, dz_ref, dgq_ref, dgk_ref):
        i = pl.program_id(0)
        cv, s1v, s2v = c_ref[...], s1_ref[...], s2_ref[...]
        kr = kr_ref[...]
        lane = lax.broadcasted_iota(jnp.int32, (tm, HEAD_PAD), 1)
        slot = ((lane >= QK_NOPE) & (lane < QK_DIM)).astype(F32)

        def one(x, g, d):
            xh, r = _head_norm(x)
            dy = _rope_t(d, cv, s1v, s2v)
            gd = dy * g
            dx = r * (gd - xh * (jnp.sum(gd * xh, axis=-1, keepdims=True) * (1.0 / QK_DIM)))
            return dx, jnp.sum(dy * xh, axis=0, keepdims=True)

        dgq = jnp.zeros((1, HEAD_PAD), F32)
        dgk = jnp.zeros((1, HEAD_PAD), F32)
        dz = jnp.zeros((tm, HEAD_PAD), F32)
        for h in range(HEADS):
            sl = slice(h * HEAD_PAD, (h + 1) * HEAD_PAD)
            dx, dg = one(q_ref[:, sl], gq_ref[...], dq_ref[:, sl].astype(F32) * ATTN_SCALE)
            dqr_ref[:, sl] = dx
            dgq = dgq + dg
            dx, dg = one(k_ref[:, sl] + kr, gk_ref[...], dk_ref[:, sl].astype(F32))
            dkr_ref[:, sl] = dx
            dgk = dgk + dg
            dz = dz + dx
        dz_ref[...] = dz * slot

        @pl.when(i == 0)
        def _():
            dgq_ref[...] = dgq
            dgk_ref[...] = dgk

        @pl.when(i > 0)
        def _():
            dgq_ref[...] += dgq
            dgk_ref[...] += dgk

    row = lambda i: (i, 0)
    full = pl.BlockSpec((tm, HEADS * HEAD_PAD), row)
    tab = pl.BlockSpec((tm, HEAD_PAD), row)
    vec = pl.BlockSpec((1, HEAD_PAD), lambda i: (0, 0))
    return pl.pallas_call(
        body, name=f"{tag}_qk_prep_bwd", grid=(T // tm,),
        in_specs=[full, full, full, full, pl.BlockSpec((tm, HEAD_PAD), lambda i: (i, 3)), vec, vec, tab, tab, tab],
        out_specs=[full, full, tab, vec, vec],
        out_shape=[jax.ShapeDtypeStruct((T, HEADS * HEAD_PAD), F32)] * 2
        + [jax.ShapeDtypeStruct((T, HEAD_PAD), F32)] + [jax.ShapeDtypeStruct((1, HEAD_PAD), F32)] * 2,
        compiler_params=_params(("arbitrary",)),
    )(dq_full, dk_full, q_raw, kk_raw, z_p, gq, gk, c, s1, s2)


def attn_fwd(tag, q_full, k_full, vv, blk=512):
    T = q_full.shape[0]
    nb = T // blk
    neg = float(jnp.finfo(jnp.float32).min)

    def body(q_ref, k_ref, v_ref, o_ref, lse_ref, m_ref, l_ref, acc_ref):
        i = pl.program_id(1)
        m_ref[...] = jnp.full_like(m_ref, neg)
        l_ref[...] = jnp.zeros_like(l_ref)
        acc_ref[...] = jnp.zeros_like(acc_ref)
        q = q_ref[...]

        def step(j, masked):
            rows = pl.ds(pl.multiple_of(j * blk, blk), blk)
            s = lax.dot_general(q, k_ref[rows, :], (((1,), (1,)), ((), ())), preferred_element_type=F32)
            if masked:
                row = lax.broadcasted_iota(jnp.int32, (blk, blk), 0)
                col = lax.broadcasted_iota(jnp.int32, (blk, blk), 1)
                s = jnp.where(col <= row, s, neg)
            m_prev = m_ref[...]
            m_new = jnp.maximum(m_prev, jnp.max(s, axis=-1, keepdims=True))
            alpha = jnp.exp(m_prev - m_new)
            p = jnp.exp(s - m_new[:, :1])
            l_ref[...] = alpha * l_ref[...] + jnp.sum(p, axis=-1, keepdims=True)
            acc_ref[...] = alpha * acc_ref[...] + jnp.dot(p.astype(BF16), v_ref[rows, :], preferred_element_type=F32)
            m_ref[...] = m_new

        def off_diagonal(j, carry):
            step(j, False)
            return carry

        lax.fori_loop(0, i, off_diagonal, 0)
        step(i, True)
        o_ref[...] = acc_ref[...] / l_ref[...]
        lse_ref[...] = m_ref[...] + jnp.log(l_ref[...])

    return pl.pallas_call(
        body, name=f"{tag}_attn_fwd", grid=(HEADS, nb),
        in_specs=[pl.BlockSpec((blk, HEAD_PAD), lambda h, i: (i, h)),
                  pl.BlockSpec((T, HEAD_PAD), lambda h, i: (0, h)), pl.BlockSpec((T, V_DIM), lambda h, i: (0, h))],
        out_specs=[pl.BlockSpec((blk, V_DIM), lambda h, i: (i, h))] * 2,
        out_shape=[jax.ShapeDtypeStruct((T, ATTN_W), F32)] * 2,
        scratch_shapes=[pltpu.VMEM((blk, V_DIM), F32)] * 3,
        compiler_params=_params(("parallel", "parallel")),
    )(q_full, k_full, vv)


def attn_bwd(tag, q_full, k_full, vv, do, lse, delta, blk=512):
    T = q_full.shape[0]
    nb = T // blk
    neg = float(jnp.finfo(jnp.float32).min)

    def body(q_ref, k_ref, v_ref, do_ref, lse_ref, dl_ref, dq_ref, dk_ref, dv_ref, dk_acc, dv_acc):
        j = pl.program_id(1)

        @pl.when(j == 0)
        def _():
            dq_ref[...] = jnp.zeros_like(dq_ref)

        dk_acc[...] = jnp.zeros_like(dk_acc)
        dv_acc[...] = jnp.zeros_like(dv_acc)
        k, v = k_ref[...], v_ref[...]

        def step(i, masked):
            rows = pl.ds(pl.multiple_of(i * blk, blk), blk)
            q = q_ref[rows, :]
            s = lax.dot_general(q, k, (((1,), (1,)), ((), ())), preferred_element_type=F32)
            if masked:
                row = lax.broadcasted_iota(jnp.int32, (blk, blk), 0)
                col = lax.broadcasted_iota(jnp.int32, (blk, blk), 1)
                s = jnp.where(col <= row, s, neg)
            p = jnp.exp(s - lse_ref[rows, :1])
            dob = _bf(do_ref[rows, :])
            dv_acc[...] += lax.dot_general(p.astype(BF16), dob, (((0,), (0,)), ((), ())), preferred_element_type=F32)
            dp = lax.dot_general(dob, v, (((1,), (1,)), ((), ())), preferred_element_type=F32)
            ds = (p * (dp - dl_ref[rows, :1])).astype(BF16)
            dk_acc[...] += lax.dot_general(ds, q, (((0,), (0,)), ((), ())), preferred_element_type=F32)
            dq_ref[rows, :] += jnp.dot(ds, k, preferred_element_type=F32)

        def off_diagonal(i, carry):
            step(i, False)
            return carry

        step(j, True)
        lax.fori_loop(j + 1, nb, off_diagonal, 0)
        dk_ref[...] = dk_acc[...]
        dv_ref[...] = dv_acc[...]

    head = lambda h, j: (0, h)
    kv_ix = lambda h, j: (j, h)
    return pl.pallas_call(
        body, name=f"{tag}_attn_bwd", grid=(HEADS, nb),
        in_specs=[pl.BlockSpec((T, HEAD_PAD), head), pl.BlockSpec((blk, HEAD_PAD), kv_ix),
                  pl.BlockSpec((blk, V_DIM), kv_ix), pl.BlockSpec((T, V_DIM), head),
                  pl.BlockSpec((T, V_DIM), head), pl.BlockSpec((T, V_DIM), head)],
        out_specs=[pl.BlockSpec((T, HEAD_PAD), head),
                   pl.BlockSpec((blk, HEAD_PAD), kv_ix), pl.BlockSpec((blk, V_DIM), kv_ix)],
        out_shape=[jax.ShapeDtypeStruct((T, HEADS * HEAD_PAD), F32)] * 2 + [jax.ShapeDtypeStruct((T, ATTN_W), F32)],
        scratch_shapes=[pltpu.VMEM((blk, HEAD_PAD), F32), pltpu.VMEM((blk, V_DIM), F32)],
        compiler_params=_params(("parallel", "arbitrary")),
    )(q_full, k_full, vv, do, lse, delta)


def _gm_forward(u, v, gv, wc_ref, bb_ref, nchunk):
    ug = _gelu(u)
    vg = _gelu(v)
    rv = lax.rsqrt(jnp.mean(vg * vg, axis=-1, keepdims=True) + EPS)
    vhat = vg * rv
    vn = (vhat * gv).astype(BF16)
    gates = []
    for cidx in range(nchunk):
        rows = slice(cidx * CHUNK, (cidx + 1) * CHUNK)
        gates.append(jnp.concatenate(
            [jnp.dot(wc_ref[gidx], vn[rows, gidx * 128:(gidx + 1) * 128], preferred_element_type=F32) + bb_ref[gidx]
             for gidx in range(GROUPS)], axis=1))
    gate = jnp.concatenate(gates, axis=0)
    return ug, vhat, rv, vn, gate


def gmlp_fwd(tag, z_p, gv, gout, wc, bb, tm=256):
    T = z_p.shape[0]
    nchunk = tm // CHUNK

    def body(u_ref, v_ref, gv_ref, go_ref, wc_ref, bb_ref, o_ref):
        ug, _, _, _, gate = _gm_forward(u_ref[...], v_ref[...], gv_ref[...], wc_ref, bb_ref, nchunk)
        go = ug * gate
        ro = lax.rsqrt(jnp.mean(go * go, axis=-1, keepdims=True) + EPS)
        o_ref[...] = (go * ro * go_ref[...]).astype(BF16)

    vec = pl.BlockSpec((1, GM_W), lambda i: (0, 0))
    w3 = pl.BlockSpec((GROUPS, CHUNK, CHUNK), lambda i: (0, 0, 0))
    return pl.pallas_call(
        body, name=f"{tag}_gmlp_fwd", grid=(T // tm,),
        in_specs=[pl.BlockSpec((tm, GM_W), lambda i: (i, 1)), pl.BlockSpec((tm, GM_W), lambda i: (i, 2)), vec, vec, w3, w3],
        out_specs=pl.BlockSpec((tm, GM_W), lambda i: (i, 0)),
        out_shape=jax.ShapeDtypeStruct((T, GM_W), BF16),
        compiler_params=_params(("parallel",)),
    )(z_p, z_p, gv.reshape(1, GM_W), gout.reshape(1, GM_W), wc, bb)


def gmlp_bwd(tag, z_p, dmixed, gv, gout, wc, bb, tm=256):
    T = z_p.shape[0]
    nchunk = tm // CHUNK

    def body(u_ref, v_ref, dm_ref, gv_ref, go_ref, wc_ref, bb_ref, du_ref, dv_ref, dwc_ref, dbb_ref, dgv_ref, dgo_ref):
        i = pl.program_id(0)
        u, v = u_ref[...], v_ref[...]
        ug, vhat, rv, vn, gate = _gm_forward(u, v, gv_ref[...], wc_ref, bb_ref, nchunk)
        go = ug * gate
        ro = lax.rsqrt(jnp.mean(go * go, axis=-1, keepdims=True) + EPS)
        ohat = go * ro
        dm = dm_ref[...].astype(F32)
        dgo_part = jnp.sum(dm * ohat, axis=0, keepdims=True)
        doh = dm * go_ref[...]
        dgo = ro * (doh - ohat * jnp.mean(doh * ohat, axis=-1, keepdims=True))
        du_ref[...] = dgo * gate * _gelu_grad(u)
        dgate = dgo * ug
        dgb = dgate.astype(BF16)
        dvn_rows = []
        dwc_parts = []
        dbb_parts = []
        for gidx in range(GROUPS):
            cols = slice(gidx * 128, (gidx + 1) * 128)
            dw = jnp.zeros((CHUNK, CHUNK), F32)
            db = jnp.zeros((CHUNK, 128), F32)
            for cidx in range(nchunk):
                rows = slice(cidx * CHUNK, (cidx + 1) * CHUNK)
                dw = dw + lax.dot_general(dgb[rows, cols], vn[rows, cols], (((1,), (1,)), ((), ())),
                                          preferred_element_type=F32)
                db = db + dgate[rows, cols]
            dwc_parts.append(dw)
            dbb_parts.append(db)
        for cidx in range(nchunk):
            rows = slice(cidx * CHUNK, (cidx + 1) * CHUNK)
            dvn_rows.append(jnp.concatenate(
                [lax.dot_general(wc_ref[gidx], dgb[rows, gidx * 128:(gidx + 1) * 128], (((0,), (0,)), ((), ())),
                                 preferred_element_type=F32) for gidx in range(GROUPS)], axis=1))
        dvn = jnp.concatenate(dvn_rows, axis=0)
        dgv_part = jnp.sum(dvn * vhat, axis=0, keepdims=True)
        dvh = dvn * gv_ref[...]
        dvg = rv * (dvh - vhat * jnp.mean(dvh * vhat, axis=-1, keepdims=True))
        dv_ref[...] = dvg * _gelu_grad(v)

        @pl.when(i == 0)
        def _():
            for gidx in range(GROUPS):
                dwc_ref[gidx] = dwc_parts[gidx]
                dbb_ref[gidx] = dbb_parts[gidx]
            dgv_ref[...] = dgv_part
            dgo_ref[...] = dgo_part

        @pl.when(i > 0)
        def _():
            for gidx in range(GROUPS):
                dwc_ref[gidx] += dwc_parts[gidx]
                dbb_ref[gidx] += dbb_parts[gidx]
            dgv_ref[...] += dgv_part
            dgo_ref[...] += dgo_part

    vec = pl.BlockSpec((1, GM_W), lambda i: (0, 0))
    w3 = pl.BlockSpec((GROUPS, CHUNK, CHUNK), lambda i: (0, 0, 0))
    blk = pl.BlockSpec((tm, GM_W), lambda i: (i, 0))
    return pl.pallas_call(
        body, name=f"{tag}_gmlp_bwd", grid=(T // tm,),
        in_specs=[pl.BlockSpec((tm, GM_W), lambda i: (i, 1)), pl.BlockSpec((tm, GM_W), lambda i: (i, 2)),
                  pl.BlockSpec((tm, GM_W), lambda i: (i, 1)), vec, vec, w3, w3],
        out_specs=[blk, blk, w3, w3, vec, vec],
        out_shape=[jax.ShapeDtypeStruct((T, GM_W), F32)] * 2 + [jax.ShapeDtypeStruct((GROUPS, CHUNK, CHUNK), F32)] * 2
        + [jax.ShapeDtypeStruct((1, GM_W), F32)] * 2,
        compiler_params=_params(("arbitrary",)),
    )(z_p, z_p, dmixed, gv.reshape(1, GM_W), gout.reshape(1, GM_W), wc, bb)


def mixer_fwd(tag, h, w, tabs, wout_g, pre):
    T = h.shape[0]
    n2 = rms_fwd(f"{tag}_mix_rms", h, w["mix_norm"], D_MODEL)
    (z_p,) = mm_simple(f"{tag}_win", n2, lambda tk, tn: op_bt(w["w_in_pt"], tk, tn), T, IN_P, D_MODEL, 512, 1024, D_MODEL)
    cqn = rms_fwd(f"{tag}_cq_rms", z_p, w["q_a_norm"], Q_RANK, col_blk=0)
    ckvn = rms_fwd(f"{tag}_ckv_rms", z_p, w["kv_a_norm"], KV_RANK, col_blk=2)
    (q_raw,) = mm_simple(f"{tag}_wq", cqn, lambda tk, tn: op_b(w["wq_p"], tk, tn), T, 2048, Q_RANK, 512, 1024, Q_RANK)
    (kk_raw,) = mm_simple(f"{tag}_wk", ckvn, lambda tk, tn: op_b(w["wk_p"], tk, tn), T, 2048, KV_RANK, 512, 1024, KV_RANK)
    (vv,) = mm_simple(f"{tag}_wv", ckvn, lambda tk, tn: op_b(w["wv"], tk, tn), T, ATTN_W, KV_RANK, 512, 1024, KV_RANK,
                      out_dtype=BF16)
    q_full, k_full = qk_prep_fwd(tag, q_raw, kk_raw, z_p, w["gq_p"], w["gk_p"], tabs)
    a_out, lse = attn_fwd(tag, q_full, k_full, vv)
    mixed_a = rms_fwd(f"{tag}_ao_rms", a_out, w["attn_out_norm"], ATTN_W)
    mixed_g = gmlp_fwd(tag, z_p, w["gm_v_norm"], w["gm_out_norm"], w["wc"], w["bb"])
    tm, tn, tk = 512, 1024, 512
    (h2,) = matmul(
        f"{tag}_wout", (T // tm, D_MODEL // tn, ATTN_W // tk),
        [op_a(mixed_a, tm, tk), op_a(mixed_g, tm, tk)],
        [op_b_rows(wout_g, pre, tk, tn), op_b_rows(wout_g, pre, tk, tn, koff=ATTN_W // tk)],
        [(0, 0, 0), (1, 1, 0)], 1, [tile_mn(h, tm, tn)], [out_mn(T, D_MODEL, tm, tn, F32)],
        lambda accs, xs: (xs[0] + accs[0],), (tm, tn))
    res = dict(n2=n2, z_p=z_p, cqn=cqn, ckvn=ckvn, q_raw=q_raw, kk_raw=kk_raw, vv=vv, q_full=q_full, k_full=k_full,
               a_out=a_out, lse=lse, mixed_a=mixed_a, mixed_g=mixed_g)
    return h2, res


def mixer_bwd(tag, dh2, h, w, tabs, wout_g, pre, r):
    T = h.shape[0]
    g = {}
    (dmixed,) = mm_simple(f"{tag}_dmixed", dh2, lambda tk, tn: op_b_rows_t(wout_g, pre, tk, tn), T, D_MODEL, D_MODEL,
                          512, 512, D_MODEL)
    (dwo_a,) = mm_simple(f"{tag}_dwout_a", r["mixed_a"], lambda tk, tn: op_b(dh2, tk, tn), ATTN_W, D_MODEL, T,
                         1024, 1024, 512, a_t=True, out_dtype=BF16)
    (dwo_g,) = mm_simple(f"{tag}_dwout_g", r["mixed_g"], lambda tk, tn: op_b(dh2, tk, tn), GM_W, D_MODEL, T,
                         1024, 1024, 512, a_t=True, out_dtype=BF16)
    g["w_out"] = jnp.concatenate([dwo_a, dwo_g], axis=0)
    da_out, g["attn_out_norm"], delta = rms_bwd(f"{tag}_ao_rms_bwd", r["a_out"], w["attn_out_norm"], dmixed, ATTN_W,
                                                with_delta=True)
    dq_full, dk_full, dvv = attn_bwd(tag, r["q_full"], r["k_full"], r["vv"], da_out, r["lse"], delta)
    dq_raw, dkk_raw, dzkr, g["gq_p"], g["gk_p"] = qk_prep_bwd(tag, dq_full, dk_full, r["q_raw"], r["kk_raw"], r["z_p"],
                                                            w["gq_p"], w["gk_p"], tabs)
    (g["wq_p"],) = mm_simple(f"{tag}_dwq", r["cqn"], lambda tk, tn: op_b(dq_raw, tk, tn), Q_RANK, 2048, T, Q_RANK, 1024, 512,
                             a_t=True, out_dtype=BF16)
    (g["wk_p"],) = mm_simple(f"{tag}_dwk", r["ckvn"], lambda tk, tn: op_b(dkk_raw, tk, tn), KV_RANK, 2048, T, KV_RANK, 1024,
                             512, a_t=True, out_dtype=BF16)
    (g["wv"],) = mm_simple(f"{tag}_dwv", r["ckvn"], lambda tk, tn: op_b(dvv, tk, tn), KV_RANK, ATTN_W, T, KV_RANK, 1024, 512,
                           a_t=True, out_dtype=BF16)
    (dcqn,) = mm_simple(f"{tag}_dcqn", dq_raw, lambda tk, tn: op_bt(w["wq_p"], tk, tn), T, Q_RANK, 2048, 512, Q_RANK, 2048)
    (dck1,) = mm_simple(f"{tag}_dckvn_k", dkk_raw, lambda tk, tn: op_bt(w["wk_p"], tk, tn), T, KV_RANK, 2048, 512, KV_RANK,
                        2048)
    (dckvn,) = mm_simple(f"{tag}_dckvn_v", dvv, lambda tk, tn: op_bt(w["wv"], tk, tn), T, KV_RANK, ATTN_W, 512, KV_RANK,
                         ATTN_W, extras=[tile_mn(dck1, 512, KV_RANK)], epilogue=lambda accs, xs: (accs[0] + xs[0],))
    dc_q, g["q_a_norm"] = rms_bwd(f"{tag}_cq_rms_bwd", r["z_p"], w["q_a_norm"], dcqn, Q_RANK, col_blk=0)
    dc_kv, g["kv_a_norm"] = rms_bwd(f"{tag}_ckv_rms_bwd", r["z_p"], w["kv_a_norm"], dckvn, KV_RANK, col_blk=2)
    du, dv, g["wc"], g["bb"], g["gm_v_norm"], g["gm_out_norm"] = gmlp_bwd(
        tag, r["z_p"], dmixed, w["gm_v_norm"], w["gm_out_norm"], w["wc"], w["bb"])
    dz_p = jnp.concatenate([dc_q, dc_kv, dzkr, du, dv], axis=1).astype(BF16)
    (g["w_in_pt"],) = mm_simple(f"{tag}_dwin", dz_p, lambda tk, tn: op_b(r["n2"], tk, tn), IN_P, D_MODEL, T, 1024, 1024, 512,
                                a_t=True, out_dtype=BF16)
    (dn2,) = mm_simple(f"{tag}_dn2", dz_p, lambda tk, tn: op_b(w["w_in_pt"], tk, tn), T, D_MODEL, IN_P, 512, 1024, IN_P)
    dh1, g["mix_norm"] = rms_bwd(f"{tag}_mix_rms_bwd", h, w["mix_norm"], dn2, D_MODEL, dres=dh2)
    return dh1, g


def ple_fwd(tag, h3, p_l, w, wpg_g, wple_g, pre):
    T = h3.shape[0]
    (pw,) = mm_simple(f"{tag}_wple", p_l, lambda tk, tn: op_b_cols(wple_g, pre, tk, tn), T, D_MODEL, PLE_DIM, 512, 512,
                      PLE_DIM)
    e = rms_fwd(f"{tag}_ple_rms", pw, w["ple_norm"], D_MODEL, out_dtype=F32)
    n4 = rms_fwd(f"{tag}_pg_rms", h3, w["ple_gate_norm"], D_MODEL)

    def epi(accs, xs):
        gt = _sigmoid(accs[0])
        return xs[0] + gt * xs[1], gt

    tm, tn, tk = 512, 1024, 512
    h4, gate = matmul(
        f"{tag}_wpg", (T // tm, D_MODEL // tn, D_MODEL // tk),
        [op_a(n4, tm, tk)], [op_b_rows(wpg_g, pre, tk, tn)], [(0, 0, 0)], 1,
        [tile_mn(h3, tm, tn), tile_mn(e, tm, tn)],
        [out_mn(T, D_MODEL, tm, tn, F32), out_mn(T, D_MODEL, tm, tn, BF16)], epi, (tm, tn))
    return h4, dict(pw=pw, e=e, n4=n4, gate=gate)


def ple_bwd(tag, dh4, h3, p_l, w, wpg_g, wple_g, pre, r, tm=256):
    T = h3.shape[0]

    def act_body(d_ref, g_ref, e_ref, dpre_ref, de_ref):
        d, gt = d_ref[...], g_ref[...].astype(F32)
        dpre_ref[...] = (d * e_ref[...] * gt * (1.0 - gt)).astype(BF16)
        de_ref[...] = d * gt

    blk = pl.BlockSpec((tm, D_MODEL), lambda i: (i, 0))
    dpre, de = pl.pallas_call(
        act_body, name=f"{tag}_ple_act_bwd", grid=(T // tm,), in_specs=[blk, blk, blk], out_specs=[blk, blk],
        out_shape=[jax.ShapeDtypeStruct((T, D_MODEL), BF16), jax.ShapeDtypeStruct((T, D_MODEL), F32)],
        compiler_params=_params(("parallel",)),
    )(dh4, r["gate"], r["e"])
    g = {}
    (g["w_ple_gate"],) = mm_simple(f"{tag}_dwpg", r["n4"], lambda tk, tn: op_b(dpre, tk, tn), D_MODEL, D_MODEL, T,
                                   1024, 1024, 512, a_t=True, out_dtype=BF16)
    (dn4,) = mm_simple(f"{tag}_dn4", dpre, lambda tk, tn: op_b_rows_t(wpg_g, pre, tk, tn), T, D_MODEL, D_MODEL, 512, 512,
                       D_MODEL)
    dh3, g["ple_gate_norm"] = rms_bwd(f"{tag}_pg_rms_bwd", h3, w["ple_gate_norm"], dn4, D_MODEL, dres=dh4)
    dpw, g["ple_norm"] = rms_bwd(f"{tag}_ple_rms_bwd", r["pw"], w["ple_norm"], de, D_MODEL)
    (g["w_ple"],) = mm_simple(f"{tag}_dwple", p_l, lambda tk, tn: op_b(dpw, tk, tn), PLE_DIM, D_MODEL, T, PLE_DIM, 512, 512,
                              a_t=True, outs=[out_cols(PLE_DIM, 512, PLE_DIM, 512, BF16)])
    return dh3, g


def loss_grad(y, target, tm=256):
    T = y.shape[0]

    def body(y_ref, t_ref, dy_ref, l_ref):
        i = pl.program_id(0)
        d = y_ref[...] - t_ref[...]
        dy_ref[...] = d * (1.0 / D_MODEL)
        part = jnp.sum((d * d).reshape(tm // 8, 8, D_MODEL), axis=0)

        @pl.when(i == 0)
        def _():
            l_ref[...] = part

        @pl.when(i > 0)
        def _():
            l_ref[...] += part

    blk = pl.BlockSpec((tm, D_MODEL), lambda i: (i, 0))
    dy, part = pl.pallas_call(
        body, name="loss_grad", grid=(T // tm,), in_specs=[blk, blk],
        out_specs=[blk, pl.BlockSpec((8, D_MODEL), lambda i: (0, 0))],
        out_shape=[jax.ShapeDtypeStruct((T, D_MODEL), F32), jax.ShapeDtypeStruct((8, D_MODEL), F32)],
        compiler_params=_params(("arbitrary",)),
    )(y, target)
    return dy, 0.5 * jnp.sum(part) / D_MODEL


def _unshard_cols(g_l):
    return g_l.transpose(1, 0, 2).reshape(g_l.shape[1], -1)


def _shard_cols(w):
    return w.reshape(w.shape[0], N_CHIPS, -1).transpose(1, 0, 2)


def layer_weights(l, Gl, small):
    w = {k: small[k][l] for k in ("mix_norm", "q_a_norm", "kv_a_norm", "gm_v_norm", "attn_out_norm", "gm_out_norm",
                                  "ple_gate_norm", "ple_norm")}
    wint = Gl["w_in"][:, :IN_SHARD].reshape(-1, D_MODEL)
    z = lambda n: jnp.zeros((n, D_MODEL), BF16)
    w["w_in_pt"] = jnp.concatenate([wint[:768], z(128), wint[768:832], z(64), wint[832:]], axis=0)
    wuq = _unshard_cols(Gl["w_uq"]).reshape(Q_RANK, HEADS, QK_DIM)
    w["wq_p"] = jnp.pad(wuq, ((0, 0), (0, 0), (0, HEAD_PAD - QK_DIM))).reshape(Q_RANK, HEADS * HEAD_PAD)
    wukv = _unshard_cols(Gl["w_ukv"]).reshape(KV_RANK, HEADS, QK_NOPE + V_DIM)
    w["wk_p"] = jnp.pad(wukv[:, :, :QK_NOPE], ((0, 0), (0, 0), (0, HEAD_PAD - QK_NOPE))).reshape(KV_RANK, HEADS * HEAD_PAD)
    w["wv"] = wukv[:, :, QK_NOPE:].reshape(KV_RANK, ATTN_W)
    w["gq_p"] = jnp.pad(small["q_norm"][l], (0, HEAD_PAD - QK_DIM)).reshape(1, HEAD_PAD)
    w["gk_p"] = jnp.pad(small["k_norm"][l], (0, HEAD_PAD - QK_DIM)).reshape(1, HEAD_PAD)
    tril = jnp.tril(jnp.ones((CHUNK, CHUNK), dtype=bool))
    w["wc"] = jnp.where(tril[None], small["gm_ws"][l], 0.0).astype(BF16)
    w["bb"] = jnp.broadcast_to(small["gm_bs"][l][:, :, None], (GROUPS, CHUNK, 128)).astype(F32)
    return w


def mixer_grads_to_shards(g):
    out = {}
    dwint = g["w_in_pt"]
    dwint = jnp.concatenate([dwint[:768], dwint[896:960], dwint[1024:]], axis=0).reshape(N_CHIPS, IN_SHARD, D_MODEL)
    out["w_in"] = jnp.pad(dwint, ((0, 0), (0, IN_SHARD_PAD - IN_SHARD), (0, 0)))
    dwuq = g["wq_p"].reshape(Q_RANK, HEADS, HEAD_PAD)[:, :, :QK_DIM].reshape(Q_RANK, HEADS * QK_DIM)
    out["w_uq"] = _shard_cols(dwuq)
    dwukv = jnp.concatenate([g["wk_p"].reshape(KV_RANK, HEADS, HEAD_PAD)[:, :, :QK_NOPE],
                             g["wv"].reshape(KV_RANK, HEADS, V_DIM)], axis=-1).reshape(KV_RANK, HEADS * (QK_NOPE + V_DIM))
    out["w_ukv"] = _shard_cols(dwukv)
    out["w_out"] = g["w_out"].reshape(N_CHIPS, D_MODEL // N_CHIPS, D_MODEL)
    out["q_norm"] = g["gq_p"][0, :QK_DIM]
    out["k_norm"] = g["gk_p"][0, :QK_DIM]
    tril = jnp.tril(jnp.ones((CHUNK, CHUNK), dtype=bool))
    out["gm_ws"] = jnp.where(tril[None], g["wc"], 0.0)
    out["gm_bs"] = jnp.sum(g["bb"], axis=-1)
    for k in ("mix_norm", "q_a_norm", "kv_a_norm", "gm_v_norm", "attn_out_norm", "gm_out_norm"):
        out[k] = g[k][0]
    return out


def layer_fwd(l, h, p_l, Gl, small, tabs, after_first_ffn=None):
    h1, r_a = ffn_fwd(f"l{l}a", h, small["ffn_a_norm"][l], Gl["ffn_a_w1"], Gl["ffn_a_w3"], Gl["ffn_a_w2"], ())
    if after_first_ffn is not None:
        Gl, small = after_first_ffn(h1, Gl, small)
    w = layer_weights(l, Gl, small)
    h2, r_m = mixer_fwd(f"l{l}", h1, w, tabs, Gl["w_out"], ())
    h3, r_b = ffn_fwd(f"l{l}b", h2, small["ffn_b_norm"][l], Gl["ffn_b_w1"], Gl["ffn_b_w3"], Gl["ffn_b_w2"], ())
    h4, r_p = ple_fwd(f"l{l}", h3, p_l, w, Gl["w_ple_gate"], Gl["w_ple"], ())
    return h4, (w, h, h1, h2, h3, r_a, r_m, r_b, r_p)


def layer_bwd(l, dh, p_l, Gl, small, tabs, saved, before_ffn_a=None):
    w, h0, h1, h2, h3, r_a, r_m, r_b, r_p = saved
    slabs = lambda d: d.reshape(N_CHIPS, FF_PAD, D_MODEL)
    gl = {}
    dh, g_p = ple_bwd(f"l{l}", dh, h3, p_l, w, Gl["w_ple_gate"], Gl["w_ple"], (), r_p)
    gl["w_ple_gate"] = g_p["w_ple_gate"].reshape(N_CHIPS, D_MODEL // N_CHIPS, D_MODEL)
    gl["w_ple"] = g_p["w_ple"]
    gl["ple_gate_norm"], gl["ple_norm"] = g_p["ple_gate_norm"][0], g_p["ple_norm"][0]
    dh, dg, dw1, dw3, dw2 = ffn_bwd(f"l{l}b", dh, h2, small["ffn_b_norm"][l], r_b,
                                    Gl["ffn_b_w1"], Gl["ffn_b_w3"], Gl["ffn_b_w2"], ())
    gl["ffn_b_norm"] = dg[0]
    gl["ffn_b_w1"], gl["ffn_b_w3"], gl["ffn_b_w2"] = slabs(dw1), slabs(dw3), slabs(dw2)
    dh, g_m = mixer_bwd(f"l{l}", dh, h1, w, tabs, Gl["w_out"], (), r_m)
    gl.update(mixer_grads_to_shards(g_m))
    dw_after = None if before_ffn_a is None else before_ffn_a(gl)
    dh, dg, dw1, dw3, dw2 = ffn_bwd(f"l{l}a", dh, h0, small["ffn_a_norm"][l], r_a,
                                    Gl["ffn_a_w1"], Gl["ffn_a_w3"], Gl["ffn_a_w2"], (), dw_after=dw_after)
    gl["ffn_a_norm"] = dg[0]
    gl["ffn_a_w1"], gl["ffn_a_w3"], gl["ffn_a_w2"] = slabs(dw1), slabs(dw3), slabs(dw2)
    return dh, gl


MESH = pl.DeviceIdType.MESH
HBM_SPEC = pl.BlockSpec(memory_space=pltpu.HBM)


def _place():
    x, y, c = lax.axis_index("x"), lax.axis_index("y"), lax.axis_index("c")
    others = [(1 - x, y), (x, 1 - y), (1 - x, 1 - y)]
    return x, y, c, 2 * x + y, others


def prep_shard(name, w, layer, rows_pad, place, after=None):
    _, ks, n = w.shape
    ksp = ks + rows_pad
    tc = 512 if n % 512 == 0 else n
    deps = [] if after is None else [after]

    def body(place_ref, x_ref, *rest):
        o_ref = rest[-1]
        o_ref[:ks] = x_ref[...].astype(BF16)
        if rows_pad:
            o_ref[ks:] = jnp.zeros((rows_pad, tc), BF16)

    return pl.pallas_call(
        body, name=name,
        grid_spec=pltpu.PrefetchScalarGridSpec(
            num_scalar_prefetch=1, grid=(n // tc,),
            in_specs=[pl.BlockSpec((None, ks, tc), lambda i, s: (layer, 0, i))] + [ANY_SPEC] * len(deps),
            out_specs=pl.BlockSpec((None, ksp, tc), lambda i, s: (s[0], 0, i))),
        out_shape=jax.ShapeDtypeStruct((N_CHIPS, ksp, n), BF16),
        compiler_params=_params(("parallel",)),
    )(place, w, *deps)


def exchange_halves(name, grads):
    n = len(grads)

    def body(*refs):
        d_refs, r_refs = refs[:n], refs[n:2 * n]
        send, recv = refs[2 * n:]
        x, y, c, _, _ = _place()
        cps = []
        for w in range(n):
            half = grads[w].shape[1] // 2
            cps.append(pltpu.make_async_remote_copy(
                src_ref=d_refs[w].at[pl.ds(0, N_CHIPS), pl.ds((1 - c) * half, half)], dst_ref=r_refs[w],
                send_sem=send.at[w], recv_sem=recv.at[w], device_id=(x, y, 1 - c), device_id_type=MESH))
        for cp in cps:
            cp.start()
        for cp in cps:
            cp.wait()

    return pl.pallas_call(
        body, name=name, in_specs=[HBM_SPEC] * n, out_specs=[HBM_SPEC] * n,
        out_shape=[jax.ShapeDtypeStruct((N_CHIPS, g.shape[1] // 2, g.shape[2]), g.dtype) for g in grads],
        scratch_shapes=[pltpu.SemaphoreType.DMA((n,))] * 2,
    )(*grads)


def share_halves(name, fulls):
    n = len(fulls)

    def body(*refs):
        o_refs = refs[n:2 * n]
        send, recv = refs[2 * n:]
        x, y, c, _, _ = _place()
        cps = []
        for w in range(n):
            kh = fulls[w].shape[0] // 2
            half = o_refs[w].at[pl.ds(c * kh, kh)]
            cps.append(pltpu.make_async_remote_copy(src_ref=half, dst_ref=half, send_sem=send.at[w], recv_sem=recv.at[w],
                                                    device_id=(x, y, 1 - c), device_id_type=MESH))
        for cp in cps:
            cp.start()
        for cp in cps:
            cp.wait()

    return pl.pallas_call(
        body, name=name, in_specs=[HBM_SPEC] * n, out_specs=[HBM_SPEC] * n,
        out_shape=[jax.ShapeDtypeStruct(f.shape, f.dtype) for f in fulls],
        input_output_aliases={w: w for w in range(n)},
        scratch_shapes=[pltpu.SemaphoreType.DMA((n,))] * 2,
    )(*fulls)


SEM_SPEC = pl.BlockSpec(memory_space=pltpu.SEMAPHORE)
ANY_SPEC = pl.BlockSpec(memory_space=pl.ANY)
DATAFLOW = pltpu.SideEffectType.DATAFLOW_SIDE_EFFECTING


def _hbm(x):
    return pltpu.with_memory_space_constraint(x, pltpu.HBM)


def _start_call(name, slots, after, issue):
    n = len(slots)
    deps = [] if after is None else [after]
    nd = len(deps)

    def body(*refs):
        issue(refs[n + nd + 2:2 * n + nd + 2], refs[n + nd], refs[n + nd + 1])
        token = refs[2 * n + nd + 2]
        token[...] = jnp.zeros_like(token)

    outs = pl.pallas_call(
        body, name=name,
        in_specs=[HBM_SPEC] * n + [ANY_SPEC] * nd,
        out_specs=(SEM_SPEC, SEM_SPEC, *([HBM_SPEC] * n), pl.BlockSpec(memory_space=pltpu.VMEM)),
        out_shape=(pltpu.SemaphoreType.DMA((n,)), pltpu.SemaphoreType.DMA((n,)),
                   *[pltpu.HBM(s.shape, s.dtype) for s in slots], jax.ShapeDtypeStruct((8, 128), F32)),
        input_output_aliases={w: w + 2 for w in range(n)},
        compiler_params=pltpu.CompilerParams(has_side_effects=DATAFLOW),
    )(*[_hbm(s) for s in slots], *deps)
    return outs[0], outs[1], list(outs[2:2 + n]), outs[2 + n]


def gather_start(name, slots, after, both_cores):
    def issue(g_refs, send, recv):
        x, y, c, jme, others = _place()
        for w in range(len(slots)):
            kh = slots[w].shape[1] // 2
            mine = g_refs[w].at[jme, pl.ds(c * kh, kh)]
            for (px, py) in others:
                for core in ((0, 1) if both_cores else (c,)):
                    pltpu.make_async_remote_copy(src_ref=mine, dst_ref=mine, send_sem=send.at[w], recv_sem=recv.at[w],
                                                 device_id=(px, py, core), device_id_type=MESH).start()

    return _start_call(name, slots, after, issue)


def forward_start(name, slots):
    def issue(g_refs, send, recv):
        x, y, c, _, others = _place()
        for w in range(len(slots)):
            kh = slots[w].shape[1] // 2
            for (px, py) in others:
                blk = g_refs[w].at[2 * px + py, pl.ds(c * kh, kh)]
                pltpu.make_async_remote_copy(src_ref=blk, dst_ref=blk, send_sem=send.at[w], recv_sem=recv.at[w],
                                             device_id=(x, y, 1 - c), device_id_type=MESH).start()

    return _start_call(name, slots, None, issue)


def gather_wait(name, send, recv, flying, after, halves):
    n = len(flying)

    def body(*refs):
        send_ref, recv_ref = refs[n], refs[n + 1]
        g_refs = refs[n + 3:]
        x, y, c, _, _ = _place()
        for w in range(n):
            rows = flying[w].shape[1] * halves // 6
            many = g_refs[w].at[pl.ds(0, 3), pl.ds(0, rows)]
            cp = pltpu.make_async_remote_copy(src_ref=many, dst_ref=many, send_sem=send_ref.at[w], recv_sem=recv_ref.at[w],
                                              device_id=(x, y, 1 - c), device_id_type=MESH)
            cp.wait_send()
            cp.wait_recv()

    return pl.pallas_call(
        body, name=name,
        in_specs=[HBM_SPEC] * n + [SEM_SPEC, SEM_SPEC, ANY_SPEC],
        out_specs=[HBM_SPEC] * n,
        out_shape=[pltpu.HBM(s.shape, s.dtype) for s in flying],
        input_output_aliases={w: w for w in range(n)},
        compiler_params=pltpu.CompilerParams(has_side_effects=DATAFLOW),
    )(*flying, send, recv, after)


def scatter_start(name, parts):
    n = len(parts)

    def body(*refs):
        p_refs, q_refs = refs[2 * n + 2:3 * n + 2], refs[3 * n + 2:4 * n + 2]
        send, recv, token = refs[2 * n], refs[2 * n + 1], refs[4 * n + 2]
        x, y, c, jme, others = _place()
        for w in range(n):
            for (px, py) in others:
                pltpu.make_async_remote_copy(
                    src_ref=p_refs[w].at[2 * px + py], dst_ref=q_refs[w].at[jme], send_sem=send.at[w], recv_sem=recv.at[w],
                    device_id=(px, py, c), device_id_type=MESH).start()
        token[...] = jnp.zeros_like(token)

    lands = [_hbm(lax.empty(p.shape, p.dtype)) for p in parts]
    outs = pl.pallas_call(
        body, name=name,
        in_specs=[HBM_SPEC] * (2 * n),
        out_specs=(SEM_SPEC, SEM_SPEC, *([HBM_SPEC] * (2 * n)), pl.BlockSpec(memory_space=pltpu.VMEM)),
        out_shape=(pltpu.SemaphoreType.DMA((n,)), pltpu.SemaphoreType.DMA((n,)),
                   *[pltpu.HBM(p.shape, p.dtype) for p in parts], *[pltpu.HBM(p.shape, p.dtype) for p in parts],
                   jax.ShapeDtypeStruct((8, 128), F32)),
        input_output_aliases={w: w + 2 for w in range(2 * n)},
        compiler_params=pltpu.CompilerParams(has_side_effects=DATAFLOW),
    )(*[_hbm(p) for p in parts], *lands)
    return outs[0], outs[1], list(outs[2:2 + n]), list(outs[2 + n:2 + 2 * n]), outs[2 + 2 * n]


def scatter_wait(name, send, recv, parts, lands, after):
    n = len(parts)

    def body(*refs):
        send_ref, recv_ref = refs[2 * n], refs[2 * n + 1]
        q_refs = refs[3 * n + 3:]
        x, y, c, _, _ = _place()
        for w in range(n):
            three = q_refs[w].at[pl.ds(0, 3)]
            cp = pltpu.make_async_remote_copy(src_ref=three, dst_ref=three, send_sem=send_ref.at[w], recv_sem=recv_ref.at[w],
                                              device_id=(x, y, 1 - c), device_id_type=MESH)
            cp.wait_send()
            cp.wait_recv()

    outs = pl.pallas_call(
        body, name=name,
        in_specs=[HBM_SPEC] * (2 * n) + [SEM_SPEC, SEM_SPEC, ANY_SPEC],
        out_specs=[HBM_SPEC] * (2 * n),
        out_shape=[pltpu.HBM(p.shape, p.dtype) for p in parts] * 2,
        input_output_aliases={w: w for w in range(2 * n)},
        compiler_params=pltpu.CompilerParams(has_side_effects=DATAFLOW),
    )(*parts, *lands, send, recv, after)
    return list(outs[:n]), list(outs[n:])


def allreduce_small(v):
    R = v.shape[0]

    def body(v_ref, o_ref, sib_ref, mine_ref, all_ref, d_send, d_recv, i_send, i_recv):
        x, y, c, jme, others = _place()
        swap = pltpu.make_async_remote_copy(src_ref=v_ref, dst_ref=sib_ref, send_sem=d_send, recv_sem=d_recv,
                                            device_id=(x, y, 1 - c), device_id_type=MESH)
        swap.start()
        swap.wait()
        mine_ref[...] = v_ref[...] + sib_ref[...]
        for (px, py) in others:
            pltpu.make_async_remote_copy(src_ref=mine_ref, dst_ref=all_ref.at[jme], send_sem=i_send, recv_sem=i_recv,
                                         device_id=(px, py, c), device_id_type=MESH).start()
        three = all_ref.at[pl.ds(0, 3)]
        wait3 = pltpu.make_async_remote_copy(src_ref=three, dst_ref=three, send_sem=i_send, recv_sem=i_recv,
                                             device_id=(x, y, c), device_id_type=MESH)
        wait3.wait_recv()
        wait3.wait_send()
        all_ref[jme] = mine_ref[...]
        o_ref[...] = ((all_ref[0] + all_ref[1]) + all_ref[2]) + all_ref[3]

    vm = pl.BlockSpec(memory_space=pltpu.VMEM)
    return pl.pallas_call(
        body, name="allreduce_small", in_specs=[vm], out_specs=vm,
        out_shape=jax.ShapeDtypeStruct(v.shape, F32),
        scratch_shapes=[pltpu.VMEM((R, 128), F32), pltpu.VMEM((R, 128), F32), pltpu.VMEM((N_CHIPS, R, 128), F32),
                        pltpu.SemaphoreType.DMA, pltpu.SemaphoreType.DMA, pltpu.SemaphoreType.DMA, pltpu.SemaphoreType.DMA],
        compiler_params=pltpu.CompilerParams(vmem_limit_bytes=VMEM_LIMIT_BYTES),
    )(v)


def _row_tile(rows, width, mult=16, cap=3 << 20):
    best = rows
    for t in range(mult, rows + 1, mult):
        if rows % t == 0 and t * width * 4 <= cap:
            best = t
    return best


def add_sibling(name, mine, theirs, place):
    _, kh, ns = theirs.shape
    tr = _row_tile(kh, ns)
    nblk = kh // tr

    def body(place_ref, a_ref, b_ref, o_ref):
        o_ref[...] = (a_ref[...].astype(F32) + b_ref[...].astype(F32)).astype(BF16)

    return pl.pallas_call(
        body, name=name,
        grid_spec=pltpu.PrefetchScalarGridSpec(
            num_scalar_prefetch=1, grid=(N_CHIPS, nblk),
            in_specs=[pl.BlockSpec((None, tr, ns), lambda j, i, s: (j, s[1] * nblk + i, 0)),
                      pl.BlockSpec((None, tr, ns), lambda j, i, s: (j, i, 0))],
            out_specs=pl.BlockSpec((None, tr, ns), lambda j, i, s: (j, i, 0))),
        out_shape=jax.ShapeDtypeStruct(theirs.shape, BF16),
        compiler_params=_params(("parallel", "parallel")),
    )(place, mine, theirs)


def add_chips(name, q, p, place):
    _, kh, ns = q.shape
    tr = _row_tile(kh, ns)
    nblk = kh // tr

    def body(place_ref, *refs):
        q_refs, own_ref, o_ref = refs[:N_CHIPS], refs[N_CHIPS], refs[-1]
        jme = place_ref[0]
        tot = None
        for j in range(N_CHIPS):
            v = jnp.where(jme == j, own_ref[...], q_refs[j][...]).astype(F32)
            tot = v if tot is None else tot + v
        o_ref[...] = tot

    def q_ix(j):
        return lambda i, s: (jnp.where(s[0] == j, (j + 1) % N_CHIPS, j), i, 0)

    in_specs = [pl.BlockSpec((None, tr, ns), q_ix(j)) for j in range(N_CHIPS)]
    in_specs.append(pl.BlockSpec((None, tr, ns), lambda i, s: (s[0], i, 0)))
    return pl.pallas_call(
        body, name=name,
        grid_spec=pltpu.PrefetchScalarGridSpec(
            num_scalar_prefetch=1, grid=(nblk,), in_specs=in_specs,
            out_specs=pl.BlockSpec((tr, ns), lambda i, s: (s[1] * nblk + i, 0))),
        out_shape=jax.ShapeDtypeStruct((2 * kh, ns), F32),
        compiler_params=_params(("parallel",)),
    )(place, q, q, q, q, p)


ADAM_LR, ADAM_B1, ADAM_B2, ADAM_EPS, ADAM_WD, ADAM_STEP = 0.001, 0.9, 0.999, 1e-08, 0.01, 10


def adamw(name, w, g, m, v, layer, prev=None):
    _, k, ns = w.shape
    nsp = g.shape[1]
    tr = _row_tile(k, nsp, mult=8, cap=2 << 20)

    def body(w_ref, g_ref, m_ref, v_ref, *rest):
        go_ref, d_ref, mo_ref, vo_ref = rest[-4:]
        gv = g_ref[:, :ns] if nsp != ns else g_ref[...]
        mn = ADAM_B1 * m_ref[...] + (1.0 - ADAM_B1) * gv
        vn = ADAM_B2 * v_ref[...] + (1.0 - ADAM_B2) * (gv * gv)
        m_hat = mn / (1.0 - ADAM_B1 ** ADAM_STEP)
        v_hat = vn / (1.0 - ADAM_B2 ** ADAM_STEP)
        go_ref[...] = gv
        d_ref[...] = -ADAM_LR * (m_hat / (jnp.sqrt(v_hat) + ADAM_EPS) + ADAM_WD * w_ref[...])
        mo_ref[...] = mn
        vo_ref[...] = vn

    blk = pl.BlockSpec((None, tr, ns), lambda i: (layer, i, 0))
    gblk = pl.BlockSpec((tr, nsp), lambda i: (i, 0))
    args, in_specs, aliases = [w, g, m, v], [blk, gblk, blk, blk], {}
    if prev is not None:
        args += list(prev)
        in_specs += [pl.BlockSpec(memory_space=pl.ANY)] * 4
        aliases = {4 + i: i for i in range(4)}
    return pl.pallas_call(
        body, name=name, grid=(k // tr,), in_specs=in_specs, out_specs=[blk] * 4,
        out_shape=[jax.ShapeDtypeStruct(w.shape, F32)] * 4, input_output_aliases=aliases,
        compiler_params=_params(("parallel",)),
    )(*args)


WEIGHTS = ("ffn_a_norm", "ffn_a_w1", "ffn_a_w3", "ffn_a_w2", "mix_norm", "w_in", "q_a_norm", "w_uq", "kv_a_norm", "w_ukv",
           "q_norm", "k_norm", "gm_v_norm", "gm_ws", "gm_bs", "attn_out_norm", "gm_out_norm", "w_out", "ffn_b_norm",
           "ffn_b_w1", "ffn_b_w3", "ffn_b_w2", "ple_gate_norm", "w_ple_gate", "w_ple", "ple_norm")
_FF = FF_PAD - FF_SHARD
BIG = {"ffn_a_w1": _FF, "ffn_a_w3": _FF, "ffn_a_w2": _FF, "ffn_b_w1": _FF, "ffn_b_w3": _FF, "ffn_b_w2": _FF,
       "w_in": IN_SHARD_PAD - IN_SHARD, "w_uq": 0, "w_ukv": 0, "w_ple": 0, "w_out": 0, "w_ple_gate": 0}
TRANSPOSED = ("ffn_a_w1", "ffn_a_w3", "ffn_b_w1", "ffn_b_w3", "w_in")
SMALL = tuple(n for n in WEIGHTS if n not in BIG)
PACK = 1024


def _pack_small(d):
    parts = []
    for n in SMALL:
        flat = d[n].reshape(-1)
        parts.append(jnp.pad(flat, (0, (-flat.shape[0]) % PACK)))
    return jnp.concatenate(parts).reshape(-1, 128)


def _unpack_small(buf, like):
    flat = buf.reshape(-1)
    out, pos = {}, 0
    for n in SMALL:
        size = math.prod(like[n].shape)
        out[n] = flat[pos:pos + size].reshape(like[n].shape)
        pos += size + (-size) % PACK
    return out


def kernel(*args):
    names = (("x", "p", "positions") + WEIGHTS + ("loss_target",) + tuple("m_" + n for n in WEIGHTS)
             + tuple("v_" + n for n in WEIGHTS))
    a = dict(zip(names, args, strict=True))
    x, p, positions, target = a["x"][0], a["p"][:, 0], a["positions"][0], a["loss_target"][0]
    for n in TRANSPOSED:
        for pre in ("", "m_", "v_"):
            a[pre + n] = jnp.swapaxes(a[pre + n], 1, 2)

    place = jnp.stack([2 * lax.axis_index("x") + lax.axis_index("y"), lax.axis_index("c")]).astype(jnp.int32)
    small = {n: a[n] for n in SMALL}
    tabs = rope_tables(positions)
    first = ("ffn_a_w1", "ffn_a_w3", "ffn_a_w2")
    rest = tuple(n for n in BIG if n not in first)
    prep = lambda n, l, after: prep_shard(f"prep_{n}_{l}", a[n], l, BIG[n], place, after)

    def finish_gather(tag, started, after):
        send, recv, flying, _ = started
        arrived = gather_wait(f"gather_{tag}_wait", send, recv, flying, after, 3)
        send, recv, flying, token = forward_start(f"forward_{tag}_start", arrived)
        return gather_wait(f"forward_{tag}_wait", send, recv, flying, token, 3)

    ga = gather_start("gather_l0a_start", [prep(n, 0, None) for n in first], None, False)
    gb = gather_start("gather_l0b_start", [prep(n, 0, ga[3]) for n in rest], None, False)
    slots1 = [prep(n, 1, gb[3]) for n in BIG]
    G0 = dict(zip(first, finish_gather("l0a", ga, slots1[-1])))
    later = {}

    def after_first_ffn(h1, Gl, small_):
        later["G0"] = {**Gl, **dict(zip(rest, finish_gather("l0b", gb, h1)))}
        later["g1"] = gather_start("gather_l1_start", slots1, later["G0"]["w_uq"], True)
        return later["G0"], {**small_, "mix_norm": small_["mix_norm"] + later["g1"][3][0, 0]}

    h, saved0 = layer_fwd(0, x, p[0], G0, small, tabs, after_first_ffn)
    G0 = later["G0"]
    G1 = dict(zip(BIG, gather_wait("gather_l1_wait", *later["g1"][:3], h, 6)))
    h, saved1 = layer_fwd(1, h, p[1], G1, small, tabs)
    dh, loss = loss_grad(h, target)
    loss = lax.psum(loss, ("x", "y", "c"))

    def start_reduce(tag, names, gl):
        mine = [gl[n] for n in names]
        theirs = exchange_halves(f"exchange_{tag}", mine)
        parts = [add_sibling(f"add_sibling_{n}_{tag}", d, r, place) for n, d, r in zip(names, mine, theirs)]
        return scatter_start(f"scatter_{tag}_start", parts)

    def finish_reduce(tag, names, started, after):
        send, recv, parts, lands, _ = started
        parts, slabs = scatter_wait(f"scatter_{tag}_wait", send, recv, parts, lands, after)
        halves = [add_chips(f"add_chips_{n}_{tag}", q, pt, place) for n, q, pt in zip(names, slabs, parts)]
        return dict(zip(names, share_halves(f"share_{tag}", halves)))

    def update(names, full, layer, prev):
        return {n: adamw(f"adamw_{n}_{layer}", a[n], full[n], a["m_" + n], a["v_" + n], layer, prev and prev[n])
                for n in names}

    group_b = ("ffn_a_w1", "ffn_a_w3", "ffn_a_w2")
    group_a = tuple(n for n in BIG if n not in group_b)
    grads = [None, None]
    dh, grads[1] = layer_bwd(1, dh, p[1], G1, small, tabs, saved1)
    red1 = start_reduce("l1", tuple(BIG), grads[1])
    w0 = {**saved0[0], "ple_gate_norm": saved0[0]["ple_gate_norm"] + red1[4][0, 0]}
    started = {}

    def before_ffn_a(gl):
        started["a"] = start_reduce("l0a", group_a, gl)
        return started["a"][4]

    gx, grads[0] = layer_bwd(0, dh, p[0], G0, small, tabs, (w0,) + saved0[1:], before_ffn_a)
    started["b"] = start_reduce("l0b", group_b, grads[0])
    outs1 = update(BIG, finish_reduce("l1", tuple(BIG), red1, started["b"][4]), 1, None)
    behind = outs1[group_b[-1]][1]
    full0 = {**finish_reduce("l0a", group_a, started["a"], behind), **finish_reduce("l0b", group_b, started["b"], behind)}
    outs0 = update(BIG, full0, 0, outs1)

    out_g, out_d, out_m, out_v = {}, {}, {}, {}
    for n in BIG:
        outs = [jnp.swapaxes(o, 1, 2) for o in outs0[n]] if n in TRANSPOSED else outs0[n]
        out_g[n], out_d[n], out_m[n], out_v[n] = outs

    gs = allreduce_small(_pack_small({n: jnp.stack([grads[0][n], grads[1][n]]) for n in SMALL}))
    rows = gs.shape[0] // 2
    packed = [_pack_small(d).reshape(2, rows, 128) for d in
              (small, {n: a["m_" + n] for n in SMALL}, {n: a["v_" + n] for n in SMALL})]
    gs = gs.reshape(2, rows, 128)
    sm = adamw("adamw_small_0", packed[0], gs[0], packed[1], packed[2], 0)
    sm = adamw("adamw_small_1", packed[0], gs[1], packed[1], packed[2], 1, sm)
    for dst, buf in zip((out_g, out_d, out_m, out_v), sm):
        dst.update(_unpack_small(buf, small))

    return (loss, gx[None], *[out_g[n] for n in WEIGHTS], *[out_d[n] for n in WEIGHTS],
            *[out_m[n] for n in WEIGHTS], *[out_v[n] for n in WEIGHTS])
```

```python
import math

import jax
import jax.numpy as jnp
from jax import lax
from jax.experimental import pallas as pl
from jax.experimental.pallas import tpu as pltpu

F32 = jnp.float32
BF16 = jnp.bfloat16

D_MODEL = 2048
D_FF = 5504
N_CHIPS = 4
FF_SHARD = D_FF // N_CHIPS
FF_PAD = 1408
FF_P = N_CHIPS * FF_PAD
HEADS = 8
QK_NOPE = 128
QK_ROPE = 64
QK_DIM = 192
HEAD_PAD = 256
V_DIM = 128
Q_RANK = 512
KV_RANK = 256
ATTN_W = 1024
GM_W = 1024
GROUPS = 8
CHUNK = 128
PLE_DIM = 256
IN_P = 3072
IN_SHARD = 720
IN_SHARD_PAD = 736
EPS = 1e-6
ROPE_BASE = 10000.0
ATTN_SCALE = QK_DIM ** -0.5
VMEM_LIMIT_BYTES = 56 * 1024 * 1024


def _params(sem):
    return pltpu.CompilerParams(dimension_semantics=sem, vmem_limit_bytes=VMEM_LIMIT_BYTES)


def _bf(x):
    return x if x.dtype == BF16 else x.astype(BF16)


def _sigmoid(x):
    return 1.0 / (1.0 + jnp.exp(-x))


_GELU_C = math.sqrt(2.0 / math.pi)


def _gelu(x):
    t = jnp.tanh(_GELU_C * (x + 0.044715 * x * x * x))
    return 0.5 * x * (1.0 + t)


def _gelu_grad(x):
    t = jnp.tanh(_GELU_C * (x + 0.044715 * x * x * x))
    return 0.5 * (1.0 + t) + 0.5 * x * (1.0 - t * t) * _GELU_C * (1.0 + 3 * 0.044715 * x * x)


def op_a(a, tm, tk):
    return (a, (tm, tk), lambda i, j, k: (i, k), 1)


def op_at(a, tm, tk):
    return (a, (tk, tm), lambda i, j, k: (k, i), 0)


def op_b(b, tk, tn):
    return (b, (tk, tn), lambda i, j, k: (k, j), 0)


def op_bt(b, tk, tn):
    return (b, (tn, tk), lambda i, j, k: (j, k), 1)


def op_b_cols(g, pre, tk, tn):
    nb = g.shape[-1] // tn
    none = (None,) * (1 + len(pre))
    return (g, none + (tk, tn), lambda i, j, k: (j // nb,) + tuple(pre) + (k, j % nb), 0)


def op_b_rows(g, pre, tk, tn, koff=0):
    nb = g.shape[-2] // tk
    none = (None,) * (1 + len(pre))
    return (g, none + (tk, tn), lambda i, j, k: ((k + koff) // nb,) + tuple(pre) + ((k + koff) % nb, j), 0)


def op_b_rows_t(g, pre, tk, tn):
    nb = g.shape[-2] // tn
    none = (None,) * (1 + len(pre))
    return (g, none + (tn, tk), lambda i, j, k: (j // nb,) + tuple(pre) + (j % nb, k), 1)


def tile_mn(x, tm, tn):
    return (x, (tm, tn), lambda i, j: (i, j))


def out_mn(M, N, tm, tn, dtype):
    return (jax.ShapeDtypeStruct((M, N), dtype), (tm, tn), lambda i, j: (i, j))


def out_cols(M, ns, tm, tn, dtype):
    nb = ns // tn
    return (jax.ShapeDtypeStruct((N_CHIPS, M, ns), dtype), (None, tm, tn), lambda i, j: (j // nb, i, j % nb))


def matmul(name, grid_mnk, a_ops, b_ops, terms, n_acc, extras, outs, epilogue, acc_tile, n_outer=False, after=None):
    gm, gn, gk = grid_mnk
    na, nb, nx, no = len(a_ops), len(b_ops), len(extras), len(outs)
    nd = 0 if after is None else 1

    def body(*refs):
        a_refs, b_refs = refs[:na], refs[na:na + nb]
        x_refs = refs[na + nb:na + nb + nx]
        o_refs = refs[na + nb + nx + nd:na + nb + nx + nd + no]
        acc_refs = refs[na + nb + nx + nd + no:]
        k = pl.program_id(2)

        @pl.when(k == 0)
        def _():
            for acc in acc_refs:
                acc[...] = jnp.zeros_like(acc)

        for ai, bi, ci in terms:
            dims = (((a_ops[ai][3],), (b_ops[bi][3],)), ((), ()))
            acc_refs[ci][...] += lax.dot_general(_bf(a_refs[ai][...]), _bf(b_refs[bi][...]), dims,
                                                 preferred_element_type=F32)

        @pl.when(k == gk - 1)
        def _():
            res = epilogue([acc[...] for acc in acc_refs], [x[...] for x in x_refs])
            for o, v in zip(o_refs, res):
                o[...] = v.astype(o.dtype)

    if n_outer:
        grid = (gn, gm, gk)

        def ix3(f):
            return lambda j, i, k: f(i, j, k)

        def ix2(f):
            return lambda j, i, k: f(i, j)
    else:
        grid = (gm, gn, gk)

        def ix3(f):
            return lambda i, j, k: f(i, j, k)

        def ix2(f):
            return lambda i, j, k: f(i, j)

    in_specs = [pl.BlockSpec(blk, ix3(f)) for (_, blk, f, _) in list(a_ops) + list(b_ops)]
    in_specs += [pl.BlockSpec(blk, ix2(f)) for (_, blk, f) in extras]
    in_specs += [pl.BlockSpec(memory_space=pl.ANY)] * nd
    out_specs = [pl.BlockSpec(blk, ix2(f)) for (_, blk, f) in outs]
    return pl.pallas_call(
        body,
        name=name,
        grid=grid,
        in_specs=in_specs,
        out_specs=out_specs,
        out_shape=[s for (s, _, _) in outs],
        scratch_shapes=[pltpu.VMEM(acc_tile, F32) for _ in range(n_acc)],
        compiler_params=_params(("parallel", "parallel", "arbitrary")),
    )(*[o[0] for o in a_ops], *[o[0] for o in b_ops], *[x[0] for x in extras], *([after] * nd))


def _acc0(accs, xs):
    return (accs[0],)


def mm_simple(name, a, b_op_fn, M, N, K, tm, tn, tk, out_dtype=F32, a_t=False, extras=(), epilogue=_acc0, outs=None):
    a_op = op_at(a, tm, tk) if a_t else op_a(a, tm, tk)
    outs = outs or [out_mn(M, N, tm, tn, out_dtype)]
    return matmul(name, (M // tm, N // tn, K // tk), [a_op], [b_op_fn(tk, tn)], [(0, 0, 0)], 1,
                  list(extras), outs, epilogue, (tm, tn))


def rms_fwd(name, x, g, width, col_blk=0, tm=256, out_dtype=BF16):
    T = x.shape[0]

    def body(x_ref, g_ref, o_ref):
        xv = x_ref[...].astype(F32)
        r = lax.rsqrt(jnp.mean(xv * xv, axis=-1, keepdims=True) + EPS)
        o_ref[...] = (xv * r * g_ref[...]).astype(o_ref.dtype)

    return pl.pallas_call(
        body, name=name, grid=(T // tm,),
        in_specs=[pl.BlockSpec((tm, width), lambda i: (i, col_blk)), pl.BlockSpec((1, width), lambda i: (0, 0))],
        out_specs=pl.BlockSpec((tm, width), lambda i: (i, 0)),
        out_shape=jax.ShapeDtypeStruct((T, width), out_dtype),
        compiler_params=_params(("parallel",)),
    )(x, g.reshape(1, width))


def rms_bwd(name, x, g, dn, width, col_blk=0, dres=None, tm=256, with_delta=False, bf16_copy=False):
    T = x.shape[0]
    has_res = dres is not None

    def body(*refs):
        x_ref, g_ref, dn_ref = refs[:3]
        pos = 3
        res_ref = None
        if has_res:
            res_ref = refs[pos]
            pos += 1
        dx_ref, dg_ref = refs[pos], refs[pos + 1]
        delta_ref = refs[pos + 2] if with_delta else None
        lo_ref = refs[-1] if bf16_copy else None
        i = pl.program_id(0)
        xv = x_ref[...].astype(F32)
        r = lax.rsqrt(jnp.mean(xv * xv, axis=-1, keepdims=True) + EPS)
        xh = xv * r
        d = dn_ref[...].astype(F32)
        gd = d * g_ref[...]
        dx = r * (gd - xh * jnp.mean(gd * xh, axis=-1, keepdims=True))
        if has_res:
            dx = dx + res_ref[...]
        dx_ref[...] = dx.astype(dx_ref.dtype)
        if bf16_copy:
            lo_ref[...] = dx.astype(BF16)
        part = jnp.sum(d * xh, axis=0, keepdims=True)

        @pl.when(i == 0)
        def _():
            dg_ref[...] = part

        @pl.when(i > 0)
        def _():
            dg_ref[...] += part

        if with_delta:
            for h in range(width // 128):
                sl = slice(h * 128, (h + 1) * 128)
                s = jnp.sum(dx[:, sl] * xv[:, sl], axis=-1, keepdims=True)
                delta_ref[:, sl] = jnp.broadcast_to(s, (tm, 128))

    in_specs = [pl.BlockSpec((tm, width), lambda i: (i, col_blk)), pl.BlockSpec((1, width), lambda i: (0, 0)),
                pl.BlockSpec((tm, width), lambda i: (i, 0))]
    args = [x, g.reshape(1, width), dn]
    if has_res:
        in_specs.append(pl.BlockSpec((tm, width), lambda i: (i, 0)))
        args.append(dres)
    out_specs = [pl.BlockSpec((tm, width), lambda i: (i, 0)), pl.BlockSpec((1, width), lambda i: (0, 0))]
    out_shape = [jax.ShapeDtypeStruct((T, width), F32), jax.ShapeDtypeStruct((1, width), F32)]
    if with_delta:
        out_specs.append(pl.BlockSpec((tm, width), lambda i: (i, 0)))
        out_shape.append(jax.ShapeDtypeStruct((T, width), F32))
    if bf16_copy:
        out_specs.append(pl.BlockSpec((tm, width), lambda i: (i, 0)))
        out_shape.append(jax.ShapeDtypeStruct((T, width), BF16))
    return pl.pallas_call(
        body, name=name, grid=(T // tm,), in_specs=in_specs, out_specs=out_specs, out_shape=out_shape,
        compiler_params=_params(("arbitrary",)),
    )(*args)


def ffn_fwd(tag, h, g, w1g, w3g, w2g, pre):
    T = h.shape[0]
    n = rms_fwd(f"{tag}_rms", h, g, D_MODEL)
    tm, tn = 512, FF_PAD

    def up_epi(accs, xs):
        a1, a3 = accs
        return a1, a3, a1 * _sigmoid(a1) * a3

    a1, a3, s = matmul(
        f"{tag}_up", (T // tm, FF_P // tn, 1),
        [op_a(n, tm, D_MODEL)], [op_b_rows_t(w1g, pre, D_MODEL, tn), op_b_rows_t(w3g, pre, D_MODEL, tn)],
        [(0, 0, 0), (0, 1, 1)], 2, [],
        [out_mn(T, FF_P, tm, tn, BF16)] * 3, up_epi, (tm, tn), n_outer=True)

    tm2, tn2 = 1024, 1024
    (h_out,) = matmul(
        f"{tag}_down", (T // tm2, D_MODEL // tn2, N_CHIPS),
        [op_a(s, tm2, FF_PAD)], [op_b_rows(w2g, pre, FF_PAD, tn2)],
        [(0, 0, 0)], 1, [tile_mn(h, tm2, tn2)],
        [out_mn(T, D_MODEL, tm2, tn2, F32)], lambda accs, xs: (xs[0] + 0.5 * accs[0],), (tm2, tn2))
    return h_out, (n, a1, a3, s)


def ffn_bwd(tag, dh_out, dh_bf, h, g, res, w1g, w3g, w2g, pre, dw_after=None):
    n, a1, a3, s = res
    T = h.shape[0]
    tm, tn = 512, FF_PAD

    def act_epi(accs, xs):
        ds = 0.5 * accs[0]
        x1, x3 = xs[0].astype(F32), xs[1].astype(F32)
        sg = _sigmoid(x1)
        silu = x1 * sg
        return ds * x3 * (sg + silu * (1.0 - sg)), ds * silu

    da1, da3 = matmul(
        f"{tag}_dact", (T // tm, FF_P // tn, 1),
        [op_a(dh_bf, tm, D_MODEL)], [op_b_rows_t(w2g, pre, D_MODEL, tn)],
        [(0, 0, 0)], 1, [tile_mn(a1, tm, tn), tile_mn(a3, tm, tn)],
        [out_mn(T, FF_P, tm, tn, BF16)] * 2, act_epi, (tm, tn), n_outer=True)

    tk = 1024

    def dw_t(nm, left, right, scale):
        (dw,) = matmul(
            f"{tag}_{nm}", (FF_P // FF_PAD, D_MODEL // 1024, T // tk),
            [op_at(left, FF_PAD, tk)], [op_b(right, tk, 1024)],
            [(0, 0, 0)], 1, [], [out_mn(FF_P, D_MODEL, FF_PAD, 1024, BF16)],
            lambda accs, xs: (scale * accs[0],), (FF_PAD, 1024), after=dw_after)
        return dw

    dw2 = dw_t("dw2", s, dh_bf, 0.5)
    dw1 = dw_t("dw1", da1, n, 1.0)
    dw3 = dw_t("dw3", da3, n, 1.0)

    tm2, tn2 = 1024, 1024
    (dn,) = matmul(
        f"{tag}_dn", (T // tm2, D_MODEL // tn2, N_CHIPS),
        [op_a(da1, tm2, FF_PAD), op_a(da3, tm2, FF_PAD)],
        [op_b_rows(w1g, pre, FF_PAD, tn2), op_b_rows(w3g, pre, FF_PAD, tn2)],
        [(0, 0, 0), (1, 1, 0)], 1, [], [out_mn(T, D_MODEL, tm2, tn2, F32)], _acc0, (tm2, tn2))
    dh, dg, dh_lo = rms_bwd(f"{tag}_rms_bwd", h, g, dn, D_MODEL, dres=dh_out, bf16_copy=True)
    return dh, dh_lo, dg, dw1, dw3, dw2


def rope_tables(positions):
    inv_freq = ROPE_BASE ** (-jnp.arange(0, QK_ROPE, 2, dtype=F32) / QK_ROPE)
    ang = positions.astype(F32)[:, None] * inv_freq
    cos, sin = jnp.cos(ang), jnp.sin(ang)
    T = positions.shape[0]
    one, zero = jnp.ones((T, QK_NOPE), F32), jnp.zeros((T, 64), F32)
    z32, z128 = jnp.zeros((T, 32), F32), jnp.zeros((T, QK_NOPE), F32)
    c = jnp.concatenate([one, cos, cos, zero], axis=1)
    s1 = jnp.concatenate([z128, -sin, z32, zero], axis=1)
    s2 = jnp.concatenate([z128, z32, sin, zero], axis=1)
    return c, s1, s2


def _rope(y, c, s1, s2):
    return y * c + pltpu.roll(y, HEAD_PAD - 32, 1) * s1 + pltpu.roll(y, 32, 1) * s2


def _rope_t(d, c, s1, s2):
    return d * c + pltpu.roll(d * s1, 32, 1) + pltpu.roll(d * s2, HEAD_PAD - 32, 1)


def _head_norm(x):
    r = lax.rsqrt(jnp.sum(x * x, axis=-1, keepdims=True) * (1.0 / QK_DIM) + EPS)
    return x * r, r


def qk_prep_fwd(tag, q_raw, kk_raw, z_p, gq, gk, tabs, tm=256):
    T = q_raw.shape[0]
    c, s1, s2 = tabs

    def body(q_ref, k_ref, kr_ref, gq_ref, gk_ref, c_ref, s1_ref, s2_ref, qo_ref, ko_ref):
        cv, s1v, s2v = c_ref[...], s1_ref[...], s2_ref[...]
        kr = kr_ref[...]
        for h in range(HEADS):
            sl = slice(h * HEAD_PAD, (h + 1) * HEAD_PAD)
            xh, _ = _head_norm(q_ref[:, sl])
            qo_ref[:, sl] = (_rope(xh * gq_ref[...], cv, s1v, s2v) * ATTN_SCALE).astype(BF16)
            xh, _ = _head_norm(k_ref[:, sl] + kr)
            ko_ref[:, sl] = _rope(xh * gk_ref[...], cv, s1v, s2v).astype(BF16)

    row = lambda i: (i, 0)
    full = pl.BlockSpec((tm, HEADS * HEAD_PAD), row)
    tab = pl.BlockSpec((tm, HEAD_PAD), row)
    vec = pl.BlockSpec((1, HEAD_PAD), lambda i: (0, 0))
    return pl.pallas_call(
        body, name=f"{tag}_qk_prep", grid=(T // tm,),
        in_specs=[full, full, pl.BlockSpec((tm, HEAD_PAD), lambda i: (i, 3)), vec, vec, tab, tab, tab],
        out_specs=[full, full],
        out_shape=[jax.ShapeDtypeStruct((T, HEADS * HEAD_PAD), BF16)] * 2,
        compiler_params=_params(("parallel",)),
    )(q_raw, kk_raw, z_p, gq, gk, c, s1, s2)


def qk_prep_bwd(tag, dq_full, dk_full, q_raw, kk_raw, z_p, gq, gk, tabs, tm=256):
    T = q_raw.shape[0]
    c, s1, s2 = tabs

    def body(dq_ref, dk_ref, q_ref, k_ref, kr_ref, gq_ref, gk_ref, c_ref, s1_ref, s2_ref,
             dqr_ref, dkr_ref, dz_ref, dgq_ref, dgk_ref):
        i = pl.program_id(0)
        cv, s1v, s2v = c_ref[...], s1_ref[...], s2_ref[...]
        kr = kr_ref[...]
        lane = lax.broadcasted_iota(jnp.int32, (tm, HEAD_PAD), 1)
        slot = ((lane >= QK_NOPE) & (lane < QK_DIM)).astype(F32)

        def one(x, g, d):
            xh, r = _head_norm(x)
            dy = _rope_t(d, cv, s1v, s2v)
            gd = dy * g
            dx = r * (gd - xh * (jnp.sum(gd * xh, axis=-1, keepdims=True) * (1.0 / QK_DIM)))
            return dx, jnp.sum(dy * xh, axis=0, keepdims=True)

        dgq = jnp.zeros((1, HEAD_PAD), F32)
        dgk = jnp.zeros((1, HEAD_PAD), F32)
        dz = jnp.zeros((tm, HEAD_PAD), F32)
        for h in range(HEADS):
            sl = slice(h * HEAD_PAD, (h + 1) * HEAD_PAD)
            dx, dg = one(q_ref[:, sl], gq_ref[...], dq_ref[:, sl].astype(F32) * ATTN_SCALE)
            dqr_ref[:, sl] = dx
            dgq = dgq + dg
            dx, dg = one(k_ref[:, sl] + kr, gk_ref[...], dk_ref[:, sl].astype(F32))
            dkr_ref[:, sl] = dx
            dgk = dgk + dg
            dz = dz + dx
        dz_ref[...] = dz * slot

        @pl.when(i == 0)
        def _():
            dgq_ref[...] = dgq
            dgk_ref[...] = dgk

        @pl.when(i > 0)
        def _():
            dgq_ref[...] += dgq
            dgk_ref[...] += dgk

    row = lambda i: (i, 0)
    full = pl.BlockSpec((tm, HEADS * HEAD_PAD), row)
    tab = pl.BlockSpec((tm, HEAD_PAD), row)
    vec = pl.BlockSpec((1, HEAD_PAD), lambda i: (0, 0))
    return pl.pallas_call(
        body, name=f"{tag}_qk_prep_bwd", grid=(T // tm,),
        in_specs=[full, full, full, full, pl.BlockSpec((tm, HEAD_PAD), lambda i: (i, 3)), vec, vec, tab, tab, tab],
        out_specs=[full, full, tab, vec, vec],
        out_shape=[jax.ShapeDtypeStruct((T, HEADS * HEAD_PAD), F32)] * 2
        + [jax.ShapeDtypeStruct((T, HEAD_PAD), F32)] + [jax.ShapeDtypeStruct((1, HEAD_PAD), F32)] * 2,
        compiler_params=_params(("arbitrary",)),
    )(dq_full, dk_full, q_raw, kk_raw, z_p, gq, gk, c, s1, s2)


def attn_fwd(tag, q_full, k_full, vv, blk=512):
    T = q_full.shape[0]
    nb = T // blk
    neg = float(jnp.finfo(jnp.float32).min)

    def body(q_ref, k_ref, v_ref, o_ref, lse_ref, m_ref, l_ref, acc_ref):
        i = pl.program_id(1)
        m_ref[...] = jnp.full_like(m_ref, neg)
        l_ref[...] = jnp.zeros_like(l_ref)
        acc_ref[...] = jnp.zeros_like(acc_ref)
        q = q_ref[...]

        def step(j, masked):
            rows = pl.ds(pl.multiple_of(j * blk, blk), blk)
            s = lax.dot_general(q, k_ref[rows, :], (((1,), (1,)), ((), ())), preferred_element_type=F32)
            if masked:
                row = lax.broadcasted_iota(jnp.int32, (blk, blk), 0)
                col = lax.broadcasted_iota(jnp.int32, (blk, blk), 1)
                s = jnp.where(col <= row, s, neg)
            m_prev = m_ref[...]
            m_new = jnp.maximum(m_prev, jnp.max(s, axis=-1, keepdims=True))
            alpha = jnp.exp(m_prev - m_new)
            p = jnp.exp(s - m_new[:, :1])
            l_ref[...] = alpha * l_ref[...] + jnp.sum(p, axis=-1, keepdims=True)
            acc_ref[...] = alpha * acc_ref[...] + jnp.dot(p.astype(BF16), v_ref[rows, :], preferred_element_type=F32)
            m_ref[...] = m_new

        def off_diagonal(j, carry):
            step(j, False)
            return carry

        lax.fori_loop(0, i, off_diagonal, 0)
        step(i, True)
        o_ref[...] = acc_ref[...] / l_ref[...]
        lse_ref[...] = m_ref[...] + jnp.log(l_ref[...])

    return pl.pallas_call(
        body, name=f"{tag}_attn_fwd", grid=(HEADS, nb),
        in_specs=[pl.BlockSpec((blk, HEAD_PAD), lambda h, i: (i, h)),
                  pl.BlockSpec((T, HEAD_PAD), lambda h, i: (0, h)), pl.BlockSpec((T, V_DIM), lambda h, i: (0, h))],
        out_specs=[pl.BlockSpec((blk, V_DIM), lambda h, i: (i, h))] * 2,
        out_shape=[jax.ShapeDtypeStruct((T, ATTN_W), F32)] * 2,
        scratch_shapes=[pltpu.VMEM((blk, V_DIM), F32)] * 3,
        compiler_params=_params(("parallel", "parallel")),
    )(q_full, k_full, vv)


def attn_bwd(tag, q_full, k_full, vv, do, lse, delta, blk=512):
    T = q_full.shape[0]
    nb = T // blk
    neg = float(jnp.finfo(jnp.float32).min)

    def body(q_ref, k_ref, v_ref, do_ref, lse_ref, dl_ref, dq_ref, dk_ref, dv_ref, dk_acc, dv_acc):
        j = pl.program_id(1)

        @pl.when(j == 0)
        def _():
            dq_ref[...] = jnp.zeros_like(dq_ref)

        dk_acc[...] = jnp.zeros_like(dk_acc)
        dv_acc[...] = jnp.zeros_like(dv_acc)
        k, v = k_ref[...], v_ref[...]

        def step(i, masked):
            rows = pl.ds(pl.multiple_of(i * blk, blk), blk)
            q = q_ref[rows, :]
            s = lax.dot_general(q, k, (((1,), (1,)), ((), ())), preferred_element_type=F32)
            if masked:
                row = lax.broadcasted_iota(jnp.int32, (blk, blk), 0)
                col = lax.broadcasted_iota(jnp.int32, (blk, blk), 1)
                s = jnp.where(col <= row, s, neg)
            p = jnp.exp(s - lse_ref[rows, :1])
            dob = _bf(do_ref[rows, :])
            dv_acc[...] += lax.dot_general(p.astype(BF16), dob, (((0,), (0,)), ((), ())), preferred_element_type=F32)
            dp = lax.dot_general(dob, v, (((1,), (1,)), ((), ())), preferred_element_type=F32)
            ds = (p * (dp - dl_ref[rows, :1])).astype(BF16)
            dk_acc[...] += lax.dot_general(ds, q, (((0,), (0,)), ((), ())), preferred_element_type=F32)
            dq_ref[rows, :] += jnp.dot(ds, k, preferred_element_type=F32)

        def off_diagonal(i, carry):
            step(i, False)
            return carry

        step(j, True)
        lax.fori_loop(j + 1, nb, off_diagonal, 0)
        dk_ref[...] = dk_acc[...]
        dv_ref[...] = dv_acc[...]

    head = lambda h, j: (0, h)
    kv_ix = lambda h, j: (j, h)
    return pl.pallas_call(
        body, name=f"{tag}_attn_bwd", grid=(HEADS, nb),
        in_specs=[pl.BlockSpec((T, HEAD_PAD), head), pl.BlockSpec((blk, HEAD_PAD), kv_ix),
                  pl.BlockSpec((blk, V_DIM), kv_ix), pl.BlockSpec((T, V_DIM), head),
                  pl.BlockSpec((T, V_DIM), head), pl.BlockSpec((T, V_DIM), head)],
        out_specs=[pl.BlockSpec((T, HEAD_PAD), head),
                   pl.BlockSpec((blk, HEAD_PAD), kv_ix), pl.BlockSpec((blk, V_DIM), kv_ix)],
        out_shape=[jax.ShapeDtypeStruct((T, HEADS * HEAD_PAD), F32)] * 2 + [jax.ShapeDtypeStruct((T, ATTN_W), F32)],
        scratch_shapes=[pltpu.VMEM((blk, HEAD_PAD), F32), pltpu.VMEM((blk, V_DIM), F32)],
        compiler_params=_params(("parallel", "arbitrary")),
    )(q_full, k_full, vv, do, lse, delta)


def _gm_forward(u, v, gv, wc_ref, bb_ref, nchunk):
    ug = _gelu(u)
    vg = _gelu(v)
    rv = lax.rsqrt(jnp.mean(vg * vg, axis=-1, keepdims=True) + EPS)
    vhat = vg * rv
    vn = (vhat * gv).astype(BF16)
    gates = []
    for cidx in range(nchunk):
        rows = slice(cidx * CHUNK, (cidx + 1) * CHUNK)
        gates.append(jnp.concatenate(
            [jnp.dot(wc_ref[gidx], vn[rows, gidx * 128:(gidx + 1) * 128], preferred_element_type=F32) + bb_ref[gidx]
             for gidx in range(GROUPS)], axis=1))
    gate = jnp.concatenate(gates, axis=0)
    return ug, vhat, rv, vn, gate


def gmlp_fwd(tag, z_p, gv, gout, wc, bb, tm=256):
    T = z_p.shape[0]
    nchunk = tm // CHUNK

    def body(u_ref, v_ref, gv_ref, go_ref, wc_ref, bb_ref, o_ref):
        ug, _, _, _, gate = _gm_forward(u_ref[...], v_ref[...], gv_ref[...], wc_ref, bb_ref, nchunk)
        go = ug * gate
        ro = lax.rsqrt(jnp.mean(go * go, axis=-1, keepdims=True) + EPS)
        o_ref[...] = (go * ro * go_ref[...]).astype(BF16)

    vec = pl.BlockSpec((1, GM_W), lambda i: (0, 0))
    w3 = pl.BlockSpec((GROUPS, CHUNK, CHUNK), lambda i: (0, 0, 0))
    return pl.pallas_call(
        body, name=f"{tag}_gmlp_fwd", grid=(T // tm,),
        in_specs=[pl.BlockSpec((tm, GM_W), lambda i: (i, 1)), pl.BlockSpec((tm, GM_W), lambda i: (i, 2)), vec, vec, w3, w3],
        out_specs=pl.BlockSpec((tm, GM_W), lambda i: (i, 0)),
        out_shape=jax.ShapeDtypeStruct((T, GM_W), BF16),
        compiler_params=_params(("parallel",)),
    )(z_p, z_p, gv.reshape(1, GM_W), gout.reshape(1, GM_W), wc, bb)


def gmlp_bwd(tag, z_p, dmixed, gv, gout, wc, bb, tm=256):
    T = z_p.shape[0]
    nchunk = tm // CHUNK

    def body(u_ref, v_ref, dm_ref, gv_ref, go_ref, wc_ref, bb_ref, du_ref, dv_ref, dwc_ref, dbb_ref, dgv_ref, dgo_ref):
        i = pl.program_id(0)
        u, v = u_ref[...], v_ref[...]
        ug, vhat, rv, vn, gate = _gm_forward(u, v, gv_ref[...], wc_ref, bb_ref, nchunk)
        go = ug * gate
        ro = lax.rsqrt(jnp.mean(go * go, axis=-1, keepdims=True) + EPS)
        ohat = go * ro
        dm = dm_ref[...].astype(F32)
        dgo_part = jnp.sum(dm * ohat, axis=0, keepdims=True)
        doh = dm * go_ref[...]
        dgo = ro * (doh - ohat * jnp.mean(doh * ohat, axis=-1, keepdims=True))
        du_ref[...] = dgo * gate * _gelu_grad(u)
        dgate = dgo * ug
        dgb = dgate.astype(BF16)
        dvn_rows = []
        dwc_parts = []
        dbb_parts = []
        for gidx in range(GROUPS):
            cols = slice(gidx * 128, (gidx + 1) * 128)
            dw = jnp.zeros((CHUNK, CHUNK), F32)
            db = jnp.zeros((CHUNK, 128), F32)
            for cidx in range(nchunk):
                rows = slice(cidx * CHUNK, (cidx + 1) * CHUNK)
                dw = dw + lax.dot_general(dgb[rows, cols], vn[rows, cols], (((1,), (1,)), ((), ())),
                                          preferred_element_type=F32)
                db = db + dgate[rows, cols]
            dwc_parts.append(dw)
            dbb_parts.append(db)
        for cidx in range(nchunk):
            rows = slice(cidx * CHUNK, (cidx + 1) * CHUNK)
            dvn_rows.append(jnp.concatenate(
                [lax.dot_general(wc_ref[gidx], dgb[rows, gidx * 128:(gidx + 1) * 128], (((0,), (0,)), ((), ())),
                                 preferred_element_type=F32) for gidx in range(GROUPS)], axis=1))
        dvn = jnp.concatenate(dvn_rows, axis=0)
        dgv_part = jnp.sum(dvn * vhat, axis=0, keepdims=True)
        dvh = dvn * gv_ref[...]
        dvg = rv * (dvh - vhat * jnp.mean(dvh * vhat, axis=-1, keepdims=True))
        dv_ref[...] = dvg * _gelu_grad(v)

        @pl.when(i == 0)
        def _():
            for gidx in range(GROUPS):
                dwc_ref[gidx] = dwc_parts[gidx]
                dbb_ref[gidx] = dbb_parts[gidx]
            dgv_ref[...] = dgv_part
            dgo_ref[...] = dgo_part

        @pl.when(i > 0)
        def _():
            for gidx in range(GROUPS):
                dwc_ref[gidx] += dwc_parts[gidx]
                dbb_ref[gidx] += dbb_parts[gidx]
            dgv_ref[...] += dgv_part
            dgo_ref[...] += dgo_part

    vec = pl.BlockSpec((1, GM_W), lambda i: (0, 0))
    w3 = pl.BlockSpec((GROUPS, CHUNK, CHUNK), lambda i: (0, 0, 0))
    blk = pl.BlockSpec((tm, GM_W), lambda i: (i, 0))
    return pl.pallas_call(
        body, name=f"{tag}_gmlp_bwd", grid=(T // tm,),
        in_specs=[pl.BlockSpec((tm, GM_W), lambda i: (i, 1)), pl.BlockSpec((tm, GM_W), lambda i: (i, 2)),
                  pl.BlockSpec((tm, GM_W), lambda i: (i, 1)), vec, vec, w3, w3],
        out_specs=[blk, blk, w3, w3, vec, vec],
        out_shape=[jax.ShapeDtypeStruct((T, GM_W), F32)] * 2 + [jax.ShapeDtypeStruct((GROUPS, CHUNK, CHUNK), F32)] * 2
        + [jax.ShapeDtypeStruct((1, GM_W), F32)] * 2,
        compiler_params=_params(("arbitrary",)),
    )(z_p, z_p, dmixed, gv.reshape(1, GM_W), gout.reshape(1, GM_W), wc, bb)


def mixer_fwd(tag, h, w, tabs, wout_g, pre):
    T = h.shape[0]
    n2 = rms_fwd(f"{tag}_mix_rms", h, w["mix_norm"], D_MODEL)
    (z_p,) = mm_simple(f"{tag}_win", n2, lambda tk, tn: op_bt(w["w_in_pt"], tk, tn), T, IN_P, D_MODEL, 512, 1024, D_MODEL)
    cqn = rms_fwd(f"{tag}_cq_rms", z_p, w["q_a_norm"], Q_RANK, col_blk=0)
    ckvn = rms_fwd(f"{tag}_ckv_rms", z_p, w["kv_a_norm"], KV_RANK, col_blk=2)
    (q_raw,) = mm_simple(f"{tag}_wq", cqn, lambda tk, tn: op_b(w["wq_p"], tk, tn), T, 2048, Q_RANK, 512, 1024, Q_RANK)
    (kk_raw,) = mm_simple(f"{tag}_wk", ckvn, lambda tk, tn: op_b(w["wk_p"], tk, tn), T, 2048, KV_RANK, 512, 1024, KV_RANK)
    (vv,) = mm_simple(f"{tag}_wv", ckvn, lambda tk, tn: op_b(w["wv"], tk, tn), T, ATTN_W, KV_RANK, 512, 1024, KV_RANK,
                      out_dtype=BF16)
    q_full, k_full = qk_prep_fwd(tag, q_raw, kk_raw, z_p, w["gq_p"], w["gk_p"], tabs)
    a_out, lse = attn_fwd(tag, q_full, k_full, vv)
    mixed_a = rms_fwd(f"{tag}_ao_rms", a_out, w["attn_out_norm"], ATTN_W)
    mixed_g = gmlp_fwd(tag, z_p, w["gm_v_norm"], w["gm_out_norm"], w["wc"], w["bb"])
    tm, tn, tk = 512, 1024, 512
    (h2,) = matmul(
        f"{tag}_wout", (T // tm, D_MODEL // tn, ATTN_W // tk),
        [op_a(mixed_a, tm, tk), op_a(mixed_g, tm, tk)],
        [op_b_rows(wout_g, pre, tk, tn), op_b_rows(wout_g, pre, tk, tn, koff=ATTN_W // tk)],
        [(0, 0, 0), (1, 1, 0)], 1, [tile_mn(h, tm, tn)], [out_mn(T, D_MODEL, tm, tn, F32)],
        lambda accs, xs: (xs[0] + accs[0],), (tm, tn))
    res = dict(n2=n2, z_p=z_p, cqn=cqn, ckvn=ckvn, q_raw=q_raw, kk_raw=kk_raw, vv=vv, q_full=q_full, k_full=k_full,
               a_out=a_out, lse=lse, mixed_a=mixed_a, mixed_g=mixed_g)
    return h2, res


def mixer_bwd(tag, dh2, dh2_bf, h, w, tabs, wout_g, pre, r):
    T = h.shape[0]
    g = {}
    (dmixed,) = mm_simple(f"{tag}_dmixed", dh2_bf, lambda tk, tn: op_b_rows_t(wout_g, pre, tk, tn), T, D_MODEL, D_MODEL,
                          512, 512, D_MODEL)
    (dwo_a,) = mm_simple(f"{tag}_dwout_a", r["mixed_a"], lambda tk, tn: op_b(dh2_bf, tk, tn), ATTN_W, D_MODEL, T,
                         1024, 1024, 512, a_t=True, out_dtype=BF16)
    (dwo_g,) = mm_simple(f"{tag}_dwout_g", r["mixed_g"], lambda tk, tn: op_b(dh2_bf, tk, tn), GM_W, D_MODEL, T,
                         1024, 1024, 512, a_t=True, out_dtype=BF16)
    g["w_out"] = jnp.concatenate([dwo_a, dwo_g], axis=0)
    da_out, g["attn_out_norm"], delta = rms_bwd(f"{tag}_ao_rms_bwd", r["a_out"], w["attn_out_norm"], dmixed, ATTN_W,
                                                with_delta=True)
    dq_full, dk_full, dvv = attn_bwd(tag, r["q_full"], r["k_full"], r["vv"], da_out, r["lse"], delta)
    dq_raw, dkk_raw, dzkr, g["gq_p"], g["gk_p"] = qk_prep_bwd(tag, dq_full, dk_full, r["q_raw"], r["kk_raw"], r["z_p"],
                                                            w["gq_p"], w["gk_p"], tabs)
    (g["wq_p"],) = mm_simple(f"{tag}_dwq", r["cqn"], lambda tk, tn: op_b(dq_raw, tk, tn), Q_RANK, 2048, T, Q_RANK, 1024, 512,
                             a_t=True, out_dtype=BF16)
    (g["wk_p"],) = mm_simple(f"{tag}_dwk", r["ckvn"], lambda tk, tn: op_b(dkk_raw, tk, tn), KV_RANK, 2048, T, KV_RANK, 1024,
                             512, a_t=True, out_dtype=BF16)
    (g["wv"],) = mm_simple(f"{tag}_dwv", r["ckvn"], lambda tk, tn: op_b(dvv, tk, tn), KV_RANK, ATTN_W, T, KV_RANK, 1024, 512,
                           a_t=True, out_dtype=BF16)
    (dcqn,) = mm_simple(f"{tag}_dcqn", dq_raw, lambda tk, tn: op_bt(w["wq_p"], tk, tn), T, Q_RANK, 2048, 512, Q_RANK, 2048)
    (dck1,) = mm_simple(f"{tag}_dckvn_k", dkk_raw, lambda tk, tn: op_bt(w["wk_p"], tk, tn), T, KV_RANK, 2048, 512, KV_RANK,
                        2048)
    (dckvn,) = mm_simple(f"{tag}_dckvn_v", dvv, lambda tk, tn: op_bt(w["wv"], tk, tn), T, KV_RANK, ATTN_W, 512, KV_RANK,
                         ATTN_W, extras=[tile_mn(dck1, 512, KV_RANK)], epilogue=lambda accs, xs: (accs[0] + xs[0],))
    dc_q, g["q_a_norm"] = rms_bwd(f"{tag}_cq_rms_bwd", r["z_p"], w["q_a_norm"], dcqn, Q_RANK, col_blk=0)
    dc_kv, g["kv_a_norm"] = rms_bwd(f"{tag}_ckv_rms_bwd", r["z_p"], w["kv_a_norm"], dckvn, KV_RANK, col_blk=2)
    du, dv, g["wc"], g["bb"], g["gm_v_norm"], g["gm_out_norm"] = gmlp_bwd(
        tag, r["z_p"], dmixed, w["gm_v_norm"], w["gm_out_norm"], w["wc"], w["bb"])
    dz_p = jnp.concatenate([dc_q, dc_kv, dzkr, du, dv], axis=1).astype(BF16)
    (g["w_in_pt"],) = mm_simple(f"{tag}_dwin", dz_p, lambda tk, tn: op_b(r["n2"], tk, tn), IN_P, D_MODEL, T, 1024, 1024, 512,
                                a_t=True, out_dtype=BF16)
    (dn2,) = mm_simple(f"{tag}_dn2", dz_p, lambda tk, tn: op_b(w["w_in_pt"], tk, tn), T, D_MODEL, IN_P, 512, 1024, IN_P)
    dh1, g["mix_norm"], dh1_bf = rms_bwd(f"{tag}_mix_rms_bwd", h, w["mix_norm"], dn2, D_MODEL, dres=dh2, bf16_copy=True)
    return dh1, dh1_bf, g


def ple_fwd(tag, h3, p_l, w, wpg_g, wple_g, pre):
    T = h3.shape[0]
    (pw,) = mm_simple(f"{tag}_wple", p_l, lambda tk, tn: op_b_cols(wple_g, pre, tk, tn), T, D_MODEL, PLE_DIM, 512, 512,
                      PLE_DIM)
    e = rms_fwd(f"{tag}_ple_rms", pw, w["ple_norm"], D_MODEL, out_dtype=F32)
    n4 = rms_fwd(f"{tag}_pg_rms", h3, w["ple_gate_norm"], D_MODEL)

    def epi(accs, xs):
        gt = _sigmoid(accs[0])
        return xs[0] + gt * xs[1], gt

    tm, tn, tk = 512, 1024, 512
    h4, gate = matmul(
        f"{tag}_wpg", (T // tm, D_MODEL // tn, D_MODEL // tk),
        [op_a(n4, tm, tk)], [op_b_rows(wpg_g, pre, tk, tn)], [(0, 0, 0)], 1,
        [tile_mn(h3, tm, tn), tile_mn(e, tm, tn)],
        [out_mn(T, D_MODEL, tm, tn, F32), out_mn(T, D_MODEL, tm, tn, BF16)], epi, (tm, tn))
    return h4, dict(pw=pw, e=e, n4=n4, gate=gate)


def ple_bwd(tag, dh4, h3, p_l, w, wpg_g, wple_g, pre, r, tm=256):
    T = h3.shape[0]

    def act_body(d_ref, g_ref, e_ref, dpre_ref, de_ref):
        d, gt = d_ref[...], g_ref[...].astype(F32)
        dpre_ref[...] = (d * e_ref[...] * gt * (1.0 - gt)).astype(BF16)
        de_ref[...] = d * gt

    blk = pl.BlockSpec((tm, D_MODEL), lambda i: (i, 0))
    dpre, de = pl.pallas_call(
        act_body, name=f"{tag}_ple_act_bwd", grid=(T // tm,), in_specs=[blk, blk, blk], out_specs=[blk, blk],
        out_shape=[jax.ShapeDtypeStruct((T, D_MODEL), BF16), jax.ShapeDtypeStruct((T, D_MODEL), F32)],
        compiler_params=_params(("parallel",)),
    )(dh4, r["gate"], r["e"])
    g = {}
    (g["w_ple_gate"],) = mm_simple(f"{tag}_dwpg", r["n4"], lambda tk, tn: op_b(dpre, tk, tn), D_MODEL, D_MODEL, T,
                                   1024, 1024, 512, a_t=True, out_dtype=BF16)
    (dn4,) = mm_simple(f"{tag}_dn4", dpre, lambda tk, tn: op_b_rows_t(wpg_g, pre, tk, tn), T, D_MODEL, D_MODEL, 512, 512,
                       D_MODEL)
    dh3, g["ple_gate_norm"], dh3_bf = rms_bwd(f"{tag}_pg_rms_bwd", h3, w["ple_gate_norm"], dn4, D_MODEL, dres=dh4,
                                              bf16_copy=True)
    dpw, g["ple_norm"] = rms_bwd(f"{tag}_ple_rms_bwd", r["pw"], w["ple_norm"], de, D_MODEL)
    (g["w_ple"],) = mm_simple(f"{tag}_dwple", p_l, lambda tk, tn: op_b(dpw, tk, tn), PLE_DIM, D_MODEL, T, PLE_DIM, 512, 512,
                              a_t=True, outs=[out_cols(PLE_DIM, 512, PLE_DIM, 512, BF16)])
    return dh3, dh3_bf, g


def loss_grad(y, target, tm=256):
    T = y.shape[0]

    def body(y_ref, t_ref, dy_ref, l_ref):
        i = pl.program_id(0)
        d = y_ref[...] - t_ref[...]
        dy_ref[...] = d * (1.0 / D_MODEL)
        part = jnp.sum((d * d).reshape(tm // 8, 8, D_MODEL), axis=0)

        @pl.when(i == 0)
        def _():
            l_ref[...] = part

        @pl.when(i > 0)
        def _():
            l_ref[...] += part

    blk = pl.BlockSpec((tm, D_MODEL), lambda i: (i, 0))
    dy, part = pl.pallas_call(
        body, name="loss_grad", grid=(T // tm,), in_specs=[blk, blk],
        out_specs=[blk, pl.BlockSpec((8, D_MODEL), lambda i: (0, 0))],
        out_shape=[jax.ShapeDtypeStruct((T, D_MODEL), F32), jax.ShapeDtypeStruct((8, D_MODEL), F32)],
        compiler_params=_params(("arbitrary",)),
    )(y, target)
    return dy, 0.5 * jnp.sum(part) / D_MODEL


def _unshard_cols(g_l):
    return g_l.transpose(1, 0, 2).reshape(g_l.shape[1], -1)


def _shard_cols(w):
    return w.reshape(w.shape[0], N_CHIPS, -1).transpose(1, 0, 2)


def layer_weights(l, Gl, small):
    w = {k: small[k][l] for k in ("mix_norm", "q_a_norm", "kv_a_norm", "gm_v_norm", "attn_out_norm", "gm_out_norm",
                                  "ple_gate_norm", "ple_norm")}
    wint = Gl["w_in"][:, :IN_SHARD].reshape(-1, D_MODEL)
    z = lambda n: jnp.zeros((n, D_MODEL), BF16)
    w["w_in_pt"] = jnp.concatenate([wint[:768], z(128), wint[768:832], z(64), wint[832:]], axis=0)
    wuq = _unshard_cols(Gl["w_uq"]).reshape(Q_RANK, HEADS, QK_DIM)
    w["wq_p"] = jnp.pad(wuq, ((0, 0), (0, 0), (0, HEAD_PAD - QK_DIM))).reshape(Q_RANK, HEADS * HEAD_PAD)
    wukv = _unshard_cols(Gl["w_ukv"]).reshape(KV_RANK, HEADS, QK_NOPE + V_DIM)
    w["wk_p"] = jnp.pad(wukv[:, :, :QK_NOPE], ((0, 0), (0, 0), (0, HEAD_PAD - QK_NOPE))).reshape(KV_RANK, HEADS * HEAD_PAD)
    w["wv"] = wukv[:, :, QK_NOPE:].reshape(KV_RANK, ATTN_W)
    w["gq_p"] = jnp.pad(small["q_norm"][l], (0, HEAD_PAD - QK_DIM)).reshape(1, HEAD_PAD)
    w["gk_p"] = jnp.pad(small["k_norm"][l], (0, HEAD_PAD - QK_DIM)).reshape(1, HEAD_PAD)
    tril = jnp.tril(jnp.ones((CHUNK, CHUNK), dtype=bool))
    w["wc"] = jnp.where(tril[None], small["gm_ws"][l], 0.0).astype(BF16)
    w["bb"] = jnp.broadcast_to(small["gm_bs"][l][:, :, None], (GROUPS, CHUNK, 128)).astype(F32)
    return w


def mixer_grads_to_shards(g):
    out = {}
    dwint = g["w_in_pt"]
    dwint = jnp.concatenate([dwint[:768], dwint[896:960], dwint[1024:]], axis=0).reshape(N_CHIPS, IN_SHARD, D_MODEL)
    out["w_in"] = jnp.pad(dwint, ((0, 0), (0, IN_SHARD_PAD - IN_SHARD), (0, 0)))
    dwuq = g["wq_p"].reshape(Q_RANK, HEADS, HEAD_PAD)[:, :, :QK_DIM].reshape(Q_RANK, HEADS * QK_DIM)
    out["w_uq"] = _shard_cols(dwuq)
    dwukv = jnp.concatenate([g["wk_p"].reshape(KV_RANK, HEADS, HEAD_PAD)[:, :, :QK_NOPE],
                             g["wv"].reshape(KV_RANK, HEADS, V_DIM)], axis=-1).reshape(KV_RANK, HEADS * (QK_NOPE + V_DIM))
    out["w_ukv"] = _shard_cols(dwukv)
    out["w_out"] = g["w_out"].reshape(N_CHIPS, D_MODEL // N_CHIPS, D_MODEL)
    out["q_norm"] = g["gq_p"][0, :QK_DIM]
    out["k_norm"] = g["gk_p"][0, :QK_DIM]
    tril = jnp.tril(jnp.ones((CHUNK, CHUNK), dtype=bool))
    out["gm_ws"] = jnp.where(tril[None], g["wc"], 0.0)
    out["gm_bs"] = jnp.sum(g["bb"], axis=-1)
    for k in ("mix_norm", "q_a_norm", "kv_a_norm", "gm_v_norm", "attn_out_norm", "gm_out_norm"):
        out[k] = g[k][0]
    return out


def layer_fwd(l, h, p_l, Gl, small, tabs, after_first_ffn=None, before_ple=None):
    h1, r_a = ffn_fwd(f"l{l}a", h, small["ffn_a_norm"][l], Gl["ffn_a_w1"], Gl["ffn_a_w3"], Gl["ffn_a_w2"], ())
    if after_first_ffn is not None:
        Gl, small = after_first_ffn(h1, Gl, small)
    w = layer_weights(l, Gl, small)
    h2, r_m = mixer_fwd(f"l{l}", h1, w, tabs, Gl["w_out"], ())
    h3, r_b = ffn_fwd(f"l{l}b", h2, small["ffn_b_norm"][l], Gl["ffn_b_w1"], Gl["ffn_b_w3"], Gl["ffn_b_w2"], ())
    if before_ple is not None:
        w = {**w, "ple_norm": w["ple_norm"] + before_ple(h3)[0, 0]}
    h4, r_p = ple_fwd(f"l{l}", h3, p_l, w, Gl["w_ple_gate"], Gl["w_ple"], ())
    return h4, (w, h, h1, h2, h3, r_a, r_m, r_b, r_p)


def layer_bwd(l, dh, p_l, Gl, small, tabs, saved, before_ffn_a=None):
    w, h0, h1, h2, h3, r_a, r_m, r_b, r_p = saved
    slabs = lambda d: d.reshape(N_CHIPS, FF_PAD, D_MODEL)
    gl = {}
    dh, dh_bf, g_p = ple_bwd(f"l{l}", dh, h3, p_l, w, Gl["w_ple_gate"], Gl["w_ple"], (), r_p)
    gl["w_ple_gate"] = g_p["w_ple_gate"].reshape(N_CHIPS, D_MODEL // N_CHIPS, D_MODEL)
    gl["w_ple"] = g_p["w_ple"]
    gl["ple_gate_norm"], gl["ple_norm"] = g_p["ple_gate_norm"][0], g_p["ple_norm"][0]
    dh, dh_bf, dg, dw1, dw3, dw2 = ffn_bwd(f"l{l}b", dh, dh_bf, h2, small["ffn_b_norm"][l], r_b,
                                           Gl["ffn_b_w1"], Gl["ffn_b_w3"], Gl["ffn_b_w2"], ())
    gl["ffn_b_norm"] = dg[0]
    gl["ffn_b_w1"], gl["ffn_b_w3"], gl["ffn_b_w2"] = slabs(dw1), slabs(dw3), slabs(dw2)
    dh, dh_bf, g_m = mixer_bwd(f"l{l}", dh, dh_bf, h1, w, tabs, Gl["w_out"], (), r_m)
    gl.update(mixer_grads_to_shards(g_m))
    dw_after = None if before_ffn_a is None else before_ffn_a(gl)
    dh, _, dg, dw1, dw3, dw2 = ffn_bwd(f"l{l}a", dh, dh_bf, h0, small["ffn_a_norm"][l], r_a,
                                       Gl["ffn_a_w1"], Gl["ffn_a_w3"], Gl["ffn_a_w2"], (), dw_after=dw_after)
    gl["ffn_a_norm"] = dg[0]
    gl["ffn_a_w1"], gl["ffn_a_w3"], gl["ffn_a_w2"] = slabs(dw1), slabs(dw3), slabs(dw2)
    return dh, gl


MESH = pl.DeviceIdType.MESH
HBM_SPEC = pl.BlockSpec(memory_space=pltpu.HBM)


def _place():
    x, y, c = lax.axis_index("x"), lax.axis_index("y"), lax.axis_index("c")
    others = [(1 - x, y), (x, 1 - y), (1 - x, 1 - y)]
    return x, y, c, 2 * x + y, others


def prep_shard(name, w, layer, rows_pad, place, after=None):
    _, ks, n = w.shape
    ksp = ks + rows_pad
    tc = 512 if n % 512 == 0 else n
    deps = [] if after is None else [after]

    def body(place_ref, x_ref, *rest):
        o_ref = rest[-1]
        o_ref[:ks] = x_ref[...].astype(BF16)
        if rows_pad:
            o_ref[ks:] = jnp.zeros((rows_pad, tc), BF16)

    return pl.pallas_call(
        body, name=name,
        grid_spec=pltpu.PrefetchScalarGridSpec(
            num_scalar_prefetch=1, grid=(n // tc,),
            in_specs=[pl.BlockSpec((None, ks, tc), lambda i, s: (layer, 0, i))] + [ANY_SPEC] * len(deps),
            out_specs=pl.BlockSpec((None, ksp, tc), lambda i, s: (s[0], 0, i))),
        out_shape=jax.ShapeDtypeStruct((N_CHIPS, ksp, n), BF16),
        compiler_params=_params(("parallel",)),
    )(place, w, *deps)


def exchange_halves(name, grads):
    n = len(grads)

    def body(*refs):
        d_refs, r_refs = refs[:n], refs[n:2 * n]
        send, recv = refs[2 * n:]
        x, y, c, _, _ = _place()
        cps = []
        for w in range(n):
            half = grads[w].shape[1] // 2
            cps.append(pltpu.make_async_remote_copy(
                src_ref=d_refs[w].at[pl.ds(0, N_CHIPS), pl.ds((1 - c) * half, half)], dst_ref=r_refs[w],
                send_sem=send.at[w], recv_sem=recv.at[w], device_id=(x, y, 1 - c), device_id_type=MESH))
        for cp in cps:
            cp.start()
        for cp in cps:
            cp.wait()

    return pl.pallas_call(
        body, name=name, in_specs=[HBM_SPEC] * n, out_specs=[HBM_SPEC] * n,
        out_shape=[jax.ShapeDtypeStruct((N_CHIPS, g.shape[1] // 2, g.shape[2]), g.dtype) for g in grads],
        scratch_shapes=[pltpu.SemaphoreType.DMA((n,))] * 2,
    )(*grads)


def share_halves(name, fulls):
    n = len(fulls)

    def body(*refs):
        o_refs = refs[n:2 * n]
        send, recv = refs[2 * n:]
        x, y, c, _, _ = _place()
        cps = []
        for w in range(n):
            kh = fulls[w].shape[0] // 2
            half = o_refs[w].at[pl.ds(c * kh, kh)]
            cps.append(pltpu.make_async_remote_copy(src_ref=half, dst_ref=half, send_sem=send.at[w], recv_sem=recv.at[w],
                                                    device_id=(x, y, 1 - c), device_id_type=MESH))
        for cp in cps:
            cp.start()
        for cp in cps:
            cp.wait()

    return pl.pallas_call(
        body, name=name, in_specs=[HBM_SPEC] * n, out_specs=[HBM_SPEC] * n,
        out_shape=[jax.ShapeDtypeStruct(f.shape, f.dtype) for f in fulls],
        input_output_aliases={w: w for w in range(n)},
        scratch_shapes=[pltpu.SemaphoreType.DMA((n,))] * 2,
    )(*fulls)


SEM_SPEC = pl.BlockSpec(memory_space=pltpu.SEMAPHORE)
ANY_SPEC = pl.BlockSpec(memory_space=pl.ANY)
DATAFLOW = pltpu.SideEffectType.DATAFLOW_SIDE_EFFECTING


def _hbm(x):
    return pltpu.with_memory_space_constraint(x, pltpu.HBM)


def _start_call(name, slots, after, issue):
    n = len(slots)
    deps = [] if after is None else [after]
    nd = len(deps)

    def body(*refs):
        issue(refs[n + nd + 2:2 * n + nd + 2], refs[n + nd], refs[n + nd + 1])
        token = refs[2 * n + nd + 2]
        token[...] = jnp.zeros_like(token)

    outs = pl.pallas_call(
        body, name=name,
        in_specs=[HBM_SPEC] * n + [ANY_SPEC] * nd,
        out_specs=(SEM_SPEC, SEM_SPEC, *([HBM_SPEC] * n), pl.BlockSpec(memory_space=pltpu.VMEM)),
        out_shape=(pltpu.SemaphoreType.DMA((n,)), pltpu.SemaphoreType.DMA((n,)),
                   *[pltpu.HBM(s.shape, s.dtype) for s in slots], jax.ShapeDtypeStruct((8, 128), F32)),
        input_output_aliases={w: w + 2 for w in range(n)},
        compiler_params=pltpu.CompilerParams(has_side_effects=DATAFLOW),
    )(*[_hbm(s) for s in slots], *deps)
    return outs[0], outs[1], list(outs[2:2 + n]), outs[2 + n]


def gather_start(name, slots, after):
    def issue(g_refs, send, recv):
        x, y, c, jme, others = _place()
        for w in range(len(slots)):
            kh = slots[w].shape[1] // 2
            mine = g_refs[w].at[jme, pl.ds(c * kh, kh)]
            for (px, py) in others:
                pltpu.make_async_remote_copy(src_ref=mine, dst_ref=mine, send_sem=send.at[w], recv_sem=recv.at[w],
                                             device_id=(px, py, c), device_id_type=MESH).start()

    return _start_call(name, slots, after, issue)


def forward_start(name, slots):
    def issue(g_refs, send, recv):
        x, y, c, _, others = _place()
        for w in range(len(slots)):
            kh = slots[w].shape[1] // 2
            for (px, py) in others:
                blk = g_refs[w].at[2 * px + py, pl.ds(c * kh, kh)]
                pltpu.make_async_remote_copy(src_ref=blk, dst_ref=blk, send_sem=send.at[w], recv_sem=recv.at[w],
                                             device_id=(x, y, 1 - c), device_id_type=MESH).start()

    return _start_call(name, slots, None, issue)


def gather_wait(name, send, recv, flying, after):
    n = len(flying)

    def body(*refs):
        send_ref, recv_ref = refs[n], refs[n + 1]
        g_refs = refs[n + 3:]
        x, y, c, _, _ = _place()
        for w in range(n):
            three = g_refs[w].at[pl.ds(0, 3), pl.ds(0, flying[w].shape[1] // 2)]
            cp = pltpu.make_async_remote_copy(src_ref=three, dst_ref=three, send_sem=send_ref.at[w], recv_sem=recv_ref.at[w],
                                              device_id=(x, y, 1 - c), device_id_type=MESH)
            cp.wait_send()
            cp.wait_recv()

    return pl.pallas_call(
        body, name=name,
        in_specs=[HBM_SPEC] * n + [SEM_SPEC, SEM_SPEC, ANY_SPEC],
        out_specs=[HBM_SPEC] * n,
        out_shape=[pltpu.HBM(s.shape, s.dtype) for s in flying],
        input_output_aliases={w: w for w in range(n)},
        compiler_params=pltpu.CompilerParams(has_side_effects=DATAFLOW),
    )(*flying, send, recv, after)


def scatter_start(name, parts):
    n = len(parts)

    def body(*refs):
        p_refs, q_refs = refs[2 * n + 2:3 * n + 2], refs[3 * n + 2:4 * n + 2]
        send, recv, token = refs[2 * n], refs[2 * n + 1], refs[4 * n + 2]
        x, y, c, jme, others = _place()
        for w in range(n):
            for (px, py) in others:
                pltpu.make_async_remote_copy(
                    src_ref=p_refs[w].at[2 * px + py], dst_ref=q_refs[w].at[jme], send_sem=send.at[w], recv_sem=recv.at[w],
                    device_id=(px, py, c), device_id_type=MESH).start()
        token[...] = jnp.zeros_like(token)

    lands = [_hbm(lax.empty(p.shape, p.dtype)) for p in parts]
    outs = pl.pallas_call(
        body, name=name,
        in_specs=[HBM_SPEC] * (2 * n),
        out_specs=(SEM_SPEC, SEM_SPEC, *([HBM_SPEC] * (2 * n)), pl.BlockSpec(memory_space=pltpu.VMEM)),
        out_shape=(pltpu.SemaphoreType.DMA((n,)), pltpu.SemaphoreType.DMA((n,)),
                   *[pltpu.HBM(p.shape, p.dtype) for p in parts], *[pltpu.HBM(p.shape, p.dtype) for p in parts],
                   jax.ShapeDtypeStruct((8, 128), F32)),
        input_output_aliases={w: w + 2 for w in range(2 * n)},
        compiler_params=pltpu.CompilerParams(has_side_effects=DATAFLOW),
    )(*[_hbm(p) for p in parts], *lands)
    return outs[0], outs[1], list(outs[2:2 + n]), list(outs[2 + n:2 + 2 * n]), outs[2 + 2 * n]


def scatter_wait(name, send, recv, parts, lands, after):
    n = len(parts)

    def body(*refs):
        send_ref, recv_ref = refs[2 * n], refs[2 * n + 1]
        q_refs = refs[3 * n + 3:]
        x, y, c, _, _ = _place()
        for w in range(n):
            three = q_refs[w].at[pl.ds(0, 3)]
            cp = pltpu.make_async_remote_copy(src_ref=three, dst_ref=three, send_sem=send_ref.at[w], recv_sem=recv_ref.at[w],
                                              device_id=(x, y, 1 - c), device_id_type=MESH)
            cp.wait_send()
            cp.wait_recv()

    outs = pl.pallas_call(
        body, name=name,
        in_specs=[HBM_SPEC] * (2 * n) + [SEM_SPEC, SEM_SPEC, ANY_SPEC],
        out_specs=[HBM_SPEC] * (2 * n),
        out_shape=[pltpu.HBM(p.shape, p.dtype) for p in parts] * 2,
        input_output_aliases={w: w for w in range(2 * n)},
        compiler_params=pltpu.CompilerParams(has_side_effects=DATAFLOW),
    )(*parts, *lands, send, recv, after)
    return list(outs[:n]), list(outs[n:])


def allreduce_small(v):
    R = v.shape[0]

    def body(v_ref, o_ref, sib_ref, mine_ref, all_ref, d_send, d_recv, i_send, i_recv):
        x, y, c, jme, others = _place()
        swap = pltpu.make_async_remote_copy(src_ref=v_ref, dst_ref=sib_ref, send_sem=d_send, recv_sem=d_recv,
                                            device_id=(x, y, 1 - c), device_id_type=MESH)
        swap.start()
        swap.wait()
        mine_ref[...] = v_ref[...] + sib_ref[...]
        for (px, py) in others:
            pltpu.make_async_remote_copy(src_ref=mine_ref, dst_ref=all_ref.at[jme], send_sem=i_send, recv_sem=i_recv,
                                         device_id=(px, py, c), device_id_type=MESH).start()
        three = all_ref.at[pl.ds(0, 3)]
        wait3 = pltpu.make_async_remote_copy(src_ref=three, dst_ref=three, send_sem=i_send, recv_sem=i_recv,
                                             device_id=(x, y, c), device_id_type=MESH)
        wait3.wait_recv()
        wait3.wait_send()
        all_ref[jme] = mine_ref[...]
        o_ref[...] = ((all_ref[0] + all_ref[1]) + all_ref[2]) + all_ref[3]

    vm = pl.BlockSpec(memory_space=pltpu.VMEM)
    return pl.pallas_call(
        body, name="allreduce_small", in_specs=[vm], out_specs=vm,
        out_shape=jax.ShapeDtypeStruct(v.shape, F32),
        scratch_shapes=[pltpu.VMEM((R, 128), F32), pltpu.VMEM((R, 128), F32), pltpu.VMEM((N_CHIPS, R, 128), F32),
                        pltpu.SemaphoreType.DMA, pltpu.SemaphoreType.DMA, pltpu.SemaphoreType.DMA, pltpu.SemaphoreType.DMA],
        compiler_params=pltpu.CompilerParams(vmem_limit_bytes=VMEM_LIMIT_BYTES),
    )(v)


def _row_tile(rows, width, mult=16, cap=3 << 20):
    best = rows
    for t in range(mult, rows + 1, mult):
        if rows % t == 0 and t * width * 4 <= cap:
            best = t
    return best


def add_sibling(name, mine, theirs, place):
    _, kh, ns = theirs.shape
    tr = _row_tile(kh, ns)
    nblk = kh // tr

    def body(place_ref, a_ref, b_ref, o_ref):
        o_ref[...] = (a_ref[...].astype(F32) + b_ref[...].astype(F32)).astype(BF16)

    return pl.pallas_call(
        body, name=name,
        grid_spec=pltpu.PrefetchScalarGridSpec(
            num_scalar_prefetch=1, grid=(N_CHIPS, nblk),
            in_specs=[pl.BlockSpec((None, tr, ns), lambda j, i, s: (j, s[1] * nblk + i, 0)),
                      pl.BlockSpec((None, tr, ns), lambda j, i, s: (j, i, 0))],
            out_specs=pl.BlockSpec((None, tr, ns), lambda j, i, s: (j, i, 0))),
        out_shape=jax.ShapeDtypeStruct(theirs.shape, BF16),
        compiler_params=_params(("parallel", "parallel")),
    )(place, mine, theirs)


def add_chips(name, q, p, place):
    _, kh, ns = q.shape
    tr = _row_tile(kh, ns)
    nblk = kh // tr

    def body(place_ref, *refs):
        q_refs, own_ref, o_ref = refs[:N_CHIPS], refs[N_CHIPS], refs[-1]
        jme = place_ref[0]
        tot = None
        for j in range(N_CHIPS):
            v = jnp.where(jme == j, own_ref[...], q_refs[j][...]).astype(F32)
            tot = v if tot is None else tot + v
        o_ref[...] = tot

    def q_ix(j):
        return lambda i, s: (jnp.where(s[0] == j, (j + 1) % N_CHIPS, j), i, 0)

    in_specs = [pl.BlockSpec((None, tr, ns), q_ix(j)) for j in range(N_CHIPS)]
    in_specs.append(pl.BlockSpec((None, tr, ns), lambda i, s: (s[0], i, 0)))
    return pl.pallas_call(
        body, name=name,
        grid_spec=pltpu.PrefetchScalarGridSpec(
            num_scalar_prefetch=1, grid=(nblk,), in_specs=in_specs,
            out_specs=pl.BlockSpec((tr, ns), lambda i, s: (s[1] * nblk + i, 0))),
        out_shape=jax.ShapeDtypeStruct((2 * kh, ns), F32),
        compiler_params=_params(("parallel",)),
    )(place, q, q, q, q, p)


ADAM_LR, ADAM_B1, ADAM_B2, ADAM_EPS, ADAM_WD, ADAM_STEP = 0.001, 0.9, 0.999, 1e-08, 0.01, 10


def adamw(name, w, g, m, v, layer, prev=None):
    _, k, ns = w.shape
    nsp = g.shape[1]
    tr = _row_tile(k, nsp, mult=8, cap=2 << 20)

    def body(w_ref, g_ref, m_ref, v_ref, *rest):
        go_ref, d_ref, mo_ref, vo_ref = rest[-4:]
        gv = g_ref[:, :ns] if nsp != ns else g_ref[...]
        mn = ADAM_B1 * m_ref[...] + (1.0 - ADAM_B1) * gv
        vn = ADAM_B2 * v_ref[...] + (1.0 - ADAM_B2) * (gv * gv)
        m_hat = mn / (1.0 - ADAM_B1 ** ADAM_STEP)
        v_hat = vn / (1.0 - ADAM_B2 ** ADAM_STEP)
        go_ref[...] = gv
        d_ref[...] = -ADAM_LR * (m_hat / (jnp.sqrt(v_hat) + ADAM_EPS) + ADAM_WD * w_ref[...])
        mo_ref[...] = mn
        vo_ref[...] = vn

    blk = pl.BlockSpec((None, tr, ns), lambda i: (layer, i, 0))
    gblk = pl.BlockSpec((tr, nsp), lambda i: (i, 0))
    args, in_specs, aliases = [w, g, m, v], [blk, gblk, blk, blk], {}
    if prev is not None:
        args += list(prev)
        in_specs += [pl.BlockSpec(memory_space=pl.ANY)] * 4
        aliases = {4 + i: i for i in range(4)}
    return pl.pallas_call(
        body, name=name, grid=(k // tr,), in_specs=in_specs, out_specs=[blk] * 4,
        out_shape=[jax.ShapeDtypeStruct(w.shape, F32)] * 4, input_output_aliases=aliases,
        compiler_params=_params(("parallel",)),
    )(*args)


WEIGHTS = ("ffn_a_norm", "ffn_a_w1", "ffn_a_w3", "ffn_a_w2", "mix_norm", "w_in", "q_a_norm", "w_uq", "kv_a_norm", "w_ukv",
           "q_norm", "k_norm", "gm_v_norm", "gm_ws", "gm_bs", "attn_out_norm", "gm_out_norm", "w_out", "ffn_b_norm",
           "ffn_b_w1", "ffn_b_w3", "ffn_b_w2", "ple_gate_norm", "w_ple_gate", "w_ple", "ple_norm")
_FF = FF_PAD - FF_SHARD
BIG = {"ffn_a_w1": _FF, "ffn_a_w3": _FF, "ffn_a_w2": _FF, "ffn_b_w1": _FF, "ffn_b_w3": _FF, "ffn_b_w2": _FF,
       "w_in": IN_SHARD_PAD - IN_SHARD, "w_uq": 0, "w_ukv": 0, "w_ple": 0, "w_out": 0, "w_ple_gate": 0}
TRANSPOSED = ("ffn_a_w1", "ffn_a_w3", "ffn_b_w1", "ffn_b_w3", "w_in")
SMALL = tuple(n for n in WEIGHTS if n not in BIG)
PACK = 1024


def _pack_small(d):
    parts = []
    for n in SMALL:
        flat = d[n].reshape(-1)
        parts.append(jnp.pad(flat, (0, (-flat.shape[0]) % PACK)))
    return jnp.concatenate(parts).reshape(-1, 128)


def _unpack_small(buf, like):
    flat = buf.reshape(-1)
    out, pos = {}, 0
    for n in SMALL:
        size = math.prod(like[n].shape)
        out[n] = flat[pos:pos + size].reshape(like[n].shape)
        pos += size + (-size) % PACK
    return out


def kernel(*args):
    names = (("x", "p", "positions") + WEIGHTS + ("loss_target",) + tuple("m_" + n for n in WEIGHTS)
             + tuple("v_" + n for n in WEIGHTS))
    a = dict(zip(names, args, strict=True))
    x, p, positions, target = a["x"][0], a["p"][:, 0], a["positions"][0], a["loss_target"][0]
    for n in TRANSPOSED:
        for pre in ("", "m_", "v_"):
            a[pre + n] = jnp.swapaxes(a[pre + n], 1, 2)

    place = jnp.stack([2 * lax.axis_index("x") + lax.axis_index("y"), lax.axis_index("c")]).astype(jnp.int32)
    small = {n: a[n] for n in SMALL}
    tabs = rope_tables(positions)
    first = ("ffn_a_w1", "ffn_a_w3", "ffn_a_w2")
    rest = tuple(n for n in BIG if n not in first)
    prep = lambda n, l, after: prep_shard(f"prep_{n}_{l}", a[n], l, BIG[n], place, after)

    def finish_gather(tag, started, after):
        send, recv, flying, _ = started
        arrived = gather_wait(f"gather_{tag}_wait", send, recv, flying, after)
        send, recv, flying, token = forward_start(f"forward_{tag}_start", arrived)
        return gather_wait(f"forward_{tag}_wait", send, recv, flying, token)

    ga = gather_start("gather_l0a_start", [prep(n, 0, None) for n in first], None)
    gb = gather_start("gather_l0b_start", [prep(n, 0, ga[3]) for n in rest], None)
    slots1 = [prep(n, 1, gb[3]) for n in BIG]
    G0 = dict(zip(first, finish_gather("l0a", ga, slots1[-1])))
    later = {}

    def after_first_ffn(h1, Gl, small_):
        later["G0"] = {**Gl, **dict(zip(rest, finish_gather("l0b", gb, h1)))}
        later["g1"] = gather_start("gather_l1_start", slots1, later["G0"]["w_uq"])
        return later["G0"], {**small_, "mix_norm": small_["mix_norm"] + later["g1"][3][0, 0]}

    def before_ple(h3):
        send, recv, flying, _ = later["g1"]
        later["f1"] = forward_start("forward_l1_start", gather_wait("gather_l1_wait", send, recv, flying, h3))
        return later["f1"][3]

    h, saved0 = layer_fwd(0, x, p[0], G0, small, tabs, after_first_ffn, before_ple)
    G0 = later["G0"]
    G1 = dict(zip(BIG, gather_wait("forward_l1_wait", *later["f1"][:3], h)))
    h, saved1 = layer_fwd(1, h, p[1], G1, small, tabs)
    dh, loss = loss_grad(h, target)
    loss = lax.psum(loss, ("x", "y", "c"))

    def start_reduce(tag, names, gl):
        mine = [gl[n] for n in names]
        theirs = exchange_halves(f"exchange_{tag}", mine)
        parts = [add_sibling(f"add_sibling_{n}_{tag}", d, r, place) for n, d, r in zip(names, mine, theirs)]
        return scatter_start(f"scatter_{tag}_start", parts)

    def finish_reduce(tag, names, started, after):
        send, recv, parts, lands, _ = started
        parts, slabs = scatter_wait(f"scatter_{tag}_wait", send, recv, parts, lands, after)
        halves = [add_chips(f"add_chips_{n}_{tag}", q, pt, place) for n, q, pt in zip(names, slabs, parts)]
        return dict(zip(names, share_halves(f"share_{tag}", halves)))

    def update(names, full, layer, prev):
        return {n: adamw(f"adamw_{n}_{layer}", a[n], full[n], a["m_" + n], a["v_" + n], layer, prev and prev[n])
                for n in names}

    group_b = ("ffn_a_w1", "ffn_a_w3", "ffn_a_w2")
    group_a = tuple(n for n in BIG if n not in group_b)
    grads = [None, None]
    dh, grads[1] = layer_bwd(1, dh, p[1], G1, small, tabs, saved1)
    red1 = start_reduce("l1", tuple(BIG), grads[1])
    w0 = {**saved0[0], "ple_gate_norm": saved0[0]["ple_gate_norm"] + red1[4][0, 0]}
    started = {}

    def before_ffn_a(gl):
        started["a"] = start_reduce("l0a", group_a, gl)
        return started["a"][4]

    gx, grads[0] = layer_bwd(0, dh, p[0], G0, small, tabs, (w0,) + saved0[1:], before_ffn_a)
    started["b"] = start_reduce("l0b", group_b, grads[0])
    outs1 = update(BIG, finish_reduce("l1", tuple(BIG), red1, started["b"][4]), 1, None)
    behind = outs1[group_b[-1]][1]
    full0 = {**finish_reduce("l0a", group_a, started["a"], behind), **finish_reduce("l0b", group_b, started["b"], behind)}
    outs0 = update(BIG, full0, 0, outs1)

    out_g, out_d, out_m, out_v = {}, {}, {}, {}
    for n in BIG:
        outs = [jnp.swapaxes(o, 1, 2) for o in outs0[n]] if n in TRANSPOSED else outs0[n]
        out_g[n], out_d[n], out_m[n], out_v[n] = outs

    gs = allreduce_small(_pack_small({n: jnp.stack([grads[0][n], grads[1][n]]) for n in SMALL}))
    rows = gs.shape[0] // 2
    packed = [_pack_small(d).reshape(2, rows, 128) for d in
              (small, {n: a["m_" + n] for n in SMALL}, {n: a["v_" + n] for n in SMALL})]
    gs = gs.reshape(2, rows, 128)
    sm = adamw("adamw_small_0", packed[0], gs[0], packed[1], packed[2], 0)
    sm = adamw("adamw_small_1", packed[0], gs[1], packed[1], packed[2], 1, sm)
    for dst, buf in zip((out_g, out_d, out_m, out_v), sm):
        dst.update(_unpack_small(buf, small))

    return (loss, gx[None], *[out_g[n] for n in WEIGHTS], *[out_d[n] for n in WEIGHTS],
            *[out_m[n] for n in WEIGHTS], *[out_v[n] for n in WEIGHTS])
```

```python
import math

import jax
import jax.numpy as jnp
from jax import lax
from jax.experimental import pallas as pl
from jax.experimental.pallas import tpu as pltpu

F32 = jnp.float32
BF16 = jnp.bfloat16

D_MODEL = 2048
D_FF = 5504
N_CHIPS = 4
FF_SHARD = D_FF // N_CHIPS
FF_PAD = 1408
FF_P = N_CHIPS * FF_PAD
HEADS = 8
QK_NOPE = 128
QK_ROPE = 64
QK_DIM = 192
HEAD_PAD = 256
V_DIM = 128
Q_RANK = 512
KV_RANK = 256
ATTN_W = 1024
GM_W = 1024
GROUPS = 8
CHUNK = 128
PLE_DIM = 256
IN_P = 3072
IN_SHARD = 720
IN_SHARD_PAD = 736
EPS = 1e-6
ROPE_BASE = 10000.0
ATTN_SCALE = QK_DIM ** -0.5
SUB_BLOCKS = 1
VMEM_LIMIT_BYTES = 56 * 1024 * 1024


def _params(sem):
    return pltpu.CompilerParams(dimension_semantics=sem, vmem_limit_bytes=VMEM_LIMIT_BYTES)


def _bf(x):
    return x if x.dtype == BF16 else x.astype(BF16)


def _sigmoid(x):
    return 1.0 / (1.0 + jnp.exp(-x))


_GELU_C = math.sqrt(2.0 / math.pi)


def _gelu(x):
    t = jnp.tanh(_GELU_C * (x + 0.044715 * x * x * x))
    return 0.5 * x * (1.0 + t)


def _gelu_grad(x):
    t = jnp.tanh(_GELU_C * (x + 0.044715 * x * x * x))
    return 0.5 * (1.0 + t) + 0.5 * x * (1.0 - t * t) * _GELU_C * (1.0 + 3 * 0.044715 * x * x)


def op_a(a, tm, tk):
    return (a, (tm, tk), lambda i, j, k: (i, k), 1)


def op_at(a, tm, tk):
    return (a, (tk, tm), lambda i, j, k: (k, i), 0)


def op_b(b, tk, tn):
    return (b, (tk, tn), lambda i, j, k: (k, j), 0)


def op_bt(b, tk, tn):
    return (b, (tn, tk), lambda i, j, k: (j, k), 1)


def op_b_cols(g, pre, tk, tn):
    nb = g.shape[-1] // tn
    none = (None,) * (1 + len(pre))
    return (g, none + (tk, tn), lambda i, j, k: (j // nb,) + tuple(pre) + (k, j % nb), 0)


def op_b_rows(g, pre, tk, tn, koff=0):
    nb = g.shape[-2] // tk
    none = (None,) * (1 + len(pre))
    return (g, none + (tk, tn), lambda i, j, k: ((k + koff) // nb,) + tuple(pre) + ((k + koff) % nb, j), 0)


def op_b_rows_t(g, pre, tk, tn):
    nb = g.shape[-2] // tn
    none = (None,) * (1 + len(pre))
    return (g, none + (tn, tk), lambda i, j, k: (j // nb,) + tuple(pre) + (j % nb, k), 1)


def tile_mn(x, tm, tn):
    return (x, (tm, tn), lambda i, j: (i, j))


def out_mn(M, N, tm, tn, dtype):
    return (jax.ShapeDtypeStruct((M, N), dtype), (tm, tn), lambda i, j: (i, j))


def out_cols(M, ns, tm, tn, dtype):
    nb = ns // tn
    return (jax.ShapeDtypeStruct((N_CHIPS, M, ns), dtype), (None, tm, tn), lambda i, j: (j // nb, i, j % nb))


def matmul(name, grid_mnk, a_ops, b_ops, terms, n_acc, extras, outs, epilogue, acc_tile, n_outer=False, after=None):
    gm, gn, gk = grid_mnk
    na, nb, nx, no = len(a_ops), len(b_ops), len(extras), len(outs)
    nd = 0 if after is None else 1

    def body(*refs):
        a_refs, b_refs = refs[:na], refs[na:na + nb]
        x_refs = refs[na + nb:na + nb + nx]
        o_refs = refs[na + nb + nx + nd:na + nb + nx + nd + no]
        acc_refs = refs[na + nb + nx + nd + no:]
        k = pl.program_id(2)

        @pl.when(k == 0)
        def _():
            for acc in acc_refs:
                acc[...] = jnp.zeros_like(acc)

        for ai, bi, ci in terms:
            dims = (((a_ops[ai][3],), (b_ops[bi][3],)), ((), ()))
            acc_refs[ci][...] += lax.dot_general(_bf(a_refs[ai][...]), _bf(b_refs[bi][...]), dims,
                                                 preferred_element_type=F32)

        @pl.when(k == gk - 1)
        def _():
            res = epilogue([acc[...] for acc in acc_refs], [x[...] for x in x_refs])
            for o, v in zip(o_refs, res):
                o[...] = v.astype(o.dtype)

    if n_outer:
        grid = (gn, gm, gk)

        def ix3(f):
            return lambda j, i, k: f(i, j, k)

        def ix2(f):
            return lambda j, i, k: f(i, j)
    else:
        grid = (gm, gn, gk)

        def ix3(f):
            return lambda i, j, k: f(i, j, k)

        def ix2(f):
            return lambda i, j, k: f(i, j)

    in_specs = [pl.BlockSpec(blk, ix3(f)) for (_, blk, f, _) in list(a_ops) + list(b_ops)]
    in_specs += [pl.BlockSpec(blk, ix2(f)) for (_, blk, f) in extras]
    in_specs += [pl.BlockSpec(memory_space=pl.ANY)] * nd
    out_specs = [pl.BlockSpec(blk, ix2(f)) for (_, blk, f) in outs]
    return pl.pallas_call(
        body,
        name=name,
        grid=grid,
        in_specs=in_specs,
        out_specs=out_specs,
        out_shape=[s for (s, _, _) in outs],
        scratch_shapes=[pltpu.VMEM(acc_tile, F32) for _ in range(n_acc)],
        compiler_params=_params(("parallel", "parallel", "arbitrary")),
    )(*[o[0] for o in a_ops], *[o[0] for o in b_ops], *[x[0] for x in extras], *([after] * nd))


def _acc0(accs, xs):
    return (accs[0],)


def mm_simple(name, a, b_op_fn, M, N, K, tm, tn, tk, out_dtype=F32, a_t=False, extras=(), epilogue=_acc0, outs=None,
              after=None):
    a_op = op_at(a, tm, tk) if a_t else op_a(a, tm, tk)
    outs = outs or [out_mn(M, N, tm, tn, out_dtype)]
    return matmul(name, (M // tm, N // tn, K // tk), [a_op], [b_op_fn(tk, tn)], [(0, 0, 0)], 1,
                  list(extras), outs, epilogue, (tm, tn), after=after)


def rms_fwd(name, x, g, width, col_blk=0, tm=256, out_dtype=BF16):
    T = x.shape[0]

    def body(x_ref, g_ref, o_ref):
        xv = x_ref[...].astype(F32)
        r = lax.rsqrt(jnp.mean(xv * xv, axis=-1, keepdims=True) + EPS)
        o_ref[...] = (xv * r * g_ref[...]).astype(o_ref.dtype)

    return pl.pallas_call(
        body, name=name, grid=(T // tm,),
        in_specs=[pl.BlockSpec((tm, width), lambda i: (i, col_blk)), pl.BlockSpec((1, width), lambda i: (0, 0))],
        out_specs=pl.BlockSpec((tm, width), lambda i: (i, 0)),
        out_shape=jax.ShapeDtypeStruct((T, width), out_dtype),
        compiler_params=_params(("parallel",)),
    )(x, g.reshape(1, width))


def rms_bwd(name, x, g, dn, width, col_blk=0, dres=None, tm=256, with_delta=False, bf16_copy=False):
    T = x.shape[0]
    has_res = dres is not None

    def body(*refs):
        x_ref, g_ref, dn_ref = refs[:3]
        pos = 3
        res_ref = None
        if has_res:
            res_ref = refs[pos]
            pos += 1
        dx_ref, dg_ref = refs[pos], refs[pos + 1]
        delta_ref = refs[pos + 2] if with_delta else None
        lo_ref = refs[-1] if bf16_copy else None
        i = pl.program_id(0)
        xv = x_ref[...].astype(F32)
        r = lax.rsqrt(jnp.mean(xv * xv, axis=-1, keepdims=True) + EPS)
        xh = xv * r
        d = dn_ref[...].astype(F32)
        gd = d * g_ref[...]
        dx = r * (gd - xh * jnp.mean(gd * xh, axis=-1, keepdims=True))
        if has_res:
            dx = dx + res_ref[...]
        dx_ref[...] = dx.astype(dx_ref.dtype)
        if bf16_copy:
            lo_ref[...] = dx.astype(BF16)
        part = jnp.sum(d * xh, axis=0, keepdims=True)

        @pl.when(i == 0)
        def _():
            dg_ref[...] = part

        @pl.when(i > 0)
        def _():
            dg_ref[...] += part

        if with_delta:
            for h in range(width // 128):
                sl = slice(h * 128, (h + 1) * 128)
                s = jnp.sum(dx[:, sl] * xv[:, sl], axis=-1, keepdims=True)
                delta_ref[:, sl] = jnp.broadcast_to(s, (tm, 128))

    in_specs = [pl.BlockSpec((tm, width), lambda i: (i, col_blk)), pl.BlockSpec((1, width), lambda i: (0, 0)),
                pl.BlockSpec((tm, width), lambda i: (i, 0))]
    args = [x, g.reshape(1, width), dn]
    if has_res:
        in_specs.append(pl.BlockSpec((tm, width), lambda i: (i, 0)))
        args.append(dres)
    out_specs = [pl.BlockSpec((tm, width), lambda i: (i, 0)), pl.BlockSpec((1, width), lambda i: (0, 0))]
    out_shape = [jax.ShapeDtypeStruct((T, width), F32), jax.ShapeDtypeStruct((1, width), F32)]
    if with_delta:
        out_specs.append(pl.BlockSpec((tm, width), lambda i: (i, 0)))
        out_shape.append(jax.ShapeDtypeStruct((T, width), F32))
    if bf16_copy:
        out_specs.append(pl.BlockSpec((tm, width), lambda i: (i, 0)))
        out_shape.append(jax.ShapeDtypeStruct((T, width), BF16))
    return pl.pallas_call(
        body, name=name, grid=(T // tm,), in_specs=in_specs, out_specs=out_specs, out_shape=out_shape,
        compiler_params=_params(("arbitrary",)),
    )(*args)


def ffn_fwd(tag, h, g, w1g, w3g, w2g, pre):
    T = h.shape[0]
    n = rms_fwd(f"{tag}_rms", h, g, D_MODEL)
    tm, tn = 512, FF_PAD

    def up_epi(accs, xs):
        a1, a3 = accs
        return a1, a3, a1 * _sigmoid(a1) * a3

    a1, a3, s = matmul(
        f"{tag}_up", (T // tm, FF_P // tn, 1),
        [op_a(n, tm, D_MODEL)], [op_b_rows_t(w1g, pre, D_MODEL, tn), op_b_rows_t(w3g, pre, D_MODEL, tn)],
        [(0, 0, 0), (0, 1, 1)], 2, [],
        [out_mn(T, FF_P, tm, tn, BF16)] * 3, up_epi, (tm, tn), n_outer=True)

    tm2, tn2 = 1024, 1024
    (h_out,) = matmul(
        f"{tag}_down", (T // tm2, D_MODEL // tn2, N_CHIPS),
        [op_a(s, tm2, FF_PAD)], [op_b_rows(w2g, pre, FF_PAD, tn2)],
        [(0, 0, 0)], 1, [tile_mn(h, tm2, tn2)],
        [out_mn(T, D_MODEL, tm2, tn2, F32)], lambda accs, xs: (xs[0] + 0.5 * accs[0],), (tm2, tn2))
    return h_out, (n, a1, a3, s)


def ffn_bwd(tag, dh_out, dh_bf, h, g, res, w1g, w3g, w2g, pre, dw_after=None):
    n, a1, a3, s = res
    T = h.shape[0]
    tm, tn = 512, FF_PAD

    def act_epi(accs, xs):
        ds = 0.5 * accs[0]
        x1, x3 = xs[0].astype(F32), xs[1].astype(F32)
        sg = _sigmoid(x1)
        silu = x1 * sg
        return ds * x3 * (sg + silu * (1.0 - sg)), ds * silu

    da1, da3 = matmul(
        f"{tag}_dact", (T // tm, FF_P // tn, 1),
        [op_a(dh_bf, tm, D_MODEL)], [op_b_rows_t(w2g, pre, D_MODEL, tn)],
        [(0, 0, 0)], 1, [tile_mn(a1, tm, tn), tile_mn(a3, tm, tn)],
        [out_mn(T, FF_P, tm, tn, BF16)] * 2, act_epi, (tm, tn), n_outer=True)

    tk = 1024

    def dw_t(nm, left, right, scale):
        (dw,) = matmul(
            f"{tag}_{nm}", (FF_P // FF_PAD, D_MODEL // 1024, T // tk),
            [op_at(left, FF_PAD, tk)], [op_b(right, tk, 1024)],
            [(0, 0, 0)], 1, [], [out_mn(FF_P, D_MODEL, FF_PAD, 1024, BF16)],
            lambda accs, xs: (scale * accs[0],), (FF_PAD, 1024), after=dw_after)
        return dw

    dw2 = dw_t("dw2", s, dh_bf, 0.5)
    dw1 = dw_t("dw1", da1, n, 1.0)
    dw3 = dw_t("dw3", da3, n, 1.0)

    tm2, tn2 = 1024, 1024
    (dn,) = matmul(
        f"{tag}_dn", (T // tm2, D_MODEL // tn2, N_CHIPS),
        [op_a(da1, tm2, FF_PAD), op_a(da3, tm2, FF_PAD)],
        [op_b_rows(w1g, pre, FF_PAD, tn2), op_b_rows(w3g, pre, FF_PAD, tn2)],
        [(0, 0, 0), (1, 1, 0)], 1, [], [out_mn(T, D_MODEL, tm2, tn2, F32)], _acc0, (tm2, tn2))
    dh, dg, dh_lo = rms_bwd(f"{tag}_rms_bwd", h, g, dn, D_MODEL, dres=dh_out, bf16_copy=True)
    return dh, dh_lo, dg, dw1, dw3, dw2


def rope_tables(positions):
    inv_freq = ROPE_BASE ** (-jnp.arange(0, QK_ROPE, 2, dtype=F32) / QK_ROPE)
    ang = positions.astype(F32)[:, None] * inv_freq
    cos, sin = jnp.cos(ang), jnp.sin(ang)
    T = positions.shape[0]
    one, zero = jnp.ones((T, QK_NOPE), F32), jnp.zeros((T, 64), F32)
    z32, z128 = jnp.zeros((T, 32), F32), jnp.zeros((T, QK_NOPE), F32)
    c = jnp.concatenate([one, cos, cos, zero], axis=1)
    s1 = jnp.concatenate([z128, -sin, z32, zero], axis=1)
    s2 = jnp.concatenate([z128, z32, sin, zero], axis=1)
    return c, s1, s2


def _rope(y, c, s1, s2):
    return y * c + pltpu.roll(y, HEAD_PAD - 32, 1) * s1 + pltpu.roll(y, 32, 1) * s2


def _rope_t(d, c, s1, s2):
    return d * c + pltpu.roll(d * s1, 32, 1) + pltpu.roll(d * s2, HEAD_PAD - 32, 1)


def _head_norm(x):
    r = lax.rsqrt(jnp.sum(x * x, axis=-1, keepdims=True) * (1.0 / QK_DIM) + EPS)
    return x * r, r


def qk_prep_fwd(tag, q_raw, kk_raw, z_p, gq, gk, tabs, tm=256):
    T = q_raw.shape[0]
    c, s1, s2 = tabs

    def body(q_ref, k_ref, kr_ref, gq_ref, gk_ref, c_ref, s1_ref, s2_ref, qo_ref, ko_ref):
        cv, s1v, s2v = c_ref[...], s1_ref[...], s2_ref[...]
        kr = kr_ref[...]
        for h in range(HEADS):
            sl = slice(h * HEAD_PAD, (h + 1) * HEAD_PAD)
            xh, _ = _head_norm(q_ref[:, sl])
            qo_ref[:, sl] = (_rope(xh * gq_ref[...], cv, s1v, s2v) * ATTN_SCALE).astype(BF16)
            xh, _ = _head_norm(k_ref[:, sl] + kr)
            ko_ref[:, sl] = _rope(xh * gk_ref[...], cv, s1v, s2v).astype(BF16)

    row = lambda i: (i, 0)
    full = pl.BlockSpec((tm, HEADS * HEAD_PAD), row)
    tab = pl.BlockSpec((tm, HEAD_PAD), row)
    vec = pl.BlockSpec((1, HEAD_PAD), lambda i: (0, 0))
    return pl.pallas_call(
        body, name=f"{tag}_qk_prep", grid=(T // tm,),
        in_specs=[full, full, pl.BlockSpec((tm, HEAD_PAD), lambda i: (i, 3)), vec, vec, tab, tab, tab],
        out_specs=[full, full],
        out_shape=[jax.ShapeDtypeStruct((T, HEADS * HEAD_PAD), BF16)] * 2,
        compiler_params=_params(("parallel",)),
    )(q_raw, kk_raw, z_p, gq, gk, c, s1, s2)


def qk_prep_bwd(tag, dq_full, dk_full, q_raw, kk_raw, z_p, gq, gk, tabs, tm=256):
    T = q_raw.shape[0]
    c, s1, s2 = tabs

    def body(dq_ref, dk_ref, q_ref, k_ref, kr_ref, gq_ref, gk_ref, c_ref, s1_ref, s2_ref,
             dqr_ref, dkr_ref, dz_ref, dgq_ref, dgk_ref):
        i = pl.program_id(0)
        cv, s1v, s2v = c_ref[...], s1_ref[...], s2_ref[...]
        kr = kr_ref[...]
        lane = lax.broadcasted_iota(jnp.int32, (tm, HEAD_PAD), 1)
        slot = ((lane >= QK_NOPE) & (lane < QK_DIM)).astype(F32)

        def one(x, g, d):
            xh, r = _head_norm(x)
            dy = _rope_t(d, cv, s1v, s2v)
            gd = dy * g
            dx = r * (gd - xh * (jnp.sum(gd * xh, axis=-1, keepdims=True) * (1.0 / QK_DIM)))
            return dx, jnp.sum(dy * xh, axis=0, keepdims=True)

        dgq = jnp.zeros((1, HEAD_PAD), F32)
        dgk = jnp.zeros((1, HEAD_PAD), F32)
        dz = jnp.zeros((tm, HEAD_PAD), F32)
        for h in range(HEADS):
            sl = slice(h * HEAD_PAD, (h + 1) * HEAD_PAD)
            dx, dg = one(q_ref[:, sl], gq_ref[...], dq_ref[:, sl].astype(F32) * ATTN_SCALE)
            dqr_ref[:, sl] = dx
            dgq = dgq + dg
            dx, dg = one(k_ref[:, sl] + kr, gk_ref[...], dk_ref[:, sl].astype(F32))
            dkr_ref[:, sl] = dx
            dgk = dgk + dg
            dz = dz + dx
        dz_ref[...] = dz * slot

        @pl.when(i == 0)
        def _():
            dgq_ref[...] = dgq
            dgk_ref[...] = dgk

        @pl.when(i > 0)
        def _():
            dgq_ref[...] += dgq
            dgk_ref[...] += dgk

    row = lambda i: (i, 0)
    full = pl.BlockSpec((tm, HEADS * HEAD_PAD), row)
    tab = pl.BlockSpec((tm, HEAD_PAD), row)
    vec = pl.BlockSpec((1, HEAD_PAD), lambda i: (0, 0))
    return pl.pallas_call(
        body, name=f"{tag}_qk_prep_bwd", grid=(T // tm,),
        in_specs=[full, full, full, full, pl.BlockSpec((tm, HEAD_PAD), lambda i: (i, 3)), vec, vec, tab, tab, tab],
        out_specs=[full, full, tab, vec, vec],
        out_shape=[jax.ShapeDtypeStruct((T, HEADS * HEAD_PAD), F32)] * 2
        + [jax.ShapeDtypeStruct((T, HEAD_PAD), F32)] + [jax.ShapeDtypeStruct((1, HEAD_PAD), F32)] * 2,
        compiler_params=_params(("arbitrary",)),
    )(dq_full, dk_full, q_raw, kk_raw, z_p, gq, gk, c, s1, s2)


def attn_fwd(tag, q_full, k_full, vv, blk=512):
    T = q_full.shape[0]
    nb = T // blk
    neg = float(jnp.finfo(jnp.float32).min)

    def body(q_ref, k_ref, v_ref, o_ref, lse_ref, m_ref, l_ref, acc_ref):
        i = pl.program_id(1)
        m_ref[...] = jnp.full_like(m_ref, neg)
        l_ref[...] = jnp.zeros_like(l_ref)
        acc_ref[...] = jnp.zeros_like(acc_ref)
        sub = blk // SUB_BLOCKS

        def step(j, masked):
            rows = pl.ds(pl.multiple_of(j * blk, blk), blk)
            k, v = k_ref[rows, :], v_ref[rows, :]
            for part in range(SUB_BLOCKS):
                qs = slice(part * sub, (part + 1) * sub)
                s = lax.dot_general(q_ref[qs, :], k, (((1,), (1,)), ((), ())), preferred_element_type=F32)
                if masked:
                    row = lax.broadcasted_iota(jnp.int32, (sub, blk), 0) + part * sub
                    col = lax.broadcasted_iota(jnp.int32, (sub, blk), 1)
                    s = jnp.where(col <= row, s, neg)
                m_prev = m_ref[qs, :]
                m_new = jnp.maximum(m_prev, jnp.max(s, axis=-1, keepdims=True))
                alpha = jnp.exp(m_prev - m_new)
                p = jnp.exp(s - m_new[:, :1])
                l_ref[qs, :] = alpha * l_ref[qs, :] + jnp.sum(p, axis=-1, keepdims=True)
                acc_ref[qs, :] = alpha * acc_ref[qs, :] + jnp.dot(p.astype(BF16), v, preferred_element_type=F32)
                m_ref[qs, :] = m_new

        def off_diagonal(j, carry):
            step(j, False)
            return carry

        lax.fori_loop(0, i, off_diagonal, 0)
        step(i, True)
        o_ref[...] = acc_ref[...] / l_ref[...]
        lse_ref[...] = m_ref[...] + jnp.log(l_ref[...])

    return pl.pallas_call(
        body, name=f"{tag}_attn_fwd", grid=(HEADS, nb),
        in_specs=[pl.BlockSpec((blk, HEAD_PAD), lambda h, i: (i, h)),
                  pl.BlockSpec((T, HEAD_PAD), lambda h, i: (0, h)), pl.BlockSpec((T, V_DIM), lambda h, i: (0, h))],
        out_specs=[pl.BlockSpec((blk, V_DIM), lambda h, i: (i, h))] * 2,
        out_shape=[jax.ShapeDtypeStruct((T, ATTN_W), F32)] * 2,
        scratch_shapes=[pltpu.VMEM((blk, V_DIM), F32)] * 3,
        compiler_params=_params(("parallel", "parallel")),
    )(q_full, k_full, vv)


def attn_bwd(tag, q_full, k_full, vv, do, lse, delta, blk=512):
    T = q_full.shape[0]
    nb = T // blk
    neg = float(jnp.finfo(jnp.float32).min)

    def body(q_ref, k_ref, v_ref, do_ref, lse_ref, dl_ref, dq_ref, dk_ref, dv_ref, dk_acc, dv_acc):
        j = pl.program_id(1)

        @pl.when(j == 0)
        def _():
            dq_ref[...] = jnp.zeros_like(dq_ref)

        dk_acc[...] = jnp.zeros_like(dk_acc)
        dv_acc[...] = jnp.zeros_like(dv_acc)
        k, v = k_ref[...], v_ref[...]

        sub = blk // SUB_BLOCKS

        def step(i, masked):
            for part in range(SUB_BLOCKS):
                rows = pl.ds(pl.multiple_of(i * blk + part * sub, sub), sub)
                q = q_ref[rows, :]
                s = lax.dot_general(q, k, (((1,), (1,)), ((), ())), preferred_element_type=F32)
                if masked:
                    row = lax.broadcasted_iota(jnp.int32, (sub, blk), 0) + part * sub
                    col = lax.broadcasted_iota(jnp.int32, (sub, blk), 1)
                    s = jnp.where(col <= row, s, neg)
                p = jnp.exp(s - lse_ref[rows, :1])
                dob = _bf(do_ref[rows, :])
                dv_acc[...] += lax.dot_general(p.astype(BF16), dob, (((0,), (0,)), ((), ())), preferred_element_type=F32)
                dp = lax.dot_general(dob, v, (((1,), (1,)), ((), ())), preferred_element_type=F32)
                ds = (p * (dp - dl_ref[rows, :1])).astype(BF16)
                dk_acc[...] += lax.dot_general(ds, q, (((0,), (0,)), ((), ())), preferred_element_type=F32)
                dq_ref[rows, :] += jnp.dot(ds, k, preferred_element_type=F32)

        def off_diagonal(i, carry):
            step(i, False)
            return carry

        step(j, True)
        lax.fori_loop(j + 1, nb, off_diagonal, 0)
        dk_ref[...] = dk_acc[...]
        dv_ref[...] = dv_acc[...]

    head = lambda h, j: (0, h)
    kv_ix = lambda h, j: (j, h)
    return pl.pallas_call(
        body, name=f"{tag}_attn_bwd", grid=(HEADS, nb),
        in_specs=[pl.BlockSpec((T, HEAD_PAD), head), pl.BlockSpec((blk, HEAD_PAD), kv_ix),
                  pl.BlockSpec((blk, V_DIM), kv_ix), pl.BlockSpec((T, V_DIM), head),
                  pl.BlockSpec((T, V_DIM), head), pl.BlockSpec((T, V_DIM), head)],
        out_specs=[pl.BlockSpec((T, HEAD_PAD), head),
                   pl.BlockSpec((blk, HEAD_PAD), kv_ix), pl.BlockSpec((blk, V_DIM), kv_ix)],
        out_shape=[jax.ShapeDtypeStruct((T, HEADS * HEAD_PAD), F32)] * 2 + [jax.ShapeDtypeStruct((T, ATTN_W), F32)],
        scratch_shapes=[pltpu.VMEM((blk, HEAD_PAD), F32), pltpu.VMEM((blk, V_DIM), F32)],
        compiler_params=_params(("parallel", "arbitrary")),
    )(q_full, k_full, vv, do, lse, delta)


def _gm_forward(u, v, gv, wc_ref, bb_ref, nchunk):
    ug = _gelu(u)
    vg = _gelu(v)
    rv = lax.rsqrt(jnp.mean(vg * vg, axis=-1, keepdims=True) + EPS)
    vhat = vg * rv
    vn = (vhat * gv).astype(BF16)
    gates = []
    for cidx in range(nchunk):
        rows = slice(cidx * CHUNK, (cidx + 1) * CHUNK)
        gates.append(jnp.concatenate(
            [jnp.dot(wc_ref[gidx], vn[rows, gidx * 128:(gidx + 1) * 128], preferred_element_type=F32) + bb_ref[gidx]
             for gidx in range(GROUPS)], axis=1))
    gate = jnp.concatenate(gates, axis=0)
    return ug, vhat, rv, vn, gate


def gmlp_fwd(tag, z_p, gv, gout, wc, bb, tm=256):
    T = z_p.shape[0]
    nchunk = tm // CHUNK

    def body(u_ref, v_ref, gv_ref, go_ref, wc_ref, bb_ref, o_ref):
        ug, _, _, _, gate = _gm_forward(u_ref[...], v_ref[...], gv_ref[...], wc_ref, bb_ref, nchunk)
        go = ug * gate
        ro = lax.rsqrt(jnp.mean(go * go, axis=-1, keepdims=True) + EPS)
        o_ref[...] = (go * ro * go_ref[...]).astype(BF16)

    vec = pl.BlockSpec((1, GM_W), lambda i: (0, 0))
    w3 = pl.BlockSpec((GROUPS, CHUNK, CHUNK), lambda i: (0, 0, 0))
    return pl.pallas_call(
        body, name=f"{tag}_gmlp_fwd", grid=(T // tm,),
        in_specs=[pl.BlockSpec((tm, GM_W), lambda i: (i, 1)), pl.BlockSpec((tm, GM_W), lambda i: (i, 2)), vec, vec, w3, w3],
        out_specs=pl.BlockSpec((tm, GM_W), lambda i: (i, 0)),
        out_shape=jax.ShapeDtypeStruct((T, GM_W), BF16),
        compiler_params=_params(("parallel",)),
    )(z_p, z_p, gv.reshape(1, GM_W), gout.reshape(1, GM_W), wc, bb)


def gmlp_bwd(tag, z_p, dmixed, gv, gout, wc, bb, tm=256):
    T = z_p.shape[0]
    nchunk = tm // CHUNK

    def body(u_ref, v_ref, dm_ref, gv_ref, go_ref, wc_ref, bb_ref, du_ref, dv_ref, dwc_ref, dbb_ref, dgv_ref, dgo_ref):
        i = pl.program_id(0)
        u, v = u_ref[...], v_ref[...]
        ug, vhat, rv, vn, gate = _gm_forward(u, v, gv_ref[...], wc_ref, bb_ref, nchunk)
        go = ug * gate
        ro = lax.rsqrt(jnp.mean(go * go, axis=-1, keepdims=True) + EPS)
        ohat = go * ro
        dm = dm_ref[...].astype(F32)
        dgo_part = jnp.sum(dm * ohat, axis=0, keepdims=True)
        doh = dm * go_ref[...]
        dgo = ro * (doh - ohat * jnp.mean(doh * ohat, axis=-1, keepdims=True))
        du_ref[...] = dgo * gate * _gelu_grad(u)
        dgate = dgo * ug
        dgb = dgate.astype(BF16)
        dvn_rows = []
        dwc_parts = []
        dbb_parts = []
        for gidx in range(GROUPS):
            cols = slice(gidx * 128, (gidx + 1) * 128)
            dw = jnp.zeros((CHUNK, CHUNK), F32)
            db = jnp.zeros((CHUNK, 128), F32)
            for cidx in range(nchunk):
                rows = slice(cidx * CHUNK, (cidx + 1) * CHUNK)
                dw = dw + lax.dot_general(dgb[rows, cols], vn[rows, cols], (((1,), (1,)), ((), ())),
                                          preferred_element_type=F32)
                db = db + dgate[rows, cols]
            dwc_parts.append(dw)
            dbb_parts.append(db)
        for cidx in range(nchunk):
            rows = slice(cidx * CHUNK, (cidx + 1) * CHUNK)
            dvn_rows.append(jnp.concatenate(
                [lax.dot_general(wc_ref[gidx], dgb[rows, gidx * 128:(gidx + 1) * 128], (((0,), (0,)), ((), ())),
                                 preferred_element_type=F32) for gidx in range(GROUPS)], axis=1))
        dvn = jnp.concatenate(dvn_rows, axis=0)
        dgv_part = jnp.sum(dvn * vhat, axis=0, keepdims=True)
        dvh = dvn * gv_ref[...]
        dvg = rv * (dvh - vhat * jnp.mean(dvh * vhat, axis=-1, keepdims=True))
        dv_ref[...] = dvg * _gelu_grad(v)

        @pl.when(i == 0)
        def _():
            for gidx in range(GROUPS):
                dwc_ref[gidx] = dwc_parts[gidx]
                dbb_ref[gidx] = dbb_parts[gidx]
            dgv_ref[...] = dgv_part
            dgo_ref[...] = dgo_part

        @pl.when(i > 0)
        def _():
            for gidx in range(GROUPS):
                dwc_ref[gidx] += dwc_parts[gidx]
                dbb_ref[gidx] += dbb_parts[gidx]
            dgv_ref[...] += dgv_part
            dgo_ref[...] += dgo_part

    vec = pl.BlockSpec((1, GM_W), lambda i: (0, 0))
    w3 = pl.BlockSpec((GROUPS, CHUNK, CHUNK), lambda i: (0, 0, 0))
    blk = pl.BlockSpec((tm, GM_W), lambda i: (i, 0))
    return pl.pallas_call(
        body, name=f"{tag}_gmlp_bwd", grid=(T // tm,),
        in_specs=[pl.BlockSpec((tm, GM_W), lambda i: (i, 1)), pl.BlockSpec((tm, GM_W), lambda i: (i, 2)),
                  pl.BlockSpec((tm, GM_W), lambda i: (i, 1)), vec, vec, w3, w3],
        out_specs=[blk, blk, w3, w3, vec, vec],
        out_shape=[jax.ShapeDtypeStruct((T, GM_W), F32)] * 2 + [jax.ShapeDtypeStruct((GROUPS, CHUNK, CHUNK), F32)] * 2
        + [jax.ShapeDtypeStruct((1, GM_W), F32)] * 2,
        compiler_params=_params(("arbitrary",)),
    )(z_p, z_p, dmixed, gv.reshape(1, GM_W), gout.reshape(1, GM_W), wc, bb)


def mixer_fwd(tag, h, w, tabs, wout_g, pre):
    T = h.shape[0]
    n2 = rms_fwd(f"{tag}_mix_rms", h, w["mix_norm"], D_MODEL)
    (z_p,) = mm_simple(f"{tag}_win", n2, lambda tk, tn: op_bt(w["w_in_pt"], tk, tn), T, IN_P, D_MODEL, 512, 1024, D_MODEL)
    cqn = rms_fwd(f"{tag}_cq_rms", z_p, w["q_a_norm"], Q_RANK, col_blk=0)
    ckvn = rms_fwd(f"{tag}_ckv_rms", z_p, w["kv_a_norm"], KV_RANK, col_blk=2)
    (q_raw,) = mm_simple(f"{tag}_wq", cqn, lambda tk, tn: op_b(w["wq_p"], tk, tn), T, 2048, Q_RANK, 512, 1024, Q_RANK)
    (kk_raw,) = mm_simple(f"{tag}_wk", ckvn, lambda tk, tn: op_b(w["wk_p"], tk, tn), T, 2048, KV_RANK, 512, 1024, KV_RANK)
    (vv,) = mm_simple(f"{tag}_wv", ckvn, lambda tk, tn: op_b(w["wv"], tk, tn), T, ATTN_W, KV_RANK, 512, 1024, KV_RANK,
                      out_dtype=BF16)
    q_full, k_full = qk_prep_fwd(tag, q_raw, kk_raw, z_p, w["gq_p"], w["gk_p"], tabs)
    a_out, lse = attn_fwd(tag, q_full, k_full, vv)
    mixed_a = rms_fwd(f"{tag}_ao_rms", a_out, w["attn_out_norm"], ATTN_W)
    mixed_g = gmlp_fwd(tag, z_p, w["gm_v_norm"], w["gm_out_norm"], w["wc"], w["bb"])
    tm, tn, tk = 512, 1024, 512
    (h2,) = matmul(
        f"{tag}_wout", (T // tm, D_MODEL // tn, ATTN_W // tk),
        [op_a(mixed_a, tm, tk), op_a(mixed_g, tm, tk)],
        [op_b_rows(wout_g, pre, tk, tn), op_b_rows(wout_g, pre, tk, tn, koff=ATTN_W // tk)],
        [(0, 0, 0), (1, 1, 0)], 1, [tile_mn(h, tm, tn)], [out_mn(T, D_MODEL, tm, tn, F32)],
        lambda accs, xs: (xs[0] + accs[0],), (tm, tn))
    res = dict(n2=n2, z_p=z_p, cqn=cqn, ckvn=ckvn, q_raw=q_raw, kk_raw=kk_raw, vv=vv, q_full=q_full, k_full=k_full,
               a_out=a_out, lse=lse, mixed_a=mixed_a, mixed_g=mixed_g)
    return h2, res


def mixer_bwd(tag, dh2, dh2_bf, h, w, tabs, wout_g, pre, r, after=None):
    T = h.shape[0]
    g = {}
    (dmixed,) = mm_simple(f"{tag}_dmixed", dh2_bf, lambda tk, tn: op_b_rows_t(wout_g, pre, tk, tn), T, D_MODEL, D_MODEL,
                          512, 512, D_MODEL, after=after)
    (dwo_a,) = mm_simple(f"{tag}_dwout_a", r["mixed_a"], lambda tk, tn: op_b(dh2_bf, tk, tn), ATTN_W, D_MODEL, T,
                         1024, 1024, 512, a_t=True, out_dtype=BF16)
    (dwo_g,) = mm_simple(f"{tag}_dwout_g", r["mixed_g"], lambda tk, tn: op_b(dh2_bf, tk, tn), GM_W, D_MODEL, T,
                         1024, 1024, 512, a_t=True, out_dtype=BF16)
    g["w_out"] = jnp.concatenate([dwo_a, dwo_g], axis=0)
    da_out, g["attn_out_norm"], delta = rms_bwd(f"{tag}_ao_rms_bwd", r["a_out"], w["attn_out_norm"], dmixed, ATTN_W,
                                                with_delta=True)
    dq_full, dk_full, dvv = attn_bwd(tag, r["q_full"], r["k_full"], r["vv"], da_out, r["lse"], delta)
    dq_raw, dkk_raw, dzkr, g["gq_p"], g["gk_p"] = qk_prep_bwd(tag, dq_full, dk_full, r["q_raw"], r["kk_raw"], r["z_p"],
                                                            w["gq_p"], w["gk_p"], tabs)
    (g["wq_p"],) = mm_simple(f"{tag}_dwq", r["cqn"], lambda tk, tn: op_b(dq_raw, tk, tn), Q_RANK, 2048, T, Q_RANK, 1024, 512,
                             a_t=True, out_dtype=BF16)
    (g["wk_p"],) = mm_simple(f"{tag}_dwk", r["ckvn"], lambda tk, tn: op_b(dkk_raw, tk, tn), KV_RANK, 2048, T, KV_RANK, 1024,
                             512, a_t=True, out_dtype=BF16)
    (g["wv"],) = mm_simple(f"{tag}_dwv", r["ckvn"], lambda tk, tn: op_b(dvv, tk, tn), KV_RANK, ATTN_W, T, KV_RANK, 1024, 512,
                           a_t=True, out_dtype=BF16)
    (dcqn,) = mm_simple(f"{tag}_dcqn", dq_raw, lambda tk, tn: op_bt(w["wq_p"], tk, tn), T, Q_RANK, 2048, 512, Q_RANK, 2048)
    (dck1,) = mm_simple(f"{tag}_dckvn_k", dkk_raw, lambda tk, tn: op_bt(w["wk_p"], tk, tn), T, KV_RANK, 2048, 512, KV_RANK,
                        2048)
    (dckvn,) = mm_simple(f"{tag}_dckvn_v", dvv, lambda tk, tn: op_bt(w["wv"], tk, tn), T, KV_RANK, ATTN_W, 512, KV_RANK,
                         ATTN_W, extras=[tile_mn(dck1, 512, KV_RANK)], epilogue=lambda accs, xs: (accs[0] + xs[0],))
    dc_q, g["q_a_norm"] = rms_bwd(f"{tag}_cq_rms_bwd", r["z_p"], w["q_a_norm"], dcqn, Q_RANK, col_blk=0)
    dc_kv, g["kv_a_norm"] = rms_bwd(f"{tag}_ckv_rms_bwd", r["z_p"], w["kv_a_norm"], dckvn, KV_RANK, col_blk=2)
    du, dv, g["wc"], g["bb"], g["gm_v_norm"], g["gm_out_norm"] = gmlp_bwd(
        tag, r["z_p"], dmixed, w["gm_v_norm"], w["gm_out_norm"], w["wc"], w["bb"])
    dz_p = jnp.concatenate([dc_q, dc_kv, dzkr, du, dv], axis=1).astype(BF16)
    (g["w_in_pt"],) = mm_simple(f"{tag}_dwin", dz_p, lambda tk, tn: op_b(r["n2"], tk, tn), IN_P, D_MODEL, T, 1024, 1024, 512,
                                a_t=True, out_dtype=BF16)
    (dn2,) = mm_simple(f"{tag}_dn2", dz_p, lambda tk, tn: op_b(w["w_in_pt"], tk, tn), T, D_MODEL, IN_P, 512, 1024, IN_P)
    dh1, g["mix_norm"], dh1_bf = rms_bwd(f"{tag}_mix_rms_bwd", h, w["mix_norm"], dn2, D_MODEL, dres=dh2, bf16_copy=True)
    return dh1, dh1_bf, g


def ple_fwd(tag, h3, p_l, w, wpg_g, wple_g, pre):
    T = h3.shape[0]
    (pw,) = mm_simple(f"{tag}_wple", p_l, lambda tk, tn: op_b_cols(wple_g, pre, tk, tn), T, D_MODEL, PLE_DIM, 512, 512,
                      PLE_DIM)
    e = rms_fwd(f"{tag}_ple_rms", pw, w["ple_norm"], D_MODEL, out_dtype=F32)
    n4 = rms_fwd(f"{tag}_pg_rms", h3, w["ple_gate_norm"], D_MODEL)

    def epi(accs, xs):
        gt = _sigmoid(accs[0])
        return xs[0] + gt * xs[1], gt

    tm, tn, tk = 512, 1024, 512
    h4, gate = matmul(
        f"{tag}_wpg", (T // tm, D_MODEL // tn, D_MODEL // tk),
        [op_a(n4, tm, tk)], [op_b_rows(wpg_g, pre, tk, tn)], [(0, 0, 0)], 1,
        [tile_mn(h3, tm, tn), tile_mn(e, tm, tn)],
        [out_mn(T, D_MODEL, tm, tn, F32), out_mn(T, D_MODEL, tm, tn, BF16)], epi, (tm, tn))
    return h4, dict(pw=pw, e=e, n4=n4, gate=gate)


def ple_bwd(tag, dh4, h3, p_l, w, wpg_g, wple_g, pre, r, tm=256):
    T = h3.shape[0]

    def act_body(d_ref, g_ref, e_ref, dpre_ref, de_ref):
        d, gt = d_ref[...], g_ref[...].astype(F32)
        dpre_ref[...] = (d * e_ref[...] * gt * (1.0 - gt)).astype(BF16)
        de_ref[...] = d * gt

    blk = pl.BlockSpec((tm, D_MODEL), lambda i: (i, 0))
    dpre, de = pl.pallas_call(
        act_body, name=f"{tag}_ple_act_bwd", grid=(T // tm,), in_specs=[blk, blk, blk], out_specs=[blk, blk],
        out_shape=[jax.ShapeDtypeStruct((T, D_MODEL), BF16), jax.ShapeDtypeStruct((T, D_MODEL), F32)],
        compiler_params=_params(("parallel",)),
    )(dh4, r["gate"], r["e"])
    g = {}
    (g["w_ple_gate"],) = mm_simple(f"{tag}_dwpg", r["n4"], lambda tk, tn: op_b(dpre, tk, tn), D_MODEL, D_MODEL, T,
                                   1024, 1024, 512, a_t=True, out_dtype=BF16)
    (dn4,) = mm_simple(f"{tag}_dn4", dpre, lambda tk, tn: op_b_rows_t(wpg_g, pre, tk, tn), T, D_MODEL, D_MODEL, 512, 512,
                       D_MODEL)
    dh3, g["ple_gate_norm"], dh3_bf = rms_bwd(f"{tag}_pg_rms_bwd", h3, w["ple_gate_norm"], dn4, D_MODEL, dres=dh4,
                                              bf16_copy=True)
    dpw, g["ple_norm"] = rms_bwd(f"{tag}_ple_rms_bwd", r["pw"], w["ple_norm"], de, D_MODEL)
    (g["w_ple"],) = mm_simple(f"{tag}_dwple", p_l, lambda tk, tn: op_b(dpw, tk, tn), PLE_DIM, D_MODEL, T, PLE_DIM, 512, 512,
                              a_t=True, outs=[out_cols(PLE_DIM, 512, PLE_DIM, 512, BF16)])
    return dh3, dh3_bf, g


def loss_grad(y, target, tm=256):
    T = y.shape[0]

    def body(y_ref, t_ref, dy_ref, l_ref):
        i = pl.program_id(0)
        d = y_ref[...] - t_ref[...]
        dy_ref[...] = d * (1.0 / D_MODEL)
        part = jnp.sum((d * d).reshape(tm // 8, 8, D_MODEL), axis=0)

        @pl.when(i == 0)
        def _():
            l_ref[...] = part

        @pl.when(i > 0)
        def _():
            l_ref[...] += part

    blk = pl.BlockSpec((tm, D_MODEL), lambda i: (i, 0))
    dy, part = pl.pallas_call(
        body, name="loss_grad", grid=(T // tm,), in_specs=[blk, blk],
        out_specs=[blk, pl.BlockSpec((8, D_MODEL), lambda i: (0, 0))],
        out_shape=[jax.ShapeDtypeStruct((T, D_MODEL), F32), jax.ShapeDtypeStruct((8, D_MODEL), F32)],
        compiler_params=_params(("arbitrary",)),
    )(y, target)
    return dy, 0.5 * jnp.sum(part) / D_MODEL


def _unshard_cols(g_l):
    return g_l.transpose(1, 0, 2).reshape(g_l.shape[1], -1)


def _shard_cols(w):
    return w.reshape(w.shape[0], N_CHIPS, -1).transpose(1, 0, 2)


def layer_weights(l, Gl, small):
    w = {k: small[k][l] for k in ("mix_norm", "q_a_norm", "kv_a_norm", "gm_v_norm", "attn_out_norm", "gm_out_norm",
                                  "ple_gate_norm", "ple_norm")}
    wint = Gl["w_in"][:, :IN_SHARD].reshape(-1, D_MODEL)
    z = lambda n: jnp.zeros((n, D_MODEL), BF16)
    w["w_in_pt"] = jnp.concatenate([wint[:768], z(128), wint[768:832], z(64), wint[832:]], axis=0)
    wuq = _unshard_cols(Gl["w_uq"]).reshape(Q_RANK, HEADS, QK_DIM)
    w["wq_p"] = jnp.pad(wuq, ((0, 0), (0, 0), (0, HEAD_PAD - QK_DIM))).reshape(Q_RANK, HEADS * HEAD_PAD)
    wukv = _unshard_cols(Gl["w_ukv"]).reshape(KV_RANK, HEADS, QK_NOPE + V_DIM)
    w["wk_p"] = jnp.pad(wukv[:, :, :QK_NOPE], ((0, 0), (0, 0), (0, HEAD_PAD - QK_NOPE))).reshape(KV_RANK, HEADS * HEAD_PAD)
    w["wv"] = wukv[:, :, QK_NOPE:].reshape(KV_RANK, ATTN_W)
    w["gq_p"] = jnp.pad(small["q_norm"][l], (0, HEAD_PAD - QK_DIM)).reshape(1, HEAD_PAD)
    w["gk_p"] = jnp.pad(small["k_norm"][l], (0, HEAD_PAD - QK_DIM)).reshape(1, HEAD_PAD)
    tril = jnp.tril(jnp.ones((CHUNK, CHUNK), dtype=bool))
    w["wc"] = jnp.where(tril[None], small["gm_ws"][l], 0.0).astype(BF16)
    w["bb"] = jnp.broadcast_to(small["gm_bs"][l][:, :, None], (GROUPS, CHUNK, 128)).astype(F32)
    return w


def mixer_grads_to_shards(g):
    out = {}
    dwint = g["w_in_pt"]
    dwint = jnp.concatenate([dwint[:768], dwint[896:960], dwint[1024:]], axis=0).reshape(N_CHIPS, IN_SHARD, D_MODEL)
    out["w_in"] = jnp.pad(dwint, ((0, 0), (0, IN_SHARD_PAD - IN_SHARD), (0, 0)))
    dwuq = g["wq_p"].reshape(Q_RANK, HEADS, HEAD_PAD)[:, :, :QK_DIM].reshape(Q_RANK, HEADS * QK_DIM)
    out["w_uq"] = _shard_cols(dwuq)
    dwukv = jnp.concatenate([g["wk_p"].reshape(KV_RANK, HEADS, HEAD_PAD)[:, :, :QK_NOPE],
                             g["wv"].reshape(KV_RANK, HEADS, V_DIM)], axis=-1).reshape(KV_RANK, HEADS * (QK_NOPE + V_DIM))
    out["w_ukv"] = _shard_cols(dwukv)
    out["w_out"] = g["w_out"].reshape(N_CHIPS, D_MODEL // N_CHIPS, D_MODEL)
    out["q_norm"] = g["gq_p"][0, :QK_DIM]
    out["k_norm"] = g["gk_p"][0, :QK_DIM]
    tril = jnp.tril(jnp.ones((CHUNK, CHUNK), dtype=bool))
    out["gm_ws"] = jnp.where(tril[None], g["wc"], 0.0)
    out["gm_bs"] = jnp.sum(g["bb"], axis=-1)
    for k in ("mix_norm", "q_a_norm", "kv_a_norm", "gm_v_norm", "attn_out_norm", "gm_out_norm"):
        out[k] = g[k][0]
    return out


def layer_fwd(l, h, p_l, Gl, small, tabs, after_first_ffn=None, before_ple=None):
    h1, r_a = ffn_fwd(f"l{l}a", h, small["ffn_a_norm"][l], Gl["ffn_a_w1"], Gl["ffn_a_w3"], Gl["ffn_a_w2"], ())
    if after_first_ffn is not None:
        Gl, small = after_first_ffn(h1, Gl, small)
    w = layer_weights(l, Gl, small)
    h2, r_m = mixer_fwd(f"l{l}", h1, w, tabs, Gl["w_out"], ())
    h3, r_b = ffn_fwd(f"l{l}b", h2, small["ffn_b_norm"][l], Gl["ffn_b_w1"], Gl["ffn_b_w3"], Gl["ffn_b_w2"], ())
    if before_ple is not None:
        w = {**w, "ple_norm": w["ple_norm"] + before_ple(h3)[0, 0]}
    h4, r_p = ple_fwd(f"l{l}", h3, p_l, w, Gl["w_ple_gate"], Gl["w_ple"], ())
    return h4, (w, h, h1, h2, h3, r_a, r_m, r_b, r_p)


def layer_bwd(l, dh, p_l, Gl, small, tabs, saved, before_mixer=None, before_ffn_a=None):
    w, h0, h1, h2, h3, r_a, r_m, r_b, r_p = saved
    slabs = lambda d: d.reshape(N_CHIPS, FF_PAD, D_MODEL)
    gl = {}
    dh, dh_bf, g_p = ple_bwd(f"l{l}", dh, h3, p_l, w, Gl["w_ple_gate"], Gl["w_ple"], (), r_p)
    gl["w_ple_gate"] = g_p["w_ple_gate"].reshape(N_CHIPS, D_MODEL // N_CHIPS, D_MODEL)
    gl["w_ple"] = g_p["w_ple"]
    gl["ple_gate_norm"], gl["ple_norm"] = g_p["ple_gate_norm"][0], g_p["ple_norm"][0]
    dh, dh_bf, dg, dw1, dw3, dw2 = ffn_bwd(f"l{l}b", dh, dh_bf, h2, small["ffn_b_norm"][l], r_b,
                                           Gl["ffn_b_w1"], Gl["ffn_b_w3"], Gl["ffn_b_w2"], ())
    gl["ffn_b_norm"] = dg[0]
    gl["ffn_b_w1"], gl["ffn_b_w3"], gl["ffn_b_w2"] = slabs(dw1), slabs(dw3), slabs(dw2)
    mixer_after = None if before_mixer is None else before_mixer(gl)
    dh, dh_bf, g_m = mixer_bwd(f"l{l}", dh, dh_bf, h1, w, tabs, Gl["w_out"], (), r_m, mixer_after)
    gl.update(mixer_grads_to_shards(g_m))
    dw_after = None if before_ffn_a is None else before_ffn_a(gl)
    dh, _, dg, dw1, dw3, dw2 = ffn_bwd(f"l{l}a", dh, dh_bf, h0, small["ffn_a_norm"][l], r_a,
                                       Gl["ffn_a_w1"], Gl["ffn_a_w3"], Gl["ffn_a_w2"], (), dw_after=dw_after)
    gl["ffn_a_norm"] = dg[0]
    gl["ffn_a_w1"], gl["ffn_a_w3"], gl["ffn_a_w2"] = slabs(dw1), slabs(dw3), slabs(dw2)
    return dh, gl


MESH = pl.DeviceIdType.MESH
HBM_SPEC = pl.BlockSpec(memory_space=pltpu.HBM)


def _place():
    x, y, c = lax.axis_index("x"), lax.axis_index("y"), lax.axis_index("c")
    others = [(1 - x, y), (x, 1 - y), (1 - x, 1 - y)]
    return x, y, c, 2 * x + y, others


def prep_shard(name, w, layer, rows_pad, place, after=None):
    _, ks, n = w.shape
    ksp = ks + rows_pad
    tc = 512 if n % 512 == 0 else n
    deps = [] if after is None else [after]

    def body(place_ref, x_ref, *rest):
        o_ref = rest[-1]
        o_ref[:ks] = x_ref[...].astype(BF16)
        if rows_pad:
            o_ref[ks:] = jnp.zeros((rows_pad, tc), BF16)

    return pl.pallas_call(
        body, name=name,
        grid_spec=pltpu.PrefetchScalarGridSpec(
            num_scalar_prefetch=1, grid=(n // tc,),
            in_specs=[pl.BlockSpec((None, ks, tc), lambda i, s: (layer, 0, i))] + [ANY_SPEC] * len(deps),
            out_specs=pl.BlockSpec((None, ksp, tc), lambda i, s: (s[0], 0, i))),
        out_shape=jax.ShapeDtypeStruct((N_CHIPS, ksp, n), BF16),
        compiler_params=_params(("parallel",)),
    )(place, w, *deps)


def exchange_halves(name, grads):
    n = len(grads)

    def body(*refs):
        d_refs, r_refs = refs[:n], refs[n:2 * n]
        send, recv = refs[2 * n:]
        x, y, c, _, _ = _place()
        cps = []
        for w in range(n):
            half = grads[w].shape[1] // 2
            cps.append(pltpu.make_async_remote_copy(
                src_ref=d_refs[w].at[pl.ds(0, N_CHIPS), pl.ds((1 - c) * half, half)], dst_ref=r_refs[w],
                send_sem=send.at[w], recv_sem=recv.at[w], device_id=(x, y, 1 - c), device_id_type=MESH))
        for cp in cps:
            cp.start()
        for cp in cps:
            cp.wait()

    return pl.pallas_call(
        body, name=name, in_specs=[HBM_SPEC] * n, out_specs=[HBM_SPEC] * n,
        out_shape=[jax.ShapeDtypeStruct((N_CHIPS, g.shape[1] // 2, g.shape[2]), g.dtype) for g in grads],
        scratch_shapes=[pltpu.SemaphoreType.DMA((n,))] * 2,
    )(*grads)


def share_halves(name, fulls):
    n = len(fulls)

    def body(*refs):
        o_refs = refs[n:2 * n]
        send, recv = refs[2 * n:]
        x, y, c, _, _ = _place()
        cps = []
        for w in range(n):
            kh = fulls[w].shape[0] // 2
            half = o_refs[w].at[pl.ds(c * kh, kh)]
            cps.append(pltpu.make_async_remote_copy(src_ref=half, dst_ref=half, send_sem=send.at[w], recv_sem=recv.at[w],
                                                    device_id=(x, y, 1 - c), device_id_type=MESH))
        for cp in cps:
            cp.start()
        for cp in cps:
            cp.wait()

    return pl.pallas_call(
        body, name=name, in_specs=[HBM_SPEC] * n, out_specs=[HBM_SPEC] * n,
        out_shape=[jax.ShapeDtypeStruct(f.shape, f.dtype) for f in fulls],
        input_output_aliases={w: w for w in range(n)},
        scratch_shapes=[pltpu.SemaphoreType.DMA((n,))] * 2,
    )(*fulls)


SEM_SPEC = pl.BlockSpec(memory_space=pltpu.SEMAPHORE)
ANY_SPEC = pl.BlockSpec(memory_space=pl.ANY)
DATAFLOW = pltpu.SideEffectType.DATAFLOW_SIDE_EFFECTING


def _hbm(x):
    return pltpu.with_memory_space_constraint(x, pltpu.HBM)


def _start_call(name, slots, after, issue):
    n = len(slots)
    deps = [] if after is None else [after]
    nd = len(deps)

    def body(*refs):
        issue(refs[n + nd + 2:2 * n + nd + 2], refs[n + nd], refs[n + nd + 1])
        token = refs[2 * n + nd + 2]
        token[...] = jnp.zeros_like(token)

    outs = pl.pallas_call(
        body, name=name,
        in_specs=[HBM_SPEC] * n + [ANY_SPEC] * nd,
        out_specs=(SEM_SPEC, SEM_SPEC, *([HBM_SPEC] * n), pl.BlockSpec(memory_space=pltpu.VMEM)),
        out_shape=(pltpu.SemaphoreType.DMA((n,)), pltpu.SemaphoreType.DMA((n,)),
                   *[pltpu.HBM(s.shape, s.dtype) for s in slots], jax.ShapeDtypeStruct((8, 128), F32)),
        input_output_aliases={w: w + 2 for w in range(n)},
        compiler_params=pltpu.CompilerParams(has_side_effects=DATAFLOW),
    )(*[_hbm(s) for s in slots], *deps)
    return outs[0], outs[1], list(outs[2:2 + n]), outs[2 + n]


def gather_start(name, slots, after):
    def issue(g_refs, send, recv):
        x, y, c, jme, others = _place()
        for w in range(len(slots)):
            kh = slots[w].shape[1] // 2
            mine = g_refs[w].at[jme, pl.ds(c * kh, kh)]
            for (px, py) in others:
                pltpu.make_async_remote_copy(src_ref=mine, dst_ref=mine, send_sem=send.at[w], recv_sem=recv.at[w],
                                             device_id=(px, py, c), device_id_type=MESH).start()

    return _start_call(name, slots, after, issue)


def forward_start(name, slots):
    def issue(g_refs, send, recv):
        x, y, c, _, others = _place()
        for w in range(len(slots)):
            kh = slots[w].shape[1] // 2
            for (px, py) in others:
                blk = g_refs[w].at[2 * px + py, pl.ds(c * kh, kh)]
                pltpu.make_async_remote_copy(src_ref=blk, dst_ref=blk, send_sem=send.at[w], recv_sem=recv.at[w],
                                             device_id=(x, y, 1 - c), device_id_type=MESH).start()

    return _start_call(name, slots, None, issue)


def gather_wait(name, send, recv, flying, after):
    n = len(flying)

    def body(*refs):
        send_ref, recv_ref = refs[n], refs[n + 1]
        g_refs = refs[n + 3:]
        x, y, c, _, _ = _place()
        for w in range(n):
            three = g_refs[w].at[pl.ds(0, 3), pl.ds(0, flying[w].shape[1] // 2)]
            cp = pltpu.make_async_remote_copy(src_ref=three, dst_ref=three, send_sem=send_ref.at[w], recv_sem=recv_ref.at[w],
                                              device_id=(x, y, 1 - c), device_id_type=MESH)
            cp.wait_send()
            cp.wait_recv()

    return pl.pallas_call(
        body, name=name,
        in_specs=[HBM_SPEC] * n + [SEM_SPEC, SEM_SPEC, ANY_SPEC],
        out_specs=[HBM_SPEC] * n,
        out_shape=[pltpu.HBM(s.shape, s.dtype) for s in flying],
        input_output_aliases={w: w for w in range(n)},
        compiler_params=pltpu.CompilerParams(has_side_effects=DATAFLOW),
    )(*flying, send, recv, after)


def scatter_start(name, parts):
    n = len(parts)

    def body(*refs):
        p_refs, q_refs = refs[2 * n + 2:3 * n + 2], refs[3 * n + 2:4 * n + 2]
        send, recv, token = refs[2 * n], refs[2 * n + 1], refs[4 * n + 2]
        x, y, c, jme, others = _place()
        for w in range(n):
            for (px, py) in others:
                pltpu.make_async_remote_copy(
                    src_ref=p_refs[w].at[2 * px + py], dst_ref=q_refs[w].at[jme], send_sem=send.at[w], recv_sem=recv.at[w],
                    device_id=(px, py, c), device_id_type=MESH).start()
        token[...] = jnp.zeros_like(token)

    lands = [_hbm(lax.empty(p.shape, p.dtype)) for p in parts]
    outs = pl.pallas_call(
        body, name=name,
        in_specs=[HBM_SPEC] * (2 * n),
        out_specs=(SEM_SPEC, SEM_SPEC, *([HBM_SPEC] * (2 * n)), pl.BlockSpec(memory_space=pltpu.VMEM)),
        out_shape=(pltpu.SemaphoreType.DMA((n,)), pltpu.SemaphoreType.DMA((n,)),
                   *[pltpu.HBM(p.shape, p.dtype) for p in parts], *[pltpu.HBM(p.shape, p.dtype) for p in parts],
                   jax.ShapeDtypeStruct((8, 128), F32)),
        input_output_aliases={w: w + 2 for w in range(2 * n)},
        compiler_params=pltpu.CompilerParams(has_side_effects=DATAFLOW),
    )(*[_hbm(p) for p in parts], *lands)
    return outs[0], outs[1], list(outs[2:2 + n]), list(outs[2 + n:2 + 2 * n]), outs[2 + 2 * n]


def scatter_wait(name, send, recv, parts, lands, after):
    n = len(parts)

    def body(*refs):
        send_ref, recv_ref = refs[2 * n], refs[2 * n + 1]
        q_refs = refs[3 * n + 3:]
        x, y, c, _, _ = _place()
        for w in range(n):
            three = q_refs[w].at[pl.ds(0, 3)]
            cp = pltpu.make_async_remote_copy(src_ref=three, dst_ref=three, send_sem=send_ref.at[w], recv_sem=recv_ref.at[w],
                                              device_id=(x, y, 1 - c), device_id_type=MESH)
            cp.wait_send()
            cp.wait_recv()

    outs = pl.pallas_call(
        body, name=name,
        in_specs=[HBM_SPEC] * (2 * n) + [SEM_SPEC, SEM_SPEC, ANY_SPEC],
        out_specs=[HBM_SPEC] * (2 * n),
        out_shape=[pltpu.HBM(p.shape, p.dtype) for p in parts] * 2,
        input_output_aliases={w: w for w in range(2 * n)},
        compiler_params=pltpu.CompilerParams(has_side_effects=DATAFLOW),
    )(*parts, *lands, send, recv, after)
    return list(outs[:n]), list(outs[n:])


def allreduce_small(v):
    R = v.shape[0]

    def body(v_ref, o_ref, sib_ref, mine_ref, all_ref, d_send, d_recv, i_send, i_recv):
        x, y, c, jme, others = _place()
        swap = pltpu.make_async_remote_copy(src_ref=v_ref, dst_ref=sib_ref, send_sem=d_send, recv_sem=d_recv,
                                            device_id=(x, y, 1 - c), device_id_type=MESH)
        swap.start()
        swap.wait()
        mine_ref[...] = v_ref[...] + sib_ref[...]
        for (px, py) in others:
            pltpu.make_async_remote_copy(src_ref=mine_ref, dst_ref=all_ref.at[jme], send_sem=i_send, recv_sem=i_recv,
                                         device_id=(px, py, c), device_id_type=MESH).start()
        three = all_ref.at[pl.ds(0, 3)]
        wait3 = pltpu.make_async_remote_copy(src_ref=three, dst_ref=three, send_sem=i_send, recv_sem=i_recv,
                                             device_id=(x, y, c), device_id_type=MESH)
        wait3.wait_recv()
        wait3.wait_send()
        all_ref[jme] = mine_ref[...]
        o_ref[...] = ((all_ref[0] + all_ref[1]) + all_ref[2]) + all_ref[3]

    vm = pl.BlockSpec(memory_space=pltpu.VMEM)
    return pl.pallas_call(
        body, name="allreduce_small", in_specs=[vm], out_specs=vm,
        out_shape=jax.ShapeDtypeStruct(v.shape, F32),
        scratch_shapes=[pltpu.VMEM((R, 128), F32), pltpu.VMEM((R, 128), F32), pltpu.VMEM((N_CHIPS, R, 128), F32),
                        pltpu.SemaphoreType.DMA, pltpu.SemaphoreType.DMA, pltpu.SemaphoreType.DMA, pltpu.SemaphoreType.DMA],
        compiler_params=pltpu.CompilerParams(vmem_limit_bytes=VMEM_LIMIT_BYTES),
    )(v)


def _row_tile(rows, width, mult=16, cap=3 << 20):
    best = rows
    for t in range(mult, rows + 1, mult):
        if rows % t == 0 and t * width * 4 <= cap:
            best = t
    return best


def add_sibling(name, mine, theirs, place):
    _, kh, ns = theirs.shape
    tr = _row_tile(kh, ns)
    nblk = kh // tr

    def body(place_ref, a_ref, b_ref, o_ref):
        o_ref[...] = (a_ref[...].astype(F32) + b_ref[...].astype(F32)).astype(BF16)

    return pl.pallas_call(
        body, name=name,
        grid_spec=pltpu.PrefetchScalarGridSpec(
            num_scalar_prefetch=1, grid=(N_CHIPS, nblk),
            in_specs=[pl.BlockSpec((None, tr, ns), lambda j, i, s: (j, s[1] * nblk + i, 0)),
                      pl.BlockSpec((None, tr, ns), lambda j, i, s: (j, i, 0))],
            out_specs=pl.BlockSpec((None, tr, ns), lambda j, i, s: (j, i, 0))),
        out_shape=jax.ShapeDtypeStruct(theirs.shape, BF16),
        compiler_params=_params(("parallel", "parallel")),
    )(place, mine, theirs)


def add_chips(name, q, p, place):
    _, kh, ns = q.shape
    tr = _row_tile(kh, ns)
    nblk = kh // tr

    def body(place_ref, *refs):
        q_refs, own_ref, o_ref = refs[:N_CHIPS], refs[N_CHIPS], refs[-1]
        jme = place_ref[0]
        tot = None
        for j in range(N_CHIPS):
            v = jnp.where(jme == j, own_ref[...], q_refs[j][...]).astype(F32)
            tot = v if tot is None else tot + v
        o_ref[...] = tot

    def q_ix(j):
        return lambda i, s: (jnp.where(s[0] == j, (j + 1) % N_CHIPS, j), i, 0)

    in_specs = [pl.BlockSpec((None, tr, ns), q_ix(j)) for j in range(N_CHIPS)]
    in_specs.append(pl.BlockSpec((None, tr, ns), lambda i, s: (s[0], i, 0)))
    return pl.pallas_call(
        body, name=name,
        grid_spec=pltpu.PrefetchScalarGridSpec(
            num_scalar_prefetch=1, grid=(nblk,), in_specs=in_specs,
            out_specs=pl.BlockSpec((tr, ns), lambda i, s: (s[1] * nblk + i, 0))),
        out_shape=jax.ShapeDtypeStruct((2 * kh, ns), F32),
        compiler_params=_params(("parallel",)),
    )(place, q, q, q, q, p)


ADAM_LR, ADAM_B1, ADAM_B2, ADAM_EPS, ADAM_WD, ADAM_STEP = 0.001, 0.9, 0.999, 1e-08, 0.01, 10


def adamw(name, w, g, m, v, layer, prev=None, after=None):
    _, k, ns = w.shape
    nsp = g.shape[1]
    tr = _row_tile(k, nsp, mult=8, cap=3 << 20)

    def body(w_ref, g_ref, m_ref, v_ref, *rest):
        go_ref, d_ref, mo_ref, vo_ref = rest[-4:]
        gv = g_ref[:, :ns] if nsp != ns else g_ref[...]
        mn = ADAM_B1 * m_ref[...] + (1.0 - ADAM_B1) * gv
        vn = ADAM_B2 * v_ref[...] + (1.0 - ADAM_B2) * (gv * gv)
        m_hat = mn / (1.0 - ADAM_B1 ** ADAM_STEP)
        v_hat = vn / (1.0 - ADAM_B2 ** ADAM_STEP)
        go_ref[...] = gv
        d_ref[...] = -ADAM_LR * (m_hat / (jnp.sqrt(v_hat) + ADAM_EPS) + ADAM_WD * w_ref[...])
        mo_ref[...] = mn
        vo_ref[...] = vn

    blk = pl.BlockSpec((None, tr, ns), lambda i: (layer, i, 0))
    gblk = pl.BlockSpec((tr, nsp), lambda i: (i, 0))
    args, in_specs, aliases = [w, g, m, v], [blk, gblk, blk, blk], {}
    if prev is not None:
        args += list(prev)
        in_specs += [pl.BlockSpec(memory_space=pl.ANY)] * 4
        aliases = {4 + i: i for i in range(4)}
    if after is not None:
        args.append(after)
        in_specs.append(pl.BlockSpec(memory_space=pl.ANY))
    return pl.pallas_call(
        body, name=name, grid=(k // tr,), in_specs=in_specs, out_specs=[blk] * 4,
        out_shape=[jax.ShapeDtypeStruct(w.shape, F32)] * 4, input_output_aliases=aliases,
        compiler_params=_params(("parallel",)),
    )(*args)


WEIGHTS = ("ffn_a_norm", "ffn_a_w1", "ffn_a_w3", "ffn_a_w2", "mix_norm", "w_in", "q_a_norm", "w_uq", "kv_a_norm", "w_ukv",
           "q_norm", "k_norm", "gm_v_norm", "gm_ws", "gm_bs", "attn_out_norm", "gm_out_norm", "w_out", "ffn_b_norm",
           "ffn_b_w1", "ffn_b_w3", "ffn_b_w2", "ple_gate_norm", "w_ple_gate", "w_ple", "ple_norm")
_FF = FF_PAD - FF_SHARD
BIG = {"ffn_a_w1": _FF, "ffn_a_w3": _FF, "ffn_a_w2": _FF, "ffn_b_w1": _FF, "ffn_b_w3": _FF, "ffn_b_w2": _FF,
       "w_in": IN_SHARD_PAD - IN_SHARD, "w_uq": 0, "w_ukv": 0, "w_ple": 0, "w_out": 0, "w_ple_gate": 0}
TRANSPOSED = ("ffn_a_w1", "ffn_a_w3", "ffn_b_w1", "ffn_b_w3", "w_in")
SMALL = tuple(n for n in WEIGHTS if n not in BIG)
PACK = 1024


def _pack_small(d):
    parts = []
    for n in SMALL:
        flat = d[n].reshape(-1)
        parts.append(jnp.pad(flat, (0, (-flat.shape[0]) % PACK)))
    return jnp.concatenate(parts).reshape(-1, 128)


def _unpack_small(buf, like):
    flat = buf.reshape(-1)
    out, pos = {}, 0
    for n in SMALL:
        size = math.prod(like[n].shape)
        out[n] = flat[pos:pos + size].reshape(like[n].shape)
        pos += size + (-size) % PACK
    return out


def kernel(*args):
    names = (("x", "p", "positions") + WEIGHTS + ("loss_target",) + tuple("m_" + n for n in WEIGHTS)
             + tuple("v_" + n for n in WEIGHTS))
    a = dict(zip(names, args, strict=True))
    x, p, positions, target = a["x"][0], a["p"][:, 0], a["positions"][0], a["loss_target"][0]
    for n in TRANSPOSED:
        for pre in ("", "m_", "v_"):
            a[pre + n] = jnp.swapaxes(a[pre + n], 1, 2)

    place = jnp.stack([2 * lax.axis_index("x") + lax.axis_index("y"), lax.axis_index("c")]).astype(jnp.int32)
    small = {n: a[n] for n in SMALL}
    tabs = rope_tables(positions)
    first = ("ffn_a_w1", "ffn_a_w3", "ffn_a_w2")
    rest = tuple(n for n in BIG if n not in first)
    prep = lambda n, l, after: prep_shard(f"prep_{n}_{l}", a[n], l, BIG[n], place, after)

    def finish_gather(tag, started, after):
        send, recv, flying, _ = started
        arrived = gather_wait(f"gather_{tag}_wait", send, recv, flying, after)
        send, recv, flying, token = forward_start(f"forward_{tag}_start", arrived)
        return gather_wait(f"forward_{tag}_wait", send, recv, flying, token)

    ga = gather_start("gather_l0a_start", [prep(n, 0, None) for n in first], None)
    gb = gather_start("gather_l0b_start", [prep(n, 0, ga[3]) for n in rest], None)
    slots1 = []
    for n in BIG:
        slots1.append(prep(n, 1, slots1[-1] if slots1 else gb[3]))
    G0 = dict(zip(first, finish_gather("l0a", ga, slots1[-1])))
    later = {}

    def after_first_ffn(h1, Gl, small_):
        later["G0"] = {**Gl, **dict(zip(rest, finish_gather("l0b", gb, h1)))}
        later["g1"] = gather_start("gather_l1_start", slots1, later["G0"]["w_uq"])
        return later["G0"], {**small_, "mix_norm": small_["mix_norm"] + later["g1"][3][0, 0]}

    def before_ple(h3):
        send, recv, flying, _ = later["g1"]
        later["f1"] = forward_start("forward_l1_start", gather_wait("gather_l1_wait", send, recv, flying, h3))
        return later["f1"][3]

    h, saved0 = layer_fwd(0, x, p[0], G0, small, tabs, after_first_ffn, before_ple)
    G0 = later["G0"]
    G1 = dict(zip(BIG, gather_wait("forward_l1_wait", *later["f1"][:3], h)))
    h, saved1 = layer_fwd(1, h, p[1], G1, small, tabs)
    dh, loss = loss_grad(h, target)
    loss = lax.psum(loss, ("x", "y", "c"))

    def start_reduce(tag, names, gl):
        mine = [gl[n] for n in names]
        theirs = exchange_halves(f"exchange_{tag}", mine)
        parts = [add_sibling(f"add_sibling_{n}_{tag}", d, r, place) for n, d, r in zip(names, mine, theirs)]
        return scatter_start(f"scatter_{tag}_start", parts)

    def finish_reduce(tag, names, started, after):
        send, recv, parts, lands, _ = started
        parts, slabs = scatter_wait(f"scatter_{tag}_wait", send, recv, parts, lands, after)
        halves = [add_chips(f"add_chips_{n}_{tag}", q, pt, place) for n, q, pt in zip(names, slabs, parts)]
        return dict(zip(names, share_halves(f"share_{tag}", halves)))

    def update(names, full, layer, prev):
        outs, last = {}, None
        for n in names:
            outs[n] = adamw(f"adamw_{n}_{layer}", a[n], full[n], a["m_" + n], a["v_" + n], layer, prev and prev[n], last)
            last = outs[n][1]
        return outs

    groups = {"l0a": ("w_ple_gate", "w_ple", "ffn_b_w1", "ffn_b_w3", "ffn_b_w2"),
              "l0b": ("w_in", "w_uq", "w_ukv", "w_out"),
              "l0c": ("ffn_a_w1", "ffn_a_w3", "ffn_a_w2")}
    grads = [None, None]
    dh, grads[1] = layer_bwd(1, dh, p[1], G1, small, tabs, saved1)
    red1 = start_reduce("l1", tuple(BIG), grads[1])
    w0 = {**saved0[0], "ple_gate_norm": saved0[0]["ple_gate_norm"] + red1[4][0, 0]}
    started = {}

    def start_group(tag):
        def hook(gl):
            started[tag] = start_reduce(tag, groups[tag], gl)
            return started[tag][4]
        return hook

    gx, grads[0] = layer_bwd(0, dh, p[0], G0, small, tabs, (w0,) + saved0[1:], start_group("l0a"), start_group("l0b"))
    start_group("l0c")(grads[0])
    outs1 = update(BIG, finish_reduce("l1", tuple(BIG), red1, started["l0c"][4]), 1, None)
    behind = outs1[tuple(BIG)[-1]][1]
    full0 = {}
    for tag in groups:
        full0.update(finish_reduce(tag, groups[tag], started[tag], behind))
    outs0 = update(BIG, full0, 0, outs1)

    out_g, out_d, out_m, out_v = {}, {}, {}, {}
    for n in BIG:
        outs = [jnp.swapaxes(o, 1, 2) for o in outs0[n]] if n in TRANSPOSED else outs0[n]
        out_g[n], out_d[n], out_m[n], out_v[n] = outs

    gs = allreduce_small(_pack_small({n: jnp.stack([grads[0][n], grads[1][n]]) for n in SMALL}))
    rows = gs.shape[0] // 2
    packed = [_pack_small(d).reshape(2, rows, 128) for d in
              (small, {n: a["m_" + n] for n in SMALL}, {n: a["v_" + n] for n in SMALL})]
    gs = gs.reshape(2, rows, 128)
    sm = adamw("adamw_small_0", packed[0], gs[0], packed[1], packed[2], 0)
    sm = adamw("adamw_small_1", packed[0], gs[1], packed[1], packed[2], 1, sm)
    for dst, buf in zip((out_g, out_d, out_m, out_v), sm):
        dst.update(_unpack_small(buf, small))

    return (loss, gx[None], *[out_g[n] for n in WEIGHTS], *[out_d[n] for n in WEIGHTS],
            *[out_m[n] for n in WEIGHTS], *[out_v[n] for n in WEIGHTS])
```

```python
import math

import jax
import jax.numpy as jnp
from jax import lax
from jax.experimental import pallas as pl
from jax.experimental.pallas import tpu as pltpu

F32 = jnp.float32
BF16 = jnp.bfloat16

D_MODEL = 2048
D_FF = 5504
N_CHIPS = 4
FF_SHARD = D_FF // N_CHIPS
FF_PAD = 1408
FF_P = N_CHIPS * FF_PAD
HEADS = 8
QK_NOPE = 128
QK_ROPE = 64
QK_DIM = 192
HEAD_PAD = 256
V_DIM = 128
Q_RANK = 512
KV_RANK = 256
ATTN_W = 1024
GM_W = 1024
GROUPS = 8
CHUNK = 128
PLE_DIM = 256
IN_P = 3072
IN_SHARD = 720
IN_SHARD_PAD = 736
EPS = 1e-6
ROPE_BASE = 10000.0
ATTN_SCALE = QK_DIM ** -0.5
SUB_BLOCKS = 1
VMEM_LIMIT_BYTES = 56 * 1024 * 1024


def _params(sem):
    return pltpu.CompilerParams(dimension_semantics=sem, vmem_limit_bytes=VMEM_LIMIT_BYTES)


def _bf(x):
    return x if x.dtype == BF16 else x.astype(BF16)


def _sigmoid(x):
    return 1.0 / (1.0 + jnp.exp(-x))


_GELU_C = math.sqrt(2.0 / math.pi)


def _gelu(x):
    t = jnp.tanh(_GELU_C * (x + 0.044715 * x * x * x))
    return 0.5 * x * (1.0 + t)


def _gelu_grad(x):
    t = jnp.tanh(_GELU_C * (x + 0.044715 * x * x * x))
    return 0.5 * (1.0 + t) + 0.5 * x * (1.0 - t * t) * _GELU_C * (1.0 + 3 * 0.044715 * x * x)


def op_a(a, tm, tk):
    return (a, (tm, tk), lambda i, j, k: (i, k), 1)


def op_at(a, tm, tk):
    return (a, (tk, tm), lambda i, j, k: (k, i), 0)


def op_b(b, tk, tn):
    return (b, (tk, tn), lambda i, j, k: (k, j), 0)


def op_bt(b, tk, tn):
    return (b, (tn, tk), lambda i, j, k: (j, k), 1)


def op_b_cols(g, pre, tk, tn):
    nb = g.shape[-1] // tn
    none = (None,) * (1 + len(pre))
    return (g, none + (tk, tn), lambda i, j, k: (j // nb,) + tuple(pre) + (k, j % nb), 0)


def op_b_rows(g, pre, tk, tn, koff=0):
    nb = g.shape[-2] // tk
    none = (None,) * (1 + len(pre))
    return (g, none + (tk, tn), lambda i, j, k: ((k + koff) // nb,) + tuple(pre) + ((k + koff) % nb, j), 0)


def op_b_rows_t(g, pre, tk, tn):
    nb = g.shape[-2] // tn
    none = (None,) * (1 + len(pre))
    return (g, none + (tn, tk), lambda i, j, k: (j // nb,) + tuple(pre) + (j % nb, k), 1)


def tile_mn(x, tm, tn):
    return (x, (tm, tn), lambda i, j: (i, j))


def out_mn(M, N, tm, tn, dtype):
    return (jax.ShapeDtypeStruct((M, N), dtype), (tm, tn), lambda i, j: (i, j))


def out_cols(M, ns, tm, tn, dtype):
    nb = ns // tn
    return (jax.ShapeDtypeStruct((N_CHIPS, M, ns), dtype), (None, tm, tn), lambda i, j: (j // nb, i, j % nb))


def matmul(name, grid_mnk, a_ops, b_ops, terms, n_acc, extras, outs, epilogue, acc_tile, n_outer=False, after=None):
    gm, gn, gk = grid_mnk
    na, nb, nx, no = len(a_ops), len(b_ops), len(extras), len(outs)
    nd = 0 if after is None else 1

    def body(*refs):
        a_refs, b_refs = refs[:na], refs[na:na + nb]
        x_refs = refs[na + nb:na + nb + nx]
        o_refs = refs[na + nb + nx + nd:na + nb + nx + nd + no]
        acc_refs = refs[na + nb + nx + nd + no:]
        k = pl.program_id(2)

        @pl.when(k == 0)
        def _():
            for acc in acc_refs:
                acc[...] = jnp.zeros_like(acc)

        for ai, bi, ci in terms:
            dims = (((a_ops[ai][3],), (b_ops[bi][3],)), ((), ()))
            acc_refs[ci][...] += lax.dot_general(_bf(a_refs[ai][...]), _bf(b_refs[bi][...]), dims,
                                                 preferred_element_type=F32)

        @pl.when(k == gk - 1)
        def _():
            res = epilogue([acc[...] for acc in acc_refs], [x[...] for x in x_refs])
            for o, v in zip(o_refs, res):
                o[...] = v.astype(o.dtype)

    if n_outer:
        grid = (gn, gm, gk)

        def ix3(f):
            return lambda j, i, k: f(i, j, k)

        def ix2(f):
            return lambda j, i, k: f(i, j)
    else:
        grid = (gm, gn, gk)

        def ix3(f):
            return lambda i, j, k: f(i, j, k)

        def ix2(f):
            return lambda i, j, k: f(i, j)

    in_specs = [pl.BlockSpec(blk, ix3(f)) for (_, blk, f, _) in list(a_ops) + list(b_ops)]
    in_specs += [pl.BlockSpec(blk, ix2(f)) for (_, blk, f) in extras]
    in_specs += [pl.BlockSpec(memory_space=pl.ANY)] * nd
    out_specs = [pl.BlockSpec(blk, ix2(f)) for (_, blk, f) in outs]
    return pl.pallas_call(
        body,
        name=name,
        grid=grid,
        in_specs=in_specs,
        out_specs=out_specs,
        out_shape=[s for (s, _, _) in outs],
        scratch_shapes=[pltpu.VMEM(acc_tile, F32) for _ in range(n_acc)],
        compiler_params=_params(("parallel", "parallel", "arbitrary")),
    )(*[o[0] for o in a_ops], *[o[0] for o in b_ops], *[x[0] for x in extras], *([after] * nd))


def _acc0(accs, xs):
    return (accs[0],)


def mm_simple(name, a, b_op_fn, M, N, K, tm, tn, tk, out_dtype=F32, a_t=False, extras=(), epilogue=_acc0, outs=None,
              after=None):
    a_op = op_at(a, tm, tk) if a_t else op_a(a, tm, tk)
    outs = outs or [out_mn(M, N, tm, tn, out_dtype)]
    return matmul(name, (M // tm, N // tn, K // tk), [a_op], [b_op_fn(tk, tn)], [(0, 0, 0)], 1,
                  list(extras), outs, epilogue, (tm, tn), after=after)


def rms_fwd(name, x, g, width, col_blk=0, tm=256, out_dtype=BF16):
    T = x.shape[0]

    def body(x_ref, g_ref, o_ref):
        xv = x_ref[...].astype(F32)
        r = lax.rsqrt(jnp.mean(xv * xv, axis=-1, keepdims=True) + EPS)
        o_ref[...] = (xv * r * g_ref[...]).astype(o_ref.dtype)

    return pl.pallas_call(
        body, name=name, grid=(T // tm,),
        in_specs=[pl.BlockSpec((tm, width), lambda i: (i, col_blk)), pl.BlockSpec((1, width), lambda i: (0, 0))],
        out_specs=pl.BlockSpec((tm, width), lambda i: (i, 0)),
        out_shape=jax.ShapeDtypeStruct((T, width), out_dtype),
        compiler_params=_params(("parallel",)),
    )(x, g.reshape(1, width))


def rms_bwd(name, x, g, dn, width, col_blk=0, dres=None, tm=256, with_delta=False, bf16_copy=False):
    T = x.shape[0]
    has_res = dres is not None

    def body(*refs):
        x_ref, g_ref, dn_ref = refs[:3]
        pos = 3
        res_ref = None
        if has_res:
            res_ref = refs[pos]
            pos += 1
        dx_ref, dg_ref = refs[pos], refs[pos + 1]
        delta_ref = refs[pos + 2] if with_delta else None
        lo_ref = refs[-1] if bf16_copy else None
        i = pl.program_id(0)
        xv = x_ref[...].astype(F32)
        r = lax.rsqrt(jnp.mean(xv * xv, axis=-1, keepdims=True) + EPS)
        xh = xv * r
        d = dn_ref[...].astype(F32)
        gd = d * g_ref[...]
        dx = r * (gd - xh * jnp.mean(gd * xh, axis=-1, keepdims=True))
        if has_res:
            dx = dx + res_ref[...]
        dx_ref[...] = dx.astype(dx_ref.dtype)
        if bf16_copy:
            lo_ref[...] = dx.astype(BF16)
        part = jnp.sum(d * xh, axis=0, keepdims=True)

        @pl.when(i == 0)
        def _():
            dg_ref[...] = part

        @pl.when(i > 0)
        def _():
            dg_ref[...] += part

        if with_delta:
            for h in range(width // 128):
                sl = slice(h * 128, (h + 1) * 128)
                s = jnp.sum(dx[:, sl] * xv[:, sl], axis=-1, keepdims=True)
                delta_ref[:, sl] = jnp.broadcast_to(s, (tm, 128))

    in_specs = [pl.BlockSpec((tm, width), lambda i: (i, col_blk)), pl.BlockSpec((1, width), lambda i: (0, 0)),
                pl.BlockSpec((tm, width), lambda i: (i, 0))]
    args = [x, g.reshape(1, width), dn]
    if has_res:
        in_specs.append(pl.BlockSpec((tm, width), lambda i: (i, 0)))
        args.append(dres)
    out_specs = [pl.BlockSpec((tm, width), lambda i: (i, 0)), pl.BlockSpec((1, width), lambda i: (0, 0))]
    out_shape = [jax.ShapeDtypeStruct((T, width), F32), jax.ShapeDtypeStruct((1, width), F32)]
    if with_delta:
        out_specs.append(pl.BlockSpec((tm, width), lambda i: (i, 0)))
        out_shape.append(jax.ShapeDtypeStruct((T, width), F32))
    if bf16_copy:
        out_specs.append(pl.BlockSpec((tm, width), lambda i: (i, 0)))
        out_shape.append(jax.ShapeDtypeStruct((T, width), BF16))
    return pl.pallas_call(
        body, name=name, grid=(T // tm,), in_specs=in_specs, out_specs=out_specs, out_shape=out_shape,
        compiler_params=_params(("arbitrary",)),
    )(*args)


def ffn_fwd(tag, h, g, w1g, w3g, w2g, pre):
    T = h.shape[0]
    n = rms_fwd(f"{tag}_rms", h, g, D_MODEL)
    tm, tn = 512, FF_PAD

    def up_epi(accs, xs):
        a1, a3 = accs
        return a1, a3, a1 * _sigmoid(a1) * a3

    a1, a3, s = matmul(
        f"{tag}_up", (T // tm, FF_P // tn, 1),
        [op_a(n, tm, D_MODEL)], [op_b_rows_t(w1g, pre, D_MODEL, tn), op_b_rows_t(w3g, pre, D_MODEL, tn)],
        [(0, 0, 0), (0, 1, 1)], 2, [],
        [out_mn(T, FF_P, tm, tn, BF16)] * 3, up_epi, (tm, tn), n_outer=True)

    tm2, tn2 = 1024, 1024
    (h_out,) = matmul(
        f"{tag}_down", (T // tm2, D_MODEL // tn2, N_CHIPS),
        [op_a(s, tm2, FF_PAD)], [op_b_rows(w2g, pre, FF_PAD, tn2)],
        [(0, 0, 0)], 1, [tile_mn(h, tm2, tn2)],
        [out_mn(T, D_MODEL, tm2, tn2, F32)], lambda accs, xs: (xs[0] + 0.5 * accs[0],), (tm2, tn2))
    return h_out, (n, a1, a3, s)


def ffn_bwd(tag, dh_out, dh_bf, h, g, res, w1g, w3g, w2g, pre, after=None):
    n, a1, a3, s = res
    T = h.shape[0]
    tm, tn = 512, FF_PAD

    def act_epi(accs, xs):
        ds = 0.5 * accs[0]
        x1, x3 = xs[0].astype(F32), xs[1].astype(F32)
        sg = _sigmoid(x1)
        silu = x1 * sg
        return ds * x3 * (sg + silu * (1.0 - sg)), ds * silu

    da1, da3 = matmul(
        f"{tag}_dact", (T // tm, FF_P // tn, 1),
        [op_a(dh_bf, tm, D_MODEL)], [op_b_rows_t(w2g, pre, D_MODEL, tn)],
        [(0, 0, 0)], 1, [tile_mn(a1, tm, tn), tile_mn(a3, tm, tn)],
        [out_mn(T, FF_P, tm, tn, BF16)] * 2, act_epi, (tm, tn), n_outer=True, after=after)

    tk = 1024

    def dw_t(nm, left, right, scale):
        (dw,) = matmul(
            f"{tag}_{nm}", (FF_P // FF_PAD, D_MODEL // 1024, T // tk),
            [op_at(left, FF_PAD, tk)], [op_b(right, tk, 1024)],
            [(0, 0, 0)], 1, [], [out_mn(FF_P, D_MODEL, FF_PAD, 1024, BF16)],
            lambda accs, xs: (scale * accs[0],), (FF_PAD, 1024), after=after)
        return dw

    dw2 = dw_t("dw2", s, dh_bf, 0.5)
    dw1 = dw_t("dw1", da1, n, 1.0)
    dw3 = dw_t("dw3", da3, n, 1.0)

    tm2, tn2 = 1024, 1024
    (dn,) = matmul(
        f"{tag}_dn", (T // tm2, D_MODEL // tn2, N_CHIPS),
        [op_a(da1, tm2, FF_PAD), op_a(da3, tm2, FF_PAD)],
        [op_b_rows(w1g, pre, FF_PAD, tn2), op_b_rows(w3g, pre, FF_PAD, tn2)],
        [(0, 0, 0), (1, 1, 0)], 1, [], [out_mn(T, D_MODEL, tm2, tn2, F32)], _acc0, (tm2, tn2))
    dh, dg, dh_lo = rms_bwd(f"{tag}_rms_bwd", h, g, dn, D_MODEL, dres=dh_out, bf16_copy=True)
    return dh, dh_lo, dg, dw1, dw3, dw2


def rope_tables(positions):
    inv_freq = ROPE_BASE ** (-jnp.arange(0, QK_ROPE, 2, dtype=F32) / QK_ROPE)
    ang = positions.astype(F32)[:, None] * inv_freq
    cos, sin = jnp.cos(ang), jnp.sin(ang)
    T = positions.shape[0]
    one, zero = jnp.ones((T, QK_NOPE), F32), jnp.zeros((T, 64), F32)
    z32, z128 = jnp.zeros((T, 32), F32), jnp.zeros((T, QK_NOPE), F32)
    c = jnp.concatenate([one, cos, cos, zero], axis=1)
    s1 = jnp.concatenate([z128, -sin, z32, zero], axis=1)
    s2 = jnp.concatenate([z128, z32, sin, zero], axis=1)
    return c, s1, s2


def _rope(y, c, s1, s2):
    return y * c + pltpu.roll(y, HEAD_PAD - 32, 1) * s1 + pltpu.roll(y, 32, 1) * s2


def _rope_t(d, c, s1, s2):
    return d * c + pltpu.roll(d * s1, 32, 1) + pltpu.roll(d * s2, HEAD_PAD - 32, 1)


def _head_norm(x):
    r = lax.rsqrt(jnp.sum(x * x, axis=-1, keepdims=True) * (1.0 / QK_DIM) + EPS)
    return x * r, r


def qk_prep_fwd(tag, q_raw, kk_raw, z_p, gq, gk, tabs, tm=256):
    T = q_raw.shape[0]
    c, s1, s2 = tabs

    def body(q_ref, k_ref, kr_ref, gq_ref, gk_ref, c_ref, s1_ref, s2_ref, qo_ref, ko_ref):
        cv, s1v, s2v = c_ref[...], s1_ref[...], s2_ref[...]
        kr = kr_ref[...]
        for h in range(HEADS):
            sl = slice(h * HEAD_PAD, (h + 1) * HEAD_PAD)
            xh, _ = _head_norm(q_ref[:, sl])
            qo_ref[:, sl] = (_rope(xh * gq_ref[...], cv, s1v, s2v) * ATTN_SCALE).astype(BF16)
            xh, _ = _head_norm(k_ref[:, sl] + kr)
            ko_ref[:, sl] = _rope(xh * gk_ref[...], cv, s1v, s2v).astype(BF16)

    row = lambda i: (i, 0)
    full = pl.BlockSpec((tm, HEADS * HEAD_PAD), row)
    tab = pl.BlockSpec((tm, HEAD_PAD), row)
    vec = pl.BlockSpec((1, HEAD_PAD), lambda i: (0, 0))
    return pl.pallas_call(
        body, name=f"{tag}_qk_prep", grid=(T // tm,),
        in_specs=[full, full, pl.BlockSpec((tm, HEAD_PAD), lambda i: (i, 3)), vec, vec, tab, tab, tab],
        out_specs=[full, full],
        out_shape=[jax.ShapeDtypeStruct((T, HEADS * HEAD_PAD), BF16)] * 2,
        compiler_params=_params(("parallel",)),
    )(q_raw, kk_raw, z_p, gq, gk, c, s1, s2)


def qk_prep_bwd(tag, dq_full, dk_full, q_raw, kk_raw, z_p, gq, gk, tabs, tm=256):
    T = q_raw.shape[0]
    c, s1, s2 = tabs

    def body(dq_ref, dk_ref, q_ref, k_ref, kr_ref, gq_ref, gk_ref, c_ref, s1_ref, s2_ref,
             dqr_ref, dkr_ref, dz_ref, dgq_ref, dgk_ref):
        i = pl.program_id(0)
        cv, s1v, s2v = c_ref[...], s1_ref[...], s2_ref[...]
        kr = kr_ref[...]
        lane = lax.broadcasted_iota(jnp.int32, (tm, HEAD_PAD), 1)
        slot = ((lane >= QK_NOPE) & (lane < QK_DIM)).astype(F32)

        def one(x, g, d):
            xh, r = _head_norm(x)
            dy = _rope_t(d, cv, s1v, s2v)
            gd = dy * g
            dx = r * (gd - xh * (jnp.sum(gd * xh, axis=-1, keepdims=True) * (1.0 / QK_DIM)))
            return dx, jnp.sum(dy * xh, axis=0, keepdims=True)

        dgq = jnp.zeros((1, HEAD_PAD), F32)
        dgk = jnp.zeros((1, HEAD_PAD), F32)
        dz = jnp.zeros((tm, HEAD_PAD), F32)
        for h in range(HEADS):
            sl = slice(h * HEAD_PAD, (h + 1) * HEAD_PAD)
            dx, dg = one(q_ref[:, sl], gq_ref[...], dq_ref[:, sl].astype(F32) * ATTN_SCALE)
            dqr_ref[:, sl] = dx
            dgq = dgq + dg
            dx, dg = one(k_ref[:, sl] + kr, gk_ref[...], dk_ref[:, sl].astype(F32))
            dkr_ref[:, sl] = dx
            dgk = dgk + dg
            dz = dz + dx
        dz_ref[...] = dz * slot

        @pl.when(i == 0)
        def _():
            dgq_ref[...] = dgq
            dgk_ref[...] = dgk

        @pl.when(i > 0)
        def _():
            dgq_ref[...] += dgq
            dgk_ref[...] += dgk

    row = lambda i: (i, 0)
    full = pl.BlockSpec((tm, HEADS * HEAD_PAD), row)
    tab = pl.BlockSpec((tm, HEAD_PAD), row)
    vec = pl.BlockSpec((1, HEAD_PAD), lambda i: (0, 0))
    return pl.pallas_call(
        body, name=f"{tag}_qk_prep_bwd", grid=(T // tm,),
        in_specs=[full, full, full, full, pl.BlockSpec((tm, HEAD_PAD), lambda i: (i, 3)), vec, vec, tab, tab, tab],
        out_specs=[full, full, tab, vec, vec],
        out_shape=[jax.ShapeDtypeStruct((T, HEADS * HEAD_PAD), F32)] * 2
        + [jax.ShapeDtypeStruct((T, HEAD_PAD), F32)] + [jax.ShapeDtypeStruct((1, HEAD_PAD), F32)] * 2,
        compiler_params=_params(("arbitrary",)),
    )(dq_full, dk_full, q_raw, kk_raw, z_p, gq, gk, c, s1, s2)


def attn_fwd(tag, q_full, k_full, vv, blk=512):
    T = q_full.shape[0]
    nb = T // blk
    neg = float(jnp.finfo(jnp.float32).min)

    def body(q_ref, k_ref, v_ref, o_ref, lse_ref, m_ref, l_ref, acc_ref):
        i = pl.program_id(1)
        m_ref[...] = jnp.full_like(m_ref, neg)
        l_ref[...] = jnp.zeros_like(l_ref)
        acc_ref[...] = jnp.zeros_like(acc_ref)
        sub = blk // SUB_BLOCKS

        def step(j, masked):
            rows = pl.ds(pl.multiple_of(j * blk, blk), blk)
            k, v = k_ref[rows, :], v_ref[rows, :]
            for part in range(SUB_BLOCKS):
                qs = slice(part * sub, (part + 1) * sub)
                s = lax.dot_general(q_ref[qs, :], k, (((1,), (1,)), ((), ())), preferred_element_type=F32)
                if masked:
                    row = lax.broadcasted_iota(jnp.int32, (sub, blk), 0) + part * sub
                    col = lax.broadcasted_iota(jnp.int32, (sub, blk), 1)
                    s = jnp.where(col <= row, s, neg)
                m_prev = m_ref[qs, :]
                m_new = jnp.maximum(m_prev, jnp.max(s, axis=-1, keepdims=True))
                alpha = jnp.exp(m_prev - m_new)
                p = jnp.exp(s - m_new[:, :1])
                l_ref[qs, :] = alpha * l_ref[qs, :] + jnp.sum(p, axis=-1, keepdims=True)
                acc_ref[qs, :] = alpha * acc_ref[qs, :] + jnp.dot(p.astype(BF16), v, preferred_element_type=F32)
                m_ref[qs, :] = m_new

        def off_diagonal(j, carry):
            step(j, False)
            return carry

        lax.fori_loop(0, i, off_diagonal, 0)
        step(i, True)
        o_ref[...] = acc_ref[...] / l_ref[...]
        lse_ref[...] = m_ref[...] + jnp.log(l_ref[...])

    return pl.pallas_call(
        body, name=f"{tag}_attn_fwd", grid=(HEADS, nb),
        in_specs=[pl.BlockSpec((blk, HEAD_PAD), lambda h, i: (i, h)),
                  pl.BlockSpec((T, HEAD_PAD), lambda h, i: (0, h)), pl.BlockSpec((T, V_DIM), lambda h, i: (0, h))],
        out_specs=[pl.BlockSpec((blk, V_DIM), lambda h, i: (i, h))] * 2,
        out_shape=[jax.ShapeDtypeStruct((T, ATTN_W), F32)] * 2,
        scratch_shapes=[pltpu.VMEM((blk, V_DIM), F32)] * 3,
        compiler_params=_params(("parallel", "parallel")),
    )(q_full, k_full, vv)


def attn_bwd(tag, q_full, k_full, vv, do, lse, delta, blk=512):
    T = q_full.shape[0]
    nb = T // blk
    neg = float(jnp.finfo(jnp.float32).min)

    def body(q_ref, k_ref, v_ref, do_ref, lse_ref, dl_ref, dq_ref, dk_ref, dv_ref, dk_acc, dv_acc):
        j = pl.program_id(1)

        @pl.when(j == 0)
        def _():
            dq_ref[...] = jnp.zeros_like(dq_ref)

        dk_acc[...] = jnp.zeros_like(dk_acc)
        dv_acc[...] = jnp.zeros_like(dv_acc)
        k, v = k_ref[...], v_ref[...]

        sub = blk // SUB_BLOCKS

        def step(i, masked):
            for part in range(SUB_BLOCKS):
                rows = pl.ds(pl.multiple_of(i * blk + part * sub, sub), sub)
                q = q_ref[rows, :]
                s = lax.dot_general(q, k, (((1,), (1,)), ((), ())), preferred_element_type=F32)
                if masked:
                    row = lax.broadcasted_iota(jnp.int32, (sub, blk), 0) + part * sub
                    col = lax.broadcasted_iota(jnp.int32, (sub, blk), 1)
                    s = jnp.where(col <= row, s, neg)
                p = jnp.exp(s - lse_ref[rows, :1])
                dob = _bf(do_ref[rows, :])
                dv_acc[...] += lax.dot_general(p.astype(BF16), dob, (((0,), (0,)), ((), ())), preferred_element_type=F32)
                dp = lax.dot_general(dob, v, (((1,), (1,)), ((), ())), preferred_element_type=F32)
                ds = (p * (dp - dl_ref[rows, :1])).astype(BF16)
                dk_acc[...] += lax.dot_general(ds, q, (((0,), (0,)), ((), ())), preferred_element_type=F32)
                dq_ref[rows, :] += jnp.dot(ds, k, preferred_element_type=F32)

        def off_diagonal(i, carry):
            step(i, False)
            return carry

        step(j, True)
        lax.fori_loop(j + 1, nb, off_diagonal, 0)
        dk_ref[...] = dk_acc[...]
        dv_ref[...] = dv_acc[...]

    head = lambda h, j: (0, h)
    kv_ix = lambda h, j: (j, h)
    return pl.pallas_call(
        body, name=f"{tag}_attn_bwd", grid=(HEADS, nb),
        in_specs=[pl.BlockSpec((T, HEAD_PAD), head), pl.BlockSpec((blk, HEAD_PAD), kv_ix),
                  pl.BlockSpec((blk, V_DIM), kv_ix), pl.BlockSpec((T, V_DIM), head),
                  pl.BlockSpec((T, V_DIM), head), pl.BlockSpec((T, V_DIM), head)],
        out_specs=[pl.BlockSpec((T, HEAD_PAD), head),
                   pl.BlockSpec((blk, HEAD_PAD), kv_ix), pl.BlockSpec((blk, V_DIM), kv_ix)],
        out_shape=[jax.ShapeDtypeStruct((T, HEADS * HEAD_PAD), F32)] * 2 + [jax.ShapeDtypeStruct((T, ATTN_W), F32)],
        scratch_shapes=[pltpu.VMEM((blk, HEAD_PAD), F32), pltpu.VMEM((blk, V_DIM), F32)],
        compiler_params=_params(("parallel", "arbitrary")),
    )(q_full, k_full, vv, do, lse, delta)


def _gm_forward(u, v, gv, wc_ref, bb_ref, nchunk):
    ug = _gelu(u)
    vg = _gelu(v)
    rv = lax.rsqrt(jnp.mean(vg * vg, axis=-1, keepdims=True) + EPS)
    vhat = vg * rv
    vn = (vhat * gv).astype(BF16)
    gates = []
    for cidx in range(nchunk):
        rows = slice(cidx * CHUNK, (cidx + 1) * CHUNK)
        gates.append(jnp.concatenate(
            [jnp.dot(wc_ref[gidx], vn[rows, gidx * 128:(gidx + 1) * 128], preferred_element_type=F32) + bb_ref[gidx]
             for gidx in range(GROUPS)], axis=1))
    gate = jnp.concatenate(gates, axis=0)
    return ug, vhat, rv, vn, gate


def gmlp_fwd(tag, z_p, gv, gout, wc, bb, tm=256):
    T = z_p.shape[0]
    nchunk = tm // CHUNK

    def body(u_ref, v_ref, gv_ref, go_ref, wc_ref, bb_ref, o_ref):
        ug, _, _, _, gate = _gm_forward(u_ref[...], v_ref[...], gv_ref[...], wc_ref, bb_ref, nchunk)
        go = ug * gate
        ro = lax.rsqrt(jnp.mean(go * go, axis=-1, keepdims=True) + EPS)
        o_ref[...] = (go * ro * go_ref[...]).astype(BF16)

    vec = pl.BlockSpec((1, GM_W), lambda i: (0, 0))
    w3 = pl.BlockSpec((GROUPS, CHUNK, CHUNK), lambda i: (0, 0, 0))
    return pl.pallas_call(
        body, name=f"{tag}_gmlp_fwd", grid=(T // tm,),
        in_specs=[pl.BlockSpec((tm, GM_W), lambda i: (i, 1)), pl.BlockSpec((tm, GM_W), lambda i: (i, 2)), vec, vec, w3, w3],
        out_specs=pl.BlockSpec((tm, GM_W), lambda i: (i, 0)),
        out_shape=jax.ShapeDtypeStruct((T, GM_W), BF16),
        compiler_params=_params(("parallel",)),
    )(z_p, z_p, gv.reshape(1, GM_W), gout.reshape(1, GM_W), wc, bb)


def gmlp_bwd(tag, z_p, dmixed, gv, gout, wc, bb, tm=256):
    T = z_p.shape[0]
    nchunk = tm // CHUNK

    def body(u_ref, v_ref, dm_ref, gv_ref, go_ref, wc_ref, bb_ref, du_ref, dv_ref, dwc_ref, dbb_ref, dgv_ref, dgo_ref):
        i = pl.program_id(0)
        u, v = u_ref[...], v_ref[...]
        ug, vhat, rv, vn, gate = _gm_forward(u, v, gv_ref[...], wc_ref, bb_ref, nchunk)
        go = ug * gate
        ro = lax.rsqrt(jnp.mean(go * go, axis=-1, keepdims=True) + EPS)
        ohat = go * ro
        dm = dm_ref[...].astype(F32)
        dgo_part = jnp.sum(dm * ohat, axis=0, keepdims=True)
        doh = dm * go_ref[...]
        dgo = ro * (doh - ohat * jnp.mean(doh * ohat, axis=-1, keepdims=True))
        du_ref[...] = dgo * gate * _gelu_grad(u)
        dgate = dgo * ug
        dgb = dgate.astype(BF16)
        dvn_rows = []
        dwc_parts = []
        dbb_parts = []
        for gidx in range(GROUPS):
            cols = slice(gidx * 128, (gidx + 1) * 128)
            dw = jnp.zeros((CHUNK, CHUNK), F32)
            db = jnp.zeros((CHUNK, 128), F32)
            for cidx in range(nchunk):
                rows = slice(cidx * CHUNK, (cidx + 1) * CHUNK)
                dw = dw + lax.dot_general(dgb[rows, cols], vn[rows, cols], (((1,), (1,)), ((), ())),
                                          preferred_element_type=F32)
                db = db + dgate[rows, cols]
            dwc_parts.append(dw)
            dbb_parts.append(db)
        for cidx in range(nchunk):
            rows = slice(cidx * CHUNK, (cidx + 1) * CHUNK)
            dvn_rows.append(jnp.concatenate(
                [lax.dot_general(wc_ref[gidx], dgb[rows, gidx * 128:(gidx + 1) * 128], (((0,), (0,)), ((), ())),
                                 preferred_element_type=F32) for gidx in range(GROUPS)], axis=1))
        dvn = jnp.concatenate(dvn_rows, axis=0)
        dgv_part = jnp.sum(dvn * vhat, axis=0, keepdims=True)
        dvh = dvn * gv_ref[...]
        dvg = rv * (dvh - vhat * jnp.mean(dvh * vhat, axis=-1, keepdims=True))
        dv_ref[...] = dvg * _gelu_grad(v)

        @pl.when(i == 0)
        def _():
            for gidx in range(GROUPS):
                dwc_ref[gidx] = dwc_parts[gidx]
                dbb_ref[gidx] = dbb_parts[gidx]
            dgv_ref[...] = dgv_part
            dgo_ref[...] = dgo_part

        @pl.when(i > 0)
        def _():
            for gidx in range(GROUPS):
                dwc_ref[gidx] += dwc_parts[gidx]
                dbb_ref[gidx] += dbb_parts[gidx]
            dgv_ref[...] += dgv_part
            dgo_ref[...] += dgo_part

    vec = pl.BlockSpec((1, GM_W), lambda i: (0, 0))
    w3 = pl.BlockSpec((GROUPS, CHUNK, CHUNK), lambda i: (0, 0, 0))
    blk = pl.BlockSpec((tm, GM_W), lambda i: (i, 0))
    return pl.pallas_call(
        body, name=f"{tag}_gmlp_bwd", grid=(T // tm,),
        in_specs=[pl.BlockSpec((tm, GM_W), lambda i: (i, 1)), pl.BlockSpec((tm, GM_W), lambda i: (i, 2)),
                  pl.BlockSpec((tm, GM_W), lambda i: (i, 1)), vec, vec, w3, w3],
        out_specs=[blk, blk, w3, w3, vec, vec],
        out_shape=[jax.ShapeDtypeStruct((T, GM_W), F32)] * 2 + [jax.ShapeDtypeStruct((GROUPS, CHUNK, CHUNK), F32)] * 2
        + [jax.ShapeDtypeStruct((1, GM_W), F32)] * 2,
        compiler_params=_params(("arbitrary",)),
    )(z_p, z_p, dmixed, gv.reshape(1, GM_W), gout.reshape(1, GM_W), wc, bb)


def mixer_fwd(tag, h, w, tabs, wout_g, pre):
    T = h.shape[0]
    n2 = rms_fwd(f"{tag}_mix_rms", h, w["mix_norm"], D_MODEL)
    (z_p,) = mm_simple(f"{tag}_win", n2, lambda tk, tn: op_bt(w["w_in_pt"], tk, tn), T, IN_P, D_MODEL, 512, 1024, D_MODEL)
    cqn = rms_fwd(f"{tag}_cq_rms", z_p, w["q_a_norm"], Q_RANK, col_blk=0)
    ckvn = rms_fwd(f"{tag}_ckv_rms", z_p, w["kv_a_norm"], KV_RANK, col_blk=2)
    (q_raw,) = mm_simple(f"{tag}_wq", cqn, lambda tk, tn: op_b(w["wq_p"], tk, tn), T, 2048, Q_RANK, 512, 1024, Q_RANK)
    (kk_raw,) = mm_simple(f"{tag}_wk", ckvn, lambda tk, tn: op_b(w["wk_p"], tk, tn), T, 2048, KV_RANK, 512, 1024, KV_RANK)
    (vv,) = mm_simple(f"{tag}_wv", ckvn, lambda tk, tn: op_b(w["wv"], tk, tn), T, ATTN_W, KV_RANK, 512, 1024, KV_RANK,
                      out_dtype=BF16)
    q_full, k_full = qk_prep_fwd(tag, q_raw, kk_raw, z_p, w["gq_p"], w["gk_p"], tabs)
    a_out, lse = attn_fwd(tag, q_full, k_full, vv)
    mixed_a = rms_fwd(f"{tag}_ao_rms", a_out, w["attn_out_norm"], ATTN_W)
    mixed_g = gmlp_fwd(tag, z_p, w["gm_v_norm"], w["gm_out_norm"], w["wc"], w["bb"])
    tm, tn, tk = 512, 1024, 512
    (h2,) = matmul(
        f"{tag}_wout", (T // tm, D_MODEL // tn, ATTN_W // tk),
        [op_a(mixed_a, tm, tk), op_a(mixed_g, tm, tk)],
        [op_b_rows(wout_g, pre, tk, tn), op_b_rows(wout_g, pre, tk, tn, koff=ATTN_W // tk)],
        [(0, 0, 0), (1, 1, 0)], 1, [tile_mn(h, tm, tn)], [out_mn(T, D_MODEL, tm, tn, F32)],
        lambda accs, xs: (xs[0] + accs[0],), (tm, tn))
    res = dict(n2=n2, z_p=z_p, cqn=cqn, ckvn=ckvn, q_raw=q_raw, kk_raw=kk_raw, vv=vv, q_full=q_full, k_full=k_full,
               a_out=a_out, lse=lse, mixed_a=mixed_a, mixed_g=mixed_g)
    return h2, res


def mixer_bwd(tag, dh2, dh2_bf, h, w, tabs, wout_g, pre, r, after=None):
    T = h.shape[0]
    g = {}
    (dmixed,) = mm_simple(f"{tag}_dmixed", dh2_bf, lambda tk, tn: op_b_rows_t(wout_g, pre, tk, tn), T, D_MODEL, D_MODEL,
                          512, 512, D_MODEL, after=after)
    (dwo_a,) = mm_simple(f"{tag}_dwout_a", r["mixed_a"], lambda tk, tn: op_b(dh2_bf, tk, tn), ATTN_W, D_MODEL, T,
                         1024, 1024, 512, a_t=True, out_dtype=BF16)
    (dwo_g,) = mm_simple(f"{tag}_dwout_g", r["mixed_g"], lambda tk, tn: op_b(dh2_bf, tk, tn), GM_W, D_MODEL, T,
                         1024, 1024, 512, a_t=True, out_dtype=BF16)
    g["w_out"] = jnp.concatenate([dwo_a, dwo_g], axis=0)
    da_out, g["attn_out_norm"], delta = rms_bwd(f"{tag}_ao_rms_bwd", r["a_out"], w["attn_out_norm"], dmixed, ATTN_W,
                                                with_delta=True)
    dq_full, dk_full, dvv = attn_bwd(tag, r["q_full"], r["k_full"], r["vv"], da_out, r["lse"], delta)
    dq_raw, dkk_raw, dzkr, g["gq_p"], g["gk_p"] = qk_prep_bwd(tag, dq_full, dk_full, r["q_raw"], r["kk_raw"], r["z_p"],
                                                            w["gq_p"], w["gk_p"], tabs)
    (g["wq_p"],) = mm_simple(f"{tag}_dwq", r["cqn"], lambda tk, tn: op_b(dq_raw, tk, tn), Q_RANK, 2048, T, Q_RANK, 1024, 512,
                             a_t=True, out_dtype=BF16)
    (g["wk_p"],) = mm_simple(f"{tag}_dwk", r["ckvn"], lambda tk, tn: op_b(dkk_raw, tk, tn), KV_RANK, 2048, T, KV_RANK, 1024,
                             512, a_t=True, out_dtype=BF16)
    (g["wv"],) = mm_simple(f"{tag}_dwv", r["ckvn"], lambda tk, tn: op_b(dvv, tk, tn), KV_RANK, ATTN_W, T, KV_RANK, 1024, 512,
                           a_t=True, out_dtype=BF16)
    (dcqn,) = mm_simple(f"{tag}_dcqn", dq_raw, lambda tk, tn: op_bt(w["wq_p"], tk, tn), T, Q_RANK, 2048, 512, Q_RANK, 2048)
    (dck1,) = mm_simple(f"{tag}_dckvn_k", dkk_raw, lambda tk, tn: op_bt(w["wk_p"], tk, tn), T, KV_RANK, 2048, 512, KV_RANK,
                        2048)
    (dckvn,) = mm_simple(f"{tag}_dckvn_v", dvv, lambda tk, tn: op_bt(w["wv"], tk, tn), T, KV_RANK, ATTN_W, 512, KV_RANK,
                         ATTN_W, extras=[tile_mn(dck1, 512, KV_RANK)], epilogue=lambda accs, xs: (accs[0] + xs[0],))
    dc_q, g["q_a_norm"] = rms_bwd(f"{tag}_cq_rms_bwd", r["z_p"], w["q_a_norm"], dcqn, Q_RANK, col_blk=0)
    dc_kv, g["kv_a_norm"] = rms_bwd(f"{tag}_ckv_rms_bwd", r["z_p"], w["kv_a_norm"], dckvn, KV_RANK, col_blk=2)
    du, dv, g["wc"], g["bb"], g["gm_v_norm"], g["gm_out_norm"] = gmlp_bwd(
        tag, r["z_p"], dmixed, w["gm_v_norm"], w["gm_out_norm"], w["wc"], w["bb"])
    dz_p = jnp.concatenate([dc_q, dc_kv, dzkr, du, dv], axis=1).astype(BF16)
    (g["w_in_pt"],) = mm_simple(f"{tag}_dwin", dz_p, lambda tk, tn: op_b(r["n2"], tk, tn), IN_P, D_MODEL, T, 1024, 1024, 512,
                                a_t=True, out_dtype=BF16)
    (dn2,) = mm_simple(f"{tag}_dn2", dz_p, lambda tk, tn: op_b(w["w_in_pt"], tk, tn), T, D_MODEL, IN_P, 512, 1024, IN_P)
    dh1, g["mix_norm"], dh1_bf = rms_bwd(f"{tag}_mix_rms_bwd", h, w["mix_norm"], dn2, D_MODEL, dres=dh2, bf16_copy=True)
    return dh1, dh1_bf, g


def ple_fwd(tag, h3, p_l, w, wpg_g, wple_g, pre):
    T = h3.shape[0]
    (pw,) = mm_simple(f"{tag}_wple", p_l, lambda tk, tn: op_b_cols(wple_g, pre, tk, tn), T, D_MODEL, PLE_DIM, 512, 512,
                      PLE_DIM)
    e = rms_fwd(f"{tag}_ple_rms", pw, w["ple_norm"], D_MODEL, out_dtype=F32)
    n4 = rms_fwd(f"{tag}_pg_rms", h3, w["ple_gate_norm"], D_MODEL)

    def epi(accs, xs):
        gt = _sigmoid(accs[0])
        return xs[0] + gt * xs[1], gt

    tm, tn, tk = 512, 1024, 512
    h4, gate = matmul(
        f"{tag}_wpg", (T // tm, D_MODEL // tn, D_MODEL // tk),
        [op_a(n4, tm, tk)], [op_b_rows(wpg_g, pre, tk, tn)], [(0, 0, 0)], 1,
        [tile_mn(h3, tm, tn), tile_mn(e, tm, tn)],
        [out_mn(T, D_MODEL, tm, tn, F32), out_mn(T, D_MODEL, tm, tn, BF16)], epi, (tm, tn))
    return h4, dict(pw=pw, e=e, n4=n4, gate=gate)


def ple_bwd(tag, dh4, h3, p_l, w, wpg_g, wple_g, pre, r, tm=256):
    T = h3.shape[0]

    def act_body(d_ref, g_ref, e_ref, dpre_ref, de_ref):
        d, gt = d_ref[...], g_ref[...].astype(F32)
        dpre_ref[...] = (d * e_ref[...] * gt * (1.0 - gt)).astype(BF16)
        de_ref[...] = d * gt

    blk = pl.BlockSpec((tm, D_MODEL), lambda i: (i, 0))
    dpre, de = pl.pallas_call(
        act_body, name=f"{tag}_ple_act_bwd", grid=(T // tm,), in_specs=[blk, blk, blk], out_specs=[blk, blk],
        out_shape=[jax.ShapeDtypeStruct((T, D_MODEL), BF16), jax.ShapeDtypeStruct((T, D_MODEL), F32)],
        compiler_params=_params(("parallel",)),
    )(dh4, r["gate"], r["e"])
    g = {}
    (g["w_ple_gate"],) = mm_simple(f"{tag}_dwpg", r["n4"], lambda tk, tn: op_b(dpre, tk, tn), D_MODEL, D_MODEL, T,
                                   1024, 1024, 512, a_t=True, out_dtype=BF16)
    (dn4,) = mm_simple(f"{tag}_dn4", dpre, lambda tk, tn: op_b_rows_t(wpg_g, pre, tk, tn), T, D_MODEL, D_MODEL, 512, 512,
                       D_MODEL)
    dh3, g["ple_gate_norm"], dh3_bf = rms_bwd(f"{tag}_pg_rms_bwd", h3, w["ple_gate_norm"], dn4, D_MODEL, dres=dh4,
                                              bf16_copy=True)
    dpw, g["ple_norm"] = rms_bwd(f"{tag}_ple_rms_bwd", r["pw"], w["ple_norm"], de, D_MODEL)
    (g["w_ple"],) = mm_simple(f"{tag}_dwple", p_l, lambda tk, tn: op_b(dpw, tk, tn), PLE_DIM, D_MODEL, T, PLE_DIM, 512, 512,
                              a_t=True, outs=[out_cols(PLE_DIM, 512, PLE_DIM, 512, BF16)])
    return dh3, dh3_bf, g


def loss_grad(y, target, tm=256):
    T = y.shape[0]

    def body(y_ref, t_ref, dy_ref, l_ref):
        i = pl.program_id(0)
        d = y_ref[...] - t_ref[...]
        dy_ref[...] = d * (1.0 / D_MODEL)
        part = jnp.sum((d * d).reshape(tm // 8, 8, D_MODEL), axis=0)

        @pl.when(i == 0)
        def _():
            l_ref[...] = part

        @pl.when(i > 0)
        def _():
            l_ref[...] += part

    blk = pl.BlockSpec((tm, D_MODEL), lambda i: (i, 0))
    dy, part = pl.pallas_call(
        body, name="loss_grad", grid=(T // tm,), in_specs=[blk, blk],
        out_specs=[blk, pl.BlockSpec((8, D_MODEL), lambda i: (0, 0))],
        out_shape=[jax.ShapeDtypeStruct((T, D_MODEL), F32), jax.ShapeDtypeStruct((8, D_MODEL), F32)],
        compiler_params=_params(("arbitrary",)),
    )(y, target)
    return dy, 0.5 * jnp.sum(part) / D_MODEL


def _unshard_cols(g_l):
    return g_l.transpose(1, 0, 2).reshape(g_l.shape[1], -1)


def _shard_cols(w):
    return w.reshape(w.shape[0], N_CHIPS, -1).transpose(1, 0, 2)


def layer_weights(l, Gl, small):
    w = {k: small[k][l] for k in ("mix_norm", "q_a_norm", "kv_a_norm", "gm_v_norm", "attn_out_norm", "gm_out_norm",
                                  "ple_gate_norm", "ple_norm")}
    wint = Gl["w_in"][:, :IN_SHARD].reshape(-1, D_MODEL)
    z = lambda n: jnp.zeros((n, D_MODEL), BF16)
    w["w_in_pt"] = jnp.concatenate([wint[:768], z(128), wint[768:832], z(64), wint[832:]], axis=0)
    wuq = _unshard_cols(Gl["w_uq"]).reshape(Q_RANK, HEADS, QK_DIM)
    w["wq_p"] = jnp.pad(wuq, ((0, 0), (0, 0), (0, HEAD_PAD - QK_DIM))).reshape(Q_RANK, HEADS * HEAD_PAD)
    wukv = _unshard_cols(Gl["w_ukv"]).reshape(KV_RANK, HEADS, QK_NOPE + V_DIM)
    w["wk_p"] = jnp.pad(wukv[:, :, :QK_NOPE], ((0, 0), (0, 0), (0, HEAD_PAD - QK_NOPE))).reshape(KV_RANK, HEADS * HEAD_PAD)
    w["wv"] = wukv[:, :, QK_NOPE:].reshape(KV_RANK, ATTN_W)
    w["gq_p"] = jnp.pad(small["q_norm"][l], (0, HEAD_PAD - QK_DIM)).reshape(1, HEAD_PAD)
    w["gk_p"] = jnp.pad(small["k_norm"][l], (0, HEAD_PAD - QK_DIM)).reshape(1, HEAD_PAD)
    tril = jnp.tril(jnp.ones((CHUNK, CHUNK), dtype=bool))
    w["wc"] = jnp.where(tril[None], small["gm_ws"][l], 0.0).astype(BF16)
    w["bb"] = jnp.broadcast_to(small["gm_bs"][l][:, :, None], (GROUPS, CHUNK, 128)).astype(F32)
    return w


def mixer_grads_to_shards(g):
    out = {}
    dwint = g["w_in_pt"]
    dwint = jnp.concatenate([dwint[:768], dwint[896:960], dwint[1024:]], axis=0).reshape(N_CHIPS, IN_SHARD, D_MODEL)
    out["w_in"] = jnp.pad(dwint, ((0, 0), (0, IN_SHARD_PAD - IN_SHARD), (0, 0)))
    dwuq = g["wq_p"].reshape(Q_RANK, HEADS, HEAD_PAD)[:, :, :QK_DIM].reshape(Q_RANK, HEADS * QK_DIM)
    out["w_uq"] = _shard_cols(dwuq)
    dwukv = jnp.concatenate([g["wk_p"].reshape(KV_RANK, HEADS, HEAD_PAD)[:, :, :QK_NOPE],
                             g["wv"].reshape(KV_RANK, HEADS, V_DIM)], axis=-1).reshape(KV_RANK, HEADS * (QK_NOPE + V_DIM))
    out["w_ukv"] = _shard_cols(dwukv)
    out["w_out"] = g["w_out"].reshape(N_CHIPS, D_MODEL // N_CHIPS, D_MODEL)
    out["q_norm"] = g["gq_p"][0, :QK_DIM]
    out["k_norm"] = g["gk_p"][0, :QK_DIM]
    tril = jnp.tril(jnp.ones((CHUNK, CHUNK), dtype=bool))
    out["gm_ws"] = jnp.where(tril[None], g["wc"], 0.0)
    out["gm_bs"] = jnp.sum(g["bb"], axis=-1)
    for k in ("mix_norm", "q_a_norm", "kv_a_norm", "gm_v_norm", "attn_out_norm", "gm_out_norm"):
        out[k] = g[k][0]
    return out


def layer_fwd(l, h, p_l, Gl, small, tabs, after_first_ffn=None, before_ple=None):
    h1, r_a = ffn_fwd(f"l{l}a", h, small["ffn_a_norm"][l], Gl["ffn_a_w1"], Gl["ffn_a_w3"], Gl["ffn_a_w2"], ())
    if after_first_ffn is not None:
        Gl, small = after_first_ffn(h1, Gl, small)
    w = layer_weights(l, Gl, small)
    h2, r_m = mixer_fwd(f"l{l}", h1, w, tabs, Gl["w_out"], ())
    h3, r_b = ffn_fwd(f"l{l}b", h2, small["ffn_b_norm"][l], Gl["ffn_b_w1"], Gl["ffn_b_w3"], Gl["ffn_b_w2"], ())
    if before_ple is not None:
        w = {**w, "ple_norm": w["ple_norm"] + before_ple(h3)[0, 0]}
    h4, r_p = ple_fwd(f"l{l}", h3, p_l, w, Gl["w_ple_gate"], Gl["w_ple"], ())
    return h4, (w, h, h1, h2, h3, r_a, r_m, r_b, r_p)


def layer_bwd(l, dh, p_l, Gl, small, tabs, saved, before=None):
    w, h0, h1, h2, h3, r_a, r_m, r_b, r_p = saved
    slabs = lambda d: d.reshape(N_CHIPS, FF_PAD, D_MODEL)
    hook = lambda block: before[block](gl, dh) if before and block in before else None
    gl = {}
    dh, dh_bf, g_p = ple_bwd(f"l{l}", dh, h3, p_l, w, Gl["w_ple_gate"], Gl["w_ple"], (), r_p)
    gl["w_ple_gate"] = g_p["w_ple_gate"].reshape(N_CHIPS, D_MODEL // N_CHIPS, D_MODEL)
    gl["w_ple"] = g_p["w_ple"]
    gl["ple_gate_norm"], gl["ple_norm"] = g_p["ple_gate_norm"][0], g_p["ple_norm"][0]
    dh, dh_bf, dg, dw1, dw3, dw2 = ffn_bwd(f"l{l}b", dh, dh_bf, h2, small["ffn_b_norm"][l], r_b,
                                           Gl["ffn_b_w1"], Gl["ffn_b_w3"], Gl["ffn_b_w2"], (), hook("ffn_b"))
    gl["ffn_b_norm"] = dg[0]
    gl["ffn_b_w1"], gl["ffn_b_w3"], gl["ffn_b_w2"] = slabs(dw1), slabs(dw3), slabs(dw2)
    dh, dh_bf, g_m = mixer_bwd(f"l{l}", dh, dh_bf, h1, w, tabs, Gl["w_out"], (), r_m, hook("mixer"))
    gl.update(mixer_grads_to_shards(g_m))
    dh, _, dg, dw1, dw3, dw2 = ffn_bwd(f"l{l}a", dh, dh_bf, h0, small["ffn_a_norm"][l], r_a,
                                       Gl["ffn_a_w1"], Gl["ffn_a_w3"], Gl["ffn_a_w2"], (), hook("ffn_a"))
    gl["ffn_a_norm"] = dg[0]
    gl["ffn_a_w1"], gl["ffn_a_w3"], gl["ffn_a_w2"] = slabs(dw1), slabs(dw3), slabs(dw2)
    return dh, gl


MESH = pl.DeviceIdType.MESH
HBM_SPEC = pl.BlockSpec(memory_space=pltpu.HBM)


def _place():
    x, y, c = lax.axis_index("x"), lax.axis_index("y"), lax.axis_index("c")
    others = [(1 - x, y), (x, 1 - y), (1 - x, 1 - y)]
    return x, y, c, 2 * x + y, others


def prep_shard(name, w, layer, rows_pad, place, after=None):
    _, ks, n = w.shape
    ksp = ks + rows_pad
    tc = 512 if n % 512 == 0 else n
    deps = [] if after is None else [after]

    def body(place_ref, x_ref, *rest):
        o_ref = rest[-1]
        o_ref[:ks] = x_ref[...].astype(BF16)
        if rows_pad:
            o_ref[ks:] = jnp.zeros((rows_pad, tc), BF16)

    return pl.pallas_call(
        body, name=name,
        grid_spec=pltpu.PrefetchScalarGridSpec(
            num_scalar_prefetch=1, grid=(n // tc,),
            in_specs=[pl.BlockSpec((None, ks, tc), lambda i, s: (layer, 0, i))] + [ANY_SPEC] * len(deps),
            out_specs=pl.BlockSpec((None, ksp, tc), lambda i, s: (s[0], 0, i))),
        out_shape=jax.ShapeDtypeStruct((N_CHIPS, ksp, n), BF16),
        compiler_params=_params(("parallel",)),
    )(place, w, *deps)


def share_halves(name, fulls):
    n = len(fulls)

    def body(*refs):
        o_refs = refs[n:2 * n]
        send, recv = refs[2 * n:]
        x, y, c, _, _ = _place()
        cps = []
        for w in range(n):
            kh = fulls[w].shape[0] // 2
            half = o_refs[w].at[pl.ds(c * kh, kh)]
            cps.append(pltpu.make_async_remote_copy(src_ref=half, dst_ref=half, send_sem=send.at[w], recv_sem=recv.at[w],
                                                    device_id=(x, y, 1 - c), device_id_type=MESH))
        for cp in cps:
            cp.start()
        for cp in cps:
            cp.wait()

    return pl.pallas_call(
        body, name=name, in_specs=[HBM_SPEC] * n, out_specs=[HBM_SPEC] * n,
        out_shape=[jax.ShapeDtypeStruct(f.shape, f.dtype) for f in fulls],
        input_output_aliases={w: w for w in range(n)},
        scratch_shapes=[pltpu.SemaphoreType.DMA((n,))] * 2,
    )(*fulls)


SEM_SPEC = pl.BlockSpec(memory_space=pltpu.SEMAPHORE)
ANY_SPEC = pl.BlockSpec(memory_space=pl.ANY)
DATAFLOW = pltpu.SideEffectType.DATAFLOW_SIDE_EFFECTING


def _hbm(x):
    return pltpu.with_memory_space_constraint(x, pltpu.HBM)


def _start_call(name, slots, after, issue):
    n = len(slots)
    deps = [] if after is None else [after]
    nd = len(deps)

    def body(*refs):
        issue(refs[n + nd + 2:2 * n + nd + 2], refs[n + nd], refs[n + nd + 1])
        token = refs[2 * n + nd + 2]
        token[...] = jnp.zeros_like(token)

    outs = pl.pallas_call(
        body, name=name,
        in_specs=[HBM_SPEC] * n + [ANY_SPEC] * nd,
        out_specs=(SEM_SPEC, SEM_SPEC, *([HBM_SPEC] * n), pl.BlockSpec(memory_space=pltpu.VMEM)),
        out_shape=(pltpu.SemaphoreType.DMA((n,)), pltpu.SemaphoreType.DMA((n,)),
                   *[pltpu.HBM(s.shape, s.dtype) for s in slots], jax.ShapeDtypeStruct((8, 128), F32)),
        input_output_aliases={w: w + 2 for w in range(n)},
        compiler_params=pltpu.CompilerParams(has_side_effects=DATAFLOW),
    )(*[_hbm(s) for s in slots], *deps)
    return outs[0], outs[1], list(outs[2:2 + n]), outs[2 + n]


def gather_start(name, slots, after):
    def issue(g_refs, send, recv):
        x, y, c, jme, others = _place()
        for w in range(len(slots)):
            kh = slots[w].shape[1] // 2
            mine = g_refs[w].at[jme, pl.ds(c * kh, kh)]
            for (px, py) in others:
                pltpu.make_async_remote_copy(src_ref=mine, dst_ref=mine, send_sem=send.at[w], recv_sem=recv.at[w],
                                             device_id=(px, py, c), device_id_type=MESH).start()

    return _start_call(name, slots, after, issue)


def forward_start(name, slots):
    def issue(g_refs, send, recv):
        x, y, c, _, others = _place()
        for w in range(len(slots)):
            kh = slots[w].shape[1] // 2
            for (px, py) in others:
                blk = g_refs[w].at[2 * px + py, pl.ds(c * kh, kh)]
                pltpu.make_async_remote_copy(src_ref=blk, dst_ref=blk, send_sem=send.at[w], recv_sem=recv.at[w],
                                             device_id=(x, y, 1 - c), device_id_type=MESH).start()

    return _start_call(name, slots, None, issue)


def gather_wait(name, send, recv, flying, after):
    n = len(flying)

    def body(*refs):
        send_ref, recv_ref = refs[n], refs[n + 1]
        g_refs = refs[n + 3:]
        x, y, c, _, _ = _place()
        for w in range(n):
            three = g_refs[w].at[pl.ds(0, 3), pl.ds(0, flying[w].shape[1] // 2)]
            cp = pltpu.make_async_remote_copy(src_ref=three, dst_ref=three, send_sem=send_ref.at[w], recv_sem=recv_ref.at[w],
                                              device_id=(x, y, 1 - c), device_id_type=MESH)
            cp.wait_send()
            cp.wait_recv()

    return pl.pallas_call(
        body, name=name,
        in_specs=[HBM_SPEC] * n + [SEM_SPEC, SEM_SPEC, ANY_SPEC],
        out_specs=[HBM_SPEC] * n,
        out_shape=[pltpu.HBM(s.shape, s.dtype) for s in flying],
        input_output_aliases={w: w for w in range(n)},
        compiler_params=pltpu.CompilerParams(has_side_effects=DATAFLOW),
    )(*flying, send, recv, after)


def _send_start(name, srcs, land_shapes, issue, after):
    n = len(srcs)
    deps = [] if after is None else [after]
    nd = len(deps)

    def body(*refs):
        base = 2 * n + nd
        issue(refs[base + 2:base + 2 + n], refs[base + 2 + n:base + 2 + 2 * n], refs[base], refs[base + 1])
        token = refs[base + 2 + 2 * n]
        token[...] = jnp.zeros_like(token)

    lands = [_hbm(lax.empty(shape, s.dtype)) for shape, s in zip(land_shapes, srcs)]
    outs = pl.pallas_call(
        body, name=name,
        in_specs=[HBM_SPEC] * (2 * n) + [ANY_SPEC] * nd,
        out_specs=(SEM_SPEC, SEM_SPEC, *([HBM_SPEC] * (2 * n)), pl.BlockSpec(memory_space=pltpu.VMEM)),
        out_shape=(pltpu.SemaphoreType.DMA((n,)), pltpu.SemaphoreType.DMA((n,)),
                   *[pltpu.HBM(s.shape, s.dtype) for s in srcs], *[pltpu.HBM(l.shape, l.dtype) for l in lands],
                   jax.ShapeDtypeStruct((8, 128), F32)),
        input_output_aliases={w: w + 2 for w in range(2 * n)},
        compiler_params=pltpu.CompilerParams(has_side_effects=DATAFLOW),
    )(*[_hbm(s) for s in srcs], *lands, *deps)
    return outs[0], outs[1], list(outs[2:2 + n]), list(outs[2 + n:2 + 2 * n]), outs[2 + 2 * n]


def _send_wait(name, send, recv, srcs, lands, after, landed):
    n = len(srcs)

    def body(*refs):
        send_ref, recv_ref = refs[2 * n], refs[2 * n + 1]
        q_refs = refs[3 * n + 3:]
        x, y, c, _, _ = _place()
        for w in range(n):
            cp = pltpu.make_async_remote_copy(src_ref=landed(q_refs[w]), dst_ref=landed(q_refs[w]), send_sem=send_ref.at[w],
                                              recv_sem=recv_ref.at[w], device_id=(x, y, 1 - c), device_id_type=MESH)
            cp.wait_send()
            cp.wait_recv()

    outs = pl.pallas_call(
        body, name=name,
        in_specs=[HBM_SPEC] * (2 * n) + [SEM_SPEC, SEM_SPEC, ANY_SPEC],
        out_specs=[HBM_SPEC] * (2 * n),
        out_shape=[pltpu.HBM(a.shape, a.dtype) for a in list(srcs) + list(lands)],
        input_output_aliases={w: w for w in range(2 * n)},
        compiler_params=pltpu.CompilerParams(has_side_effects=DATAFLOW),
    )(*srcs, *lands, send, recv, after)
    return list(outs[:n]), list(outs[n:])


def exchange_start(name, grads, after):
    def issue(d_refs, r_refs, send, recv):
        x, y, c, _, _ = _place()
        for w in range(len(grads)):
            half = grads[w].shape[1] // 2
            pltpu.make_async_remote_copy(
                src_ref=d_refs[w].at[pl.ds(0, N_CHIPS), pl.ds((1 - c) * half, half)], dst_ref=r_refs[w],
                send_sem=send.at[w], recv_sem=recv.at[w], device_id=(x, y, 1 - c), device_id_type=MESH).start()

    return _send_start(name, grads, [(N_CHIPS, g.shape[1] // 2, g.shape[2]) for g in grads], issue, after)


def exchange_wait(name, send, recv, grads, lands, after):
    return _send_wait(name, send, recv, grads, lands, after, lambda r: r)


def scatter_start(name, parts):
    def issue(p_refs, q_refs, send, recv):
        x, y, c, jme, others = _place()
        for w in range(len(parts)):
            for (px, py) in others:
                pltpu.make_async_remote_copy(
                    src_ref=p_refs[w].at[2 * px + py], dst_ref=q_refs[w].at[jme], send_sem=send.at[w], recv_sem=recv.at[w],
                    device_id=(px, py, c), device_id_type=MESH).start()

    return _send_start(name, parts, [p.shape for p in parts], issue, None)


def scatter_wait(name, send, recv, parts, lands, after):
    return _send_wait(name, send, recv, parts, lands, after, lambda r: r.at[pl.ds(0, 3)])


def allreduce_small(v):
    R = v.shape[0]

    def body(v_ref, o_ref, sib_ref, mine_ref, all_ref, d_send, d_recv, i_send, i_recv):
        x, y, c, jme, others = _place()
        swap = pltpu.make_async_remote_copy(src_ref=v_ref, dst_ref=sib_ref, send_sem=d_send, recv_sem=d_recv,
                                            device_id=(x, y, 1 - c), device_id_type=MESH)
        swap.start()
        swap.wait()
        mine_ref[...] = v_ref[...] + sib_ref[...]
        for (px, py) in others:
            pltpu.make_async_remote_copy(src_ref=mine_ref, dst_ref=all_ref.at[jme], send_sem=i_send, recv_sem=i_recv,
                                         device_id=(px, py, c), device_id_type=MESH).start()
        three = all_ref.at[pl.ds(0, 3)]
        wait3 = pltpu.make_async_remote_copy(src_ref=three, dst_ref=three, send_sem=i_send, recv_sem=i_recv,
                                             device_id=(x, y, c), device_id_type=MESH)
        wait3.wait_recv()
        wait3.wait_send()
        all_ref[jme] = mine_ref[...]
        o_ref[...] = ((all_ref[0] + all_ref[1]) + all_ref[2]) + all_ref[3]

    vm = pl.BlockSpec(memory_space=pltpu.VMEM)
    return pl.pallas_call(
        body, name="allreduce_small", in_specs=[vm], out_specs=vm,
        out_shape=jax.ShapeDtypeStruct(v.shape, F32),
        scratch_shapes=[pltpu.VMEM((R, 128), F32), pltpu.VMEM((R, 128), F32), pltpu.VMEM((N_CHIPS, R, 128), F32),
                        pltpu.SemaphoreType.DMA, pltpu.SemaphoreType.DMA, pltpu.SemaphoreType.DMA, pltpu.SemaphoreType.DMA],
        compiler_params=pltpu.CompilerParams(vmem_limit_bytes=VMEM_LIMIT_BYTES),
    )(v)


def _row_tile(rows, width, mult=16, cap=3 << 20):
    best = rows
    for t in range(mult, rows + 1, mult):
        if rows % t == 0 and t * width * 4 <= cap:
            best = t
    return best


def add_sibling(name, mine, theirs, place):
    _, kh, ns = theirs.shape
    tr = _row_tile(kh, ns)
    nblk = kh // tr

    def body(place_ref, a_ref, b_ref, o_ref):
        o_ref[...] = (a_ref[...].astype(F32) + b_ref[...].astype(F32)).astype(BF16)

    return pl.pallas_call(
        body, name=name,
        grid_spec=pltpu.PrefetchScalarGridSpec(
            num_scalar_prefetch=1, grid=(N_CHIPS, nblk),
            in_specs=[pl.BlockSpec((None, tr, ns), lambda j, i, s: (j, s[1] * nblk + i, 0)),
                      pl.BlockSpec((None, tr, ns), lambda j, i, s: (j, i, 0))],
            out_specs=pl.BlockSpec((None, tr, ns), lambda j, i, s: (j, i, 0))),
        out_shape=jax.ShapeDtypeStruct(theirs.shape, BF16),
        compiler_params=_params(("parallel", "parallel")),
    )(place, mine, theirs)


def add_chips(name, q, p, place):
    _, kh, ns = q.shape
    tr = _row_tile(kh, ns)
    nblk = kh // tr

    def body(place_ref, *refs):
        q_refs, own_ref, o_ref = refs[:N_CHIPS], refs[N_CHIPS], refs[-1]
        jme = place_ref[0]
        tot = None
        for j in range(N_CHIPS):
            v = jnp.where(jme == j, own_ref[...], q_refs[j][...]).astype(F32)
            tot = v if tot is None else tot + v
        o_ref[...] = tot

    def q_ix(j):
        return lambda i, s: (jnp.where(s[0] == j, (j + 1) % N_CHIPS, j), i, 0)

    in_specs = [pl.BlockSpec((None, tr, ns), q_ix(j)) for j in range(N_CHIPS)]
    in_specs.append(pl.BlockSpec((None, tr, ns), lambda i, s: (s[0], i, 0)))
    return pl.pallas_call(
        body, name=name,
        grid_spec=pltpu.PrefetchScalarGridSpec(
            num_scalar_prefetch=1, grid=(nblk,), in_specs=in_specs,
            out_specs=pl.BlockSpec((tr, ns), lambda i, s: (s[1] * nblk + i, 0))),
        out_shape=jax.ShapeDtypeStruct((2 * kh, ns), F32),
        compiler_params=_params(("parallel",)),
    )(place, q, q, q, q, p)


ADAM_LR, ADAM_B1, ADAM_B2, ADAM_EPS, ADAM_WD, ADAM_STEP = 0.001, 0.9, 0.999, 1e-08, 0.01, 10


def adamw(name, w, g, m, v, layer, prev=None, after=None):
    _, k, ns = w.shape
    nsp = g.shape[1]
    tr = _row_tile(k, nsp, mult=8, cap=3 << 20)

    def body(w_ref, g_ref, m_ref, v_ref, *rest):
        go_ref, d_ref, mo_ref, vo_ref = rest[-4:]
        gv = g_ref[:, :ns] if nsp != ns else g_ref[...]
        mn = ADAM_B1 * m_ref[...] + (1.0 - ADAM_B1) * gv
        vn = ADAM_B2 * v_ref[...] + (1.0 - ADAM_B2) * (gv * gv)
        m_hat = mn / (1.0 - ADAM_B1 ** ADAM_STEP)
        v_hat = vn / (1.0 - ADAM_B2 ** ADAM_STEP)
        go_ref[...] = gv
        d_ref[...] = -ADAM_LR * (m_hat / (jnp.sqrt(v_hat) + ADAM_EPS) + ADAM_WD * w_ref[...])
        mo_ref[...] = mn
        vo_ref[...] = vn

    blk = pl.BlockSpec((None, tr, ns), lambda i: (layer, i, 0))
    gblk = pl.BlockSpec((tr, nsp), lambda i: (i, 0))
    args, in_specs, aliases = [w, g, m, v], [blk, gblk, blk, blk], {}
    if prev is not None:
        args += list(prev)
        in_specs += [pl.BlockSpec(memory_space=pl.ANY)] * 4
        aliases = {4 + i: i for i in range(4)}
    if after is not None:
        args.append(after)
        in_specs.append(pl.BlockSpec(memory_space=pl.ANY))
    return pl.pallas_call(
        body, name=name, grid=(k // tr,), in_specs=in_specs, out_specs=[blk] * 4,
        out_shape=[jax.ShapeDtypeStruct(w.shape, F32)] * 4, input_output_aliases=aliases,
        compiler_params=_params(("parallel",)),
    )(*args)


WEIGHTS = ("ffn_a_norm", "ffn_a_w1", "ffn_a_w3", "ffn_a_w2", "mix_norm", "w_in", "q_a_norm", "w_uq", "kv_a_norm", "w_ukv",
           "q_norm", "k_norm", "gm_v_norm", "gm_ws", "gm_bs", "attn_out_norm", "gm_out_norm", "w_out", "ffn_b_norm",
           "ffn_b_w1", "ffn_b_w3", "ffn_b_w2", "ple_gate_norm", "w_ple_gate", "w_ple", "ple_norm")
_FF = FF_PAD - FF_SHARD
BIG = {"ffn_a_w1": _FF, "ffn_a_w3": _FF, "ffn_a_w2": _FF, "ffn_b_w1": _FF, "ffn_b_w3": _FF, "ffn_b_w2": _FF,
       "w_in": IN_SHARD_PAD - IN_SHARD, "w_uq": 0, "w_ukv": 0, "w_ple": 0, "w_out": 0, "w_ple_gate": 0}
TRANSPOSED = ("ffn_a_w1", "ffn_a_w3", "ffn_b_w1", "ffn_b_w3", "w_in")
SMALL = tuple(n for n in WEIGHTS if n not in BIG)
PACK = 1024


def _pack_small(d):
    parts = []
    for n in SMALL:
        flat = d[n].reshape(-1)
        parts.append(jnp.pad(flat, (0, (-flat.shape[0]) % PACK)))
    return jnp.concatenate(parts).reshape(-1, 128)


def _unpack_small(buf, like):
    flat = buf.reshape(-1)
    out, pos = {}, 0
    for n in SMALL:
        size = math.prod(like[n].shape)
        out[n] = flat[pos:pos + size].reshape(like[n].shape)
        pos += size + (-size) % PACK
    return out


def kernel(*args):
    names = (("x", "p", "positions") + WEIGHTS + ("loss_target",) + tuple("m_" + n for n in WEIGHTS)
             + tuple("v_" + n for n in WEIGHTS))
    a = dict(zip(names, args, strict=True))
    x, p, positions, target = a["x"][0], a["p"][:, 0], a["positions"][0], a["loss_target"][0]
    for n in TRANSPOSED:
        for pre in ("", "m_", "v_"):
            a[pre + n] = jnp.swapaxes(a[pre + n], 1, 2)

    place = jnp.stack([2 * lax.axis_index("x") + lax.axis_index("y"), lax.axis_index("c")]).astype(jnp.int32)
    small = {n: a[n] for n in SMALL}
    tabs = rope_tables(positions)
    first = ("ffn_a_w1", "ffn_a_w3", "ffn_a_w2")
    rest = tuple(n for n in BIG if n not in first)
    prep = lambda n, l, after: prep_shard(f"prep_{n}_{l}", a[n], l, BIG[n], place, after)

    def finish_gather(tag, started, after):
        send, recv, flying, _ = started
        arrived = gather_wait(f"gather_{tag}_wait", send, recv, flying, after)
        send, recv, flying, token = forward_start(f"forward_{tag}_start", arrived)
        return gather_wait(f"forward_{tag}_wait", send, recv, flying, token)

    ga = gather_start("gather_l0a_start", [prep(n, 0, None) for n in first], None)
    gb = gather_start("gather_l0b_start", [prep(n, 0, ga[3]) for n in rest], None)
    slots1 = []
    for n in BIG:
        slots1.append(prep(n, 1, slots1[-1] if slots1 else gb[3]))
    G0 = dict(zip(first, finish_gather("l0a", ga, slots1[-1])))
    later = {}

    def after_first_ffn(h1, Gl, small_):
        later["G0"] = {**Gl, **dict(zip(rest, finish_gather("l0b", gb, h1)))}
        later["g1"] = gather_start("gather_l1_start", slots1, later["G0"]["w_uq"])
        return later["G0"], {**small_, "mix_norm": small_["mix_norm"] + later["g1"][3][0, 0]}

    def before_ple(h3):
        send, recv, flying, _ = later["g1"]
        later["f1"] = forward_start("forward_l1_start", gather_wait("gather_l1_wait", send, recv, flying, h3))
        return later["f1"][3]

    h, saved0 = layer_fwd(0, x, p[0], G0, small, tabs, after_first_ffn, before_ple)
    G0 = later["G0"]
    G1 = dict(zip(BIG, gather_wait("forward_l1_wait", *later["f1"][:3], h)))
    h, saved1 = layer_fwd(1, h, p[1], G1, small, tabs)
    dh, loss = loss_grad(h, target)
    loss = lax.psum(loss, ("x", "y", "c"))

    groups = {"l1": tuple(BIG),
              "l0a": ("w_ple_gate", "w_ple", "ffn_b_w1", "ffn_b_w3", "ffn_b_w2"),
              "l0b": ("w_in", "w_uq", "w_ukv", "w_out"),
              "l0c": ("ffn_a_w1", "ffn_a_w3", "ffn_a_w2")}
    crossing, started = [], {}

    def begin(tag, gl, after):
        ex = exchange_start(f"exchange_{tag}_start", [gl[n] for n in groups[tag]], after)
        crossing.append((tag, ex))
        return ex[4]

    def advance(after):
        tag, (send, recv, mine, lands, _) = crossing.pop()
        mine, theirs = exchange_wait(f"exchange_{tag}_wait", send, recv, mine, lands, after)
        parts = [add_sibling(f"add_sibling_{n}_{tag}", d, r, place) for n, d, r in zip(groups[tag], mine, theirs)]
        started[tag] = scatter_start(f"scatter_{tag}_start", parts)
        return started[tag][4]

    def finish(tag, after):
        send, recv, parts, lands, _ = started[tag]
        parts, slabs = scatter_wait(f"scatter_{tag}_wait", send, recv, parts, lands, after)
        halves = [add_chips(f"add_chips_{n}_{tag}", q, pt, place) for n, q, pt in zip(groups[tag], slabs, parts)]
        return dict(zip(groups[tag], share_halves(f"share_{tag}", halves)))

    def update(names, full, layer, prev, after):
        outs = {}
        for n in names:
            outs[n] = adamw(f"adamw_{n}_{layer}", a[n], full[n], a["m_" + n], a["v_" + n], layer, prev and prev[n], after)
            after = outs[n][1]
        return outs, after

    grads = [None, None]
    dh, grads[1] = layer_bwd(1, dh, p[1], G1, small, tabs, saved1)
    token = begin("l1", grads[1], None)
    w0 = {**saved0[0], "ple_gate_norm": saved0[0]["ple_gate_norm"] + token[0, 0]}
    hooks = {"ffn_b": lambda gl, dh_: advance(dh_),
             "mixer": lambda gl, dh_: begin("l0a", gl, None),
             "ffn_a": lambda gl, dh_: begin("l0b", gl, advance(dh_))}
    gx, grads[0] = layer_bwd(0, dh, p[0], G0, small, tabs, (w0,) + saved0[1:], hooks)
    token = begin("l0c", grads[0], advance(gx))
    full1 = finish("l1", token)
    outs1, behind = update(BIG, full1, 1, None, advance(full1[tuple(BIG)[-1]]))
    full0 = {}
    for tag in ("l0a", "l0b", "l0c"):
        full0.update(finish(tag, behind))
    outs0, _ = update(BIG, full0, 0, outs1, None)

    out_g, out_d, out_m, out_v = {}, {}, {}, {}
    for n in BIG:
        outs = [jnp.swapaxes(o, 1, 2) for o in outs0[n]] if n in TRANSPOSED else outs0[n]
        out_g[n], out_d[n], out_m[n], out_v[n] = outs

    gs = allreduce_small(_pack_small({n: jnp.stack([grads[0][n], grads[1][n]]) for n in SMALL}))
    rows = gs.shape[0] // 2
    packed = [_pack_small(d).reshape(2, rows, 128) for d in
              (small, {n: a["m_" + n] for n in SMALL}, {n: a["v_" + n] for n in SMALL})]
    gs = gs.reshape(2, rows, 128)
    sm = adamw("adamw_small_0", packed[0], gs[0], packed[1], packed[2], 0)
    sm = adamw("adamw_small_1", packed[0], gs[1], packed[1], packed[2], 1, sm)
    for dst, buf in zip((out_g, out_d, out_m, out_v), sm):
        dst.update(_unpack_small(buf, small))

    return (loss, gx[None], *[out_g[n] for n in WEIGHTS], *[out_d[n] for n in WEIGHTS],
            *[out_m[n] for n in WEIGHTS], *[out_v[n] for n in WEIGHTS])
```

```python
import math

import jax
import jax.numpy as jnp
from jax import lax
from jax.experimental import pallas as pl
from jax.experimental.pallas import tpu as pltpu

F32 = jnp.float32
BF16 = jnp.bfloat16

D_MODEL = 2048
D_FF = 5504
N_CHIPS = 4
FF_SHARD = D_FF // N_CHIPS
FF_PAD = 1408
FF_P = N_CHIPS * FF_PAD
HEADS = 8
QK_NOPE = 128
QK_ROPE = 64
QK_DIM = 192
HEAD_PAD = 256
V_DIM = 128
Q_RANK = 512
KV_RANK = 256
ATTN_W = 1024
GM_W = 1024
GROUPS = 8
CHUNK = 128
PLE_DIM = 256
IN_P = 3072
IN_SHARD = 720
IN_SHARD_PAD = 736
EPS = 1e-6
ROPE_BASE = 10000.0
ATTN_SCALE = QK_DIM ** -0.5
SUB_BLOCKS = 1
VMEM_LIMIT_BYTES = 56 * 1024 * 1024


def _params(sem):
    return pltpu.CompilerParams(dimension_semantics=sem, vmem_limit_bytes=VMEM_LIMIT_BYTES)


def _bf(x):
    return x if x.dtype == BF16 else x.astype(BF16)


def _sigmoid(x):
    return 1.0 / (1.0 + jnp.exp(-x))


_GELU_C = math.sqrt(2.0 / math.pi)


def _gelu(x):
    t = jnp.tanh(_GELU_C * (x + 0.044715 * x * x * x))
    return 0.5 * x * (1.0 + t)


def _gelu_grad(x):
    t = jnp.tanh(_GELU_C * (x + 0.044715 * x * x * x))
    return 0.5 * (1.0 + t) + 0.5 * x * (1.0 - t * t) * _GELU_C * (1.0 + 3 * 0.044715 * x * x)


def op_a(a, tm, tk):
    return (a, (tm, tk), lambda i, j, k: (i, k), 1)


def op_at(a, tm, tk):
    return (a, (tk, tm), lambda i, j, k: (k, i), 0)


def op_b(b, tk, tn):
    return (b, (tk, tn), lambda i, j, k: (k, j), 0)


def op_bt(b, tk, tn):
    return (b, (tn, tk), lambda i, j, k: (j, k), 1)


def op_b_cols(g, pre, tk, tn):
    nb = g.shape[-1] // tn
    none = (None,) * (1 + len(pre))
    return (g, none + (tk, tn), lambda i, j, k: (j // nb,) + tuple(pre) + (k, j % nb), 0)


def op_b_rows(g, pre, tk, tn, koff=0):
    nb = g.shape[-2] // tk
    none = (None,) * (1 + len(pre))
    return (g, none + (tk, tn), lambda i, j, k: ((k + koff) // nb,) + tuple(pre) + ((k + koff) % nb, j), 0)


def op_b_rows_t(g, pre, tk, tn):
    nb = g.shape[-2] // tn
    none = (None,) * (1 + len(pre))
    return (g, none + (tn, tk), lambda i, j, k: (j // nb,) + tuple(pre) + (j % nb, k), 1)


def tile_mn(x, tm, tn):
    return (x, (tm, tn), lambda i, j: (i, j))


def out_mn(M, N, tm, tn, dtype):
    return (jax.ShapeDtypeStruct((M, N), dtype), (tm, tn), lambda i, j: (i, j))


def out_cols(M, ns, tm, tn, dtype):
    nb = ns // tn
    return (jax.ShapeDtypeStruct((N_CHIPS, M, ns), dtype), (None, tm, tn), lambda i, j: (j // nb, i, j % nb))


def matmul(name, grid_mnk, a_ops, b_ops, terms, n_acc, extras, outs, epilogue, acc_tile, n_outer=False, after=None):
    gm, gn, gk = grid_mnk
    na, nb, nx, no = len(a_ops), len(b_ops), len(extras), len(outs)
    nd = 0 if after is None else 1

    def body(*refs):
        a_refs, b_refs = refs[:na], refs[na:na + nb]
        x_refs = refs[na + nb:na + nb + nx]
        o_refs = refs[na + nb + nx + nd:na + nb + nx + nd + no]
        acc_refs = refs[na + nb + nx + nd + no:]
        k = pl.program_id(2)

        @pl.when(k == 0)
        def _():
            for acc in acc_refs:
                acc[...] = jnp.zeros_like(acc)

        for ai, bi, ci in terms:
            dims = (((a_ops[ai][3],), (b_ops[bi][3],)), ((), ()))
            acc_refs[ci][...] += lax.dot_general(_bf(a_refs[ai][...]), _bf(b_refs[bi][...]), dims,
                                                 preferred_element_type=F32)

        @pl.when(k == gk - 1)
        def _():
            res = epilogue([acc[...] for acc in acc_refs], [x[...] for x in x_refs])
            for o, v in zip(o_refs, res):
                o[...] = v.astype(o.dtype)

    if n_outer:
        grid = (gn, gm, gk)

        def ix3(f):
            return lambda j, i, k: f(i, j, k)

        def ix2(f):
            return lambda j, i, k: f(i, j)
    else:
        grid = (gm, gn, gk)

        def ix3(f):
            return lambda i, j, k: f(i, j, k)

        def ix2(f):
            return lambda i, j, k: f(i, j)

    in_specs = [pl.BlockSpec(blk, ix3(f)) for (_, blk, f, _) in list(a_ops) + list(b_ops)]
    in_specs += [pl.BlockSpec(blk, ix2(f)) for (_, blk, f) in extras]
    in_specs += [pl.BlockSpec(memory_space=pl.ANY)] * nd
    out_specs = [pl.BlockSpec(blk, ix2(f)) for (_, blk, f) in outs]
    return pl.pallas_call(
        body,
        name=name,
        grid=grid,
        in_specs=in_specs,
        out_specs=out_specs,
        out_shape=[s for (s, _, _) in outs],
        scratch_shapes=[pltpu.VMEM(acc_tile, F32) for _ in range(n_acc)],
        compiler_params=_params(("parallel", "parallel", "arbitrary")),
    )(*[o[0] for o in a_ops], *[o[0] for o in b_ops], *[x[0] for x in extras], *([after] * nd))


def _acc0(accs, xs):
    return (accs[0],)


def mm_simple(name, a, b_op_fn, M, N, K, tm, tn, tk, out_dtype=F32, a_t=False, extras=(), epilogue=_acc0, outs=None,
              after=None):
    a_op = op_at(a, tm, tk) if a_t else op_a(a, tm, tk)
    outs = outs or [out_mn(M, N, tm, tn, out_dtype)]
    return matmul(name, (M // tm, N // tn, K // tk), [a_op], [b_op_fn(tk, tn)], [(0, 0, 0)], 1,
                  list(extras), outs, epilogue, (tm, tn), after=after)


def rms_fwd(name, x, g, width, col_blk=0, tm=256, out_dtype=BF16):
    T = x.shape[0]

    def body(x_ref, g_ref, o_ref):
        xv = x_ref[...].astype(F32)
        r = lax.rsqrt(jnp.mean(xv * xv, axis=-1, keepdims=True) + EPS)
        o_ref[...] = (xv * r * g_ref[...]).astype(o_ref.dtype)

    return pl.pallas_call(
        body, name=name, grid=(T // tm,),
        in_specs=[pl.BlockSpec((tm, width), lambda i: (i, col_blk)), pl.BlockSpec((1, width), lambda i: (0, 0))],
        out_specs=pl.BlockSpec((tm, width), lambda i: (i, 0)),
        out_shape=jax.ShapeDtypeStruct((T, width), out_dtype),
        compiler_params=_params(("parallel",)),
    )(x, g.reshape(1, width))


def rms_bwd(name, x, g, dn, width, col_blk=0, dres=None, tm=256, with_delta=False, bf16_copy=False):
    T = x.shape[0]
    has_res = dres is not None

    def body(*refs):
        x_ref, g_ref, dn_ref = refs[:3]
        pos = 3
        res_ref = None
        if has_res:
            res_ref = refs[pos]
            pos += 1
        dx_ref, dg_ref = refs[pos], refs[pos + 1]
        delta_ref = refs[pos + 2] if with_delta else None
        lo_ref = refs[-1] if bf16_copy else None
        i = pl.program_id(0)
        xv = x_ref[...].astype(F32)
        r = lax.rsqrt(jnp.mean(xv * xv, axis=-1, keepdims=True) + EPS)
        xh = xv * r
        d = dn_ref[...].astype(F32)
        gd = d * g_ref[...]
        dx = r * (gd - xh * jnp.mean(gd * xh, axis=-1, keepdims=True))
        if has_res:
            dx = dx + res_ref[...]
        dx_ref[...] = dx.astype(dx_ref.dtype)
        if bf16_copy:
            lo_ref[...] = dx.astype(BF16)
        part = jnp.sum(d * xh, axis=0, keepdims=True)

        @pl.when(i == 0)
        def _():
            dg_ref[...] = part

        @pl.when(i > 0)
        def _():
            dg_ref[...] += part

        if with_delta:
            for h in range(width // 128):
                sl = slice(h * 128, (h + 1) * 128)
                s = jnp.sum(dx[:, sl] * xv[:, sl], axis=-1, keepdims=True)
                delta_ref[:, sl] = jnp.broadcast_to(s, (tm, 128))

    in_specs = [pl.BlockSpec((tm, width), lambda i: (i, col_blk)), pl.BlockSpec((1, width), lambda i: (0, 0)),
                pl.BlockSpec((tm, width), lambda i: (i, 0))]
    args = [x, g.reshape(1, width), dn]
    if has_res:
        in_specs.append(pl.BlockSpec((tm, width), lambda i: (i, 0)))
        args.append(dres)
    out_specs = [pl.BlockSpec((tm, width), lambda i: (i, 0)), pl.BlockSpec((1, width), lambda i: (0, 0))]
    out_shape = [jax.ShapeDtypeStruct((T, width), F32), jax.ShapeDtypeStruct((1, width), F32)]
    if with_delta:
        out_specs.append(pl.BlockSpec((tm, width), lambda i: (i, 0)))
        out_shape.append(jax.ShapeDtypeStruct((T, width), F32))
    if bf16_copy:
        out_specs.append(pl.BlockSpec((tm, width), lambda i: (i, 0)))
        out_shape.append(jax.ShapeDtypeStruct((T, width), BF16))
    return pl.pallas_call(
        body, name=name, grid=(T // tm,), in_specs=in_specs, out_specs=out_specs, out_shape=out_shape,
        compiler_params=_params(("arbitrary",)),
    )(*args)


def ffn_fwd(tag, h, g, w1g, w3g, w2g, pre):
    T = h.shape[0]
    n = rms_fwd(f"{tag}_rms", h, g, D_MODEL)
    tm, tn = 512, FF_PAD

    def up_epi(accs, xs):
        a1, a3 = accs
        return a1, a3, a1 * _sigmoid(a1) * a3

    a1, a3, s = matmul(
        f"{tag}_up", (T // tm, FF_P // tn, 1),
        [op_a(n, tm, D_MODEL)], [op_b_rows_t(w1g, pre, D_MODEL, tn), op_b_rows_t(w3g, pre, D_MODEL, tn)],
        [(0, 0, 0), (0, 1, 1)], 2, [],
        [out_mn(T, FF_P, tm, tn, BF16)] * 3, up_epi, (tm, tn), n_outer=True)

    tm2, tn2 = 1024, 1024
    (h_out,) = matmul(
        f"{tag}_down", (T // tm2, D_MODEL // tn2, N_CHIPS),
        [op_a(s, tm2, FF_PAD)], [op_b_rows(w2g, pre, FF_PAD, tn2)],
        [(0, 0, 0)], 1, [tile_mn(h, tm2, tn2)],
        [out_mn(T, D_MODEL, tm2, tn2, F32)], lambda accs, xs: (xs[0] + 0.5 * accs[0],), (tm2, tn2))
    return h_out, (n, a1, a3, s)


def ffn_bwd(tag, dh_out, dh_bf, h, g, res, w1g, w3g, w2g, pre, after=None, before_dw=None):
    n, a1, a3, s = res
    T = h.shape[0]
    tm, tn = 512, FF_PAD

    def act_epi(accs, xs):
        ds = 0.5 * accs[0]
        x1, x3 = xs[0].astype(F32), xs[1].astype(F32)
        sg = _sigmoid(x1)
        silu = x1 * sg
        return ds * x3 * (sg + silu * (1.0 - sg)), ds * silu

    da1, da3 = matmul(
        f"{tag}_dact", (T // tm, FF_P // tn, 1),
        [op_a(dh_bf, tm, D_MODEL)], [op_b_rows_t(w2g, pre, D_MODEL, tn)],
        [(0, 0, 0)], 1, [tile_mn(a1, tm, tn), tile_mn(a3, tm, tn)],
        [out_mn(T, FF_P, tm, tn, BF16)] * 2, act_epi, (tm, tn), n_outer=True, after=after)

    tm2, tn2 = 1024, 1024
    (dn,) = matmul(
        f"{tag}_dn", (T // tm2, D_MODEL // tn2, N_CHIPS),
        [op_a(da1, tm2, FF_PAD), op_a(da3, tm2, FF_PAD)],
        [op_b_rows(w1g, pre, FF_PAD, tn2), op_b_rows(w3g, pre, FF_PAD, tn2)],
        [(0, 0, 0), (1, 1, 0)], 1, [], [out_mn(T, D_MODEL, tm2, tn2, F32)], _acc0, (tm2, tn2))
    dh, dg, dh_lo = rms_bwd(f"{tag}_rms_bwd", h, g, dn, D_MODEL, dres=dh_out, bf16_copy=True)
    if before_dw is not None:
        after = before_dw(dh)

    tk = 1024

    def dw_t(nm, left, right, scale):
        (dw,) = matmul(
            f"{tag}_{nm}", (FF_P // FF_PAD, D_MODEL // 1024, T // tk),
            [op_at(left, FF_PAD, tk)], [op_b(right, tk, 1024)],
            [(0, 0, 0)], 1, [], [out_mn(FF_P, D_MODEL, FF_PAD, 1024, BF16)],
            lambda accs, xs: (scale * accs[0],), (FF_PAD, 1024), after=after)
        return dw

    dw2 = dw_t("dw2", s, dh_bf, 0.5)
    dw1 = dw_t("dw1", da1, n, 1.0)
    dw3 = dw_t("dw3", da3, n, 1.0)
    return dh, dh_lo, dg, dw1, dw3, dw2


def rope_tables(positions):
    inv_freq = ROPE_BASE ** (-jnp.arange(0, QK_ROPE, 2, dtype=F32) / QK_ROPE)
    ang = positions.astype(F32)[:, None] * inv_freq
    cos, sin = jnp.cos(ang), jnp.sin(ang)
    T = positions.shape[0]
    one, zero = jnp.ones((T, QK_NOPE), F32), jnp.zeros((T, 64), F32)
    z32, z128 = jnp.zeros((T, 32), F32), jnp.zeros((T, QK_NOPE), F32)
    c = jnp.concatenate([one, cos, cos, zero], axis=1)
    s1 = jnp.concatenate([z128, -sin, z32, zero], axis=1)
    s2 = jnp.concatenate([z128, z32, sin, zero], axis=1)
    return c, s1, s2


def _rope(y, c, s1, s2):
    return y * c + pltpu.roll(y, HEAD_PAD - 32, 1) * s1 + pltpu.roll(y, 32, 1) * s2


def _rope_t(d, c, s1, s2):
    return d * c + pltpu.roll(d * s1, 32, 1) + pltpu.roll(d * s2, HEAD_PAD - 32, 1)


def _head_norm(x):
    r = lax.rsqrt(jnp.sum(x * x, axis=-1, keepdims=True) * (1.0 / QK_DIM) + EPS)
    return x * r, r


def qk_prep_fwd(tag, q_raw, kk_raw, z_p, gq, gk, tabs, tm=256):
    T = q_raw.shape[0]
    c, s1, s2 = tabs

    def body(q_ref, k_ref, kr_ref, gq_ref, gk_ref, c_ref, s1_ref, s2_ref, qo_ref, ko_ref):
        cv, s1v, s2v = c_ref[...], s1_ref[...], s2_ref[...]
        kr = kr_ref[...]
        for h in range(HEADS):
            sl = slice(h * HEAD_PAD, (h + 1) * HEAD_PAD)
            xh, _ = _head_norm(q_ref[:, sl])
            qo_ref[:, sl] = (_rope(xh * gq_ref[...], cv, s1v, s2v) * ATTN_SCALE).astype(BF16)
            xh, _ = _head_norm(k_ref[:, sl] + kr)
            ko_ref[:, sl] = _rope(xh * gk_ref[...], cv, s1v, s2v).astype(BF16)

    row = lambda i: (i, 0)
    full = pl.BlockSpec((tm, HEADS * HEAD_PAD), row)
    tab = pl.BlockSpec((tm, HEAD_PAD), row)
    vec = pl.BlockSpec((1, HEAD_PAD), lambda i: (0, 0))
    return pl.pallas_call(
        body, name=f"{tag}_qk_prep", grid=(T // tm,),
        in_specs=[full, full, pl.BlockSpec((tm, HEAD_PAD), lambda i: (i, 3)), vec, vec, tab, tab, tab],
        out_specs=[full, full],
        out_shape=[jax.ShapeDtypeStruct((T, HEADS * HEAD_PAD), BF16)] * 2,
        compiler_params=_params(("parallel",)),
    )(q_raw, kk_raw, z_p, gq, gk, c, s1, s2)


def qk_prep_bwd(tag, dq_full, dk_full, q_raw, kk_raw, z_p, gq, gk, tabs, tm=256):
    T = q_raw.shape[0]
    c, s1, s2 = tabs

    def body(dq_ref, dk_ref, q_ref, k_ref, kr_ref, gq_ref, gk_ref, c_ref, s1_ref, s2_ref,
             dqr_ref, dkr_ref, dz_ref, dgq_ref, dgk_ref):
        i = pl.program_id(0)
        cv, s1v, s2v = c_ref[...], s1_ref[...], s2_ref[...]
        kr = kr_ref[...]
        lane = lax.broadcasted_iota(jnp.int32, (tm, HEAD_PAD), 1)
        slot = ((lane >= QK_NOPE) & (lane < QK_DIM)).astype(F32)

        def one(x, g, d):
            xh, r = _head_norm(x)
            dy = _rope_t(d, cv, s1v, s2v)
            gd = dy * g
            dx = r * (gd - xh * (jnp.sum(gd * xh, axis=-1, keepdims=True) * (1.0 / QK_DIM)))
            return dx, jnp.sum(dy * xh, axis=0, keepdims=True)

        dgq = jnp.zeros((1, HEAD_PAD), F32)
        dgk = jnp.zeros((1, HEAD_PAD), F32)
        dz = jnp.zeros((tm, HEAD_PAD), F32)
        for h in range(HEADS):
            sl = slice(h * HEAD_PAD, (h + 1) * HEAD_PAD)
            dx, dg = one(q_ref[:, sl], gq_ref[...], dq_ref[:, sl].astype(F32) * ATTN_SCALE)
            dqr_ref[:, sl] = dx
            dgq = dgq + dg
            dx, dg = one(k_ref[:, sl] + kr, gk_ref[...], dk_ref[:, sl].astype(F32))
            dkr_ref[:, sl] = dx
            dgk = dgk + dg
            dz = dz + dx
        dz_ref[...] = dz * slot

        @pl.when(i == 0)
        def _():
            dgq_ref[...] = dgq
            dgk_ref[...] = dgk

        @pl.when(i > 0)
        def _():
            dgq_ref[...] += dgq
            dgk_ref[...] += dgk

    row = lambda i: (i, 0)
    full = pl.BlockSpec((tm, HEADS * HEAD_PAD), row)
    tab = pl.BlockSpec((tm, HEAD_PAD), row)
    vec = pl.BlockSpec((1, HEAD_PAD), lambda i: (0, 0))
    return pl.pallas_call(
        body, name=f"{tag}_qk_prep_bwd", grid=(T // tm,),
        in_specs=[full, full, full, full, pl.BlockSpec((tm, HEAD_PAD), lambda i: (i, 3)), vec, vec, tab, tab, tab],
        out_specs=[full, full, tab, vec, vec],
        out_shape=[jax.ShapeDtypeStruct((T, HEADS * HEAD_PAD), F32)] * 2
        + [jax.ShapeDtypeStruct((T, HEAD_PAD), F32)] + [jax.ShapeDtypeStruct((1, HEAD_PAD), F32)] * 2,
        compiler_params=_params(("arbitrary",)),
    )(dq_full, dk_full, q_raw, kk_raw, z_p, gq, gk, c, s1, s2)


def attn_fwd(tag, q_full, k_full, vv, blk=512):
    T = q_full.shape[0]
    nb = T // blk
    neg = float(jnp.finfo(jnp.float32).min)

    def body(q_ref, k_ref, v_ref, o_ref, lse_ref, m_ref, l_ref, acc_ref):
        i = pl.program_id(1)
        m_ref[...] = jnp.full_like(m_ref, neg)
        l_ref[...] = jnp.zeros_like(l_ref)
        acc_ref[...] = jnp.zeros_like(acc_ref)
        sub = blk // SUB_BLOCKS

        def step(j, masked):
            rows = pl.ds(pl.multiple_of(j * blk, blk), blk)
            k, v = k_ref[rows, :], v_ref[rows, :]
            for part in range(SUB_BLOCKS):
                qs = slice(part * sub, (part + 1) * sub)
                s = lax.dot_general(q_ref[qs, :], k, (((1,), (1,)), ((), ())), preferred_element_type=F32)
                if masked:
                    row = lax.broadcasted_iota(jnp.int32, (sub, blk), 0) + part * sub
                    col = lax.broadcasted_iota(jnp.int32, (sub, blk), 1)
                    s = jnp.where(col <= row, s, neg)
                m_prev = m_ref[qs, :]
                m_new = jnp.maximum(m_prev, jnp.max(s, axis=-1, keepdims=True))
                alpha = jnp.exp(m_prev - m_new)
                p = jnp.exp(s - m_new[:, :1])
                l_ref[qs, :] = alpha * l_ref[qs, :] + jnp.sum(p, axis=-1, keepdims=True)
                acc_ref[qs, :] = alpha * acc_ref[qs, :] + jnp.dot(p.astype(BF16), v, preferred_element_type=F32)
                m_ref[qs, :] = m_new

        def off_diagonal(j, carry):
            step(j, False)
            return carry

        lax.fori_loop(0, i, off_diagonal, 0)
        step(i, True)
        o_ref[...] = acc_ref[...] / l_ref[...]
        lse_ref[...] = m_ref[...] + jnp.log(l_ref[...])

    return pl.pallas_call(
        body, name=f"{tag}_attn_fwd", grid=(HEADS, nb),
        in_specs=[pl.BlockSpec((blk, HEAD_PAD), lambda h, i: (i, h)),
                  pl.BlockSpec((T, HEAD_PAD), lambda h, i: (0, h)), pl.BlockSpec((T, V_DIM), lambda h, i: (0, h))],
        out_specs=[pl.BlockSpec((blk, V_DIM), lambda h, i: (i, h))] * 2,
        out_shape=[jax.ShapeDtypeStruct((T, ATTN_W), F32)] * 2,
        scratch_shapes=[pltpu.VMEM((blk, V_DIM), F32)] * 3,
        compiler_params=_params(("parallel", "parallel")),
    )(q_full, k_full, vv)


def attn_bwd(tag, q_full, k_full, vv, do, lse, delta, blk=512):
    T = q_full.shape[0]
    nb = T // blk
    neg = float(jnp.finfo(jnp.float32).min)

    def body(q_ref, k_ref, v_ref, do_ref, lse_ref, dl_ref, dq_ref, dk_ref, dv_ref, dk_acc, dv_acc):
        j = pl.program_id(1)

        @pl.when(j == 0)
        def _():
            dq_ref[...] = jnp.zeros_like(dq_ref)

        dk_acc[...] = jnp.zeros_like(dk_acc)
        dv_acc[...] = jnp.zeros_like(dv_acc)
        k, v = k_ref[...], v_ref[...]

        sub = blk // SUB_BLOCKS

        def step(i, masked):
            for part in range(SUB_BLOCKS):
                rows = pl.ds(pl.multiple_of(i * blk + part * sub, sub), sub)
                q = q_ref[rows, :]
                s = lax.dot_general(q, k, (((1,), (1,)), ((), ())), preferred_element_type=F32)
                if masked:
                    row = lax.broadcasted_iota(jnp.int32, (sub, blk), 0) + part * sub
                    col = lax.broadcasted_iota(jnp.int32, (sub, blk), 1)
                    s = jnp.where(col <= row, s, neg)
                p = jnp.exp(s - lse_ref[rows, :1])
                dob = _bf(do_ref[rows, :])
                dv_acc[...] += lax.dot_general(p.astype(BF16), dob, (((0,), (0,)), ((), ())), preferred_element_type=F32)
                dp = lax.dot_general(dob, v, (((1,), (1,)), ((), ())), preferred_element_type=F32)
                ds = (p * (dp - dl_ref[rows, :1])).astype(BF16)
                dk_acc[...] += lax.dot_general(ds, q, (((0,), (0,)), ((), ())), preferred_element_type=F32)
                dq_ref[rows, :] += jnp.dot(ds, k, preferred_element_type=F32)

        def off_diagonal(i, carry):
            step(i, False)
            return carry

        step(j, True)
        lax.fori_loop(j + 1, nb, off_diagonal, 0)
        dk_ref[...] = dk_acc[...]
        dv_ref[...] = dv_acc[...]

    head = lambda h, j: (0, h)
    kv_ix = lambda h, j: (j, h)
    return pl.pallas_call(
        body, name=f"{tag}_attn_bwd", grid=(HEADS, nb),
        in_specs=[pl.BlockSpec((T, HEAD_PAD), head), pl.BlockSpec((blk, HEAD_PAD), kv_ix),
                  pl.BlockSpec((blk, V_DIM), kv_ix), pl.BlockSpec((T, V_DIM), head),
                  pl.BlockSpec((T, V_DIM), head), pl.BlockSpec((T, V_DIM), head)],
        out_specs=[pl.BlockSpec((T, HEAD_PAD), head),
                   pl.BlockSpec((blk, HEAD_PAD), kv_ix), pl.BlockSpec((blk, V_DIM), kv_ix)],
        out_shape=[jax.ShapeDtypeStruct((T, HEADS * HEAD_PAD), F32)] * 2 + [jax.ShapeDtypeStruct((T, ATTN_W), F32)],
        scratch_shapes=[pltpu.VMEM((blk, HEAD_PAD), F32), pltpu.VMEM((blk, V_DIM), F32)],
        compiler_params=_params(("parallel", "arbitrary")),
    )(q_full, k_full, vv, do, lse, delta)


def _gm_forward(u, v, gv, wc_ref, bb_ref, nchunk):
    ug = _gelu(u)
    vg = _gelu(v)
    rv = lax.rsqrt(jnp.mean(vg * vg, axis=-1, keepdims=True) + EPS)
    vhat = vg * rv
    vn = (vhat * gv).astype(BF16)
    gates = []
    for cidx in range(nchunk):
        rows = slice(cidx * CHUNK, (cidx + 1) * CHUNK)
        gates.append(jnp.concatenate(
            [jnp.dot(wc_ref[gidx], vn[rows, gidx * 128:(gidx + 1) * 128], preferred_element_type=F32) + bb_ref[gidx]
             for gidx in range(GROUPS)], axis=1))
    gate = jnp.concatenate(gates, axis=0)
    return ug, vhat, rv, vn, gate


def gmlp_fwd(tag, z_p, gv, gout, wc, bb, tm=256):
    T = z_p.shape[0]
    nchunk = tm // CHUNK

    def body(u_ref, v_ref, gv_ref, go_ref, wc_ref, bb_ref, o_ref):
        ug, _, _, _, gate = _gm_forward(u_ref[...], v_ref[...], gv_ref[...], wc_ref, bb_ref, nchunk)
        go = ug * gate
        ro = lax.rsqrt(jnp.mean(go * go, axis=-1, keepdims=True) + EPS)
        o_ref[...] = (go * ro * go_ref[...]).astype(BF16)

    vec = pl.BlockSpec((1, GM_W), lambda i: (0, 0))
    w3 = pl.BlockSpec((GROUPS, CHUNK, CHUNK), lambda i: (0, 0, 0))
    return pl.pallas_call(
        body, name=f"{tag}_gmlp_fwd", grid=(T // tm,),
        in_specs=[pl.BlockSpec((tm, GM_W), lambda i: (i, 1)), pl.BlockSpec((tm, GM_W), lambda i: (i, 2)), vec, vec, w3, w3],
        out_specs=pl.BlockSpec((tm, GM_W), lambda i: (i, 0)),
        out_shape=jax.ShapeDtypeStruct((T, GM_W), BF16),
        compiler_params=_params(("parallel",)),
    )(z_p, z_p, gv.reshape(1, GM_W), gout.reshape(1, GM_W), wc, bb)


def gmlp_bwd(tag, z_p, dmixed, gv, gout, wc, bb, tm=256):
    T = z_p.shape[0]
    nchunk = tm // CHUNK

    def body(u_ref, v_ref, dm_ref, gv_ref, go_ref, wc_ref, bb_ref, du_ref, dv_ref, dwc_ref, dbb_ref, dgv_ref, dgo_ref):
        i = pl.program_id(0)
        u, v = u_ref[...], v_ref[...]
        ug, vhat, rv, vn, gate = _gm_forward(u, v, gv_ref[...], wc_ref, bb_ref, nchunk)
        go = ug * gate
        ro = lax.rsqrt(jnp.mean(go * go, axis=-1, keepdims=True) + EPS)
        ohat = go * ro
        dm = dm_ref[...].astype(F32)
        dgo_part = jnp.sum(dm * ohat, axis=0, keepdims=True)
        doh = dm * go_ref[...]
        dgo = ro * (doh - ohat * jnp.mean(doh * ohat, axis=-1, keepdims=True))
        du_ref[...] = dgo * gate * _gelu_grad(u)
        dgate = dgo * ug
        dgb = dgate.astype(BF16)
        dvn_rows = []
        dwc_parts = []
        dbb_parts = []
        for gidx in range(GROUPS):
            cols = slice(gidx * 128, (gidx + 1) * 128)
            dw = jnp.zeros((CHUNK, CHUNK), F32)
            db = jnp.zeros((CHUNK, 128), F32)
            for cidx in range(nchunk):
                rows = slice(cidx * CHUNK, (cidx + 1) * CHUNK)
                dw = dw + lax.dot_general(dgb[rows, cols], vn[rows, cols], (((1,), (1,)), ((), ())),
                                          preferred_element_type=F32)
                db = db + dgate[rows, cols]
            dwc_parts.append(dw)
            dbb_parts.append(db)
        for cidx in range(nchunk):
            rows = slice(cidx * CHUNK, (cidx + 1) * CHUNK)
            dvn_rows.append(jnp.concatenate(
                [lax.dot_general(wc_ref[gidx], dgb[rows, gidx * 128:(gidx + 1) * 128], (((0,), (0,)), ((), ())),
                                 preferred_element_type=F32) for gidx in range(GROUPS)], axis=1))
        dvn = jnp.concatenate(dvn_rows, axis=0)
        dgv_part = jnp.sum(dvn * vhat, axis=0, keepdims=True)
        dvh = dvn * gv_ref[...]
        dvg = rv * (dvh - vhat * jnp.mean(dvh * vhat, axis=-1, keepdims=True))
        dv_ref[...] = dvg * _gelu_grad(v)

        @pl.when(i == 0)
        def _():
            for gidx in range(GROUPS):
                dwc_ref[gidx] = dwc_parts[gidx]
                dbb_ref[gidx] = dbb_parts[gidx]
            dgv_ref[...] = dgv_part
            dgo_ref[...] = dgo_part

        @pl.when(i > 0)
        def _():
            for gidx in range(GROUPS):
                dwc_ref[gidx] += dwc_parts[gidx]
                dbb_ref[gidx] += dbb_parts[gidx]
            dgv_ref[...] += dgv_part
            dgo_ref[...] += dgo_part

    vec = pl.BlockSpec((1, GM_W), lambda i: (0, 0))
    w3 = pl.BlockSpec((GROUPS, CHUNK, CHUNK), lambda i: (0, 0, 0))
    blk = pl.BlockSpec((tm, GM_W), lambda i: (i, 0))
    return pl.pallas_call(
        body, name=f"{tag}_gmlp_bwd", grid=(T // tm,),
        in_specs=[pl.BlockSpec((tm, GM_W), lambda i: (i, 1)), pl.BlockSpec((tm, GM_W), lambda i: (i, 2)),
                  pl.BlockSpec((tm, GM_W), lambda i: (i, 1)), vec, vec, w3, w3],
        out_specs=[blk, blk, w3, w3, vec, vec],
        out_shape=[jax.ShapeDtypeStruct((T, GM_W), F32)] * 2 + [jax.ShapeDtypeStruct((GROUPS, CHUNK, CHUNK), F32)] * 2
        + [jax.ShapeDtypeStruct((1, GM_W), F32)] * 2,
        compiler_params=_params(("arbitrary",)),
    )(z_p, z_p, dmixed, gv.reshape(1, GM_W), gout.reshape(1, GM_W), wc, bb)


def mixer_fwd(tag, h, w, tabs, wout_g, pre):
    T = h.shape[0]
    n2 = rms_fwd(f"{tag}_mix_rms", h, w["mix_norm"], D_MODEL)
    (z_p,) = mm_simple(f"{tag}_win", n2, lambda tk, tn: op_bt(w["w_in_pt"], tk, tn), T, IN_P, D_MODEL, 512, 1024, D_MODEL)
    cqn = rms_fwd(f"{tag}_cq_rms", z_p, w["q_a_norm"], Q_RANK, col_blk=0)
    ckvn = rms_fwd(f"{tag}_ckv_rms", z_p, w["kv_a_norm"], KV_RANK, col_blk=2)
    (q_raw,) = mm_simple(f"{tag}_wq", cqn, lambda tk, tn: op_b(w["wq_p"], tk, tn), T, 2048, Q_RANK, 512, 1024, Q_RANK)
    (kk_raw,) = mm_simple(f"{tag}_wk", ckvn, lambda tk, tn: op_b(w["wk_p"], tk, tn), T, 2048, KV_RANK, 512, 1024, KV_RANK)
    (vv,) = mm_simple(f"{tag}_wv", ckvn, lambda tk, tn: op_b(w["wv"], tk, tn), T, ATTN_W, KV_RANK, 512, 1024, KV_RANK,
                      out_dtype=BF16)
    q_full, k_full = qk_prep_fwd(tag, q_raw, kk_raw, z_p, w["gq_p"], w["gk_p"], tabs)
    a_out, lse = attn_fwd(tag, q_full, k_full, vv)
    mixed_a = rms_fwd(f"{tag}_ao_rms", a_out, w["attn_out_norm"], ATTN_W)
    mixed_g = gmlp_fwd(tag, z_p, w["gm_v_norm"], w["gm_out_norm"], w["wc"], w["bb"])
    tm, tn, tk = 512, 1024, 512
    (h2,) = matmul(
        f"{tag}_wout", (T // tm, D_MODEL // tn, ATTN_W // tk),
        [op_a(mixed_a, tm, tk), op_a(mixed_g, tm, tk)],
        [op_b_rows(wout_g, pre, tk, tn), op_b_rows(wout_g, pre, tk, tn, koff=ATTN_W // tk)],
        [(0, 0, 0), (1, 1, 0)], 1, [tile_mn(h, tm, tn)], [out_mn(T, D_MODEL, tm, tn, F32)],
        lambda accs, xs: (xs[0] + accs[0],), (tm, tn))
    res = dict(n2=n2, z_p=z_p, cqn=cqn, ckvn=ckvn, q_raw=q_raw, kk_raw=kk_raw, vv=vv, q_full=q_full, k_full=k_full,
               a_out=a_out, lse=lse, mixed_a=mixed_a, mixed_g=mixed_g)
    return h2, res


def mixer_bwd(tag, dh2, dh2_bf, h, w, tabs, wout_g, pre, r, after=None):
    T = h.shape[0]
    g = {}
    (dmixed,) = mm_simple(f"{tag}_dmixed", dh2_bf, lambda tk, tn: op_b_rows_t(wout_g, pre, tk, tn), T, D_MODEL, D_MODEL,
                          512, 512, D_MODEL, after=after)
    (dwo_a,) = mm_simple(f"{tag}_dwout_a", r["mixed_a"], lambda tk, tn: op_b(dh2_bf, tk, tn), ATTN_W, D_MODEL, T,
                         1024, 1024, 512, a_t=True, out_dtype=BF16)
    (dwo_g,) = mm_simple(f"{tag}_dwout_g", r["mixed_g"], lambda tk, tn: op_b(dh2_bf, tk, tn), GM_W, D_MODEL, T,
                         1024, 1024, 512, a_t=True, out_dtype=BF16)
    g["w_out"] = jnp.concatenate([dwo_a, dwo_g], axis=0)
    da_out, g["attn_out_norm"], delta = rms_bwd(f"{tag}_ao_rms_bwd", r["a_out"], w["attn_out_norm"], dmixed, ATTN_W,
                                                with_delta=True)
    dq_full, dk_full, dvv = attn_bwd(tag, r["q_full"], r["k_full"], r["vv"], da_out, r["lse"], delta)
    dq_raw, dkk_raw, dzkr, g["gq_p"], g["gk_p"] = qk_prep_bwd(tag, dq_full, dk_full, r["q_raw"], r["kk_raw"], r["z_p"],
                                                            w["gq_p"], w["gk_p"], tabs)
    (g["wq_p"],) = mm_simple(f"{tag}_dwq", r["cqn"], lambda tk, tn: op_b(dq_raw, tk, tn), Q_RANK, 2048, T, Q_RANK, 1024, 512,
                             a_t=True, out_dtype=BF16)
    (g["wk_p"],) = mm_simple(f"{tag}_dwk", r["ckvn"], lambda tk, tn: op_b(dkk_raw, tk, tn), KV_RANK, 2048, T, KV_RANK, 1024,
                             512, a_t=True, out_dtype=BF16)
    (g["wv"],) = mm_simple(f"{tag}_dwv", r["ckvn"], lambda tk, tn: op_b(dvv, tk, tn), KV_RANK, ATTN_W, T, KV_RANK, 1024, 512,
                           a_t=True, out_dtype=BF16)
    (dcqn,) = mm_simple(f"{tag}_dcqn", dq_raw, lambda tk, tn: op_bt(w["wq_p"], tk, tn), T, Q_RANK, 2048, 512, Q_RANK, 2048)
    (dck1,) = mm_simple(f"{tag}_dckvn_k", dkk_raw, lambda tk, tn: op_bt(w["wk_p"], tk, tn), T, KV_RANK, 2048, 512, KV_RANK,
                        2048)
    (dckvn,) = mm_simple(f"{tag}_dckvn_v", dvv, lambda tk, tn: op_bt(w["wv"], tk, tn), T, KV_RANK, ATTN_W, 512, KV_RANK,
                         ATTN_W, extras=[tile_mn(dck1, 512, KV_RANK)], epilogue=lambda accs, xs: (accs[0] + xs[0],))
    dc_q, g["q_a_norm"] = rms_bwd(f"{tag}_cq_rms_bwd", r["z_p"], w["q_a_norm"], dcqn, Q_RANK, col_blk=0)
    dc_kv, g["kv_a_norm"] = rms_bwd(f"{tag}_ckv_rms_bwd", r["z_p"], w["kv_a_norm"], dckvn, KV_RANK, col_blk=2)
    du, dv, g["wc"], g["bb"], g["gm_v_norm"], g["gm_out_norm"] = gmlp_bwd(
        tag, r["z_p"], dmixed, w["gm_v_norm"], w["gm_out_norm"], w["wc"], w["bb"])
    dz_p = jnp.concatenate([dc_q, dc_kv, dzkr, du, dv], axis=1).astype(BF16)
    (g["w_in_pt"],) = mm_simple(f"{tag}_dwin", dz_p, lambda tk, tn: op_b(r["n2"], tk, tn), IN_P, D_MODEL, T, 1024, 1024, 512,
                                a_t=True, out_dtype=BF16)
    (dn2,) = mm_simple(f"{tag}_dn2", dz_p, lambda tk, tn: op_b(w["w_in_pt"], tk, tn), T, D_MODEL, IN_P, 512, 1024, IN_P)
    dh1, g["mix_norm"], dh1_bf = rms_bwd(f"{tag}_mix_rms_bwd", h, w["mix_norm"], dn2, D_MODEL, dres=dh2, bf16_copy=True)
    return dh1, dh1_bf, g


def ple_fwd(tag, h3, p_l, w, wpg_g, wple_g, pre):
    T = h3.shape[0]
    (pw,) = mm_simple(f"{tag}_wple", p_l, lambda tk, tn: op_b_cols(wple_g, pre, tk, tn), T, D_MODEL, PLE_DIM, 512, 512,
                      PLE_DIM)
    e = rms_fwd(f"{tag}_ple_rms", pw, w["ple_norm"], D_MODEL, out_dtype=F32)
    n4 = rms_fwd(f"{tag}_pg_rms", h3, w["ple_gate_norm"], D_MODEL)

    def epi(accs, xs):
        gt = _sigmoid(accs[0])
        return xs[0] + gt * xs[1], gt

    tm, tn, tk = 512, 1024, 512
    h4, gate = matmul(
        f"{tag}_wpg", (T // tm, D_MODEL // tn, D_MODEL // tk),
        [op_a(n4, tm, tk)], [op_b_rows(wpg_g, pre, tk, tn)], [(0, 0, 0)], 1,
        [tile_mn(h3, tm, tn), tile_mn(e, tm, tn)],
        [out_mn(T, D_MODEL, tm, tn, F32), out_mn(T, D_MODEL, tm, tn, BF16)], epi, (tm, tn))
    return h4, dict(pw=pw, e=e, n4=n4, gate=gate)


def ple_bwd(tag, dh4, h3, p_l, w, wpg_g, wple_g, pre, r, tm=256):
    T = h3.shape[0]

    def act_body(d_ref, g_ref, e_ref, dpre_ref, de_ref):
        d, gt = d_ref[...], g_ref[...].astype(F32)
        dpre_ref[...] = (d * e_ref[...] * gt * (1.0 - gt)).astype(BF16)
        de_ref[...] = d * gt

    blk = pl.BlockSpec((tm, D_MODEL), lambda i: (i, 0))
    dpre, de = pl.pallas_call(
        act_body, name=f"{tag}_ple_act_bwd", grid=(T // tm,), in_specs=[blk, blk, blk], out_specs=[blk, blk],
        out_shape=[jax.ShapeDtypeStruct((T, D_MODEL), BF16), jax.ShapeDtypeStruct((T, D_MODEL), F32)],
        compiler_params=_params(("parallel",)),
    )(dh4, r["gate"], r["e"])
    g = {}
    (g["w_ple_gate"],) = mm_simple(f"{tag}_dwpg", r["n4"], lambda tk, tn: op_b(dpre, tk, tn), D_MODEL, D_MODEL, T,
                                   1024, 1024, 512, a_t=True, out_dtype=BF16)
    (dn4,) = mm_simple(f"{tag}_dn4", dpre, lambda tk, tn: op_b_rows_t(wpg_g, pre, tk, tn), T, D_MODEL, D_MODEL, 512, 512,
                       D_MODEL)
    dh3, g["ple_gate_norm"], dh3_bf = rms_bwd(f"{tag}_pg_rms_bwd", h3, w["ple_gate_norm"], dn4, D_MODEL, dres=dh4,
                                              bf16_copy=True)
    dpw, g["ple_norm"] = rms_bwd(f"{tag}_ple_rms_bwd", r["pw"], w["ple_norm"], de, D_MODEL)
    (g["w_ple"],) = mm_simple(f"{tag}_dwple", p_l, lambda tk, tn: op_b(dpw, tk, tn), PLE_DIM, D_MODEL, T, PLE_DIM, 512, 512,
                              a_t=True, outs=[out_cols(PLE_DIM, 512, PLE_DIM, 512, BF16)])
    return dh3, dh3_bf, g


def loss_grad(y, target, tm=256):
    T = y.shape[0]

    def body(y_ref, t_ref, dy_ref, l_ref):
        i = pl.program_id(0)
        d = y_ref[...] - t_ref[...]
        dy_ref[...] = d * (1.0 / D_MODEL)
        part = jnp.sum((d * d).reshape(tm // 8, 8, D_MODEL), axis=0)

        @pl.when(i == 0)
        def _():
            l_ref[...] = part

        @pl.when(i > 0)
        def _():
            l_ref[...] += part

    blk = pl.BlockSpec((tm, D_MODEL), lambda i: (i, 0))
    dy, part = pl.pallas_call(
        body, name="loss_grad", grid=(T // tm,), in_specs=[blk, blk],
        out_specs=[blk, pl.BlockSpec((8, D_MODEL), lambda i: (0, 0))],
        out_shape=[jax.ShapeDtypeStruct((T, D_MODEL), F32), jax.ShapeDtypeStruct((8, D_MODEL), F32)],
        compiler_params=_params(("arbitrary",)),
    )(y, target)
    return dy, 0.5 * jnp.sum(part) / D_MODEL


def _unshard_cols(g_l):
    return g_l.transpose(1, 0, 2).reshape(g_l.shape[1], -1)


def _shard_cols(w):
    return w.reshape(w.shape[0], N_CHIPS, -1).transpose(1, 0, 2)


def layer_weights(l, Gl, small):
    w = {k: small[k][l] for k in ("mix_norm", "q_a_norm", "kv_a_norm", "gm_v_norm", "attn_out_norm", "gm_out_norm",
                                  "ple_gate_norm", "ple_norm")}
    wint = Gl["w_in"][:, :IN_SHARD].reshape(-1, D_MODEL)
    z = lambda n: jnp.zeros((n, D_MODEL), BF16)
    w["w_in_pt"] = jnp.concatenate([wint[:768], z(128), wint[768:832], z(64), wint[832:]], axis=0)
    wuq = _unshard_cols(Gl["w_uq"]).reshape(Q_RANK, HEADS, QK_DIM)
    w["wq_p"] = jnp.pad(wuq, ((0, 0), (0, 0), (0, HEAD_PAD - QK_DIM))).reshape(Q_RANK, HEADS * HEAD_PAD)
    wukv = _unshard_cols(Gl["w_ukv"]).reshape(KV_RANK, HEADS, QK_NOPE + V_DIM)
    w["wk_p"] = jnp.pad(wukv[:, :, :QK_NOPE], ((0, 0), (0, 0), (0, HEAD_PAD - QK_NOPE))).reshape(KV_RANK, HEADS * HEAD_PAD)
    w["wv"] = wukv[:, :, QK_NOPE:].reshape(KV_RANK, ATTN_W)
    w["gq_p"] = jnp.pad(small["q_norm"][l], (0, HEAD_PAD - QK_DIM)).reshape(1, HEAD_PAD)
    w["gk_p"] = jnp.pad(small["k_norm"][l], (0, HEAD_PAD - QK_DIM)).reshape(1, HEAD_PAD)
    tril = jnp.tril(jnp.ones((CHUNK, CHUNK), dtype=bool))
    w["wc"] = jnp.where(tril[None], small["gm_ws"][l], 0.0).astype(BF16)
    w["bb"] = jnp.broadcast_to(small["gm_bs"][l][:, :, None], (GROUPS, CHUNK, 128)).astype(F32)
    return w


def mixer_grads_to_shards(g):
    out = {}
    dwint = g["w_in_pt"]
    dwint = jnp.concatenate([dwint[:768], dwint[896:960], dwint[1024:]], axis=0).reshape(N_CHIPS, IN_SHARD, D_MODEL)
    out["w_in"] = jnp.pad(dwint, ((0, 0), (0, IN_SHARD_PAD - IN_SHARD), (0, 0)))
    dwuq = g["wq_p"].reshape(Q_RANK, HEADS, HEAD_PAD)[:, :, :QK_DIM].reshape(Q_RANK, HEADS * QK_DIM)
    out["w_uq"] = _shard_cols(dwuq)
    dwukv = jnp.concatenate([g["wk_p"].reshape(KV_RANK, HEADS, HEAD_PAD)[:, :, :QK_NOPE],
                             g["wv"].reshape(KV_RANK, HEADS, V_DIM)], axis=-1).reshape(KV_RANK, HEADS * (QK_NOPE + V_DIM))
    out["w_ukv"] = _shard_cols(dwukv)
    out["w_out"] = g["w_out"].reshape(N_CHIPS, D_MODEL // N_CHIPS, D_MODEL)
    out["q_norm"] = g["gq_p"][0, :QK_DIM]
    out["k_norm"] = g["gk_p"][0, :QK_DIM]
    tril = jnp.tril(jnp.ones((CHUNK, CHUNK), dtype=bool))
    out["gm_ws"] = jnp.where(tril[None], g["wc"], 0.0)
    out["gm_bs"] = jnp.sum(g["bb"], axis=-1)
    for k in ("mix_norm", "q_a_norm", "kv_a_norm", "gm_v_norm", "attn_out_norm", "gm_out_norm"):
        out[k] = g[k][0]
    return out


def layer_fwd(l, h, p_l, Gl, small, tabs, after_first_ffn=None, before_ple=None):
    h1, r_a = ffn_fwd(f"l{l}a", h, small["ffn_a_norm"][l], Gl["ffn_a_w1"], Gl["ffn_a_w3"], Gl["ffn_a_w2"], ())
    if after_first_ffn is not None:
        Gl, small = after_first_ffn(h1, Gl, small)
    w = layer_weights(l, Gl, small)
    h2, r_m = mixer_fwd(f"l{l}", h1, w, tabs, Gl["w_out"], ())
    h3, r_b = ffn_fwd(f"l{l}b", h2, small["ffn_b_norm"][l], Gl["ffn_b_w1"], Gl["ffn_b_w3"], Gl["ffn_b_w2"], ())
    if before_ple is not None:
        w = {**w, "ple_norm": w["ple_norm"] + before_ple(h3)[0, 0]}
    h4, r_p = ple_fwd(f"l{l}", h3, p_l, w, Gl["w_ple_gate"], Gl["w_ple"], ())
    return h4, (w, h, h1, h2, h3, r_a, r_m, r_b, r_p)


def layer_bwd(l, dh, p_l, Gl, small, tabs, saved, before=None):
    w, h0, h1, h2, h3, r_a, r_m, r_b, r_p = saved
    slabs = lambda d: d.reshape(N_CHIPS, FF_PAD, D_MODEL)
    hook = lambda block: before[block](gl, dh) if before and block in before else None
    gl = {}
    dh, dh_bf, g_p = ple_bwd(f"l{l}", dh, h3, p_l, w, Gl["w_ple_gate"], Gl["w_ple"], (), r_p)
    gl["w_ple_gate"] = g_p["w_ple_gate"].reshape(N_CHIPS, D_MODEL // N_CHIPS, D_MODEL)
    gl["w_ple"] = g_p["w_ple"]
    gl["ple_gate_norm"], gl["ple_norm"] = g_p["ple_gate_norm"][0], g_p["ple_norm"][0]
    dh, dh_bf, dg, dw1, dw3, dw2 = ffn_bwd(f"l{l}b", dh, dh_bf, h2, small["ffn_b_norm"][l], r_b,
                                           Gl["ffn_b_w1"], Gl["ffn_b_w3"], Gl["ffn_b_w2"], (), hook("ffn_b"))
    gl["ffn_b_norm"] = dg[0]
    gl["ffn_b_w1"], gl["ffn_b_w3"], gl["ffn_b_w2"] = slabs(dw1), slabs(dw3), slabs(dw2)
    dh, dh_bf, g_m = mixer_bwd(f"l{l}", dh, dh_bf, h1, w, tabs, Gl["w_out"], (), r_m, hook("mixer"))
    gl.update(mixer_grads_to_shards(g_m))
    last_dw = (lambda dh_: before["ffn_a_dw"](gl, dh_)) if before and "ffn_a_dw" in before else None
    dh, _, dg, dw1, dw3, dw2 = ffn_bwd(f"l{l}a", dh, dh_bf, h0, small["ffn_a_norm"][l], r_a,
                                       Gl["ffn_a_w1"], Gl["ffn_a_w3"], Gl["ffn_a_w2"], (), hook("ffn_a"), last_dw)
    gl["ffn_a_norm"] = dg[0]
    gl["ffn_a_w1"], gl["ffn_a_w3"], gl["ffn_a_w2"] = slabs(dw1), slabs(dw3), slabs(dw2)
    return dh, gl


MESH = pl.DeviceIdType.MESH
HBM_SPEC = pl.BlockSpec(memory_space=pltpu.HBM)


def _place():
    x, y, c = lax.axis_index("x"), lax.axis_index("y"), lax.axis_index("c")
    others = [(1 - x, y), (x, 1 - y), (1 - x, 1 - y)]
    return x, y, c, 2 * x + y, others


def prep_shard(name, w, layer, rows_pad, place, after=None):
    _, ks, n = w.shape
    ksp = ks + rows_pad
    tc = 512 if n % 512 == 0 else n
    deps = [] if after is None else [after]

    def body(place_ref, x_ref, *rest):
        o_ref = rest[-1]
        o_ref[:ks] = x_ref[...].astype(BF16)
        if rows_pad:
            o_ref[ks:] = jnp.zeros((rows_pad, tc), BF16)

    return pl.pallas_call(
        body, name=name,
        grid_spec=pltpu.PrefetchScalarGridSpec(
            num_scalar_prefetch=1, grid=(n // tc,),
            in_specs=[pl.BlockSpec((None, ks, tc), lambda i, s: (layer, 0, i))] + [ANY_SPEC] * len(deps),
            out_specs=pl.BlockSpec((None, ksp, tc), lambda i, s: (s[0], 0, i))),
        out_shape=jax.ShapeDtypeStruct((N_CHIPS, ksp, n), BF16),
        compiler_params=_params(("parallel",)),
    )(place, w, *deps)


def share_halves(name, fulls):
    n = len(fulls)

    def body(*refs):
        o_refs = refs[n:2 * n]
        send, recv = refs[2 * n:]
        x, y, c, _, _ = _place()
        cps = []
        for w in range(n):
            kh = fulls[w].shape[0] // 2
            half = o_refs[w].at[pl.ds(c * kh, kh)]
            cps.append(pltpu.make_async_remote_copy(src_ref=half, dst_ref=half, send_sem=send.at[w], recv_sem=recv.at[w],
                                                    device_id=(x, y, 1 - c), device_id_type=MESH))
        for cp in cps:
            cp.start()
        for cp in cps:
            cp.wait()

    return pl.pallas_call(
        body, name=name, in_specs=[HBM_SPEC] * n, out_specs=[HBM_SPEC] * n,
        out_shape=[jax.ShapeDtypeStruct(f.shape, f.dtype) for f in fulls],
        input_output_aliases={w: w for w in range(n)},
        scratch_shapes=[pltpu.SemaphoreType.DMA((n,))] * 2,
    )(*fulls)


SEM_SPEC = pl.BlockSpec(memory_space=pltpu.SEMAPHORE)
ANY_SPEC = pl.BlockSpec(memory_space=pl.ANY)
DATAFLOW = pltpu.SideEffectType.DATAFLOW_SIDE_EFFECTING


def _hbm(x):
    return pltpu.with_memory_space_constraint(x, pltpu.HBM)


def _start_call(name, slots, after, issue):
    n = len(slots)
    deps = [] if after is None else [after]
    nd = len(deps)

    def body(*refs):
        issue(refs[n + nd + 2:2 * n + nd + 2], refs[n + nd], refs[n + nd + 1])
        token = refs[2 * n + nd + 2]
        token[...] = jnp.zeros_like(token)

    outs = pl.pallas_call(
        body, name=name,
        in_specs=[HBM_SPEC] * n + [ANY_SPEC] * nd,
        out_specs=(SEM_SPEC, SEM_SPEC, *([HBM_SPEC] * n), pl.BlockSpec(memory_space=pltpu.VMEM)),
        out_shape=(pltpu.SemaphoreType.DMA((n,)), pltpu.SemaphoreType.DMA((n,)),
                   *[pltpu.HBM(s.shape, s.dtype) for s in slots], jax.ShapeDtypeStruct((8, 128), F32)),
        input_output_aliases={w: w + 2 for w in range(n)},
        compiler_params=pltpu.CompilerParams(has_side_effects=DATAFLOW),
    )(*[_hbm(s) for s in slots], *deps)
    return outs[0], outs[1], list(outs[2:2 + n]), outs[2 + n]


def gather_start(name, slots, after):
    def issue(g_refs, send, recv):
        x, y, c, jme, others = _place()
        for w in range(len(slots)):
            kh = slots[w].shape[1] // 2
            mine = g_refs[w].at[jme, pl.ds(c * kh, kh)]
            for (px, py) in others:
                pltpu.make_async_remote_copy(src_ref=mine, dst_ref=mine, send_sem=send.at[w], recv_sem=recv.at[w],
                                             device_id=(px, py, c), device_id_type=MESH).start()

    return _start_call(name, slots, after, issue)


def forward_start(name, slots):
    def issue(g_refs, send, recv):
        x, y, c, _, others = _place()
        for w in range(len(slots)):
            kh = slots[w].shape[1] // 2
            for (px, py) in others:
                blk = g_refs[w].at[2 * px + py, pl.ds(c * kh, kh)]
                pltpu.make_async_remote_copy(src_ref=blk, dst_ref=blk, send_sem=send.at[w], recv_sem=recv.at[w],
                                             device_id=(x, y, 1 - c), device_id_type=MESH).start()

    return _start_call(name, slots, None, issue)


def gather_wait(name, send, recv, flying, after):
    n = len(flying)

    def body(*refs):
        send_ref, recv_ref = refs[n], refs[n + 1]
        g_refs = refs[n + 3:]
        x, y, c, _, _ = _place()
        for w in range(n):
            three = g_refs[w].at[pl.ds(0, 3), pl.ds(0, flying[w].shape[1] // 2)]
            cp = pltpu.make_async_remote_copy(src_ref=three, dst_ref=three, send_sem=send_ref.at[w], recv_sem=recv_ref.at[w],
                                              device_id=(x, y, 1 - c), device_id_type=MESH)
            cp.wait_send()
            cp.wait_recv()

    return pl.pallas_call(
        body, name=name,
        in_specs=[HBM_SPEC] * n + [SEM_SPEC, SEM_SPEC, ANY_SPEC],
        out_specs=[HBM_SPEC] * n,
        out_shape=[pltpu.HBM(s.shape, s.dtype) for s in flying],
        input_output_aliases={w: w for w in range(n)},
        compiler_params=pltpu.CompilerParams(has_side_effects=DATAFLOW),
    )(*flying, send, recv, after)


def _send_start(name, srcs, land_shapes, issue, after):
    n = len(srcs)
    deps = [] if after is None else [after]
    nd = len(deps)

    def body(*refs):
        base = 2 * n + nd
        issue(refs[base + 2:base + 2 + n], refs[base + 2 + n:base + 2 + 2 * n], refs[base], refs[base + 1])
        token = refs[base + 2 + 2 * n]
        token[...] = jnp.zeros_like(token)

    lands = [_hbm(lax.empty(shape, s.dtype)) for shape, s in zip(land_shapes, srcs)]
    outs = pl.pallas_call(
        body, name=name,
        in_specs=[HBM_SPEC] * (2 * n) + [ANY_SPEC] * nd,
        out_specs=(SEM_SPEC, SEM_SPEC, *([HBM_SPEC] * (2 * n)), pl.BlockSpec(memory_space=pltpu.VMEM)),
        out_shape=(pltpu.SemaphoreType.DMA((n,)), pltpu.SemaphoreType.DMA((n,)),
                   *[pltpu.HBM(s.shape, s.dtype) for s in srcs], *[pltpu.HBM(l.shape, l.dtype) for l in lands],
                   jax.ShapeDtypeStruct((8, 128), F32)),
        input_output_aliases={w: w + 2 for w in range(2 * n)},
        compiler_params=pltpu.CompilerParams(has_side_effects=DATAFLOW),
    )(*[_hbm(s) for s in srcs], *lands, *deps)
    return outs[0], outs[1], list(outs[2:2 + n]), list(outs[2 + n:2 + 2 * n]), outs[2 + 2 * n]


def _send_wait(name, send, recv, srcs, lands, after, landed):
    n = len(srcs)

    def body(*refs):
        send_ref, recv_ref = refs[2 * n], refs[2 * n + 1]
        q_refs = refs[3 * n + 3:]
        x, y, c, _, _ = _place()
        for w in range(n):
            cp = pltpu.make_async_remote_copy(src_ref=landed(q_refs[w]), dst_ref=landed(q_refs[w]), send_sem=send_ref.at[w],
                                              recv_sem=recv_ref.at[w], device_id=(x, y, 1 - c), device_id_type=MESH)
            cp.wait_send()
            cp.wait_recv()

    outs = pl.pallas_call(
        body, name=name,
        in_specs=[HBM_SPEC] * (2 * n) + [SEM_SPEC, SEM_SPEC, ANY_SPEC],
        out_specs=[HBM_SPEC] * (2 * n),
        out_shape=[pltpu.HBM(a.shape, a.dtype) for a in list(srcs) + list(lands)],
        input_output_aliases={w: w for w in range(2 * n)},
        compiler_params=pltpu.CompilerParams(has_side_effects=DATAFLOW),
    )(*srcs, *lands, send, recv, after)
    return list(outs[:n]), list(outs[n:])


def exchange_start(name, grads, after):
    def issue(d_refs, r_refs, send, recv):
        x, y, c, _, _ = _place()
        for w in range(len(grads)):
            half = grads[w].shape[1] // 2
            pltpu.make_async_remote_copy(
                src_ref=d_refs[w].at[pl.ds(0, N_CHIPS), pl.ds((1 - c) * half, half)], dst_ref=r_refs[w],
                send_sem=send.at[w], recv_sem=recv.at[w], device_id=(x, y, 1 - c), device_id_type=MESH).start()

    return _send_start(name, grads, [(N_CHIPS, g.shape[1] // 2, g.shape[2]) for g in grads], issue, after)


def exchange_wait(name, send, recv, grads, lands, after):
    return _send_wait(name, send, recv, grads, lands, after, lambda r: r)


def scatter_start(name, parts):
    def issue(p_refs, q_refs, send, recv):
        x, y, c, jme, others = _place()
        for w in range(len(parts)):
            for (px, py) in others:
                pltpu.make_async_remote_copy(
                    src_ref=p_refs[w].at[2 * px + py], dst_ref=q_refs[w].at[jme], send_sem=send.at[w], recv_sem=recv.at[w],
                    device_id=(px, py, c), device_id_type=MESH).start()

    return _send_start(name, parts, [p.shape for p in parts], issue, None)


def scatter_wait(name, send, recv, parts, lands, after):
    return _send_wait(name, send, recv, parts, lands, after, lambda r: r.at[pl.ds(0, 3)])


def allreduce_small(v):
    R = v.shape[0]

    def body(v_ref, o_ref, sib_ref, mine_ref, all_ref, d_send, d_recv, i_send, i_recv):
        x, y, c, jme, others = _place()
        swap = pltpu.make_async_remote_copy(src_ref=v_ref, dst_ref=sib_ref, send_sem=d_send, recv_sem=d_recv,
                                            device_id=(x, y, 1 - c), device_id_type=MESH)
        swap.start()
        swap.wait()
        mine_ref[...] = v_ref[...] + sib_ref[...]
        for (px, py) in others:
            pltpu.make_async_remote_copy(src_ref=mine_ref, dst_ref=all_ref.at[jme], send_sem=i_send, recv_sem=i_recv,
                                         device_id=(px, py, c), device_id_type=MESH).start()
        three = all_ref.at[pl.ds(0, 3)]
        wait3 = pltpu.make_async_remote_copy(src_ref=three, dst_ref=three, send_sem=i_send, recv_sem=i_recv,
                                             device_id=(x, y, c), device_id_type=MESH)
        wait3.wait_recv()
        wait3.wait_send()
        all_ref[jme] = mine_ref[...]
        o_ref[...] = ((all_ref[0] + all_ref[1]) + all_ref[2]) + all_ref[3]

    vm = pl.BlockSpec(memory_space=pltpu.VMEM)
    return pl.pallas_call(
        body, name="allreduce_small", in_specs=[vm], out_specs=vm,
        out_shape=jax.ShapeDtypeStruct(v.shape, F32),
        scratch_shapes=[pltpu.VMEM((R, 128), F32), pltpu.VMEM((R, 128), F32), pltpu.VMEM((N_CHIPS, R, 128), F32),
                        pltpu.SemaphoreType.DMA, pltpu.SemaphoreType.DMA, pltpu.SemaphoreType.DMA, pltpu.SemaphoreType.DMA],
        compiler_params=pltpu.CompilerParams(vmem_limit_bytes=VMEM_LIMIT_BYTES),
    )(v)


def _row_tile(rows, width, mult=16, cap=3 << 20):
    best = rows
    for t in range(mult, rows + 1, mult):
        if rows % t == 0 and t * width * 4 <= cap:
            best = t
    return best


def add_sibling(name, mine, theirs, place):
    _, kh, ns = theirs.shape
    tr = _row_tile(kh, ns)
    nblk = kh // tr

    def body(place_ref, a_ref, b_ref, o_ref):
        o_ref[...] = (a_ref[...].astype(F32) + b_ref[...].astype(F32)).astype(BF16)

    return pl.pallas_call(
        body, name=name,
        grid_spec=pltpu.PrefetchScalarGridSpec(
            num_scalar_prefetch=1, grid=(N_CHIPS, nblk),
            in_specs=[pl.BlockSpec((None, tr, ns), lambda j, i, s: (j, s[1] * nblk + i, 0)),
                      pl.BlockSpec((None, tr, ns), lambda j, i, s: (j, i, 0))],
            out_specs=pl.BlockSpec((None, tr, ns), lambda j, i, s: (j, i, 0))),
        out_shape=jax.ShapeDtypeStruct(theirs.shape, BF16),
        compiler_params=_params(("parallel", "parallel")),
    )(place, mine, theirs)


def add_chips(name, q, p, place):
    _, kh, ns = q.shape
    tr = _row_tile(kh, ns)
    nblk = kh // tr

    def body(place_ref, *refs):
        q_refs, own_ref, o_ref = refs[:N_CHIPS], refs[N_CHIPS], refs[-1]
        jme = place_ref[0]
        tot = None
        for j in range(N_CHIPS):
            v = jnp.where(jme == j, own_ref[...], q_refs[j][...]).astype(F32)
            tot = v if tot is None else tot + v
        o_ref[...] = tot

    def q_ix(j):
        return lambda i, s: (jnp.where(s[0] == j, (j + 1) % N_CHIPS, j), i, 0)

    in_specs = [pl.BlockSpec((None, tr, ns), q_ix(j)) for j in range(N_CHIPS)]
    in_specs.append(pl.BlockSpec((None, tr, ns), lambda i, s: (s[0], i, 0)))
    return pl.pallas_call(
        body, name=name,
        grid_spec=pltpu.PrefetchScalarGridSpec(
            num_scalar_prefetch=1, grid=(nblk,), in_specs=in_specs,
            out_specs=pl.BlockSpec((tr, ns), lambda i, s: (s[1] * nblk + i, 0))),
        out_shape=jax.ShapeDtypeStruct((2 * kh, ns), F32),
        compiler_params=_params(("parallel",)),
    )(place, q, q, q, q, p)


ADAM_LR, ADAM_B1, ADAM_B2, ADAM_EPS, ADAM_WD, ADAM_STEP = 0.001, 0.9, 0.999, 1e-08, 0.01, 10


def adamw(name, w, g, m, v, layer, prev=None, after=None):
    _, k, ns = w.shape
    nsp = g.shape[1]
    tr = _row_tile(k, nsp, mult=8, cap=3 << 20)

    def body(w_ref, g_ref, m_ref, v_ref, *rest):
        go_ref, d_ref, mo_ref, vo_ref = rest[-4:]
        gv = g_ref[:, :ns] if nsp != ns else g_ref[...]
        mn = ADAM_B1 * m_ref[...] + (1.0 - ADAM_B1) * gv
        vn = ADAM_B2 * v_ref[...] + (1.0 - ADAM_B2) * (gv * gv)
        m_hat = mn / (1.0 - ADAM_B1 ** ADAM_STEP)
        v_hat = vn / (1.0 - ADAM_B2 ** ADAM_STEP)
        go_ref[...] = gv
        d_ref[...] = -ADAM_LR * (m_hat / (jnp.sqrt(v_hat) + ADAM_EPS) + ADAM_WD * w_ref[...])
        mo_ref[...] = mn
        vo_ref[...] = vn

    blk = pl.BlockSpec((None, tr, ns), lambda i: (layer, i, 0))
    gblk = pl.BlockSpec((tr, nsp), lambda i: (i, 0))
    args, in_specs, aliases = [w, g, m, v], [blk, gblk, blk, blk], {}
    if prev is not None:
        args += list(prev)
        in_specs += [pl.BlockSpec(memory_space=pl.ANY)] * 4
        aliases = {4 + i: i for i in range(4)}
    if after is not None:
        args.append(after)
        in_specs.append(pl.BlockSpec(memory_space=pl.ANY))
    return pl.pallas_call(
        body, name=name, grid=(k // tr,), in_specs=in_specs, out_specs=[blk] * 4,
        out_shape=[jax.ShapeDtypeStruct(w.shape, F32)] * 4, input_output_aliases=aliases,
        compiler_params=_params(("parallel",)),
    )(*args)


WEIGHTS = ("ffn_a_norm", "ffn_a_w1", "ffn_a_w3", "ffn_a_w2", "mix_norm", "w_in", "q_a_norm", "w_uq", "kv_a_norm", "w_ukv",
           "q_norm", "k_norm", "gm_v_norm", "gm_ws", "gm_bs", "attn_out_norm", "gm_out_norm", "w_out", "ffn_b_norm",
           "ffn_b_w1", "ffn_b_w3", "ffn_b_w2", "ple_gate_norm", "w_ple_gate", "w_ple", "ple_norm")
_FF = FF_PAD - FF_SHARD
BIG = {"ffn_a_w1": _FF, "ffn_a_w3": _FF, "ffn_a_w2": _FF, "ffn_b_w1": _FF, "ffn_b_w3": _FF, "ffn_b_w2": _FF,
       "w_in": IN_SHARD_PAD - IN_SHARD, "w_uq": 0, "w_ukv": 0, "w_ple": 0, "w_out": 0, "w_ple_gate": 0}
TRANSPOSED = ("ffn_a_w1", "ffn_a_w3", "ffn_b_w1", "ffn_b_w3", "w_in")
SMALL = tuple(n for n in WEIGHTS if n not in BIG)
PACK = 1024


def _pack_small(d):
    parts = []
    for n in SMALL:
        flat = d[n].reshape(-1)
        parts.append(jnp.pad(flat, (0, (-flat.shape[0]) % PACK)))
    return jnp.concatenate(parts).reshape(-1, 128)


def _unpack_small(buf, like):
    flat = buf.reshape(-1)
    out, pos = {}, 0
    for n in SMALL:
        size = math.prod(like[n].shape)
        out[n] = flat[pos:pos + size].reshape(like[n].shape)
        pos += size + (-size) % PACK
    return out


def kernel(*args):
    names = (("x", "p", "positions") + WEIGHTS + ("loss_target",) + tuple("m_" + n for n in WEIGHTS)
             + tuple("v_" + n for n in WEIGHTS))
    a = dict(zip(names, args, strict=True))
    x, p, positions, target = a["x"][0], a["p"][:, 0], a["positions"][0], a["loss_target"][0]
    for n in TRANSPOSED:
        for pre in ("", "m_", "v_"):
            a[pre + n] = jnp.swapaxes(a[pre + n], 1, 2)

    place = jnp.stack([2 * lax.axis_index("x") + lax.axis_index("y"), lax.axis_index("c")]).astype(jnp.int32)
    small = {n: a[n] for n in SMALL}
    tabs = rope_tables(positions)
    first = ("ffn_a_w1", "ffn_a_w3", "ffn_a_w2")
    rest = tuple(n for n in BIG if n not in first)
    prep = lambda n, l, after: prep_shard(f"prep_{n}_{l}", a[n], l, BIG[n], place, after)

    def finish_gather(tag, started, after):
        send, recv, flying, _ = started
        arrived = gather_wait(f"gather_{tag}_wait", send, recv, flying, after)
        send, recv, flying, token = forward_start(f"forward_{tag}_start", arrived)
        return gather_wait(f"forward_{tag}_wait", send, recv, flying, token)

    ga = gather_start("gather_l0a_start", [prep(n, 0, None) for n in first], None)
    gb = gather_start("gather_l0b_start", [prep(n, 0, ga[3]) for n in rest], None)
    slots1 = []
    for n in BIG:
        slots1.append(prep(n, 1, slots1[-1] if slots1 else gb[3]))
    G0 = dict(zip(first, finish_gather("l0a", ga, slots1[-1])))
    later = {}

    def after_first_ffn(h1, Gl, small_):
        later["G0"] = {**Gl, **dict(zip(rest, finish_gather("l0b", gb, h1)))}
        later["g1"] = gather_start("gather_l1_start", slots1, later["G0"]["w_uq"])
        return later["G0"], {**small_, "mix_norm": small_["mix_norm"] + later["g1"][3][0, 0]}

    def before_ple(h3):
        send, recv, flying, _ = later["g1"]
        later["f1"] = forward_start("forward_l1_start", gather_wait("gather_l1_wait", send, recv, flying, h3))
        return later["f1"][3]

    h, saved0 = layer_fwd(0, x, p[0], G0, small, tabs, after_first_ffn, before_ple)
    G0 = later["G0"]
    G1 = dict(zip(BIG, gather_wait("forward_l1_wait", *later["f1"][:3], h)))
    h, saved1 = layer_fwd(1, h, p[1], G1, small, tabs)
    dh, loss = loss_grad(h, target)
    loss = lax.psum(loss, ("x", "y", "c"))

    groups = {"l1": tuple(BIG),
              "l0a": ("w_ple_gate", "w_ple", "ffn_b_w1", "ffn_b_w3", "ffn_b_w2"),
              "l0b": ("w_in", "w_uq", "w_ukv", "w_out"),
              "l0c": ("ffn_a_w1", "ffn_a_w3", "ffn_a_w2")}
    crossing, started = [], {}

    def begin(tag, gl, after):
        ex = exchange_start(f"exchange_{tag}_start", [gl[n] for n in groups[tag]], after)
        crossing.append((tag, ex))
        return ex[4]

    def advance(after):
        tag, (send, recv, mine, lands, _) = crossing.pop()
        mine, theirs = exchange_wait(f"exchange_{tag}_wait", send, recv, mine, lands, after)
        parts = [add_sibling(f"add_sibling_{n}_{tag}", d, r, place) for n, d, r in zip(groups[tag], mine, theirs)]
        started[tag] = scatter_start(f"scatter_{tag}_start", parts)
        return started[tag][4]

    def finish(tag, after):
        send, recv, parts, lands, _ = started[tag]
        parts, slabs = scatter_wait(f"scatter_{tag}_wait", send, recv, parts, lands, after)
        halves = [add_chips(f"add_chips_{n}_{tag}", q, pt, place) for n, q, pt in zip(groups[tag], slabs, parts)]
        return dict(zip(groups[tag], share_halves(f"share_{tag}", halves)))

    def update(names, full, layer, prev, after):
        outs = {}
        for n in names:
            outs[n] = adamw(f"adamw_{n}_{layer}", a[n], full[n], a["m_" + n], a["v_" + n], layer, prev and prev[n], after)
            after = outs[n][1]
        return outs, after

    grads = [None, None]
    dh, grads[1] = layer_bwd(1, dh, p[1], G1, small, tabs, saved1)
    token = begin("l1", grads[1], None)
    w0 = {**saved0[0], "ple_gate_norm": saved0[0]["ple_gate_norm"] + token[0, 0]}
    hooks = {"ffn_b": lambda gl, dh_: advance(dh_),
             "mixer": lambda gl, dh_: begin("l0a", gl, None),
             "ffn_a": lambda gl, dh_: begin("l0b", gl, advance(dh_)),
             "ffn_a_dw": lambda gl, dh_: advance(dh_)}
    gx, grads[0] = layer_bwd(0, dh, p[0], G0, small, tabs, (w0,) + saved0[1:], hooks)
    token = begin("l0c", grads[0], None)
    full1 = finish("l1", token)
    last = tuple(BIG)[-1]
    outs1, behind = update(BIG, full1, 1, None, advance(full1[last]))
    full0 = finish("l0a", behind)
    full0.update(finish("l0b", full0[groups["l0a"][-1]]))
    early = groups["l0a"] + groups["l0b"]
    outs0, behind = update(early, full0, 0, outs1, full0[groups["l0b"][-1]])
    outs0.update(update(groups["l0c"], finish("l0c", behind), 0, outs1, None)[0])

    out_g, out_d, out_m, out_v = {}, {}, {}, {}
    for n in BIG:
        outs = [jnp.swapaxes(o, 1, 2) for o in outs0[n]] if n in TRANSPOSED else outs0[n]
        out_g[n], out_d[n], out_m[n], out_v[n] = outs

    gs = allreduce_small(_pack_small({n: jnp.stack([grads[0][n], grads[1][n]]) for n in SMALL}))
    rows = gs.shape[0] // 2
    packed = [_pack_small(d).reshape(2, rows, 128) for d in
              (small, {n: a["m_" + n] for n in SMALL}, {n: a["v_" + n] for n in SMALL})]
    gs = gs.reshape(2, rows, 128)
    sm = adamw("adamw_small_0", packed[0], gs[0], packed[1], packed[2], 0)
    sm = adamw("adamw_small_1", packed[0], gs[1], packed[1], packed[2], 1, sm)
    for dst, buf in zip((out_g, out_d, out_m, out_v), sm):
        dst.update(_unpack_small(buf, small))

    return (loss, gx[None], *[out_g[n] for n in WEIGHTS], *[out_d[n] for n in WEIGHTS],
            *[out_m[n] for n in WEIGHTS], *[out_v[n] for n in WEIGHTS])
```

```python
import math

import jax
import jax.numpy as jnp
from jax import lax
from jax.experimental import pallas as pl
from jax.experimental.pallas import tpu as pltpu

F32 = jnp.float32
BF16 = jnp.bfloat16

D_MODEL = 2048
D_FF = 5504
N_CHIPS = 4
FF_SHARD = D_FF // N_CHIPS
FF_PAD = 1408
FF_P = N_CHIPS * FF_PAD
HEADS = 8
QK_NOPE = 128
QK_ROPE = 64
QK_DIM = 192
HEAD_PAD = 256
V_DIM = 128
Q_RANK = 512
KV_RANK = 256
ATTN_W = 1024
GM_W = 1024
GROUPS = 8
CHUNK = 128
PLE_DIM = 256
IN_P = 3072
IN_SHARD = 720
IN_SHARD_PAD = 736
EPS = 1e-6
ROPE_BASE = 10000.0
ATTN_SCALE = QK_DIM ** -0.5
VMEM_LIMIT_BYTES = 56 * 1024 * 1024


def _params(sem):
    return pltpu.CompilerParams(dimension_semantics=sem, vmem_limit_bytes=VMEM_LIMIT_BYTES)


def _bf(x):
    return x if x.dtype == BF16 else x.astype(BF16)


def _sigmoid(x):
    return 1.0 / (1.0 + jnp.exp(-x))


_GELU_C = math.sqrt(2.0 / math.pi)


def _gelu(x):
    t = jnp.tanh(_GELU_C * (x + 0.044715 * x * x * x))
    return 0.5 * x * (1.0 + t)


def _gelu_grad(x):
    t = jnp.tanh(_GELU_C * (x + 0.044715 * x * x * x))
    return 0.5 * (1.0 + t) + 0.5 * x * (1.0 - t * t) * _GELU_C * (1.0 + 3 * 0.044715 * x * x)


def op_a(a, tm, tk):
    return (a, (tm, tk), lambda i, j, k: (i, k), 1)


def op_at(a, tm, tk):
    return (a, (tk, tm), lambda i, j, k: (k, i), 0)


def op_b(b, tk, tn):
    return (b, (tk, tn), lambda i, j, k: (k, j), 0)


def op_bt(b, tk, tn):
    return (b, (tn, tk), lambda i, j, k: (j, k), 1)


def op_b_cols(g, pre, tk, tn):
    nb = g.shape[-1] // tn
    none = (None,) * (1 + len(pre))
    return (g, none + (tk, tn), lambda i, j, k: (j // nb,) + tuple(pre) + (k, j % nb), 0)


def op_b_rows(g, pre, tk, tn, koff=0):
    nb = g.shape[-2] // tk
    none = (None,) * (1 + len(pre))
    return (g, none + (tk, tn), lambda i, j, k: ((k + koff) // nb,) + tuple(pre) + ((k + koff) % nb, j), 0)


def op_b_rows_t(g, pre, tk, tn):
    nb = g.shape[-2] // tn
    none = (None,) * (1 + len(pre))
    return (g, none + (tn, tk), lambda i, j, k: (j // nb,) + tuple(pre) + (j % nb, k), 1)


def tile_mn(x, tm, tn):
    return (x, (tm, tn), lambda i, j: (i, j))


def out_mn(M, N, tm, tn, dtype):
    return (jax.ShapeDtypeStruct((M, N), dtype), (tm, tn), lambda i, j: (i, j))


def out_cols(M, ns, tm, tn, dtype):
    nb = ns // tn
    return (jax.ShapeDtypeStruct((N_CHIPS, M, ns), dtype), (None, tm, tn), lambda i, j: (j // nb, i, j % nb))


def matmul(name, grid_mnk, a_ops, b_ops, terms, n_acc, extras, outs, epilogue, acc_tile, n_outer=False, after=None):
    gm, gn, gk = grid_mnk
    na, nb, nx, no = len(a_ops), len(b_ops), len(extras), len(outs)
    nd = 0 if after is None else 1

    def body(*refs):
        a_refs, b_refs = refs[:na], refs[na:na + nb]
        x_refs = refs[na + nb:na + nb + nx]
        o_refs = refs[na + nb + nx + nd:na + nb + nx + nd + no]
        acc_refs = refs[na + nb + nx + nd + no:]
        k = pl.program_id(2)

        @pl.when(k == 0)
        def _():
            for acc in acc_refs:
                acc[...] = jnp.zeros_like(acc)

        for ai, bi, ci in terms:
            dims = (((a_ops[ai][3],), (b_ops[bi][3],)), ((), ()))
            acc_refs[ci][...] += lax.dot_general(_bf(a_refs[ai][...]), _bf(b_refs[bi][...]), dims,
                                                 preferred_element_type=F32)

        @pl.when(k == gk - 1)
        def _():
            res = epilogue([acc[...] for acc in acc_refs], [x[...] for x in x_refs])
            for o, v in zip(o_refs, res):
                o[...] = v.astype(o.dtype)

    if n_outer:
        grid = (gn, gm, gk)

        def ix3(f):
            return lambda j, i, k: f(i, j, k)

        def ix2(f):
            return lambda j, i, k: f(i, j)
    else:
        grid = (gm, gn, gk)

        def ix3(f):
            return lambda i, j, k: f(i, j, k)

        def ix2(f):
            return lambda i, j, k: f(i, j)

    in_specs = [pl.BlockSpec(blk, ix3(f)) for (_, blk, f, _) in list(a_ops) + list(b_ops)]
    in_specs += [pl.BlockSpec(blk, ix2(f)) for (_, blk, f) in extras]
    in_specs += [pl.BlockSpec(memory_space=pl.ANY)] * nd
    out_specs = [pl.BlockSpec(blk, ix2(f)) for (_, blk, f) in outs]
    return pl.pallas_call(
        body,
        name=name,
        grid=grid,
        in_specs=in_specs,
        out_specs=out_specs,
        out_shape=[s for (s, _, _) in outs],
        scratch_shapes=[pltpu.VMEM(acc_tile, F32) for _ in range(n_acc)],
        compiler_params=_params(("parallel", "parallel", "arbitrary")),
    )(*[o[0] for o in a_ops], *[o[0] for o in b_ops], *[x[0] for x in extras], *([after] * nd))


def _acc0(accs, xs):
    return (accs[0],)


def mm_simple(name, a, b_op_fn, M, N, K, tm, tn, tk, out_dtype=F32, a_t=False, extras=(), epilogue=_acc0, outs=None,
              after=None):
    a_op = op_at(a, tm, tk) if a_t else op_a(a, tm, tk)
    outs = outs or [out_mn(M, N, tm, tn, out_dtype)]
    return matmul(name, (M // tm, N // tn, K // tk), [a_op], [b_op_fn(tk, tn)], [(0, 0, 0)], 1,
                  list(extras), outs, epilogue, (tm, tn), after=after)


def rms_fwd(name, x, g, width, col_blk=0, tm=256, out_dtype=BF16):
    T = x.shape[0]

    def body(x_ref, g_ref, o_ref):
        xv = x_ref[...].astype(F32)
        r = lax.rsqrt(jnp.mean(xv * xv, axis=-1, keepdims=True) + EPS)
        o_ref[...] = (xv * r * g_ref[...]).astype(o_ref.dtype)

    return pl.pallas_call(
        body, name=name, grid=(T // tm,),
        in_specs=[pl.BlockSpec((tm, width), lambda i: (i, col_blk)), pl.BlockSpec((1, width), lambda i: (0, 0))],
        out_specs=pl.BlockSpec((tm, width), lambda i: (i, 0)),
        out_shape=jax.ShapeDtypeStruct((T, width), out_dtype),
        compiler_params=_params(("parallel",)),
    )(x, g.reshape(1, width))


def rms_bwd(name, x, g, dn, width, col_blk=0, dres=None, tm=256, with_delta=False, bf16_copy=False):
    T = x.shape[0]
    has_res = dres is not None

    def body(*refs):
        x_ref, g_ref, dn_ref = refs[:3]
        pos = 3
        res_ref = None
        if has_res:
            res_ref = refs[pos]
            pos += 1
        dx_ref, dg_ref = refs[pos], refs[pos + 1]
        delta_ref = refs[pos + 2] if with_delta else None
        lo_ref = refs[-1] if bf16_copy else None
        i = pl.program_id(0)
        xv = x_ref[...].astype(F32)
        r = lax.rsqrt(jnp.mean(xv * xv, axis=-1, keepdims=True) + EPS)
        xh = xv * r
        d = dn_ref[...].astype(F32)
        gd = d * g_ref[...]
        dx = r * (gd - xh * jnp.mean(gd * xh, axis=-1, keepdims=True))
        if has_res:
            dx = dx + res_ref[...]
        dx_ref[...] = dx.astype(dx_ref.dtype)
        if bf16_copy:
            lo_ref[...] = dx.astype(BF16)
        part = jnp.sum(d * xh, axis=0, keepdims=True)

        @pl.when(i == 0)
        def _():
            dg_ref[...] = part

        @pl.when(i > 0)
        def _():
            dg_ref[...] += part

        if with_delta:
            for h in range(width // 128):
                sl = slice(h * 128, (h + 1) * 128)
                s = jnp.sum(dx[:, sl] * xv[:, sl], axis=-1, keepdims=True)
                delta_ref[:, sl] = jnp.broadcast_to(s, (tm, 128))

    in_specs = [pl.BlockSpec((tm, width), lambda i: (i, col_blk)), pl.BlockSpec((1, width), lambda i: (0, 0)),
                pl.BlockSpec((tm, width), lambda i: (i, 0))]
    args = [x, g.reshape(1, width), dn]
    if has_res:
        in_specs.append(pl.BlockSpec((tm, width), lambda i: (i, 0)))
        args.append(dres)
    out_specs = [pl.BlockSpec((tm, width), lambda i: (i, 0)), pl.BlockSpec((1, width), lambda i: (0, 0))]
    out_shape = [jax.ShapeDtypeStruct((T, width), F32), jax.ShapeDtypeStruct((1, width), F32)]
    if with_delta:
        out_specs.append(pl.BlockSpec((tm, width), lambda i: (i, 0)))
        out_shape.append(jax.ShapeDtypeStruct((T, width), F32))
    if bf16_copy:
        out_specs.append(pl.BlockSpec((tm, width), lambda i: (i, 0)))
        out_shape.append(jax.ShapeDtypeStruct((T, width), BF16))
    return pl.pallas_call(
        body, name=name, grid=(T // tm,), in_specs=in_specs, out_specs=out_specs, out_shape=out_shape,
        compiler_params=_params(("arbitrary",)),
    )(*args)


def ffn_fwd(tag, h, g, w1g, w3g, w2g, pre, w2_late=None):
    T = h.shape[0]
    n = rms_fwd(f"{tag}_rms", h, g, D_MODEL)
    tm, tn = 512, FF_PAD

    def up_epi(accs, xs):
        a1, a3 = accs
        return a1, a3, a1 * _sigmoid(a1) * a3

    a1, a3, s = matmul(
        f"{tag}_up", (T // tm, FF_P // tn, 1),
        [op_a(n, tm, D_MODEL)], [op_b_rows_t(w1g, pre, D_MODEL, tn), op_b_rows_t(w3g, pre, D_MODEL, tn)],
        [(0, 0, 0), (0, 1, 1)], 2, [],
        [out_mn(T, FF_P, tm, tn, BF16)] * 3, up_epi, (tm, tn), n_outer=True)

    if w2_late is not None:
        w2g = w2_late(s)
    tm2, tn2 = 1024, 1024
    (h_out,) = matmul(
        f"{tag}_down", (T // tm2, D_MODEL // tn2, N_CHIPS),
        [op_a(s, tm2, FF_PAD)], [op_b_rows(w2g, pre, FF_PAD, tn2)],
        [(0, 0, 0)], 1, [tile_mn(h, tm2, tn2)],
        [out_mn(T, D_MODEL, tm2, tn2, F32)], lambda accs, xs: (xs[0] + 0.5 * accs[0],), (tm2, tn2))
    return h_out, (n, a1, a3, s)


def ffn_bwd(tag, dh_out, dh_bf, h, g, res, w1g, w3g, w2g, pre, after=None, before_dw=None):
    n, a1, a3, s = res
    T = h.shape[0]
    tm, tn = 512, FF_PAD

    def act_epi(accs, xs):
        ds = 0.5 * accs[0]
        x1, x3 = xs[0].astype(F32), xs[1].astype(F32)
        sg = _sigmoid(x1)
        silu = x1 * sg
        return ds * x3 * (sg + silu * (1.0 - sg)), ds * silu

    da1, da3 = matmul(
        f"{tag}_dact", (T // tm, FF_P // tn, 1),
        [op_a(dh_bf, tm, D_MODEL)], [op_b_rows_t(w2g, pre, D_MODEL, tn)],
        [(0, 0, 0)], 1, [tile_mn(a1, tm, tn), tile_mn(a3, tm, tn)],
        [out_mn(T, FF_P, tm, tn, BF16)] * 2, act_epi, (tm, tn), n_outer=True, after=after)

    tm2, tn2 = 1024, 1024
    (dn,) = matmul(
        f"{tag}_dn", (T // tm2, D_MODEL // tn2, N_CHIPS),
        [op_a(da1, tm2, FF_PAD), op_a(da3, tm2, FF_PAD)],
        [op_b_rows(w1g, pre, FF_PAD, tn2), op_b_rows(w3g, pre, FF_PAD, tn2)],
        [(0, 0, 0), (1, 1, 0)], 1, [], [out_mn(T, D_MODEL, tm2, tn2, F32)], _acc0, (tm2, tn2))
    dh, dg, dh_lo = rms_bwd(f"{tag}_rms_bwd", h, g, dn, D_MODEL, dres=dh_out, bf16_copy=True)
    if before_dw is not None:
        after = before_dw(dh)

    tk = 1024

    def dw_t(nm, left, right, scale):
        (dw,) = matmul(
            f"{tag}_{nm}", (FF_P // FF_PAD, D_MODEL // 1024, T // tk),
            [op_at(left, FF_PAD, tk)], [op_b(right, tk, 1024)],
            [(0, 0, 0)], 1, [], [out_mn(FF_P, D_MODEL, FF_PAD, 1024, BF16)],
            lambda accs, xs: (scale * accs[0],), (FF_PAD, 1024), after=after)
        return dw

    dw2 = dw_t("dw2", s, dh_bf, 0.5)
    dw1 = dw_t("dw1", da1, n, 1.0)
    dw3 = dw_t("dw3", da3, n, 1.0)
    return dh, dh_lo, dg, dw1, dw3, dw2


def rope_tables(positions):
    inv_freq = ROPE_BASE ** (-jnp.arange(0, QK_ROPE, 2, dtype=F32) / QK_ROPE)
    ang = positions.astype(F32)[:, None] * inv_freq
    cos, sin = jnp.cos(ang), jnp.sin(ang)
    T = positions.shape[0]
    one, zero = jnp.ones((T, QK_NOPE), F32), jnp.zeros((T, 64), F32)
    z32, z128 = jnp.zeros((T, 32), F32), jnp.zeros((T, QK_NOPE), F32)
    c = jnp.concatenate([one, cos, cos, zero], axis=1)
    s1 = jnp.concatenate([z128, -sin, z32, zero], axis=1)
    s2 = jnp.concatenate([z128, z32, sin, zero], axis=1)
    return c, s1, s2


def _rope(y, c, s1, s2):
    return y * c + pltpu.roll(y, HEAD_PAD - 32, 1) * s1 + pltpu.roll(y, 32, 1) * s2


def _rope_t(d, c, s1, s2):
    return d * c + pltpu.roll(d * s1, 32, 1) + pltpu.roll(d * s2, HEAD_PAD - 32, 1)


def _head_norm(x):
    r = lax.rsqrt(jnp.sum(x * x, axis=-1, keepdims=True) * (1.0 / QK_DIM) + EPS)
    return x * r, r


def qk_prep_fwd(tag, q_raw, kk_raw, z_p, gq, gk, tabs, tm=256):
    T = q_raw.shape[0]
    c, s1, s2 = tabs

    def body(q_ref, k_ref, kr_ref, gq_ref, gk_ref, c_ref, s1_ref, s2_ref, qo_ref, ko_ref):
        cv, s1v, s2v = c_ref[...], s1_ref[...], s2_ref[...]
        kr = kr_ref[...]
        for h in range(HEADS):
            sl = slice(h * HEAD_PAD, (h + 1) * HEAD_PAD)
            xh, _ = _head_norm(q_ref[:, sl])
            qo_ref[:, sl] = (_rope(xh * gq_ref[...], cv, s1v, s2v) * ATTN_SCALE).astype(BF16)
            xh, _ = _head_norm(k_ref[:, sl] + kr)
            ko_ref[:, sl] = _rope(xh * gk_ref[...], cv, s1v, s2v).astype(BF16)

    row = lambda i: (i, 0)
    full = pl.BlockSpec((tm, HEADS * HEAD_PAD), row)
    tab = pl.BlockSpec((tm, HEAD_PAD), row)
    vec = pl.BlockSpec((1, HEAD_PAD), lambda i: (0, 0))
    return pl.pallas_call(
        body, name=f"{tag}_qk_prep", grid=(T // tm,),
        in_specs=[full, full, pl.BlockSpec((tm, HEAD_PAD), lambda i: (i, 3)), vec, vec, tab, tab, tab],
        out_specs=[full, full],
        out_shape=[jax.ShapeDtypeStruct((T, HEADS * HEAD_PAD), BF16)] * 2,
        compiler_params=_params(("parallel",)),
    )(q_raw, kk_raw, z_p, gq, gk, c, s1, s2)


def qk_prep_bwd(tag, dq_full, dk_full, q_raw, kk_raw, z_p, gq, gk, tabs, tm=256):
    T = q_raw.shape[0]
    c, s1, s2 = tabs

    def body(dq_ref, dk_ref, q_ref, k_ref, kr_ref, gq_ref, gk_ref, c_ref, s1_ref, s2_ref,
             dqr_ref, dkr_ref, dz_ref, dgq_ref, dgk_ref):
        i = pl.program_id(0)
        cv, s1v, s2v = c_ref[...], s1_ref[...], s2_ref[...]
        kr = kr_ref[...]
        lane = lax.broadcasted_iota(jnp.int32, (tm, HEAD_PAD), 1)
        slot = ((lane >= QK_NOPE) & (lane < QK_DIM)).astype(F32)

        def one(x, g, d):
            xh, r = _head_norm(x)
            dy = _rope_t(d, cv, s1v, s2v)
            gd = dy * g
            dx = r * (gd - xh * (jnp.sum(gd * xh, axis=-1, keepdims=True) * (1.0 / QK_DIM)))
            return dx, jnp.sum(dy * xh, axis=0, keepdims=True)

        dgq = jnp.zeros((1, HEAD_PAD), F32)
        dgk = jnp.zeros((1, HEAD_PAD), F32)
        dz = jnp.zeros((tm, HEAD_PAD), F32)
        for h in range(HEADS):
            sl = slice(h * HEAD_PAD, (h + 1) * HEAD_PAD)
            dx, dg = one(q_ref[:, sl], gq_ref[...], dq_ref[:, sl].astype(F32) * ATTN_SCALE)
            dqr_ref[:, sl] = dx
            dgq = dgq + dg
            dx, dg = one(k_ref[:, sl] + kr, gk_ref[...], dk_ref[:, sl].astype(F32))
            dkr_ref[:, sl] = dx
            dgk = dgk + dg
            dz = dz + dx
        dz_ref[...] = dz * slot

        @pl.when(i == 0)
        def _():
            dgq_ref[...] = dgq
            dgk_ref[...] = dgk

        @pl.when(i > 0)
        def _():
            dgq_ref[...] += dgq
            dgk_ref[...] += dgk

    row = lambda i: (i, 0)
    full = pl.BlockSpec((tm, HEADS * HEAD_PAD), row)
    tab = pl.BlockSpec((tm, HEAD_PAD), row)
    vec = pl.BlockSpec((1, HEAD_PAD), lambda i: (0, 0))
    return pl.pallas_call(
        body, name=f"{tag}_qk_prep_bwd", grid=(T // tm,),
        in_specs=[full, full, full, full, pl.BlockSpec((tm, HEAD_PAD), lambda i: (i, 3)), vec, vec, tab, tab, tab],
        out_specs=[full, full, tab, vec, vec],
        out_shape=[jax.ShapeDtypeStruct((T, HEADS * HEAD_PAD), F32)] * 2
        + [jax.ShapeDtypeStruct((T, HEAD_PAD), F32)] + [jax.ShapeDtypeStruct((1, HEAD_PAD), F32)] * 2,
        compiler_params=_params(("arbitrary",)),
    )(dq_full, dk_full, q_raw, kk_raw, z_p, gq, gk, c, s1, s2)


def attn_fwd(tag, q_full, k_full, vv, blk=512):
    T = q_full.shape[0]
    nb = T // blk
    neg = float(jnp.finfo(jnp.float32).min)

    def body(q_ref, k_ref, v_ref, o_ref, lse_ref, m_ref, l_ref, acc_ref):
        i = pl.program_id(1)
        m_ref[...] = jnp.full_like(m_ref, neg)
        l_ref[...] = jnp.zeros_like(l_ref)
        acc_ref[...] = jnp.zeros_like(acc_ref)
        q = q_ref[...]

        def step(j, masked):
            rows = pl.ds(pl.multiple_of(j * blk, blk), blk)
            s = lax.dot_general(q, k_ref[rows, :], (((1,), (1,)), ((), ())), preferred_element_type=F32)
            if masked:
                row = lax.broadcasted_iota(jnp.int32, (blk, blk), 0)
                col = lax.broadcasted_iota(jnp.int32, (blk, blk), 1)
                s = jnp.where(col <= row, s, neg)
            m_prev = m_ref[...]
            m_new = jnp.maximum(m_prev, jnp.max(s, axis=-1, keepdims=True))
            alpha = jnp.exp(m_prev - m_new)
            p = jnp.exp(s - m_new[:, :1])
            l_ref[...] = alpha * l_ref[...] + jnp.sum(p, axis=-1, keepdims=True)
            acc_ref[...] = alpha * acc_ref[...] + jnp.dot(p.astype(BF16), v_ref[rows, :], preferred_element_type=F32)
            m_ref[...] = m_new

        def off_diagonal(j, carry):
            step(j, False)
            return carry

        lax.fori_loop(0, i, off_diagonal, 0)
        step(i, True)
        o_ref[...] = acc_ref[...] / l_ref[...]
        lse_ref[...] = m_ref[...] + jnp.log(l_ref[...])

    return pl.pallas_call(
        body, name=f"{tag}_attn_fwd", grid=(HEADS, nb),
        in_specs=[pl.BlockSpec((blk, HEAD_PAD), lambda h, i: (i, h)),
                  pl.BlockSpec((T, HEAD_PAD), lambda h, i: (0, h)), pl.BlockSpec((T, V_DIM), lambda h, i: (0, h))],
        out_specs=[pl.BlockSpec((blk, V_DIM), lambda h, i: (i, h))] * 2,
        out_shape=[jax.ShapeDtypeStruct((T, ATTN_W), F32)] * 2,
        scratch_shapes=[pltpu.VMEM((blk, V_DIM), F32)] * 3,
        compiler_params=_params(("parallel", "parallel")),
    )(q_full, k_full, vv)


def attn_bwd(tag, q_full, k_full, vv, do, lse, delta, blk=512):
    T = q_full.shape[0]
    nb = T // blk
    neg = float(jnp.finfo(jnp.float32).min)

    def body(q_ref, k_ref, v_ref, do_ref, lse_ref, dl_ref, dq_ref, dk_ref, dv_ref, dk_acc, dv_acc):
        j = pl.program_id(1)

        @pl.when(j == 0)
        def _():
            dq_ref[...] = jnp.zeros_like(dq_ref)

        dk_acc[...] = jnp.zeros_like(dk_acc)
        dv_acc[...] = jnp.zeros_like(dv_acc)
        k, v = k_ref[...], v_ref[...]

        def step(i, masked):
            rows = pl.ds(pl.multiple_of(i * blk, blk), blk)
            q = q_ref[rows, :]
            s = lax.dot_general(q, k, (((1,), (1,)), ((), ())), preferred_element_type=F32)
            if masked:
                row = lax.broadcasted_iota(jnp.int32, (blk, blk), 0)
                col = lax.broadcasted_iota(jnp.int32, (blk, blk), 1)
                s = jnp.where(col <= row, s, neg)
            p = jnp.exp(s - lse_ref[rows, :1])
            dob = _bf(do_ref[rows, :])
            dv_acc[...] += lax.dot_general(p.astype(BF16), dob, (((0,), (0,)), ((), ())), preferred_element_type=F32)
            dp = lax.dot_general(dob, v, (((1,), (1,)), ((), ())), preferred_element_type=F32)
            ds = (p * (dp - dl_ref[rows, :1])).astype(BF16)
            dk_acc[...] += lax.dot_general(ds, q, (((0,), (0,)), ((), ())), preferred_element_type=F32)
            dq_ref[rows, :] += jnp.dot(ds, k, preferred_element_type=F32)

        def off_diagonal(i, carry):
            step(i, False)
            return carry

        step(j, True)
        lax.fori_loop(j + 1, nb, off_diagonal, 0)
        dk_ref[...] = dk_acc[...]
        dv_ref[...] = dv_acc[...]

    head = lambda h, j: (0, h)
    kv_ix = lambda h, j: (j, h)
    return pl.pallas_call(
        body, name=f"{tag}_attn_bwd", grid=(HEADS, nb),
        in_specs=[pl.BlockSpec((T, HEAD_PAD), head), pl.BlockSpec((blk, HEAD_PAD), kv_ix),
                  pl.BlockSpec((blk, V_DIM), kv_ix), pl.BlockSpec((T, V_DIM), head),
                  pl.BlockSpec((T, V_DIM), head), pl.BlockSpec((T, V_DIM), head)],
        out_specs=[pl.BlockSpec((T, HEAD_PAD), head),
                   pl.BlockSpec((blk, HEAD_PAD), kv_ix), pl.BlockSpec((blk, V_DIM), kv_ix)],
        out_shape=[jax.ShapeDtypeStruct((T, HEADS * HEAD_PAD), F32)] * 2 + [jax.ShapeDtypeStruct((T, ATTN_W), F32)],
        scratch_shapes=[pltpu.VMEM((blk, HEAD_PAD), F32), pltpu.VMEM((blk, V_DIM), F32)],
        compiler_params=_params(("parallel", "arbitrary")),
    )(q_full, k_full, vv, do, lse, delta)


def _gm_forward(u, v, gv, wc_ref, bb_ref, nchunk):
    ug = _gelu(u)
    vg = _gelu(v)
    rv = lax.rsqrt(jnp.mean(vg * vg, axis=-1, keepdims=True) + EPS)
    vhat = vg * rv
    vn = (vhat * gv).astype(BF16)
    gates = []
    for cidx in range(nchunk):
        rows = slice(cidx * CHUNK, (cidx + 1) * CHUNK)
        gates.append(jnp.concatenate(
            [jnp.dot(wc_ref[gidx], vn[rows, gidx * 128:(gidx + 1) * 128], preferred_element_type=F32) + bb_ref[gidx]
             for gidx in range(GROUPS)], axis=1))
    gate = jnp.concatenate(gates, axis=0)
    return ug, vhat, rv, vn, gate


def gmlp_fwd(tag, z_p, gv, gout, wc, bb, tm=256):
    T = z_p.shape[0]
    nchunk = tm // CHUNK

    def body(u_ref, v_ref, gv_ref, go_ref, wc_ref, bb_ref, o_ref):
        ug, _, _, _, gate = _gm_forward(u_ref[...], v_ref[...], gv_ref[...], wc_ref, bb_ref, nchunk)
        go = ug * gate
        ro = lax.rsqrt(jnp.mean(go * go, axis=-1, keepdims=True) + EPS)
        o_ref[...] = (go * ro * go_ref[...]).astype(BF16)

    vec = pl.BlockSpec((1, GM_W), lambda i: (0, 0))
    w3 = pl.BlockSpec((GROUPS, CHUNK, CHUNK), lambda i: (0, 0, 0))
    return pl.pallas_call(
        body, name=f"{tag}_gmlp_fwd", grid=(T // tm,),
        in_specs=[pl.BlockSpec((tm, GM_W), lambda i: (i, 1)), pl.BlockSpec((tm, GM_W), lambda i: (i, 2)), vec, vec, w3, w3],
        out_specs=pl.BlockSpec((tm, GM_W), lambda i: (i, 0)),
        out_shape=jax.ShapeDtypeStruct((T, GM_W), BF16),
        compiler_params=_params(("parallel",)),
    )(z_p, z_p, gv.reshape(1, GM_W), gout.reshape(1, GM_W), wc, bb)


def gmlp_bwd(tag, z_p, dmixed, gv, gout, wc, bb, tm=256):
    T = z_p.shape[0]
    nchunk = tm // CHUNK

    def body(u_ref, v_ref, dm_ref, gv_ref, go_ref, wc_ref, bb_ref, du_ref, dv_ref, dwc_ref, dbb_ref, dgv_ref, dgo_ref):
        i = pl.program_id(0)
        u, v = u_ref[...], v_ref[...]
        ug, vhat, rv, vn, gate = _gm_forward(u, v, gv_ref[...], wc_ref, bb_ref, nchunk)
        go = ug * gate
        ro = lax.rsqrt(jnp.mean(go * go, axis=-1, keepdims=True) + EPS)
        ohat = go * ro
        dm = dm_ref[...].astype(F32)
        dgo_part = jnp.sum(dm * ohat, axis=0, keepdims=True)
        doh = dm * go_ref[...]
        dgo = ro * (doh - ohat * jnp.mean(doh * ohat, axis=-1, keepdims=True))
        du_ref[...] = dgo * gate * _gelu_grad(u)
        dgate = dgo * ug
        dgb = dgate.astype(BF16)
        dvn_rows = []
        dwc_parts = []
        dbb_parts = []
        for gidx in range(GROUPS):
            cols = slice(gidx * 128, (gidx + 1) * 128)
            dw = jnp.zeros((CHUNK, CHUNK), F32)
            db = jnp.zeros((CHUNK, 128), F32)
            for cidx in range(nchunk):
                rows = slice(cidx * CHUNK, (cidx + 1) * CHUNK)
                dw = dw + lax.dot_general(dgb[rows, cols], vn[rows, cols], (((1,), (1,)), ((), ())),
                                          preferred_element_type=F32)
                db = db + dgate[rows, cols]
            dwc_parts.append(dw)
            dbb_parts.append(db)
        for cidx in range(nchunk):
            rows = slice(cidx * CHUNK, (cidx + 1) * CHUNK)
            dvn_rows.append(jnp.concatenate(
                [lax.dot_general(wc_ref[gidx], dgb[rows, gidx * 128:(gidx + 1) * 128], (((0,), (0,)), ((), ())),
                                 preferred_element_type=F32) for gidx in range(GROUPS)], axis=1))
        dvn = jnp.concatenate(dvn_rows, axis=0)
        dgv_part = jnp.sum(dvn * vhat, axis=0, keepdims=True)
        dvh = dvn * gv_ref[...]
        dvg = rv * (dvh - vhat * jnp.mean(dvh * vhat, axis=-1, keepdims=True))
        dv_ref[...] = dvg * _gelu_grad(v)

        @pl.when(i == 0)
        def _():
            for gidx in range(GROUPS):
                dwc_ref[gidx] = dwc_parts[gidx]
                dbb_ref[gidx] = dbb_parts[gidx]
            dgv_ref[...] = dgv_part
            dgo_ref[...] = dgo_part

        @pl.when(i > 0)
        def _():
            for gidx in range(GROUPS):
                dwc_ref[gidx] += dwc_parts[gidx]
                dbb_ref[gidx] += dbb_parts[gidx]
            dgv_ref[...] += dgv_part
            dgo_ref[...] += dgo_part

    vec = pl.BlockSpec((1, GM_W), lambda i: (0, 0))
    w3 = pl.BlockSpec((GROUPS, CHUNK, CHUNK), lambda i: (0, 0, 0))
    blk = pl.BlockSpec((tm, GM_W), lambda i: (i, 0))
    return pl.pallas_call(
        body, name=f"{tag}_gmlp_bwd", grid=(T // tm,),
        in_specs=[pl.BlockSpec((tm, GM_W), lambda i: (i, 1)), pl.BlockSpec((tm, GM_W), lambda i: (i, 2)),
                  pl.BlockSpec((tm, GM_W), lambda i: (i, 1)), vec, vec, w3, w3],
        out_specs=[blk, blk, w3, w3, vec, vec],
        out_shape=[jax.ShapeDtypeStruct((T, GM_W), F32)] * 2 + [jax.ShapeDtypeStruct((GROUPS, CHUNK, CHUNK), F32)] * 2
        + [jax.ShapeDtypeStruct((1, GM_W), F32)] * 2,
        compiler_params=_params(("arbitrary",)),
    )(z_p, z_p, dmixed, gv.reshape(1, GM_W), gout.reshape(1, GM_W), wc, bb)


def mixer_fwd(tag, h, w, tabs, wout_g, pre):
    T = h.shape[0]
    n2 = rms_fwd(f"{tag}_mix_rms", h, w["mix_norm"], D_MODEL)
    (z_p,) = mm_simple(f"{tag}_win", n2, lambda tk, tn: op_bt(w["w_in_pt"], tk, tn), T, IN_P, D_MODEL, 512, 1024, D_MODEL)
    cqn = rms_fwd(f"{tag}_cq_rms", z_p, w["q_a_norm"], Q_RANK, col_blk=0)
    ckvn = rms_fwd(f"{tag}_ckv_rms", z_p, w["kv_a_norm"], KV_RANK, col_blk=2)
    (q_raw,) = mm_simple(f"{tag}_wq", cqn, lambda tk, tn: op_b(w["wq_p"], tk, tn), T, 2048, Q_RANK, 512, 1024, Q_RANK)
    (kk_raw,) = mm_simple(f"{tag}_wk", ckvn, lambda tk, tn: op_b(w["wk_p"], tk, tn), T, 2048, KV_RANK, 512, 1024, KV_RANK)
    (vv,) = mm_simple(f"{tag}_wv", ckvn, lambda tk, tn: op_b(w["wv"], tk, tn), T, ATTN_W, KV_RANK, 512, 1024, KV_RANK,
                      out_dtype=BF16)
    q_full, k_full = qk_prep_fwd(tag, q_raw, kk_raw, z_p, w["gq_p"], w["gk_p"], tabs)
    a_out, lse = attn_fwd(tag, q_full, k_full, vv)
    mixed_a = rms_fwd(f"{tag}_ao_rms", a_out, w["attn_out_norm"], ATTN_W)
    mixed_g = gmlp_fwd(tag, z_p, w["gm_v_norm"], w["gm_out_norm"], w["wc"], w["bb"])
    tm, tn, tk = 512, 1024, 512
    (h2,) = matmul(
        f"{tag}_wout", (T // tm, D_MODEL // tn, ATTN_W // tk),
        [op_a(mixed_a, tm, tk), op_a(mixed_g, tm, tk)],
        [op_b_rows(wout_g, pre, tk, tn), op_b_rows(wout_g, pre, tk, tn, koff=ATTN_W // tk)],
        [(0, 0, 0), (1, 1, 0)], 1, [tile_mn(h, tm, tn)], [out_mn(T, D_MODEL, tm, tn, F32)],
        lambda accs, xs: (xs[0] + accs[0],), (tm, tn))
    res = dict(n2=n2, z_p=z_p, cqn=cqn, ckvn=ckvn, q_raw=q_raw, kk_raw=kk_raw, vv=vv, q_full=q_full, k_full=k_full,
               a_out=a_out, lse=lse, mixed_a=mixed_a, mixed_g=mixed_g)
    return h2, res


def mixer_bwd(tag, dh2, dh2_bf, h, w, tabs, wout_g, pre, r, after=None):
    T = h.shape[0]
    g = {}
    (dmixed,) = mm_simple(f"{tag}_dmixed", dh2_bf, lambda tk, tn: op_b_rows_t(wout_g, pre, tk, tn), T, D_MODEL, D_MODEL,
                          512, 512, D_MODEL, after=after)
    (dwo_a,) = mm_simple(f"{tag}_dwout_a", r["mixed_a"], lambda tk, tn: op_b(dh2_bf, tk, tn), ATTN_W, D_MODEL, T,
                         1024, 1024, 512, a_t=True, out_dtype=BF16)
    (dwo_g,) = mm_simple(f"{tag}_dwout_g", r["mixed_g"], lambda tk, tn: op_b(dh2_bf, tk, tn), GM_W, D_MODEL, T,
                         1024, 1024, 512, a_t=True, out_dtype=BF16)
    g["w_out"] = jnp.concatenate([dwo_a, dwo_g], axis=0)
    da_out, g["attn_out_norm"], delta = rms_bwd(f"{tag}_ao_rms_bwd", r["a_out"], w["attn_out_norm"], dmixed, ATTN_W,
                                                with_delta=True)
    dq_full, dk_full, dvv = attn_bwd(tag, r["q_full"], r["k_full"], r["vv"], da_out, r["lse"], delta)
    dq_raw, dkk_raw, dzkr, g["gq_p"], g["gk_p"] = qk_prep_bwd(tag, dq_full, dk_full, r["q_raw"], r["kk_raw"], r["z_p"],
                                                            w["gq_p"], w["gk_p"], tabs)
    (g["wq_p"],) = mm_simple(f"{tag}_dwq", r["cqn"], lambda tk, tn: op_b(dq_raw, tk, tn), Q_RANK, 2048, T, Q_RANK, 1024, 512,
                             a_t=True, out_dtype=BF16)
    (g["wk_p"],) = mm_simple(f"{tag}_dwk", r["ckvn"], lambda tk, tn: op_b(dkk_raw, tk, tn), KV_RANK, 2048, T, KV_RANK, 1024,
                             512, a_t=True, out_dtype=BF16)
    (g["wv"],) = mm_simple(f"{tag}_dwv", r["ckvn"], lambda tk, tn: op_b(dvv, tk, tn), KV_RANK, ATTN_W, T, KV_RANK, 1024, 512,
                           a_t=True, out_dtype=BF16)
    (dcqn,) = mm_simple(f"{tag}_dcqn", dq_raw, lambda tk, tn: op_bt(w["wq_p"], tk, tn), T, Q_RANK, 2048, 512, Q_RANK, 2048)
    (dck1,) = mm_simple(f"{tag}_dckvn_k", dkk_raw, lambda tk, tn: op_bt(w["wk_p"], tk, tn), T, KV_RANK, 2048, 512, KV_RANK,
                        2048)
    (dckvn,) = mm_simple(f"{tag}_dckvn_v", dvv, lambda tk, tn: op_bt(w["wv"], tk, tn), T, KV_RANK, ATTN_W, 512, KV_RANK,
                         ATTN_W, extras=[tile_mn(dck1, 512, KV_RANK)], epilogue=lambda accs, xs: (accs[0] + xs[0],))
    dc_q, g["q_a_norm"] = rms_bwd(f"{tag}_cq_rms_bwd", r["z_p"], w["q_a_norm"], dcqn, Q_RANK, col_blk=0)
    dc_kv, g["kv_a_norm"] = rms_bwd(f"{tag}_ckv_rms_bwd", r["z_p"], w["kv_a_norm"], dckvn, KV_RANK, col_blk=2)
    du, dv, g["wc"], g["bb"], g["gm_v_norm"], g["gm_out_norm"] = gmlp_bwd(
        tag, r["z_p"], dmixed, w["gm_v_norm"], w["gm_out_norm"], w["wc"], w["bb"])
    dz_p = jnp.concatenate([dc_q, dc_kv, dzkr, du, dv], axis=1).astype(BF16)
    (g["w_in_pt"],) = mm_simple(f"{tag}_dwin", dz_p, lambda tk, tn: op_b(r["n2"], tk, tn), IN_P, D_MODEL, T, 1024, 1024, 512,
                                a_t=True, out_dtype=BF16)
    (dn2,) = mm_simple(f"{tag}_dn2", dz_p, lambda tk, tn: op_b(w["w_in_pt"], tk, tn), T, D_MODEL, IN_P, 512, 1024, IN_P)
    dh1, g["mix_norm"], dh1_bf = rms_bwd(f"{tag}_mix_rms_bwd", h, w["mix_norm"], dn2, D_MODEL, dres=dh2, bf16_copy=True)
    return dh1, dh1_bf, g


def ple_fwd(tag, h3, p_l, w, wpg_g, wple_g, pre):
    T = h3.shape[0]
    (pw,) = mm_simple(f"{tag}_wple", p_l, lambda tk, tn: op_b_cols(wple_g, pre, tk, tn), T, D_MODEL, PLE_DIM, 512, 512,
                      PLE_DIM)
    e = rms_fwd(f"{tag}_ple_rms", pw, w["ple_norm"], D_MODEL, out_dtype=F32)
    n4 = rms_fwd(f"{tag}_pg_rms", h3, w["ple_gate_norm"], D_MODEL)

    def epi(accs, xs):
        gt = _sigmoid(accs[0])
        return xs[0] + gt * xs[1], gt

    tm, tn, tk = 512, 1024, 512
    h4, gate = matmul(
        f"{tag}_wpg", (T // tm, D_MODEL // tn, D_MODEL // tk),
        [op_a(n4, tm, tk)], [op_b_rows(wpg_g, pre, tk, tn)], [(0, 0, 0)], 1,
        [tile_mn(h3, tm, tn), tile_mn(e, tm, tn)],
        [out_mn(T, D_MODEL, tm, tn, F32), out_mn(T, D_MODEL, tm, tn, BF16)], epi, (tm, tn))
    return h4, dict(pw=pw, e=e, n4=n4, gate=gate)


def ple_bwd(tag, dh4, h3, p_l, w, wpg_g, wple_g, pre, r, tm=256):
    T = h3.shape[0]

    def act_body(d_ref, g_ref, e_ref, dpre_ref, de_ref):
        d, gt = d_ref[...], g_ref[...].astype(F32)
        dpre_ref[...] = (d * e_ref[...] * gt * (1.0 - gt)).astype(BF16)
        de_ref[...] = d * gt

    blk = pl.BlockSpec((tm, D_MODEL), lambda i: (i, 0))
    dpre, de = pl.pallas_call(
        act_body, name=f"{tag}_ple_act_bwd", grid=(T // tm,), in_specs=[blk, blk, blk], out_specs=[blk, blk],
        out_shape=[jax.ShapeDtypeStruct((T, D_MODEL), BF16), jax.ShapeDtypeStruct((T, D_MODEL), F32)],
        compiler_params=_params(("parallel",)),
    )(dh4, r["gate"], r["e"])
    g = {}
    (g["w_ple_gate"],) = mm_simple(f"{tag}_dwpg", r["n4"], lambda tk, tn: op_b(dpre, tk, tn), D_MODEL, D_MODEL, T,
                                   1024, 1024, 512, a_t=True, out_dtype=BF16)
    (dn4,) = mm_simple(f"{tag}_dn4", dpre, lambda tk, tn: op_b_rows_t(wpg_g, pre, tk, tn), T, D_MODEL, D_MODEL, 512, 512,
                       D_MODEL)
    dh3, g["ple_gate_norm"], dh3_bf = rms_bwd(f"{tag}_pg_rms_bwd", h3, w["ple_gate_norm"], dn4, D_MODEL, dres=dh4,
                                              bf16_copy=True)
    dpw, g["ple_norm"] = rms_bwd(f"{tag}_ple_rms_bwd", r["pw"], w["ple_norm"], de, D_MODEL)
    (g["w_ple"],) = mm_simple(f"{tag}_dwple", p_l, lambda tk, tn: op_b(dpw, tk, tn), PLE_DIM, D_MODEL, T, PLE_DIM, 512, 512,
                              a_t=True, outs=[out_cols(PLE_DIM, 512, PLE_DIM, 512, BF16)])
    return dh3, dh3_bf, g


def loss_grad(y, target, tm=256):
    T = y.shape[0]

    def body(y_ref, t_ref, dy_ref, l_ref):
        i = pl.program_id(0)
        d = y_ref[...] - t_ref[...]
        dy_ref[...] = d * (1.0 / D_MODEL)
        part = jnp.sum((d * d).reshape(tm // 8, 8, D_MODEL), axis=0)

        @pl.when(i == 0)
        def _():
            l_ref[...] = part

        @pl.when(i > 0)
        def _():
            l_ref[...] += part

    blk = pl.BlockSpec((tm, D_MODEL), lambda i: (i, 0))
    dy, part = pl.pallas_call(
        body, name="loss_grad", grid=(T // tm,), in_specs=[blk, blk],
        out_specs=[blk, pl.BlockSpec((8, D_MODEL), lambda i: (0, 0))],
        out_shape=[jax.ShapeDtypeStruct((T, D_MODEL), F32), jax.ShapeDtypeStruct((8, D_MODEL), F32)],
        compiler_params=_params(("arbitrary",)),
    )(y, target)
    return dy, 0.5 * jnp.sum(part) / D_MODEL


def _unshard_cols(g_l):
    return g_l.transpose(1, 0, 2).reshape(g_l.shape[1], -1)


def _shard_cols(w):
    return w.reshape(w.shape[0], N_CHIPS, -1).transpose(1, 0, 2)


def layer_weights(l, Gl, small):
    w = {k: small[k][l] for k in ("mix_norm", "q_a_norm", "kv_a_norm", "gm_v_norm", "attn_out_norm", "gm_out_norm",
                                  "ple_gate_norm", "ple_norm")}
    wint = Gl["w_in"][:, :IN_SHARD].reshape(-1, D_MODEL)
    z = lambda n: jnp.zeros((n, D_MODEL), BF16)
    w["w_in_pt"] = jnp.concatenate([wint[:768], z(128), wint[768:832], z(64), wint[832:]], axis=0)
    wuq = _unshard_cols(Gl["w_uq"]).reshape(Q_RANK, HEADS, QK_DIM)
    w["wq_p"] = jnp.pad(wuq, ((0, 0), (0, 0), (0, HEAD_PAD - QK_DIM))).reshape(Q_RANK, HEADS * HEAD_PAD)
    wukv = _unshard_cols(Gl["w_ukv"]).reshape(KV_RANK, HEADS, QK_NOPE + V_DIM)
    w["wk_p"] = jnp.pad(wukv[:, :, :QK_NOPE], ((0, 0), (0, 0), (0, HEAD_PAD - QK_NOPE))).reshape(KV_RANK, HEADS * HEAD_PAD)
    w["wv"] = wukv[:, :, QK_NOPE:].reshape(KV_RANK, ATTN_W)
    w["gq_p"] = jnp.pad(small["q_norm"][l], (0, HEAD_PAD - QK_DIM)).reshape(1, HEAD_PAD)
    w["gk_p"] = jnp.pad(small["k_norm"][l], (0, HEAD_PAD - QK_DIM)).reshape(1, HEAD_PAD)
    tril = jnp.tril(jnp.ones((CHUNK, CHUNK), dtype=bool))
    w["wc"] = jnp.where(tril[None], small["gm_ws"][l], 0.0).astype(BF16)
    w["bb"] = jnp.broadcast_to(small["gm_bs"][l][:, :, None], (GROUPS, CHUNK, 128)).astype(F32)
    return w


def mixer_grads_to_shards(g):
    out = {}
    dwint = g["w_in_pt"]
    dwint = jnp.concatenate([dwint[:768], dwint[896:960], dwint[1024:]], axis=0).reshape(N_CHIPS, IN_SHARD, D_MODEL)
    out["w_in"] = jnp.pad(dwint, ((0, 0), (0, IN_SHARD_PAD - IN_SHARD), (0, 0)))
    dwuq = g["wq_p"].reshape(Q_RANK, HEADS, HEAD_PAD)[:, :, :QK_DIM].reshape(Q_RANK, HEADS * QK_DIM)
    out["w_uq"] = _shard_cols(dwuq)
    dwukv = jnp.concatenate([g["wk_p"].reshape(KV_RANK, HEADS, HEAD_PAD)[:, :, :QK_NOPE],
                             g["wv"].reshape(KV_RANK, HEADS, V_DIM)], axis=-1).reshape(KV_RANK, HEADS * (QK_NOPE + V_DIM))
    out["w_ukv"] = _shard_cols(dwukv)
    out["w_out"] = g["w_out"].reshape(N_CHIPS, D_MODEL // N_CHIPS, D_MODEL)
    out["q_norm"] = g["gq_p"][0, :QK_DIM]
    out["k_norm"] = g["gk_p"][0, :QK_DIM]
    tril = jnp.tril(jnp.ones((CHUNK, CHUNK), dtype=bool))
    out["gm_ws"] = jnp.where(tril[None], g["wc"], 0.0)
    out["gm_bs"] = jnp.sum(g["bb"], axis=-1)
    for k in ("mix_norm", "q_a_norm", "kv_a_norm", "gm_v_norm", "attn_out_norm", "gm_out_norm"):
        out[k] = g[k][0]
    return out


def layer_fwd(l, h, p_l, Gl, small, tabs, before=None):
    before = before or {}
    h1, r_a = ffn_fwd(f"l{l}a", h, small["ffn_a_norm"][l], Gl["ffn_a_w1"], Gl["ffn_a_w3"], Gl.get("ffn_a_w2"), (),
                      before.get("down_a"))
    if "mixer" in before:
        Gl, small = before["mixer"](h1, Gl, small)
    w = layer_weights(l, Gl, small)
    h2, r_m = mixer_fwd(f"l{l}", h1, w, tabs, Gl["w_out"], ())
    if "ffn_b" in before:
        Gl = before["ffn_b"](h2, Gl)
    h3, r_b = ffn_fwd(f"l{l}b", h2, small["ffn_b_norm"][l], Gl["ffn_b_w1"], Gl["ffn_b_w3"], Gl["ffn_b_w2"], ())
    if "ple" in before:
        w = {**w, "ple_norm": w["ple_norm"] + before["ple"](h3)[0, 0]}
    h4, r_p = ple_fwd(f"l{l}", h3, p_l, w, Gl["w_ple_gate"], Gl["w_ple"], ())
    return h4, (w, h, h1, h2, h3, r_a, r_m, r_b, r_p)


def layer_bwd(l, dh, p_l, Gl, small, tabs, saved, before=None):
    w, h0, h1, h2, h3, r_a, r_m, r_b, r_p = saved
    slabs = lambda d: d.reshape(N_CHIPS, FF_PAD, D_MODEL)
    hook = lambda block: before[block](gl, dh) if before and block in before else None
    gl = {}
    dh, dh_bf, g_p = ple_bwd(f"l{l}", dh, h3, p_l, w, Gl["w_ple_gate"], Gl["w_ple"], (), r_p)
    gl["w_ple_gate"] = g_p["w_ple_gate"].reshape(N_CHIPS, D_MODEL // N_CHIPS, D_MODEL)
    gl["w_ple"] = g_p["w_ple"]
    gl["ple_gate_norm"], gl["ple_norm"] = g_p["ple_gate_norm"][0], g_p["ple_norm"][0]
    dh, dh_bf, dg, dw1, dw3, dw2 = ffn_bwd(f"l{l}b", dh, dh_bf, h2, small["ffn_b_norm"][l], r_b,
                                           Gl["ffn_b_w1"], Gl["ffn_b_w3"], Gl["ffn_b_w2"], (), hook("ffn_b"))
    gl["ffn_b_norm"] = dg[0]
    gl["ffn_b_w1"], gl["ffn_b_w3"], gl["ffn_b_w2"] = slabs(dw1), slabs(dw3), slabs(dw2)
    dh, dh_bf, g_m = mixer_bwd(f"l{l}", dh, dh_bf, h1, w, tabs, Gl["w_out"], (), r_m, hook("mixer"))
    gl.update(mixer_grads_to_shards(g_m))
    last_dw = (lambda dh_: before["ffn_a_dw"](gl, dh_)) if before and "ffn_a_dw" in before else None
    dh, _, dg, dw1, dw3, dw2 = ffn_bwd(f"l{l}a", dh, dh_bf, h0, small["ffn_a_norm"][l], r_a,
                                       Gl["ffn_a_w1"], Gl["ffn_a_w3"], Gl["ffn_a_w2"], (), hook("ffn_a"), last_dw)
    gl["ffn_a_norm"] = dg[0]
    gl["ffn_a_w1"], gl["ffn_a_w3"], gl["ffn_a_w2"] = slabs(dw1), slabs(dw3), slabs(dw2)
    return dh, gl


MESH = pl.DeviceIdType.MESH
HBM_SPEC = pl.BlockSpec(memory_space=pltpu.HBM)


def _place():
    x, y, c = lax.axis_index("x"), lax.axis_index("y"), lax.axis_index("c")
    others = [(1 - x, y), (x, 1 - y), (1 - x, 1 - y)]
    return x, y, c, 2 * x + y, others


def prep_shard(name, w, layer, rows_pad, place, after=None):
    _, ks, n = w.shape
    ksp = ks + rows_pad
    tc = 512 if n % 512 == 0 else n
    deps = [] if after is None else [after]

    def body(place_ref, x_ref, *rest):
        o_ref = rest[-1]
        o_ref[:ks] = x_ref[...].astype(BF16)
        if rows_pad:
            o_ref[ks:] = jnp.zeros((rows_pad, tc), BF16)

    return pl.pallas_call(
        body, name=name,
        grid_spec=pltpu.PrefetchScalarGridSpec(
            num_scalar_prefetch=1, grid=(n // tc,),
            in_specs=[pl.BlockSpec((None, ks, tc), lambda i, s: (layer, 0, i))] + [ANY_SPEC] * len(deps),
            out_specs=pl.BlockSpec((None, ksp, tc), lambda i, s: (s[0], 0, i))),
        out_shape=jax.ShapeDtypeStruct((N_CHIPS, ksp, n), BF16),
        compiler_params=_params(("parallel",)),
    )(place, w, *deps)


def share_halves(name, fulls):
    n = len(fulls)

    def body(*refs):
        o_refs = refs[n:2 * n]
        send, recv = refs[2 * n:]
        x, y, c, _, _ = _place()
        cps = []
        for w in range(n):
            kh = fulls[w].shape[0] // 2
            half = o_refs[w].at[pl.ds(c * kh, kh)]
            cps.append(pltpu.make_async_remote_copy(src_ref=half, dst_ref=half, send_sem=send.at[w], recv_sem=recv.at[w],
                                                    device_id=(x, y, 1 - c), device_id_type=MESH))
        for cp in cps:
            cp.start()
        for cp in cps:
            cp.wait()

    return pl.pallas_call(
        body, name=name, in_specs=[HBM_SPEC] * n, out_specs=[HBM_SPEC] * n,
        out_shape=[jax.ShapeDtypeStruct(f.shape, f.dtype) for f in fulls],
        input_output_aliases={w: w for w in range(n)},
        scratch_shapes=[pltpu.SemaphoreType.DMA((n,))] * 2,
    )(*fulls)


SEM_SPEC = pl.BlockSpec(memory_space=pltpu.SEMAPHORE)
ANY_SPEC = pl.BlockSpec(memory_space=pl.ANY)
DATAFLOW = pltpu.SideEffectType.DATAFLOW_SIDE_EFFECTING


def _hbm(x):
    return pltpu.with_memory_space_constraint(x, pltpu.HBM)


def _start_call(name, slots, after, issue):
    n = len(slots)
    deps = [] if after is None else [after]
    nd = len(deps)

    def body(*refs):
        issue(refs[n + nd + 2:2 * n + nd + 2], refs[n + nd], refs[n + nd + 1])
        token = refs[2 * n + nd + 2]
        token[...] = jnp.zeros_like(token)

    outs = pl.pallas_call(
        body, name=name,
        in_specs=[HBM_SPEC] * n + [ANY_SPEC] * nd,
        out_specs=(SEM_SPEC, SEM_SPEC, *([HBM_SPEC] * n), pl.BlockSpec(memory_space=pltpu.VMEM)),
        out_shape=(pltpu.SemaphoreType.DMA((n,)), pltpu.SemaphoreType.DMA((n,)),
                   *[pltpu.HBM(s.shape, s.dtype) for s in slots], jax.ShapeDtypeStruct((8, 128), F32)),
        input_output_aliases={w: w + 2 for w in range(n)},
        compiler_params=pltpu.CompilerParams(has_side_effects=DATAFLOW),
    )(*[_hbm(s) for s in slots], *deps)
    return outs[0], outs[1], list(outs[2:2 + n]), outs[2 + n]


def gather_start(name, slots, after):
    def issue(g_refs, send, recv):
        x, y, c, jme, others = _place()
        for w in range(len(slots)):
            kh = slots[w].shape[1] // 2
            mine = g_refs[w].at[jme, pl.ds(c * kh, kh)]
            for (px, py) in others:
                pltpu.make_async_remote_copy(src_ref=mine, dst_ref=mine, send_sem=send.at[w], recv_sem=recv.at[w],
                                             device_id=(px, py, c), device_id_type=MESH).start()

    return _start_call(name, slots, after, issue)


def forward_start(name, slots):
    def issue(g_refs, send, recv):
        x, y, c, _, others = _place()
        for w in range(len(slots)):
            kh = slots[w].shape[1] // 2
            for (px, py) in others:
                blk = g_refs[w].at[2 * px + py, pl.ds(c * kh, kh)]
                pltpu.make_async_remote_copy(src_ref=blk, dst_ref=blk, send_sem=send.at[w], recv_sem=recv.at[w],
                                             device_id=(x, y, 1 - c), device_id_type=MESH).start()

    return _start_call(name, slots, None, issue)


def gather_wait(name, send, recv, flying, after):
    n = len(flying)

    def body(*refs):
        send_ref, recv_ref = refs[n], refs[n + 1]
        g_refs = refs[n + 3:]
        x, y, c, _, _ = _place()
        for w in range(n):
            three = g_refs[w].at[pl.ds(0, 3), pl.ds(0, flying[w].shape[1] // 2)]
            cp = pltpu.make_async_remote_copy(src_ref=three, dst_ref=three, send_sem=send_ref.at[w], recv_sem=recv_ref.at[w],
                                              device_id=(x, y, 1 - c), device_id_type=MESH)
            cp.wait_send()
            cp.wait_recv()

    return pl.pallas_call(
        body, name=name,
        in_specs=[HBM_SPEC] * n + [SEM_SPEC, SEM_SPEC, ANY_SPEC],
        out_specs=[HBM_SPEC] * n,
        out_shape=[pltpu.HBM(s.shape, s.dtype) for s in flying],
        input_output_aliases={w: w for w in range(n)},
        compiler_params=pltpu.CompilerParams(has_side_effects=DATAFLOW),
    )(*flying, send, recv, after)


def _send_start(name, srcs, land_shapes, issue, after):
    n = len(srcs)
    deps = [] if after is None else [after]
    nd = len(deps)

    def body(*refs):
        base = 2 * n + nd
        issue(refs[base + 2:base + 2 + n], refs[base + 2 + n:base + 2 + 2 * n], refs[base], refs[base + 1])
        token = refs[base + 2 + 2 * n]
        token[...] = jnp.zeros_like(token)

    lands = [_hbm(lax.empty(shape, s.dtype)) for shape, s in zip(land_shapes, srcs)]
    outs = pl.pallas_call(
        body, name=name,
        in_specs=[HBM_SPEC] * (2 * n) + [ANY_SPEC] * nd,
        out_specs=(SEM_SPEC, SEM_SPEC, *([HBM_SPEC] * (2 * n)), pl.BlockSpec(memory_space=pltpu.VMEM)),
        out_shape=(pltpu.SemaphoreType.DMA((n,)), pltpu.SemaphoreType.DMA((n,)),
                   *[pltpu.HBM(s.shape, s.dtype) for s in srcs], *[pltpu.HBM(l.shape, l.dtype) for l in lands],
                   jax.ShapeDtypeStruct((8, 128), F32)),
        input_output_aliases={w: w + 2 for w in range(2 * n)},
        compiler_params=pltpu.CompilerParams(has_side_effects=DATAFLOW),
    )(*[_hbm(s) for s in srcs], *lands, *deps)
    return outs[0], outs[1], list(outs[2:2 + n]), list(outs[2 + n:2 + 2 * n]), outs[2 + 2 * n]


def _send_wait(name, send, recv, srcs, lands, after, landed):
    n = len(srcs)

    def body(*refs):
        send_ref, recv_ref = refs[2 * n], refs[2 * n + 1]
        q_refs = refs[3 * n + 3:]
        x, y, c, _, _ = _place()
        for w in range(n):
            cp = pltpu.make_async_remote_copy(src_ref=landed(q_refs[w]), dst_ref=landed(q_refs[w]), send_sem=send_ref.at[w],
                                              recv_sem=recv_ref.at[w], device_id=(x, y, 1 - c), device_id_type=MESH)
            cp.wait_send()
            cp.wait_recv()

    outs = pl.pallas_call(
        body, name=name,
        in_specs=[HBM_SPEC] * (2 * n) + [SEM_SPEC, SEM_SPEC, ANY_SPEC],
        out_specs=[HBM_SPEC] * (2 * n),
        out_shape=[pltpu.HBM(a.shape, a.dtype) for a in list(srcs) + list(lands)],
        input_output_aliases={w: w for w in range(2 * n)},
        compiler_params=pltpu.CompilerParams(has_side_effects=DATAFLOW),
    )(*srcs, *lands, send, recv, after)
    return list(outs[:n]), list(outs[n:])


def exchange_start(name, grads, after):
    def issue(d_refs, r_refs, send, recv):
        x, y, c, _, _ = _place()
        for w in range(len(grads)):
            half = grads[w].shape[1] // 2
            pltpu.make_async_remote_copy(
                src_ref=d_refs[w].at[pl.ds(0, N_CHIPS), pl.ds((1 - c) * half, half)], dst_ref=r_refs[w],
                send_sem=send.at[w], recv_sem=recv.at[w], device_id=(x, y, 1 - c), device_id_type=MESH).start()

    return _send_start(name, grads, [(N_CHIPS, g.shape[1] // 2, g.shape[2]) for g in grads], issue, after)


def exchange_wait(name, send, recv, grads, lands, after):
    return _send_wait(name, send, recv, grads, lands, after, lambda r: r)


def scatter_start(name, parts):
    def issue(p_refs, q_refs, send, recv):
        x, y, c, jme, others = _place()
        for w in range(len(parts)):
            for (px, py) in others:
                pltpu.make_async_remote_copy(
                    src_ref=p_refs[w].at[2 * px + py], dst_ref=q_refs[w].at[jme], send_sem=send.at[w], recv_sem=recv.at[w],
                    device_id=(px, py, c), device_id_type=MESH).start()

    return _send_start(name, parts, [p.shape for p in parts], issue, None)


def scatter_wait(name, send, recv, parts, lands, after):
    return _send_wait(name, send, recv, parts, lands, after, lambda r: r.at[pl.ds(0, 3)])


def allreduce_small(v):
    R = v.shape[0]

    def body(v_ref, o_ref, sib_ref, mine_ref, all_ref, d_send, d_recv, i_send, i_recv):
        x, y, c, jme, others = _place()
        swap = pltpu.make_async_remote_copy(src_ref=v_ref, dst_ref=sib_ref, send_sem=d_send, recv_sem=d_recv,
                                            device_id=(x, y, 1 - c), device_id_type=MESH)
        swap.start()
        swap.wait()
        mine_ref[...] = v_ref[...] + sib_ref[...]
        for (px, py) in others:
            pltpu.make_async_remote_copy(src_ref=mine_ref, dst_ref=all_ref.at[jme], send_sem=i_send, recv_sem=i_recv,
                                         device_id=(px, py, c), device_id_type=MESH).start()
        three = all_ref.at[pl.ds(0, 3)]
        wait3 = pltpu.make_async_remote_copy(src_ref=three, dst_ref=three, send_sem=i_send, recv_sem=i_recv,
                                             device_id=(x, y, c), device_id_type=MESH)
        wait3.wait_recv()
        wait3.wait_send()
        all_ref[jme] = mine_ref[...]
        o_ref[...] = ((all_ref[0] + all_ref[1]) + all_ref[2]) + all_ref[3]

    vm = pl.BlockSpec(memory_space=pltpu.VMEM)
    return pl.pallas_call(
        body, name="allreduce_small", in_specs=[vm], out_specs=vm,
        out_shape=jax.ShapeDtypeStruct(v.shape, F32),
        scratch_shapes=[pltpu.VMEM((R, 128), F32), pltpu.VMEM((R, 128), F32), pltpu.VMEM((N_CHIPS, R, 128), F32),
                        pltpu.SemaphoreType.DMA, pltpu.SemaphoreType.DMA, pltpu.SemaphoreType.DMA, pltpu.SemaphoreType.DMA],
        compiler_params=pltpu.CompilerParams(vmem_limit_bytes=VMEM_LIMIT_BYTES),
    )(v)


def _row_tile(rows, width, mult=16, cap=3 << 20):
    best = rows
    for t in range(mult, rows + 1, mult):
        if rows % t == 0 and t * width * 4 <= cap:
            best = t
    return best


def add_sibling(name, mine, theirs, place):
    _, kh, ns = theirs.shape
    tr = _row_tile(kh, ns)
    nblk = kh // tr

    def body(place_ref, a_ref, b_ref, o_ref):
        o_ref[...] = (a_ref[...].astype(F32) + b_ref[...].astype(F32)).astype(BF16)

    return pl.pallas_call(
        body, name=name,
        grid_spec=pltpu.PrefetchScalarGridSpec(
            num_scalar_prefetch=1, grid=(N_CHIPS, nblk),
            in_specs=[pl.BlockSpec((None, tr, ns), lambda j, i, s: (j, s[1] * nblk + i, 0)),
                      pl.BlockSpec((None, tr, ns), lambda j, i, s: (j, i, 0))],
            out_specs=pl.BlockSpec((None, tr, ns), lambda j, i, s: (j, i, 0))),
        out_shape=jax.ShapeDtypeStruct(theirs.shape, BF16),
        compiler_params=_params(("parallel", "parallel")),
    )(place, mine, theirs)


def add_chips(name, q, p, place):
    _, kh, ns = q.shape
    tr = _row_tile(kh, ns)
    nblk = kh // tr

    def body(place_ref, *refs):
        q_refs, own_ref, o_ref = refs[:N_CHIPS], refs[N_CHIPS], refs[-1]
        jme = place_ref[0]
        tot = None
        for j in range(N_CHIPS):
            v = jnp.where(jme == j, own_ref[...], q_refs[j][...]).astype(F32)
            tot = v if tot is None else tot + v
        o_ref[...] = tot

    def q_ix(j):
        return lambda i, s: (jnp.where(s[0] == j, (j + 1) % N_CHIPS, j), i, 0)

    in_specs = [pl.BlockSpec((None, tr, ns), q_ix(j)) for j in range(N_CHIPS)]
    in_specs.append(pl.BlockSpec((None, tr, ns), lambda i, s: (s[0], i, 0)))
    return pl.pallas_call(
        body, name=name,
        grid_spec=pltpu.PrefetchScalarGridSpec(
            num_scalar_prefetch=1, grid=(nblk,), in_specs=in_specs,
            out_specs=pl.BlockSpec((tr, ns), lambda i, s: (s[1] * nblk + i, 0))),
        out_shape=jax.ShapeDtypeStruct((2 * kh, ns), F32),
        compiler_params=_params(("parallel",)),
    )(place, q, q, q, q, p)


ADAM_LR, ADAM_B1, ADAM_B2, ADAM_EPS, ADAM_WD, ADAM_STEP = 0.001, 0.9, 0.999, 1e-08, 0.01, 10


def adamw(name, w, g, m, v, layer, prev=None, after=None):
    _, k, ns = w.shape
    nsp = g.shape[1]
    tr = _row_tile(k, nsp, mult=8, cap=3 << 20)

    def body(w_ref, g_ref, m_ref, v_ref, *rest):
        go_ref, d_ref, mo_ref, vo_ref = rest[-4:]
        gv = g_ref[:, :ns] if nsp != ns else g_ref[...]
        mn = ADAM_B1 * m_ref[...] + (1.0 - ADAM_B1) * gv
        vn = ADAM_B2 * v_ref[...] + (1.0 - ADAM_B2) * (gv * gv)
        m_hat = mn / (1.0 - ADAM_B1 ** ADAM_STEP)
        v_hat = vn / (1.0 - ADAM_B2 ** ADAM_STEP)
        go_ref[...] = gv
        d_ref[...] = -ADAM_LR * (m_hat / (jnp.sqrt(v_hat) + ADAM_EPS) + ADAM_WD * w_ref[...])
        mo_ref[...] = mn
        vo_ref[...] = vn

    blk = pl.BlockSpec((None, tr, ns), lambda i: (layer, i, 0))
    gblk = pl.BlockSpec((tr, nsp), lambda i: (i, 0))
    args, in_specs, aliases = [w, g, m, v], [blk, gblk, blk, blk], {}
    if prev is not None:
        args += list(prev)
        in_specs += [pl.BlockSpec(memory_space=pl.ANY)] * 4
        aliases = {4 + i: i for i in range(4)}
    if after is not None:
        args.append(after)
        in_specs.append(pl.BlockSpec(memory_space=pl.ANY))
    return pl.pallas_call(
        body, name=name, grid=(k // tr,), in_specs=in_specs, out_specs=[blk] * 4,
        out_shape=[jax.ShapeDtypeStruct(w.shape, F32)] * 4, input_output_aliases=aliases,
        compiler_params=_params(("parallel",)),
    )(*args)


WEIGHTS = ("ffn_a_norm", "ffn_a_w1", "ffn_a_w3", "ffn_a_w2", "mix_norm", "w_in", "q_a_norm", "w_uq", "kv_a_norm", "w_ukv",
           "q_norm", "k_norm", "gm_v_norm", "gm_ws", "gm_bs", "attn_out_norm", "gm_out_norm", "w_out", "ffn_b_norm",
           "ffn_b_w1", "ffn_b_w3", "ffn_b_w2", "ple_gate_norm", "w_ple_gate", "w_ple", "ple_norm")
_FF = FF_PAD - FF_SHARD
BIG = {"ffn_a_w1": _FF, "ffn_a_w3": _FF, "ffn_a_w2": _FF, "ffn_b_w1": _FF, "ffn_b_w3": _FF, "ffn_b_w2": _FF,
       "w_in": IN_SHARD_PAD - IN_SHARD, "w_uq": 0, "w_ukv": 0, "w_ple": 0, "w_out": 0, "w_ple_gate": 0}
TRANSPOSED = ("ffn_a_w1", "ffn_a_w3", "ffn_b_w1", "ffn_b_w3", "w_in")
SMALL = tuple(n for n in WEIGHTS if n not in BIG)
PACK = 1024


def _pack_small(d):
    parts = []
    for n in SMALL:
        flat = d[n].reshape(-1)
        parts.append(jnp.pad(flat, (0, (-flat.shape[0]) % PACK)))
    return jnp.concatenate(parts).reshape(-1, 128)


def _unpack_small(buf, like):
    flat = buf.reshape(-1)
    out, pos = {}, 0
    for n in SMALL:
        size = math.prod(like[n].shape)
        out[n] = flat[pos:pos + size].reshape(like[n].shape)
        pos += size + (-size) % PACK
    return out


def kernel(*args):
    names = (("x", "p", "positions") + WEIGHTS + ("loss_target",) + tuple("m_" + n for n in WEIGHTS)
             + tuple("v_" + n for n in WEIGHTS))
    a = dict(zip(names, args, strict=True))
    x, p, positions, target = a["x"][0], a["p"][:, 0], a["positions"][0], a["loss_target"][0]
    for n in TRANSPOSED:
        for pre in ("", "m_", "v_"):
            a[pre + n] = jnp.swapaxes(a[pre + n], 1, 2)

    place = jnp.stack([2 * lax.axis_index("x") + lax.axis_index("y"), lax.axis_index("c")]).astype(jnp.int32)
    small = {n: a[n] for n in SMALL}
    tabs = rope_tables(positions)
    order = {"l0a": ("ffn_a_w1", "ffn_a_w3"), "l0b": ("ffn_a_w2",), "l0c": ("w_in", "w_uq", "w_ukv", "w_out"),
             "l0d": ("ffn_b_w1", "ffn_b_w3", "ffn_b_w2", "w_ple_gate", "w_ple")}
    prep = lambda n, l, after: prep_shard(f"prep_{n}_{l}", a[n], l, BIG[n], place, after)
    flights, token = {}, None
    for tag, names in order.items():
        flights[tag] = gather_start(f"gather_{tag}_start", [prep(n, 0, token) for n in names], None)
        token = flights[tag][3]
    slots1 = []
    for n in BIG:
        slots1.append(prep(n, 1, slots1[-1] if slots1 else token))

    def arrive(tag, after):
        send, recv, flying, _ = flights[tag]
        arrived = gather_wait(f"gather_{tag}_wait", send, recv, flying, after)
        send, recv, flying, token = forward_start(f"forward_{tag}_start", arrived)
        return dict(zip(order[tag], gather_wait(f"forward_{tag}_wait", send, recv, flying, token)))

    G0 = arrive("l0a", slots1[-1])

    def before_down(s):
        G0.update(arrive("l0b", s))
        return G0["ffn_a_w2"]

    def before_mixer(h1, Gl, small_):
        G0.update(arrive("l0c", h1))
        flights["l1"] = gather_start("gather_l1_start", slots1, G0["w_uq"])
        return G0, {**small_, "mix_norm": small_["mix_norm"] + flights["l1"][3][0, 0]}

    def before_ffn_b(h2, Gl):
        G0.update(arrive("l0d", h2))
        return G0

    def before_ple(h3):
        send, recv, flying, _ = flights["l1"]
        flights["f1"] = forward_start("forward_l1_start", gather_wait("gather_l1_wait", send, recv, flying, h3))
        return flights["f1"][3]

    h, saved0 = layer_fwd(0, x, p[0], G0, small, tabs,
                          {"down_a": before_down, "mixer": before_mixer, "ffn_b": before_ffn_b, "ple": before_ple})
    G1 = dict(zip(BIG, gather_wait("forward_l1_wait", *flights["f1"][:3], h)))
    h, saved1 = layer_fwd(1, h, p[1], G1, small, tabs)
    dh, loss = loss_grad(h, target)
    loss = lax.psum(loss, ("x", "y", "c"))

    groups = {"l1": tuple(BIG),
              "l0a": ("w_ple_gate", "w_ple", "ffn_b_w1", "ffn_b_w3", "ffn_b_w2"),
              "l0b": ("w_in", "w_uq", "w_ukv", "w_out"),
              "l0c": ("ffn_a_w1", "ffn_a_w3", "ffn_a_w2")}
    crossing, started = [], {}

    def begin(tag, gl, after):
        ex = exchange_start(f"exchange_{tag}_start", [gl[n] for n in groups[tag]], after)
        crossing.append((tag, ex))
        return ex[4]

    def advance(after):
        tag, (send, recv, mine, lands, _) = crossing.pop()
        mine, theirs = exchange_wait(f"exchange_{tag}_wait", send, recv, mine, lands, after)
        parts = [add_sibling(f"add_sibling_{n}_{tag}", d, r, place) for n, d, r in zip(groups[tag], mine, theirs)]
        started[tag] = scatter_start(f"scatter_{tag}_start", parts)
        return started[tag][4]

    def finish(tag, after):
        send, recv, parts, lands, _ = started[tag]
        parts, slabs = scatter_wait(f"scatter_{tag}_wait", send, recv, parts, lands, after)
        halves = [add_chips(f"add_chips_{n}_{tag}", q, pt, place) for n, q, pt in zip(groups[tag], slabs, parts)]
        return dict(zip(groups[tag], share_halves(f"share_{tag}", halves)))

    def update(names, full, layer, prev, after):
        outs = {}
        for n in names:
            outs[n] = adamw(f"adamw_{n}_{layer}", a[n], full[n], a["m_" + n], a["v_" + n], layer, prev and prev[n], after)
            after = outs[n][1]
        return outs, after

    grads = [None, None]
    dh, grads[1] = layer_bwd(1, dh, p[1], G1, small, tabs, saved1)
    token = begin("l1", grads[1], None)
    w0 = {**saved0[0], "ple_gate_norm": saved0[0]["ple_gate_norm"] + token[0, 0]}
    hooks = {"ffn_b": lambda gl, dh_: advance(dh_),
             "mixer": lambda gl, dh_: begin("l0a", gl, None),
             "ffn_a": lambda gl, dh_: begin("l0b", gl, advance(dh_)),
             "ffn_a_dw": lambda gl, dh_: advance(dh_)}
    gx, grads[0] = layer_bwd(0, dh, p[0], G0, small, tabs, (w0,) + saved0[1:], hooks)
    token = begin("l0c", grads[0], None)
    full1 = finish("l1", token)
    last = tuple(BIG)[-1]
    outs1, behind = update(BIG, full1, 1, None, advance(full1[last]))
    full0 = finish("l0a", behind)
    full0.update(finish("l0b", full0[groups["l0a"][-1]]))
    early = groups["l0a"] + groups["l0b"]
    outs0, behind = update(early, full0, 0, outs1, full0[groups["l0b"][-1]])
    outs0.update(update(groups["l0c"], finish("l0c", behind), 0, outs1, None)[0])

    out_g, out_d, out_m, out_v = {}, {}, {}, {}
    for n in BIG:
        outs = [jnp.swapaxes(o, 1, 2) for o in outs0[n]] if n in TRANSPOSED else outs0[n]
        out_g[n], out_d[n], out_m[n], out_v[n] = outs

    gs = allreduce_small(_pack_small({n: jnp.stack([grads[0][n], grads[1][n]]) for n in SMALL}))
    rows = gs.shape[0] // 2
    packed = [_pack_small(d).reshape(2, rows, 128) for d in
              (small, {n: a["m_" + n] for n in SMALL}, {n: a["v_" + n] for n in SMALL})]
    gs = gs.reshape(2, rows, 128)
    sm = adamw("adamw_small_0", packed[0], gs[0], packed[1], packed[2], 0)
    sm = adamw("adamw_small_1", packed[0], gs[1], packed[1], packed[2], 1, sm)
    for dst, buf in zip((out_g, out_d, out_m, out_v), sm):
        dst.update(_unpack_small(buf, small))

    return (loss, gx[None], *[out_g[n] for n in WEIGHTS], *[out_d[n] for n in WEIGHTS],
            *[out_m[n] for n in WEIGHTS], *[out_v[n] for n in WEIGHTS])
```

```python
import math

import jax
import jax.numpy as jnp
from jax import lax
from jax.experimental import pallas as pl
from jax.experimental.pallas import tpu as pltpu

F32 = jnp.float32
BF16 = jnp.bfloat16

D_MODEL = 2048
D_FF = 5504
N_CHIPS = 4
FF_SHARD = D_FF // N_CHIPS
FF_PAD = 1408
FF_P = N_CHIPS * FF_PAD
HEADS = 8
QK_NOPE = 128
QK_ROPE = 64
QK_DIM = 192
HEAD_PAD = 256
V_DIM = 128
Q_RANK = 512
KV_RANK = 256
ATTN_W = 1024
GM_W = 1024
GROUPS = 8
CHUNK = 128
PLE_DIM = 256
IN_P = 3072
IN_SHARD = 720
IN_SHARD_PAD = 736
EPS = 1e-6
ROPE_BASE = 10000.0
ATTN_SCALE = QK_DIM ** -0.5
VMEM_LIMIT_BYTES = 56 * 1024 * 1024


def _params(sem):
    return pltpu.CompilerParams(dimension_semantics=sem, vmem_limit_bytes=VMEM_LIMIT_BYTES)


def _bf(x):
    return x if x.dtype == BF16 else x.astype(BF16)


def _sigmoid(x):
    return 1.0 / (1.0 + jnp.exp(-x))


_GELU_C = math.sqrt(2.0 / math.pi)


def _gelu(x):
    t = jnp.tanh(_GELU_C * (x + 0.044715 * x * x * x))
    return 0.5 * x * (1.0 + t)


def _gelu_grad(x):
    t = jnp.tanh(_GELU_C * (x + 0.044715 * x * x * x))
    return 0.5 * (1.0 + t) + 0.5 * x * (1.0 - t * t) * _GELU_C * (1.0 + 3 * 0.044715 * x * x)


def op_a(a, tm, tk):
    return (a, (tm, tk), lambda i, j, k: (i, k), 1)


def op_at(a, tm, tk):
    return (a, (tk, tm), lambda i, j, k: (k, i), 0)


def op_b(b, tk, tn):
    return (b, (tk, tn), lambda i, j, k: (k, j), 0)


def op_bt(b, tk, tn):
    return (b, (tn, tk), lambda i, j, k: (j, k), 1)


def op_b_cols(g, pre, tk, tn):
    nb = g.shape[-1] // tn
    none = (None,) * (1 + len(pre))
    return (g, none + (tk, tn), lambda i, j, k: (j // nb,) + tuple(pre) + (k, j % nb), 0)


def op_b_rows(g, pre, tk, tn, koff=0):
    nb = g.shape[-2] // tk
    none = (None,) * (1 + len(pre))
    return (g, none + (tk, tn), lambda i, j, k: ((k + koff) // nb,) + tuple(pre) + ((k + koff) % nb, j), 0)


def op_b_rows_t(g, pre, tk, tn):
    nb = g.shape[-2] // tn
    none = (None,) * (1 + len(pre))
    return (g, none + (tn, tk), lambda i, j, k: (j // nb,) + tuple(pre) + (j % nb, k), 1)


def tile_mn(x, tm, tn):
    return (x, (tm, tn), lambda i, j: (i, j))


def out_mn(M, N, tm, tn, dtype):
    return (jax.ShapeDtypeStruct((M, N), dtype), (tm, tn), lambda i, j: (i, j))


def out_cols(M, ns, tm, tn, dtype):
    nb = ns // tn
    return (jax.ShapeDtypeStruct((N_CHIPS, M, ns), dtype), (None, tm, tn), lambda i, j: (j // nb, i, j % nb))


def matmul(name, grid_mnk, a_ops, b_ops, terms, n_acc, extras, outs, epilogue, acc_tile, n_outer=False, after=None):
    gm, gn, gk = grid_mnk
    na, nb, nx, no = len(a_ops), len(b_ops), len(extras), len(outs)
    nd = 0 if after is None else 1

    def body(*refs):
        a_refs, b_refs = refs[:na], refs[na:na + nb]
        x_refs = refs[na + nb:na + nb + nx]
        o_refs = refs[na + nb + nx + nd:na + nb + nx + nd + no]
        acc_refs = refs[na + nb + nx + nd + no:]
        k = pl.program_id(2)

        @pl.when(k == 0)
        def _():
            for acc in acc_refs:
                acc[...] = jnp.zeros_like(acc)

        for ai, bi, ci in terms:
            dims = (((a_ops[ai][3],), (b_ops[bi][3],)), ((), ()))
            acc_refs[ci][...] += lax.dot_general(_bf(a_refs[ai][...]), _bf(b_refs[bi][...]), dims,
                                                 preferred_element_type=F32)

        @pl.when(k == gk - 1)
        def _():
            res = epilogue([acc[...] for acc in acc_refs], [x[...] for x in x_refs])
            for o, v in zip(o_refs, res):
                o[...] = v.astype(o.dtype)

    if n_outer:
        grid = (gn, gm, gk)

        def ix3(f):
            return lambda j, i, k: f(i, j, k)

        def ix2(f):
            return lambda j, i, k: f(i, j)
    else:
        grid = (gm, gn, gk)

        def ix3(f):
            return lambda i, j, k: f(i, j, k)

        def ix2(f):
            return lambda i, j, k: f(i, j)

    in_specs = [pl.BlockSpec(blk, ix3(f)) for (_, blk, f, _) in list(a_ops) + list(b_ops)]
    in_specs += [pl.BlockSpec(blk, ix2(f)) for (_, blk, f) in extras]
    in_specs += [pl.BlockSpec(memory_space=pl.ANY)] * nd
    out_specs = [pl.BlockSpec(blk, ix2(f)) for (_, blk, f) in outs]
    return pl.pallas_call(
        body,
        name=name,
        grid=grid,
        in_specs=in_specs,
        out_specs=out_specs,
        out_shape=[s for (s, _, _) in outs],
        scratch_shapes=[pltpu.VMEM(acc_tile, F32) for _ in range(n_acc)],
        compiler_params=_params(("parallel", "parallel", "arbitrary")),
    )(*[o[0] for o in a_ops], *[o[0] for o in b_ops], *[x[0] for x in extras], *([after] * nd))


def _acc0(accs, xs):
    return (accs[0],)


def mm_simple(name, a, b_op_fn, M, N, K, tm, tn, tk, out_dtype=F32, a_t=False, extras=(), epilogue=_acc0, outs=None,
              after=None):
    a_op = op_at(a, tm, tk) if a_t else op_a(a, tm, tk)
    outs = outs or [out_mn(M, N, tm, tn, out_dtype)]
    return matmul(name, (M // tm, N // tn, K // tk), [a_op], [b_op_fn(tk, tn)], [(0, 0, 0)], 1,
                  list(extras), outs, epilogue, (tm, tn), after=after)


def rms_fwd(name, x, g, width, col_blk=0, tm=256, out_dtype=BF16):
    T = x.shape[0]

    def body(x_ref, g_ref, o_ref):
        xv = x_ref[...].astype(F32)
        r = lax.rsqrt(jnp.mean(xv * xv, axis=-1, keepdims=True) + EPS)
        o_ref[...] = (xv * r * g_ref[...]).astype(o_ref.dtype)

    return pl.pallas_call(
        body, name=name, grid=(T // tm,),
        in_specs=[pl.BlockSpec((tm, width), lambda i: (i, col_blk)), pl.BlockSpec((1, width), lambda i: (0, 0))],
        out_specs=pl.BlockSpec((tm, width), lambda i: (i, 0)),
        out_shape=jax.ShapeDtypeStruct((T, width), out_dtype),
        compiler_params=_params(("parallel",)),
    )(x, g.reshape(1, width))


def rms_bwd(name, x, g, dn, width, col_blk=0, dres=None, tm=256, with_delta=False, bf16_copy=False):
    T = x.shape[0]
    has_res = dres is not None

    def body(*refs):
        x_ref, g_ref, dn_ref = refs[:3]
        pos = 3
        res_ref = None
        if has_res:
            res_ref = refs[pos]
            pos += 1
        dx_ref, dg_ref = refs[pos], refs[pos + 1]
        delta_ref = refs[pos + 2] if with_delta else None
        lo_ref = refs[-1] if bf16_copy else None
        i = pl.program_id(0)
        xv = x_ref[...].astype(F32)
        r = lax.rsqrt(jnp.mean(xv * xv, axis=-1, keepdims=True) + EPS)
        xh = xv * r
        d = dn_ref[...].astype(F32)
        gd = d * g_ref[...]
        dx = r * (gd - xh * jnp.mean(gd * xh, axis=-1, keepdims=True))
        if has_res:
            dx = dx + res_ref[...]
        dx_ref[...] = dx.astype(dx_ref.dtype)
        if bf16_copy:
            lo_ref[...] = dx.astype(BF16)
        part = jnp.sum(d * xh, axis=0, keepdims=True)

        @pl.when(i == 0)
        def _():
            dg_ref[...] = part

        @pl.when(i > 0)
        def _():
            dg_ref[...] += part

        if with_delta:
            for h in range(width // 128):
                sl = slice(h * 128, (h + 1) * 128)
                s = jnp.sum(dx[:, sl] * xv[:, sl], axis=-1, keepdims=True)
                delta_ref[:, sl] = jnp.broadcast_to(s, (tm, 128))

    in_specs = [pl.BlockSpec((tm, width), lambda i: (i, col_blk)), pl.BlockSpec((1, width), lambda i: (0, 0)),
                pl.BlockSpec((tm, width), lambda i: (i, 0))]
    args = [x, g.reshape(1, width), dn]
    if has_res:
        in_specs.append(pl.BlockSpec((tm, width), lambda i: (i, 0)))
        args.append(dres)
    out_specs = [pl.BlockSpec((tm, width), lambda i: (i, 0)), pl.BlockSpec((1, width), lambda i: (0, 0))]
    out_shape = [jax.ShapeDtypeStruct((T, width), F32), jax.ShapeDtypeStruct((1, width), F32)]
    if with_delta:
        out_specs.append(pl.BlockSpec((tm, width), lambda i: (i, 0)))
        out_shape.append(jax.ShapeDtypeStruct((T, width), F32))
    if bf16_copy:
        out_specs.append(pl.BlockSpec((tm, width), lambda i: (i, 0)))
        out_shape.append(jax.ShapeDtypeStruct((T, width), BF16))
    return pl.pallas_call(
        body, name=name, grid=(T // tm,), in_specs=in_specs, out_specs=out_specs, out_shape=out_shape,
        compiler_params=_params(("arbitrary",)),
    )(*args)


def ffn_fwd(tag, h, g, w1g, w3g, w2g, pre, w2_late=None):
    T = h.shape[0]
    n = rms_fwd(f"{tag}_rms", h, g, D_MODEL)
    tm, tn = 512, FF_PAD

    def up_epi(accs, xs):
        a1, a3 = accs
        return a1, a3, a1 * _sigmoid(a1) * a3

    a1, a3, s = matmul(
        f"{tag}_up", (T // tm, FF_P // tn, 1),
        [op_a(n, tm, D_MODEL)], [op_b_rows_t(w1g, pre, D_MODEL, tn), op_b_rows_t(w3g, pre, D_MODEL, tn)],
        [(0, 0, 0), (0, 1, 1)], 2, [],
        [out_mn(T, FF_P, tm, tn, BF16)] * 3, up_epi, (tm, tn), n_outer=True)

    if w2_late is not None:
        w2g = w2_late(s)
    tm2, tn2 = 1024, 1024
    (h_out,) = matmul(
        f"{tag}_down", (T // tm2, D_MODEL // tn2, N_CHIPS),
        [op_a(s, tm2, FF_PAD)], [op_b_rows(w2g, pre, FF_PAD, tn2)],
        [(0, 0, 0)], 1, [tile_mn(h, tm2, tn2)],
        [out_mn(T, D_MODEL, tm2, tn2, F32)], lambda accs, xs: (xs[0] + 0.5 * accs[0],), (tm2, tn2))
    return h_out, (n, a1, a3, s)


def ffn_bwd(tag, dh_out, dh_bf, h, g, res, w1g, w3g, w2g, pre, after=None, before_dw=None):
    n, a1, a3, s = res
    T = h.shape[0]
    tm, tn = 512, FF_PAD

    def act_epi(accs, xs):
        ds = 0.5 * accs[0]
        x1, x3 = xs[0].astype(F32), xs[1].astype(F32)
        sg = _sigmoid(x1)
        silu = x1 * sg
        return ds * x3 * (sg + silu * (1.0 - sg)), ds * silu

    da1, da3 = matmul(
        f"{tag}_dact", (T // tm, FF_P // tn, 1),
        [op_a(dh_bf, tm, D_MODEL)], [op_b_rows_t(w2g, pre, D_MODEL, tn)],
        [(0, 0, 0)], 1, [tile_mn(a1, tm, tn), tile_mn(a3, tm, tn)],
        [out_mn(T, FF_P, tm, tn, BF16)] * 2, act_epi, (tm, tn), n_outer=True, after=after)

    tm2, tn2 = 1024, 1024
    (dn,) = matmul(
        f"{tag}_dn", (T // tm2, D_MODEL // tn2, N_CHIPS),
        [op_a(da1, tm2, FF_PAD), op_a(da3, tm2, FF_PAD)],
        [op_b_rows(w1g, pre, FF_PAD, tn2), op_b_rows(w3g, pre, FF_PAD, tn2)],
        [(0, 0, 0), (1, 1, 0)], 1, [], [out_mn(T, D_MODEL, tm2, tn2, F32)], _acc0, (tm2, tn2))
    dh, dg, dh_lo = rms_bwd(f"{tag}_rms_bwd", h, g, dn, D_MODEL, dres=dh_out, bf16_copy=True)
    if before_dw is not None:
        after = before_dw(dh)

    tk = 1024

    def dw_t(nm, left, right, scale):
        (dw,) = matmul(
            f"{tag}_{nm}", (FF_P // FF_PAD, D_MODEL // 1024, T // tk),
            [op_at(left, FF_PAD, tk)], [op_b(right, tk, 1024)],
            [(0, 0, 0)], 1, [], [out_mn(FF_P, D_MODEL, FF_PAD, 1024, BF16)],
            lambda accs, xs: (scale * accs[0],), (FF_PAD, 1024), after=after)
        return dw

    dw2 = dw_t("dw2", s, dh_bf, 0.5)
    dw1 = dw_t("dw1", da1, n, 1.0)
    dw3 = dw_t("dw3", da3, n, 1.0)
    return dh, dh_lo, dg, dw1, dw3, dw2


def rope_tables(positions):
    inv_freq = ROPE_BASE ** (-jnp.arange(0, QK_ROPE, 2, dtype=F32) / QK_ROPE)
    ang = positions.astype(F32)[:, None] * inv_freq
    cos, sin = jnp.cos(ang), jnp.sin(ang)
    T = positions.shape[0]
    one, zero = jnp.ones((T, QK_NOPE), F32), jnp.zeros((T, 64), F32)
    z32, z128 = jnp.zeros((T, 32), F32), jnp.zeros((T, QK_NOPE), F32)
    c = jnp.concatenate([one, cos, cos, zero], axis=1)
    s1 = jnp.concatenate([z128, -sin, z32, zero], axis=1)
    s2 = jnp.concatenate([z128, z32, sin, zero], axis=1)
    return c, s1, s2


def _rope(y, c, s1, s2):
    return y * c + pltpu.roll(y, HEAD_PAD - 32, 1) * s1 + pltpu.roll(y, 32, 1) * s2


def _rope_t(d, c, s1, s2):
    return d * c + pltpu.roll(d * s1, 32, 1) + pltpu.roll(d * s2, HEAD_PAD - 32, 1)


def _head_norm(x):
    r = lax.rsqrt(jnp.sum(x * x, axis=-1, keepdims=True) * (1.0 / QK_DIM) + EPS)
    return x * r, r


def qk_prep_fwd(tag, q_raw, kk_raw, z_p, gq, gk, tabs, tm=256):
    T = q_raw.shape[0]
    c, s1, s2 = tabs

    def body(q_ref, k_ref, kr_ref, gq_ref, gk_ref, c_ref, s1_ref, s2_ref, qo_ref, ko_ref):
        cv, s1v, s2v = c_ref[...], s1_ref[...], s2_ref[...]
        kr = kr_ref[...]
        for h in range(HEADS):
            sl = slice(h * HEAD_PAD, (h + 1) * HEAD_PAD)
            xh, _ = _head_norm(q_ref[:, sl])
            qo_ref[:, sl] = (_rope(xh * gq_ref[...], cv, s1v, s2v) * ATTN_SCALE).astype(BF16)
            xh, _ = _head_norm(k_ref[:, sl] + kr)
            ko_ref[:, sl] = _rope(xh * gk_ref[...], cv, s1v, s2v).astype(BF16)

    row = lambda i: (i, 0)
    full = pl.BlockSpec((tm, HEADS * HEAD_PAD), row)
    tab = pl.BlockSpec((tm, HEAD_PAD), row)
    vec = pl.BlockSpec((1, HEAD_PAD), lambda i: (0, 0))
    return pl.pallas_call(
        body, name=f"{tag}_qk_prep", grid=(T // tm,),
        in_specs=[full, full, pl.BlockSpec((tm, HEAD_PAD), lambda i: (i, 3)), vec, vec, tab, tab, tab],
        out_specs=[full, full],
        out_shape=[jax.ShapeDtypeStruct((T, HEADS * HEAD_PAD), BF16)] * 2,
        compiler_params=_params(("parallel",)),
    )(q_raw, kk_raw, z_p, gq, gk, c, s1, s2)


def qk_prep_bwd(tag, dq_full, dk_full, q_raw, kk_raw, z_p, gq, gk, tabs, tm=256):
    T = q_raw.shape[0]
    c, s1, s2 = tabs

    def body(dq_ref, dk_ref, q_ref, k_ref, kr_ref, gq_ref, gk_ref, c_ref, s1_ref, s2_ref,
             dqr_ref, dkr_ref, dz_ref, dgq_ref, dgk_ref):
        i = pl.program_id(0)
        cv, s1v, s2v = c_ref[...], s1_ref[...], s2_ref[...]
        kr = kr_ref[...]
        lane = lax.broadcasted_iota(jnp.int32, (tm, HEAD_PAD), 1)
        slot = ((lane >= QK_NOPE) & (lane < QK_DIM)).astype(F32)

        def one(x, g, d):
            xh, r = _head_norm(x)
            dy = _rope_t(d, cv, s1v, s2v)
            gd = dy * g
            dx = r * (gd - xh * (jnp.sum(gd * xh, axis=-1, keepdims=True) * (1.0 / QK_DIM)))
            return dx, jnp.sum(dy * xh, axis=0, keepdims=True)

        dgq = jnp.zeros((1, HEAD_PAD), F32)
        dgk = jnp.zeros((1, HEAD_PAD), F32)
        dz = jnp.zeros((tm, HEAD_PAD), F32)
        for h in range(HEADS):
            sl = slice(h * HEAD_PAD, (h + 1) * HEAD_PAD)
            dx, dg = one(q_ref[:, sl], gq_ref[...], dq_ref[:, sl].astype(F32) * ATTN_SCALE)
            dqr_ref[:, sl] = dx
            dgq = dgq + dg
            dx, dg = one(k_ref[:, sl] + kr, gk_ref[...], dk_ref[:, sl].astype(F32))
            dkr_ref[:, sl] = dx
            dgk = dgk + dg
            dz = dz + dx
        dz_ref[...] = dz * slot

        @pl.when(i == 0)
        def _():
            dgq_ref[...] = dgq
            dgk_ref[...] = dgk

        @pl.when(i > 0)
        def _():
            dgq_ref[...] += dgq
            dgk_ref[...] += dgk

    row = lambda i: (i, 0)
    full = pl.BlockSpec((tm, HEADS * HEAD_PAD), row)
    tab = pl.BlockSpec((tm, HEAD_PAD), row)
    vec = pl.BlockSpec((1, HEAD_PAD), lambda i: (0, 0))
    return pl.pallas_call(
        body, name=f"{tag}_qk_prep_bwd", grid=(T // tm,),
        in_specs=[full, full, full, full, pl.BlockSpec((tm, HEAD_PAD), lambda i: (i, 3)), vec, vec, tab, tab, tab],
        out_specs=[full, full, tab, vec, vec],
        out_shape=[jax.ShapeDtypeStruct((T, HEADS * HEAD_PAD), F32)] * 2
        + [jax.ShapeDtypeStruct((T, HEAD_PAD), F32)] + [jax.ShapeDtypeStruct((1, HEAD_PAD), F32)] * 2,
        compiler_params=_params(("arbitrary",)),
    )(dq_full, dk_full, q_raw, kk_raw, z_p, gq, gk, c, s1, s2)


def attn_fwd(tag, q_full, k_full, vv, blk=512):
    T = q_full.shape[0]
    nb = T // blk
    neg = float(jnp.finfo(jnp.float32).min)

    def body(q_ref, k_ref, v_ref, o_ref, lse_ref, m_ref, l_ref, acc_ref):
        i = pl.program_id(1)
        m_ref[...] = jnp.full_like(m_ref, neg)
        l_ref[...] = jnp.zeros_like(l_ref)
        acc_ref[...] = jnp.zeros_like(acc_ref)
        q = q_ref[...]

        def step(j, masked):
            rows = pl.ds(pl.multiple_of(j * blk, blk), blk)
            s = lax.dot_general(q, k_ref[rows, :], (((1,), (1,)), ((), ())), preferred_element_type=F32)
            if masked:
                row = lax.broadcasted_iota(jnp.int32, (blk, blk), 0)
                col = lax.broadcasted_iota(jnp.int32, (blk, blk), 1)
                s = jnp.where(col <= row, s, neg)
            m_prev = m_ref[...]
            m_new = jnp.maximum(m_prev, jnp.max(s, axis=-1, keepdims=True))
            alpha = jnp.exp(m_prev - m_new)
            p = jnp.exp(s - m_new[:, :1])
            l_ref[...] = alpha * l_ref[...] + jnp.sum(p, axis=-1, keepdims=True)
            acc_ref[...] = alpha * acc_ref[...] + jnp.dot(p.astype(BF16), v_ref[rows, :], preferred_element_type=F32)
            m_ref[...] = m_new

        def off_diagonal(j, carry):
            step(j, False)
            return carry

        lax.fori_loop(0, i, off_diagonal, 0)
        step(i, True)
        o_ref[...] = acc_ref[...] / l_ref[...]
        lse_ref[...] = m_ref[...] + jnp.log(l_ref[...])

    return pl.pallas_call(
        body, name=f"{tag}_attn_fwd", grid=(HEADS, nb),
        in_specs=[pl.BlockSpec((blk, HEAD_PAD), lambda h, i: (i, h)),
                  pl.BlockSpec((T, HEAD_PAD), lambda h, i: (0, h)), pl.BlockSpec((T, V_DIM), lambda h, i: (0, h))],
        out_specs=[pl.BlockSpec((blk, V_DIM), lambda h, i: (i, h))] * 2,
        out_shape=[jax.ShapeDtypeStruct((T, ATTN_W), F32)] * 2,
        scratch_shapes=[pltpu.VMEM((blk, V_DIM), F32)] * 3,
        compiler_params=_params(("parallel", "parallel")),
    )(q_full, k_full, vv)


def attn_bwd(tag, q_full, k_full, vv, do, lse, delta, blk=512):
    T = q_full.shape[0]
    nb = T // blk
    neg = float(jnp.finfo(jnp.float32).min)

    def body(q_ref, k_ref, v_ref, do_ref, lse_ref, dl_ref, dq_ref, dk_ref, dv_ref, dk_acc, dv_acc):
        j = pl.program_id(1)

        @pl.when(j == 0)
        def _():
            dq_ref[...] = jnp.zeros_like(dq_ref)

        dk_acc[...] = jnp.zeros_like(dk_acc)
        dv_acc[...] = jnp.zeros_like(dv_acc)
        k, v = k_ref[...], v_ref[...]

        def step(i, masked):
            rows = pl.ds(pl.multiple_of(i * blk, blk), blk)
            q = q_ref[rows, :]
            s = lax.dot_general(q, k, (((1,), (1,)), ((), ())), preferred_element_type=F32)
            if masked:
                row = lax.broadcasted_iota(jnp.int32, (blk, blk), 0)
                col = lax.broadcasted_iota(jnp.int32, (blk, blk), 1)
                s = jnp.where(col <= row, s, neg)
            p = jnp.exp(s - lse_ref[rows, :1])
            dob = _bf(do_ref[rows, :])
            dv_acc[...] += lax.dot_general(p.astype(BF16), dob, (((0,), (0,)), ((), ())), preferred_element_type=F32)
            dp = lax.dot_general(dob, v, (((1,), (1,)), ((), ())), preferred_element_type=F32)
            ds = (p * (dp - dl_ref[rows, :1])).astype(BF16)
            dk_acc[...] += lax.dot_general(ds, q, (((0,), (0,)), ((), ())), preferred_element_type=F32)
            dq_ref[rows, :] += jnp.dot(ds, k, preferred_element_type=F32)

        def off_diagonal(i, carry):
            step(i, False)
            return carry

        step(j, True)
        lax.fori_loop(j + 1, nb, off_diagonal, 0)
        dk_ref[...] = dk_acc[...]
        dv_ref[...] = dv_acc[...]

    head = lambda h, j: (0, h)
    kv_ix = lambda h, j: (j, h)
    return pl.pallas_call(
        body, name=f"{tag}_attn_bwd", grid=(HEADS, nb),
        in_specs=[pl.BlockSpec((T, HEAD_PAD), head), pl.BlockSpec((blk, HEAD_PAD), kv_ix),
                  pl.BlockSpec((blk, V_DIM), kv_ix), pl.BlockSpec((T, V_DIM), head),
                  pl.BlockSpec((T, V_DIM), head), pl.BlockSpec((T, V_DIM), head)],
        out_specs=[pl.BlockSpec((T, HEAD_PAD), head),
                   pl.BlockSpec((blk, HEAD_PAD), kv_ix), pl.BlockSpec((blk, V_DIM), kv_ix)],
        out_shape=[jax.ShapeDtypeStruct((T, HEADS * HEAD_PAD), F32)] * 2 + [jax.ShapeDtypeStruct((T, ATTN_W), F32)],
        scratch_shapes=[pltpu.VMEM((blk, HEAD_PAD), F32), pltpu.VMEM((blk, V_DIM), F32)],
        compiler_params=_params(("parallel", "arbitrary")),
    )(q_full, k_full, vv, do, lse, delta)


def _gm_forward(u, v, gv, wc_ref, bb_ref, nchunk):
    ug = _gelu(u)
    vg = _gelu(v)
    rv = lax.rsqrt(jnp.mean(vg * vg, axis=-1, keepdims=True) + EPS)
    vhat = vg * rv
    vn = (vhat * gv).astype(BF16)
    gates = []
    for cidx in range(nchunk):
        rows = slice(cidx * CHUNK, (cidx + 1) * CHUNK)
        gates.append(jnp.concatenate(
            [jnp.dot(wc_ref[gidx], vn[rows, gidx * 128:(gidx + 1) * 128], preferred_element_type=F32) + bb_ref[gidx]
             for gidx in range(GROUPS)], axis=1))
    gate = jnp.concatenate(gates, axis=0)
    return ug, vhat, rv, vn, gate


def gmlp_fwd(tag, z_p, gv, gout, wc, bb, tm=256):
    T = z_p.shape[0]
    nchunk = tm // CHUNK

    def body(u_ref, v_ref, gv_ref, go_ref, wc_ref, bb_ref, o_ref):
        ug, _, _, _, gate = _gm_forward(u_ref[...], v_ref[...], gv_ref[...], wc_ref, bb_ref, nchunk)
        go = ug * gate
        ro = lax.rsqrt(jnp.mean(go * go, axis=-1, keepdims=True) + EPS)
        o_ref[...] = (go * ro * go_ref[...]).astype(BF16)

    vec = pl.BlockSpec((1, GM_W), lambda i: (0, 0))
    w3 = pl.BlockSpec((GROUPS, CHUNK, CHUNK), lambda i: (0, 0, 0))
    return pl.pallas_call(
        body, name=f"{tag}_gmlp_fwd", grid=(T // tm,),
        in_specs=[pl.BlockSpec((tm, GM_W), lambda i: (i, 1)), pl.BlockSpec((tm, GM_W), lambda i: (i, 2)), vec, vec, w3, w3],
        out_specs=pl.BlockSpec((tm, GM_W), lambda i: (i, 0)),
        out_shape=jax.ShapeDtypeStruct((T, GM_W), BF16),
        compiler_params=_params(("parallel",)),
    )(z_p, z_p, gv.reshape(1, GM_W), gout.reshape(1, GM_W), wc, bb)


def gmlp_bwd(tag, z_p, dmixed, gv, gout, wc, bb, tm=256):
    T = z_p.shape[0]
    nchunk = tm // CHUNK

    def body(u_ref, v_ref, dm_ref, gv_ref, go_ref, wc_ref, bb_ref, du_ref, dv_ref, dwc_ref, dbb_ref, dgv_ref, dgo_ref):
        i = pl.program_id(0)
        u, v = u_ref[...], v_ref[...]
        ug, vhat, rv, vn, gate = _gm_forward(u, v, gv_ref[...], wc_ref, bb_ref, nchunk)
        go = ug * gate
        ro = lax.rsqrt(jnp.mean(go * go, axis=-1, keepdims=True) + EPS)
        ohat = go * ro
        dm = dm_ref[...].astype(F32)
        dgo_part = jnp.sum(dm * ohat, axis=0, keepdims=True)
        doh = dm * go_ref[...]
        dgo = ro * (doh - ohat * jnp.mean(doh * ohat, axis=-1, keepdims=True))
        du_ref[...] = dgo * gate * _gelu_grad(u)
        dgate = dgo * ug
        dgb = dgate.astype(BF16)
        dvn_rows = []
        dwc_parts = []
        dbb_parts = []
        for gidx in range(GROUPS):
            cols = slice(gidx * 128, (gidx + 1) * 128)
            dw = jnp.zeros((CHUNK, CHUNK), F32)
            db = jnp.zeros((CHUNK, 128), F32)
            for cidx in range(nchunk):
                rows = slice(cidx * CHUNK, (cidx + 1) * CHUNK)
                dw = dw + lax.dot_general(dgb[rows, cols], vn[rows, cols], (((1,), (1,)), ((), ())),
                                          preferred_element_type=F32)
                db = db + dgate[rows, cols]
            dwc_parts.append(dw)
            dbb_parts.append(db)
        for cidx in range(nchunk):
            rows = slice(cidx * CHUNK, (cidx + 1) * CHUNK)
            dvn_rows.append(jnp.concatenate(
                [lax.dot_general(wc_ref[gidx], dgb[rows, gidx * 128:(gidx + 1) * 128], (((0,), (0,)), ((), ())),
                                 preferred_element_type=F32) for gidx in range(GROUPS)], axis=1))
        dvn = jnp.concatenate(dvn_rows, axis=0)
        dgv_part = jnp.sum(dvn * vhat, axis=0, keepdims=True)
        dvh = dvn * gv_ref[...]
        dvg = rv * (dvh - vhat * jnp.mean(dvh * vhat, axis=-1, keepdims=True))
        dv_ref[...] = dvg * _gelu_grad(v)

        @pl.when(i == 0)
        def _():
            for gidx in range(GROUPS):
                dwc_ref[gidx] = dwc_parts[gidx]
                dbb_ref[gidx] = dbb_parts[gidx]
            dgv_ref[...] = dgv_part
            dgo_ref[...] = dgo_part

        @pl.when(i > 0)
        def _():
            for gidx in range(GROUPS):
                dwc_ref[gidx] += dwc_parts[gidx]
                dbb_ref[gidx] += dbb_parts[gidx]
            dgv_ref[...] += dgv_part
            dgo_ref[...] += dgo_part

    vec = pl.BlockSpec((1, GM_W), lambda i: (0, 0))
    w3 = pl.BlockSpec((GROUPS, CHUNK, CHUNK), lambda i: (0, 0, 0))
    blk = pl.BlockSpec((tm, GM_W), lambda i: (i, 0))
    return pl.pallas_call(
        body, name=f"{tag}_gmlp_bwd", grid=(T // tm,),
        in_specs=[pl.BlockSpec((tm, GM_W), lambda i: (i, 1)), pl.BlockSpec((tm, GM_W), lambda i: (i, 2)),
                  pl.BlockSpec((tm, GM_W), lambda i: (i, 1)), vec, vec, w3, w3],
        out_specs=[blk, blk, w3, w3, vec, vec],
        out_shape=[jax.ShapeDtypeStruct((T, GM_W), F32)] * 2 + [jax.ShapeDtypeStruct((GROUPS, CHUNK, CHUNK), F32)] * 2
        + [jax.ShapeDtypeStruct((1, GM_W), F32)] * 2,
        compiler_params=_params(("arbitrary",)),
    )(z_p, z_p, dmixed, gv.reshape(1, GM_W), gout.reshape(1, GM_W), wc, bb)


def mixer_fwd(tag, h, w, tabs, wout_g, pre):
    T = h.shape[0]
    n2 = rms_fwd(f"{tag}_mix_rms", h, w["mix_norm"], D_MODEL)
    (z_p,) = mm_simple(f"{tag}_win", n2, lambda tk, tn: op_bt(w["w_in_pt"], tk, tn), T, IN_P, D_MODEL, 512, 1024, D_MODEL)
    cqn = rms_fwd(f"{tag}_cq_rms", z_p, w["q_a_norm"], Q_RANK, col_blk=0)
    ckvn = rms_fwd(f"{tag}_ckv_rms", z_p, w["kv_a_norm"], KV_RANK, col_blk=2)
    (q_raw,) = mm_simple(f"{tag}_wq", cqn, lambda tk, tn: op_b(w["wq_p"], tk, tn), T, 2048, Q_RANK, 512, 1024, Q_RANK)
    (kk_raw,) = mm_simple(f"{tag}_wk", ckvn, lambda tk, tn: op_b(w["wk_p"], tk, tn), T, 2048, KV_RANK, 512, 1024, KV_RANK)
    (vv,) = mm_simple(f"{tag}_wv", ckvn, lambda tk, tn: op_b(w["wv"], tk, tn), T, ATTN_W, KV_RANK, 512, 1024, KV_RANK,
                      out_dtype=BF16)
    q_full, k_full = qk_prep_fwd(tag, q_raw, kk_raw, z_p, w["gq_p"], w["gk_p"], tabs)
    a_out, lse = attn_fwd(tag, q_full, k_full, vv)
    mixed_a = rms_fwd(f"{tag}_ao_rms", a_out, w["attn_out_norm"], ATTN_W)
    mixed_g = gmlp_fwd(tag, z_p, w["gm_v_norm"], w["gm_out_norm"], w["wc"], w["bb"])
    tm, tn, tk = 512, 1024, 512
    (h2,) = matmul(
        f"{tag}_wout", (T // tm, D_MODEL // tn, ATTN_W // tk),
        [op_a(mixed_a, tm, tk), op_a(mixed_g, tm, tk)],
        [op_b_rows(wout_g, pre, tk, tn), op_b_rows(wout_g, pre, tk, tn, koff=ATTN_W // tk)],
        [(0, 0, 0), (1, 1, 0)], 1, [tile_mn(h, tm, tn)], [out_mn(T, D_MODEL, tm, tn, F32)],
        lambda accs, xs: (xs[0] + accs[0],), (tm, tn))
    res = dict(n2=n2, z_p=z_p, cqn=cqn, ckvn=ckvn, q_raw=q_raw, kk_raw=kk_raw, vv=vv, q_full=q_full, k_full=k_full,
               a_out=a_out, lse=lse, mixed_a=mixed_a, mixed_g=mixed_g)
    return h2, res


def mixer_bwd(tag, dh2, dh2_bf, h, w, tabs, wout_g, pre, r, after=None):
    T = h.shape[0]
    g = {}
    (dmixed,) = mm_simple(f"{tag}_dmixed", dh2_bf, lambda tk, tn: op_b_rows_t(wout_g, pre, tk, tn), T, D_MODEL, D_MODEL,
                          512, 512, D_MODEL, after=after)
    (dwo_a,) = mm_simple(f"{tag}_dwout_a", r["mixed_a"], lambda tk, tn: op_b(dh2_bf, tk, tn), ATTN_W, D_MODEL, T,
                         1024, 1024, 512, a_t=True, out_dtype=BF16)
    (dwo_g,) = mm_simple(f"{tag}_dwout_g", r["mixed_g"], lambda tk, tn: op_b(dh2_bf, tk, tn), GM_W, D_MODEL, T,
                         1024, 1024, 512, a_t=True, out_dtype=BF16)
    g["w_out"] = jnp.concatenate([dwo_a, dwo_g], axis=0)
    da_out, g["attn_out_norm"], delta = rms_bwd(f"{tag}_ao_rms_bwd", r["a_out"], w["attn_out_norm"], dmixed, ATTN_W,
                                                with_delta=True)
    dq_full, dk_full, dvv = attn_bwd(tag, r["q_full"], r["k_full"], r["vv"], da_out, r["lse"], delta)
    dq_raw, dkk_raw, dzkr, g["gq_p"], g["gk_p"] = qk_prep_bwd(tag, dq_full, dk_full, r["q_raw"], r["kk_raw"], r["z_p"],
                                                            w["gq_p"], w["gk_p"], tabs)
    (g["wq_p"],) = mm_simple(f"{tag}_dwq", r["cqn"], lambda tk, tn: op_b(dq_raw, tk, tn), Q_RANK, 2048, T, Q_RANK, 1024, 512,
                             a_t=True, out_dtype=BF16)
    (g["wk_p"],) = mm_simple(f"{tag}_dwk", r["ckvn"], lambda tk, tn: op_b(dkk_raw, tk, tn), KV_RANK, 2048, T, KV_RANK, 1024,
                             512, a_t=True, out_dtype=BF16)
    (g["wv"],) = mm_simple(f"{tag}_dwv", r["ckvn"], lambda tk, tn: op_b(dvv, tk, tn), KV_RANK, ATTN_W, T, KV_RANK, 1024, 512,
                           a_t=True, out_dtype=BF16)
    (dcqn,) = mm_simple(f"{tag}_dcqn", dq_raw, lambda tk, tn: op_bt(w["wq_p"], tk, tn), T, Q_RANK, 2048, 512, Q_RANK, 2048)
    (dck1,) = mm_simple(f"{tag}_dckvn_k", dkk_raw, lambda tk, tn: op_bt(w["wk_p"], tk, tn), T, KV_RANK, 2048, 512, KV_RANK,
                        2048)
    (dckvn,) = mm_simple(f"{tag}_dckvn_v", dvv, lambda tk, tn: op_bt(w["wv"], tk, tn), T, KV_RANK, ATTN_W, 512, KV_RANK,
                         ATTN_W, extras=[tile_mn(dck1, 512, KV_RANK)], epilogue=lambda accs, xs: (accs[0] + xs[0],))
    dc_q, g["q_a_norm"] = rms_bwd(f"{tag}_cq_rms_bwd", r["z_p"], w["q_a_norm"], dcqn, Q_RANK, col_blk=0)
    dc_kv, g["kv_a_norm"] = rms_bwd(f"{tag}_ckv_rms_bwd", r["z_p"], w["kv_a_norm"], dckvn, KV_RANK, col_blk=2)
    du, dv, g["wc"], g["bb"], g["gm_v_norm"], g["gm_out_norm"] = gmlp_bwd(
        tag, r["z_p"], dmixed, w["gm_v_norm"], w["gm_out_norm"], w["wc"], w["bb"])
    dz_p = jnp.concatenate([dc_q, dc_kv, dzkr, du, dv], axis=1).astype(BF16)
    (g["w_in_pt"],) = mm_simple(f"{tag}_dwin", dz_p, lambda tk, tn: op_b(r["n2"], tk, tn), IN_P, D_MODEL, T, 1024, 1024, 512,
                                a_t=True, out_dtype=BF16)
    (dn2,) = mm_simple(f"{tag}_dn2", dz_p, lambda tk, tn: op_b(w["w_in_pt"], tk, tn), T, D_MODEL, IN_P, 512, 1024, IN_P)
    dh1, g["mix_norm"], dh1_bf = rms_bwd(f"{tag}_mix_rms_bwd", h, w["mix_norm"], dn2, D_MODEL, dres=dh2, bf16_copy=True)
    return dh1, dh1_bf, g


def ple_fwd(tag, h3, p_l, w, wpg_g, wple_g, pre):
    T = h3.shape[0]
    (pw,) = mm_simple(f"{tag}_wple", p_l, lambda tk, tn: op_b_cols(wple_g, pre, tk, tn), T, D_MODEL, PLE_DIM, 512, 512,
                      PLE_DIM)
    e = rms_fwd(f"{tag}_ple_rms", pw, w["ple_norm"], D_MODEL, out_dtype=F32)
    n4 = rms_fwd(f"{tag}_pg_rms", h3, w["ple_gate_norm"], D_MODEL)

    def epi(accs, xs):
        gt = _sigmoid(accs[0])
        return xs[0] + gt * xs[1], gt

    tm, tn, tk = 512, 1024, 512
    h4, gate = matmul(
        f"{tag}_wpg", (T // tm, D_MODEL // tn, D_MODEL // tk),
        [op_a(n4, tm, tk)], [op_b_rows(wpg_g, pre, tk, tn)], [(0, 0, 0)], 1,
        [tile_mn(h3, tm, tn), tile_mn(e, tm, tn)],
        [out_mn(T, D_MODEL, tm, tn, F32), out_mn(T, D_MODEL, tm, tn, BF16)], epi, (tm, tn))
    return h4, dict(pw=pw, e=e, n4=n4, gate=gate)


def ple_bwd(tag, dh4, h3, p_l, w, wpg_g, wple_g, pre, r, tm=256):
    T = h3.shape[0]

    def act_body(d_ref, g_ref, e_ref, dpre_ref, de_ref):
        d, gt = d_ref[...], g_ref[...].astype(F32)
        dpre_ref[...] = (d * e_ref[...] * gt * (1.0 - gt)).astype(BF16)
        de_ref[...] = d * gt

    blk = pl.BlockSpec((tm, D_MODEL), lambda i: (i, 0))
    dpre, de = pl.pallas_call(
        act_body, name=f"{tag}_ple_act_bwd", grid=(T // tm,), in_specs=[blk, blk, blk], out_specs=[blk, blk],
        out_shape=[jax.ShapeDtypeStruct((T, D_MODEL), BF16), jax.ShapeDtypeStruct((T, D_MODEL), F32)],
        compiler_params=_params(("parallel",)),
    )(dh4, r["gate"], r["e"])
    g = {}
    (g["w_ple_gate"],) = mm_simple(f"{tag}_dwpg", r["n4"], lambda tk, tn: op_b(dpre, tk, tn), D_MODEL, D_MODEL, T,
                                   1024, 1024, 512, a_t=True, out_dtype=BF16)
    (dn4,) = mm_simple(f"{tag}_dn4", dpre, lambda tk, tn: op_b_rows_t(wpg_g, pre, tk, tn), T, D_MODEL, D_MODEL, 512, 512,
                       D_MODEL)
    dh3, g["ple_gate_norm"], dh3_bf = rms_bwd(f"{tag}_pg_rms_bwd", h3, w["ple_gate_norm"], dn4, D_MODEL, dres=dh4,
                                              bf16_copy=True)
    dpw, g["ple_norm"] = rms_bwd(f"{tag}_ple_rms_bwd", r["pw"], w["ple_norm"], de, D_MODEL)
    (g["w_ple"],) = mm_simple(f"{tag}_dwple", p_l, lambda tk, tn: op_b(dpw, tk, tn), PLE_DIM, D_MODEL, T, PLE_DIM, 512, 512,
                              a_t=True, outs=[out_cols(PLE_DIM, 512, PLE_DIM, 512, BF16)])
    return dh3, dh3_bf, g


def loss_grad(y, target, tm=256):
    T = y.shape[0]

    def body(y_ref, t_ref, dy_ref, l_ref):
        i = pl.program_id(0)
        d = y_ref[...] - t_ref[...]
        dy_ref[...] = d * (1.0 / D_MODEL)
        part = jnp.sum((d * d).reshape(tm // 8, 8, D_MODEL), axis=0)

        @pl.when(i == 0)
        def _():
            l_ref[...] = part

        @pl.when(i > 0)
        def _():
            l_ref[...] += part

    blk = pl.BlockSpec((tm, D_MODEL), lambda i: (i, 0))
    dy, part = pl.pallas_call(
        body, name="loss_grad", grid=(T // tm,), in_specs=[blk, blk],
        out_specs=[blk, pl.BlockSpec((8, D_MODEL), lambda i: (0, 0))],
        out_shape=[jax.ShapeDtypeStruct((T, D_MODEL), F32), jax.ShapeDtypeStruct((8, D_MODEL), F32)],
        compiler_params=_params(("arbitrary",)),
    )(y, target)
    return dy, 0.5 * jnp.sum(part) / D_MODEL


def _unshard_cols(g_l):
    return g_l.transpose(1, 0, 2).reshape(g_l.shape[1], -1)


def _shard_cols(w):
    return w.reshape(w.shape[0], N_CHIPS, -1).transpose(1, 0, 2)


def layer_weights(l, Gl, small):
    w = {k: small[k][l] for k in ("mix_norm", "q_a_norm", "kv_a_norm", "gm_v_norm", "attn_out_norm", "gm_out_norm",
                                  "ple_gate_norm", "ple_norm")}
    wint = Gl["w_in"][:, :IN_SHARD].reshape(-1, D_MODEL)
    z = lambda n: jnp.zeros((n, D_MODEL), BF16)
    w["w_in_pt"] = jnp.concatenate([wint[:768], z(128), wint[768:832], z(64), wint[832:]], axis=0)
    wuq = _unshard_cols(Gl["w_uq"]).reshape(Q_RANK, HEADS, QK_DIM)
    w["wq_p"] = jnp.pad(wuq, ((0, 0), (0, 0), (0, HEAD_PAD - QK_DIM))).reshape(Q_RANK, HEADS * HEAD_PAD)
    wukv = _unshard_cols(Gl["w_ukv"]).reshape(KV_RANK, HEADS, QK_NOPE + V_DIM)
    w["wk_p"] = jnp.pad(wukv[:, :, :QK_NOPE], ((0, 0), (0, 0), (0, HEAD_PAD - QK_NOPE))).reshape(KV_RANK, HEADS * HEAD_PAD)
    w["wv"] = wukv[:, :, QK_NOPE:].reshape(KV_RANK, ATTN_W)
    w["gq_p"] = jnp.pad(small["q_norm"][l], (0, HEAD_PAD - QK_DIM)).reshape(1, HEAD_PAD)
    w["gk_p"] = jnp.pad(small["k_norm"][l], (0, HEAD_PAD - QK_DIM)).reshape(1, HEAD_PAD)
    tril = jnp.tril(jnp.ones((CHUNK, CHUNK), dtype=bool))
    w["wc"] = jnp.where(tril[None], small["gm_ws"][l], 0.0).astype(BF16)
    w["bb"] = jnp.broadcast_to(small["gm_bs"][l][:, :, None], (GROUPS, CHUNK, 128)).astype(F32)
    return w


def mixer_grads_to_shards(g):
    out = {}
    dwint = g["w_in_pt"]
    dwint = jnp.concatenate([dwint[:768], dwint[896:960], dwint[1024:]], axis=0).reshape(N_CHIPS, IN_SHARD, D_MODEL)
    out["w_in"] = jnp.pad(dwint, ((0, 0), (0, IN_SHARD_PAD - IN_SHARD), (0, 0)))
    dwuq = g["wq_p"].reshape(Q_RANK, HEADS, HEAD_PAD)[:, :, :QK_DIM].reshape(Q_RANK, HEADS * QK_DIM)
    out["w_uq"] = _shard_cols(dwuq)
    dwukv = jnp.concatenate([g["wk_p"].reshape(KV_RANK, HEADS, HEAD_PAD)[:, :, :QK_NOPE],
                             g["wv"].reshape(KV_RANK, HEADS, V_DIM)], axis=-1).reshape(KV_RANK, HEADS * (QK_NOPE + V_DIM))
    out["w_ukv"] = _shard_cols(dwukv)
    out["w_out"] = g["w_out"].reshape(N_CHIPS, D_MODEL // N_CHIPS, D_MODEL)
    out["q_norm"] = g["gq_p"][0, :QK_DIM]
    out["k_norm"] = g["gk_p"][0, :QK_DIM]
    tril = jnp.tril(jnp.ones((CHUNK, CHUNK), dtype=bool))
    out["gm_ws"] = jnp.where(tril[None], g["wc"], 0.0)
    out["gm_bs"] = jnp.sum(g["bb"], axis=-1)
    for k in ("mix_norm", "q_a_norm", "kv_a_norm", "gm_v_norm", "attn_out_norm", "gm_out_norm"):
        out[k] = g[k][0]
    return out


def layer_fwd(l, h, p_l, Gl, small, tabs, before=None):
    before = before or {}
    h1, r_a = ffn_fwd(f"l{l}a", h, small["ffn_a_norm"][l], Gl["ffn_a_w1"], Gl["ffn_a_w3"], Gl.get("ffn_a_w2"), (),
                      before.get("down_a"))
    if "mixer" in before:
        Gl, small = before["mixer"](h1, Gl, small)
    w = layer_weights(l, Gl, small)
    h2, r_m = mixer_fwd(f"l{l}", h1, w, tabs, Gl["w_out"], ())
    if "ffn_b" in before:
        Gl = before["ffn_b"](h2, Gl)
    h3, r_b = ffn_fwd(f"l{l}b", h2, small["ffn_b_norm"][l], Gl["ffn_b_w1"], Gl["ffn_b_w3"], Gl["ffn_b_w2"], ())
    if "ple" in before:
        w = {**w, "ple_norm": w["ple_norm"] + before["ple"](h3)[0, 0]}
    h4, r_p = ple_fwd(f"l{l}", h3, p_l, w, Gl["w_ple_gate"], Gl["w_ple"], ())
    return h4, (w, h, h1, h2, h3, r_a, r_m, r_b, r_p)


def layer_bwd(l, dh, p_l, Gl, small, tabs, saved, before=None):
    w, h0, h1, h2, h3, r_a, r_m, r_b, r_p = saved
    slabs = lambda d: d.reshape(N_CHIPS, FF_PAD, D_MODEL)
    hook = lambda block: before[block](gl, dh) if before and block in before else None
    gl = {}
    dh, dh_bf, g_p = ple_bwd(f"l{l}", dh, h3, p_l, w, Gl["w_ple_gate"], Gl["w_ple"], (), r_p)
    gl["w_ple_gate"] = g_p["w_ple_gate"].reshape(N_CHIPS, D_MODEL // N_CHIPS, D_MODEL)
    gl["w_ple"] = g_p["w_ple"]
    gl["ple_gate_norm"], gl["ple_norm"] = g_p["ple_gate_norm"][0], g_p["ple_norm"][0]
    dh, dh_bf, dg, dw1, dw3, dw2 = ffn_bwd(f"l{l}b", dh, dh_bf, h2, small["ffn_b_norm"][l], r_b,
                                           Gl["ffn_b_w1"], Gl["ffn_b_w3"], Gl["ffn_b_w2"], (), hook("ffn_b"))
    gl["ffn_b_norm"] = dg[0]
    gl["ffn_b_w1"], gl["ffn_b_w3"], gl["ffn_b_w2"] = slabs(dw1), slabs(dw3), slabs(dw2)
    dh, dh_bf, g_m = mixer_bwd(f"l{l}", dh, dh_bf, h1, w, tabs, Gl["w_out"], (), r_m, hook("mixer"))
    gl.update(mixer_grads_to_shards(g_m))
    last_dw = (lambda dh_: before["ffn_a_dw"](gl, dh_)) if before and "ffn_a_dw" in before else None
    dh, _, dg, dw1, dw3, dw2 = ffn_bwd(f"l{l}a", dh, dh_bf, h0, small["ffn_a_norm"][l], r_a,
                                       Gl["ffn_a_w1"], Gl["ffn_a_w3"], Gl["ffn_a_w2"], (), hook("ffn_a"), last_dw)
    gl["ffn_a_norm"] = dg[0]
    gl["ffn_a_w1"], gl["ffn_a_w3"], gl["ffn_a_w2"] = slabs(dw1), slabs(dw3), slabs(dw2)
    return dh, gl


MESH = pl.DeviceIdType.MESH
HBM_SPEC = pl.BlockSpec(memory_space=pltpu.HBM)


def _place():
    x, y, c = lax.axis_index("x"), lax.axis_index("y"), lax.axis_index("c")
    others = [(1 - x, y), (x, 1 - y), (1 - x, 1 - y)]
    return x, y, c, 2 * x + y, others


def prep_shard(name, w, layer, rows_pad, place, after=None):
    _, ks, n = w.shape
    ksp = ks + rows_pad
    tc = 512 if n % 512 == 0 else n
    deps = [] if after is None else [after]

    def body(place_ref, x_ref, *rest):
        o_ref = rest[-1]
        o_ref[:ks] = x_ref[...].astype(BF16)
        if rows_pad:
            o_ref[ks:] = jnp.zeros((rows_pad, tc), BF16)

    return pl.pallas_call(
        body, name=name,
        grid_spec=pltpu.PrefetchScalarGridSpec(
            num_scalar_prefetch=1, grid=(n // tc,),
            in_specs=[pl.BlockSpec((None, ks, tc), lambda i, s: (layer, 0, i))] + [ANY_SPEC] * len(deps),
            out_specs=pl.BlockSpec((None, ksp, tc), lambda i, s: (s[0], 0, i))),
        out_shape=jax.ShapeDtypeStruct((N_CHIPS, ksp, n), BF16),
        compiler_params=_params(("parallel",)),
    )(place, w, *deps)


SEM_SPEC = pl.BlockSpec(memory_space=pltpu.SEMAPHORE)
ANY_SPEC = pl.BlockSpec(memory_space=pl.ANY)
DATAFLOW = pltpu.SideEffectType.DATAFLOW_SIDE_EFFECTING


def _hbm(x):
    return pltpu.with_memory_space_constraint(x, pltpu.HBM)


def _start_call(name, slots, after, issue):
    n = len(slots)
    deps = [] if after is None else [after]
    nd = len(deps)

    def body(*refs):
        issue(refs[n + nd + 2:2 * n + nd + 2], refs[n + nd], refs[n + nd + 1])
        token = refs[2 * n + nd + 2]
        token[...] = jnp.zeros_like(token)

    outs = pl.pallas_call(
        body, name=name,
        in_specs=[HBM_SPEC] * n + [ANY_SPEC] * nd,
        out_specs=(SEM_SPEC, SEM_SPEC, *([HBM_SPEC] * n), pl.BlockSpec(memory_space=pltpu.VMEM)),
        out_shape=(pltpu.SemaphoreType.DMA((n,)), pltpu.SemaphoreType.DMA((n,)),
                   *[pltpu.HBM(s.shape, s.dtype) for s in slots], jax.ShapeDtypeStruct((8, 128), F32)),
        input_output_aliases={w: w + 2 for w in range(n)},
        compiler_params=pltpu.CompilerParams(has_side_effects=DATAFLOW),
    )(*[_hbm(s) for s in slots], *deps)
    return outs[0], outs[1], list(outs[2:2 + n]), outs[2 + n]


def gather_start(name, slots, after):
    def issue(g_refs, send, recv):
        x, y, c, jme, others = _place()
        for w in range(len(slots)):
            kh = slots[w].shape[1] // 2
            mine = g_refs[w].at[jme, pl.ds(c * kh, kh)]
            for (px, py) in others:
                pltpu.make_async_remote_copy(src_ref=mine, dst_ref=mine, send_sem=send.at[w], recv_sem=recv.at[w],
                                             device_id=(px, py, c), device_id_type=MESH).start()

    return _start_call(name, slots, after, issue)


def forward_start(name, slots):
    def issue(g_refs, send, recv):
        x, y, c, _, others = _place()
        for w in range(len(slots)):
            kh = slots[w].shape[1] // 2
            for (px, py) in others:
                blk = g_refs[w].at[2 * px + py, pl.ds(c * kh, kh)]
                pltpu.make_async_remote_copy(src_ref=blk, dst_ref=blk, send_sem=send.at[w], recv_sem=recv.at[w],
                                             device_id=(x, y, 1 - c), device_id_type=MESH).start()

    return _start_call(name, slots, None, issue)


def share_start(name, fulls):
    def issue(o_refs, send, recv):
        x, y, c, _, _ = _place()
        for w in range(len(fulls)):
            kh = fulls[w].shape[0] // 2
            half = o_refs[w].at[pl.ds(c * kh, kh)]
            pltpu.make_async_remote_copy(src_ref=half, dst_ref=half, send_sem=send.at[w], recv_sem=recv.at[w],
                                         device_id=(x, y, 1 - c), device_id_type=MESH).start()

    return _start_call(name, fulls, None, issue)


def share_wait(name, send, recv, flying, after):
    return _wait_call(name, send, recv, flying, after, lambda r: r.at[pl.ds(0, r.shape[0] // 2)])


def gather_wait(name, send, recv, flying, after):
    return _wait_call(name, send, recv, flying, after, lambda r: r.at[pl.ds(0, 3), pl.ds(0, r.shape[1] // 2)])


def _wait_call(name, send, recv, flying, after, landed):
    n = len(flying)

    def body(*refs):
        send_ref, recv_ref = refs[n], refs[n + 1]
        g_refs = refs[n + 3:]
        x, y, c, _, _ = _place()
        for w in range(n):
            cp = pltpu.make_async_remote_copy(src_ref=landed(g_refs[w]), dst_ref=landed(g_refs[w]),
                                              send_sem=send_ref.at[w], recv_sem=recv_ref.at[w],
                                              device_id=(x, y, 1 - c), device_id_type=MESH)
            cp.wait_send()
            cp.wait_recv()

    return pl.pallas_call(
        body, name=name,
        in_specs=[HBM_SPEC] * n + [SEM_SPEC, SEM_SPEC, ANY_SPEC],
        out_specs=[HBM_SPEC] * n,
        out_shape=[pltpu.HBM(s.shape, s.dtype) for s in flying],
        input_output_aliases={w: w for w in range(n)},
        compiler_params=pltpu.CompilerParams(has_side_effects=DATAFLOW),
    )(*flying, send, recv, after)


def _send_start(name, srcs, land_shapes, issue, after):
    n = len(srcs)
    deps = [] if after is None else [after]
    nd = len(deps)

    def body(*refs):
        base = 2 * n + nd
        issue(refs[base + 2:base + 2 + n], refs[base + 2 + n:base + 2 + 2 * n], refs[base], refs[base + 1])
        token = refs[base + 2 + 2 * n]
        token[...] = jnp.zeros_like(token)

    lands = [_hbm(lax.empty(shape, s.dtype)) for shape, s in zip(land_shapes, srcs)]
    outs = pl.pallas_call(
        body, name=name,
        in_specs=[HBM_SPEC] * (2 * n) + [ANY_SPEC] * nd,
        out_specs=(SEM_SPEC, SEM_SPEC, *([HBM_SPEC] * (2 * n)), pl.BlockSpec(memory_space=pltpu.VMEM)),
        out_shape=(pltpu.SemaphoreType.DMA((n,)), pltpu.SemaphoreType.DMA((n,)),
                   *[pltpu.HBM(s.shape, s.dtype) for s in srcs], *[pltpu.HBM(l.shape, l.dtype) for l in lands],
                   jax.ShapeDtypeStruct((8, 128), F32)),
        input_output_aliases={w: w + 2 for w in range(2 * n)},
        compiler_params=pltpu.CompilerParams(has_side_effects=DATAFLOW),
    )(*[_hbm(s) for s in srcs], *lands, *deps)
    return outs[0], outs[1], list(outs[2:2 + n]), list(outs[2 + n:2 + 2 * n]), outs[2 + 2 * n]


def _send_wait(name, send, recv, srcs, lands, after, landed):
    n = len(srcs)

    def body(*refs):
        send_ref, recv_ref = refs[2 * n], refs[2 * n + 1]
        q_refs = refs[3 * n + 3:]
        x, y, c, _, _ = _place()
        for w in range(n):
            cp = pltpu.make_async_remote_copy(src_ref=landed(q_refs[w]), dst_ref=landed(q_refs[w]), send_sem=send_ref.at[w],
                                              recv_sem=recv_ref.at[w], device_id=(x, y, 1 - c), device_id_type=MESH)
            cp.wait_send()
            cp.wait_recv()

    outs = pl.pallas_call(
        body, name=name,
        in_specs=[HBM_SPEC] * (2 * n) + [SEM_SPEC, SEM_SPEC, ANY_SPEC],
        out_specs=[HBM_SPEC] * (2 * n),
        out_shape=[pltpu.HBM(a.shape, a.dtype) for a in list(srcs) + list(lands)],
        input_output_aliases={w: w for w in range(2 * n)},
        compiler_params=pltpu.CompilerParams(has_side_effects=DATAFLOW),
    )(*srcs, *lands, send, recv, after)
    return list(outs[:n]), list(outs[n:])


def exchange_start(name, grads, after):
    def issue(d_refs, r_refs, send, recv):
        x, y, c, _, _ = _place()
        for w in range(len(grads)):
            half = grads[w].shape[1] // 2
            pltpu.make_async_remote_copy(
                src_ref=d_refs[w].at[pl.ds(0, N_CHIPS), pl.ds((1 - c) * half, half)], dst_ref=r_refs[w],
                send_sem=send.at[w], recv_sem=recv.at[w], device_id=(x, y, 1 - c), device_id_type=MESH).start()

    return _send_start(name, grads, [(N_CHIPS, g.shape[1] // 2, g.shape[2]) for g in grads], issue, after)


def exchange_wait(name, send, recv, grads, lands, after):
    return _send_wait(name, send, recv, grads, lands, after, lambda r: r)


def scatter_start(name, parts):
    def issue(p_refs, q_refs, send, recv):
        x, y, c, jme, others = _place()
        for w in range(len(parts)):
            for (px, py) in others:
                pltpu.make_async_remote_copy(
                    src_ref=p_refs[w].at[2 * px + py], dst_ref=q_refs[w].at[jme], send_sem=send.at[w], recv_sem=recv.at[w],
                    device_id=(px, py, c), device_id_type=MESH).start()

    return _send_start(name, parts, [p.shape for p in parts], issue, None)


def scatter_wait(name, send, recv, parts, lands, after):
    return _send_wait(name, send, recv, parts, lands, after, lambda r: r.at[pl.ds(0, 3)])


def allreduce_small(v):
    R = v.shape[0]

    def body(v_ref, o_ref, sib_ref, mine_ref, all_ref, d_send, d_recv, i_send, i_recv):
        x, y, c, jme, others = _place()
        swap = pltpu.make_async_remote_copy(src_ref=v_ref, dst_ref=sib_ref, send_sem=d_send, recv_sem=d_recv,
                                            device_id=(x, y, 1 - c), device_id_type=MESH)
        swap.start()
        swap.wait()
        mine_ref[...] = v_ref[...] + sib_ref[...]
        for (px, py) in others:
            pltpu.make_async_remote_copy(src_ref=mine_ref, dst_ref=all_ref.at[jme], send_sem=i_send, recv_sem=i_recv,
                                         device_id=(px, py, c), device_id_type=MESH).start()
        three = all_ref.at[pl.ds(0, 3)]
        wait3 = pltpu.make_async_remote_copy(src_ref=three, dst_ref=three, send_sem=i_send, recv_sem=i_recv,
                                             device_id=(x, y, c), device_id_type=MESH)
        wait3.wait_recv()
        wait3.wait_send()
        all_ref[jme] = mine_ref[...]
        o_ref[...] = ((all_ref[0] + all_ref[1]) + all_ref[2]) + all_ref[3]

    vm = pl.BlockSpec(memory_space=pltpu.VMEM)
    return pl.pallas_call(
        body, name="allreduce_small", in_specs=[vm], out_specs=vm,
        out_shape=jax.ShapeDtypeStruct(v.shape, F32),
        scratch_shapes=[pltpu.VMEM((R, 128), F32), pltpu.VMEM((R, 128), F32), pltpu.VMEM((N_CHIPS, R, 128), F32),
                        pltpu.SemaphoreType.DMA, pltpu.SemaphoreType.DMA, pltpu.SemaphoreType.DMA, pltpu.SemaphoreType.DMA],
        compiler_params=pltpu.CompilerParams(vmem_limit_bytes=VMEM_LIMIT_BYTES),
    )(v)


def _row_tile(rows, width, mult=16, cap=3 << 20):
    best = rows
    for t in range(mult, rows + 1, mult):
        if rows % t == 0 and t * width * 4 <= cap:
            best = t
    return best


def add_sibling(name, mine, theirs, place):
    _, kh, ns = theirs.shape
    tr = _row_tile(kh, ns)
    nblk = kh // tr

    def body(place_ref, a_ref, b_ref, o_ref):
        o_ref[...] = (a_ref[...].astype(F32) + b_ref[...].astype(F32)).astype(BF16)

    return pl.pallas_call(
        body, name=name,
        grid_spec=pltpu.PrefetchScalarGridSpec(
            num_scalar_prefetch=1, grid=(N_CHIPS, nblk),
            in_specs=[pl.BlockSpec((None, tr, ns), lambda j, i, s: (j, s[1] * nblk + i, 0)),
                      pl.BlockSpec((None, tr, ns), lambda j, i, s: (j, i, 0))],
            out_specs=pl.BlockSpec((None, tr, ns), lambda j, i, s: (j, i, 0))),
        out_shape=jax.ShapeDtypeStruct(theirs.shape, BF16),
        compiler_params=_params(("parallel", "parallel")),
    )(place, mine, theirs)


def add_chips(name, q, p, place):
    _, kh, ns = q.shape
    tr = _row_tile(kh, ns)
    nblk = kh // tr

    def body(place_ref, *refs):
        q_refs, own_ref, o_ref = refs[:N_CHIPS], refs[N_CHIPS], refs[-1]
        jme = place_ref[0]
        tot = None
        for j in range(N_CHIPS):
            v = jnp.where(jme == j, own_ref[...], q_refs[j][...]).astype(F32)
            tot = v if tot is None else tot + v
        o_ref[...] = tot

    def q_ix(j):
        return lambda i, s: (jnp.where(s[0] == j, (j + 1) % N_CHIPS, j), i, 0)

    in_specs = [pl.BlockSpec((None, tr, ns), q_ix(j)) for j in range(N_CHIPS)]
    in_specs.append(pl.BlockSpec((None, tr, ns), lambda i, s: (s[0], i, 0)))
    return pl.pallas_call(
        body, name=name,
        grid_spec=pltpu.PrefetchScalarGridSpec(
            num_scalar_prefetch=1, grid=(nblk,), in_specs=in_specs,
            out_specs=pl.BlockSpec((tr, ns), lambda i, s: (s[1] * nblk + i, 0))),
        out_shape=jax.ShapeDtypeStruct((2 * kh, ns), F32),
        compiler_params=_params(("parallel",)),
    )(place, q, q, q, q, p)


ADAM_LR, ADAM_B1, ADAM_B2, ADAM_EPS, ADAM_WD, ADAM_STEP = 0.001, 0.9, 0.999, 1e-08, 0.01, 10


def adamw(name, w, g, m, v, layer, prev=None, after=None):
    _, k, ns = w.shape
    nsp = g.shape[1]
    tr = _row_tile(k, nsp, mult=8, cap=3 << 20)

    def body(w_ref, g_ref, m_ref, v_ref, *rest):
        go_ref, d_ref, mo_ref, vo_ref = rest[-4:]
        gv = g_ref[:, :ns] if nsp != ns else g_ref[...]
        mn = ADAM_B1 * m_ref[...] + (1.0 - ADAM_B1) * gv
        vn = ADAM_B2 * v_ref[...] + (1.0 - ADAM_B2) * (gv * gv)
        m_hat = mn / (1.0 - ADAM_B1 ** ADAM_STEP)
        v_hat = vn / (1.0 - ADAM_B2 ** ADAM_STEP)
        go_ref[...] = gv
        d_ref[...] = -ADAM_LR * (m_hat / (jnp.sqrt(v_hat) + ADAM_EPS) + ADAM_WD * w_ref[...])
        mo_ref[...] = mn
        vo_ref[...] = vn

    blk = pl.BlockSpec((None, tr, ns), lambda i: (layer, i, 0))
    gblk = pl.BlockSpec((tr, nsp), lambda i: (i, 0))
    args, in_specs, aliases = [w, g, m, v], [blk, gblk, blk, blk], {}
    if prev is not None:
        args += list(prev)
        in_specs += [pl.BlockSpec(memory_space=pl.ANY)] * 4
        aliases = {4 + i: i for i in range(4)}
    if after is not None:
        args.append(after)
        in_specs.append(pl.BlockSpec(memory_space=pl.ANY))
    return pl.pallas_call(
        body, name=name, grid=(k // tr,), in_specs=in_specs, out_specs=[blk] * 4,
        out_shape=[jax.ShapeDtypeStruct(w.shape, F32)] * 4, input_output_aliases=aliases,
        compiler_params=_params(("parallel",)),
    )(*args)


WEIGHTS = ("ffn_a_norm", "ffn_a_w1", "ffn_a_w3", "ffn_a_w2", "mix_norm", "w_in", "q_a_norm", "w_uq", "kv_a_norm", "w_ukv",
           "q_norm", "k_norm", "gm_v_norm", "gm_ws", "gm_bs", "attn_out_norm", "gm_out_norm", "w_out", "ffn_b_norm",
           "ffn_b_w1", "ffn_b_w3", "ffn_b_w2", "ple_gate_norm", "w_ple_gate", "w_ple", "ple_norm")
_FF = FF_PAD - FF_SHARD
BIG = {"ffn_a_w1": _FF, "ffn_a_w3": _FF, "ffn_a_w2": _FF, "ffn_b_w1": _FF, "ffn_b_w3": _FF, "ffn_b_w2": _FF,
       "w_in": IN_SHARD_PAD - IN_SHARD, "w_uq": 0, "w_ukv": 0, "w_ple": 0, "w_out": 0, "w_ple_gate": 0}
TRANSPOSED = ("ffn_a_w1", "ffn_a_w3", "ffn_b_w1", "ffn_b_w3", "w_in")
SMALL = tuple(n for n in WEIGHTS if n not in BIG)
PACK = 1024


def _pack_small(d):
    parts = []
    for n in SMALL:
        flat = d[n].reshape(-1)
        parts.append(jnp.pad(flat, (0, (-flat.shape[0]) % PACK)))
    return jnp.concatenate(parts).reshape(-1, 128)


def _unpack_small(buf, like):
    flat = buf.reshape(-1)
    out, pos = {}, 0
    for n in SMALL:
        size = math.prod(like[n].shape)
        out[n] = flat[pos:pos + size].reshape(like[n].shape)
        pos += size + (-size) % PACK
    return out


def kernel(*args):
    names = (("x", "p", "positions") + WEIGHTS + ("loss_target",) + tuple("m_" + n for n in WEIGHTS)
             + tuple("v_" + n for n in WEIGHTS))
    a = dict(zip(names, args, strict=True))
    x, p, positions, target = a["x"][0], a["p"][:, 0], a["positions"][0], a["loss_target"][0]
    for n in TRANSPOSED:
        for pre in ("", "m_", "v_"):
            a[pre + n] = jnp.swapaxes(a[pre + n], 1, 2)

    place = jnp.stack([2 * lax.axis_index("x") + lax.axis_index("y"), lax.axis_index("c")]).astype(jnp.int32)
    small = {n: a[n] for n in SMALL}
    tabs = rope_tables(positions)
    order = {"l0a": ("ffn_a_w1", "ffn_a_w3"), "l0b": ("ffn_a_w2",), "l0c": ("w_in", "w_uq", "w_ukv", "w_out"),
             "l0d": ("ffn_b_w1", "ffn_b_w3", "ffn_b_w2", "w_ple_gate", "w_ple")}
    prep = lambda n, l, after: prep_shard(f"prep_{n}_{l}", a[n], l, BIG[n], place, after)
    flights, token = {}, None
    for tag, names in order.items():
        flights[tag] = gather_start(f"gather_{tag}_start", [prep(n, 0, token) for n in names], None)
        token = flights[tag][3]
    slots1 = []
    for n in BIG:
        slots1.append(prep(n, 1, slots1[-1] if slots1 else token))

    def arrive(tag, after):
        send, recv, flying, _ = flights[tag]
        arrived = gather_wait(f"gather_{tag}_wait", send, recv, flying, after)
        send, recv, flying, token = forward_start(f"forward_{tag}_start", arrived)
        return dict(zip(order[tag], gather_wait(f"forward_{tag}_wait", send, recv, flying, token)))

    G0 = arrive("l0a", slots1[-1])

    def before_down(s):
        G0.update(arrive("l0b", s))
        return G0["ffn_a_w2"]

    def before_mixer(h1, Gl, small_):
        G0.update(arrive("l0c", h1))
        flights["l1"] = gather_start("gather_l1_start", slots1, G0["w_uq"])
        return G0, {**small_, "mix_norm": small_["mix_norm"] + flights["l1"][3][0, 0]}

    def before_ffn_b(h2, Gl):
        G0.update(arrive("l0d", h2))
        return G0

    def before_ple(h3):
        send, recv, flying, _ = flights["l1"]
        flights["f1"] = forward_start("forward_l1_start", gather_wait("gather_l1_wait", send, recv, flying, h3))
        return flights["f1"][3]

    h, saved0 = layer_fwd(0, x, p[0], G0, small, tabs,
                          {"down_a": before_down, "mixer": before_mixer, "ffn_b": before_ffn_b, "ple": before_ple})
    G1 = dict(zip(BIG, gather_wait("forward_l1_wait", *flights["f1"][:3], h)))
    h, saved1 = layer_fwd(1, h, p[1], G1, small, tabs)
    dh, loss = loss_grad(h, target)
    loss = lax.psum(loss, ("x", "y", "c"))

    groups = {"l1": tuple(BIG),
              "l0a": ("w_ple_gate", "w_ple", "ffn_b_w1", "ffn_b_w3", "ffn_b_w2"),
              "l0b": ("w_in", "w_uq", "w_ukv", "w_out"),
              "l0c": ("ffn_a_w1", "ffn_a_w3", "ffn_a_w2")}
    crossing, started = [], {}

    def begin(tag, gl, after):
        ex = exchange_start(f"exchange_{tag}_start", [gl[n] for n in groups[tag]], after)
        crossing.append((tag, ex))
        return ex[4]

    def advance(after):
        tag, (send, recv, mine, lands, _) = crossing.pop()
        mine, theirs = exchange_wait(f"exchange_{tag}_wait", send, recv, mine, lands, after)
        parts = [add_sibling(f"add_sibling_{n}_{tag}", d, r, place) for n, d, r in zip(groups[tag], mine, theirs)]
        started[tag] = scatter_start(f"scatter_{tag}_start", parts)
        return started[tag][4]

    def sum_chips(tag, after):
        send, recv, parts, lands, _ = started[tag]
        parts, slabs = scatter_wait(f"scatter_{tag}_wait", send, recv, parts, lands, after)
        halves = [add_chips(f"add_chips_{n}_{tag}", q, pt, place) for n, q, pt in zip(groups[tag], slabs, parts)]
        return share_start(f"share_{tag}_start", halves)

    def shared(tag, sharing, after):
        send, recv, flying, _ = sharing
        return dict(zip(groups[tag], share_wait(f"share_{tag}_wait", send, recv, flying, after)))

    def update(names, full, layer, prev, after):
        outs = {}
        for n in names:
            outs[n] = adamw(f"adamw_{n}_{layer}", a[n], full[n], a["m_" + n], a["v_" + n], layer, prev and prev[n], after)
            after = outs[n][1]
        return outs, after

    grads = [None, None]
    dh, grads[1] = layer_bwd(1, dh, p[1], G1, small, tabs, saved1)
    token = begin("l1", grads[1], None)
    w0 = {**saved0[0], "ple_gate_norm": saved0[0]["ple_gate_norm"] + token[0, 0]}
    hooks = {"ffn_b": lambda gl, dh_: advance(dh_),
             "mixer": lambda gl, dh_: begin("l0a", gl, None),
             "ffn_a": lambda gl, dh_: begin("l0b", gl, advance(dh_)),
             "ffn_a_dw": lambda gl, dh_: advance(dh_)}
    gx, grads[0] = layer_bwd(0, dh, p[0], G0, small, tabs, (w0,) + saved0[1:], hooks)
    token = begin("l0c", grads[0], None)
    sharing = sum_chips("l1", token)
    full1 = shared("l1", sharing, advance(sharing[3]))
    outs1, behind = update(BIG, full1, 1, None, None)
    sharing_a = sum_chips("l0a", behind)
    sharing_b = sum_chips("l0b", sharing_a[3])
    full0 = shared("l0a", sharing_a, sharing_b[3])
    outs0, behind = update(groups["l0a"], full0, 0, outs1, None)
    sharing_c = sum_chips("l0c", behind)
    full0.update(shared("l0b", sharing_b, sharing_c[3]))
    outs, behind = update(groups["l0b"], full0, 0, outs1, None)
    outs0.update(outs)
    full0.update(shared("l0c", sharing_c, behind))
    outs0.update(update(groups["l0c"], full0, 0, outs1, None)[0])

    out_g, out_d, out_m, out_v = {}, {}, {}, {}
    for n in BIG:
        outs = [jnp.swapaxes(o, 1, 2) for o in outs0[n]] if n in TRANSPOSED else outs0[n]
        out_g[n], out_d[n], out_m[n], out_v[n] = outs

    gs = allreduce_small(_pack_small({n: jnp.stack([grads[0][n], grads[1][n]]) for n in SMALL}))
    rows = gs.shape[0] // 2
    packed = [_pack_small(d).reshape(2, rows, 128) for d in
              (small, {n: a["m_" + n] for n in SMALL}, {n: a["v_" + n] for n in SMALL})]
    gs = gs.reshape(2, rows, 128)
    sm = adamw("adamw_small_0", packed[0], gs[0], packed[1], packed[2], 0)
    sm = adamw("adamw_small_1", packed[0], gs[1], packed[1], packed[2], 1, sm)
    for dst, buf in zip((out_g, out_d, out_m, out_v), sm):
        dst.update(_unpack_small(buf, small))

    return (loss, gx[None], *[out_g[n] for n in WEIGHTS], *[out_d[n] for n in WEIGHTS],
            *[out_m[n] for n in WEIGHTS], *[out_v[n] for n in WEIGHTS])
```

```python
import math

import jax
import jax.numpy as jnp
from jax import lax
from jax.experimental import pallas as pl
from jax.experimental.pallas import tpu as pltpu

F32 = jnp.float32
BF16 = jnp.bfloat16

D_MODEL = 2048
D_FF = 5504
N_CHIPS = 4
FF_SHARD = D_FF // N_CHIPS
FF_PAD = 1408
FF_P = N_CHIPS * FF_PAD
HEADS = 8
QK_NOPE = 128
QK_ROPE = 64
QK_DIM = 192
HEAD_PAD = 256
V_DIM = 128
Q_RANK = 512
KV_RANK = 256
ATTN_W = 1024
GM_W = 1024
GROUPS = 8
CHUNK = 128
PLE_DIM = 256
IN_P = 3072
IN_SHARD = 720
IN_SHARD_PAD = 736
EPS = 1e-6
ROPE_BASE = 10000.0
ATTN_SCALE = QK_DIM ** -0.5
VMEM_LIMIT_BYTES = 56 * 1024 * 1024


def _params(sem):
    return pltpu.CompilerParams(dimension_semantics=sem, vmem_limit_bytes=VMEM_LIMIT_BYTES)


def _bf(x):
    return x if x.dtype == BF16 else x.astype(BF16)


def _sigmoid(x):
    return 1.0 / (1.0 + jnp.exp(-x))


_GELU_C = math.sqrt(2.0 / math.pi)


def _gelu(x):
    t = jnp.tanh(_GELU_C * (x + 0.044715 * x * x * x))
    return 0.5 * x * (1.0 + t)


def _gelu_grad(x):
    t = jnp.tanh(_GELU_C * (x + 0.044715 * x * x * x))
    return 0.5 * (1.0 + t) + 0.5 * x * (1.0 - t * t) * _GELU_C * (1.0 + 3 * 0.044715 * x * x)


def op_a(a, tm, tk):
    return (a, (tm, tk), lambda i, j, k: (i, k), 1)


def op_at(a, tm, tk):
    return (a, (tk, tm), lambda i, j, k: (k, i), 0)


def op_b(b, tk, tn):
    return (b, (tk, tn), lambda i, j, k: (k, j), 0)


def op_bt(b, tk, tn):
    return (b, (tn, tk), lambda i, j, k: (j, k), 1)


def op_b_cols(g, pre, tk, tn):
    nb = g.shape[-1] // tn
    none = (None,) * (1 + len(pre))
    return (g, none + (tk, tn), lambda i, j, k: (j // nb,) + tuple(pre) + (k, j % nb), 0)


def op_b_rows(g, pre, tk, tn, koff=0):
    nb = g.shape[-2] // tk
    none = (None,) * (1 + len(pre))
    return (g, none + (tk, tn), lambda i, j, k: ((k + koff) // nb,) + tuple(pre) + ((k + koff) % nb, j), 0)


def op_b_rows_t(g, pre, tk, tn):
    nb = g.shape[-2] // tn
    none = (None,) * (1 + len(pre))
    return (g, none + (tn, tk), lambda i, j, k: (j // nb,) + tuple(pre) + (j % nb, k), 1)


def tile_mn(x, tm, tn):
    return (x, (tm, tn), lambda i, j: (i, j))


def out_mn(M, N, tm, tn, dtype):
    return (jax.ShapeDtypeStruct((M, N), dtype), (tm, tn), lambda i, j: (i, j))


def out_cols(M, ns, tm, tn, dtype):
    nb = ns // tn
    return (jax.ShapeDtypeStruct((N_CHIPS, M, ns), dtype), (None, tm, tn), lambda i, j: (j // nb, i, j % nb))


def matmul(name, grid_mnk, a_ops, b_ops, terms, n_acc, extras, outs, epilogue, acc_tile, n_outer=False, after=None):
    gm, gn, gk = grid_mnk
    na, nb, nx, no = len(a_ops), len(b_ops), len(extras), len(outs)
    nd = 0 if after is None else 1

    def body(*refs):
        a_refs, b_refs = refs[:na], refs[na:na + nb]
        x_refs = refs[na + nb:na + nb + nx]
        o_refs = refs[na + nb + nx + nd:na + nb + nx + nd + no]
        acc_refs = refs[na + nb + nx + nd + no:]
        k = pl.program_id(2)

        @pl.when(k == 0)
        def _():
            for acc in acc_refs:
                acc[...] = jnp.zeros_like(acc)

        for ai, bi, ci in terms:
            dims = (((a_ops[ai][3],), (b_ops[bi][3],)), ((), ()))
            acc_refs[ci][...] += lax.dot_general(_bf(a_refs[ai][...]), _bf(b_refs[bi][...]), dims,
                                                 preferred_element_type=F32)

        @pl.when(k == gk - 1)
        def _():
            res = epilogue([acc[...] for acc in acc_refs], [x[...] for x in x_refs])
            for o, v in zip(o_refs, res):
                o[...] = v.astype(o.dtype)

    if n_outer:
        grid = (gn, gm, gk)

        def ix3(f):
            return lambda j, i, k: f(i, j, k)

        def ix2(f):
            return lambda j, i, k: f(i, j)
    else:
        grid = (gm, gn, gk)

        def ix3(f):
            return lambda i, j, k: f(i, j, k)

        def ix2(f):
            return lambda i, j, k: f(i, j)

    in_specs = [pl.BlockSpec(blk, ix3(f)) for (_, blk, f, _) in list(a_ops) + list(b_ops)]
    in_specs += [pl.BlockSpec(blk, ix2(f)) for (_, blk, f) in extras]
    in_specs += [pl.BlockSpec(memory_space=pl.ANY)] * nd
    out_specs = [pl.BlockSpec(blk, ix2(f)) for (_, blk, f) in outs]
    return pl.pallas_call(
        body,
        name=name,
        grid=grid,
        in_specs=in_specs,
        out_specs=out_specs,
        out_shape=[s for (s, _, _) in outs],
        scratch_shapes=[pltpu.VMEM(acc_tile, F32) for _ in range(n_acc)],
        compiler_params=_params(("parallel", "parallel", "arbitrary")),
    )(*[o[0] for o in a_ops], *[o[0] for o in b_ops], *[x[0] for x in extras], *([after] * nd))


def _acc0(accs, xs):
    return (accs[0],)


def mm_simple(name, a, b_op_fn, M, N, K, tm, tn, tk, out_dtype=F32, a_t=False, extras=(), epilogue=_acc0, outs=None,
              after=None):
    a_op = op_at(a, tm, tk) if a_t else op_a(a, tm, tk)
    outs = outs or [out_mn(M, N, tm, tn, out_dtype)]
    return matmul(name, (M // tm, N // tn, K // tk), [a_op], [b_op_fn(tk, tn)], [(0, 0, 0)], 1,
                  list(extras), outs, epilogue, (tm, tn), after=after)


def rms_fwd(name, x, g, width, col_blk=0, tm=512, out_dtype=BF16):
    T = x.shape[0]

    def body(x_ref, g_ref, o_ref):
        xv = x_ref[...].astype(F32)
        r = lax.rsqrt(jnp.mean(xv * xv, axis=-1, keepdims=True) + EPS)
        o_ref[...] = (xv * r * g_ref[...]).astype(o_ref.dtype)

    return pl.pallas_call(
        body, name=name, grid=(T // tm,),
        in_specs=[pl.BlockSpec((tm, width), lambda i: (i, col_blk)), pl.BlockSpec((1, width), lambda i: (0, 0))],
        out_specs=pl.BlockSpec((tm, width), lambda i: (i, 0)),
        out_shape=jax.ShapeDtypeStruct((T, width), out_dtype),
        compiler_params=_params(("parallel",)),
    )(x, g.reshape(1, width))


def rms_bwd(name, x, g, dn, width, col_blk=0, dres=None, tm=512, with_delta=False, bf16_copy=False):
    T = x.shape[0]
    has_res = dres is not None

    def body(*refs):
        x_ref, g_ref, dn_ref = refs[:3]
        pos = 3
        res_ref = None
        if has_res:
            res_ref = refs[pos]
            pos += 1
        dx_ref, dg_ref = refs[pos], refs[pos + 1]
        delta_ref = refs[pos + 2] if with_delta else None
        lo_ref = refs[-1] if bf16_copy else None
        i = pl.program_id(0)
        xv = x_ref[...].astype(F32)
        r = lax.rsqrt(jnp.mean(xv * xv, axis=-1, keepdims=True) + EPS)
        xh = xv * r
        d = dn_ref[...].astype(F32)
        gd = d * g_ref[...]
        dx = r * (gd - xh * jnp.mean(gd * xh, axis=-1, keepdims=True))
        if has_res:
            dx = dx + res_ref[...]
        dx_ref[...] = dx.astype(dx_ref.dtype)
        if bf16_copy:
            lo_ref[...] = dx.astype(BF16)
        part = jnp.sum(d * xh, axis=0, keepdims=True)

        @pl.when(i == 0)
        def _():
            dg_ref[...] = part

        @pl.when(i > 0)
        def _():
            dg_ref[...] += part

        if with_delta:
            for h in range(width // 128):
                sl = slice(h * 128, (h + 1) * 128)
                s = jnp.sum(dx[:, sl] * xv[:, sl], axis=-1, keepdims=True)
                delta_ref[:, sl] = jnp.broadcast_to(s, (tm, 128))

    in_specs = [pl.BlockSpec((tm, width), lambda i: (i, col_blk)), pl.BlockSpec((1, width), lambda i: (0, 0)),
                pl.BlockSpec((tm, width), lambda i: (i, 0))]
    args = [x, g.reshape(1, width), dn]
    if has_res:
        in_specs.append(pl.BlockSpec((tm, width), lambda i: (i, 0)))
        args.append(dres)
    out_specs = [pl.BlockSpec((tm, width), lambda i: (i, 0)), pl.BlockSpec((1, width), lambda i: (0, 0))]
    out_shape = [jax.ShapeDtypeStruct((T, width), F32), jax.ShapeDtypeStruct((1, width), F32)]
    if with_delta:
        out_specs.append(pl.BlockSpec((tm, width), lambda i: (i, 0)))
        out_shape.append(jax.ShapeDtypeStruct((T, width), F32))
    if bf16_copy:
        out_specs.append(pl.BlockSpec((tm, width), lambda i: (i, 0)))
        out_shape.append(jax.ShapeDtypeStruct((T, width), BF16))
    return pl.pallas_call(
        body, name=name, grid=(T // tm,), in_specs=in_specs, out_specs=out_specs, out_shape=out_shape,
        compiler_params=_params(("arbitrary",)),
    )(*args)


def ffn_fwd(tag, h, g, w1g, w3g, w2g, pre, w2_late=None):
    T = h.shape[0]
    n = rms_fwd(f"{tag}_rms", h, g, D_MODEL)
    tm, tn = 512, FF_PAD

    def up_epi(accs, xs):
        a1, a3 = accs
        return a1, a3, a1 * _sigmoid(a1) * a3

    a1, a3, s = matmul(
        f"{tag}_up", (T // tm, FF_P // tn, 1),
        [op_a(n, tm, D_MODEL)], [op_b_rows_t(w1g, pre, D_MODEL, tn), op_b_rows_t(w3g, pre, D_MODEL, tn)],
        [(0, 0, 0), (0, 1, 1)], 2, [],
        [out_mn(T, FF_P, tm, tn, BF16)] * 3, up_epi, (tm, tn), n_outer=True)

    if w2_late is not None:
        w2g = w2_late(s)
    tm2, tn2 = 1024, 1024
    (h_out,) = matmul(
        f"{tag}_down", (T // tm2, D_MODEL // tn2, N_CHIPS),
        [op_a(s, tm2, FF_PAD)], [op_b_rows(w2g, pre, FF_PAD, tn2)],
        [(0, 0, 0)], 1, [tile_mn(h, tm2, tn2)],
        [out_mn(T, D_MODEL, tm2, tn2, F32)], lambda accs, xs: (xs[0] + 0.5 * accs[0],), (tm2, tn2))
    return h_out, (n, a1, a3, s)


def ffn_bwd(tag, dh_out, dh_bf, h, g, res, w1g, w3g, w2g, pre, after=None, before_dw=None):
    n, a1, a3, s = res
    T = h.shape[0]
    tm, tn = 512, FF_PAD

    def act_epi(accs, xs):
        ds = 0.5 * accs[0]
        x1, x3 = xs[0].astype(F32), xs[1].astype(F32)
        sg = _sigmoid(x1)
        silu = x1 * sg
        return ds * x3 * (sg + silu * (1.0 - sg)), ds * silu

    da1, da3 = matmul(
        f"{tag}_dact", (T // tm, FF_P // tn, 1),
        [op_a(dh_bf, tm, D_MODEL)], [op_b_rows_t(w2g, pre, D_MODEL, tn)],
        [(0, 0, 0)], 1, [tile_mn(a1, tm, tn), tile_mn(a3, tm, tn)],
        [out_mn(T, FF_P, tm, tn, BF16)] * 2, act_epi, (tm, tn), n_outer=True, after=after)

    tm2, tn2 = 1024, 1024
    (dn,) = matmul(
        f"{tag}_dn", (T // tm2, D_MODEL // tn2, N_CHIPS),
        [op_a(da1, tm2, FF_PAD), op_a(da3, tm2, FF_PAD)],
        [op_b_rows(w1g, pre, FF_PAD, tn2), op_b_rows(w3g, pre, FF_PAD, tn2)],
        [(0, 0, 0), (1, 1, 0)], 1, [], [out_mn(T, D_MODEL, tm2, tn2, F32)], _acc0, (tm2, tn2))
    dh, dg, dh_lo = rms_bwd(f"{tag}_rms_bwd", h, g, dn, D_MODEL, dres=dh_out, bf16_copy=True)
    if before_dw is not None:
        after = before_dw(dh)

    tk = 2048

    def dw_t(nm, left, right, scale):
        (dw,) = matmul(
            f"{tag}_{nm}", (FF_P // FF_PAD, D_MODEL // 1024, T // tk),
            [op_at(left, FF_PAD, tk)], [op_b(right, tk, 1024)],
            [(0, 0, 0)], 1, [], [out_mn(FF_P, D_MODEL, FF_PAD, 1024, BF16)],
            lambda accs, xs: (scale * accs[0],), (FF_PAD, 1024), after=after)
        return dw

    dw2 = dw_t("dw2", s, dh_bf, 0.5)
    dw1 = dw_t("dw1", da1, n, 1.0)
    dw3 = dw_t("dw3", da3, n, 1.0)
    return dh, dh_lo, dg, dw1, dw3, dw2


def rope_tables(positions):
    inv_freq = ROPE_BASE ** (-jnp.arange(0, QK_ROPE, 2, dtype=F32) / QK_ROPE)
    ang = positions.astype(F32)[:, None] * inv_freq
    cos, sin = jnp.cos(ang), jnp.sin(ang)
    T = positions.shape[0]
    one, zero = jnp.ones((T, QK_NOPE), F32), jnp.zeros((T, 64), F32)
    z32, z128 = jnp.zeros((T, 32), F32), jnp.zeros((T, QK_NOPE), F32)
    c = jnp.concatenate([one, cos, cos, zero], axis=1)
    s1 = jnp.concatenate([z128, -sin, z32, zero], axis=1)
    s2 = jnp.concatenate([z128, z32, sin, zero], axis=1)
    return c, s1, s2


def _rope(y, c, s1, s2):
    return y * c + pltpu.roll(y, HEAD_PAD - 32, 1) * s1 + pltpu.roll(y, 32, 1) * s2


def _rope_t(d, c, s1, s2):
    return d * c + pltpu.roll(d * s1, 32, 1) + pltpu.roll(d * s2, HEAD_PAD - 32, 1)


def _head_norm(x):
    r = lax.rsqrt(jnp.sum(x * x, axis=-1, keepdims=True) * (1.0 / QK_DIM) + EPS)
    return x * r, r


def qk_prep_fwd(tag, q_raw, kk_raw, z_p, gq, gk, tabs, tm=256):
    T = q_raw.shape[0]
    c, s1, s2 = tabs

    def body(q_ref, k_ref, kr_ref, gq_ref, gk_ref, c_ref, s1_ref, s2_ref, qo_ref, ko_ref):
        cv, s1v, s2v = c_ref[...], s1_ref[...], s2_ref[...]
        kr = kr_ref[...]
        for h in range(HEADS):
            sl = slice(h * HEAD_PAD, (h + 1) * HEAD_PAD)
            xh, _ = _head_norm(q_ref[:, sl])
            qo_ref[:, sl] = (_rope(xh * gq_ref[...], cv, s1v, s2v) * ATTN_SCALE).astype(BF16)
            xh, _ = _head_norm(k_ref[:, sl] + kr)
            ko_ref[:, sl] = _rope(xh * gk_ref[...], cv, s1v, s2v).astype(BF16)

    row = lambda i: (i, 0)
    full = pl.BlockSpec((tm, HEADS * HEAD_PAD), row)
    tab = pl.BlockSpec((tm, HEAD_PAD), row)
    vec = pl.BlockSpec((1, HEAD_PAD), lambda i: (0, 0))
    return pl.pallas_call(
        body, name=f"{tag}_qk_prep", grid=(T // tm,),
        in_specs=[full, full, pl.BlockSpec((tm, HEAD_PAD), lambda i: (i, 3)), vec, vec, tab, tab, tab],
        out_specs=[full, full],
        out_shape=[jax.ShapeDtypeStruct((T, HEADS * HEAD_PAD), BF16)] * 2,
        compiler_params=_params(("parallel",)),
    )(q_raw, kk_raw, z_p, gq, gk, c, s1, s2)


def qk_prep_bwd(tag, dq_full, dk_full, q_raw, kk_raw, z_p, gq, gk, tabs, tm=256):
    T = q_raw.shape[0]
    c, s1, s2 = tabs

    def body(dq_ref, dk_ref, q_ref, k_ref, kr_ref, gq_ref, gk_ref, c_ref, s1_ref, s2_ref,
             dqr_ref, dkr_ref, dz_ref, dgq_ref, dgk_ref):
        i = pl.program_id(0)
        cv, s1v, s2v = c_ref[...], s1_ref[...], s2_ref[...]
        kr = kr_ref[...]
        lane = lax.broadcasted_iota(jnp.int32, (tm, HEAD_PAD), 1)
        slot = ((lane >= QK_NOPE) & (lane < QK_DIM)).astype(F32)

        def one(x, g, d):
            xh, r = _head_norm(x)
            dy = _rope_t(d, cv, s1v, s2v)
            gd = dy * g
            dx = r * (gd - xh * (jnp.sum(gd * xh, axis=-1, keepdims=True) * (1.0 / QK_DIM)))
            return dx, jnp.sum(dy * xh, axis=0, keepdims=True)

        dgq = jnp.zeros((1, HEAD_PAD), F32)
        dgk = jnp.zeros((1, HEAD_PAD), F32)
        dz = jnp.zeros((tm, HEAD_PAD), F32)
        for h in range(HEADS):
            sl = slice(h * HEAD_PAD, (h + 1) * HEAD_PAD)
            dx, dg = one(q_ref[:, sl], gq_ref[...], dq_ref[:, sl].astype(F32) * ATTN_SCALE)
            dqr_ref[:, sl] = dx
            dgq = dgq + dg
            dx, dg = one(k_ref[:, sl] + kr, gk_ref[...], dk_ref[:, sl].astype(F32))
            dkr_ref[:, sl] = dx
            dgk = dgk + dg
            dz = dz + dx
        dz_ref[...] = dz * slot

        @pl.when(i == 0)
        def _():
            dgq_ref[...] = dgq
            dgk_ref[...] = dgk

        @pl.when(i > 0)
        def _():
            dgq_ref[...] += dgq
            dgk_ref[...] += dgk

    row = lambda i: (i, 0)
    full = pl.BlockSpec((tm, HEADS * HEAD_PAD), row)
    tab = pl.BlockSpec((tm, HEAD_PAD), row)
    vec = pl.BlockSpec((1, HEAD_PAD), lambda i: (0, 0))
    return pl.pallas_call(
        body, name=f"{tag}_qk_prep_bwd", grid=(T // tm,),
        in_specs=[full, full, full, full, pl.BlockSpec((tm, HEAD_PAD), lambda i: (i, 3)), vec, vec, tab, tab, tab],
        out_specs=[full, full, tab, vec, vec],
        out_shape=[jax.ShapeDtypeStruct((T, HEADS * HEAD_PAD), F32)] * 2
        + [jax.ShapeDtypeStruct((T, HEAD_PAD), F32)] + [jax.ShapeDtypeStruct((1, HEAD_PAD), F32)] * 2,
        compiler_params=_params(("arbitrary",)),
    )(dq_full, dk_full, q_raw, kk_raw, z_p, gq, gk, c, s1, s2)


def attn_fwd(tag, q_full, k_full, vv, blk=512):
    T = q_full.shape[0]
    nb = T // blk
    neg = float(jnp.finfo(jnp.float32).min)

    def body(q_ref, k_ref, v_ref, o_ref, lse_ref, m_ref, l_ref, acc_ref):
        i = pl.program_id(1)
        m_ref[...] = jnp.full_like(m_ref, neg)
        l_ref[...] = jnp.zeros_like(l_ref)
        acc_ref[...] = jnp.zeros_like(acc_ref)
        q = q_ref[...]

        def step(j, masked):
            rows = pl.ds(pl.multiple_of(j * blk, blk), blk)
            s = lax.dot_general(q, k_ref[rows, :], (((1,), (1,)), ((), ())), preferred_element_type=F32)
            if masked:
                row = lax.broadcasted_iota(jnp.int32, (blk, blk), 0)
                col = lax.broadcasted_iota(jnp.int32, (blk, blk), 1)
                s = jnp.where(col <= row, s, neg)
            m_prev = m_ref[...]
            m_new = jnp.maximum(m_prev, jnp.max(s, axis=-1, keepdims=True))
            alpha = jnp.exp(m_prev - m_new)
            p = jnp.exp(s - m_new[:, :1])
            l_ref[...] = alpha * l_ref[...] + jnp.sum(p, axis=-1, keepdims=True)
            acc_ref[...] = alpha * acc_ref[...] + jnp.dot(p.astype(BF16), v_ref[rows, :], preferred_element_type=F32)
            m_ref[...] = m_new

        def off_diagonal(j, carry):
            step(j, False)
            return carry

        lax.fori_loop(0, i, off_diagonal, 0)
        step(i, True)
        o_ref[...] = acc_ref[...] / l_ref[...]
        lse_ref[...] = m_ref[...] + jnp.log(l_ref[...])

    return pl.pallas_call(
        body, name=f"{tag}_attn_fwd", grid=(HEADS, nb),
        in_specs=[pl.BlockSpec((blk, HEAD_PAD), lambda h, i: (i, h)),
                  pl.BlockSpec((T, HEAD_PAD), lambda h, i: (0, h)), pl.BlockSpec((T, V_DIM), lambda h, i: (0, h))],
        out_specs=[pl.BlockSpec((blk, V_DIM), lambda h, i: (i, h))] * 2,
        out_shape=[jax.ShapeDtypeStruct((T, ATTN_W), F32)] * 2,
        scratch_shapes=[pltpu.VMEM((blk, V_DIM), F32)] * 3,
        compiler_params=_params(("parallel", "parallel")),
    )(q_full, k_full, vv)


def attn_bwd(tag, q_full, k_full, vv, do, lse, delta, blk=512):
    T = q_full.shape[0]
    nb = T // blk
    neg = float(jnp.finfo(jnp.float32).min)

    def body(q_ref, k_ref, v_ref, do_ref, lse_ref, dl_ref, dq_ref, dk_ref, dv_ref, dk_acc, dv_acc):
        j = pl.program_id(1)

        @pl.when(j == 0)
        def _():
            dq_ref[...] = jnp.zeros_like(dq_ref)

        dk_acc[...] = jnp.zeros_like(dk_acc)
        dv_acc[...] = jnp.zeros_like(dv_acc)
        k, v = k_ref[...], v_ref[...]

        def step(i, masked):
            rows = pl.ds(pl.multiple_of(i * blk, blk), blk)
            q = q_ref[rows, :]
            s = lax.dot_general(q, k, (((1,), (1,)), ((), ())), preferred_element_type=F32)
            if masked:
                row = lax.broadcasted_iota(jnp.int32, (blk, blk), 0)
                col = lax.broadcasted_iota(jnp.int32, (blk, blk), 1)
                s = jnp.where(col <= row, s, neg)
            p = jnp.exp(s - lse_ref[rows, :1])
            dob = _bf(do_ref[rows, :])
            dv_acc[...] += lax.dot_general(p.astype(BF16), dob, (((0,), (0,)), ((), ())), preferred_element_type=F32)
            dp = lax.dot_general(dob, v, (((1,), (1,)), ((), ())), preferred_element_type=F32)
            ds = (p * (dp - dl_ref[rows, :1])).astype(BF16)
            dk_acc[...] += lax.dot_general(ds, q, (((0,), (0,)), ((), ())), preferred_element_type=F32)
            dq_ref[rows, :] += jnp.dot(ds, k, preferred_element_type=F32)

        def off_diagonal(i, carry):
            step(i, False)
            return carry

        step(j, True)
        lax.fori_loop(j + 1, nb, off_diagonal, 0)
        dk_ref[...] = dk_acc[...]
        dv_ref[...] = dv_acc[...]

    head = lambda h, j: (0, h)
    kv_ix = lambda h, j: (j, h)
    return pl.pallas_call(
        body, name=f"{tag}_attn_bwd", grid=(HEADS, nb),
        in_specs=[pl.BlockSpec((T, HEAD_PAD), head), pl.BlockSpec((blk, HEAD_PAD), kv_ix),
                  pl.BlockSpec((blk, V_DIM), kv_ix), pl.BlockSpec((T, V_DIM), head),
                  pl.BlockSpec((T, V_DIM), head), pl.BlockSpec((T, V_DIM), head)],
        out_specs=[pl.BlockSpec((T, HEAD_PAD), head),
                   pl.BlockSpec((blk, HEAD_PAD), kv_ix), pl.BlockSpec((blk, V_DIM), kv_ix)],
        out_shape=[jax.ShapeDtypeStruct((T, HEADS * HEAD_PAD), F32)] * 2 + [jax.ShapeDtypeStruct((T, ATTN_W), F32)],
        scratch_shapes=[pltpu.VMEM((blk, HEAD_PAD), F32), pltpu.VMEM((blk, V_DIM), F32)],
        compiler_params=_params(("parallel", "arbitrary")),
    )(q_full, k_full, vv, do, lse, delta)


def _gm_forward(u, v, gv, wc_ref, bb_ref, nchunk):
    ug = _gelu(u)
    vg = _gelu(v)
    rv = lax.rsqrt(jnp.mean(vg * vg, axis=-1, keepdims=True) + EPS)
    vhat = vg * rv
    vn = (vhat * gv).astype(BF16)
    gates = []
    for cidx in range(nchunk):
        rows = slice(cidx * CHUNK, (cidx + 1) * CHUNK)
        gates.append(jnp.concatenate(
            [jnp.dot(wc_ref[gidx], vn[rows, gidx * 128:(gidx + 1) * 128], preferred_element_type=F32) + bb_ref[gidx]
             for gidx in range(GROUPS)], axis=1))
    gate = jnp.concatenate(gates, axis=0)
    return ug, vhat, rv, vn, gate


def gmlp_fwd(tag, z_p, gv, gout, wc, bb, tm=256):
    T = z_p.shape[0]
    nchunk = tm // CHUNK

    def body(u_ref, v_ref, gv_ref, go_ref, wc_ref, bb_ref, o_ref):
        ug, _, _, _, gate = _gm_forward(u_ref[...], v_ref[...], gv_ref[...], wc_ref, bb_ref, nchunk)
        go = ug * gate
        ro = lax.rsqrt(jnp.mean(go * go, axis=-1, keepdims=True) + EPS)
        o_ref[...] = (go * ro * go_ref[...]).astype(BF16)

    vec = pl.BlockSpec((1, GM_W), lambda i: (0, 0))
    w3 = pl.BlockSpec((GROUPS, CHUNK, CHUNK), lambda i: (0, 0, 0))
    return pl.pallas_call(
        body, name=f"{tag}_gmlp_fwd", grid=(T // tm,),
        in_specs=[pl.BlockSpec((tm, GM_W), lambda i: (i, 1)), pl.BlockSpec((tm, GM_W), lambda i: (i, 2)), vec, vec, w3, w3],
        out_specs=pl.BlockSpec((tm, GM_W), lambda i: (i, 0)),
        out_shape=jax.ShapeDtypeStruct((T, GM_W), BF16),
        compiler_params=_params(("parallel",)),
    )(z_p, z_p, gv.reshape(1, GM_W), gout.reshape(1, GM_W), wc, bb)


def gmlp_bwd(tag, z_p, dmixed, gv, gout, wc, bb, tm=256):
    T = z_p.shape[0]
    nchunk = tm // CHUNK

    def body(u_ref, v_ref, dm_ref, gv_ref, go_ref, wc_ref, bb_ref, du_ref, dv_ref, dwc_ref, dbb_ref, dgv_ref, dgo_ref):
        i = pl.program_id(0)
        u, v = u_ref[...], v_ref[...]
        ug, vhat, rv, vn, gate = _gm_forward(u, v, gv_ref[...], wc_ref, bb_ref, nchunk)
        go = ug * gate
        ro = lax.rsqrt(jnp.mean(go * go, axis=-1, keepdims=True) + EPS)
        ohat = go * ro
        dm = dm_ref[...].astype(F32)
        dgo_part = jnp.sum(dm * ohat, axis=0, keepdims=True)
        doh = dm * go_ref[...]
        dgo = ro * (doh - ohat * jnp.mean(doh * ohat, axis=-1, keepdims=True))
        du_ref[...] = dgo * gate * _gelu_grad(u)
        dgate = dgo * ug
        dgb = dgate.astype(BF16)
        dvn_rows = []
        dwc_parts = []
        dbb_parts = []
        for gidx in range(GROUPS):
            cols = slice(gidx * 128, (gidx + 1) * 128)
            dw = jnp.zeros((CHUNK, CHUNK), F32)
            db = jnp.zeros((CHUNK, 128), F32)
            for cidx in range(nchunk):
                rows = slice(cidx * CHUNK, (cidx + 1) * CHUNK)
                dw = dw + lax.dot_general(dgb[rows, cols], vn[rows, cols], (((1,), (1,)), ((), ())),
                                          preferred_element_type=F32)
                db = db + dgate[rows, cols]
            dwc_parts.append(dw)
            dbb_parts.append(db)
        for cidx in range(nchunk):
            rows = slice(cidx * CHUNK, (cidx + 1) * CHUNK)
            dvn_rows.append(jnp.concatenate(
                [lax.dot_general(wc_ref[gidx], dgb[rows, gidx * 128:(gidx + 1) * 128], (((0,), (0,)), ((), ())),
                                 preferred_element_type=F32) for gidx in range(GROUPS)], axis=1))
        dvn = jnp.concatenate(dvn_rows, axis=0)
        dgv_part = jnp.sum(dvn * vhat, axis=0, keepdims=True)
        dvh = dvn * gv_ref[...]
        dvg = rv * (dvh - vhat * jnp.mean(dvh * vhat, axis=-1, keepdims=True))
        dv_ref[...] = dvg * _gelu_grad(v)

        @pl.when(i == 0)
        def _():
            for gidx in range(GROUPS):
                dwc_ref[gidx] = dwc_parts[gidx]
                dbb_ref[gidx] = dbb_parts[gidx]
            dgv_ref[...] = dgv_part
            dgo_ref[...] = dgo_part

        @pl.when(i > 0)
        def _():
            for gidx in range(GROUPS):
                dwc_ref[gidx] += dwc_parts[gidx]
                dbb_ref[gidx] += dbb_parts[gidx]
            dgv_ref[...] += dgv_part
            dgo_ref[...] += dgo_part

    vec = pl.BlockSpec((1, GM_W), lambda i: (0, 0))
    w3 = pl.BlockSpec((GROUPS, CHUNK, CHUNK), lambda i: (0, 0, 0))
    blk = pl.BlockSpec((tm, GM_W), lambda i: (i, 0))
    return pl.pallas_call(
        body, name=f"{tag}_gmlp_bwd", grid=(T // tm,),
        in_specs=[pl.BlockSpec((tm, GM_W), lambda i: (i, 1)), pl.BlockSpec((tm, GM_W), lambda i: (i, 2)),
                  pl.BlockSpec((tm, GM_W), lambda i: (i, 1)), vec, vec, w3, w3],
        out_specs=[blk, blk, w3, w3, vec, vec],
        out_shape=[jax.ShapeDtypeStruct((T, GM_W), F32)] * 2 + [jax.ShapeDtypeStruct((GROUPS, CHUNK, CHUNK), F32)] * 2
        + [jax.ShapeDtypeStruct((1, GM_W), F32)] * 2,
        compiler_params=_params(("arbitrary",)),
    )(z_p, z_p, dmixed, gv.reshape(1, GM_W), gout.reshape(1, GM_W), wc, bb)


def mixer_fwd(tag, h, w, tabs, wout_g, pre):
    T = h.shape[0]
    n2 = rms_fwd(f"{tag}_mix_rms", h, w["mix_norm"], D_MODEL)
    (z_p,) = mm_simple(f"{tag}_win", n2, lambda tk, tn: op_bt(w["w_in_pt"], tk, tn), T, IN_P, D_MODEL, 512, 1024, D_MODEL)
    cqn = rms_fwd(f"{tag}_cq_rms", z_p, w["q_a_norm"], Q_RANK, col_blk=0)
    ckvn = rms_fwd(f"{tag}_ckv_rms", z_p, w["kv_a_norm"], KV_RANK, col_blk=2)
    (q_raw,) = mm_simple(f"{tag}_wq", cqn, lambda tk, tn: op_b(w["wq_p"], tk, tn), T, 2048, Q_RANK, 512, 1024, Q_RANK)
    (kk_raw,) = mm_simple(f"{tag}_wk", ckvn, lambda tk, tn: op_b(w["wk_p"], tk, tn), T, 2048, KV_RANK, 512, 1024, KV_RANK)
    (vv,) = mm_simple(f"{tag}_wv", ckvn, lambda tk, tn: op_b(w["wv"], tk, tn), T, ATTN_W, KV_RANK, 512, 1024, KV_RANK,
                      out_dtype=BF16)
    q_full, k_full = qk_prep_fwd(tag, q_raw, kk_raw, z_p, w["gq_p"], w["gk_p"], tabs)
    a_out, lse = attn_fwd(tag, q_full, k_full, vv)
    mixed_a = rms_fwd(f"{tag}_ao_rms", a_out, w["attn_out_norm"], ATTN_W)
    mixed_g = gmlp_fwd(tag, z_p, w["gm_v_norm"], w["gm_out_norm"], w["wc"], w["bb"])
    tm, tn, tk = 512, 1024, 512
    (h2,) = matmul(
        f"{tag}_wout", (T // tm, D_MODEL // tn, ATTN_W // tk),
        [op_a(mixed_a, tm, tk), op_a(mixed_g, tm, tk)],
        [op_b_rows(wout_g, pre, tk, tn), op_b_rows(wout_g, pre, tk, tn, koff=ATTN_W // tk)],
        [(0, 0, 0), (1, 1, 0)], 1, [tile_mn(h, tm, tn)], [out_mn(T, D_MODEL, tm, tn, F32)],
        lambda accs, xs: (xs[0] + accs[0],), (tm, tn))
    res = dict(n2=n2, z_p=z_p, cqn=cqn, ckvn=ckvn, q_raw=q_raw, kk_raw=kk_raw, vv=vv, q_full=q_full, k_full=k_full,
               a_out=a_out, lse=lse, mixed_a=mixed_a, mixed_g=mixed_g)
    return h2, res


def mixer_bwd(tag, dh2, dh2_bf, h, w, tabs, wout_g, pre, r, after=None):
    T = h.shape[0]
    g = {}
    (dmixed,) = mm_simple(f"{tag}_dmixed", dh2_bf, lambda tk, tn: op_b_rows_t(wout_g, pre, tk, tn), T, D_MODEL, D_MODEL,
                          512, 512, D_MODEL, after=after)
    (dwo_a,) = mm_simple(f"{tag}_dwout_a", r["mixed_a"], lambda tk, tn: op_b(dh2_bf, tk, tn), ATTN_W, D_MODEL, T,
                         1024, 1024, 512, a_t=True, out_dtype=BF16)
    (dwo_g,) = mm_simple(f"{tag}_dwout_g", r["mixed_g"], lambda tk, tn: op_b(dh2_bf, tk, tn), GM_W, D_MODEL, T,
                         1024, 1024, 512, a_t=True, out_dtype=BF16)
    g["w_out"] = jnp.concatenate([dwo_a, dwo_g], axis=0)
    da_out, g["attn_out_norm"], delta = rms_bwd(f"{tag}_ao_rms_bwd", r["a_out"], w["attn_out_norm"], dmixed, ATTN_W,
                                                with_delta=True)
    dq_full, dk_full, dvv = attn_bwd(tag, r["q_full"], r["k_full"], r["vv"], da_out, r["lse"], delta)
    dq_raw, dkk_raw, dzkr, g["gq_p"], g["gk_p"] = qk_prep_bwd(tag, dq_full, dk_full, r["q_raw"], r["kk_raw"], r["z_p"],
                                                            w["gq_p"], w["gk_p"], tabs)
    (g["wq_p"],) = mm_simple(f"{tag}_dwq", r["cqn"], lambda tk, tn: op_b(dq_raw, tk, tn), Q_RANK, 2048, T, Q_RANK, 1024, 512,
                             a_t=True, out_dtype=BF16)
    (g["wk_p"],) = mm_simple(f"{tag}_dwk", r["ckvn"], lambda tk, tn: op_b(dkk_raw, tk, tn), KV_RANK, 2048, T, KV_RANK, 1024,
                             512, a_t=True, out_dtype=BF16)
    (g["wv"],) = mm_simple(f"{tag}_dwv", r["ckvn"], lambda tk, tn: op_b(dvv, tk, tn), KV_RANK, ATTN_W, T, KV_RANK, 1024, 512,
                           a_t=True, out_dtype=BF16)
    (dcqn,) = mm_simple(f"{tag}_dcqn", dq_raw, lambda tk, tn: op_bt(w["wq_p"], tk, tn), T, Q_RANK, 2048, 512, Q_RANK, 2048)
    (dck1,) = mm_simple(f"{tag}_dckvn_k", dkk_raw, lambda tk, tn: op_bt(w["wk_p"], tk, tn), T, KV_RANK, 2048, 512, KV_RANK,
                        2048)
    (dckvn,) = mm_simple(f"{tag}_dckvn_v", dvv, lambda tk, tn: op_bt(w["wv"], tk, tn), T, KV_RANK, ATTN_W, 512, KV_RANK,
                         ATTN_W, extras=[tile_mn(dck1, 512, KV_RANK)], epilogue=lambda accs, xs: (accs[0] + xs[0],))
    dc_q, g["q_a_norm"] = rms_bwd(f"{tag}_cq_rms_bwd", r["z_p"], w["q_a_norm"], dcqn, Q_RANK, col_blk=0)
    dc_kv, g["kv_a_norm"] = rms_bwd(f"{tag}_ckv_rms_bwd", r["z_p"], w["kv_a_norm"], dckvn, KV_RANK, col_blk=2)
    du, dv, g["wc"], g["bb"], g["gm_v_norm"], g["gm_out_norm"] = gmlp_bwd(
        tag, r["z_p"], dmixed, w["gm_v_norm"], w["gm_out_norm"], w["wc"], w["bb"])
    dz_p = jnp.concatenate([dc_q, dc_kv, dzkr, du, dv], axis=1).astype(BF16)
    (g["w_in_pt"],) = mm_simple(f"{tag}_dwin", dz_p, lambda tk, tn: op_b(r["n2"], tk, tn), IN_P, D_MODEL, T, 1024, 1024, 512,
                                a_t=True, out_dtype=BF16)
    (dn2,) = mm_simple(f"{tag}_dn2", dz_p, lambda tk, tn: op_b(w["w_in_pt"], tk, tn), T, D_MODEL, IN_P, 512, 1024, IN_P)
    dh1, g["mix_norm"], dh1_bf = rms_bwd(f"{tag}_mix_rms_bwd", h, w["mix_norm"], dn2, D_MODEL, dres=dh2, bf16_copy=True)
    return dh1, dh1_bf, g


def ple_fwd(tag, h3, p_l, w, wpg_g, wple_g, pre):
    T = h3.shape[0]
    (pw,) = mm_simple(f"{tag}_wple", p_l, lambda tk, tn: op_b_cols(wple_g, pre, tk, tn), T, D_MODEL, PLE_DIM, 512, 512,
                      PLE_DIM)
    e = rms_fwd(f"{tag}_ple_rms", pw, w["ple_norm"], D_MODEL, out_dtype=F32)
    n4 = rms_fwd(f"{tag}_pg_rms", h3, w["ple_gate_norm"], D_MODEL)

    def epi(accs, xs):
        gt = _sigmoid(accs[0])
        return xs[0] + gt * xs[1], gt

    tm, tn, tk = 512, 1024, 512
    h4, gate = matmul(
        f"{tag}_wpg", (T // tm, D_MODEL // tn, D_MODEL // tk),
        [op_a(n4, tm, tk)], [op_b_rows(wpg_g, pre, tk, tn)], [(0, 0, 0)], 1,
        [tile_mn(h3, tm, tn), tile_mn(e, tm, tn)],
        [out_mn(T, D_MODEL, tm, tn, F32), out_mn(T, D_MODEL, tm, tn, BF16)], epi, (tm, tn))
    return h4, dict(pw=pw, e=e, n4=n4, gate=gate)


def ple_bwd(tag, dh4, h3, p_l, w, wpg_g, wple_g, pre, r, tm=256):
    T = h3.shape[0]

    def act_body(d_ref, g_ref, e_ref, dpre_ref, de_ref):
        d, gt = d_ref[...], g_ref[...].astype(F32)
        dpre_ref[...] = (d * e_ref[...] * gt * (1.0 - gt)).astype(BF16)
        de_ref[...] = d * gt

    blk = pl.BlockSpec((tm, D_MODEL), lambda i: (i, 0))
    dpre, de = pl.pallas_call(
        act_body, name=f"{tag}_ple_act_bwd", grid=(T // tm,), in_specs=[blk, blk, blk], out_specs=[blk, blk],
        out_shape=[jax.ShapeDtypeStruct((T, D_MODEL), BF16), jax.ShapeDtypeStruct((T, D_MODEL), F32)],
        compiler_params=_params(("parallel",)),
    )(dh4, r["gate"], r["e"])
    g = {}
    (g["w_ple_gate"],) = mm_simple(f"{tag}_dwpg", r["n4"], lambda tk, tn: op_b(dpre, tk, tn), D_MODEL, D_MODEL, T,
                                   1024, 1024, 512, a_t=True, out_dtype=BF16)
    (dn4,) = mm_simple(f"{tag}_dn4", dpre, lambda tk, tn: op_b_rows_t(wpg_g, pre, tk, tn), T, D_MODEL, D_MODEL, 512, 512,
                       D_MODEL)
    dh3, g["ple_gate_norm"], dh3_bf = rms_bwd(f"{tag}_pg_rms_bwd", h3, w["ple_gate_norm"], dn4, D_MODEL, dres=dh4,
                                              bf16_copy=True)
    dpw, g["ple_norm"] = rms_bwd(f"{tag}_ple_rms_bwd", r["pw"], w["ple_norm"], de, D_MODEL)
    (g["w_ple"],) = mm_simple(f"{tag}_dwple", p_l, lambda tk, tn: op_b(dpw, tk, tn), PLE_DIM, D_MODEL, T, PLE_DIM, 512, 512,
                              a_t=True, outs=[out_cols(PLE_DIM, 512, PLE_DIM, 512, BF16)])
    return dh3, dh3_bf, g


def loss_grad(y, target, tm=256):
    T = y.shape[0]

    def body(y_ref, t_ref, dy_ref, l_ref):
        i = pl.program_id(0)
        d = y_ref[...] - t_ref[...]
        dy_ref[...] = d * (1.0 / D_MODEL)
        part = jnp.sum((d * d).reshape(tm // 8, 8, D_MODEL), axis=0)

        @pl.when(i == 0)
        def _():
            l_ref[...] = part

        @pl.when(i > 0)
        def _():
            l_ref[...] += part

    blk = pl.BlockSpec((tm, D_MODEL), lambda i: (i, 0))
    dy, part = pl.pallas_call(
        body, name="loss_grad", grid=(T // tm,), in_specs=[blk, blk],
        out_specs=[blk, pl.BlockSpec((8, D_MODEL), lambda i: (0, 0))],
        out_shape=[jax.ShapeDtypeStruct((T, D_MODEL), F32), jax.ShapeDtypeStruct((8, D_MODEL), F32)],
        compiler_params=_params(("arbitrary",)),
    )(y, target)
    return dy, 0.5 * jnp.sum(part) / D_MODEL


def _unshard_cols(g_l):
    return g_l.transpose(1, 0, 2).reshape(g_l.shape[1], -1)


def _shard_cols(w):
    return w.reshape(w.shape[0], N_CHIPS, -1).transpose(1, 0, 2)


def layer_weights(l, Gl, small):
    w = {k: small[k][l] for k in ("mix_norm", "q_a_norm", "kv_a_norm", "gm_v_norm", "attn_out_norm", "gm_out_norm",
                                  "ple_gate_norm", "ple_norm")}
    wint = Gl["w_in"][:, :IN_SHARD].reshape(-1, D_MODEL)
    z = lambda n: jnp.zeros((n, D_MODEL), BF16)
    w["w_in_pt"] = jnp.concatenate([wint[:768], z(128), wint[768:832], z(64), wint[832:]], axis=0)
    wuq = _unshard_cols(Gl["w_uq"]).reshape(Q_RANK, HEADS, QK_DIM)
    w["wq_p"] = jnp.pad(wuq, ((0, 0), (0, 0), (0, HEAD_PAD - QK_DIM))).reshape(Q_RANK, HEADS * HEAD_PAD)
    wukv = _unshard_cols(Gl["w_ukv"]).reshape(KV_RANK, HEADS, QK_NOPE + V_DIM)
    w["wk_p"] = jnp.pad(wukv[:, :, :QK_NOPE], ((0, 0), (0, 0), (0, HEAD_PAD - QK_NOPE))).reshape(KV_RANK, HEADS * HEAD_PAD)
    w["wv"] = wukv[:, :, QK_NOPE:].reshape(KV_RANK, ATTN_W)
    w["gq_p"] = jnp.pad(small["q_norm"][l], (0, HEAD_PAD - QK_DIM)).reshape(1, HEAD_PAD)
    w["gk_p"] = jnp.pad(small["k_norm"][l], (0, HEAD_PAD - QK_DIM)).reshape(1, HEAD_PAD)
    tril = jnp.tril(jnp.ones((CHUNK, CHUNK), dtype=bool))
    w["wc"] = jnp.where(tril[None], small["gm_ws"][l], 0.0).astype(BF16)
    w["bb"] = jnp.broadcast_to(small["gm_bs"][l][:, :, None], (GROUPS, CHUNK, 128)).astype(F32)
    return w


def mixer_grads_to_shards(g):
    out = {}
    dwint = g["w_in_pt"]
    dwint = jnp.concatenate([dwint[:768], dwint[896:960], dwint[1024:]], axis=0).reshape(N_CHIPS, IN_SHARD, D_MODEL)
    out["w_in"] = jnp.pad(dwint, ((0, 0), (0, IN_SHARD_PAD - IN_SHARD), (0, 0)))
    dwuq = g["wq_p"].reshape(Q_RANK, HEADS, HEAD_PAD)[:, :, :QK_DIM].reshape(Q_RANK, HEADS * QK_DIM)
    out["w_uq"] = _shard_cols(dwuq)
    dwukv = jnp.concatenate([g["wk_p"].reshape(KV_RANK, HEADS, HEAD_PAD)[:, :, :QK_NOPE],
                             g["wv"].reshape(KV_RANK, HEADS, V_DIM)], axis=-1).reshape(KV_RANK, HEADS * (QK_NOPE + V_DIM))
    out["w_ukv"] = _shard_cols(dwukv)
    out["w_out"] = g["w_out"].reshape(N_CHIPS, D_MODEL // N_CHIPS, D_MODEL)
    out["q_norm"] = g["gq_p"][0, :QK_DIM]
    out["k_norm"] = g["gk_p"][0, :QK_DIM]
    tril = jnp.tril(jnp.ones((CHUNK, CHUNK), dtype=bool))
    out["gm_ws"] = jnp.where(tril[None], g["wc"], 0.0)
    out["gm_bs"] = jnp.sum(g["bb"], axis=-1)
    for k in ("mix_norm", "q_a_norm", "kv_a_norm", "gm_v_norm", "attn_out_norm", "gm_out_norm"):
        out[k] = g[k][0]
    return out


def layer_fwd(l, h, p_l, Gl, small, tabs, before=None):
    before = before or {}
    h1, r_a = ffn_fwd(f"l{l}a", h, small["ffn_a_norm"][l], Gl["ffn_a_w1"], Gl["ffn_a_w3"], Gl.get("ffn_a_w2"), (),
                      before.get("down_a"))
    if "mixer" in before:
        Gl, small = before["mixer"](h1, Gl, small)
    w = layer_weights(l, Gl, small)
    h2, r_m = mixer_fwd(f"l{l}", h1, w, tabs, Gl["w_out"], ())
    if "ffn_b" in before:
        Gl = before["ffn_b"](h2, Gl)
    h3, r_b = ffn_fwd(f"l{l}b", h2, small["ffn_b_norm"][l], Gl["ffn_b_w1"], Gl["ffn_b_w3"], Gl["ffn_b_w2"], ())
    if "ple" in before:
        w = {**w, "ple_norm": w["ple_norm"] + before["ple"](h3)[0, 0]}
    h4, r_p = ple_fwd(f"l{l}", h3, p_l, w, Gl["w_ple_gate"], Gl["w_ple"], ())
    return h4, (w, h, h1, h2, h3, r_a, r_m, r_b, r_p)


def layer_bwd(l, dh, p_l, Gl, small, tabs, saved, before=None):
    w, h0, h1, h2, h3, r_a, r_m, r_b, r_p = saved
    slabs = lambda d: d.reshape(N_CHIPS, FF_PAD, D_MODEL)
    hook = lambda block: before[block](gl, dh) if before and block in before else None
    gl = {}
    dh, dh_bf, g_p = ple_bwd(f"l{l}", dh, h3, p_l, w, Gl["w_ple_gate"], Gl["w_ple"], (), r_p)
    gl["w_ple_gate"] = g_p["w_ple_gate"].reshape(N_CHIPS, D_MODEL // N_CHIPS, D_MODEL)
    gl["w_ple"] = g_p["w_ple"]
    gl["ple_gate_norm"], gl["ple_norm"] = g_p["ple_gate_norm"][0], g_p["ple_norm"][0]
    dh, dh_bf, dg, dw1, dw3, dw2 = ffn_bwd(f"l{l}b", dh, dh_bf, h2, small["ffn_b_norm"][l], r_b,
                                           Gl["ffn_b_w1"], Gl["ffn_b_w3"], Gl["ffn_b_w2"], (), hook("ffn_b"))
    gl["ffn_b_norm"] = dg[0]
    gl["ffn_b_w1"], gl["ffn_b_w3"], gl["ffn_b_w2"] = slabs(dw1), slabs(dw3), slabs(dw2)
    dh, dh_bf, g_m = mixer_bwd(f"l{l}", dh, dh_bf, h1, w, tabs, Gl["w_out"], (), r_m, hook("mixer"))
    gl.update(mixer_grads_to_shards(g_m))
    last_dw = (lambda dh_: before["ffn_a_dw"](gl, dh_)) if before and "ffn_a_dw" in before else None
    dh, _, dg, dw1, dw3, dw2 = ffn_bwd(f"l{l}a", dh, dh_bf, h0, small["ffn_a_norm"][l], r_a,
                                       Gl["ffn_a_w1"], Gl["ffn_a_w3"], Gl["ffn_a_w2"], (), hook("ffn_a"), last_dw)
    gl["ffn_a_norm"] = dg[0]
    gl["ffn_a_w1"], gl["ffn_a_w3"], gl["ffn_a_w2"] = slabs(dw1), slabs(dw3), slabs(dw2)
    return dh, gl


MESH = pl.DeviceIdType.MESH
HBM_SPEC = pl.BlockSpec(memory_space=pltpu.HBM)


def _place():
    x, y, c = lax.axis_index("x"), lax.axis_index("y"), lax.axis_index("c")
    others = [(1 - x, y), (x, 1 - y), (1 - x, 1 - y)]
    return x, y, c, 2 * x + y, others


def prep_shard(name, w, layer, rows_pad, place, after=None):
    _, ks, n = w.shape
    ksp = ks + rows_pad
    tc = 512 if n % 512 == 0 else n
    deps = [] if after is None else [after]

    def body(place_ref, x_ref, *rest):
        o_ref = rest[-1]
        o_ref[:ks] = x_ref[...].astype(BF16)
        if rows_pad:
            o_ref[ks:] = jnp.zeros((rows_pad, tc), BF16)

    return pl.pallas_call(
        body, name=name,
        grid_spec=pltpu.PrefetchScalarGridSpec(
            num_scalar_prefetch=1, grid=(n // tc,),
            in_specs=[pl.BlockSpec((None, ks, tc), lambda i, s: (layer, 0, i))] + [ANY_SPEC] * len(deps),
            out_specs=pl.BlockSpec((None, ksp, tc), lambda i, s: (s[0], 0, i))),
        out_shape=jax.ShapeDtypeStruct((N_CHIPS, ksp, n), BF16),
        compiler_params=_params(("parallel",)),
    )(place, w, *deps)


SEM_SPEC = pl.BlockSpec(memory_space=pltpu.SEMAPHORE)
ANY_SPEC = pl.BlockSpec(memory_space=pl.ANY)
DATAFLOW = pltpu.SideEffectType.DATAFLOW_SIDE_EFFECTING


def _hbm(x):
    return pltpu.with_memory_space_constraint(x, pltpu.HBM)


def _start_call(name, slots, after, issue):
    n = len(slots)
    deps = [] if after is None else [after]
    nd = len(deps)

    def body(*refs):
        issue(refs[n + nd + 2:2 * n + nd + 2], refs[n + nd], refs[n + nd + 1])
        token = refs[2 * n + nd + 2]
        token[...] = jnp.zeros_like(token)

    outs = pl.pallas_call(
        body, name=name,
        in_specs=[HBM_SPEC] * n + [ANY_SPEC] * nd,
        out_specs=(SEM_SPEC, SEM_SPEC, *([HBM_SPEC] * n), pl.BlockSpec(memory_space=pltpu.VMEM)),
        out_shape=(pltpu.SemaphoreType.DMA((n,)), pltpu.SemaphoreType.DMA((n,)),
                   *[pltpu.HBM(s.shape, s.dtype) for s in slots], jax.ShapeDtypeStruct((8, 128), F32)),
        input_output_aliases={w: w + 2 for w in range(n)},
        compiler_params=pltpu.CompilerParams(has_side_effects=DATAFLOW),
    )(*[_hbm(s) for s in slots], *deps)
    return outs[0], outs[1], list(outs[2:2 + n]), outs[2 + n]


def gather_start(name, slots, after):
    def issue(g_refs, send, recv):
        x, y, c, jme, others = _place()
        for w in range(len(slots)):
            kh = slots[w].shape[1] // 2
            mine = g_refs[w].at[jme, pl.ds(c * kh, kh)]
            for (px, py) in others:
                pltpu.make_async_remote_copy(src_ref=mine, dst_ref=mine, send_sem=send.at[w], recv_sem=recv.at[w],
                                             device_id=(px, py, c), device_id_type=MESH).start()

    return _start_call(name, slots, after, issue)


def forward_start(name, slots):
    def issue(g_refs, send, recv):
        x, y, c, _, others = _place()
        for w in range(len(slots)):
            kh = slots[w].shape[1] // 2
            for (px, py) in others:
                blk = g_refs[w].at[2 * px + py, pl.ds(c * kh, kh)]
                pltpu.make_async_remote_copy(src_ref=blk, dst_ref=blk, send_sem=send.at[w], recv_sem=recv.at[w],
                                             device_id=(x, y, 1 - c), device_id_type=MESH).start()

    return _start_call(name, slots, None, issue)


def share_start(name, fulls):
    def issue(o_refs, send, recv):
        x, y, c, _, _ = _place()
        for w in range(len(fulls)):
            kh = fulls[w].shape[0] // 2
            half = o_refs[w].at[pl.ds(c * kh, kh)]
            pltpu.make_async_remote_copy(src_ref=half, dst_ref=half, send_sem=send.at[w], recv_sem=recv.at[w],
                                         device_id=(x, y, 1 - c), device_id_type=MESH).start()

    return _start_call(name, fulls, None, issue)


def share_wait(name, send, recv, flying, after):
    return _wait_call(name, send, recv, flying, after, lambda r: r.at[pl.ds(0, r.shape[0] // 2)])


def gather_wait(name, send, recv, flying, after):
    return _wait_call(name, send, recv, flying, after, lambda r: r.at[pl.ds(0, 3), pl.ds(0, r.shape[1] // 2)])


def _wait_call(name, send, recv, flying, after, landed):
    n = len(flying)

    def body(*refs):
        send_ref, recv_ref = refs[n], refs[n + 1]
        g_refs = refs[n + 3:]
        x, y, c, _, _ = _place()
        for w in range(n):
            cp = pltpu.make_async_remote_copy(src_ref=landed(g_refs[w]), dst_ref=landed(g_refs[w]),
                                              send_sem=send_ref.at[w], recv_sem=recv_ref.at[w],
                                              device_id=(x, y, 1 - c), device_id_type=MESH)
            cp.wait_send()
            cp.wait_recv()

    return pl.pallas_call(
        body, name=name,
        in_specs=[HBM_SPEC] * n + [SEM_SPEC, SEM_SPEC, ANY_SPEC],
        out_specs=[HBM_SPEC] * n,
        out_shape=[pltpu.HBM(s.shape, s.dtype) for s in flying],
        input_output_aliases={w: w for w in range(n)},
        compiler_params=pltpu.CompilerParams(has_side_effects=DATAFLOW),
    )(*flying, send, recv, after)


def _send_start(name, srcs, land_shapes, issue, after):
    n = len(srcs)
    deps = [] if after is None else [after]
    nd = len(deps)

    def body(*refs):
        base = 2 * n + nd
        issue(refs[base + 2:base + 2 + n], refs[base + 2 + n:base + 2 + 2 * n], refs[base], refs[base + 1])
        token = refs[base + 2 + 2 * n]
        token[...] = jnp.zeros_like(token)

    lands = [_hbm(lax.empty(shape, s.dtype)) for shape, s in zip(land_shapes, srcs)]
    outs = pl.pallas_call(
        body, name=name,
        in_specs=[HBM_SPEC] * (2 * n) + [ANY_SPEC] * nd,
        out_specs=(SEM_SPEC, SEM_SPEC, *([HBM_SPEC] * (2 * n)), pl.BlockSpec(memory_space=pltpu.VMEM)),
        out_shape=(pltpu.SemaphoreType.DMA((n,)), pltpu.SemaphoreType.DMA((n,)),
                   *[pltpu.HBM(s.shape, s.dtype) for s in srcs], *[pltpu.HBM(l.shape, l.dtype) for l in lands],
                   jax.ShapeDtypeStruct((8, 128), F32)),
        input_output_aliases={w: w + 2 for w in range(2 * n)},
        compiler_params=pltpu.CompilerParams(has_side_effects=DATAFLOW),
    )(*[_hbm(s) for s in srcs], *lands, *deps)
    return outs[0], outs[1], list(outs[2:2 + n]), list(outs[2 + n:2 + 2 * n]), outs[2 + 2 * n]


def _send_wait(name, send, recv, srcs, lands, after, landed):
    n = len(srcs)

    def body(*refs):
        send_ref, recv_ref = refs[2 * n], refs[2 * n + 1]
        q_refs = refs[3 * n + 3:]
        x, y, c, _, _ = _place()
        for w in range(n):
            cp = pltpu.make_async_remote_copy(src_ref=landed(q_refs[w]), dst_ref=landed(q_refs[w]), send_sem=send_ref.at[w],
                                              recv_sem=recv_ref.at[w], device_id=(x, y, 1 - c), device_id_type=MESH)
            cp.wait_send()
            cp.wait_recv()

    outs = pl.pallas_call(
        body, name=name,
        in_specs=[HBM_SPEC] * (2 * n) + [SEM_SPEC, SEM_SPEC, ANY_SPEC],
        out_specs=[HBM_SPEC] * (2 * n),
        out_shape=[pltpu.HBM(a.shape, a.dtype) for a in list(srcs) + list(lands)],
        input_output_aliases={w: w for w in range(2 * n)},
        compiler_params=pltpu.CompilerParams(has_side_effects=DATAFLOW),
    )(*srcs, *lands, send, recv, after)
    return list(outs[:n]), list(outs[n:])


def exchange_start(name, grads, after):
    def issue(d_refs, r_refs, send, recv):
        x, y, c, _, _ = _place()
        for w in range(len(grads)):
            half = grads[w].shape[1] // 2
            pltpu.make_async_remote_copy(
                src_ref=d_refs[w].at[pl.ds(0, N_CHIPS), pl.ds((1 - c) * half, half)], dst_ref=r_refs[w],
                send_sem=send.at[w], recv_sem=recv.at[w], device_id=(x, y, 1 - c), device_id_type=MESH).start()

    return _send_start(name, grads, [(N_CHIPS, g.shape[1] // 2, g.shape[2]) for g in grads], issue, after)


def exchange_wait(name, send, recv, grads, lands, after):
    return _send_wait(name, send, recv, grads, lands, after, lambda r: r)


def scatter_start(name, parts):
    def issue(p_refs, q_refs, send, recv):
        x, y, c, jme, others = _place()
        for w in range(len(parts)):
            for (px, py) in others:
                pltpu.make_async_remote_copy(
                    src_ref=p_refs[w].at[2 * px + py], dst_ref=q_refs[w].at[jme], send_sem=send.at[w], recv_sem=recv.at[w],
                    device_id=(px, py, c), device_id_type=MESH).start()

    return _send_start(name, parts, [p.shape for p in parts], issue, None)


def scatter_wait(name, send, recv, parts, lands, after):
    return _send_wait(name, send, recv, parts, lands, after, lambda r: r.at[pl.ds(0, 3)])


def allreduce_small(v):
    R = v.shape[0]

    def body(v_ref, o_ref, sib_ref, mine_ref, all_ref, d_send, d_recv, i_send, i_recv):
        x, y, c, jme, others = _place()
        swap = pltpu.make_async_remote_copy(src_ref=v_ref, dst_ref=sib_ref, send_sem=d_send, recv_sem=d_recv,
                                            device_id=(x, y, 1 - c), device_id_type=MESH)
        swap.start()
        swap.wait()
        mine_ref[...] = v_ref[...] + sib_ref[...]
        for (px, py) in others:
            pltpu.make_async_remote_copy(src_ref=mine_ref, dst_ref=all_ref.at[jme], send_sem=i_send, recv_sem=i_recv,
                                         device_id=(px, py, c), device_id_type=MESH).start()
        three = all_ref.at[pl.ds(0, 3)]
        wait3 = pltpu.make_async_remote_copy(src_ref=three, dst_ref=three, send_sem=i_send, recv_sem=i_recv,
                                             device_id=(x, y, c), device_id_type=MESH)
        wait3.wait_recv()
        wait3.wait_send()
        all_ref[jme] = mine_ref[...]
        o_ref[...] = ((all_ref[0] + all_ref[1]) + all_ref[2]) + all_ref[3]

    vm = pl.BlockSpec(memory_space=pltpu.VMEM)
    return pl.pallas_call(
        body, name="allreduce_small", in_specs=[vm], out_specs=vm,
        out_shape=jax.ShapeDtypeStruct(v.shape, F32),
        scratch_shapes=[pltpu.VMEM((R, 128), F32), pltpu.VMEM((R, 128), F32), pltpu.VMEM((N_CHIPS, R, 128), F32),
                        pltpu.SemaphoreType.DMA, pltpu.SemaphoreType.DMA, pltpu.SemaphoreType.DMA, pltpu.SemaphoreType.DMA],
        compiler_params=pltpu.CompilerParams(vmem_limit_bytes=VMEM_LIMIT_BYTES),
    )(v)


def _row_tile(rows, width, mult=16, cap=3 << 20):
    best = rows
    for t in range(mult, rows + 1, mult):
        if rows % t == 0 and t * width * 4 <= cap:
            best = t
    return best


def add_sibling(name, mine, theirs, place):
    _, kh, ns = theirs.shape
    tr = _row_tile(kh, ns)
    nblk = kh // tr

    def body(place_ref, a_ref, b_ref, o_ref):
        o_ref[...] = (a_ref[...].astype(F32) + b_ref[...].astype(F32)).astype(BF16)

    return pl.pallas_call(
        body, name=name,
        grid_spec=pltpu.PrefetchScalarGridSpec(
            num_scalar_prefetch=1, grid=(N_CHIPS, nblk),
            in_specs=[pl.BlockSpec((None, tr, ns), lambda j, i, s: (j, s[1] * nblk + i, 0)),
                      pl.BlockSpec((None, tr, ns), lambda j, i, s: (j, i, 0))],
            out_specs=pl.BlockSpec((None, tr, ns), lambda j, i, s: (j, i, 0))),
        out_shape=jax.ShapeDtypeStruct(theirs.shape, BF16),
        compiler_params=_params(("parallel", "parallel")),
    )(place, mine, theirs)


def add_chips(name, q, p, place):
    _, kh, ns = q.shape
    tr = _row_tile(kh, ns)
    nblk = kh // tr

    def body(place_ref, *refs):
        q_refs, own_ref, o_ref = refs[:N_CHIPS], refs[N_CHIPS], refs[-1]
        jme = place_ref[0]
        tot = None
        for j in range(N_CHIPS):
            v = jnp.where(jme == j, own_ref[...], q_refs[j][...]).astype(F32)
            tot = v if tot is None else tot + v
        o_ref[...] = tot

    def q_ix(j):
        return lambda i, s: (jnp.where(s[0] == j, (j + 1) % N_CHIPS, j), i, 0)

    in_specs = [pl.BlockSpec((None, tr, ns), q_ix(j)) for j in range(N_CHIPS)]
    in_specs.append(pl.BlockSpec((None, tr, ns), lambda i, s: (s[0], i, 0)))
    return pl.pallas_call(
        body, name=name,
        grid_spec=pltpu.PrefetchScalarGridSpec(
            num_scalar_prefetch=1, grid=(nblk,), in_specs=in_specs,
            out_specs=pl.BlockSpec((tr, ns), lambda i, s: (s[1] * nblk + i, 0))),
        out_shape=jax.ShapeDtypeStruct((2 * kh, ns), F32),
        compiler_params=_params(("parallel",)),
    )(place, q, q, q, q, p)


ADAM_LR, ADAM_B1, ADAM_B2, ADAM_EPS, ADAM_WD, ADAM_STEP = 0.001, 0.9, 0.999, 1e-08, 0.01, 10


def adamw(name, w, g, m, v, layer, prev=None, after=None):
    _, k, ns = w.shape
    nsp = g.shape[1]
    tr = _row_tile(k, nsp, mult=8, cap=3 << 20)

    def body(w_ref, g_ref, m_ref, v_ref, *rest):
        go_ref, d_ref, mo_ref, vo_ref = rest[-4:]
        gv = g_ref[:, :ns] if nsp != ns else g_ref[...]
        mn = ADAM_B1 * m_ref[...] + (1.0 - ADAM_B1) * gv
        vn = ADAM_B2 * v_ref[...] + (1.0 - ADAM_B2) * (gv * gv)
        m_hat = mn / (1.0 - ADAM_B1 ** ADAM_STEP)
        v_hat = vn / (1.0 - ADAM_B2 ** ADAM_STEP)
        go_ref[...] = gv
        d_ref[...] = -ADAM_LR * (m_hat / (jnp.sqrt(v_hat) + ADAM_EPS) + ADAM_WD * w_ref[...])
        mo_ref[...] = mn
        vo_ref[...] = vn

    blk = pl.BlockSpec((None, tr, ns), lambda i: (layer, i, 0))
    gblk = pl.BlockSpec((tr, nsp), lambda i: (i, 0))
    args, in_specs, aliases = [w, g, m, v], [blk, gblk, blk, blk], {}
    if prev is not None:
        args += list(prev)
        in_specs += [pl.BlockSpec(memory_space=pl.ANY)] * 4
        aliases = {4 + i: i for i in range(4)}
    if after is not None:
        args.append(after)
        in_specs.append(pl.BlockSpec(memory_space=pl.ANY))
    return pl.pallas_call(
        body, name=name, grid=(k // tr,), in_specs=in_specs, out_specs=[blk] * 4,
        out_shape=[jax.ShapeDtypeStruct(w.shape, F32)] * 4, input_output_aliases=aliases,
        compiler_params=_params(("parallel",)),
    )(*args)


WEIGHTS = ("ffn_a_norm", "ffn_a_w1", "ffn_a_w3", "ffn_a_w2", "mix_norm", "w_in", "q_a_norm", "w_uq", "kv_a_norm", "w_ukv",
           "q_norm", "k_norm", "gm_v_norm", "gm_ws", "gm_bs", "attn_out_norm", "gm_out_norm", "w_out", "ffn_b_norm",
           "ffn_b_w1", "ffn_b_w3", "ffn_b_w2", "ple_gate_norm", "w_ple_gate", "w_ple", "ple_norm")
_FF = FF_PAD - FF_SHARD
BIG = {"ffn_a_w1": _FF, "ffn_a_w3": _FF, "ffn_a_w2": _FF, "ffn_b_w1": _FF, "ffn_b_w3": _FF, "ffn_b_w2": _FF,
       "w_in": IN_SHARD_PAD - IN_SHARD, "w_uq": 0, "w_ukv": 0, "w_ple": 0, "w_out": 0, "w_ple_gate": 0}
TRANSPOSED = ("ffn_a_w1", "ffn_a_w3", "ffn_b_w1", "ffn_b_w3", "w_in")
SMALL = tuple(n for n in WEIGHTS if n not in BIG)
PACK = 1024


def _pack_small(d):
    parts = []
    for n in SMALL:
        flat = d[n].reshape(-1)
        parts.append(jnp.pad(flat, (0, (-flat.shape[0]) % PACK)))
    return jnp.concatenate(parts).reshape(-1, 128)


def _unpack_small(buf, like):
    flat = buf.reshape(-1)
    out, pos = {}, 0
    for n in SMALL:
        size = math.prod(like[n].shape)
        out[n] = flat[pos:pos + size].reshape(like[n].shape)
        pos += size + (-size) % PACK
    return out


def kernel(*args):
    names = (("x", "p", "positions") + WEIGHTS + ("loss_target",) + tuple("m_" + n for n in WEIGHTS)
             + tuple("v_" + n for n in WEIGHTS))
    a = dict(zip(names, args, strict=True))
    x, p, positions, target = a["x"][0], a["p"][:, 0], a["positions"][0], a["loss_target"][0]
    for n in TRANSPOSED:
        for pre in ("", "m_", "v_"):
            a[pre + n] = jnp.swapaxes(a[pre + n], 1, 2)

    place = jnp.stack([2 * lax.axis_index("x") + lax.axis_index("y"), lax.axis_index("c")]).astype(jnp.int32)
    small = {n: a[n] for n in SMALL}
    tabs = rope_tables(positions)
    order = {"l0a": ("ffn_a_w1", "ffn_a_w3"), "l0b": ("ffn_a_w2",), "l0c": ("w_in", "w_uq", "w_ukv", "w_out"),
             "l0d": ("ffn_b_w1", "ffn_b_w3", "ffn_b_w2", "w_ple_gate", "w_ple")}
    prep = lambda n, l, after: prep_shard(f"prep_{n}_{l}", a[n], l, BIG[n], place, after)
    flights, token = {}, None
    for tag, names in order.items():
        flights[tag] = gather_start(f"gather_{tag}_start", [prep(n, 0, token) for n in names], None)
        token = flights[tag][3]
    slots1 = []
    for n in BIG:
        slots1.append(prep(n, 1, slots1[-1] if slots1 else token))

    def arrive(tag, after):
        send, recv, flying, _ = flights[tag]
        arrived = gather_wait(f"gather_{tag}_wait", send, recv, flying, after)
        send, recv, flying, token = forward_start(f"forward_{tag}_start", arrived)
        return dict(zip(order[tag], gather_wait(f"forward_{tag}_wait", send, recv, flying, token)))

    G0 = arrive("l0a", slots1[-1])

    def before_down(s):
        G0.update(arrive("l0b", s))
        return G0["ffn_a_w2"]

    def before_mixer(h1, Gl, small_):
        G0.update(arrive("l0c", h1))
        flights["l1"] = gather_start("gather_l1_start", slots1, G0["w_uq"])
        return G0, {**small_, "mix_norm": small_["mix_norm"] + flights["l1"][3][0, 0]}

    def before_ffn_b(h2, Gl):
        G0.update(arrive("l0d", h2))
        return G0

    def before_ple(h3):
        send, recv, flying, _ = flights["l1"]
        flights["f1"] = forward_start("forward_l1_start", gather_wait("gather_l1_wait", send, recv, flying, h3))
        return flights["f1"][3]

    h, saved0 = layer_fwd(0, x, p[0], G0, small, tabs,
                          {"down_a": before_down, "mixer": before_mixer, "ffn_b": before_ffn_b, "ple": before_ple})
    G1 = dict(zip(BIG, gather_wait("forward_l1_wait", *flights["f1"][:3], h)))
    h, saved1 = layer_fwd(1, h, p[1], G1, small, tabs)
    dh, loss = loss_grad(h, target)
    loss = lax.psum(loss, ("x", "y", "c"))

    groups = {"l1": tuple(BIG),
              "l0a": ("w_ple_gate", "w_ple", "ffn_b_w1", "ffn_b_w3", "ffn_b_w2"),
              "l0b": ("w_in", "w_uq", "w_ukv", "w_out"),
              "l0c": ("ffn_a_w1", "ffn_a_w3", "ffn_a_w2")}
    crossing, started = [], {}

    def begin(tag, gl, after):
        ex = exchange_start(f"exchange_{tag}_start", [gl[n] for n in groups[tag]], after)
        crossing.append((tag, ex))
        return ex[4]

    def advance(after):
        tag, (send, recv, mine, lands, _) = crossing.pop()
        mine, theirs = exchange_wait(f"exchange_{tag}_wait", send, recv, mine, lands, after)
        parts = [add_sibling(f"add_sibling_{n}_{tag}", d, r, place) for n, d, r in zip(groups[tag], mine, theirs)]
        started[tag] = scatter_start(f"scatter_{tag}_start", parts)
        return started[tag][4]

    def sum_chips(tag, after):
        send, recv, parts, lands, _ = started[tag]
        parts, slabs = scatter_wait(f"scatter_{tag}_wait", send, recv, parts, lands, after)
        halves = [add_chips(f"add_chips_{n}_{tag}", q, pt, place) for n, q, pt in zip(groups[tag], slabs, parts)]
        return share_start(f"share_{tag}_start", halves)

    def shared(tag, sharing, after):
        send, recv, flying, _ = sharing
        return dict(zip(groups[tag], share_wait(f"share_{tag}_wait", send, recv, flying, after)))

    def update(names, full, layer, prev, after):
        outs = {}
        for n in names:
            outs[n] = adamw(f"adamw_{n}_{layer}", a[n], full[n], a["m_" + n], a["v_" + n], layer, prev and prev[n], after)
            after = outs[n][1]
        return outs, after

    grads = [None, None]
    dh, grads[1] = layer_bwd(1, dh, p[1], G1, small, tabs, saved1)
    token = begin("l1", grads[1], None)
    w0 = {**saved0[0], "ple_gate_norm": saved0[0]["ple_gate_norm"] + token[0, 0]}
    hooks = {"ffn_b": lambda gl, dh_: advance(dh_),
             "mixer": lambda gl, dh_: begin("l0a", gl, None),
             "ffn_a": lambda gl, dh_: begin("l0b", gl, advance(dh_)),
             "ffn_a_dw": lambda gl, dh_: advance(dh_)}
    gx, grads[0] = layer_bwd(0, dh, p[0], G0, small, tabs, (w0,) + saved0[1:], hooks)
    token = begin("l0c", grads[0], None)
    sharing = sum_chips("l1", token)
    full1 = shared("l1", sharing, advance(sharing[3]))
    outs1, behind = update(BIG, full1, 1, None, None)
    sharing_a = sum_chips("l0a", behind)
    sharing_b = sum_chips("l0b", sharing_a[3])
    full0 = shared("l0a", sharing_a, sharing_b[3])
    outs0, behind = update(groups["l0a"], full0, 0, outs1, None)
    sharing_c = sum_chips("l0c", behind)
    full0.update(shared("l0b", sharing_b, sharing_c[3]))
    outs, behind = update(groups["l0b"], full0, 0, outs1, None)
    outs0.update(outs)
    full0.update(shared("l0c", sharing_c, behind))
    outs0.update(update(groups["l0c"], full0, 0, outs1, None)[0])

    out_g, out_d, out_m, out_v = {}, {}, {}, {}
    for n in BIG:
        outs = [jnp.swapaxes(o, 1, 2) for o in outs0[n]] if n in TRANSPOSED else outs0[n]
        out_g[n], out_d[n], out_m[n], out_v[n] = outs

    gs = allreduce_small(_pack_small({n: jnp.stack([grads[0][n], grads[1][n]]) for n in SMALL}))
    rows = gs.shape[0] // 2
    packed = [_pack_small(d).reshape(2, rows, 128) for d in
              (small, {n: a["m_" + n] for n in SMALL}, {n: a["v_" + n] for n in SMALL})]
    gs = gs.reshape(2, rows, 128)
    sm = adamw("adamw_small_0", packed[0], gs[0], packed[1], packed[2], 0)
    sm = adamw("adamw_small_1", packed[0], gs[1], packed[1], packed[2], 1, sm)
    for dst, buf in zip((out_g, out_d, out_m, out_v), sm):
        dst.update(_unpack_small(buf, small))

    return (loss, gx[None], *[out_g[n] for n in WEIGHTS], *[out_d[n] for n in WEIGHTS],
            *[out_m[n] for n in WEIGHTS], *[out_v[n] for n in WEIGHTS])
```

```python
import math

import jax
import jax.numpy as jnp
from jax import lax
from jax.experimental import pallas as pl
from jax.experimental.pallas import tpu as pltpu

F32 = jnp.float32
BF16 = jnp.bfloat16

D_MODEL = 2048
D_FF = 5504
N_CHIPS = 4
FF_SHARD = D_FF // N_CHIPS
FF_PAD = 1408
FF_P = N_CHIPS * FF_PAD
HEADS = 8
QK_NOPE = 128
QK_ROPE = 64
QK_DIM = 192
HEAD_PAD = 256
V_DIM = 128
Q_RANK = 512
KV_RANK = 256
ATTN_W = 1024
GM_W = 1024
GROUPS = 8
CHUNK = 128
PLE_DIM = 256
IN_P = 3072
IN_SHARD = 720
IN_SHARD_PAD = 736
EPS = 1e-6
ROPE_BASE = 10000.0
ATTN_SCALE = QK_DIM ** -0.5
VMEM_LIMIT_BYTES = 56 * 1024 * 1024


def _params(sem):
    return pltpu.CompilerParams(dimension_semantics=sem, vmem_limit_bytes=VMEM_LIMIT_BYTES)


def _bf(x):
    return x if x.dtype == BF16 else x.astype(BF16)


def _sigmoid(x):
    return 1.0 / (1.0 + jnp.exp(-x))


_GELU_C = math.sqrt(2.0 / math.pi)


def _gelu(x):
    t = jnp.tanh(_GELU_C * (x + 0.044715 * x * x * x))
    return 0.5 * x * (1.0 + t)


def _gelu_grad(x):
    t = jnp.tanh(_GELU_C * (x + 0.044715 * x * x * x))
    return 0.5 * (1.0 + t) + 0.5 * x * (1.0 - t * t) * _GELU_C * (1.0 + 3 * 0.044715 * x * x)


def op_a(a, tm, tk):
    return (a, (tm, tk), lambda i, j, k: (i, k), 1)


def op_at(a, tm, tk):
    return (a, (tk, tm), lambda i, j, k: (k, i), 0)


def op_b(b, tk, tn):
    return (b, (tk, tn), lambda i, j, k: (k, j), 0)


def op_bt(b, tk, tn):
    return (b, (tn, tk), lambda i, j, k: (j, k), 1)


def op_b_cols(g, pre, tk, tn):
    nb = g.shape[-1] // tn
    none = (None,) * (1 + len(pre))
    return (g, none + (tk, tn), lambda i, j, k: (j // nb,) + tuple(pre) + (k, j % nb), 0)


def op_b_rows(g, pre, tk, tn, koff=0):
    nb = g.shape[-2] // tk
    none = (None,) * (1 + len(pre))
    return (g, none + (tk, tn), lambda i, j, k: ((k + koff) // nb,) + tuple(pre) + ((k + koff) % nb, j), 0)


def op_b_rows_t(g, pre, tk, tn):
    nb = g.shape[-2] // tn
    none = (None,) * (1 + len(pre))
    return (g, none + (tn, tk), lambda i, j, k: (j // nb,) + tuple(pre) + (j % nb, k), 1)


def tile_mn(x, tm, tn):
    return (x, (tm, tn), lambda i, j: (i, j))


def out_mn(M, N, tm, tn, dtype):
    return (jax.ShapeDtypeStruct((M, N), dtype), (tm, tn), lambda i, j: (i, j))


def out_cols(M, ns, tm, tn, dtype):
    nb = ns // tn
    return (jax.ShapeDtypeStruct((N_CHIPS, M, ns), dtype), (None, tm, tn), lambda i, j: (j // nb, i, j % nb))


def matmul(name, grid_mnk, a_ops, b_ops, terms, n_acc, extras, outs, epilogue, acc_tile, n_outer=False, after=None):
    gm, gn, gk = grid_mnk
    na, nb, nx, no = len(a_ops), len(b_ops), len(extras), len(outs)
    nd = 0 if after is None else 1

    def body(*refs):
        a_refs, b_refs = refs[:na], refs[na:na + nb]
        x_refs = refs[na + nb:na + nb + nx]
        o_refs = refs[na + nb + nx + nd:na + nb + nx + nd + no]
        acc_refs = refs[na + nb + nx + nd + no:]
        k = pl.program_id(2)

        @pl.when(k == 0)
        def _():
            for acc in acc_refs:
                acc[...] = jnp.zeros_like(acc)

        for ai, bi, ci in terms:
            dims = (((a_ops[ai][3],), (b_ops[bi][3],)), ((), ()))
            acc_refs[ci][...] += lax.dot_general(_bf(a_refs[ai][...]), _bf(b_refs[bi][...]), dims,
                                                 preferred_element_type=F32)

        @pl.when(k == gk - 1)
        def _():
            res = epilogue([acc[...] for acc in acc_refs], [x[...] for x in x_refs])
            for o, v in zip(o_refs, res):
                o[...] = v.astype(o.dtype)

    if n_outer:
        grid = (gn, gm, gk)

        def ix3(f):
            return lambda j, i, k: f(i, j, k)

        def ix2(f):
            return lambda j, i, k: f(i, j)
    else:
        grid = (gm, gn, gk)

        def ix3(f):
            return lambda i, j, k: f(i, j, k)

        def ix2(f):
            return lambda i, j, k: f(i, j)

    in_specs = [pl.BlockSpec(blk, ix3(f)) for (_, blk, f, _) in list(a_ops) + list(b_ops)]
    in_specs += [pl.BlockSpec(blk, ix2(f)) for (_, blk, f) in extras]
    in_specs += [pl.BlockSpec(memory_space=pl.ANY)] * nd
    out_specs = [pl.BlockSpec(blk, ix2(f)) for (_, blk, f) in outs]
    return pl.pallas_call(
        body,
        name=name,
        grid=grid,
        in_specs=in_specs,
        out_specs=out_specs,
        out_shape=[s for (s, _, _) in outs],
        scratch_shapes=[pltpu.VMEM(acc_tile, F32) for _ in range(n_acc)],
        compiler_params=_params(("parallel", "parallel", "arbitrary")),
    )(*[o[0] for o in a_ops], *[o[0] for o in b_ops], *[x[0] for x in extras], *([after] * nd))


def _acc0(accs, xs):
    return (accs[0],)


def mm_simple(name, a, b_op_fn, M, N, K, tm, tn, tk, out_dtype=F32, a_t=False, extras=(), epilogue=_acc0, outs=None,
              after=None):
    a_op = op_at(a, tm, tk) if a_t else op_a(a, tm, tk)
    outs = outs or [out_mn(M, N, tm, tn, out_dtype)]
    return matmul(name, (M // tm, N // tn, K // tk), [a_op], [b_op_fn(tk, tn)], [(0, 0, 0)], 1,
                  list(extras), outs, epilogue, (tm, tn), after=after)


def rms_fwd(name, x, g, width, col_blk=0, tm=512, out_dtype=BF16):
    T = x.shape[0]

    def body(x_ref, g_ref, o_ref):
        xv = x_ref[...].astype(F32)
        r = lax.rsqrt(jnp.mean(xv * xv, axis=-1, keepdims=True) + EPS)
        o_ref[...] = (xv * r * g_ref[...]).astype(o_ref.dtype)

    return pl.pallas_call(
        body, name=name, grid=(T // tm,),
        in_specs=[pl.BlockSpec((tm, width), lambda i: (i, col_blk)), pl.BlockSpec((1, width), lambda i: (0, 0))],
        out_specs=pl.BlockSpec((tm, width), lambda i: (i, 0)),
        out_shape=jax.ShapeDtypeStruct((T, width), out_dtype),
        compiler_params=_params(("parallel",)),
    )(x, g.reshape(1, width))


def rms_bwd(name, x, g, dn, width, col_blk=0, dres=None, tm=512, with_delta=False, bf16_copy=False):
    T = x.shape[0]
    has_res = dres is not None

    def body(*refs):
        x_ref, g_ref, dn_ref = refs[:3]
        pos = 3
        res_ref = None
        if has_res:
            res_ref = refs[pos]
            pos += 1
        dx_ref, dg_ref = refs[pos], refs[pos + 1]
        delta_ref = refs[pos + 2] if with_delta else None
        lo_ref = refs[-1] if bf16_copy else None
        i = pl.program_id(0)
        xv = x_ref[...].astype(F32)
        r = lax.rsqrt(jnp.mean(xv * xv, axis=-1, keepdims=True) + EPS)
        xh = xv * r
        d = dn_ref[...].astype(F32)
        gd = d * g_ref[...]
        dx = r * (gd - xh * jnp.mean(gd * xh, axis=-1, keepdims=True))
        if has_res:
            dx = dx + res_ref[...]
        dx_ref[...] = dx.astype(dx_ref.dtype)
        if bf16_copy:
            lo_ref[...] = dx.astype(BF16)
        part = jnp.sum(d * xh, axis=0, keepdims=True)

        @pl.when(i == 0)
        def _():
            dg_ref[...] = part

        @pl.when(i > 0)
        def _():
            dg_ref[...] += part

        if with_delta:
            for h in range(width // 128):
                sl = slice(h * 128, (h + 1) * 128)
                s = jnp.sum(dx[:, sl] * xv[:, sl], axis=-1, keepdims=True)
                delta_ref[:, sl] = jnp.broadcast_to(s, (tm, 128))

    in_specs = [pl.BlockSpec((tm, width), lambda i: (i, col_blk)), pl.BlockSpec((1, width), lambda i: (0, 0)),
                pl.BlockSpec((tm, width), lambda i: (i, 0))]
    args = [x, g.reshape(1, width), dn]
    if has_res:
        in_specs.append(pl.BlockSpec((tm, width), lambda i: (i, 0)))
        args.append(dres)
    out_specs = [pl.BlockSpec((tm, width), lambda i: (i, 0)), pl.BlockSpec((1, width), lambda i: (0, 0))]
    out_shape = [jax.ShapeDtypeStruct((T, width), F32), jax.ShapeDtypeStruct((1, width), F32)]
    if with_delta:
        out_specs.append(pl.BlockSpec((tm, width), lambda i: (i, 0)))
        out_shape.append(jax.ShapeDtypeStruct((T, width), F32))
    if bf16_copy:
        out_specs.append(pl.BlockSpec((tm, width), lambda i: (i, 0)))
        out_shape.append(jax.ShapeDtypeStruct((T, width), BF16))
    return pl.pallas_call(
        body, name=name, grid=(T // tm,), in_specs=in_specs, out_specs=out_specs, out_shape=out_shape,
        compiler_params=_params(("arbitrary",)),
    )(*args)


def ffn_fwd(tag, h, g, w1g, w3g, w2g, pre, w2_late=None):
    T = h.shape[0]
    n = rms_fwd(f"{tag}_rms", h, g, D_MODEL)
    tm, tn = 512, FF_PAD

    def up_epi(accs, xs):
        a1, a3 = accs
        return a1, a3, a1 * _sigmoid(a1) * a3

    a1, a3, s = matmul(
        f"{tag}_up", (T // tm, FF_P // tn, 1),
        [op_a(n, tm, D_MODEL)], [op_b_rows_t(w1g, pre, D_MODEL, tn), op_b_rows_t(w3g, pre, D_MODEL, tn)],
        [(0, 0, 0), (0, 1, 1)], 2, [],
        [out_mn(T, FF_P, tm, tn, BF16)] * 3, up_epi, (tm, tn), n_outer=True)

    if w2_late is not None:
        w2g = w2_late(s)
    tm2, tn2 = 1024, 1024
    (h_out,) = matmul(
        f"{tag}_down", (T // tm2, D_MODEL // tn2, N_CHIPS),
        [op_a(s, tm2, FF_PAD)], [op_b_rows(w2g, pre, FF_PAD, tn2)],
        [(0, 0, 0)], 1, [tile_mn(h, tm2, tn2)],
        [out_mn(T, D_MODEL, tm2, tn2, F32)], lambda accs, xs: (xs[0] + 0.5 * accs[0],), (tm2, tn2))
    return h_out, (n, a1, a3, s)


def ffn_bwd(tag, dh_out, dh_bf, h, g, res, w1g, w3g, w2g, pre, after=None, before_dw=None):
    n, a1, a3, s = res
    T = h.shape[0]
    tm, tn = 512, FF_PAD

    def act_epi(accs, xs):
        ds = 0.5 * accs[0]
        x1, x3 = xs[0].astype(F32), xs[1].astype(F32)
        sg = _sigmoid(x1)
        silu = x1 * sg
        return ds * x3 * (sg + silu * (1.0 - sg)), ds * silu

    da1, da3 = matmul(
        f"{tag}_dact", (T // tm, FF_P // tn, 1),
        [op_a(dh_bf, tm, D_MODEL)], [op_b_rows_t(w2g, pre, D_MODEL, tn)],
        [(0, 0, 0)], 1, [tile_mn(a1, tm, tn), tile_mn(a3, tm, tn)],
        [out_mn(T, FF_P, tm, tn, BF16)] * 2, act_epi, (tm, tn), n_outer=True, after=after)

    tm2, tn2 = 1024, 1024
    (dn,) = matmul(
        f"{tag}_dn", (T // tm2, D_MODEL // tn2, N_CHIPS),
        [op_a(da1, tm2, FF_PAD), op_a(da3, tm2, FF_PAD)],
        [op_b_rows(w1g, pre, FF_PAD, tn2), op_b_rows(w3g, pre, FF_PAD, tn2)],
        [(0, 0, 0), (1, 1, 0)], 1, [], [out_mn(T, D_MODEL, tm2, tn2, F32)], _acc0, (tm2, tn2))
    dh, dg, dh_lo = rms_bwd(f"{tag}_rms_bwd", h, g, dn, D_MODEL, dres=dh_out, bf16_copy=True)
    if before_dw is not None:
        after = before_dw(dh)

    tk, tn3 = T, 512

    def dw_t(nm, left, right, scale):
        (dw,) = matmul(
            f"{tag}_{nm}", (FF_P // FF_PAD, D_MODEL // tn3, T // tk),
            [op_at(left, FF_PAD, tk)], [op_b(right, tk, tn3)],
            [(0, 0, 0)], 1, [], [out_mn(FF_P, D_MODEL, FF_PAD, tn3, BF16)],
            lambda accs, xs: (scale * accs[0],), (FF_PAD, tn3), after=after)
        return dw

    dw2 = dw_t("dw2", s, dh_bf, 0.5)
    dw1 = dw_t("dw1", da1, n, 1.0)
    dw3 = dw_t("dw3", da3, n, 1.0)
    return dh, dh_lo, dg, dw1, dw3, dw2


def rope_tables(positions):
    inv_freq = ROPE_BASE ** (-jnp.arange(0, QK_ROPE, 2, dtype=F32) / QK_ROPE)
    ang = positions.astype(F32)[:, None] * inv_freq
    cos, sin = jnp.cos(ang), jnp.sin(ang)
    T = positions.shape[0]
    one, zero = jnp.ones((T, QK_NOPE), F32), jnp.zeros((T, 64), F32)
    z32, z128 = jnp.zeros((T, 32), F32), jnp.zeros((T, QK_NOPE), F32)
    c = jnp.concatenate([one, cos, cos, zero], axis=1)
    s1 = jnp.concatenate([z128, -sin, z32, zero], axis=1)
    s2 = jnp.concatenate([z128, z32, sin, zero], axis=1)
    return c, s1, s2


def _rope(y, c, s1, s2):
    return y * c + pltpu.roll(y, HEAD_PAD - 32, 1) * s1 + pltpu.roll(y, 32, 1) * s2


def _rope_t(d, c, s1, s2):
    return d * c + pltpu.roll(d * s1, 32, 1) + pltpu.roll(d * s2, HEAD_PAD - 32, 1)


def _head_norm(x):
    r = lax.rsqrt(jnp.sum(x * x, axis=-1, keepdims=True) * (1.0 / QK_DIM) + EPS)
    return x * r, r


def qk_prep_fwd(tag, q_raw, kk_raw, z_p, gq, gk, tabs, tm=256):
    T = q_raw.shape[0]
    c, s1, s2 = tabs

    def body(q_ref, k_ref, kr_ref, gq_ref, gk_ref, c_ref, s1_ref, s2_ref, qo_ref, ko_ref):
        cv, s1v, s2v = c_ref[...], s1_ref[...], s2_ref[...]
        kr = kr_ref[...]
        for h in range(HEADS):
            sl = slice(h * HEAD_PAD, (h + 1) * HEAD_PAD)
            xh, _ = _head_norm(q_ref[:, sl])
            qo_ref[:, sl] = (_rope(xh * gq_ref[...], cv, s1v, s2v) * ATTN_SCALE).astype(BF16)
            xh, _ = _head_norm(k_ref[:, sl] + kr)
            ko_ref[:, sl] = _rope(xh * gk_ref[...], cv, s1v, s2v).astype(BF16)

    row = lambda i: (i, 0)
    full = pl.BlockSpec((tm, HEADS * HEAD_PAD), row)
    tab = pl.BlockSpec((tm, HEAD_PAD), row)
    vec = pl.BlockSpec((1, HEAD_PAD), lambda i: (0, 0))
    return pl.pallas_call(
        body, name=f"{tag}_qk_prep", grid=(T // tm,),
        in_specs=[full, full, pl.BlockSpec((tm, HEAD_PAD), lambda i: (i, 3)), vec, vec, tab, tab, tab],
        out_specs=[full, full],
        out_shape=[jax.ShapeDtypeStruct((T, HEADS * HEAD_PAD), BF16)] * 2,
        compiler_params=_params(("parallel",)),
    )(q_raw, kk_raw, z_p, gq, gk, c, s1, s2)


def qk_prep_bwd(tag, dq_full, dk_full, q_raw, kk_raw, z_p, gq, gk, tabs, tm=256):
    T = q_raw.shape[0]
    c, s1, s2 = tabs

    def body(dq_ref, dk_ref, q_ref, k_ref, kr_ref, gq_ref, gk_ref, c_ref, s1_ref, s2_ref,
             dqr_ref, dkr_ref, dz_ref, dgq_ref, dgk_ref):
        i = pl.program_id(0)
        cv, s1v, s2v = c_ref[...], s1_ref[...], s2_ref[...]
        kr = kr_ref[...]
        lane = lax.broadcasted_iota(jnp.int32, (tm, HEAD_PAD), 1)
        slot = ((lane >= QK_NOPE) & (lane < QK_DIM)).astype(F32)

        def one(x, g, d):
            xh, r = _head_norm(x)
            dy = _rope_t(d, cv, s1v, s2v)
            gd = dy * g
            dx = r * (gd - xh * (jnp.sum(gd * xh, axis=-1, keepdims=True) * (1.0 / QK_DIM)))
            return dx, jnp.sum(dy * xh, axis=0, keepdims=True)

        dgq = jnp.zeros((1, HEAD_PAD), F32)
        dgk = jnp.zeros((1, HEAD_PAD), F32)
        dz = jnp.zeros((tm, HEAD_PAD), F32)
        for h in range(HEADS):
            sl = slice(h * HEAD_PAD, (h + 1) * HEAD_PAD)
            dx, dg = one(q_ref[:, sl], gq_ref[...], dq_ref[:, sl].astype(F32) * ATTN_SCALE)
            dqr_ref[:, sl] = dx
            dgq = dgq + dg
            dx, dg = one(k_ref[:, sl] + kr, gk_ref[...], dk_ref[:, sl].astype(F32))
            dkr_ref[:, sl] = dx
            dgk = dgk + dg
            dz = dz + dx
        dz_ref[...] = dz * slot

        @pl.when(i == 0)
        def _():
            dgq_ref[...] = dgq
            dgk_ref[...] = dgk

        @pl.when(i > 0)
        def _():
            dgq_ref[...] += dgq
            dgk_ref[...] += dgk

    row = lambda i: (i, 0)
    full = pl.BlockSpec((tm, HEADS * HEAD_PAD), row)
    tab = pl.BlockSpec((tm, HEAD_PAD), row)
    vec = pl.BlockSpec((1, HEAD_PAD), lambda i: (0, 0))
    return pl.pallas_call(
        body, name=f"{tag}_qk_prep_bwd", grid=(T // tm,),
        in_specs=[full, full, full, full, pl.BlockSpec((tm, HEAD_PAD), lambda i: (i, 3)), vec, vec, tab, tab, tab],
        out_specs=[full, full, tab, vec, vec],
        out_shape=[jax.ShapeDtypeStruct((T, HEADS * HEAD_PAD), F32)] * 2
        + [jax.ShapeDtypeStruct((T, HEAD_PAD), F32)] + [jax.ShapeDtypeStruct((1, HEAD_PAD), F32)] * 2,
        compiler_params=_params(("arbitrary",)),
    )(dq_full, dk_full, q_raw, kk_raw, z_p, gq, gk, c, s1, s2)


def attn_fwd(tag, q_full, k_full, vv, blk=512):
    T = q_full.shape[0]
    nb = T // blk
    neg = float(jnp.finfo(jnp.float32).min)

    def body(q_ref, k_ref, v_ref, o_ref, lse_ref, m_ref, l_ref, acc_ref):
        i = pl.program_id(1)
        m_ref[...] = jnp.full_like(m_ref, neg)
        l_ref[...] = jnp.zeros_like(l_ref)
        acc_ref[...] = jnp.zeros_like(acc_ref)
        q = q_ref[...]

        def step(j, masked):
            rows = pl.ds(pl.multiple_of(j * blk, blk), blk)
            s = lax.dot_general(q, k_ref[rows, :], (((1,), (1,)), ((), ())), preferred_element_type=F32)
            if masked:
                row = lax.broadcasted_iota(jnp.int32, (blk, blk), 0)
                col = lax.broadcasted_iota(jnp.int32, (blk, blk), 1)
                s = jnp.where(col <= row, s, neg)
            m_prev = m_ref[...]
            m_new = jnp.maximum(m_prev, jnp.max(s, axis=-1, keepdims=True))
            alpha = jnp.exp(m_prev - m_new)
            p = jnp.exp(s - m_new[:, :1])
            l_ref[...] = alpha * l_ref[...] + jnp.sum(p, axis=-1, keepdims=True)
            acc_ref[...] = alpha * acc_ref[...] + jnp.dot(p.astype(BF16), v_ref[rows, :], preferred_element_type=F32)
            m_ref[...] = m_new

        def off_diagonal(j, carry):
            step(j, False)
            return carry

        lax.fori_loop(0, i, off_diagonal, 0)
        step(i, True)
        o_ref[...] = acc_ref[...] / l_ref[...]
        lse_ref[...] = m_ref[...] + jnp.log(l_ref[...])

    return pl.pallas_call(
        body, name=f"{tag}_attn_fwd", grid=(HEADS, nb),
        in_specs=[pl.BlockSpec((blk, HEAD_PAD), lambda h, i: (i, h)),
                  pl.BlockSpec((T, HEAD_PAD), lambda h, i: (0, h)), pl.BlockSpec((T, V_DIM), lambda h, i: (0, h))],
        out_specs=[pl.BlockSpec((blk, V_DIM), lambda h, i: (i, h))] * 2,
        out_shape=[jax.ShapeDtypeStruct((T, ATTN_W), F32)] * 2,
        scratch_shapes=[pltpu.VMEM((blk, V_DIM), F32)] * 3,
        compiler_params=_params(("parallel", "parallel")),
    )(q_full, k_full, vv)


def attn_bwd(tag, q_full, k_full, vv, do, lse, delta, blk=512):
    T = q_full.shape[0]
    nb = T // blk
    neg = float(jnp.finfo(jnp.float32).min)

    def body(q_ref, k_ref, v_ref, do_ref, lse_ref, dl_ref, dq_ref, dk_ref, dv_ref, dk_acc, dv_acc):
        j = pl.program_id(1)

        @pl.when(j == 0)
        def _():
            dq_ref[...] = jnp.zeros_like(dq_ref)

        dk_acc[...] = jnp.zeros_like(dk_acc)
        dv_acc[...] = jnp.zeros_like(dv_acc)
        k, v = k_ref[...], v_ref[...]

        def step(i, masked):
            rows = pl.ds(pl.multiple_of(i * blk, blk), blk)
            q = q_ref[rows, :]
            s = lax.dot_general(q, k, (((1,), (1,)), ((), ())), preferred_element_type=F32)
            if masked:
                row = lax.broadcasted_iota(jnp.int32, (blk, blk), 0)
                col = lax.broadcasted_iota(jnp.int32, (blk, blk), 1)
                s = jnp.where(col <= row, s, neg)
            p = jnp.exp(s - lse_ref[rows, :1])
            dob = _bf(do_ref[rows, :])
            dv_acc[...] += lax.dot_general(p.astype(BF16), dob, (((0,), (0,)), ((), ())), preferred_element_type=F32)
            dp = lax.dot_general(dob, v, (((1,), (1,)), ((), ())), preferred_element_type=F32)
            ds = (p * (dp - dl_ref[rows, :1])).astype(BF16)
            dk_acc[...] += lax.dot_general(ds, q, (((0,), (0,)), ((), ())), preferred_element_type=F32)
            dq_ref[rows, :] += jnp.dot(ds, k, preferred_element_type=F32)

        def off_diagonal(i, carry):
            step(i, False)
            return carry

        step(j, True)
        lax.fori_loop(j + 1, nb, off_diagonal, 0)
        dk_ref[...] = dk_acc[...]
        dv_ref[...] = dv_acc[...]

    head = lambda h, j: (0, h)
    kv_ix = lambda h, j: (j, h)
    return pl.pallas_call(
        body, name=f"{tag}_attn_bwd", grid=(HEADS, nb),
        in_specs=[pl.BlockSpec((T, HEAD_PAD), head), pl.BlockSpec((blk, HEAD_PAD), kv_ix),
                  pl.BlockSpec((blk, V_DIM), kv_ix), pl.BlockSpec((T, V_DIM), head),
                  pl.BlockSpec((T, V_DIM), head), pl.BlockSpec((T, V_DIM), head)],
        out_specs=[pl.BlockSpec((T, HEAD_PAD), head),
                   pl.BlockSpec((blk, HEAD_PAD), kv_ix), pl.BlockSpec((blk, V_DIM), kv_ix)],
        out_shape=[jax.ShapeDtypeStruct((T, HEADS * HEAD_PAD), F32)] * 2 + [jax.ShapeDtypeStruct((T, ATTN_W), F32)],
        scratch_shapes=[pltpu.VMEM((blk, HEAD_PAD), F32), pltpu.VMEM((blk, V_DIM), F32)],
        compiler_params=_params(("parallel", "arbitrary")),
    )(q_full, k_full, vv, do, lse, delta)


def _gm_forward(u, v, gv, wc_ref, bb_ref, nchunk):
    ug = _gelu(u)
    vg = _gelu(v)
    rv = lax.rsqrt(jnp.mean(vg * vg, axis=-1, keepdims=True) + EPS)
    vhat = vg * rv
    vn = (vhat * gv).astype(BF16)
    gates = []
    for cidx in range(nchunk):
        rows = slice(cidx * CHUNK, (cidx + 1) * CHUNK)
        gates.append(jnp.concatenate(
            [jnp.dot(wc_ref[gidx], vn[rows, gidx * 128:(gidx + 1) * 128], preferred_element_type=F32) + bb_ref[gidx]
             for gidx in range(GROUPS)], axis=1))
    gate = jnp.concatenate(gates, axis=0)
    return ug, vhat, rv, vn, gate


def gmlp_fwd(tag, z_p, gv, gout, wc, bb, tm=256):
    T = z_p.shape[0]
    nchunk = tm // CHUNK

    def body(u_ref, v_ref, gv_ref, go_ref, wc_ref, bb_ref, o_ref):
        ug, _, _, _, gate = _gm_forward(u_ref[...], v_ref[...], gv_ref[...], wc_ref, bb_ref, nchunk)
        go = ug * gate
        ro = lax.rsqrt(jnp.mean(go * go, axis=-1, keepdims=True) + EPS)
        o_ref[...] = (go * ro * go_ref[...]).astype(BF16)

    vec = pl.BlockSpec((1, GM_W), lambda i: (0, 0))
    w3 = pl.BlockSpec((GROUPS, CHUNK, CHUNK), lambda i: (0, 0, 0))
    return pl.pallas_call(
        body, name=f"{tag}_gmlp_fwd", grid=(T // tm,),
        in_specs=[pl.BlockSpec((tm, GM_W), lambda i: (i, 1)), pl.BlockSpec((tm, GM_W), lambda i: (i, 2)), vec, vec, w3, w3],
        out_specs=pl.BlockSpec((tm, GM_W), lambda i: (i, 0)),
        out_shape=jax.ShapeDtypeStruct((T, GM_W), BF16),
        compiler_params=_params(("parallel",)),
    )(z_p, z_p, gv.reshape(1, GM_W), gout.reshape(1, GM_W), wc, bb)


def gmlp_bwd(tag, z_p, dmixed, gv, gout, wc, bb, tm=256):
    T = z_p.shape[0]
    nchunk = tm // CHUNK

    def body(u_ref, v_ref, dm_ref, gv_ref, go_ref, wc_ref, bb_ref, du_ref, dv_ref, dwc_ref, dbb_ref, dgv_ref, dgo_ref):
        i = pl.program_id(0)
        u, v = u_ref[...], v_ref[...]
        ug, vhat, rv, vn, gate = _gm_forward(u, v, gv_ref[...], wc_ref, bb_ref, nchunk)
        go = ug * gate
        ro = lax.rsqrt(jnp.mean(go * go, axis=-1, keepdims=True) + EPS)
        ohat = go * ro
        dm = dm_ref[...].astype(F32)
        dgo_part = jnp.sum(dm * ohat, axis=0, keepdims=True)
        doh = dm * go_ref[...]
        dgo = ro * (doh - ohat * jnp.mean(doh * ohat, axis=-1, keepdims=True))
        du_ref[...] = dgo * gate * _gelu_grad(u)
        dgate = dgo * ug
        dgb = dgate.astype(BF16)
        dvn_rows = []
        dwc_parts = []
        dbb_parts = []
        for gidx in range(GROUPS):
            cols = slice(gidx * 128, (gidx + 1) * 128)
            dw = jnp.zeros((CHUNK, CHUNK), F32)
            db = jnp.zeros((CHUNK, 128), F32)
            for cidx in range(nchunk):
                rows = slice(cidx * CHUNK, (cidx + 1) * CHUNK)
                dw = dw + lax.dot_general(dgb[rows, cols], vn[rows, cols], (((1,), (1,)), ((), ())),
                                          preferred_element_type=F32)
                db = db + dgate[rows, cols]
            dwc_parts.append(dw)
            dbb_parts.append(db)
        for cidx in range(nchunk):
            rows = slice(cidx * CHUNK, (cidx + 1) * CHUNK)
            dvn_rows.append(jnp.concatenate(
                [lax.dot_general(wc_ref[gidx], dgb[rows, gidx * 128:(gidx + 1) * 128], (((0,), (0,)), ((), ())),
                                 preferred_element_type=F32) for gidx in range(GROUPS)], axis=1))
        dvn = jnp.concatenate(dvn_rows, axis=0)
        dgv_part = jnp.sum(dvn * vhat, axis=0, keepdims=True)
        dvh = dvn * gv_ref[...]
        dvg = rv * (dvh - vhat * jnp.mean(dvh * vhat, axis=-1, keepdims=True))
        dv_ref[...] = dvg * _gelu_grad(v)

        @pl.when(i == 0)
        def _():
            for gidx in range(GROUPS):
                dwc_ref[gidx] = dwc_parts[gidx]
                dbb_ref[gidx] = dbb_parts[gidx]
            dgv_ref[...] = dgv_part
            dgo_ref[...] = dgo_part

        @pl.when(i > 0)
        def _():
            for gidx in range(GROUPS):
                dwc_ref[gidx] += dwc_parts[gidx]
                dbb_ref[gidx] += dbb_parts[gidx]
            dgv_ref[...] += dgv_part
            dgo_ref[...] += dgo_part

    vec = pl.BlockSpec((1, GM_W), lambda i: (0, 0))
    w3 = pl.BlockSpec((GROUPS, CHUNK, CHUNK), lambda i: (0, 0, 0))
    blk = pl.BlockSpec((tm, GM_W), lambda i: (i, 0))
    return pl.pallas_call(
        body, name=f"{tag}_gmlp_bwd", grid=(T // tm,),
        in_specs=[pl.BlockSpec((tm, GM_W), lambda i: (i, 1)), pl.BlockSpec((tm, GM_W), lambda i: (i, 2)),
                  pl.BlockSpec((tm, GM_W), lambda i: (i, 1)), vec, vec, w3, w3],
        out_specs=[blk, blk, w3, w3, vec, vec],
        out_shape=[jax.ShapeDtypeStruct((T, GM_W), F32)] * 2 + [jax.ShapeDtypeStruct((GROUPS, CHUNK, CHUNK), F32)] * 2
        + [jax.ShapeDtypeStruct((1, GM_W), F32)] * 2,
        compiler_params=_params(("arbitrary",)),
    )(z_p, z_p, dmixed, gv.reshape(1, GM_W), gout.reshape(1, GM_W), wc, bb)


def mixer_fwd(tag, h, w, tabs, wout_g, pre):
    T = h.shape[0]
    n2 = rms_fwd(f"{tag}_mix_rms", h, w["mix_norm"], D_MODEL)
    (z_p,) = mm_simple(f"{tag}_win", n2, lambda tk, tn: op_bt(w["w_in_pt"], tk, tn), T, IN_P, D_MODEL, 512, 1024, D_MODEL)
    cqn = rms_fwd(f"{tag}_cq_rms", z_p, w["q_a_norm"], Q_RANK, col_blk=0)
    ckvn = rms_fwd(f"{tag}_ckv_rms", z_p, w["kv_a_norm"], KV_RANK, col_blk=2)
    (q_raw,) = mm_simple(f"{tag}_wq", cqn, lambda tk, tn: op_b(w["wq_p"], tk, tn), T, 2048, Q_RANK, 512, 1024, Q_RANK)
    (kk_raw,) = mm_simple(f"{tag}_wk", ckvn, lambda tk, tn: op_b(w["wk_p"], tk, tn), T, 2048, KV_RANK, 512, 1024, KV_RANK)
    (vv,) = mm_simple(f"{tag}_wv", ckvn, lambda tk, tn: op_b(w["wv"], tk, tn), T, ATTN_W, KV_RANK, 512, 1024, KV_RANK,
                      out_dtype=BF16)
    q_full, k_full = qk_prep_fwd(tag, q_raw, kk_raw, z_p, w["gq_p"], w["gk_p"], tabs)
    a_out, lse = attn_fwd(tag, q_full, k_full, vv)
    mixed_a = rms_fwd(f"{tag}_ao_rms", a_out, w["attn_out_norm"], ATTN_W)
    mixed_g = gmlp_fwd(tag, z_p, w["gm_v_norm"], w["gm_out_norm"], w["wc"], w["bb"])
    tm, tn, tk = 512, 1024, 512
    (h2,) = matmul(
        f"{tag}_wout", (T // tm, D_MODEL // tn, ATTN_W // tk),
        [op_a(mixed_a, tm, tk), op_a(mixed_g, tm, tk)],
        [op_b_rows(wout_g, pre, tk, tn), op_b_rows(wout_g, pre, tk, tn, koff=ATTN_W // tk)],
        [(0, 0, 0), (1, 1, 0)], 1, [tile_mn(h, tm, tn)], [out_mn(T, D_MODEL, tm, tn, F32)],
        lambda accs, xs: (xs[0] + accs[0],), (tm, tn))
    res = dict(n2=n2, z_p=z_p, cqn=cqn, ckvn=ckvn, q_raw=q_raw, kk_raw=kk_raw, vv=vv, q_full=q_full, k_full=k_full,
               a_out=a_out, lse=lse, mixed_a=mixed_a, mixed_g=mixed_g)
    return h2, res


def mixer_bwd(tag, dh2, dh2_bf, h, w, tabs, wout_g, pre, r, after=None):
    T = h.shape[0]
    g = {}
    (dmixed,) = mm_simple(f"{tag}_dmixed", dh2_bf, lambda tk, tn: op_b_rows_t(wout_g, pre, tk, tn), T, D_MODEL, D_MODEL,
                          512, 512, D_MODEL, after=after)
    (dwo_a,) = mm_simple(f"{tag}_dwout_a", r["mixed_a"], lambda tk, tn: op_b(dh2_bf, tk, tn), ATTN_W, D_MODEL, T,
                         1024, 1024, 512, a_t=True, out_dtype=BF16)
    (dwo_g,) = mm_simple(f"{tag}_dwout_g", r["mixed_g"], lambda tk, tn: op_b(dh2_bf, tk, tn), GM_W, D_MODEL, T,
                         1024, 1024, 512, a_t=True, out_dtype=BF16)
    g["w_out"] = jnp.concatenate([dwo_a, dwo_g], axis=0)
    da_out, g["attn_out_norm"], delta = rms_bwd(f"{tag}_ao_rms_bwd", r["a_out"], w["attn_out_norm"], dmixed, ATTN_W,
                                                with_delta=True)
    dq_full, dk_full, dvv = attn_bwd(tag, r["q_full"], r["k_full"], r["vv"], da_out, r["lse"], delta)
    dq_raw, dkk_raw, dzkr, g["gq_p"], g["gk_p"] = qk_prep_bwd(tag, dq_full, dk_full, r["q_raw"], r["kk_raw"], r["z_p"],
                                                            w["gq_p"], w["gk_p"], tabs)
    (g["wq_p"],) = mm_simple(f"{tag}_dwq", r["cqn"], lambda tk, tn: op_b(dq_raw, tk, tn), Q_RANK, 2048, T, Q_RANK, 1024, 512,
                             a_t=True, out_dtype=BF16)
    (g["wk_p"],) = mm_simple(f"{tag}_dwk", r["ckvn"], lambda tk, tn: op_b(dkk_raw, tk, tn), KV_RANK, 2048, T, KV_RANK, 1024,
                             512, a_t=True, out_dtype=BF16)
    (g["wv"],) = mm_simple(f"{tag}_dwv", r["ckvn"], lambda tk, tn: op_b(dvv, tk, tn), KV_RANK, ATTN_W, T, KV_RANK, 1024, 512,
                           a_t=True, out_dtype=BF16)
    (dcqn,) = mm_simple(f"{tag}_dcqn", dq_raw, lambda tk, tn: op_bt(w["wq_p"], tk, tn), T, Q_RANK, 2048, 512, Q_RANK, 2048)
    (dck1,) = mm_simple(f"{tag}_dckvn_k", dkk_raw, lambda tk, tn: op_bt(w["wk_p"], tk, tn), T, KV_RANK, 2048, 512, KV_RANK,
                        2048)
    (dckvn,) = mm_simple(f"{tag}_dckvn_v", dvv, lambda tk, tn: op_bt(w["wv"], tk, tn), T, KV_RANK, ATTN_W, 512, KV_RANK,
                         ATTN_W, extras=[tile_mn(dck1, 512, KV_RANK)], epilogue=lambda accs, xs: (accs[0] + xs[0],))
    dc_q, g["q_a_norm"] = rms_bwd(f"{tag}_cq_rms_bwd", r["z_p"], w["q_a_norm"], dcqn, Q_RANK, col_blk=0)
    dc_kv, g["kv_a_norm"] = rms_bwd(f"{tag}_ckv_rms_bwd", r["z_p"], w["kv_a_norm"], dckvn, KV_RANK, col_blk=2)
    du, dv, g["wc"], g["bb"], g["gm_v_norm"], g["gm_out_norm"] = gmlp_bwd(
        tag, r["z_p"], dmixed, w["gm_v_norm"], w["gm_out_norm"], w["wc"], w["bb"])
    dz_p = jnp.concatenate([dc_q, dc_kv, dzkr, du, dv], axis=1).astype(BF16)
    (g["w_in_pt"],) = mm_simple(f"{tag}_dwin", dz_p, lambda tk, tn: op_b(r["n2"], tk, tn), IN_P, D_MODEL, T, 1024, 1024, 512,
                                a_t=True, out_dtype=BF16)
    (dn2,) = mm_simple(f"{tag}_dn2", dz_p, lambda tk, tn: op_b(w["w_in_pt"], tk, tn), T, D_MODEL, IN_P, 512, 1024, IN_P)
    dh1, g["mix_norm"], dh1_bf = rms_bwd(f"{tag}_mix_rms_bwd", h, w["mix_norm"], dn2, D_MODEL, dres=dh2, bf16_copy=True)
    return dh1, dh1_bf, g


def ple_fwd(tag, h3, p_l, w, wpg_g, wple_g, pre):
    T = h3.shape[0]
    (pw,) = mm_simple(f"{tag}_wple", p_l, lambda tk, tn: op_b_cols(wple_g, pre, tk, tn), T, D_MODEL, PLE_DIM, 512, 512,
                      PLE_DIM)
    e = rms_fwd(f"{tag}_ple_rms", pw, w["ple_norm"], D_MODEL, out_dtype=F32)
    n4 = rms_fwd(f"{tag}_pg_rms", h3, w["ple_gate_norm"], D_MODEL)

    def epi(accs, xs):
        gt = _sigmoid(accs[0])
        return xs[0] + gt * xs[1], gt

    tm, tn, tk = 512, 1024, 512
    h4, gate = matmul(
        f"{tag}_wpg", (T // tm, D_MODEL // tn, D_MODEL // tk),
        [op_a(n4, tm, tk)], [op_b_rows(wpg_g, pre, tk, tn)], [(0, 0, 0)], 1,
        [tile_mn(h3, tm, tn), tile_mn(e, tm, tn)],
        [out_mn(T, D_MODEL, tm, tn, F32), out_mn(T, D_MODEL, tm, tn, BF16)], epi, (tm, tn))
    return h4, dict(pw=pw, e=e, n4=n4, gate=gate)


def ple_bwd(tag, dh4, h3, p_l, w, wpg_g, wple_g, pre, r, tm=256):
    T = h3.shape[0]

    def act_body(d_ref, g_ref, e_ref, dpre_ref, de_ref):
        d, gt = d_ref[...], g_ref[...].astype(F32)
        dpre_ref[...] = (d * e_ref[...] * gt * (1.0 - gt)).astype(BF16)
        de_ref[...] = d * gt

    blk = pl.BlockSpec((tm, D_MODEL), lambda i: (i, 0))
    dpre, de = pl.pallas_call(
        act_body, name=f"{tag}_ple_act_bwd", grid=(T // tm,), in_specs=[blk, blk, blk], out_specs=[blk, blk],
        out_shape=[jax.ShapeDtypeStruct((T, D_MODEL), BF16), jax.ShapeDtypeStruct((T, D_MODEL), F32)],
        compiler_params=_params(("parallel",)),
    )(dh4, r["gate"], r["e"])
    g = {}
    (g["w_ple_gate"],) = mm_simple(f"{tag}_dwpg", r["n4"], lambda tk, tn: op_b(dpre, tk, tn), D_MODEL, D_MODEL, T,
                                   1024, 1024, 512, a_t=True, out_dtype=BF16)
    (dn4,) = mm_simple(f"{tag}_dn4", dpre, lambda tk, tn: op_b_rows_t(wpg_g, pre, tk, tn), T, D_MODEL, D_MODEL, 512, 512,
                       D_MODEL)
    dh3, g["ple_gate_norm"], dh3_bf = rms_bwd(f"{tag}_pg_rms_bwd", h3, w["ple_gate_norm"], dn4, D_MODEL, dres=dh4,
                                              bf16_copy=True)
    dpw, g["ple_norm"] = rms_bwd(f"{tag}_ple_rms_bwd", r["pw"], w["ple_norm"], de, D_MODEL)
    (g["w_ple"],) = mm_simple(f"{tag}_dwple", p_l, lambda tk, tn: op_b(dpw, tk, tn), PLE_DIM, D_MODEL, T, PLE_DIM, 512, 512,
                              a_t=True, outs=[out_cols(PLE_DIM, 512, PLE_DIM, 512, BF16)])
    return dh3, dh3_bf, g


def loss_grad(y, target, tm=256):
    T = y.shape[0]

    def body(y_ref, t_ref, dy_ref, l_ref):
        i = pl.program_id(0)
        d = y_ref[...] - t_ref[...]
        dy_ref[...] = d * (1.0 / D_MODEL)
        part = jnp.sum((d * d).reshape(tm // 8, 8, D_MODEL), axis=0)

        @pl.when(i == 0)
        def _():
            l_ref[...] = part

        @pl.when(i > 0)
        def _():
            l_ref[...] += part

    blk = pl.BlockSpec((tm, D_MODEL), lambda i: (i, 0))
    dy, part = pl.pallas_call(
        body, name="loss_grad", grid=(T // tm,), in_specs=[blk, blk],
        out_specs=[blk, pl.BlockSpec((8, D_MODEL), lambda i: (0, 0))],
        out_shape=[jax.ShapeDtypeStruct((T, D_MODEL), F32), jax.ShapeDtypeStruct((8, D_MODEL), F32)],
        compiler_params=_params(("arbitrary",)),
    )(y, target)
    return dy, 0.5 * jnp.sum(part) / D_MODEL


def _unshard_cols(g_l):
    return g_l.transpose(1, 0, 2).reshape(g_l.shape[1], -1)


def _shard_cols(w):
    return w.reshape(w.shape[0], N_CHIPS, -1).transpose(1, 0, 2)


def layer_weights(l, Gl, small):
    w = {k: small[k][l] for k in ("mix_norm", "q_a_norm", "kv_a_norm", "gm_v_norm", "attn_out_norm", "gm_out_norm",
                                  "ple_gate_norm", "ple_norm")}
    wint = Gl["w_in"][:, :IN_SHARD].reshape(-1, D_MODEL)
    z = lambda n: jnp.zeros((n, D_MODEL), BF16)
    w["w_in_pt"] = jnp.concatenate([wint[:768], z(128), wint[768:832], z(64), wint[832:]], axis=0)
    wuq = _unshard_cols(Gl["w_uq"]).reshape(Q_RANK, HEADS, QK_DIM)
    w["wq_p"] = jnp.pad(wuq, ((0, 0), (0, 0), (0, HEAD_PAD - QK_DIM))).reshape(Q_RANK, HEADS * HEAD_PAD)
    wukv = _unshard_cols(Gl["w_ukv"]).reshape(KV_RANK, HEADS, QK_NOPE + V_DIM)
    w["wk_p"] = jnp.pad(wukv[:, :, :QK_NOPE], ((0, 0), (0, 0), (0, HEAD_PAD - QK_NOPE))).reshape(KV_RANK, HEADS * HEAD_PAD)
    w["wv"] = wukv[:, :, QK_NOPE:].reshape(KV_RANK, ATTN_W)
    w["gq_p"] = jnp.pad(small["q_norm"][l], (0, HEAD_PAD - QK_DIM)).reshape(1, HEAD_PAD)
    w["gk_p"] = jnp.pad(small["k_norm"][l], (0, HEAD_PAD - QK_DIM)).reshape(1, HEAD_PAD)
    tril = jnp.tril(jnp.ones((CHUNK, CHUNK), dtype=bool))
    w["wc"] = jnp.where(tril[None], small["gm_ws"][l], 0.0).astype(BF16)
    w["bb"] = jnp.broadcast_to(small["gm_bs"][l][:, :, None], (GROUPS, CHUNK, 128)).astype(F32)
    return w


def mixer_grads_to_shards(g):
    out = {}
    dwint = g["w_in_pt"]
    dwint = jnp.concatenate([dwint[:768], dwint[896:960], dwint[1024:]], axis=0).reshape(N_CHIPS, IN_SHARD, D_MODEL)
    out["w_in"] = jnp.pad(dwint, ((0, 0), (0, IN_SHARD_PAD - IN_SHARD), (0, 0)))
    dwuq = g["wq_p"].reshape(Q_RANK, HEADS, HEAD_PAD)[:, :, :QK_DIM].reshape(Q_RANK, HEADS * QK_DIM)
    out["w_uq"] = _shard_cols(dwuq)
    dwukv = jnp.concatenate([g["wk_p"].reshape(KV_RANK, HEADS, HEAD_PAD)[:, :, :QK_NOPE],
                             g["wv"].reshape(KV_RANK, HEADS, V_DIM)], axis=-1).reshape(KV_RANK, HEADS * (QK_NOPE + V_DIM))
    out["w_ukv"] = _shard_cols(dwukv)
    out["w_out"] = g["w_out"].reshape(N_CHIPS, D_MODEL // N_CHIPS, D_MODEL)
    out["q_norm"] = g["gq_p"][0, :QK_DIM]
    out["k_norm"] = g["gk_p"][0, :QK_DIM]
    tril = jnp.tril(jnp.ones((CHUNK, CHUNK), dtype=bool))
    out["gm_ws"] = jnp.where(tril[None], g["wc"], 0.0)
    out["gm_bs"] = jnp.sum(g["bb"], axis=-1)
    for k in ("mix_norm", "q_a_norm", "kv_a_norm", "gm_v_norm", "attn_out_norm", "gm_out_norm"):
        out[k] = g[k][0]
    return out


def layer_fwd(l, h, p_l, Gl, small, tabs, before=None):
    before = before or {}
    h1, r_a = ffn_fwd(f"l{l}a", h, small["ffn_a_norm"][l], Gl["ffn_a_w1"], Gl["ffn_a_w3"], Gl.get("ffn_a_w2"), (),
                      before.get("down_a"))
    if "mixer" in before:
        Gl, small = before["mixer"](h1, Gl, small)
    w = layer_weights(l, Gl, small)
    h2, r_m = mixer_fwd(f"l{l}", h1, w, tabs, Gl["w_out"], ())
    if "ffn_b" in before:
        Gl = before["ffn_b"](h2, Gl)
    h3, r_b = ffn_fwd(f"l{l}b", h2, small["ffn_b_norm"][l], Gl["ffn_b_w1"], Gl["ffn_b_w3"], Gl["ffn_b_w2"], ())
    if "ple" in before:
        w = {**w, "ple_norm": w["ple_norm"] + before["ple"](h3)[0, 0]}
    h4, r_p = ple_fwd(f"l{l}", h3, p_l, w, Gl["w_ple_gate"], Gl["w_ple"], ())
    return h4, (w, h, h1, h2, h3, r_a, r_m, r_b, r_p)


def layer_bwd(l, dh, p_l, Gl, small, tabs, saved, before=None):
    w, h0, h1, h2, h3, r_a, r_m, r_b, r_p = saved
    slabs = lambda d: d.reshape(N_CHIPS, FF_PAD, D_MODEL)
    hook = lambda block: before[block](gl, dh) if before and block in before else None
    gl = {}
    dh, dh_bf, g_p = ple_bwd(f"l{l}", dh, h3, p_l, w, Gl["w_ple_gate"], Gl["w_ple"], (), r_p)
    gl["w_ple_gate"] = g_p["w_ple_gate"].reshape(N_CHIPS, D_MODEL // N_CHIPS, D_MODEL)
    gl["w_ple"] = g_p["w_ple"]
    gl["ple_gate_norm"], gl["ple_norm"] = g_p["ple_gate_norm"][0], g_p["ple_norm"][0]
    dh, dh_bf, dg, dw1, dw3, dw2 = ffn_bwd(f"l{l}b", dh, dh_bf, h2, small["ffn_b_norm"][l], r_b,
                                           Gl["ffn_b_w1"], Gl["ffn_b_w3"], Gl["ffn_b_w2"], (), hook("ffn_b"))
    gl["ffn_b_norm"] = dg[0]
    gl["ffn_b_w1"], gl["ffn_b_w3"], gl["ffn_b_w2"] = slabs(dw1), slabs(dw3), slabs(dw2)
    dh, dh_bf, g_m = mixer_bwd(f"l{l}", dh, dh_bf, h1, w, tabs, Gl["w_out"], (), r_m, hook("mixer"))
    gl.update(mixer_grads_to_shards(g_m))
    last_dw = (lambda dh_: before["ffn_a_dw"](gl, dh_)) if before and "ffn_a_dw" in before else None
    dh, _, dg, dw1, dw3, dw2 = ffn_bwd(f"l{l}a", dh, dh_bf, h0, small["ffn_a_norm"][l], r_a,
                                       Gl["ffn_a_w1"], Gl["ffn_a_w3"], Gl["ffn_a_w2"], (), hook("ffn_a"), last_dw)
    gl["ffn_a_norm"] = dg[0]
    gl["ffn_a_w1"], gl["ffn_a_w3"], gl["ffn_a_w2"] = slabs(dw1), slabs(dw3), slabs(dw2)
    return dh, gl


MESH = pl.DeviceIdType.MESH
HBM_SPEC = pl.BlockSpec(memory_space=pltpu.HBM)


def _place():
    x, y, c = lax.axis_index("x"), lax.axis_index("y"), lax.axis_index("c")
    others = [(1 - x, y), (x, 1 - y), (1 - x, 1 - y)]
    return x, y, c, 2 * x + y, others


def prep_shard(name, w, layer, rows_pad, place, after=None):
    _, ks, n = w.shape
    ksp = ks + rows_pad
    tc = 512 if n % 512 == 0 else n
    deps = [] if after is None else [after]

    def body(place_ref, x_ref, *rest):
        o_ref = rest[-1]
        o_ref[:ks] = x_ref[...].astype(BF16)
        if rows_pad:
            o_ref[ks:] = jnp.zeros((rows_pad, tc), BF16)

    return pl.pallas_call(
        body, name=name,
        grid_spec=pltpu.PrefetchScalarGridSpec(
            num_scalar_prefetch=1, grid=(n // tc,),
            in_specs=[pl.BlockSpec((None, ks, tc), lambda i, s: (layer, 0, i))] + [ANY_SPEC] * len(deps),
            out_specs=pl.BlockSpec((None, ksp, tc), lambda i, s: (s[0], 0, i))),
        out_shape=jax.ShapeDtypeStruct((N_CHIPS, ksp, n), BF16),
        compiler_params=_params(("parallel",)),
    )(place, w, *deps)


SEM_SPEC = pl.BlockSpec(memory_space=pltpu.SEMAPHORE)
ANY_SPEC = pl.BlockSpec(memory_space=pl.ANY)
DATAFLOW = pltpu.SideEffectType.DATAFLOW_SIDE_EFFECTING


def _hbm(x):
    return pltpu.with_memory_space_constraint(x, pltpu.HBM)


def _start_call(name, slots, after, issue):
    n = len(slots)
    deps = [] if after is None else [after]
    nd = len(deps)

    def body(*refs):
        issue(refs[n + nd + 2:2 * n + nd + 2], refs[n + nd], refs[n + nd + 1])
        token = refs[2 * n + nd + 2]
        token[...] = jnp.zeros_like(token)

    outs = pl.pallas_call(
        body, name=name,
        in_specs=[HBM_SPEC] * n + [ANY_SPEC] * nd,
        out_specs=(SEM_SPEC, SEM_SPEC, *([HBM_SPEC] * n), pl.BlockSpec(memory_space=pltpu.VMEM)),
        out_shape=(pltpu.SemaphoreType.DMA((n,)), pltpu.SemaphoreType.DMA((n,)),
                   *[pltpu.HBM(s.shape, s.dtype) for s in slots], jax.ShapeDtypeStruct((8, 128), F32)),
        input_output_aliases={w: w + 2 for w in range(n)},
        compiler_params=pltpu.CompilerParams(has_side_effects=DATAFLOW),
    )(*[_hbm(s) for s in slots], *deps)
    return outs[0], outs[1], list(outs[2:2 + n]), outs[2 + n]


def gather_start(name, slots, after):
    def issue(g_refs, send, recv):
        x, y, c, jme, others = _place()
        for w in range(len(slots)):
            kh = slots[w].shape[1] // 2
            mine = g_refs[w].at[jme, pl.ds(c * kh, kh)]
            for (px, py) in others:
                pltpu.make_async_remote_copy(src_ref=mine, dst_ref=mine, send_sem=send.at[w], recv_sem=recv.at[w],
                                             device_id=(px, py, c), device_id_type=MESH).start()

    return _start_call(name, slots, after, issue)


def forward_start(name, slots):
    def issue(g_refs, send, recv):
        x, y, c, _, others = _place()
        for w in range(len(slots)):
            kh = slots[w].shape[1] // 2
            for (px, py) in others:
                blk = g_refs[w].at[2 * px + py, pl.ds(c * kh, kh)]
                pltpu.make_async_remote_copy(src_ref=blk, dst_ref=blk, send_sem=send.at[w], recv_sem=recv.at[w],
                                             device_id=(x, y, 1 - c), device_id_type=MESH).start()

    return _start_call(name, slots, None, issue)


def share_start(name, fulls):
    def issue(o_refs, send, recv):
        x, y, c, _, _ = _place()
        for w in range(len(fulls)):
            kh = fulls[w].shape[0] // 2
            half = o_refs[w].at[pl.ds(c * kh, kh)]
            pltpu.make_async_remote_copy(src_ref=half, dst_ref=half, send_sem=send.at[w], recv_sem=recv.at[w],
                                         device_id=(x, y, 1 - c), device_id_type=MESH).start()

    return _start_call(name, fulls, None, issue)


def share_wait(name, send, recv, flying, after):
    return _wait_call(name, send, recv, flying, after, lambda r: r.at[pl.ds(0, r.shape[0] // 2)])


def gather_wait(name, send, recv, flying, after):
    return _wait_call(name, send, recv, flying, after, lambda r: r.at[pl.ds(0, 3), pl.ds(0, r.shape[1] // 2)])


def _wait_call(name, send, recv, flying, after, landed):
    n = len(flying)

    def body(*refs):
        send_ref, recv_ref = refs[n], refs[n + 1]
        g_refs = refs[n + 3:]
        x, y, c, _, _ = _place()
        for w in range(n):
            cp = pltpu.make_async_remote_copy(src_ref=landed(g_refs[w]), dst_ref=landed(g_refs[w]),
                                              send_sem=send_ref.at[w], recv_sem=recv_ref.at[w],
                                              device_id=(x, y, 1 - c), device_id_type=MESH)
            cp.wait_send()
            cp.wait_recv()

    return pl.pallas_call(
        body, name=name,
        in_specs=[HBM_SPEC] * n + [SEM_SPEC, SEM_SPEC, ANY_SPEC],
        out_specs=[HBM_SPEC] * n,
        out_shape=[pltpu.HBM(s.shape, s.dtype) for s in flying],
        input_output_aliases={w: w for w in range(n)},
        compiler_params=pltpu.CompilerParams(has_side_effects=DATAFLOW),
    )(*flying, send, recv, after)


def _send_start(name, srcs, land_shapes, issue, after):
    n = len(srcs)
    deps = [] if after is None else [after]
    nd = len(deps)

    def body(*refs):
        base = 2 * n + nd
        issue(refs[base + 2:base + 2 + n], refs[base + 2 + n:base + 2 + 2 * n], refs[base], refs[base + 1])
        token = refs[base + 2 + 2 * n]
        token[...] = jnp.zeros_like(token)

    lands = [_hbm(lax.empty(shape, s.dtype)) for shape, s in zip(land_shapes, srcs)]
    outs = pl.pallas_call(
        body, name=name,
        in_specs=[HBM_SPEC] * (2 * n) + [ANY_SPEC] * nd,
        out_specs=(SEM_SPEC, SEM_SPEC, *([HBM_SPEC] * (2 * n)), pl.BlockSpec(memory_space=pltpu.VMEM)),
        out_shape=(pltpu.SemaphoreType.DMA((n,)), pltpu.SemaphoreType.DMA((n,)),
                   *[pltpu.HBM(s.shape, s.dtype) for s in srcs], *[pltpu.HBM(l.shape, l.dtype) for l in lands],
                   jax.ShapeDtypeStruct((8, 128), F32)),
        input_output_aliases={w: w + 2 for w in range(2 * n)},
        compiler_params=pltpu.CompilerParams(has_side_effects=DATAFLOW),
    )(*[_hbm(s) for s in srcs], *lands, *deps)
    return outs[0], outs[1], list(outs[2:2 + n]), list(outs[2 + n:2 + 2 * n]), outs[2 + 2 * n]


def _send_wait(name, send, recv, srcs, lands, after, landed):
    n = len(srcs)

    def body(*refs):
        send_ref, recv_ref = refs[2 * n], refs[2 * n + 1]
        q_refs = refs[3 * n + 3:]
        x, y, c, _, _ = _place()
        for w in range(n):
            cp = pltpu.make_async_remote_copy(src_ref=landed(q_refs[w]), dst_ref=landed(q_refs[w]), send_sem=send_ref.at[w],
                                              recv_sem=recv_ref.at[w], device_id=(x, y, 1 - c), device_id_type=MESH)
            cp.wait_send()
            cp.wait_recv()

    outs = pl.pallas_call(
        body, name=name,
        in_specs=[HBM_SPEC] * (2 * n) + [SEM_SPEC, SEM_SPEC, ANY_SPEC],
        out_specs=[HBM_SPEC] * (2 * n),
        out_shape=[pltpu.HBM(a.shape, a.dtype) for a in list(srcs) + list(lands)],
        input_output_aliases={w: w for w in range(2 * n)},
        compiler_params=pltpu.CompilerParams(has_side_effects=DATAFLOW),
    )(*srcs, *lands, send, recv, after)
    return list(outs[:n]), list(outs[n:])


def exchange_start(name, grads, after):
    def issue(d_refs, r_refs, send, recv):
        x, y, c, _, _ = _place()
        for w in range(len(grads)):
            half = grads[w].shape[1] // 2
            pltpu.make_async_remote_copy(
                src_ref=d_refs[w].at[pl.ds(0, N_CHIPS), pl.ds((1 - c) * half, half)], dst_ref=r_refs[w],
                send_sem=send.at[w], recv_sem=recv.at[w], device_id=(x, y, 1 - c), device_id_type=MESH).start()

    return _send_start(name, grads, [(N_CHIPS, g.shape[1] // 2, g.shape[2]) for g in grads], issue, after)


def exchange_wait(name, send, recv, grads, lands, after):
    return _send_wait(name, send, recv, grads, lands, after, lambda r: r)


def scatter_start(name, parts):
    def issue(p_refs, q_refs, send, recv):
        x, y, c, jme, others = _place()
        for w in range(len(parts)):
            for (px, py) in others:
                pltpu.make_async_remote_copy(
                    src_ref=p_refs[w].at[2 * px + py], dst_ref=q_refs[w].at[jme], send_sem=send.at[w], recv_sem=recv.at[w],
                    device_id=(px, py, c), device_id_type=MESH).start()

    return _send_start(name, parts, [p.shape for p in parts], issue, None)


def scatter_wait(name, send, recv, parts, lands, after):
    return _send_wait(name, send, recv, parts, lands, after, lambda r: r.at[pl.ds(0, 3)])


def allreduce_small(v):
    R = v.shape[0]

    def body(v_ref, o_ref, sib_ref, mine_ref, all_ref, d_send, d_recv, i_send, i_recv):
        x, y, c, jme, others = _place()
        swap = pltpu.make_async_remote_copy(src_ref=v_ref, dst_ref=sib_ref, send_sem=d_send, recv_sem=d_recv,
                                            device_id=(x, y, 1 - c), device_id_type=MESH)
        swap.start()
        swap.wait()
        mine_ref[...] = v_ref[...] + sib_ref[...]
        for (px, py) in others:
            pltpu.make_async_remote_copy(src_ref=mine_ref, dst_ref=all_ref.at[jme], send_sem=i_send, recv_sem=i_recv,
                                         device_id=(px, py, c), device_id_type=MESH).start()
        three = all_ref.at[pl.ds(0, 3)]
        wait3 = pltpu.make_async_remote_copy(src_ref=three, dst_ref=three, send_sem=i_send, recv_sem=i_recv,
                                             device_id=(x, y, c), device_id_type=MESH)
        wait3.wait_recv()
        wait3.wait_send()
        all_ref[jme] = mine_ref[...]
        o_ref[...] = ((all_ref[0] + all_ref[1]) + all_ref[2]) + all_ref[3]

    vm = pl.BlockSpec(memory_space=pltpu.VMEM)
    return pl.pallas_call(
        body, name="allreduce_small", in_specs=[vm], out_specs=vm,
        out_shape=jax.ShapeDtypeStruct(v.shape, F32),
        scratch_shapes=[pltpu.VMEM((R, 128), F32), pltpu.VMEM((R, 128), F32), pltpu.VMEM((N_CHIPS, R, 128), F32),
                        pltpu.SemaphoreType.DMA, pltpu.SemaphoreType.DMA, pltpu.SemaphoreType.DMA, pltpu.SemaphoreType.DMA],
        compiler_params=pltpu.CompilerParams(vmem_limit_bytes=VMEM_LIMIT_BYTES),
    )(v)


def _row_tile(rows, width, mult=16, cap=3 << 20):
    best = rows
    for t in range(mult, rows + 1, mult):
        if rows % t == 0 and t * width * 4 <= cap:
            best = t
    return best


def add_sibling(name, mine, theirs, place):
    _, kh, ns = theirs.shape
    tr = _row_tile(kh, ns)
    nblk = kh // tr

    def body(place_ref, a_ref, b_ref, o_ref):
        o_ref[...] = (a_ref[...].astype(F32) + b_ref[...].astype(F32)).astype(BF16)

    return pl.pallas_call(
        body, name=name,
        grid_spec=pltpu.PrefetchScalarGridSpec(
            num_scalar_prefetch=1, grid=(N_CHIPS, nblk),
            in_specs=[pl.BlockSpec((None, tr, ns), lambda j, i, s: (j, s[1] * nblk + i, 0)),
                      pl.BlockSpec((None, tr, ns), lambda j, i, s: (j, i, 0))],
            out_specs=pl.BlockSpec((None, tr, ns), lambda j, i, s: (j, i, 0))),
        out_shape=jax.ShapeDtypeStruct(theirs.shape, BF16),
        compiler_params=_params(("parallel", "parallel")),
    )(place, mine, theirs)


def add_chips(name, q, p, place):
    _, kh, ns = q.shape
    tr = _row_tile(kh, ns)
    nblk = kh // tr

    def body(place_ref, *refs):
        q_refs, own_ref, o_ref = refs[:N_CHIPS], refs[N_CHIPS], refs[-1]
        jme = place_ref[0]
        tot = None
        for j in range(N_CHIPS):
            v = jnp.where(jme == j, own_ref[...], q_refs[j][...]).astype(F32)
            tot = v if tot is None else tot + v
        o_ref[...] = tot

    def q_ix(j):
        return lambda i, s: (jnp.where(s[0] == j, (j + 1) % N_CHIPS, j), i, 0)

    in_specs = [pl.BlockSpec((None, tr, ns), q_ix(j)) for j in range(N_CHIPS)]
    in_specs.append(pl.BlockSpec((None, tr, ns), lambda i, s: (s[0], i, 0)))
    return pl.pallas_call(
        body, name=name,
        grid_spec=pltpu.PrefetchScalarGridSpec(
            num_scalar_prefetch=1, grid=(nblk,), in_specs=in_specs,
            out_specs=pl.BlockSpec((tr, ns), lambda i, s: (s[1] * nblk + i, 0))),
        out_shape=jax.ShapeDtypeStruct((2 * kh, ns), F32),
        compiler_params=_params(("parallel",)),
    )(place, q, q, q, q, p)


ADAM_LR, ADAM_B1, ADAM_B2, ADAM_EPS, ADAM_WD, ADAM_STEP = 0.001, 0.9, 0.999, 1e-08, 0.01, 10


def adamw(name, w, g, m, v, layer, prev=None, after=None):
    _, k, ns = w.shape
    nsp = g.shape[1]
    tr = _row_tile(k, nsp, mult=8, cap=3 << 20)

    def body(w_ref, g_ref, m_ref, v_ref, *rest):
        go_ref, d_ref, mo_ref, vo_ref = rest[-4:]
        gv = g_ref[:, :ns] if nsp != ns else g_ref[...]
        mn = ADAM_B1 * m_ref[...] + (1.0 - ADAM_B1) * gv
        vn = ADAM_B2 * v_ref[...] + (1.0 - ADAM_B2) * (gv * gv)
        m_hat = mn / (1.0 - ADAM_B1 ** ADAM_STEP)
        v_hat = vn / (1.0 - ADAM_B2 ** ADAM_STEP)
        go_ref[...] = gv
        d_ref[...] = -ADAM_LR * (m_hat / (jnp.sqrt(v_hat) + ADAM_EPS) + ADAM_WD * w_ref[...])
        mo_ref[...] = mn
        vo_ref[...] = vn

    blk = pl.BlockSpec((None, tr, ns), lambda i: (layer, i, 0))
    gblk = pl.BlockSpec((tr, nsp), lambda i: (i, 0))
    args, in_specs, aliases = [w, g, m, v], [blk, gblk, blk, blk], {}
    if prev is not None:
        args += list(prev)
        in_specs += [pl.BlockSpec(memory_space=pl.ANY)] * 4
        aliases = {4 + i: i for i in range(4)}
    if after is not None:
        args.append(after)
        in_specs.append(pl.BlockSpec(memory_space=pl.ANY))
    return pl.pallas_call(
        body, name=name, grid=(k // tr,), in_specs=in_specs, out_specs=[blk] * 4,
        out_shape=[jax.ShapeDtypeStruct(w.shape, F32)] * 4, input_output_aliases=aliases,
        compiler_params=_params(("parallel",)),
    )(*args)


WEIGHTS = ("ffn_a_norm", "ffn_a_w1", "ffn_a_w3", "ffn_a_w2", "mix_norm", "w_in", "q_a_norm", "w_uq", "kv_a_norm", "w_ukv",
           "q_norm", "k_norm", "gm_v_norm", "gm_ws", "gm_bs", "attn_out_norm", "gm_out_norm", "w_out", "ffn_b_norm",
           "ffn_b_w1", "ffn_b_w3", "ffn_b_w2", "ple_gate_norm", "w_ple_gate", "w_ple", "ple_norm")
_FF = FF_PAD - FF_SHARD
BIG = {"ffn_a_w1": _FF, "ffn_a_w3": _FF, "ffn_a_w2": _FF, "ffn_b_w1": _FF, "ffn_b_w3": _FF, "ffn_b_w2": _FF,
       "w_in": IN_SHARD_PAD - IN_SHARD, "w_uq": 0, "w_ukv": 0, "w_ple": 0, "w_out": 0, "w_ple_gate": 0}
TRANSPOSED = ("ffn_a_w1", "ffn_a_w3", "ffn_b_w1", "ffn_b_w3", "w_in")
SMALL = tuple(n for n in WEIGHTS if n not in BIG)
PACK = 1024


def _pack_small(d):
    parts = []
    for n in SMALL:
        flat = d[n].reshape(-1)
        parts.append(jnp.pad(flat, (0, (-flat.shape[0]) % PACK)))
    return jnp.concatenate(parts).reshape(-1, 128)


def _unpack_small(buf, like):
    flat = buf.reshape(-1)
    out, pos = {}, 0
    for n in SMALL:
        size = math.prod(like[n].shape)
        out[n] = flat[pos:pos + size].reshape(like[n].shape)
        pos += size + (-size) % PACK
    return out


def kernel(*args):
    names = (("x", "p", "positions") + WEIGHTS + ("loss_target",) + tuple("m_" + n for n in WEIGHTS)
             + tuple("v_" + n for n in WEIGHTS))
    a = dict(zip(names, args, strict=True))
    x, p, positions, target = a["x"][0], a["p"][:, 0], a["positions"][0], a["loss_target"][0]
    for n in TRANSPOSED:
        for pre in ("", "m_", "v_"):
            a[pre + n] = jnp.swapaxes(a[pre + n], 1, 2)

    place = jnp.stack([2 * lax.axis_index("x") + lax.axis_index("y"), lax.axis_index("c")]).astype(jnp.int32)
    small = {n: a[n] for n in SMALL}
    tabs = rope_tables(positions)
    order = {"l0a": ("ffn_a_w1", "ffn_a_w3"), "l0b": ("ffn_a_w2",), "l0c": ("w_in", "w_uq", "w_ukv", "w_out"),
             "l0d": ("ffn_b_w1", "ffn_b_w3", "ffn_b_w2", "w_ple_gate", "w_ple")}
    prep = lambda n, l, after: prep_shard(f"prep_{n}_{l}", a[n], l, BIG[n], place, after)
    flights, token = {}, None
    for tag, names in order.items():
        flights[tag] = gather_start(f"gather_{tag}_start", [prep(n, 0, token) for n in names], None)
        token = flights[tag][3]
    slots1 = []
    for n in BIG:
        slots1.append(prep(n, 1, slots1[-1] if slots1 else token))

    def arrive(tag, after):
        send, recv, flying, _ = flights[tag]
        arrived = gather_wait(f"gather_{tag}_wait", send, recv, flying, after)
        send, recv, flying, token = forward_start(f"forward_{tag}_start", arrived)
        return dict(zip(order[tag], gather_wait(f"forward_{tag}_wait", send, recv, flying, token)))

    G0 = arrive("l0a", slots1[-1])

    def before_down(s):
        G0.update(arrive("l0b", s))
        return G0["ffn_a_w2"]

    def before_mixer(h1, Gl, small_):
        G0.update(arrive("l0c", h1))
        flights["l1"] = gather_start("gather_l1_start", slots1, G0["w_uq"])
        return G0, {**small_, "mix_norm": small_["mix_norm"] + flights["l1"][3][0, 0]}

    def before_ffn_b(h2, Gl):
        G0.update(arrive("l0d", h2))
        return G0

    def before_ple(h3):
        send, recv, flying, _ = flights["l1"]
        flights["f1"] = forward_start("forward_l1_start", gather_wait("gather_l1_wait", send, recv, flying, h3))
        return flights["f1"][3]

    h, saved0 = layer_fwd(0, x, p[0], G0, small, tabs,
                          {"down_a": before_down, "mixer": before_mixer, "ffn_b": before_ffn_b, "ple": before_ple})
    G1 = dict(zip(BIG, gather_wait("forward_l1_wait", *flights["f1"][:3], h)))
    h, saved1 = layer_fwd(1, h, p[1], G1, small, tabs)
    dh, loss = loss_grad(h, target)
    loss = lax.psum(loss, ("x", "y", "c"))

    groups = {"l1": tuple(BIG),
              "l0a": ("w_ple_gate", "w_ple", "ffn_b_w1", "ffn_b_w3", "ffn_b_w2"),
              "l0b": ("w_in", "w_uq", "w_ukv", "w_out"),
              "l0c": ("ffn_a_w1", "ffn_a_w3", "ffn_a_w2")}
    crossing, started = [], {}

    def begin(tag, gl, after):
        ex = exchange_start(f"exchange_{tag}_start", [gl[n] for n in groups[tag]], after)
        crossing.append((tag, ex))
        return ex[4]

    def advance(after):
        tag, (send, recv, mine, lands, _) = crossing.pop()
        mine, theirs = exchange_wait(f"exchange_{tag}_wait", send, recv, mine, lands, after)
        parts = [add_sibling(f"add_sibling_{n}_{tag}", d, r, place) for n, d, r in zip(groups[tag], mine, theirs)]
        started[tag] = scatter_start(f"scatter_{tag}_start", parts)
        return started[tag][4]

    def sum_chips(tag, after):
        send, recv, parts, lands, _ = started[tag]
        parts, slabs = scatter_wait(f"scatter_{tag}_wait", send, recv, parts, lands, after)
        halves = [add_chips(f"add_chips_{n}_{tag}", q, pt, place) for n, q, pt in zip(groups[tag], slabs, parts)]
        return share_start(f"share_{tag}_start", halves)

    def shared(tag, sharing, after):
        send, recv, flying, _ = sharing
        return dict(zip(groups[tag], share_wait(f"share_{tag}_wait", send, recv, flying, after)))

    def update(names, full, layer, prev, after):
        outs = {}
        for n in names:
            outs[n] = adamw(f"adamw_{n}_{layer}", a[n], full[n], a["m_" + n], a["v_" + n], layer, prev and prev[n], after)
            after = outs[n][1]
        return outs, after

    grads = [None, None]
    dh, grads[1] = layer_bwd(1, dh, p[1], G1, small, tabs, saved1)
    token = begin("l1", grads[1], None)
    w0 = {**saved0[0], "ple_gate_norm": saved0[0]["ple_gate_norm"] + token[0, 0]}
    hooks = {"ffn_b": lambda gl, dh_: advance(dh_),
             "mixer": lambda gl, dh_: begin("l0a", gl, None),
             "ffn_a": lambda gl, dh_: begin("l0b", gl, advance(dh_)),
             "ffn_a_dw": lambda gl, dh_: advance(dh_)}
    gx, grads[0] = layer_bwd(0, dh, p[0], G0, small, tabs, (w0,) + saved0[1:], hooks)
    token = begin("l0c", grads[0], None)
    sharing = sum_chips("l1", token)
    full1 = shared("l1", sharing, advance(sharing[3]))
    outs1, behind = update(BIG, full1, 1, None, None)
    sharing_a = sum_chips("l0a", behind)
    sharing_b = sum_chips("l0b", sharing_a[3])
    full0 = shared("l0a", sharing_a, sharing_b[3])
    outs0, behind = update(groups["l0a"], full0, 0, outs1, None)
    sharing_c = sum_chips("l0c", behind)
    full0.update(shared("l0b", sharing_b, sharing_c[3]))
    outs, behind = update(groups["l0b"], full0, 0, outs1, None)
    outs0.update(outs)
    full0.update(shared("l0c", sharing_c, behind))
    outs0.update(update(groups["l0c"], full0, 0, outs1, None)[0])

    out_g, out_d, out_m, out_v = {}, {}, {}, {}
    for n in BIG:
        outs = [jnp.swapaxes(o, 1, 2) for o in outs0[n]] if n in TRANSPOSED else outs0[n]
        out_g[n], out_d[n], out_m[n], out_v[n] = outs

    gs = allreduce_small(_pack_small({n: jnp.stack([grads[0][n], grads[1][n]]) for n in SMALL}))
    rows = gs.shape[0] // 2
    packed = [_pack_small(d).reshape(2, rows, 128) for d in
              (small, {n: a["m_" + n] for n in SMALL}, {n: a["v_" + n] for n in SMALL})]
    gs = gs.reshape(2, rows, 128)
    sm = adamw("adamw_small_0", packed[0], gs[0], packed[1], packed[2], 0)
    sm = adamw("adamw_small_1", packed[0], gs[1], packed[1], packed[2], 1, sm)
    for dst, buf in zip((out_g, out_d, out_m, out_v), sm):
        dst.update(_unpack_small(buf, small))

    return (loss, gx[None], *[out_g[n] for n in WEIGHTS], *[out_d[n] for n in WEIGHTS],
            *[out_m[n] for n in WEIGHTS], *[out_v[n] for n in WEIGHTS])
```

```python
import math

import jax
import jax.numpy as jnp
from jax import lax
from jax.experimental import pallas as pl
from jax.experimental.pallas import tpu as pltpu

F32 = jnp.float32
BF16 = jnp.bfloat16

D_MODEL = 2048
D_FF = 5504
N_CHIPS = 4
FF_SHARD = D_FF // N_CHIPS
FF_PAD = 1408
FF_P = N_CHIPS * FF_PAD
HEADS = 8
QK_NOPE = 128
QK_ROPE = 64
QK_DIM = 192
HEAD_PAD = 256
V_DIM = 128
Q_RANK = 512
KV_RANK = 256
ATTN_W = 1024
GM_W = 1024
GROUPS = 8
CHUNK = 128
PLE_DIM = 256
IN_P = 3072
IN_SHARD = 720
IN_SHARD_PAD = 736
EPS = 1e-6
ROPE_BASE = 10000.0
ATTN_SCALE = QK_DIM ** -0.5
VMEM_LIMIT_BYTES = 56 * 1024 * 1024


def _params(sem):
    return pltpu.CompilerParams(dimension_semantics=sem, vmem_limit_bytes=VMEM_LIMIT_BYTES)


def _bf(x):
    return x if x.dtype == BF16 else x.astype(BF16)


def _sigmoid(x):
    return 1.0 / (1.0 + jnp.exp(-x))


_GELU_C = math.sqrt(2.0 / math.pi)


def _gelu(x):
    t = jnp.tanh(_GELU_C * (x + 0.044715 * x * x * x))
    return 0.5 * x * (1.0 + t)


def _gelu_grad(x):
    t = jnp.tanh(_GELU_C * (x + 0.044715 * x * x * x))
    return 0.5 * (1.0 + t) + 0.5 * x * (1.0 - t * t) * _GELU_C * (1.0 + 3 * 0.044715 * x * x)


def op_a(a, tm, tk):
    return (a, (tm, tk), lambda i, j, k: (i, k), 1)


def op_at(a, tm, tk):
    return (a, (tk, tm), lambda i, j, k: (k, i), 0)


def op_b(b, tk, tn):
    return (b, (tk, tn), lambda i, j, k: (k, j), 0)


def op_bt(b, tk, tn):
    return (b, (tn, tk), lambda i, j, k: (j, k), 1)


def op_b_cols(g, pre, tk, tn):
    nb = g.shape[-1] // tn
    none = (None,) * (1 + len(pre))
    return (g, none + (tk, tn), lambda i, j, k: (j // nb,) + tuple(pre) + (k, j % nb), 0)


def op_b_rows(g, pre, tk, tn, koff=0):
    nb = g.shape[-2] // tk
    none = (None,) * (1 + len(pre))
    return (g, none + (tk, tn), lambda i, j, k: ((k + koff) // nb,) + tuple(pre) + ((k + koff) % nb, j), 0)


def op_b_rows_t(g, pre, tk, tn):
    nb = g.shape[-2] // tn
    none = (None,) * (1 + len(pre))
    return (g, none + (tn, tk), lambda i, j, k: (j // nb,) + tuple(pre) + (j % nb, k), 1)


def tile_mn(x, tm, tn):
    return (x, (tm, tn), lambda i, j: (i, j))


def out_mn(M, N, tm, tn, dtype):
    return (jax.ShapeDtypeStruct((M, N), dtype), (tm, tn), lambda i, j: (i, j))


def out_cols(M, ns, tm, tn, dtype):
    nb = ns // tn
    return (jax.ShapeDtypeStruct((N_CHIPS, M, ns), dtype), (None, tm, tn), lambda i, j: (j // nb, i, j % nb))


def matmul(name, grid_mnk, a_ops, b_ops, terms, n_acc, extras, outs, epilogue, acc_tile, n_outer=False, after=None):
    gm, gn, gk = grid_mnk
    na, nb, nx, no = len(a_ops), len(b_ops), len(extras), len(outs)
    nd = 0 if after is None else 1

    def body(*refs):
        a_refs, b_refs = refs[:na], refs[na:na + nb]
        x_refs = refs[na + nb:na + nb + nx]
        o_refs = refs[na + nb + nx + nd:na + nb + nx + nd + no]
        acc_refs = refs[na + nb + nx + nd + no:]
        k = pl.program_id(2)

        @pl.when(k == 0)
        def _():
            for acc in acc_refs:
                acc[...] = jnp.zeros_like(acc)

        for ai, bi, ci in terms:
            dims = (((a_ops[ai][3],), (b_ops[bi][3],)), ((), ()))
            acc_refs[ci][...] += lax.dot_general(_bf(a_refs[ai][...]), _bf(b_refs[bi][...]), dims,
                                                 preferred_element_type=F32)

        @pl.when(k == gk - 1)
        def _():
            res = epilogue([acc[...] for acc in acc_refs], [x[...] for x in x_refs])
            for o, v in zip(o_refs, res):
                o[...] = v.astype(o.dtype)

    if n_outer:
        grid = (gn, gm, gk)

        def ix3(f):
            return lambda j, i, k: f(i, j, k)

        def ix2(f):
            return lambda j, i, k: f(i, j)
    else:
        grid = (gm, gn, gk)

        def ix3(f):
            return lambda i, j, k: f(i, j, k)

        def ix2(f):
            return lambda i, j, k: f(i, j)

    in_specs = [pl.BlockSpec(blk, ix3(f)) for (_, blk, f, _) in list(a_ops) + list(b_ops)]
    in_specs += [pl.BlockSpec(blk, ix2(f)) for (_, blk, f) in extras]
    in_specs += [pl.BlockSpec(memory_space=pl.ANY)] * nd
    out_specs = [pl.BlockSpec(blk, ix2(f)) for (_, blk, f) in outs]
    return pl.pallas_call(
        body,
        name=name,
        grid=grid,
        in_specs=in_specs,
        out_specs=out_specs,
        out_shape=[s for (s, _, _) in outs],
        scratch_shapes=[pltpu.VMEM(acc_tile, F32) for _ in range(n_acc)],
        compiler_params=_params(("parallel", "parallel", "arbitrary")),
    )(*[o[0] for o in a_ops], *[o[0] for o in b_ops], *[x[0] for x in extras], *([after] * nd))


def _acc0(accs, xs):
    return (accs[0],)


def mm_simple(name, a, b_op_fn, M, N, K, tm, tn, tk, out_dtype=F32, a_t=False, extras=(), epilogue=_acc0, outs=None,
              after=None):
    a_op = op_at(a, tm, tk) if a_t else op_a(a, tm, tk)
    outs = outs or [out_mn(M, N, tm, tn, out_dtype)]
    return matmul(name, (M // tm, N // tn, K // tk), [a_op], [b_op_fn(tk, tn)], [(0, 0, 0)], 1,
                  list(extras), outs, epilogue, (tm, tn), after=after)


def rms_fwd(name, x, g, width, col_blk=0, tm=512, out_dtype=BF16):
    T = x.shape[0]

    def body(x_ref, g_ref, o_ref):
        xv = x_ref[...].astype(F32)
        r = lax.rsqrt(jnp.mean(xv * xv, axis=-1, keepdims=True) + EPS)
        o_ref[...] = (xv * r * g_ref[...]).astype(o_ref.dtype)

    return pl.pallas_call(
        body, name=name, grid=(T // tm,),
        in_specs=[pl.BlockSpec((tm, width), lambda i: (i, col_blk)), pl.BlockSpec((1, width), lambda i: (0, 0))],
        out_specs=pl.BlockSpec((tm, width), lambda i: (i, 0)),
        out_shape=jax.ShapeDtypeStruct((T, width), out_dtype),
        compiler_params=_params(("parallel",)),
    )(x, g.reshape(1, width))


def rms_bwd(name, x, g, dn, width, col_blk=0, dres=None, tm=512, with_delta=False, bf16_copy=False):
    T = x.shape[0]
    has_res = dres is not None

    def body(*refs):
        x_ref, g_ref, dn_ref = refs[:3]
        pos = 3
        res_ref = None
        if has_res:
            res_ref = refs[pos]
            pos += 1
        dx_ref, dg_ref = refs[pos], refs[pos + 1]
        delta_ref = refs[pos + 2] if with_delta else None
        lo_ref = refs[-1] if bf16_copy else None
        i = pl.program_id(0)
        xv = x_ref[...].astype(F32)
        r = lax.rsqrt(jnp.mean(xv * xv, axis=-1, keepdims=True) + EPS)
        xh = xv * r
        d = dn_ref[...].astype(F32)
        gd = d * g_ref[...]
        dx = r * (gd - xh * jnp.mean(gd * xh, axis=-1, keepdims=True))
        if has_res:
            dx = dx + res_ref[...]
        dx_ref[...] = dx.astype(dx_ref.dtype)
        if bf16_copy:
            lo_ref[...] = dx.astype(BF16)
        part = jnp.sum(d * xh, axis=0, keepdims=True)

        @pl.when(i == 0)
        def _():
            dg_ref[...] = part

        @pl.when(i > 0)
        def _():
            dg_ref[...] += part

        if with_delta:
            for h in range(width // 128):
                sl = slice(h * 128, (h + 1) * 128)
                s = jnp.sum(dx[:, sl] * xv[:, sl], axis=-1, keepdims=True)
                delta_ref[:, sl] = jnp.broadcast_to(s, (tm, 128))

    in_specs = [pl.BlockSpec((tm, width), lambda i: (i, col_blk)), pl.BlockSpec((1, width), lambda i: (0, 0)),
                pl.BlockSpec((tm, width), lambda i: (i, 0))]
    args = [x, g.reshape(1, width), dn]
    if has_res:
        in_specs.append(pl.BlockSpec((tm, width), lambda i: (i, 0)))
        args.append(dres)
    out_specs = [pl.BlockSpec((tm, width), lambda i: (i, 0)), pl.BlockSpec((1, width), lambda i: (0, 0))]
    out_shape = [jax.ShapeDtypeStruct((T, width), F32), jax.ShapeDtypeStruct((1, width), F32)]
    if with_delta:
        out_specs.append(pl.BlockSpec((tm, width), lambda i: (i, 0)))
        out_shape.append(jax.ShapeDtypeStruct((T, width), F32))
    if bf16_copy:
        out_specs.append(pl.BlockSpec((tm, width), lambda i: (i, 0)))
        out_shape.append(jax.ShapeDtypeStruct((T, width), BF16))
    return pl.pallas_call(
        body, name=name, grid=(T // tm,), in_specs=in_specs, out_specs=out_specs, out_shape=out_shape,
        compiler_params=_params(("arbitrary",)),
    )(*args)


def ffn_fwd(tag, h, g, w1g, w3g, w2g, pre, w2_late=None):
    T = h.shape[0]
    n = rms_fwd(f"{tag}_rms", h, g, D_MODEL)
    tm, tn = 512, FF_PAD

    def up_epi(accs, xs):
        a1, a3 = accs
        return a1, a3, a1 * _sigmoid(a1) * a3

    a1, a3, s = matmul(
        f"{tag}_up", (T // tm, FF_P // tn, 1),
        [op_a(n, tm, D_MODEL)], [op_b_rows_t(w1g, pre, D_MODEL, tn), op_b_rows_t(w3g, pre, D_MODEL, tn)],
        [(0, 0, 0), (0, 1, 1)], 2, [],
        [out_mn(T, FF_P, tm, tn, BF16)] * 3, up_epi, (tm, tn), n_outer=True)

    if w2_late is not None:
        w2g = w2_late(s)
    tm2, tn2 = 1024, 1024
    (h_out,) = matmul(
        f"{tag}_down", (T // tm2, D_MODEL // tn2, N_CHIPS),
        [op_a(s, tm2, FF_PAD)], [op_b_rows(w2g, pre, FF_PAD, tn2)],
        [(0, 0, 0)], 1, [tile_mn(h, tm2, tn2)],
        [out_mn(T, D_MODEL, tm2, tn2, F32)], lambda accs, xs: (xs[0] + 0.5 * accs[0],), (tm2, tn2))
    return h_out, (n, a1, a3, s)


def ffn_bwd(tag, dh_out, dh_bf, h, g, res, w1g, w3g, w2g, pre, after=None, before_dw=None):
    n, a1, a3, s = res
    T = h.shape[0]
    tm, tn = 512, FF_PAD

    def act_epi(accs, xs):
        ds = 0.5 * accs[0]
        x1, x3 = xs[0].astype(F32), xs[1].astype(F32)
        sg = _sigmoid(x1)
        silu = x1 * sg
        return ds * x3 * (sg + silu * (1.0 - sg)), ds * silu

    da1, da3 = matmul(
        f"{tag}_dact", (T // tm, FF_P // tn, 1),
        [op_a(dh_bf, tm, D_MODEL)], [op_b_rows_t(w2g, pre, D_MODEL, tn)],
        [(0, 0, 0)], 1, [tile_mn(a1, tm, tn), tile_mn(a3, tm, tn)],
        [out_mn(T, FF_P, tm, tn, BF16)] * 2, act_epi, (tm, tn), n_outer=True, after=after)

    tm2, tn2 = 1024, 1024
    (dn,) = matmul(
        f"{tag}_dn", (T // tm2, D_MODEL // tn2, N_CHIPS),
        [op_a(da1, tm2, FF_PAD), op_a(da3, tm2, FF_PAD)],
        [op_b_rows(w1g, pre, FF_PAD, tn2), op_b_rows(w3g, pre, FF_PAD, tn2)],
        [(0, 0, 0), (1, 1, 0)], 1, [], [out_mn(T, D_MODEL, tm2, tn2, F32)], _acc0, (tm2, tn2))
    dh, dg, dh_lo = rms_bwd(f"{tag}_rms_bwd", h, g, dn, D_MODEL, dres=dh_out, bf16_copy=True)
    if before_dw is not None:
        after = before_dw(dh)

    tk, tn3 = T, 512

    def dw_t(nm, left, right, scale):
        (dw,) = matmul(
            f"{tag}_{nm}", (FF_P // FF_PAD, D_MODEL // tn3, T // tk),
            [op_at(left, FF_PAD, tk)], [op_b(right, tk, tn3)],
            [(0, 0, 0)], 1, [], [out_mn(FF_P, D_MODEL, FF_PAD, tn3, BF16)],
            lambda accs, xs: (scale * accs[0],), (FF_PAD, tn3), after=after)
        return dw

    dw2 = dw_t("dw2", s, dh_bf, 0.5)
    dw1 = dw_t("dw1", da1, n, 1.0)
    dw3 = dw_t("dw3", da3, n, 1.0)
    return dh, dh_lo, dg, dw1, dw3, dw2


def rope_tables(positions):
    inv_freq = ROPE_BASE ** (-jnp.arange(0, QK_ROPE, 2, dtype=F32) / QK_ROPE)
    ang = positions.astype(F32)[:, None] * inv_freq
    cos, sin = jnp.cos(ang), jnp.sin(ang)
    T = positions.shape[0]
    one, zero = jnp.ones((T, QK_NOPE), F32), jnp.zeros((T, 64), F32)
    z32, z128 = jnp.zeros((T, 32), F32), jnp.zeros((T, QK_NOPE), F32)
    c = jnp.concatenate([one, cos, cos, zero], axis=1)
    s1 = jnp.concatenate([z128, -sin, z32, zero], axis=1)
    s2 = jnp.concatenate([z128, z32, sin, zero], axis=1)
    return c, s1, s2


def _rope(y, c, s1, s2):
    return y * c + pltpu.roll(y, HEAD_PAD - 32, 1) * s1 + pltpu.roll(y, 32, 1) * s2


def _rope_t(d, c, s1, s2):
    return d * c + pltpu.roll(d * s1, 32, 1) + pltpu.roll(d * s2, HEAD_PAD - 32, 1)


def _head_norm(x):
    r = lax.rsqrt(jnp.sum(x * x, axis=-1, keepdims=True) * (1.0 / QK_DIM) + EPS)
    return x * r, r


def qk_prep_fwd(tag, q_raw, kk_raw, z_p, gq, gk, tabs, tm=256):
    T = q_raw.shape[0]
    c, s1, s2 = tabs

    def body(q_ref, k_ref, kr_ref, gq_ref, gk_ref, c_ref, s1_ref, s2_ref, qo_ref, ko_ref):
        cv, s1v, s2v = c_ref[...], s1_ref[...], s2_ref[...]
        kr = kr_ref[...]
        for h in range(HEADS):
            sl = slice(h * HEAD_PAD, (h + 1) * HEAD_PAD)
            xh, _ = _head_norm(q_ref[:, sl])
            qo_ref[:, sl] = (_rope(xh * gq_ref[...], cv, s1v, s2v) * ATTN_SCALE).astype(BF16)
            xh, _ = _head_norm(k_ref[:, sl] + kr)
            ko_ref[:, sl] = _rope(xh * gk_ref[...], cv, s1v, s2v).astype(BF16)

    row = lambda i: (i, 0)
    full = pl.BlockSpec((tm, HEADS * HEAD_PAD), row)
    tab = pl.BlockSpec((tm, HEAD_PAD), row)
    vec = pl.BlockSpec((1, HEAD_PAD), lambda i: (0, 0))
    return pl.pallas_call(
        body, name=f"{tag}_qk_prep", grid=(T // tm,),
        in_specs=[full, full, pl.BlockSpec((tm, HEAD_PAD), lambda i: (i, 3)), vec, vec, tab, tab, tab],
        out_specs=[full, full],
        out_shape=[jax.ShapeDtypeStruct((T, HEADS * HEAD_PAD), BF16)] * 2,
        compiler_params=_params(("parallel",)),
    )(q_raw, kk_raw, z_p, gq, gk, c, s1, s2)


def qk_prep_bwd(tag, dq_full, dk_full, q_raw, kk_raw, z_p, gq, gk, tabs, tm=256):
    T = q_raw.shape[0]
    c, s1, s2 = tabs

    def body(dq_ref, dk_ref, q_ref, k_ref, kr_ref, gq_ref, gk_ref, c_ref, s1_ref, s2_ref,
             dqr_ref, dkr_ref, dz_ref, dgq_ref, dgk_ref):
        i = pl.program_id(0)
        cv, s1v, s2v = c_ref[...], s1_ref[...], s2_ref[...]
        kr = kr_ref[...]
        lane = lax.broadcasted_iota(jnp.int32, (tm, HEAD_PAD), 1)
        slot = ((lane >= QK_NOPE) & (lane < QK_DIM)).astype(F32)

        def one(x, g, d):
            xh, r = _head_norm(x)
            dy = _rope_t(d, cv, s1v, s2v)
            gd = dy * g
            dx = r * (gd - xh * (jnp.sum(gd * xh, axis=-1, keepdims=True) * (1.0 / QK_DIM)))
            return dx, jnp.sum(dy * xh, axis=0, keepdims=True)

        dgq = jnp.zeros((1, HEAD_PAD), F32)
        dgk = jnp.zeros((1, HEAD_PAD), F32)
        dz = jnp.zeros((tm, HEAD_PAD), F32)
        for h in range(HEADS):
            sl = slice(h * HEAD_PAD, (h + 1) * HEAD_PAD)
            dx, dg = one(q_ref[:, sl], gq_ref[...], dq_ref[:, sl].astype(F32) * ATTN_SCALE)
            dqr_ref[:, sl] = dx
            dgq = dgq + dg
            dx, dg = one(k_ref[:, sl] + kr, gk_ref[...], dk_ref[:, sl].astype(F32))
            dkr_ref[:, sl] = dx
            dgk = dgk + dg
            dz = dz + dx
        dz_ref[...] = dz * slot

        @pl.when(i == 0)
        def _():
            dgq_ref[...] = dgq
            dgk_ref[...] = dgk

        @pl.when(i > 0)
        def _():
            dgq_ref[...] += dgq
            dgk_ref[...] += dgk

    row = lambda i: (i, 0)
    full = pl.BlockSpec((tm, HEADS * HEAD_PAD), row)
    tab = pl.BlockSpec((tm, HEAD_PAD), row)
    vec = pl.BlockSpec((1, HEAD_PAD), lambda i: (0, 0))
    return pl.pallas_call(
        body, name=f"{tag}_qk_prep_bwd", grid=(T // tm,),
        in_specs=[full, full, full, full, pl.BlockSpec((tm, HEAD_PAD), lambda i: (i, 3)), vec, vec, tab, tab, tab],
        out_specs=[full, full, tab, vec, vec],
        out_shape=[jax.ShapeDtypeStruct((T, HEADS * HEAD_PAD), F32)] * 2
        + [jax.ShapeDtypeStruct((T, HEAD_PAD), F32)] + [jax.ShapeDtypeStruct((1, HEAD_PAD), F32)] * 2,
        compiler_params=_params(("arbitrary",)),
    )(dq_full, dk_full, q_raw, kk_raw, z_p, gq, gk, c, s1, s2)


def attn_fwd(tag, q_full, k_full, vv, blk=512):
    T = q_full.shape[0]
    nb = T // blk
    neg = float(jnp.finfo(jnp.float32).min)

    def body(q_ref, k_ref, v_ref, o_ref, lse_ref, m_ref, l_ref, acc_ref):
        i = pl.program_id(1)
        m_ref[...] = jnp.full_like(m_ref, neg)
        l_ref[...] = jnp.zeros_like(l_ref)
        acc_ref[...] = jnp.zeros_like(acc_ref)
        q = q_ref[...]

        def step(j, masked):
            rows = pl.ds(pl.multiple_of(j * blk, blk), blk)
            s = lax.dot_general(q, k_ref[rows, :], (((1,), (1,)), ((), ())), preferred_element_type=F32)
            if masked:
                row = lax.broadcasted_iota(jnp.int32, (blk, blk), 0)
                col = lax.broadcasted_iota(jnp.int32, (blk, blk), 1)
                s = jnp.where(col <= row, s, neg)
            m_prev = m_ref[...]
            m_new = jnp.maximum(m_prev, jnp.max(s, axis=-1, keepdims=True))
            alpha = jnp.exp(m_prev - m_new)
            p = jnp.exp(s - m_new[:, :1])
            l_ref[...] = alpha * l_ref[...] + jnp.sum(p, axis=-1, keepdims=True)
            acc_ref[...] = alpha * acc_ref[...] + jnp.dot(p.astype(BF16), v_ref[rows, :], preferred_element_type=F32)
            m_ref[...] = m_new

        def off_diagonal(j, carry):
            step(j, False)
            return carry

        lax.fori_loop(0, i, off_diagonal, 0)
        step(i, True)
        o_ref[...] = acc_ref[...] / l_ref[...]
        lse_ref[...] = m_ref[...] + jnp.log(l_ref[...])

    return pl.pallas_call(
        body, name=f"{tag}_attn_fwd", grid=(HEADS, nb),
        in_specs=[pl.BlockSpec((blk, HEAD_PAD), lambda h, i: (i, h)),
                  pl.BlockSpec((T, HEAD_PAD), lambda h, i: (0, h)), pl.BlockSpec((T, V_DIM), lambda h, i: (0, h))],
        out_specs=[pl.BlockSpec((blk, V_DIM), lambda h, i: (i, h))] * 2,
        out_shape=[jax.ShapeDtypeStruct((T, ATTN_W), F32)] * 2,
        scratch_shapes=[pltpu.VMEM((blk, V_DIM), F32)] * 3,
        compiler_params=_params(("parallel", "parallel")),
    )(q_full, k_full, vv)


def attn_bwd(tag, q_full, k_full, vv, do, lse, delta, blk=512):
    T = q_full.shape[0]
    nb = T // blk
    neg = float(jnp.finfo(jnp.float32).min)

    def body(q_ref, k_ref, v_ref, do_ref, lse_ref, dl_ref, dq_ref, dk_ref, dv_ref, dk_acc, dv_acc):
        j = pl.program_id(1)

        @pl.when(j == 0)
        def _():
            dq_ref[...] = jnp.zeros_like(dq_ref)

        dk_acc[...] = jnp.zeros_like(dk_acc)
        dv_acc[...] = jnp.zeros_like(dv_acc)
        k, v = k_ref[...], v_ref[...]

        def step(i, masked):
            rows = pl.ds(pl.multiple_of(i * blk, blk), blk)
            q = q_ref[rows, :]
            s = lax.dot_general(q, k, (((1,), (1,)), ((), ())), preferred_element_type=F32)
            if masked:
                row = lax.broadcasted_iota(jnp.int32, (blk, blk), 0)
                col = lax.broadcasted_iota(jnp.int32, (blk, blk), 1)
                s = jnp.where(col <= row, s, neg)
            p = jnp.exp(s - lse_ref[rows, :1])
            dob = _bf(do_ref[rows, :])
            dv_acc[...] += lax.dot_general(p.astype(BF16), dob, (((0,), (0,)), ((), ())), preferred_element_type=F32)
            dp = lax.dot_general(dob, v, (((1,), (1,)), ((), ())), preferred_element_type=F32)
            ds = (p * (dp - dl_ref[rows, :1])).astype(BF16)
            dk_acc[...] += lax.dot_general(ds, q, (((0,), (0,)), ((), ())), preferred_element_type=F32)
            dq_ref[rows, :] += jnp.dot(ds, k, preferred_element_type=F32)

        def off_diagonal(i, carry):
            step(i, False)
            return carry

        step(j, True)
        lax.fori_loop(j + 1, nb, off_diagonal, 0)
        dk_ref[...] = dk_acc[...]
        dv_ref[...] = dv_acc[...]

    head = lambda h, j: (0, h)
    kv_ix = lambda h, j: (j, h)
    return pl.pallas_call(
        body, name=f"{tag}_attn_bwd", grid=(HEADS, nb),
        in_specs=[pl.BlockSpec((T, HEAD_PAD), head), pl.BlockSpec((blk, HEAD_PAD), kv_ix),
                  pl.BlockSpec((blk, V_DIM), kv_ix), pl.BlockSpec((T, V_DIM), head),
                  pl.BlockSpec((T, V_DIM), head), pl.BlockSpec((T, V_DIM), head)],
        out_specs=[pl.BlockSpec((T, HEAD_PAD), head),
                   pl.BlockSpec((blk, HEAD_PAD), kv_ix), pl.BlockSpec((blk, V_DIM), kv_ix)],
        out_shape=[jax.ShapeDtypeStruct((T, HEADS * HEAD_PAD), F32)] * 2 + [jax.ShapeDtypeStruct((T, ATTN_W), F32)],
        scratch_shapes=[pltpu.VMEM((blk, HEAD_PAD), F32), pltpu.VMEM((blk, V_DIM), F32)],
        compiler_params=_params(("parallel", "arbitrary")),
    )(q_full, k_full, vv, do, lse, delta)


def _gm_forward(u, v, gv, wc_ref, bb_ref, nchunk):
    ug = _gelu(u)
    vg = _gelu(v)
    rv = lax.rsqrt(jnp.mean(vg * vg, axis=-1, keepdims=True) + EPS)
    vhat = vg * rv
    vn = (vhat * gv).astype(BF16)
    gates = []
    for cidx in range(nchunk):
        rows = slice(cidx * CHUNK, (cidx + 1) * CHUNK)
        gates.append(jnp.concatenate(
            [jnp.dot(wc_ref[gidx], vn[rows, gidx * 128:(gidx + 1) * 128], preferred_element_type=F32) + bb_ref[gidx]
             for gidx in range(GROUPS)], axis=1))
    gate = jnp.concatenate(gates, axis=0)
    return ug, vhat, rv, vn, gate


def gmlp_fwd(tag, z_p, gv, gout, wc, bb, tm=256):
    T = z_p.shape[0]
    nchunk = tm // CHUNK

    def body(u_ref, v_ref, gv_ref, go_ref, wc_ref, bb_ref, o_ref):
        ug, _, _, _, gate = _gm_forward(u_ref[...], v_ref[...], gv_ref[...], wc_ref, bb_ref, nchunk)
        go = ug * gate
        ro = lax.rsqrt(jnp.mean(go * go, axis=-1, keepdims=True) + EPS)
        o_ref[...] = (go * ro * go_ref[...]).astype(BF16)

    vec = pl.BlockSpec((1, GM_W), lambda i: (0, 0))
    w3 = pl.BlockSpec((GROUPS, CHUNK, CHUNK), lambda i: (0, 0, 0))
    return pl.pallas_call(
        body, name=f"{tag}_gmlp_fwd", grid=(T // tm,),
        in_specs=[pl.BlockSpec((tm, GM_W), lambda i: (i, 1)), pl.BlockSpec((tm, GM_W), lambda i: (i, 2)), vec, vec, w3, w3],
        out_specs=pl.BlockSpec((tm, GM_W), lambda i: (i, 0)),
        out_shape=jax.ShapeDtypeStruct((T, GM_W), BF16),
        compiler_params=_params(("parallel",)),
    )(z_p, z_p, gv.reshape(1, GM_W), gout.reshape(1, GM_W), wc, bb)


def gmlp_bwd(tag, z_p, dmixed, gv, gout, wc, bb, tm=256):
    T = z_p.shape[0]
    nchunk = tm // CHUNK

    def body(u_ref, v_ref, dm_ref, gv_ref, go_ref, wc_ref, bb_ref, du_ref, dv_ref, dwc_ref, dbb_ref, dgv_ref, dgo_ref):
        i = pl.program_id(0)
        u, v = u_ref[...], v_ref[...]
        ug, vhat, rv, vn, gate = _gm_forward(u, v, gv_ref[...], wc_ref, bb_ref, nchunk)
        go = ug * gate
        ro = lax.rsqrt(jnp.mean(go * go, axis=-1, keepdims=True) + EPS)
        ohat = go * ro
        dm = dm_ref[...].astype(F32)
        dgo_part = jnp.sum(dm * ohat, axis=0, keepdims=True)
        doh = dm * go_ref[...]
        dgo = ro * (doh - ohat * jnp.mean(doh * ohat, axis=-1, keepdims=True))
        du_ref[...] = dgo * gate * _gelu_grad(u)
        dgate = dgo * ug
        dgb = dgate.astype(BF16)
        dvn_rows = []
        dwc_parts = []
        dbb_parts = []
        for gidx in range(GROUPS):
            cols = slice(gidx * 128, (gidx + 1) * 128)
            dw = jnp.zeros((CHUNK, CHUNK), F32)
            db = jnp.zeros((CHUNK, 128), F32)
            for cidx in range(nchunk):
                rows = slice(cidx * CHUNK, (cidx + 1) * CHUNK)
                dw = dw + lax.dot_general(dgb[rows, cols], vn[rows, cols], (((1,), (1,)), ((), ())),
                                          preferred_element_type=F32)
                db = db + dgate[rows, cols]
            dwc_parts.append(dw)
            dbb_parts.append(db)
        for cidx in range(nchunk):
            rows = slice(cidx * CHUNK, (cidx + 1) * CHUNK)
            dvn_rows.append(jnp.concatenate(
                [lax.dot_general(wc_ref[gidx], dgb[rows, gidx * 128:(gidx + 1) * 128], (((0,), (0,)), ((), ())),
                                 preferred_element_type=F32) for gidx in range(GROUPS)], axis=1))
        dvn = jnp.concatenate(dvn_rows, axis=0)
        dgv_part = jnp.sum(dvn * vhat, axis=0, keepdims=True)
        dvh = dvn * gv_ref[...]
        dvg = rv * (dvh - vhat * jnp.mean(dvh * vhat, axis=-1, keepdims=True))
        dv_ref[...] = dvg * _gelu_grad(v)

        @pl.when(i == 0)
        def _():
            for gidx in range(GROUPS):
                dwc_ref[gidx] = dwc_parts[gidx]
                dbb_ref[gidx] = dbb_parts[gidx]
            dgv_ref[...] = dgv_part
            dgo_ref[...] = dgo_part

        @pl.when(i > 0)
        def _():
            for gidx in range(GROUPS):
                dwc_ref[gidx] += dwc_parts[gidx]
                dbb_ref[gidx] += dbb_parts[gidx]
            dgv_ref[...] += dgv_part
            dgo_ref[...] += dgo_part

    vec = pl.BlockSpec((1, GM_W), lambda i: (0, 0))
    w3 = pl.BlockSpec((GROUPS, CHUNK, CHUNK), lambda i: (0, 0, 0))
    blk = pl.BlockSpec((tm, GM_W), lambda i: (i, 0))
    return pl.pallas_call(
        body, name=f"{tag}_gmlp_bwd", grid=(T // tm,),
        in_specs=[pl.BlockSpec((tm, GM_W), lambda i: (i, 1)), pl.BlockSpec((tm, GM_W), lambda i: (i, 2)),
                  pl.BlockSpec((tm, GM_W), lambda i: (i, 1)), vec, vec, w3, w3],
        out_specs=[blk, blk, w3, w3, vec, vec],
        out_shape=[jax.ShapeDtypeStruct((T, GM_W), F32)] * 2 + [jax.ShapeDtypeStruct((GROUPS, CHUNK, CHUNK), F32)] * 2
        + [jax.ShapeDtypeStruct((1, GM_W), F32)] * 2,
        compiler_params=_params(("arbitrary",)),
    )(z_p, z_p, dmixed, gv.reshape(1, GM_W), gout.reshape(1, GM_W), wc, bb)


def mixer_fwd(tag, h, w, tabs, wout_g, pre):
    T = h.shape[0]
    n2 = rms_fwd(f"{tag}_mix_rms", h, w["mix_norm"], D_MODEL)
    (z_p,) = mm_simple(f"{tag}_win", n2, lambda tk, tn: op_bt(w["w_in_pt"], tk, tn), T, IN_P, D_MODEL, 512, 1024, D_MODEL)
    cqn = rms_fwd(f"{tag}_cq_rms", z_p, w["q_a_norm"], Q_RANK, col_blk=0)
    ckvn = rms_fwd(f"{tag}_ckv_rms", z_p, w["kv_a_norm"], KV_RANK, col_blk=2)
    (q_raw,) = mm_simple(f"{tag}_wq", cqn, lambda tk, tn: op_b(w["wq_p"], tk, tn), T, 2048, Q_RANK, 512, 1024, Q_RANK)
    (kk_raw,) = mm_simple(f"{tag}_wk", ckvn, lambda tk, tn: op_b(w["wk_p"], tk, tn), T, 2048, KV_RANK, 512, 1024, KV_RANK)
    (vv,) = mm_simple(f"{tag}_wv", ckvn, lambda tk, tn: op_b(w["wv"], tk, tn), T, ATTN_W, KV_RANK, 512, 1024, KV_RANK,
                      out_dtype=BF16)
    q_full, k_full = qk_prep_fwd(tag, q_raw, kk_raw, z_p, w["gq_p"], w["gk_p"], tabs)
    a_out, lse = attn_fwd(tag, q_full, k_full, vv)
    mixed_a = rms_fwd(f"{tag}_ao_rms", a_out, w["attn_out_norm"], ATTN_W)
    mixed_g = gmlp_fwd(tag, z_p, w["gm_v_norm"], w["gm_out_norm"], w["wc"], w["bb"])
    tm, tn, tk = 512, 1024, 512
    (h2,) = matmul(
        f"{tag}_wout", (T // tm, D_MODEL // tn, ATTN_W // tk),
        [op_a(mixed_a, tm, tk), op_a(mixed_g, tm, tk)],
        [op_b_rows(wout_g, pre, tk, tn), op_b_rows(wout_g, pre, tk, tn, koff=ATTN_W // tk)],
        [(0, 0, 0), (1, 1, 0)], 1, [tile_mn(h, tm, tn)], [out_mn(T, D_MODEL, tm, tn, F32)],
        lambda accs, xs: (xs[0] + accs[0],), (tm, tn))
    res = dict(n2=n2, z_p=z_p, cqn=cqn, ckvn=ckvn, q_raw=q_raw, kk_raw=kk_raw, vv=vv, q_full=q_full, k_full=k_full,
               a_out=a_out, lse=lse, mixed_a=mixed_a, mixed_g=mixed_g)
    return h2, res


def mixer_bwd(tag, dh2, dh2_bf, h, w, tabs, wout_g, pre, r, after=None):
    T = h.shape[0]
    g = {}
    (dmixed,) = mm_simple(f"{tag}_dmixed", dh2_bf, lambda tk, tn: op_b_rows_t(wout_g, pre, tk, tn), T, D_MODEL, D_MODEL,
                          512, 512, D_MODEL, after=after)
    (dwo_a,) = mm_simple(f"{tag}_dwout_a", r["mixed_a"], lambda tk, tn: op_b(dh2_bf, tk, tn), ATTN_W, D_MODEL, T,
                         1024, 1024, T, a_t=True, out_dtype=BF16)
    (dwo_g,) = mm_simple(f"{tag}_dwout_g", r["mixed_g"], lambda tk, tn: op_b(dh2_bf, tk, tn), GM_W, D_MODEL, T,
                         1024, 1024, T, a_t=True, out_dtype=BF16)
    g["w_out"] = jnp.concatenate([dwo_a, dwo_g], axis=0)
    da_out, g["attn_out_norm"], delta = rms_bwd(f"{tag}_ao_rms_bwd", r["a_out"], w["attn_out_norm"], dmixed, ATTN_W,
                                                with_delta=True)
    dq_full, dk_full, dvv = attn_bwd(tag, r["q_full"], r["k_full"], r["vv"], da_out, r["lse"], delta)
    dq_raw, dkk_raw, dzkr, g["gq_p"], g["gk_p"] = qk_prep_bwd(tag, dq_full, dk_full, r["q_raw"], r["kk_raw"], r["z_p"],
                                                            w["gq_p"], w["gk_p"], tabs)
    (g["wq_p"],) = mm_simple(f"{tag}_dwq", r["cqn"], lambda tk, tn: op_b(dq_raw, tk, tn), Q_RANK, 2048, T, Q_RANK, 1024, 2048,
                             a_t=True, out_dtype=BF16)
    (g["wk_p"],) = mm_simple(f"{tag}_dwk", r["ckvn"], lambda tk, tn: op_b(dkk_raw, tk, tn), KV_RANK, 2048, T, KV_RANK, 1024,
                             2048, a_t=True, out_dtype=BF16)
    (g["wv"],) = mm_simple(f"{tag}_dwv", r["ckvn"], lambda tk, tn: op_b(dvv, tk, tn), KV_RANK, ATTN_W, T, KV_RANK, 1024, 2048,
                           a_t=True, out_dtype=BF16)
    (dcqn,) = mm_simple(f"{tag}_dcqn", dq_raw, lambda tk, tn: op_bt(w["wq_p"], tk, tn), T, Q_RANK, 2048, 512, Q_RANK, 2048)
    (dck1,) = mm_simple(f"{tag}_dckvn_k", dkk_raw, lambda tk, tn: op_bt(w["wk_p"], tk, tn), T, KV_RANK, 2048, 512, KV_RANK,
                        2048)
    (dckvn,) = mm_simple(f"{tag}_dckvn_v", dvv, lambda tk, tn: op_bt(w["wv"], tk, tn), T, KV_RANK, ATTN_W, 512, KV_RANK,
                         ATTN_W, extras=[tile_mn(dck1, 512, KV_RANK)], epilogue=lambda accs, xs: (accs[0] + xs[0],))
    dc_q, g["q_a_norm"] = rms_bwd(f"{tag}_cq_rms_bwd", r["z_p"], w["q_a_norm"], dcqn, Q_RANK, col_blk=0)
    dc_kv, g["kv_a_norm"] = rms_bwd(f"{tag}_ckv_rms_bwd", r["z_p"], w["kv_a_norm"], dckvn, KV_RANK, col_blk=2)
    du, dv, g["wc"], g["bb"], g["gm_v_norm"], g["gm_out_norm"] = gmlp_bwd(
        tag, r["z_p"], dmixed, w["gm_v_norm"], w["gm_out_norm"], w["wc"], w["bb"])
    dz_p = jnp.concatenate([dc_q, dc_kv, dzkr, du, dv], axis=1).astype(BF16)
    (g["w_in_pt"],) = mm_simple(f"{tag}_dwin", dz_p, lambda tk, tn: op_b(r["n2"], tk, tn), IN_P, D_MODEL, T, 1024, 1024, T,
                                a_t=True, out_dtype=BF16)
    (dn2,) = mm_simple(f"{tag}_dn2", dz_p, lambda tk, tn: op_b(w["w_in_pt"], tk, tn), T, D_MODEL, IN_P, 512, 1024, IN_P)
    dh1, g["mix_norm"], dh1_bf = rms_bwd(f"{tag}_mix_rms_bwd", h, w["mix_norm"], dn2, D_MODEL, dres=dh2, bf16_copy=True)
    return dh1, dh1_bf, g


def ple_fwd(tag, h3, p_l, w, wpg_g, wple_g, pre):
    T = h3.shape[0]
    (pw,) = mm_simple(f"{tag}_wple", p_l, lambda tk, tn: op_b_cols(wple_g, pre, tk, tn), T, D_MODEL, PLE_DIM, 512, 512,
                      PLE_DIM)
    e = rms_fwd(f"{tag}_ple_rms", pw, w["ple_norm"], D_MODEL, out_dtype=F32)
    n4 = rms_fwd(f"{tag}_pg_rms", h3, w["ple_gate_norm"], D_MODEL)

    def epi(accs, xs):
        gt = _sigmoid(accs[0])
        return xs[0] + gt * xs[1], gt

    tm, tn, tk = 512, 1024, 512
    h4, gate = matmul(
        f"{tag}_wpg", (T // tm, D_MODEL // tn, D_MODEL // tk),
        [op_a(n4, tm, tk)], [op_b_rows(wpg_g, pre, tk, tn)], [(0, 0, 0)], 1,
        [tile_mn(h3, tm, tn), tile_mn(e, tm, tn)],
        [out_mn(T, D_MODEL, tm, tn, F32), out_mn(T, D_MODEL, tm, tn, BF16)], epi, (tm, tn))
    return h4, dict(pw=pw, e=e, n4=n4, gate=gate)


def ple_bwd(tag, dh4, h3, p_l, w, wpg_g, wple_g, pre, r, tm=256):
    T = h3.shape[0]

    def act_body(d_ref, g_ref, e_ref, dpre_ref, de_ref):
        d, gt = d_ref[...], g_ref[...].astype(F32)
        dpre_ref[...] = (d * e_ref[...] * gt * (1.0 - gt)).astype(BF16)
        de_ref[...] = d * gt

    blk = pl.BlockSpec((tm, D_MODEL), lambda i: (i, 0))
    dpre, de = pl.pallas_call(
        act_body, name=f"{tag}_ple_act_bwd", grid=(T // tm,), in_specs=[blk, blk, blk], out_specs=[blk, blk],
        out_shape=[jax.ShapeDtypeStruct((T, D_MODEL), BF16), jax.ShapeDtypeStruct((T, D_MODEL), F32)],
        compiler_params=_params(("parallel",)),
    )(dh4, r["gate"], r["e"])
    g = {}
    (g["w_ple_gate"],) = mm_simple(f"{tag}_dwpg", r["n4"], lambda tk, tn: op_b(dpre, tk, tn), D_MODEL, D_MODEL, T,
                                   1024, 1024, T, a_t=True, out_dtype=BF16)
    (dn4,) = mm_simple(f"{tag}_dn4", dpre, lambda tk, tn: op_b_rows_t(wpg_g, pre, tk, tn), T, D_MODEL, D_MODEL, 512, 512,
                       D_MODEL)
    dh3, g["ple_gate_norm"], dh3_bf = rms_bwd(f"{tag}_pg_rms_bwd", h3, w["ple_gate_norm"], dn4, D_MODEL, dres=dh4,
                                              bf16_copy=True)
    dpw, g["ple_norm"] = rms_bwd(f"{tag}_ple_rms_bwd", r["pw"], w["ple_norm"], de, D_MODEL)
    (g["w_ple"],) = mm_simple(f"{tag}_dwple", p_l, lambda tk, tn: op_b(dpw, tk, tn), PLE_DIM, D_MODEL, T, PLE_DIM, 512, T,
                              a_t=True, outs=[out_cols(PLE_DIM, 512, PLE_DIM, 512, BF16)])
    return dh3, dh3_bf, g


def loss_grad(y, target, tm=256):
    T = y.shape[0]

    def body(y_ref, t_ref, dy_ref, l_ref):
        i = pl.program_id(0)
        d = y_ref[...] - t_ref[...]
        dy_ref[...] = d * (1.0 / D_MODEL)
        part = jnp.sum((d * d).reshape(tm // 8, 8, D_MODEL), axis=0)

        @pl.when(i == 0)
        def _():
            l_ref[...] = part

        @pl.when(i > 0)
        def _():
            l_ref[...] += part

    blk = pl.BlockSpec((tm, D_MODEL), lambda i: (i, 0))
    dy, part = pl.pallas_call(
        body, name="loss_grad", grid=(T // tm,), in_specs=[blk, blk],
        out_specs=[blk, pl.BlockSpec((8, D_MODEL), lambda i: (0, 0))],
        out_shape=[jax.ShapeDtypeStruct((T, D_MODEL), F32), jax.ShapeDtypeStruct((8, D_MODEL), F32)],
        compiler_params=_params(("arbitrary",)),
    )(y, target)
    return dy, 0.5 * jnp.sum(part) / D_MODEL


def _unshard_cols(g_l):
    return g_l.transpose(1, 0, 2).reshape(g_l.shape[1], -1)


def _shard_cols(w):
    return w.reshape(w.shape[0], N_CHIPS, -1).transpose(1, 0, 2)


def layer_weights(l, Gl, small):
    w = {k: small[k][l] for k in ("mix_norm", "q_a_norm", "kv_a_norm", "gm_v_norm", "attn_out_norm", "gm_out_norm",
                                  "ple_gate_norm", "ple_norm")}
    wint = Gl["w_in"][:, :IN_SHARD].reshape(-1, D_MODEL)
    z = lambda n: jnp.zeros((n, D_MODEL), BF16)
    w["w_in_pt"] = jnp.concatenate([wint[:768], z(128), wint[768:832], z(64), wint[832:]], axis=0)
    wuq = _unshard_cols(Gl["w_uq"]).reshape(Q_RANK, HEADS, QK_DIM)
    w["wq_p"] = jnp.pad(wuq, ((0, 0), (0, 0), (0, HEAD_PAD - QK_DIM))).reshape(Q_RANK, HEADS * HEAD_PAD)
    wukv = _unshard_cols(Gl["w_ukv"]).reshape(KV_RANK, HEADS, QK_NOPE + V_DIM)
    w["wk_p"] = jnp.pad(wukv[:, :, :QK_NOPE], ((0, 0), (0, 0), (0, HEAD_PAD - QK_NOPE))).reshape(KV_RANK, HEADS * HEAD_PAD)
    w["wv"] = wukv[:, :, QK_NOPE:].reshape(KV_RANK, ATTN_W)
    w["gq_p"] = jnp.pad(small["q_norm"][l], (0, HEAD_PAD - QK_DIM)).reshape(1, HEAD_PAD)
    w["gk_p"] = jnp.pad(small["k_norm"][l], (0, HEAD_PAD - QK_DIM)).reshape(1, HEAD_PAD)
    tril = jnp.tril(jnp.ones((CHUNK, CHUNK), dtype=bool))
    w["wc"] = jnp.where(tril[None], small["gm_ws"][l], 0.0).astype(BF16)
    w["bb"] = jnp.broadcast_to(small["gm_bs"][l][:, :, None], (GROUPS, CHUNK, 128)).astype(F32)
    return w


def mixer_grads_to_shards(g):
    out = {}
    dwint = g["w_in_pt"]
    dwint = jnp.concatenate([dwint[:768], dwint[896:960], dwint[1024:]], axis=0).reshape(N_CHIPS, IN_SHARD, D_MODEL)
    out["w_in"] = jnp.pad(dwint, ((0, 0), (0, IN_SHARD_PAD - IN_SHARD), (0, 0)))
    dwuq = g["wq_p"].reshape(Q_RANK, HEADS, HEAD_PAD)[:, :, :QK_DIM].reshape(Q_RANK, HEADS * QK_DIM)
    out["w_uq"] = _shard_cols(dwuq)
    dwukv = jnp.concatenate([g["wk_p"].reshape(KV_RANK, HEADS, HEAD_PAD)[:, :, :QK_NOPE],
                             g["wv"].reshape(KV_RANK, HEADS, V_DIM)], axis=-1).reshape(KV_RANK, HEADS * (QK_NOPE + V_DIM))
    out["w_ukv"] = _shard_cols(dwukv)
    out["w_out"] = g["w_out"].reshape(N_CHIPS, D_MODEL // N_CHIPS, D_MODEL)
    out["q_norm"] = g["gq_p"][0, :QK_DIM]
    out["k_norm"] = g["gk_p"][0, :QK_DIM]
    tril = jnp.tril(jnp.ones((CHUNK, CHUNK), dtype=bool))
    out["gm_ws"] = jnp.where(tril[None], g["wc"], 0.0)
    out["gm_bs"] = jnp.sum(g["bb"], axis=-1)
    for k in ("mix_norm", "q_a_norm", "kv_a_norm", "gm_v_norm", "attn_out_norm", "gm_out_norm"):
        out[k] = g[k][0]
    return out


def layer_fwd(l, h, p_l, Gl, small, tabs, before=None):
    before = before or {}
    h1, r_a = ffn_fwd(f"l{l}a", h, small["ffn_a_norm"][l], Gl["ffn_a_w1"], Gl["ffn_a_w3"], Gl.get("ffn_a_w2"), (),
                      before.get("down_a"))
    if "mixer" in before:
        Gl, small = before["mixer"](h1, Gl, small)
    w = layer_weights(l, Gl, small)
    h2, r_m = mixer_fwd(f"l{l}", h1, w, tabs, Gl["w_out"], ())
    if "ffn_b" in before:
        Gl = before["ffn_b"](h2, Gl)
    h3, r_b = ffn_fwd(f"l{l}b", h2, small["ffn_b_norm"][l], Gl["ffn_b_w1"], Gl["ffn_b_w3"], Gl["ffn_b_w2"], ())
    if "ple" in before:
        w = {**w, "ple_norm": w["ple_norm"] + before["ple"](h3)[0, 0]}
    h4, r_p = ple_fwd(f"l{l}", h3, p_l, w, Gl["w_ple_gate"], Gl["w_ple"], ())
    return h4, (w, h, h1, h2, h3, r_a, r_m, r_b, r_p)


def layer_bwd(l, dh, p_l, Gl, small, tabs, saved, before=None):
    w, h0, h1, h2, h3, r_a, r_m, r_b, r_p = saved
    slabs = lambda d: d.reshape(N_CHIPS, FF_PAD, D_MODEL)
    hook = lambda block: before[block](gl, dh) if before and block in before else None
    gl = {}
    dh, dh_bf, g_p = ple_bwd(f"l{l}", dh, h3, p_l, w, Gl["w_ple_gate"], Gl["w_ple"], (), r_p)
    gl["w_ple_gate"] = g_p["w_ple_gate"].reshape(N_CHIPS, D_MODEL // N_CHIPS, D_MODEL)
    gl["w_ple"] = g_p["w_ple"]
    gl["ple_gate_norm"], gl["ple_norm"] = g_p["ple_gate_norm"][0], g_p["ple_norm"][0]
    dh, dh_bf, dg, dw1, dw3, dw2 = ffn_bwd(f"l{l}b", dh, dh_bf, h2, small["ffn_b_norm"][l], r_b,
                                           Gl["ffn_b_w1"], Gl["ffn_b_w3"], Gl["ffn_b_w2"], (), hook("ffn_b"))
    gl["ffn_b_norm"] = dg[0]
    gl["ffn_b_w1"], gl["ffn_b_w3"], gl["ffn_b_w2"] = slabs(dw1), slabs(dw3), slabs(dw2)
    dh, dh_bf, g_m = mixer_bwd(f"l{l}", dh, dh_bf, h1, w, tabs, Gl["w_out"], (), r_m, hook("mixer"))
    gl.update(mixer_grads_to_shards(g_m))
    last_dw = (lambda dh_: before["ffn_a_dw"](gl, dh_)) if before and "ffn_a_dw" in before else None
    dh, _, dg, dw1, dw3, dw2 = ffn_bwd(f"l{l}a", dh, dh_bf, h0, small["ffn_a_norm"][l], r_a,
                                       Gl["ffn_a_w1"], Gl["ffn_a_w3"], Gl["ffn_a_w2"], (), hook("ffn_a"), last_dw)
    gl["ffn_a_norm"] = dg[0]
    gl["ffn_a_w1"], gl["ffn_a_w3"], gl["ffn_a_w2"] = slabs(dw1), slabs(dw3), slabs(dw2)
    return dh, gl


MESH = pl.DeviceIdType.MESH
HBM_SPEC = pl.BlockSpec(memory_space=pltpu.HBM)


def _place():
    x, y, c = lax.axis_index("x"), lax.axis_index("y"), lax.axis_index("c")
    others = [(1 - x, y), (x, 1 - y), (1 - x, 1 - y)]
    return x, y, c, 2 * x + y, others


def prep_shard(name, w, layer, rows_pad, place, after=None):
    _, ks, n = w.shape
    ksp = ks + rows_pad
    tc = 512 if n % 512 == 0 else n
    deps = [] if after is None else [after]

    def body(place_ref, x_ref, *rest):
        o_ref = rest[-1]
        o_ref[:ks] = x_ref[...].astype(BF16)
        if rows_pad:
            o_ref[ks:] = jnp.zeros((rows_pad, tc), BF16)

    return pl.pallas_call(
        body, name=name,
        grid_spec=pltpu.PrefetchScalarGridSpec(
            num_scalar_prefetch=1, grid=(n // tc,),
            in_specs=[pl.BlockSpec((None, ks, tc), lambda i, s: (layer, 0, i))] + [ANY_SPEC] * len(deps),
            out_specs=pl.BlockSpec((None, ksp, tc), lambda i, s: (s[0], 0, i))),
        out_shape=jax.ShapeDtypeStruct((N_CHIPS, ksp, n), BF16),
        compiler_params=_params(("parallel",)),
    )(place, w, *deps)


SEM_SPEC = pl.BlockSpec(memory_space=pltpu.SEMAPHORE)
ANY_SPEC = pl.BlockSpec(memory_space=pl.ANY)
DATAFLOW = pltpu.SideEffectType.DATAFLOW_SIDE_EFFECTING


def _hbm(x):
    return pltpu.with_memory_space_constraint(x, pltpu.HBM)


def _start_call(name, slots, after, issue):
    n = len(slots)
    deps = [] if after is None else [after]
    nd = len(deps)

    def body(*refs):
        issue(refs[n + nd + 2:2 * n + nd + 2], refs[n + nd], refs[n + nd + 1])
        token = refs[2 * n + nd + 2]
        token[...] = jnp.zeros_like(token)

    outs = pl.pallas_call(
        body, name=name,
        in_specs=[HBM_SPEC] * n + [ANY_SPEC] * nd,
        out_specs=(SEM_SPEC, SEM_SPEC, *([HBM_SPEC] * n), pl.BlockSpec(memory_space=pltpu.VMEM)),
        out_shape=(pltpu.SemaphoreType.DMA((n,)), pltpu.SemaphoreType.DMA((n,)),
                   *[pltpu.HBM(s.shape, s.dtype) for s in slots], jax.ShapeDtypeStruct((8, 128), F32)),
        input_output_aliases={w: w + 2 for w in range(n)},
        compiler_params=pltpu.CompilerParams(has_side_effects=DATAFLOW),
    )(*[_hbm(s) for s in slots], *deps)
    return outs[0], outs[1], list(outs[2:2 + n]), outs[2 + n]


def gather_start(name, slots, after):
    def issue(g_refs, send, recv):
        x, y, c, jme, others = _place()
        for w in range(len(slots)):
            kh = slots[w].shape[1] // 2
            mine = g_refs[w].at[jme, pl.ds(c * kh, kh)]
            for (px, py) in others:
                pltpu.make_async_remote_copy(src_ref=mine, dst_ref=mine, send_sem=send.at[w], recv_sem=recv.at[w],
                                             device_id=(px, py, c), device_id_type=MESH).start()

    return _start_call(name, slots, after, issue)


def forward_start(name, slots):
    def issue(g_refs, send, recv):
        x, y, c, _, others = _place()
        for w in range(len(slots)):
            kh = slots[w].shape[1] // 2
            for (px, py) in others:
                blk = g_refs[w].at[2 * px + py, pl.ds(c * kh, kh)]
                pltpu.make_async_remote_copy(src_ref=blk, dst_ref=blk, send_sem=send.at[w], recv_sem=recv.at[w],
                                             device_id=(x, y, 1 - c), device_id_type=MESH).start()

    return _start_call(name, slots, None, issue)


def share_start(name, fulls):
    def issue(o_refs, send, recv):
        x, y, c, _, _ = _place()
        for w in range(len(fulls)):
            kh = fulls[w].shape[0] // 2
            half = o_refs[w].at[pl.ds(c * kh, kh)]
            pltpu.make_async_remote_copy(src_ref=half, dst_ref=half, send_sem=send.at[w], recv_sem=recv.at[w],
                                         device_id=(x, y, 1 - c), device_id_type=MESH).start()

    return _start_call(name, fulls, None, issue)


def share_wait(name, send, recv, flying, after):
    return _wait_call(name, send, recv, flying, after, lambda r: r.at[pl.ds(0, r.shape[0] // 2)])


def gather_wait(name, send, recv, flying, after):
    return _wait_call(name, send, recv, flying, after, lambda r: r.at[pl.ds(0, 3), pl.ds(0, r.shape[1] // 2)])


def _wait_call(name, send, recv, flying, after, landed):
    n = len(flying)

    def body(*refs):
        send_ref, recv_ref = refs[n], refs[n + 1]
        g_refs = refs[n + 3:]
        x, y, c, _, _ = _place()
        for w in range(n):
            cp = pltpu.make_async_remote_copy(src_ref=landed(g_refs[w]), dst_ref=landed(g_refs[w]),
                                              send_sem=send_ref.at[w], recv_sem=recv_ref.at[w],
                                              device_id=(x, y, 1 - c), device_id_type=MESH)
            cp.wait_send()
            cp.wait_recv()

    return pl.pallas_call(
        body, name=name,
        in_specs=[HBM_SPEC] * n + [SEM_SPEC, SEM_SPEC, ANY_SPEC],
        out_specs=[HBM_SPEC] * n,
        out_shape=[pltpu.HBM(s.shape, s.dtype) for s in flying],
        input_output_aliases={w: w for w in range(n)},
        compiler_params=pltpu.CompilerParams(has_side_effects=DATAFLOW),
    )(*flying, send, recv, after)


def _send_start(name, srcs, land_shapes, issue, after):
    n = len(srcs)
    deps = [] if after is None else [after]
    nd = len(deps)

    def body(*refs):
        base = 2 * n + nd
        issue(refs[base + 2:base + 2 + n], refs[base + 2 + n:base + 2 + 2 * n], refs[base], refs[base + 1])
        token = refs[base + 2 + 2 * n]
        token[...] = jnp.zeros_like(token)

    lands = [_hbm(lax.empty(shape, s.dtype)) for shape, s in zip(land_shapes, srcs)]
    outs = pl.pallas_call(
        body, name=name,
        in_specs=[HBM_SPEC] * (2 * n) + [ANY_SPEC] * nd,
        out_specs=(SEM_SPEC, SEM_SPEC, *([HBM_SPEC] * (2 * n)), pl.BlockSpec(memory_space=pltpu.VMEM)),
        out_shape=(pltpu.SemaphoreType.DMA((n,)), pltpu.SemaphoreType.DMA((n,)),
                   *[pltpu.HBM(s.shape, s.dtype) for s in srcs], *[pltpu.HBM(l.shape, l.dtype) for l in lands],
                   jax.ShapeDtypeStruct((8, 128), F32)),
        input_output_aliases={w: w + 2 for w in range(2 * n)},
        compiler_params=pltpu.CompilerParams(has_side_effects=DATAFLOW),
    )(*[_hbm(s) for s in srcs], *lands, *deps)
    return outs[0], outs[1], list(outs[2:2 + n]), list(outs[2 + n:2 + 2 * n]), outs[2 + 2 * n]


def _send_wait(name, send, recv, srcs, lands, after, landed):
    n = len(srcs)

    def body(*refs):
        send_ref, recv_ref = refs[2 * n], refs[2 * n + 1]
        q_refs = refs[3 * n + 3:]
        x, y, c, _, _ = _place()
        for w in range(n):
            cp = pltpu.make_async_remote_copy(src_ref=landed(q_refs[w]), dst_ref=landed(q_refs[w]), send_sem=send_ref.at[w],
                                              recv_sem=recv_ref.at[w], device_id=(x, y, 1 - c), device_id_type=MESH)
            cp.wait_send()
            cp.wait_recv()

    outs = pl.pallas_call(
        body, name=name,
        in_specs=[HBM_SPEC] * (2 * n) + [SEM_SPEC, SEM_SPEC, ANY_SPEC],
        out_specs=[HBM_SPEC] * (2 * n),
        out_shape=[pltpu.HBM(a.shape, a.dtype) for a in list(srcs) + list(lands)],
        input_output_aliases={w: w for w in range(2 * n)},
        compiler_params=pltpu.CompilerParams(has_side_effects=DATAFLOW),
    )(*srcs, *lands, send, recv, after)
    return list(outs[:n]), list(outs[n:])


def exchange_start(name, grads, after):
    def issue(d_refs, r_refs, send, recv):
        x, y, c, _, _ = _place()
        for w in range(len(grads)):
            half = grads[w].shape[1] // 2
            pltpu.make_async_remote_copy(
                src_ref=d_refs[w].at[pl.ds(0, N_CHIPS), pl.ds((1 - c) * half, half)], dst_ref=r_refs[w],
                send_sem=send.at[w], recv_sem=recv.at[w], device_id=(x, y, 1 - c), device_id_type=MESH).start()

    return _send_start(name, grads, [(N_CHIPS, g.shape[1] // 2, g.shape[2]) for g in grads], issue, after)


def exchange_wait(name, send, recv, grads, lands, after):
    return _send_wait(name, send, recv, grads, lands, after, lambda r: r)


def scatter_start(name, parts):
    def issue(p_refs, q_refs, send, recv):
        x, y, c, jme, others = _place()
        for w in range(len(parts)):
            for (px, py) in others:
                pltpu.make_async_remote_copy(
                    src_ref=p_refs[w].at[2 * px + py], dst_ref=q_refs[w].at[jme], send_sem=send.at[w], recv_sem=recv.at[w],
                    device_id=(px, py, c), device_id_type=MESH).start()

    return _send_start(name, parts, [p.shape for p in parts], issue, None)


def scatter_wait(name, send, recv, parts, lands, after):
    return _send_wait(name, send, recv, parts, lands, after, lambda r: r.at[pl.ds(0, 3)])


def allreduce_small(v):
    R = v.shape[0]

    def body(v_ref, o_ref, sib_ref, mine_ref, all_ref, d_send, d_recv, i_send, i_recv):
        x, y, c, jme, others = _place()
        swap = pltpu.make_async_remote_copy(src_ref=v_ref, dst_ref=sib_ref, send_sem=d_send, recv_sem=d_recv,
                                            device_id=(x, y, 1 - c), device_id_type=MESH)
        swap.start()
        swap.wait()
        mine_ref[...] = v_ref[...] + sib_ref[...]
        for (px, py) in others:
            pltpu.make_async_remote_copy(src_ref=mine_ref, dst_ref=all_ref.at[jme], send_sem=i_send, recv_sem=i_recv,
                                         device_id=(px, py, c), device_id_type=MESH).start()
        three = all_ref.at[pl.ds(0, 3)]
        wait3 = pltpu.make_async_remote_copy(src_ref=three, dst_ref=three, send_sem=i_send, recv_sem=i_recv,
                                             device_id=(x, y, c), device_id_type=MESH)
        wait3.wait_recv()
        wait3.wait_send()
        all_ref[jme] = mine_ref[...]
        o_ref[...] = ((all_ref[0] + all_ref[1]) + all_ref[2]) + all_ref[3]

    vm = pl.BlockSpec(memory_space=pltpu.VMEM)
    return pl.pallas_call(
        body, name="allreduce_small", in_specs=[vm], out_specs=vm,
        out_shape=jax.ShapeDtypeStruct(v.shape, F32),
        scratch_shapes=[pltpu.VMEM((R, 128), F32), pltpu.VMEM((R, 128), F32), pltpu.VMEM((N_CHIPS, R, 128), F32),
                        pltpu.SemaphoreType.DMA, pltpu.SemaphoreType.DMA, pltpu.SemaphoreType.DMA, pltpu.SemaphoreType.DMA],
        compiler_params=pltpu.CompilerParams(vmem_limit_bytes=VMEM_LIMIT_BYTES),
    )(v)


def _row_tile(rows, width, mult=16, cap=3 << 20):
    best = rows
    for t in range(mult, rows + 1, mult):
        if rows % t == 0 and t * width * 4 <= cap:
            best = t
    return best


def add_sibling(name, mine, theirs, place):
    _, kh, ns = theirs.shape
    tr = _row_tile(kh, ns)
    nblk = kh // tr

    def body(place_ref, a_ref, b_ref, o_ref):
        o_ref[...] = (a_ref[...].astype(F32) + b_ref[...].astype(F32)).astype(BF16)

    return pl.pallas_call(
        body, name=name,
        grid_spec=pltpu.PrefetchScalarGridSpec(
            num_scalar_prefetch=1, grid=(N_CHIPS, nblk),
            in_specs=[pl.BlockSpec((None, tr, ns), lambda j, i, s: (j, s[1] * nblk + i, 0)),
                      pl.BlockSpec((None, tr, ns), lambda j, i, s: (j, i, 0))],
            out_specs=pl.BlockSpec((None, tr, ns), lambda j, i, s: (j, i, 0))),
        out_shape=jax.ShapeDtypeStruct(theirs.shape, BF16),
        compiler_params=_params(("parallel", "parallel")),
    )(place, mine, theirs)


def add_chips(name, q, p, place):
    _, kh, ns = q.shape
    tr = _row_tile(kh, ns)
    nblk = kh // tr

    def body(place_ref, *refs):
        q_refs, own_ref, o_ref = refs[:N_CHIPS], refs[N_CHIPS], refs[-1]
        jme = place_ref[0]
        tot = None
        for j in range(N_CHIPS):
            v = jnp.where(jme == j, own_ref[...], q_refs[j][...]).astype(F32)
            tot = v if tot is None else tot + v
        o_ref[...] = tot

    def q_ix(j):
        return lambda i, s: (jnp.where(s[0] == j, (j + 1) % N_CHIPS, j), i, 0)

    in_specs = [pl.BlockSpec((None, tr, ns), q_ix(j)) for j in range(N_CHIPS)]
    in_specs.append(pl.BlockSpec((None, tr, ns), lambda i, s: (s[0], i, 0)))
    return pl.pallas_call(
        body, name=name,
        grid_spec=pltpu.PrefetchScalarGridSpec(
            num_scalar_prefetch=1, grid=(nblk,), in_specs=in_specs,
            out_specs=pl.BlockSpec((tr, ns), lambda i, s: (s[1] * nblk + i, 0))),
        out_shape=jax.ShapeDtypeStruct((2 * kh, ns), F32),
        compiler_params=_params(("parallel",)),
    )(place, q, q, q, q, p)


ADAM_LR, ADAM_B1, ADAM_B2, ADAM_EPS, ADAM_WD, ADAM_STEP = 0.001, 0.9, 0.999, 1e-08, 0.01, 10


def adamw(name, w, g, m, v, layer, prev=None, after=None):
    _, k, ns = w.shape
    nsp = g.shape[1]
    tr = _row_tile(k, nsp, mult=8, cap=3 << 20)

    def body(w_ref, g_ref, m_ref, v_ref, *rest):
        go_ref, d_ref, mo_ref, vo_ref = rest[-4:]
        gv = g_ref[:, :ns] if nsp != ns else g_ref[...]
        mn = ADAM_B1 * m_ref[...] + (1.0 - ADAM_B1) * gv
        vn = ADAM_B2 * v_ref[...] + (1.0 - ADAM_B2) * (gv * gv)
        m_hat = mn / (1.0 - ADAM_B1 ** ADAM_STEP)
        v_hat = vn / (1.0 - ADAM_B2 ** ADAM_STEP)
        go_ref[...] = gv
        d_ref[...] = -ADAM_LR * (m_hat / (jnp.sqrt(v_hat) + ADAM_EPS) + ADAM_WD * w_ref[...])
        mo_ref[...] = mn
        vo_ref[...] = vn

    blk = pl.BlockSpec((None, tr, ns), lambda i: (layer, i, 0))
    gblk = pl.BlockSpec((tr, nsp), lambda i: (i, 0))
    args, in_specs, aliases = [w, g, m, v], [blk, gblk, blk, blk], {}
    if prev is not None:
        args += list(prev)
        in_specs += [pl.BlockSpec(memory_space=pl.ANY)] * 4
        aliases = {4 + i: i for i in range(4)}
    if after is not None:
        args.append(after)
        in_specs.append(pl.BlockSpec(memory_space=pl.ANY))
    return pl.pallas_call(
        body, name=name, grid=(k // tr,), in_specs=in_specs, out_specs=[blk] * 4,
        out_shape=[jax.ShapeDtypeStruct(w.shape, F32)] * 4, input_output_aliases=aliases,
        compiler_params=_params(("parallel",)),
    )(*args)


WEIGHTS = ("ffn_a_norm", "ffn_a_w1", "ffn_a_w3", "ffn_a_w2", "mix_norm", "w_in", "q_a_norm", "w_uq", "kv_a_norm", "w_ukv",
           "q_norm", "k_norm", "gm_v_norm", "gm_ws", "gm_bs", "attn_out_norm", "gm_out_norm", "w_out", "ffn_b_norm",
           "ffn_b_w1", "ffn_b_w3", "ffn_b_w2", "ple_gate_norm", "w_ple_gate", "w_ple", "ple_norm")
_FF = FF_PAD - FF_SHARD
BIG = {"ffn_a_w1": _FF, "ffn_a_w3": _FF, "ffn_a_w2": _FF, "ffn_b_w1": _FF, "ffn_b_w3": _FF, "ffn_b_w2": _FF,
       "w_in": IN_SHARD_PAD - IN_SHARD, "w_uq": 0, "w_ukv": 0, "w_ple": 0, "w_out": 0, "w_ple_gate": 0}
TRANSPOSED = ("ffn_a_w1", "ffn_a_w3", "ffn_b_w1", "ffn_b_w3", "w_in")
SMALL = tuple(n for n in WEIGHTS if n not in BIG)
PACK = 1024


def _pack_small(d):
    parts = []
    for n in SMALL:
        flat = d[n].reshape(-1)
        parts.append(jnp.pad(flat, (0, (-flat.shape[0]) % PACK)))
    return jnp.concatenate(parts).reshape(-1, 128)


def _unpack_small(buf, like):
    flat = buf.reshape(-1)
    out, pos = {}, 0
    for n in SMALL:
        size = math.prod(like[n].shape)
        out[n] = flat[pos:pos + size].reshape(like[n].shape)
        pos += size + (-size) % PACK
    return out


def kernel(*args):
    names = (("x", "p", "positions") + WEIGHTS + ("loss_target",) + tuple("m_" + n for n in WEIGHTS)
             + tuple("v_" + n for n in WEIGHTS))
    a = dict(zip(names, args, strict=True))
    x, p, positions, target = a["x"][0], a["p"][:, 0], a["positions"][0], a["loss_target"][0]
    for n in TRANSPOSED:
        for pre in ("", "m_", "v_"):
            a[pre + n] = jnp.swapaxes(a[pre + n], 1, 2)

    place = jnp.stack([2 * lax.axis_index("x") + lax.axis_index("y"), lax.axis_index("c")]).astype(jnp.int32)
    small = {n: a[n] for n in SMALL}
    tabs = rope_tables(positions)
    order = {"l0a": ("ffn_a_w1", "ffn_a_w3"), "l0b": ("ffn_a_w2",), "l0c": ("w_in", "w_uq", "w_ukv", "w_out"),
             "l0d": ("ffn_b_w1", "ffn_b_w3", "ffn_b_w2", "w_ple_gate", "w_ple")}
    prep = lambda n, l, after: prep_shard(f"prep_{n}_{l}", a[n], l, BIG[n], place, after)
    flights, token = {}, None
    for tag, names in order.items():
        flights[tag] = gather_start(f"gather_{tag}_start", [prep(n, 0, token) for n in names], None)
        token = flights[tag][3]
    slots1 = []
    for n in BIG:
        slots1.append(prep(n, 1, slots1[-1] if slots1 else token))

    def arrive(tag, after):
        send, recv, flying, _ = flights[tag]
        arrived = gather_wait(f"gather_{tag}_wait", send, recv, flying, after)
        send, recv, flying, token = forward_start(f"forward_{tag}_start", arrived)
        return dict(zip(order[tag], gather_wait(f"forward_{tag}_wait", send, recv, flying, token)))

    G0 = arrive("l0a", slots1[-1])

    def before_down(s):
        G0.update(arrive("l0b", s))
        return G0["ffn_a_w2"]

    def before_mixer(h1, Gl, small_):
        G0.update(arrive("l0c", h1))
        flights["l1"] = gather_start("gather_l1_start", slots1, G0["w_uq"])
        return G0, {**small_, "mix_norm": small_["mix_norm"] + flights["l1"][3][0, 0]}

    def before_ffn_b(h2, Gl):
        G0.update(arrive("l0d", h2))
        return G0

    def before_ple(h3):
        send, recv, flying, _ = flights["l1"]
        flights["f1"] = forward_start("forward_l1_start", gather_wait("gather_l1_wait", send, recv, flying, h3))
        return flights["f1"][3]

    h, saved0 = layer_fwd(0, x, p[0], G0, small, tabs,
                          {"down_a": before_down, "mixer": before_mixer, "ffn_b": before_ffn_b, "ple": before_ple})
    G1 = dict(zip(BIG, gather_wait("forward_l1_wait", *flights["f1"][:3], h)))
    h, saved1 = layer_fwd(1, h, p[1], G1, small, tabs)
    dh, loss = loss_grad(h, target)
    loss = lax.psum(loss, ("x", "y", "c"))

    groups = {"l1": tuple(BIG),
              "l0a": ("w_ple_gate", "w_ple", "ffn_b_w1", "ffn_b_w3", "ffn_b_w2"),
              "l0b": ("w_in", "w_uq", "w_ukv", "w_out"),
              "l0c": ("ffn_a_w1", "ffn_a_w3", "ffn_a_w2")}
    crossing, started = [], {}

    def begin(tag, gl, after):
        ex = exchange_start(f"exchange_{tag}_start", [gl[n] for n in groups[tag]], after)
        crossing.append((tag, ex))
        return ex[4]

    def advance(after):
        tag, (send, recv, mine, lands, _) = crossing.pop()
        mine, theirs = exchange_wait(f"exchange_{tag}_wait", send, recv, mine, lands, after)
        parts = [add_sibling(f"add_sibling_{n}_{tag}", d, r, place) for n, d, r in zip(groups[tag], mine, theirs)]
        started[tag] = scatter_start(f"scatter_{tag}_start", parts)
        return started[tag][4]

    def sum_chips(tag, after):
        send, recv, parts, lands, _ = started[tag]
        parts, slabs = scatter_wait(f"scatter_{tag}_wait", send, recv, parts, lands, after)
        halves = [add_chips(f"add_chips_{n}_{tag}", q, pt, place) for n, q, pt in zip(groups[tag], slabs, parts)]
        return share_start(f"share_{tag}_start", halves)

    def shared(tag, sharing, after):
        send, recv, flying, _ = sharing
        return dict(zip(groups[tag], share_wait(f"share_{tag}_wait", send, recv, flying, after)))

    def update(names, full, layer, prev, after):
        outs = {}
        for n in names:
            outs[n] = adamw(f"adamw_{n}_{layer}", a[n], full[n], a["m_" + n], a["v_" + n], layer, prev and prev[n], after)
            after = outs[n][1]
        return outs, after

    grads = [None, None]
    dh, grads[1] = layer_bwd(1, dh, p[1], G1, small, tabs, saved1)
    token = begin("l1", grads[1], None)
    w0 = {**saved0[0], "ple_gate_norm": saved0[0]["ple_gate_norm"] + token[0, 0]}
    hooks = {"ffn_b": lambda gl, dh_: advance(dh_),
             "mixer": lambda gl, dh_: begin("l0a", gl, None),
             "ffn_a": lambda gl, dh_: begin("l0b", gl, advance(dh_)),
             "ffn_a_dw": lambda gl, dh_: advance(dh_)}
    gx, grads[0] = layer_bwd(0, dh, p[0], G0, small, tabs, (w0,) + saved0[1:], hooks)
    token = begin("l0c", grads[0], None)
    sharing = sum_chips("l1", token)
    full1 = shared("l1", sharing, advance(sharing[3]))
    outs1, behind = update(BIG, full1, 1, None, None)
    sharing_a = sum_chips("l0a", behind)
    sharing_b = sum_chips("l0b", sharing_a[3])
    full0 = shared("l0a", sharing_a, sharing_b[3])
    outs0, behind = update(groups["l0a"], full0, 0, outs1, None)
    sharing_c = sum_chips("l0c", behind)
    full0.update(shared("l0b", sharing_b, sharing_c[3]))
    outs, behind = update(groups["l0b"], full0, 0, outs1, None)
    outs0.update(outs)
    full0.update(shared("l0c", sharing_c, behind))
    outs0.update(update(groups["l0c"], full0, 0, outs1, None)[0])

    out_g, out_d, out_m, out_v = {}, {}, {}, {}
    for n in BIG:
        outs = [jnp.swapaxes(o, 1, 2) for o in outs0[n]] if n in TRANSPOSED else outs0[n]
        out_g[n], out_d[n], out_m[n], out_v[n] = outs

    gs = allreduce_small(_pack_small({n: jnp.stack([grads[0][n], grads[1][n]]) for n in SMALL}))
    rows = gs.shape[0] // 2
    packed = [_pack_small(d).reshape(2, rows, 128) for d in
              (small, {n: a["m_" + n] for n in SMALL}, {n: a["v_" + n] for n in SMALL})]
    gs = gs.reshape(2, rows, 128)
    sm = adamw("adamw_small_0", packed[0], gs[0], packed[1], packed[2], 0)
    sm = adamw("adamw_small_1", packed[0], gs[1], packed[1], packed[2], 1, sm)
    for dst, buf in zip((out_g, out_d, out_m, out_v), sm):
        dst.update(_unpack_small(buf, small))

    return (loss, gx[None], *[out_g[n] for n in WEIGHTS], *[out_d[n] for n in WEIGHTS],
            *[out_m[n] for n in WEIGHTS], *[out_v[n] for n in WEIGHTS])
```

```python
import math

import jax
import jax.numpy as jnp
from jax import lax
from jax.experimental import pallas as pl
from jax.experimental.pallas import tpu as pltpu

F32 = jnp.float32
BF16 = jnp.bfloat16

D_MODEL = 2048
D_FF = 5504
N_CHIPS = 4
FF_SHARD = D_FF // N_CHIPS
FF_PAD = 1408
FF_P = N_CHIPS * FF_PAD
HEADS = 8
QK_NOPE = 128
QK_ROPE = 64
QK_DIM = 192
HEAD_PAD = 256
V_DIM = 128
Q_RANK = 512
KV_RANK = 256
ATTN_W = 1024
GM_W = 1024
GROUPS = 8
CHUNK = 128
PLE_DIM = 256
IN_P = 3072
IN_SHARD = 720
IN_SHARD_PAD = 736
EPS = 1e-6
ROPE_BASE = 10000.0
ATTN_SCALE = QK_DIM ** -0.5
VMEM_LIMIT_BYTES = 56 * 1024 * 1024


def _params(sem):
    return pltpu.CompilerParams(dimension_semantics=sem, vmem_limit_bytes=VMEM_LIMIT_BYTES)


def _bf(x):
    return x if x.dtype == BF16 else x.astype(BF16)


def _sigmoid(x):
    return 1.0 / (1.0 + jnp.exp(-x))


_GELU_C = math.sqrt(2.0 / math.pi)


def _gelu(x):
    t = jnp.tanh(_GELU_C * (x + 0.044715 * x * x * x))
    return 0.5 * x * (1.0 + t)


def _gelu_grad(x):
    t = jnp.tanh(_GELU_C * (x + 0.044715 * x * x * x))
    return 0.5 * (1.0 + t) + 0.5 * x * (1.0 - t * t) * _GELU_C * (1.0 + 3 * 0.044715 * x * x)


def op_a(a, tm, tk):
    return (a, (tm, tk), lambda i, j, k: (i, k), 1)


def op_at(a, tm, tk):
    return (a, (tk, tm), lambda i, j, k: (k, i), 0)


def op_b(b, tk, tn):
    return (b, (tk, tn), lambda i, j, k: (k, j), 0)


def op_bt(b, tk, tn):
    return (b, (tn, tk), lambda i, j, k: (j, k), 1)


def op_b_cols(g, pre, tk, tn):
    nb = g.shape[-1] // tn
    none = (None,) * (1 + len(pre))
    return (g, none + (tk, tn), lambda i, j, k: (j // nb,) + tuple(pre) + (k, j % nb), 0)


def op_b_rows(g, pre, tk, tn, koff=0):
    nb = g.shape[-2] // tk
    none = (None,) * (1 + len(pre))
    return (g, none + (tk, tn), lambda i, j, k: ((k + koff) // nb,) + tuple(pre) + ((k + koff) % nb, j), 0)


def op_b_rows_t(g, pre, tk, tn):
    nb = g.shape[-2] // tn
    none = (None,) * (1 + len(pre))
    return (g, none + (tn, tk), lambda i, j, k: (j // nb,) + tuple(pre) + (j % nb, k), 1)


def tile_mn(x, tm, tn):
    return (x, (tm, tn), lambda i, j: (i, j))


def out_mn(M, N, tm, tn, dtype):
    return (jax.ShapeDtypeStruct((M, N), dtype), (tm, tn), lambda i, j: (i, j))


def out_cols(M, ns, tm, tn, dtype):
    nb = ns // tn
    return (jax.ShapeDtypeStruct((N_CHIPS, M, ns), dtype), (None, tm, tn), lambda i, j: (j // nb, i, j % nb))


def matmul(name, grid_mnk, a_ops, b_ops, terms, n_acc, extras, outs, epilogue, acc_tile, n_outer=False, after=None):
    gm, gn, gk = grid_mnk
    na, nb, nx, no = len(a_ops), len(b_ops), len(extras), len(outs)
    nd = 0 if after is None else 1

    def body(*refs):
        a_refs, b_refs = refs[:na], refs[na:na + nb]
        x_refs = refs[na + nb:na + nb + nx]
        o_refs = refs[na + nb + nx + nd:na + nb + nx + nd + no]
        acc_refs = refs[na + nb + nx + nd + no:]
        k = pl.program_id(2)

        @pl.when(k == 0)
        def _():
            for acc in acc_refs:
                acc[...] = jnp.zeros_like(acc)

        for ai, bi, ci in terms:
            dims = (((a_ops[ai][3],), (b_ops[bi][3],)), ((), ()))
            acc_refs[ci][...] += lax.dot_general(_bf(a_refs[ai][...]), _bf(b_refs[bi][...]), dims,
                                                 preferred_element_type=F32)

        @pl.when(k == gk - 1)
        def _():
            res = epilogue([acc[...] for acc in acc_refs], [x[...] for x in x_refs])
            for o, v in zip(o_refs, res):
                o[...] = v.astype(o.dtype)

    if n_outer:
        grid = (gn, gm, gk)

        def ix3(f):
            return lambda j, i, k: f(i, j, k)

        def ix2(f):
            return lambda j, i, k: f(i, j)
    else:
        grid = (gm, gn, gk)

        def ix3(f):
            return lambda i, j, k: f(i, j, k)

        def ix2(f):
            return lambda i, j, k: f(i, j)

    in_specs = [pl.BlockSpec(blk, ix3(f)) for (_, blk, f, _) in list(a_ops) + list(b_ops)]
    in_specs += [pl.BlockSpec(blk, ix2(f)) for (_, blk, f) in extras]
    in_specs += [pl.BlockSpec(memory_space=pl.ANY)] * nd
    out_specs = [pl.BlockSpec(blk, ix2(f)) for (_, blk, f) in outs]
    return pl.pallas_call(
        body,
        name=name,
        grid=grid,
        in_specs=in_specs,
        out_specs=out_specs,
        out_shape=[s for (s, _, _) in outs],
        scratch_shapes=[pltpu.VMEM(acc_tile, F32) for _ in range(n_acc)],
        compiler_params=_params(("parallel", "parallel", "arbitrary")),
    )(*[o[0] for o in a_ops], *[o[0] for o in b_ops], *[x[0] for x in extras], *([after] * nd))


def _acc0(accs, xs):
    return (accs[0],)


def mm_simple(name, a, b_op_fn, M, N, K, tm, tn, tk, out_dtype=F32, a_t=False, extras=(), epilogue=_acc0, outs=None,
              after=None):
    a_op = op_at(a, tm, tk) if a_t else op_a(a, tm, tk)
    outs = outs or [out_mn(M, N, tm, tn, out_dtype)]
    return matmul(name, (M // tm, N // tn, K // tk), [a_op], [b_op_fn(tk, tn)], [(0, 0, 0)], 1,
                  list(extras), outs, epilogue, (tm, tn), after=after)


def rms_fwd(name, x, g, width, col_blk=0, tm=512, out_dtype=BF16):
    T = x.shape[0]

    def body(x_ref, g_ref, o_ref):
        xv = x_ref[...].astype(F32)
        r = lax.rsqrt(jnp.mean(xv * xv, axis=-1, keepdims=True) + EPS)
        o_ref[...] = (xv * r * g_ref[...]).astype(o_ref.dtype)

    return pl.pallas_call(
        body, name=name, grid=(T // tm,),
        in_specs=[pl.BlockSpec((tm, width), lambda i: (i, col_blk)), pl.BlockSpec((1, width), lambda i: (0, 0))],
        out_specs=pl.BlockSpec((tm, width), lambda i: (i, 0)),
        out_shape=jax.ShapeDtypeStruct((T, width), out_dtype),
        compiler_params=_params(("parallel",)),
    )(x, g.reshape(1, width))


def rms_bwd(name, x, g, dn, width, col_blk=0, dres=None, tm=512, with_delta=False, bf16_copy=False):
    T = x.shape[0]
    has_res = dres is not None

    def body(*refs):
        x_ref, g_ref, dn_ref = refs[:3]
        pos = 3
        res_ref = None
        if has_res:
            res_ref = refs[pos]
            pos += 1
        dx_ref, dg_ref = refs[pos], refs[pos + 1]
        delta_ref = refs[pos + 2] if with_delta else None
        lo_ref = refs[-1] if bf16_copy else None
        i = pl.program_id(0)
        xv = x_ref[...].astype(F32)
        r = lax.rsqrt(jnp.mean(xv * xv, axis=-1, keepdims=True) + EPS)
        xh = xv * r
        d = dn_ref[...].astype(F32)
        gd = d * g_ref[...]
        dx = r * (gd - xh * jnp.mean(gd * xh, axis=-1, keepdims=True))
        if has_res:
            dx = dx + res_ref[...]
        dx_ref[...] = dx.astype(dx_ref.dtype)
        if bf16_copy:
            lo_ref[...] = dx.astype(BF16)
        part = jnp.sum(d * xh, axis=0, keepdims=True)

        @pl.when(i == 0)
        def _():
            dg_ref[...] = part

        @pl.when(i > 0)
        def _():
            dg_ref[...] += part

        if with_delta:
            for h in range(width // 128):
                sl = slice(h * 128, (h + 1) * 128)
                s = jnp.sum(dx[:, sl] * xv[:, sl], axis=-1, keepdims=True)
                delta_ref[:, sl] = jnp.broadcast_to(s, (tm, 128))

    in_specs = [pl.BlockSpec((tm, width), lambda i: (i, col_blk)), pl.BlockSpec((1, width), lambda i: (0, 0)),
                pl.BlockSpec((tm, width), lambda i: (i, 0))]
    args = [x, g.reshape(1, width), dn]
    if has_res:
        in_specs.append(pl.BlockSpec((tm, width), lambda i: (i, 0)))
        args.append(dres)
    out_specs = [pl.BlockSpec((tm, width), lambda i: (i, 0)), pl.BlockSpec((1, width), lambda i: (0, 0))]
    out_shape = [jax.ShapeDtypeStruct((T, width), F32), jax.ShapeDtypeStruct((1, width), F32)]
    if with_delta:
        out_specs.append(pl.BlockSpec((tm, width), lambda i: (i, 0)))
        out_shape.append(jax.ShapeDtypeStruct((T, width), F32))
    if bf16_copy:
        out_specs.append(pl.BlockSpec((tm, width), lambda i: (i, 0)))
        out_shape.append(jax.ShapeDtypeStruct((T, width), BF16))
    return pl.pallas_call(
        body, name=name, grid=(T // tm,), in_specs=in_specs, out_specs=out_specs, out_shape=out_shape,
        compiler_params=_params(("arbitrary",)),
    )(*args)


def ffn_fwd(tag, h, g, w1g, w3g, w2g, pre, w2_late=None):
    T = h.shape[0]
    n = rms_fwd(f"{tag}_rms", h, g, D_MODEL)
    tm, tn = 512, FF_PAD

    def up_epi(accs, xs):
        a1, a3 = accs
        return a1, a3, a1 * _sigmoid(a1) * a3

    a1, a3, s = matmul(
        f"{tag}_up", (T // tm, FF_P // tn, 1),
        [op_a(n, tm, D_MODEL)], [op_b_rows_t(w1g, pre, D_MODEL, tn), op_b_rows_t(w3g, pre, D_MODEL, tn)],
        [(0, 0, 0), (0, 1, 1)], 2, [],
        [out_mn(T, FF_P, tm, tn, BF16)] * 3, up_epi, (tm, tn), n_outer=True)

    if w2_late is not None:
        w2g = w2_late(s)
    tm2, tn2 = 1024, 1024
    (h_out,) = matmul(
        f"{tag}_down", (T // tm2, D_MODEL // tn2, N_CHIPS),
        [op_a(s, tm2, FF_PAD)], [op_b_rows(w2g, pre, FF_PAD, tn2)],
        [(0, 0, 0)], 1, [tile_mn(h, tm2, tn2)],
        [out_mn(T, D_MODEL, tm2, tn2, F32)], lambda accs, xs: (xs[0] + 0.5 * accs[0],), (tm2, tn2))
    return h_out, (n, a1, a3, s)


def ffn_bwd(tag, dh_out, dh_bf, h, g, res, w1g, w3g, w2g, pre, after=None, before_dw=None):
    n, a1, a3, s = res
    T = h.shape[0]
    tm, tn = 512, FF_PAD

    def act_epi(accs, xs):
        ds = 0.5 * accs[0]
        x1, x3 = xs[0].astype(F32), xs[1].astype(F32)
        sg = _sigmoid(x1)
        silu = x1 * sg
        return ds * x3 * (sg + silu * (1.0 - sg)), ds * silu

    da1, da3 = matmul(
        f"{tag}_dact", (T // tm, FF_P // tn, 1),
        [op_a(dh_bf, tm, D_MODEL)], [op_b_rows_t(w2g, pre, D_MODEL, tn)],
        [(0, 0, 0)], 1, [tile_mn(a1, tm, tn), tile_mn(a3, tm, tn)],
        [out_mn(T, FF_P, tm, tn, BF16)] * 2, act_epi, (tm, tn), n_outer=True, after=after)

    tm2, tn2 = 1024, 1024
    (dn,) = matmul(
        f"{tag}_dn", (T // tm2, D_MODEL // tn2, N_CHIPS),
        [op_a(da1, tm2, FF_PAD), op_a(da3, tm2, FF_PAD)],
        [op_b_rows(w1g, pre, FF_PAD, tn2), op_b_rows(w3g, pre, FF_PAD, tn2)],
        [(0, 0, 0), (1, 1, 0)], 1, [], [out_mn(T, D_MODEL, tm2, tn2, F32)], _acc0, (tm2, tn2))
    dh, dg, dh_lo = rms_bwd(f"{tag}_rms_bwd", h, g, dn, D_MODEL, dres=dh_out, bf16_copy=True)
    if before_dw is not None:
        after = before_dw(dh)

    tk, tn3 = T, 512

    def dw_t(nm, left, right, scale):
        (dw,) = matmul(
            f"{tag}_{nm}", (FF_P // FF_PAD, D_MODEL // tn3, T // tk),
            [op_at(left, FF_PAD, tk)], [op_b(right, tk, tn3)],
            [(0, 0, 0)], 1, [], [out_mn(FF_P, D_MODEL, FF_PAD, tn3, BF16)],
            lambda accs, xs: (scale * accs[0],), (FF_PAD, tn3), after=after)
        return dw

    dw2 = dw_t("dw2", s, dh_bf, 0.5)
    dw1 = dw_t("dw1", da1, n, 1.0)
    dw3 = dw_t("dw3", da3, n, 1.0)
    return dh, dh_lo, dg, dw1, dw3, dw2


def rope_tables(positions):
    inv_freq = ROPE_BASE ** (-jnp.arange(0, QK_ROPE, 2, dtype=F32) / QK_ROPE)
    ang = positions.astype(F32)[:, None] * inv_freq
    cos, sin = jnp.cos(ang), jnp.sin(ang)
    T = positions.shape[0]
    one, zero = jnp.ones((T, QK_NOPE), F32), jnp.zeros((T, 64), F32)
    z32, z128 = jnp.zeros((T, 32), F32), jnp.zeros((T, QK_NOPE), F32)
    c = jnp.concatenate([one, cos, cos, zero], axis=1)
    s1 = jnp.concatenate([z128, -sin, z32, zero], axis=1)
    s2 = jnp.concatenate([z128, z32, sin, zero], axis=1)
    return c, s1, s2


def _rope(y, c, s1, s2):
    return y * c + pltpu.roll(y, HEAD_PAD - 32, 1) * s1 + pltpu.roll(y, 32, 1) * s2


def _rope_t(d, c, s1, s2):
    return d * c + pltpu.roll(d * s1, 32, 1) + pltpu.roll(d * s2, HEAD_PAD - 32, 1)


def _head_norm(x):
    r = lax.rsqrt(jnp.sum(x * x, axis=-1, keepdims=True) * (1.0 / QK_DIM) + EPS)
    return x * r, r


def qk_prep_fwd(tag, q_raw, kk_raw, z_p, gq, gk, tabs, tm=256):
    T = q_raw.shape[0]
    c, s1, s2 = tabs

    def body(q_ref, k_ref, kr_ref, gq_ref, gk_ref, c_ref, s1_ref, s2_ref, qo_ref, ko_ref):
        cv, s1v, s2v = c_ref[...], s1_ref[...], s2_ref[...]
        kr = kr_ref[...]
        for h in range(HEADS):
            sl = slice(h * HEAD_PAD, (h + 1) * HEAD_PAD)
            xh, _ = _head_norm(q_ref[:, sl])
            qo_ref[:, sl] = (_rope(xh * gq_ref[...], cv, s1v, s2v) * ATTN_SCALE).astype(BF16)
            xh, _ = _head_norm(k_ref[:, sl] + kr)
            ko_ref[:, sl] = _rope(xh * gk_ref[...], cv, s1v, s2v).astype(BF16)

    row = lambda i: (i, 0)
    full = pl.BlockSpec((tm, HEADS * HEAD_PAD), row)
    tab = pl.BlockSpec((tm, HEAD_PAD), row)
    vec = pl.BlockSpec((1, HEAD_PAD), lambda i: (0, 0))
    return pl.pallas_call(
        body, name=f"{tag}_qk_prep", grid=(T // tm,),
        in_specs=[full, full, pl.BlockSpec((tm, HEAD_PAD), lambda i: (i, 3)), vec, vec, tab, tab, tab],
        out_specs=[full, full],
        out_shape=[jax.ShapeDtypeStruct((T, HEADS * HEAD_PAD), BF16)] * 2,
        compiler_params=_params(("parallel",)),
    )(q_raw, kk_raw, z_p, gq, gk, c, s1, s2)


def qk_prep_bwd(tag, dq_full, dk_full, q_raw, kk_raw, z_p, gq, gk, tabs, tm=256):
    T = q_raw.shape[0]
    c, s1, s2 = tabs

    def body(dq_ref, dk_ref, q_ref, k_ref, kr_ref, gq_ref, gk_ref, c_ref, s1_ref, s2_ref,
             dqr_ref, dkr_ref, dz_ref, dgq_ref, dgk_ref):
        i = pl.program_id(0)
        cv, s1v, s2v = c_ref[...], s1_ref[...], s2_ref[...]
        kr = kr_ref[...]
        lane = lax.broadcasted_iota(jnp.int32, (tm, HEAD_PAD), 1)
        slot = ((lane >= QK_NOPE) & (lane < QK_DIM)).astype(F32)

        def one(x, g, d):
            xh, r = _head_norm(x)
            dy = _rope_t(d, cv, s1v, s2v)
            gd = dy * g
            dx = r * (gd - xh * (jnp.sum(gd * xh, axis=-1, keepdims=True) * (1.0 / QK_DIM)))
            return dx, jnp.sum(dy * xh, axis=0, keepdims=True)

        dgq = jnp.zeros((1, HEAD_PAD), F32)
        dgk = jnp.zeros((1, HEAD_PAD), F32)
        dz = jnp.zeros((tm, HEAD_PAD), F32)
        for h in range(HEADS):
            sl = slice(h * HEAD_PAD, (h + 1) * HEAD_PAD)
            dx, dg = one(q_ref[:, sl], gq_ref[...], dq_ref[:, sl].astype(F32) * ATTN_SCALE)
            dqr_ref[:, sl] = dx
            dgq = dgq + dg
            dx, dg = one(k_ref[:, sl] + kr, gk_ref[...], dk_ref[:, sl].astype(F32))
            dkr_ref[:, sl] = dx
            dgk = dgk + dg
            dz = dz + dx
        dz_ref[...] = dz * slot

        @pl.when(i == 0)
        def _():
            dgq_ref[...] = dgq
            dgk_ref[...] = dgk

        @pl.when(i > 0)
        def _():
            dgq_ref[...] += dgq
            dgk_ref[...] += dgk

    row = lambda i: (i, 0)
    full = pl.BlockSpec((tm, HEADS * HEAD_PAD), row)
    tab = pl.BlockSpec((tm, HEAD_PAD), row)
    vec = pl.BlockSpec((1, HEAD_PAD), lambda i: (0, 0))
    return pl.pallas_call(
        body, name=f"{tag}_qk_prep_bwd", grid=(T // tm,),
        in_specs=[full, full, full, full, pl.BlockSpec((tm, HEAD_PAD), lambda i: (i, 3)), vec, vec, tab, tab, tab],
        out_specs=[full, full, tab, vec, vec],
        out_shape=[jax.ShapeDtypeStruct((T, HEADS * HEAD_PAD), F32)] * 2
        + [jax.ShapeDtypeStruct((T, HEAD_PAD), F32)] + [jax.ShapeDtypeStruct((1, HEAD_PAD), F32)] * 2,
        compiler_params=_params(("arbitrary",)),
    )(dq_full, dk_full, q_raw, kk_raw, z_p, gq, gk, c, s1, s2)


def attn_fwd(tag, q_full, k_full, vv, blk=512):
    T = q_full.shape[0]
    nb = T // blk
    neg = float(jnp.finfo(jnp.float32).min)

    def body(q_ref, k_ref, v_ref, o_ref, lse_ref, m_ref, l_ref, acc_ref):
        i = pl.program_id(1)
        m_ref[...] = jnp.full_like(m_ref, neg)
        l_ref[...] = jnp.zeros_like(l_ref)
        acc_ref[...] = jnp.zeros_like(acc_ref)
        q = q_ref[...]

        def step(j, masked):
            rows = pl.ds(pl.multiple_of(j * blk, blk), blk)
            s = lax.dot_general(q, k_ref[rows, :], (((1,), (1,)), ((), ())), preferred_element_type=F32)
            if masked:
                row = lax.broadcasted_iota(jnp.int32, (blk, blk), 0)
                col = lax.broadcasted_iota(jnp.int32, (blk, blk), 1)
                s = jnp.where(col <= row, s, neg)
            m_prev = m_ref[...]
            m_new = jnp.maximum(m_prev, jnp.max(s, axis=-1, keepdims=True))
            alpha = jnp.exp(m_prev - m_new)
            p = jnp.exp(s - m_new[:, :1])
            l_ref[...] = alpha * l_ref[...] + jnp.sum(p, axis=-1, keepdims=True)
            acc_ref[...] = alpha * acc_ref[...] + jnp.dot(p.astype(BF16), v_ref[rows, :], preferred_element_type=F32)
            m_ref[...] = m_new

        def off_diagonal(j, carry):
            step(j, False)
            return carry

        lax.fori_loop(0, i, off_diagonal, 0)
        step(i, True)
        o_ref[...] = acc_ref[...] / l_ref[...]
        lse_ref[...] = m_ref[...] + jnp.log(l_ref[...])

    return pl.pallas_call(
        body, name=f"{tag}_attn_fwd", grid=(HEADS, nb),
        in_specs=[pl.BlockSpec((blk, HEAD_PAD), lambda h, i: (i, h)),
                  pl.BlockSpec((T, HEAD_PAD), lambda h, i: (0, h)), pl.BlockSpec((T, V_DIM), lambda h, i: (0, h))],
        out_specs=[pl.BlockSpec((blk, V_DIM), lambda h, i: (i, h))] * 2,
        out_shape=[jax.ShapeDtypeStruct((T, ATTN_W), F32)] * 2,
        scratch_shapes=[pltpu.VMEM((blk, V_DIM), F32)] * 3,
        compiler_params=_params(("parallel", "parallel")),
    )(q_full, k_full, vv)


def attn_bwd(tag, q_full, k_full, vv, do, lse, delta, blk=512):
    T = q_full.shape[0]
    nb = T // blk
    neg = float(jnp.finfo(jnp.float32).min)

    def body(q_ref, k_ref, v_ref, do_ref, lse_ref, dl_ref, dq_ref, dk_ref, dv_ref, dk_acc, dv_acc):
        j = pl.program_id(1)

        @pl.when(j == 0)
        def _():
            dq_ref[...] = jnp.zeros_like(dq_ref)

        dk_acc[...] = jnp.zeros_like(dk_acc)
        dv_acc[...] = jnp.zeros_like(dv_acc)
        k, v = k_ref[...], v_ref[...]

        def step(i, masked):
            rows = pl.ds(pl.multiple_of(i * blk, blk), blk)
            q = q_ref[rows, :]
            s = lax.dot_general(q, k, (((1,), (1,)), ((), ())), preferred_element_type=F32)
            if masked:
                row = lax.broadcasted_iota(jnp.int32, (blk, blk), 0)
                col = lax.broadcasted_iota(jnp.int32, (blk, blk), 1)
                s = jnp.where(col <= row, s, neg)
            p = jnp.exp(s - lse_ref[rows, :1])
            dob = _bf(do_ref[rows, :])
            dv_acc[...] += lax.dot_general(p.astype(BF16), dob, (((0,), (0,)), ((), ())), preferred_element_type=F32)
            dp = lax.dot_general(dob, v, (((1,), (1,)), ((), ())), preferred_element_type=F32)
            ds = (p * (dp - dl_ref[rows, :1])).astype(BF16)
            dk_acc[...] += lax.dot_general(ds, q, (((0,), (0,)), ((), ())), preferred_element_type=F32)
            dq_ref[rows, :] += jnp.dot(ds, k, preferred_element_type=F32)

        def off_diagonal(i, carry):
            step(i, False)
            return carry

        step(j, True)
        lax.fori_loop(j + 1, nb, off_diagonal, 0)
        dk_ref[...] = dk_acc[...]
        dv_ref[...] = dv_acc[...]

    head = lambda h, j: (0, h)
    kv_ix = lambda h, j: (j, h)
    return pl.pallas_call(
        body, name=f"{tag}_attn_bwd", grid=(HEADS, nb),
        in_specs=[pl.BlockSpec((T, HEAD_PAD), head), pl.BlockSpec((blk, HEAD_PAD), kv_ix),
                  pl.BlockSpec((blk, V_DIM), kv_ix), pl.BlockSpec((T, V_DIM), head),
                  pl.BlockSpec((T, V_DIM), head), pl.BlockSpec((T, V_DIM), head)],
        out_specs=[pl.BlockSpec((T, HEAD_PAD), head),
                   pl.BlockSpec((blk, HEAD_PAD), kv_ix), pl.BlockSpec((blk, V_DIM), kv_ix)],
        out_shape=[jax.ShapeDtypeStruct((T, HEADS * HEAD_PAD), F32)] * 2 + [jax.ShapeDtypeStruct((T, ATTN_W), F32)],
        scratch_shapes=[pltpu.VMEM((blk, HEAD_PAD), F32), pltpu.VMEM((blk, V_DIM), F32)],
        compiler_params=_params(("parallel", "arbitrary")),
    )(q_full, k_full, vv, do, lse, delta)


def _gm_forward(u, v, gv, wc_ref, bb_ref, nchunk):
    ug = _gelu(u)
    vg = _gelu(v)
    rv = lax.rsqrt(jnp.mean(vg * vg, axis=-1, keepdims=True) + EPS)
    vhat = vg * rv
    vn = (vhat * gv).astype(BF16)
    gates = []
    for cidx in range(nchunk):
        rows = slice(cidx * CHUNK, (cidx + 1) * CHUNK)
        gates.append(jnp.concatenate(
            [jnp.dot(wc_ref[gidx], vn[rows, gidx * 128:(gidx + 1) * 128], preferred_element_type=F32) + bb_ref[gidx]
             for gidx in range(GROUPS)], axis=1))
    gate = jnp.concatenate(gates, axis=0)
    return ug, vhat, rv, vn, gate


def gmlp_fwd(tag, z_p, gv, gout, wc, bb, tm=256):
    T = z_p.shape[0]
    nchunk = tm // CHUNK

    def body(u_ref, v_ref, gv_ref, go_ref, wc_ref, bb_ref, o_ref):
        ug, _, _, _, gate = _gm_forward(u_ref[...], v_ref[...], gv_ref[...], wc_ref, bb_ref, nchunk)
        go = ug * gate
        ro = lax.rsqrt(jnp.mean(go * go, axis=-1, keepdims=True) + EPS)
        o_ref[...] = (go * ro * go_ref[...]).astype(BF16)

    vec = pl.BlockSpec((1, GM_W), lambda i: (0, 0))
    w3 = pl.BlockSpec((GROUPS, CHUNK, CHUNK), lambda i: (0, 0, 0))
    return pl.pallas_call(
        body, name=f"{tag}_gmlp_fwd", grid=(T // tm,),
        in_specs=[pl.BlockSpec((tm, GM_W), lambda i: (i, 1)), pl.BlockSpec((tm, GM_W), lambda i: (i, 2)), vec, vec, w3, w3],
        out_specs=pl.BlockSpec((tm, GM_W), lambda i: (i, 0)),
        out_shape=jax.ShapeDtypeStruct((T, GM_W), BF16),
        compiler_params=_params(("parallel",)),
    )(z_p, z_p, gv.reshape(1, GM_W), gout.reshape(1, GM_W), wc, bb)


def gmlp_bwd(tag, z_p, dmixed, gv, gout, wc, bb, tm=256):
    T = z_p.shape[0]
    nchunk = tm // CHUNK

    def body(u_ref, v_ref, dm_ref, gv_ref, go_ref, wc_ref, bb_ref, du_ref, dv_ref, dwc_ref, dbb_ref, dgv_ref, dgo_ref):
        i = pl.program_id(0)
        u, v = u_ref[...], v_ref[...]
        ug, vhat, rv, vn, gate = _gm_forward(u, v, gv_ref[...], wc_ref, bb_ref, nchunk)
        go = ug * gate
        ro = lax.rsqrt(jnp.mean(go * go, axis=-1, keepdims=True) + EPS)
        ohat = go * ro
        dm = dm_ref[...].astype(F32)
        dgo_part = jnp.sum(dm * ohat, axis=0, keepdims=True)
        doh = dm * go_ref[...]
        dgo = ro * (doh - ohat * jnp.mean(doh * ohat, axis=-1, keepdims=True))
        du_ref[...] = dgo * gate * _gelu_grad(u)
        dgate = dgo * ug
        dgb = dgate.astype(BF16)
        dvn_rows = []
        dwc_parts = []
        dbb_parts = []
        for gidx in range(GROUPS):
            cols = slice(gidx * 128, (gidx + 1) * 128)
            dw = jnp.zeros((CHUNK, CHUNK), F32)
            db = jnp.zeros((CHUNK, 128), F32)
            for cidx in range(nchunk):
                rows = slice(cidx * CHUNK, (cidx + 1) * CHUNK)
                dw = dw + lax.dot_general(dgb[rows, cols], vn[rows, cols], (((1,), (1,)), ((), ())),
                                          preferred_element_type=F32)
                db = db + dgate[rows, cols]
            dwc_parts.append(dw)
            dbb_parts.append(db)
        for cidx in range(nchunk):
            rows = slice(cidx * CHUNK, (cidx + 1) * CHUNK)
            dvn_rows.append(jnp.concatenate(
                [lax.dot_general(wc_ref[gidx], dgb[rows, gidx * 128:(gidx + 1) * 128], (((0,), (0,)), ((), ())),
                                 preferred_element_type=F32) for gidx in range(GROUPS)], axis=1))
        dvn = jnp.concatenate(dvn_rows, axis=0)
        dgv_part = jnp.sum(dvn * vhat, axis=0, keepdims=True)
        dvh = dvn * gv_ref[...]
        dvg = rv * (dvh - vhat * jnp.mean(dvh * vhat, axis=-1, keepdims=True))
        dv_ref[...] = dvg * _gelu_grad(v)

        @pl.when(i == 0)
        def _():
            for gidx in range(GROUPS):
                dwc_ref[gidx] = dwc_parts[gidx]
                dbb_ref[gidx] = dbb_parts[gidx]
            dgv_ref[...] = dgv_part
            dgo_ref[...] = dgo_part

        @pl.when(i > 0)
        def _():
            for gidx in range(GROUPS):
                dwc_ref[gidx] += dwc_parts[gidx]
                dbb_ref[gidx] += dbb_parts[gidx]
            dgv_ref[...] += dgv_part
            dgo_ref[...] += dgo_part

    vec = pl.BlockSpec((1, GM_W), lambda i: (0, 0))
    w3 = pl.BlockSpec((GROUPS, CHUNK, CHUNK), lambda i: (0, 0, 0))
    blk = pl.BlockSpec((tm, GM_W), lambda i: (i, 0))
    return pl.pallas_call(
        body, name=f"{tag}_gmlp_bwd", grid=(T // tm,),
        in_specs=[pl.BlockSpec((tm, GM_W), lambda i: (i, 1)), pl.BlockSpec((tm, GM_W), lambda i: (i, 2)),
                  pl.BlockSpec((tm, GM_W), lambda i: (i, 1)), vec, vec, w3, w3],
        out_specs=[blk, blk, w3, w3, vec, vec],
        out_shape=[jax.ShapeDtypeStruct((T, GM_W), F32)] * 2 + [jax.ShapeDtypeStruct((GROUPS, CHUNK, CHUNK), F32)] * 2
        + [jax.ShapeDtypeStruct((1, GM_W), F32)] * 2,
        compiler_params=_params(("arbitrary",)),
    )(z_p, z_p, dmixed, gv.reshape(1, GM_W), gout.reshape(1, GM_W), wc, bb)


def mixer_fwd(tag, h, w, tabs, wout_g, pre):
    T = h.shape[0]
    n2 = rms_fwd(f"{tag}_mix_rms", h, w["mix_norm"], D_MODEL)
    (z_p,) = mm_simple(f"{tag}_win", n2, lambda tk, tn: op_bt(w["w_in_pt"], tk, tn), T, IN_P, D_MODEL, 512, 1024, D_MODEL)
    cqn = rms_fwd(f"{tag}_cq_rms", z_p, w["q_a_norm"], Q_RANK, col_blk=0)
    ckvn = rms_fwd(f"{tag}_ckv_rms", z_p, w["kv_a_norm"], KV_RANK, col_blk=2)
    (q_raw,) = mm_simple(f"{tag}_wq", cqn, lambda tk, tn: op_b(w["wq_p"], tk, tn), T, 2048, Q_RANK, 512, 1024, Q_RANK)
    (kk_raw,) = mm_simple(f"{tag}_wk", ckvn, lambda tk, tn: op_b(w["wk_p"], tk, tn), T, 2048, KV_RANK, 512, 1024, KV_RANK)
    (vv,) = mm_simple(f"{tag}_wv", ckvn, lambda tk, tn: op_b(w["wv"], tk, tn), T, ATTN_W, KV_RANK, 512, 1024, KV_RANK,
                      out_dtype=BF16)
    q_full, k_full = qk_prep_fwd(tag, q_raw, kk_raw, z_p, w["gq_p"], w["gk_p"], tabs)
    a_out, lse = attn_fwd(tag, q_full, k_full, vv)
    mixed_a = rms_fwd(f"{tag}_ao_rms", a_out, w["attn_out_norm"], ATTN_W)
    mixed_g = gmlp_fwd(tag, z_p, w["gm_v_norm"], w["gm_out_norm"], w["wc"], w["bb"])
    tm, tn, tk = 1024, 1024, 512
    (h2,) = matmul(
        f"{tag}_wout", (T // tm, D_MODEL // tn, ATTN_W // tk),
        [op_a(mixed_a, tm, tk), op_a(mixed_g, tm, tk)],
        [op_b_rows(wout_g, pre, tk, tn), op_b_rows(wout_g, pre, tk, tn, koff=ATTN_W // tk)],
        [(0, 0, 0), (1, 1, 0)], 1, [tile_mn(h, tm, tn)], [out_mn(T, D_MODEL, tm, tn, F32)],
        lambda accs, xs: (xs[0] + accs[0],), (tm, tn))
    res = dict(n2=n2, z_p=z_p, cqn=cqn, ckvn=ckvn, q_raw=q_raw, kk_raw=kk_raw, vv=vv, q_full=q_full, k_full=k_full,
               a_out=a_out, lse=lse, mixed_a=mixed_a, mixed_g=mixed_g)
    return h2, res


def mixer_bwd(tag, dh2, dh2_bf, h, w, tabs, wout_g, pre, r, after=None):
    T = h.shape[0]
    g = {}
    (dmixed,) = mm_simple(f"{tag}_dmixed", dh2_bf, lambda tk, tn: op_b_rows_t(wout_g, pre, tk, tn), T, D_MODEL, D_MODEL,
                          1024, 512, D_MODEL, after=after)
    (dwo_a,) = mm_simple(f"{tag}_dwout_a", r["mixed_a"], lambda tk, tn: op_b(dh2_bf, tk, tn), ATTN_W, D_MODEL, T,
                         1024, 1024, T, a_t=True, out_dtype=BF16)
    (dwo_g,) = mm_simple(f"{tag}_dwout_g", r["mixed_g"], lambda tk, tn: op_b(dh2_bf, tk, tn), GM_W, D_MODEL, T,
                         1024, 1024, T, a_t=True, out_dtype=BF16)
    g["w_out"] = jnp.concatenate([dwo_a, dwo_g], axis=0)
    da_out, g["attn_out_norm"], delta = rms_bwd(f"{tag}_ao_rms_bwd", r["a_out"], w["attn_out_norm"], dmixed, ATTN_W,
                                                with_delta=True)
    dq_full, dk_full, dvv = attn_bwd(tag, r["q_full"], r["k_full"], r["vv"], da_out, r["lse"], delta)
    dq_raw, dkk_raw, dzkr, g["gq_p"], g["gk_p"] = qk_prep_bwd(tag, dq_full, dk_full, r["q_raw"], r["kk_raw"], r["z_p"],
                                                            w["gq_p"], w["gk_p"], tabs)
    (g["wq_p"],) = mm_simple(f"{tag}_dwq", r["cqn"], lambda tk, tn: op_b(dq_raw, tk, tn), Q_RANK, 2048, T, Q_RANK, 1024, 2048,
                             a_t=True, out_dtype=BF16)
    (g["wk_p"],) = mm_simple(f"{tag}_dwk", r["ckvn"], lambda tk, tn: op_b(dkk_raw, tk, tn), KV_RANK, 2048, T, KV_RANK, 1024,
                             2048, a_t=True, out_dtype=BF16)
    (g["wv"],) = mm_simple(f"{tag}_dwv", r["ckvn"], lambda tk, tn: op_b(dvv, tk, tn), KV_RANK, ATTN_W, T, KV_RANK, 1024, 2048,
                           a_t=True, out_dtype=BF16)
    (dcqn,) = mm_simple(f"{tag}_dcqn", dq_raw, lambda tk, tn: op_bt(w["wq_p"], tk, tn), T, Q_RANK, 2048, 512, Q_RANK, 2048)
    (dck1,) = mm_simple(f"{tag}_dckvn_k", dkk_raw, lambda tk, tn: op_bt(w["wk_p"], tk, tn), T, KV_RANK, 2048, 512, KV_RANK,
                        2048)
    (dckvn,) = mm_simple(f"{tag}_dckvn_v", dvv, lambda tk, tn: op_bt(w["wv"], tk, tn), T, KV_RANK, ATTN_W, 512, KV_RANK,
                         ATTN_W, extras=[tile_mn(dck1, 512, KV_RANK)], epilogue=lambda accs, xs: (accs[0] + xs[0],))
    dc_q, g["q_a_norm"] = rms_bwd(f"{tag}_cq_rms_bwd", r["z_p"], w["q_a_norm"], dcqn, Q_RANK, col_blk=0)
    dc_kv, g["kv_a_norm"] = rms_bwd(f"{tag}_ckv_rms_bwd", r["z_p"], w["kv_a_norm"], dckvn, KV_RANK, col_blk=2)
    du, dv, g["wc"], g["bb"], g["gm_v_norm"], g["gm_out_norm"] = gmlp_bwd(
        tag, r["z_p"], dmixed, w["gm_v_norm"], w["gm_out_norm"], w["wc"], w["bb"])
    dz_p = jnp.concatenate([dc_q, dc_kv, dzkr, du, dv], axis=1).astype(BF16)
    (g["w_in_pt"],) = mm_simple(f"{tag}_dwin", dz_p, lambda tk, tn: op_b(r["n2"], tk, tn), IN_P, D_MODEL, T, 1024, 1024, T,
                                a_t=True, out_dtype=BF16)
    (dn2,) = mm_simple(f"{tag}_dn2", dz_p, lambda tk, tn: op_b(w["w_in_pt"], tk, tn), T, D_MODEL, IN_P, 512, 1024, IN_P)
    dh1, g["mix_norm"], dh1_bf = rms_bwd(f"{tag}_mix_rms_bwd", h, w["mix_norm"], dn2, D_MODEL, dres=dh2, bf16_copy=True)
    return dh1, dh1_bf, g


def ple_fwd(tag, h3, p_l, w, wpg_g, wple_g, pre):
    T = h3.shape[0]
    (pw,) = mm_simple(f"{tag}_wple", p_l, lambda tk, tn: op_b_cols(wple_g, pre, tk, tn), T, D_MODEL, PLE_DIM, 512, 512,
                      PLE_DIM)
    e = rms_fwd(f"{tag}_ple_rms", pw, w["ple_norm"], D_MODEL, out_dtype=F32)
    n4 = rms_fwd(f"{tag}_pg_rms", h3, w["ple_gate_norm"], D_MODEL)

    def epi(accs, xs):
        gt = _sigmoid(accs[0])
        return xs[0] + gt * xs[1], gt

    tm, tn, tk = 1024, 1024, 512
    h4, gate = matmul(
        f"{tag}_wpg", (T // tm, D_MODEL // tn, D_MODEL // tk),
        [op_a(n4, tm, tk)], [op_b_rows(wpg_g, pre, tk, tn)], [(0, 0, 0)], 1,
        [tile_mn(h3, tm, tn), tile_mn(e, tm, tn)],
        [out_mn(T, D_MODEL, tm, tn, F32), out_mn(T, D_MODEL, tm, tn, BF16)], epi, (tm, tn))
    return h4, dict(pw=pw, e=e, n4=n4, gate=gate)


def ple_bwd(tag, dh4, h3, p_l, w, wpg_g, wple_g, pre, r, tm=256):
    T = h3.shape[0]

    def act_body(d_ref, g_ref, e_ref, dpre_ref, de_ref):
        d, gt = d_ref[...], g_ref[...].astype(F32)
        dpre_ref[...] = (d * e_ref[...] * gt * (1.0 - gt)).astype(BF16)
        de_ref[...] = d * gt

    blk = pl.BlockSpec((tm, D_MODEL), lambda i: (i, 0))
    dpre, de = pl.pallas_call(
        act_body, name=f"{tag}_ple_act_bwd", grid=(T // tm,), in_specs=[blk, blk, blk], out_specs=[blk, blk],
        out_shape=[jax.ShapeDtypeStruct((T, D_MODEL), BF16), jax.ShapeDtypeStruct((T, D_MODEL), F32)],
        compiler_params=_params(("parallel",)),
    )(dh4, r["gate"], r["e"])
    g = {}
    (g["w_ple_gate"],) = mm_simple(f"{tag}_dwpg", r["n4"], lambda tk, tn: op_b(dpre, tk, tn), D_MODEL, D_MODEL, T,
                                   1024, 1024, T, a_t=True, out_dtype=BF16)
    (dn4,) = mm_simple(f"{tag}_dn4", dpre, lambda tk, tn: op_b_rows_t(wpg_g, pre, tk, tn), T, D_MODEL, D_MODEL, 1024, 512,
                       D_MODEL)
    dh3, g["ple_gate_norm"], dh3_bf = rms_bwd(f"{tag}_pg_rms_bwd", h3, w["ple_gate_norm"], dn4, D_MODEL, dres=dh4,
                                              bf16_copy=True)
    dpw, g["ple_norm"] = rms_bwd(f"{tag}_ple_rms_bwd", r["pw"], w["ple_norm"], de, D_MODEL)
    (g["w_ple"],) = mm_simple(f"{tag}_dwple", p_l, lambda tk, tn: op_b(dpw, tk, tn), PLE_DIM, D_MODEL, T, PLE_DIM, 512, T,
                              a_t=True, outs=[out_cols(PLE_DIM, 512, PLE_DIM, 512, BF16)])
    return dh3, dh3_bf, g


def loss_grad(y, target, tm=256):
    T = y.shape[0]

    def body(y_ref, t_ref, dy_ref, l_ref):
        i = pl.program_id(0)
        d = y_ref[...] - t_ref[...]
        dy_ref[...] = d * (1.0 / D_MODEL)
        part = jnp.sum((d * d).reshape(tm // 8, 8, D_MODEL), axis=0)

        @pl.when(i == 0)
        def _():
            l_ref[...] = part

        @pl.when(i > 0)
        def _():
            l_ref[...] += part

    blk = pl.BlockSpec((tm, D_MODEL), lambda i: (i, 0))
    dy, part = pl.pallas_call(
        body, name="loss_grad", grid=(T // tm,), in_specs=[blk, blk],
        out_specs=[blk, pl.BlockSpec((8, D_MODEL), lambda i: (0, 0))],
        out_shape=[jax.ShapeDtypeStruct((T, D_MODEL), F32), jax.ShapeDtypeStruct((8, D_MODEL), F32)],
        compiler_params=_params(("arbitrary",)),
    )(y, target)
    return dy, 0.5 * jnp.sum(part) / D_MODEL


def _unshard_cols(g_l):
    return g_l.transpose(1, 0, 2).reshape(g_l.shape[1], -1)


def _shard_cols(w):
    return w.reshape(w.shape[0], N_CHIPS, -1).transpose(1, 0, 2)


def layer_weights(l, Gl, small):
    w = {k: small[k][l] for k in ("mix_norm", "q_a_norm", "kv_a_norm", "gm_v_norm", "attn_out_norm", "gm_out_norm",
                                  "ple_gate_norm", "ple_norm")}
    wint = Gl["w_in"][:, :IN_SHARD].reshape(-1, D_MODEL)
    z = lambda n: jnp.zeros((n, D_MODEL), BF16)
    w["w_in_pt"] = jnp.concatenate([wint[:768], z(128), wint[768:832], z(64), wint[832:]], axis=0)
    wuq = _unshard_cols(Gl["w_uq"]).reshape(Q_RANK, HEADS, QK_DIM)
    w["wq_p"] = jnp.pad(wuq, ((0, 0), (0, 0), (0, HEAD_PAD - QK_DIM))).reshape(Q_RANK, HEADS * HEAD_PAD)
    wukv = _unshard_cols(Gl["w_ukv"]).reshape(KV_RANK, HEADS, QK_NOPE + V_DIM)
    w["wk_p"] = jnp.pad(wukv[:, :, :QK_NOPE], ((0, 0), (0, 0), (0, HEAD_PAD - QK_NOPE))).reshape(KV_RANK, HEADS * HEAD_PAD)
    w["wv"] = wukv[:, :, QK_NOPE:].reshape(KV_RANK, ATTN_W)
    w["gq_p"] = jnp.pad(small["q_norm"][l], (0, HEAD_PAD - QK_DIM)).reshape(1, HEAD_PAD)
    w["gk_p"] = jnp.pad(small["k_norm"][l], (0, HEAD_PAD - QK_DIM)).reshape(1, HEAD_PAD)
    tril = jnp.tril(jnp.ones((CHUNK, CHUNK), dtype=bool))
    w["wc"] = jnp.where(tril[None], small["gm_ws"][l], 0.0).astype(BF16)
    w["bb"] = jnp.broadcast_to(small["gm_bs"][l][:, :, None], (GROUPS, CHUNK, 128)).astype(F32)
    return w


def mixer_grads_to_shards(g):
    out = {}
    dwint = g["w_in_pt"]
    dwint = jnp.concatenate([dwint[:768], dwint[896:960], dwint[1024:]], axis=0).reshape(N_CHIPS, IN_SHARD, D_MODEL)
    out["w_in"] = jnp.pad(dwint, ((0, 0), (0, IN_SHARD_PAD - IN_SHARD), (0, 0)))
    dwuq = g["wq_p"].reshape(Q_RANK, HEADS, HEAD_PAD)[:, :, :QK_DIM].reshape(Q_RANK, HEADS * QK_DIM)
    out["w_uq"] = _shard_cols(dwuq)
    dwukv = jnp.concatenate([g["wk_p"].reshape(KV_RANK, HEADS, HEAD_PAD)[:, :, :QK_NOPE],
                             g["wv"].reshape(KV_RANK, HEADS, V_DIM)], axis=-1).reshape(KV_RANK, HEADS * (QK_NOPE + V_DIM))
    out["w_ukv"] = _shard_cols(dwukv)
    out["w_out"] = g["w_out"].reshape(N_CHIPS, D_MODEL // N_CHIPS, D_MODEL)
    out["q_norm"] = g["gq_p"][0, :QK_DIM]
    out["k_norm"] = g["gk_p"][0, :QK_DIM]
    tril = jnp.tril(jnp.ones((CHUNK, CHUNK), dtype=bool))
    out["gm_ws"] = jnp.where(tril[None], g["wc"], 0.0)
    out["gm_bs"] = jnp.sum(g["bb"], axis=-1)
    for k in ("mix_norm", "q_a_norm", "kv_a_norm", "gm_v_norm", "attn_out_norm", "gm_out_norm"):
        out[k] = g[k][0]
    return out


def layer_fwd(l, h, p_l, Gl, small, tabs, before=None):
    before = before or {}
    h1, r_a = ffn_fwd(f"l{l}a", h, small["ffn_a_norm"][l], Gl["ffn_a_w1"], Gl["ffn_a_w3"], Gl.get("ffn_a_w2"), (),
                      before.get("down_a"))
    if "mixer" in before:
        Gl, small = before["mixer"](h1, Gl, small)
    w = layer_weights(l, Gl, small)
    h2, r_m = mixer_fwd(f"l{l}", h1, w, tabs, Gl["w_out"], ())
    if "ffn_b" in before:
        Gl = before["ffn_b"](h2, Gl)
    h3, r_b = ffn_fwd(f"l{l}b", h2, small["ffn_b_norm"][l], Gl["ffn_b_w1"], Gl["ffn_b_w3"], Gl["ffn_b_w2"], ())
    if "ple" in before:
        w = {**w, "ple_norm": w["ple_norm"] + before["ple"](h3)[0, 0]}
    h4, r_p = ple_fwd(f"l{l}", h3, p_l, w, Gl["w_ple_gate"], Gl["w_ple"], ())
    return h4, (w, h, h1, h2, h3, r_a, r_m, r_b, r_p)


def layer_bwd(l, dh, p_l, Gl, small, tabs, saved, before=None):
    w, h0, h1, h2, h3, r_a, r_m, r_b, r_p = saved
    slabs = lambda d: d.reshape(N_CHIPS, FF_PAD, D_MODEL)
    hook = lambda block: before[block](gl, dh) if before and block in before else None
    gl = {}
    dh, dh_bf, g_p = ple_bwd(f"l{l}", dh, h3, p_l, w, Gl["w_ple_gate"], Gl["w_ple"], (), r_p)
    gl["w_ple_gate"] = g_p["w_ple_gate"].reshape(N_CHIPS, D_MODEL // N_CHIPS, D_MODEL)
    gl["w_ple"] = g_p["w_ple"]
    gl["ple_gate_norm"], gl["ple_norm"] = g_p["ple_gate_norm"][0], g_p["ple_norm"][0]
    dh, dh_bf, dg, dw1, dw3, dw2 = ffn_bwd(f"l{l}b", dh, dh_bf, h2, small["ffn_b_norm"][l], r_b,
                                           Gl["ffn_b_w1"], Gl["ffn_b_w3"], Gl["ffn_b_w2"], (), hook("ffn_b"))
    gl["ffn_b_norm"] = dg[0]
    gl["ffn_b_w1"], gl["ffn_b_w3"], gl["ffn_b_w2"] = slabs(dw1), slabs(dw3), slabs(dw2)
    dh, dh_bf, g_m = mixer_bwd(f"l{l}", dh, dh_bf, h1, w, tabs, Gl["w_out"], (), r_m, hook("mixer"))
    gl.update(mixer_grads_to_shards(g_m))
    last_dw = (lambda dh_: before["ffn_a_dw"](gl, dh_)) if before and "ffn_a_dw" in before else None
    dh, _, dg, dw1, dw3, dw2 = ffn_bwd(f"l{l}a", dh, dh_bf, h0, small["ffn_a_norm"][l], r_a,
                                       Gl["ffn_a_w1"], Gl["ffn_a_w3"], Gl["ffn_a_w2"], (), hook("ffn_a"), last_dw)
    gl["ffn_a_norm"] = dg[0]
    gl["ffn_a_w1"], gl["ffn_a_w3"], gl["ffn_a_w2"] = slabs(dw1), slabs(dw3), slabs(dw2)
    return dh, gl


MESH = pl.DeviceIdType.MESH
HBM_SPEC = pl.BlockSpec(memory_space=pltpu.HBM)


def _place():
    x, y, c = lax.axis_index("x"), lax.axis_index("y"), lax.axis_index("c")
    others = [(1 - x, y), (x, 1 - y), (1 - x, 1 - y)]
    return x, y, c, 2 * x + y, others


def prep_shard(name, w, layer, rows_pad, place, after=None):
    _, ks, n = w.shape
    ksp = ks + rows_pad
    tc = 512 if n % 512 == 0 else n
    deps = [] if after is None else [after]

    def body(place_ref, x_ref, *rest):
        o_ref = rest[-1]
        o_ref[:ks] = x_ref[...].astype(BF16)
        if rows_pad:
            o_ref[ks:] = jnp.zeros((rows_pad, tc), BF16)

    return pl.pallas_call(
        body, name=name,
        grid_spec=pltpu.PrefetchScalarGridSpec(
            num_scalar_prefetch=1, grid=(n // tc,),
            in_specs=[pl.BlockSpec((None, ks, tc), lambda i, s: (layer, 0, i))] + [ANY_SPEC] * len(deps),
            out_specs=pl.BlockSpec((None, ksp, tc), lambda i, s: (s[0], 0, i))),
        out_shape=jax.ShapeDtypeStruct((N_CHIPS, ksp, n), BF16),
        compiler_params=_params(("parallel",)),
    )(place, w, *deps)


SEM_SPEC = pl.BlockSpec(memory_space=pltpu.SEMAPHORE)
ANY_SPEC = pl.BlockSpec(memory_space=pl.ANY)
DATAFLOW = pltpu.SideEffectType.DATAFLOW_SIDE_EFFECTING


def _hbm(x):
    return pltpu.with_memory_space_constraint(x, pltpu.HBM)


def _start_call(name, slots, after, issue):
    n = len(slots)
    deps = [] if after is None else [after]
    nd = len(deps)

    def body(*refs):
        issue(refs[n + nd + 2:2 * n + nd + 2], refs[n + nd], refs[n + nd + 1])
        token = refs[2 * n + nd + 2]
        token[...] = jnp.zeros_like(token)

    outs = pl.pallas_call(
        body, name=name,
        in_specs=[HBM_SPEC] * n + [ANY_SPEC] * nd,
        out_specs=(SEM_SPEC, SEM_SPEC, *([HBM_SPEC] * n), pl.BlockSpec(memory_space=pltpu.VMEM)),
        out_shape=(pltpu.SemaphoreType.DMA((n,)), pltpu.SemaphoreType.DMA((n,)),
                   *[pltpu.HBM(s.shape, s.dtype) for s in slots], jax.ShapeDtypeStruct((8, 128), F32)),
        input_output_aliases={w: w + 2 for w in range(n)},
        compiler_params=pltpu.CompilerParams(has_side_effects=DATAFLOW),
    )(*[_hbm(s) for s in slots], *deps)
    return outs[0], outs[1], list(outs[2:2 + n]), outs[2 + n]


def gather_start(name, slots, after):
    def issue(g_refs, send, recv):
        x, y, c, jme, others = _place()
        for w in range(len(slots)):
            kh = slots[w].shape[1] // 2
            mine = g_refs[w].at[jme, pl.ds(c * kh, kh)]
            for (px, py) in others:
                pltpu.make_async_remote_copy(src_ref=mine, dst_ref=mine, send_sem=send.at[w], recv_sem=recv.at[w],
                                             device_id=(px, py, c), device_id_type=MESH).start()

    return _start_call(name, slots, after, issue)


def forward_start(name, slots):
    def issue(g_refs, send, recv):
        x, y, c, _, others = _place()
        for w in range(len(slots)):
            kh = slots[w].shape[1] // 2
            for (px, py) in others:
                blk = g_refs[w].at[2 * px + py, pl.ds(c * kh, kh)]
                pltpu.make_async_remote_copy(src_ref=blk, dst_ref=blk, send_sem=send.at[w], recv_sem=recv.at[w],
                                             device_id=(x, y, 1 - c), device_id_type=MESH).start()

    return _start_call(name, slots, None, issue)


def share_start(name, fulls):
    def issue(o_refs, send, recv):
        x, y, c, _, _ = _place()
        for w in range(len(fulls)):
            kh = fulls[w].shape[0] // 2
            half = o_refs[w].at[pl.ds(c * kh, kh)]
            pltpu.make_async_remote_copy(src_ref=half, dst_ref=half, send_sem=send.at[w], recv_sem=recv.at[w],
                                         device_id=(x, y, 1 - c), device_id_type=MESH).start()

    return _start_call(name, fulls, None, issue)


def share_wait(name, send, recv, flying, after):
    return _wait_call(name, send, recv, flying, after, lambda r: r.at[pl.ds(0, r.shape[0] // 2)])


def gather_wait(name, send, recv, flying, after):
    return _wait_call(name, send, recv, flying, after, lambda r: r.at[pl.ds(0, 3), pl.ds(0, r.shape[1] // 2)])


def _wait_call(name, send, recv, flying, after, landed):
    n = len(flying)

    def body(*refs):
        send_ref, recv_ref = refs[n], refs[n + 1]
        g_refs = refs[n + 3:]
        x, y, c, _, _ = _place()
        for w in range(n):
            cp = pltpu.make_async_remote_copy(src_ref=landed(g_refs[w]), dst_ref=landed(g_refs[w]),
                                              send_sem=send_ref.at[w], recv_sem=recv_ref.at[w],
                                              device_id=(x, y, 1 - c), device_id_type=MESH)
            cp.wait_send()
            cp.wait_recv()

    return pl.pallas_call(
        body, name=name,
        in_specs=[HBM_SPEC] * n + [SEM_SPEC, SEM_SPEC, ANY_SPEC],
        out_specs=[HBM_SPEC] * n,
        out_shape=[pltpu.HBM(s.shape, s.dtype) for s in flying],
        input_output_aliases={w: w for w in range(n)},
        compiler_params=pltpu.CompilerParams(has_side_effects=DATAFLOW),
    )(*flying, send, recv, after)


def _send_start(name, srcs, land_shapes, issue, after):
    n = len(srcs)
    deps = [] if after is None else [after]
    nd = len(deps)

    def body(*refs):
        base = 2 * n + nd
        issue(refs[base + 2:base + 2 + n], refs[base + 2 + n:base + 2 + 2 * n], refs[base], refs[base + 1])
        token = refs[base + 2 + 2 * n]
        token[...] = jnp.zeros_like(token)

    lands = [_hbm(lax.empty(shape, s.dtype)) for shape, s in zip(land_shapes, srcs)]
    outs = pl.pallas_call(
        body, name=name,
        in_specs=[HBM_SPEC] * (2 * n) + [ANY_SPEC] * nd,
        out_specs=(SEM_SPEC, SEM_SPEC, *([HBM_SPEC] * (2 * n)), pl.BlockSpec(memory_space=pltpu.VMEM)),
        out_shape=(pltpu.SemaphoreType.DMA((n,)), pltpu.SemaphoreType.DMA((n,)),
                   *[pltpu.HBM(s.shape, s.dtype) for s in srcs], *[pltpu.HBM(l.shape, l.dtype) for l in lands],
                   jax.ShapeDtypeStruct((8, 128), F32)),
        input_output_aliases={w: w + 2 for w in range(2 * n)},
        compiler_params=pltpu.CompilerParams(has_side_effects=DATAFLOW),
    )(*[_hbm(s) for s in srcs], *lands, *deps)
    return outs[0], outs[1], list(outs[2:2 + n]), list(outs[2 + n:2 + 2 * n]), outs[2 + 2 * n]


def _send_wait(name, send, recv, srcs, lands, after, landed):
    n = len(srcs)

    def body(*refs):
        send_ref, recv_ref = refs[2 * n], refs[2 * n + 1]
        q_refs = refs[3 * n + 3:]
        x, y, c, _, _ = _place()
        for w in range(n):
            cp = pltpu.make_async_remote_copy(src_ref=landed(q_refs[w]), dst_ref=landed(q_refs[w]), send_sem=send_ref.at[w],
                                              recv_sem=recv_ref.at[w], device_id=(x, y, 1 - c), device_id_type=MESH)
            cp.wait_send()
            cp.wait_recv()

    outs = pl.pallas_call(
        body, name=name,
        in_specs=[HBM_SPEC] * (2 * n) + [SEM_SPEC, SEM_SPEC, ANY_SPEC],
        out_specs=[HBM_SPEC] * (2 * n),
        out_shape=[pltpu.HBM(a.shape, a.dtype) for a in list(srcs) + list(lands)],
        input_output_aliases={w: w for w in range(2 * n)},
        compiler_params=pltpu.CompilerParams(has_side_effects=DATAFLOW),
    )(*srcs, *lands, send, recv, after)
    return list(outs[:n]), list(outs[n:])


def exchange_start(name, grads, after):
    def issue(d_refs, r_refs, send, recv):
        x, y, c, _, _ = _place()
        for w in range(len(grads)):
            half = grads[w].shape[1] // 2
            pltpu.make_async_remote_copy(
                src_ref=d_refs[w].at[pl.ds(0, N_CHIPS), pl.ds((1 - c) * half, half)], dst_ref=r_refs[w],
                send_sem=send.at[w], recv_sem=recv.at[w], device_id=(x, y, 1 - c), device_id_type=MESH).start()

    return _send_start(name, grads, [(N_CHIPS, g.shape[1] // 2, g.shape[2]) for g in grads], issue, after)


def exchange_wait(name, send, recv, grads, lands, after):
    return _send_wait(name, send, recv, grads, lands, after, lambda r: r)


def scatter_start(name, parts):
    def issue(p_refs, q_refs, send, recv):
        x, y, c, jme, others = _place()
        for w in range(len(parts)):
            for (px, py) in others:
                pltpu.make_async_remote_copy(
                    src_ref=p_refs[w].at[2 * px + py], dst_ref=q_refs[w].at[jme], send_sem=send.at[w], recv_sem=recv.at[w],
                    device_id=(px, py, c), device_id_type=MESH).start()

    return _send_start(name, parts, [p.shape for p in parts], issue, None)


def scatter_wait(name, send, recv, parts, lands, after):
    return _send_wait(name, send, recv, parts, lands, after, lambda r: r.at[pl.ds(0, 3)])


def allreduce_small(v):
    R = v.shape[0]

    def body(v_ref, o_ref, sib_ref, mine_ref, all_ref, d_send, d_recv, i_send, i_recv):
        x, y, c, jme, others = _place()
        swap = pltpu.make_async_remote_copy(src_ref=v_ref, dst_ref=sib_ref, send_sem=d_send, recv_sem=d_recv,
                                            device_id=(x, y, 1 - c), device_id_type=MESH)
        swap.start()
        swap.wait()
        mine_ref[...] = v_ref[...] + sib_ref[...]
        for (px, py) in others:
            pltpu.make_async_remote_copy(src_ref=mine_ref, dst_ref=all_ref.at[jme], send_sem=i_send, recv_sem=i_recv,
                                         device_id=(px, py, c), device_id_type=MESH).start()
        three = all_ref.at[pl.ds(0, 3)]
        wait3 = pltpu.make_async_remote_copy(src_ref=three, dst_ref=three, send_sem=i_send, recv_sem=i_recv,
                                             device_id=(x, y, c), device_id_type=MESH)
        wait3.wait_recv()
        wait3.wait_send()
        all_ref[jme] = mine_ref[...]
        o_ref[...] = ((all_ref[0] + all_ref[1]) + all_ref[2]) + all_ref[3]

    vm = pl.BlockSpec(memory_space=pltpu.VMEM)
    return pl.pallas_call(
        body, name="allreduce_small", in_specs=[vm], out_specs=vm,
        out_shape=jax.ShapeDtypeStruct(v.shape, F32),
        scratch_shapes=[pltpu.VMEM((R, 128), F32), pltpu.VMEM((R, 128), F32), pltpu.VMEM((N_CHIPS, R, 128), F32),
                        pltpu.SemaphoreType.DMA, pltpu.SemaphoreType.DMA, pltpu.SemaphoreType.DMA, pltpu.SemaphoreType.DMA],
        compiler_params=pltpu.CompilerParams(vmem_limit_bytes=VMEM_LIMIT_BYTES),
    )(v)


def _row_tile(rows, width, mult=16, cap=3 << 20):
    best = rows
    for t in range(mult, rows + 1, mult):
        if rows % t == 0 and t * width * 4 <= cap:
            best = t
    return best


def add_sibling(name, mine, theirs, place):
    _, kh, ns = theirs.shape
    tr = _row_tile(kh, ns)
    nblk = kh // tr

    def body(place_ref, a_ref, b_ref, o_ref):
        o_ref[...] = (a_ref[...].astype(F32) + b_ref[...].astype(F32)).astype(BF16)

    return pl.pallas_call(
        body, name=name,
        grid_spec=pltpu.PrefetchScalarGridSpec(
            num_scalar_prefetch=1, grid=(N_CHIPS, nblk),
            in_specs=[pl.BlockSpec((None, tr, ns), lambda j, i, s: (j, s[1] * nblk + i, 0)),
                      pl.BlockSpec((None, tr, ns), lambda j, i, s: (j, i, 0))],
            out_specs=pl.BlockSpec((None, tr, ns), lambda j, i, s: (j, i, 0))),
        out_shape=jax.ShapeDtypeStruct(theirs.shape, BF16),
        compiler_params=_params(("parallel", "parallel")),
    )(place, mine, theirs)


def add_chips(name, q, p, place):
    _, kh, ns = q.shape
    tr = _row_tile(kh, ns)
    nblk = kh // tr

    def body(place_ref, *refs):
        q_refs, own_ref, o_ref = refs[:N_CHIPS], refs[N_CHIPS], refs[-1]
        jme = place_ref[0]
        tot = None
        for j in range(N_CHIPS):
            v = jnp.where(jme == j, own_ref[...], q_refs[j][...]).astype(F32)
            tot = v if tot is None else tot + v
        o_ref[...] = tot

    def q_ix(j):
        return lambda i, s: (jnp.where(s[0] == j, (j + 1) % N_CHIPS, j), i, 0)

    in_specs = [pl.BlockSpec((None, tr, ns), q_ix(j)) for j in range(N_CHIPS)]
    in_specs.append(pl.BlockSpec((None, tr, ns), lambda i, s: (s[0], i, 0)))
    return pl.pallas_call(
        body, name=name,
        grid_spec=pltpu.PrefetchScalarGridSpec(
            num_scalar_prefetch=1, grid=(nblk,), in_specs=in_specs,
            out_specs=pl.BlockSpec((tr, ns), lambda i, s: (s[1] * nblk + i, 0))),
        out_shape=jax.ShapeDtypeStruct((2 * kh, ns), F32),
        compiler_params=_params(("parallel",)),
    )(place, q, q, q, q, p)


ADAM_LR, ADAM_B1, ADAM_B2, ADAM_EPS, ADAM_WD, ADAM_STEP = 0.001, 0.9, 0.999, 1e-08, 0.01, 10


def adamw(name, w, g, m, v, layer, prev=None, after=None):
    _, k, ns = w.shape
    nsp = g.shape[1]
    tr = _row_tile(k, nsp, mult=8, cap=3 << 20)

    def body(w_ref, g_ref, m_ref, v_ref, *rest):
        go_ref, d_ref, mo_ref, vo_ref = rest[-4:]
        gv = g_ref[:, :ns] if nsp != ns else g_ref[...]
        mn = ADAM_B1 * m_ref[...] + (1.0 - ADAM_B1) * gv
        vn = ADAM_B2 * v_ref[...] + (1.0 - ADAM_B2) * (gv * gv)
        m_hat = mn / (1.0 - ADAM_B1 ** ADAM_STEP)
        v_hat = vn / (1.0 - ADAM_B2 ** ADAM_STEP)
        go_ref[...] = gv
        d_ref[...] = -ADAM_LR * (m_hat / (jnp.sqrt(v_hat) + ADAM_EPS) + ADAM_WD * w_ref[...])
        mo_ref[...] = mn
        vo_ref[...] = vn

    blk = pl.BlockSpec((None, tr, ns), lambda i: (layer, i, 0))
    gblk = pl.BlockSpec((tr, nsp), lambda i: (i, 0))
    args, in_specs, aliases = [w, g, m, v], [blk, gblk, blk, blk], {}
    if prev is not None:
        args += list(prev)
        in_specs += [pl.BlockSpec(memory_space=pl.ANY)] * 4
        aliases = {4 + i: i for i in range(4)}
    if after is not None:
        args.append(after)
        in_specs.append(pl.BlockSpec(memory_space=pl.ANY))
    return pl.pallas_call(
        body, name=name, grid=(k // tr,), in_specs=in_specs, out_specs=[blk] * 4,
        out_shape=[jax.ShapeDtypeStruct(w.shape, F32)] * 4, input_output_aliases=aliases,
        compiler_params=_params(("parallel",)),
    )(*args)


WEIGHTS = ("ffn_a_norm", "ffn_a_w1", "ffn_a_w3", "ffn_a_w2", "mix_norm", "w_in", "q_a_norm", "w_uq", "kv_a_norm", "w_ukv",
           "q_norm", "k_norm", "gm_v_norm", "gm_ws", "gm_bs", "attn_out_norm", "gm_out_norm", "w_out", "ffn_b_norm",
           "ffn_b_w1", "ffn_b_w3", "ffn_b_w2", "ple_gate_norm", "w_ple_gate", "w_ple", "ple_norm")
_FF = FF_PAD - FF_SHARD
BIG = {"ffn_a_w1": _FF, "ffn_a_w3": _FF, "ffn_a_w2": _FF, "ffn_b_w1": _FF, "ffn_b_w3": _FF, "ffn_b_w2": _FF,
       "w_in": IN_SHARD_PAD - IN_SHARD, "w_uq": 0, "w_ukv": 0, "w_ple": 0, "w_out": 0, "w_ple_gate": 0}
TRANSPOSED = ("ffn_a_w1", "ffn_a_w3", "ffn_b_w1", "ffn_b_w3", "w_in")
SMALL = tuple(n for n in WEIGHTS if n not in BIG)
PACK = 1024


def _pack_small(d):
    parts = []
    for n in SMALL:
        flat = d[n].reshape(-1)
        parts.append(jnp.pad(flat, (0, (-flat.shape[0]) % PACK)))
    return jnp.concatenate(parts).reshape(-1, 128)


def _unpack_small(buf, like):
    flat = buf.reshape(-1)
    out, pos = {}, 0
    for n in SMALL:
        size = math.prod(like[n].shape)
        out[n] = flat[pos:pos + size].reshape(like[n].shape)
        pos += size + (-size) % PACK
    return out


def kernel(*args):
    names = (("x", "p", "positions") + WEIGHTS + ("loss_target",) + tuple("m_" + n for n in WEIGHTS)
             + tuple("v_" + n for n in WEIGHTS))
    a = dict(zip(names, args, strict=True))
    x, p, positions, target = a["x"][0], a["p"][:, 0], a["positions"][0], a["loss_target"][0]
    for n in TRANSPOSED:
        for pre in ("", "m_", "v_"):
            a[pre + n] = jnp.swapaxes(a[pre + n], 1, 2)

    place = jnp.stack([2 * lax.axis_index("x") + lax.axis_index("y"), lax.axis_index("c")]).astype(jnp.int32)
    small = {n: a[n] for n in SMALL}
    tabs = rope_tables(positions)
    order = {"l0a": ("ffn_a_w1", "ffn_a_w3"), "l0b": ("ffn_a_w2",), "l0c": ("w_in", "w_uq", "w_ukv", "w_out"),
             "l0d": ("ffn_b_w1", "ffn_b_w3", "ffn_b_w2", "w_ple_gate", "w_ple")}
    prep = lambda n, l, after: prep_shard(f"prep_{n}_{l}", a[n], l, BIG[n], place, after)
    flights, token = {}, None
    for tag, names in order.items():
        flights[tag] = gather_start(f"gather_{tag}_start", [prep(n, 0, token) for n in names], None)
        token = flights[tag][3]
    slots1 = []
    for n in BIG:
        slots1.append(prep(n, 1, slots1[-1] if slots1 else token))

    def arrive(tag, after):
        send, recv, flying, _ = flights[tag]
        arrived = gather_wait(f"gather_{tag}_wait", send, recv, flying, after)
        send, recv, flying, token = forward_start(f"forward_{tag}_start", arrived)
        return dict(zip(order[tag], gather_wait(f"forward_{tag}_wait", send, recv, flying, token)))

    G0 = arrive("l0a", slots1[-1])

    def before_down(s):
        G0.update(arrive("l0b", s))
        return G0["ffn_a_w2"]

    def before_mixer(h1, Gl, small_):
        G0.update(arrive("l0c", h1))
        flights["l1"] = gather_start("gather_l1_start", slots1, G0["w_uq"])
        return G0, {**small_, "mix_norm": small_["mix_norm"] + flights["l1"][3][0, 0]}

    def before_ffn_b(h2, Gl):
        G0.update(arrive("l0d", h2))
        return G0

    def before_ple(h3):
        send, recv, flying, _ = flights["l1"]
        flights["f1"] = forward_start("forward_l1_start", gather_wait("gather_l1_wait", send, recv, flying, h3))
        return flights["f1"][3]

    h, saved0 = layer_fwd(0, x, p[0], G0, small, tabs,
                          {"down_a": before_down, "mixer": before_mixer, "ffn_b": before_ffn_b, "ple": before_ple})
    G1 = dict(zip(BIG, gather_wait("forward_l1_wait", *flights["f1"][:3], h)))
    h, saved1 = layer_fwd(1, h, p[1], G1, small, tabs)
    dh, loss = loss_grad(h, target)
    loss = lax.psum(loss, ("x", "y", "c"))

    groups = {"l1": tuple(BIG),
              "l0a": ("w_ple_gate", "w_ple", "ffn_b_w1", "ffn_b_w3", "ffn_b_w2"),
              "l0b": ("w_in", "w_uq", "w_ukv", "w_out"),
              "l0c": ("ffn_a_w1", "ffn_a_w3", "ffn_a_w2")}
    crossing, started = [], {}

    def begin(tag, gl, after):
        ex = exchange_start(f"exchange_{tag}_start", [gl[n] for n in groups[tag]], after)
        crossing.append((tag, ex))
        return ex[4]

    def advance(after):
        tag, (send, recv, mine, lands, _) = crossing.pop()
        mine, theirs = exchange_wait(f"exchange_{tag}_wait", send, recv, mine, lands, after)
        parts = [add_sibling(f"add_sibling_{n}_{tag}", d, r, place) for n, d, r in zip(groups[tag], mine, theirs)]
        started[tag] = scatter_start(f"scatter_{tag}_start", parts)
        return started[tag][4]

    def sum_chips(tag, after):
        send, recv, parts, lands, _ = started[tag]
        parts, slabs = scatter_wait(f"scatter_{tag}_wait", send, recv, parts, lands, after)
        halves = [add_chips(f"add_chips_{n}_{tag}", q, pt, place) for n, q, pt in zip(groups[tag], slabs, parts)]
        return share_start(f"share_{tag}_start", halves)

    def shared(tag, sharing, after):
        send, recv, flying, _ = sharing
        return dict(zip(groups[tag], share_wait(f"share_{tag}_wait", send, recv, flying, after)))

    def update(names, full, layer, prev, after):
        outs = {}
        for n in names:
            outs[n] = adamw(f"adamw_{n}_{layer}", a[n], full[n], a["m_" + n], a["v_" + n], layer, prev and prev[n], after)
            after = outs[n][1]
        return outs, after

    grads = [None, None]
    dh, grads[1] = layer_bwd(1, dh, p[1], G1, small, tabs, saved1)
    token = begin("l1", grads[1], None)
    w0 = {**saved0[0], "ple_gate_norm": saved0[0]["ple_gate_norm"] + token[0, 0]}
    hooks = {"ffn_b": lambda gl, dh_: advance(dh_),
             "mixer": lambda gl, dh_: begin("l0a", gl, None),
             "ffn_a": lambda gl, dh_: begin("l0b", gl, advance(dh_)),
             "ffn_a_dw": lambda gl, dh_: advance(dh_)}
    gx, grads[0] = layer_bwd(0, dh, p[0], G0, small, tabs, (w0,) + saved0[1:], hooks)
    token = begin("l0c", grads[0], None)
    sharing = sum_chips("l1", token)
    full1 = shared("l1", sharing, advance(sharing[3]))
    outs1, behind = update(BIG, full1, 1, None, None)
    sharing_a = sum_chips("l0a", behind)
    sharing_b = sum_chips("l0b", sharing_a[3])
    full0 = shared("l0a", sharing_a, sharing_b[3])
    outs0, behind = update(groups["l0a"], full0, 0, outs1, None)
    sharing_c = sum_chips("l0c", behind)
    full0.update(shared("l0b", sharing_b, sharing_c[3]))
    outs, behind = update(groups["l0b"], full0, 0, outs1, None)
    outs0.update(outs)
    full0.update(shared("l0c", sharing_c, behind))
    outs0.update(update(groups["l0c"], full0, 0, outs1, None)[0])

    out_g, out_d, out_m, out_v = {}, {}, {}, {}
    for n in BIG:
        outs = [jnp.swapaxes(o, 1, 2) for o in outs0[n]] if n in TRANSPOSED else outs0[n]
        out_g[n], out_d[n], out_m[n], out_v[n] = outs

    gs = allreduce_small(_pack_small({n: jnp.stack([grads[0][n], grads[1][n]]) for n in SMALL}))
    rows = gs.shape[0] // 2
    packed = [_pack_small(d).reshape(2, rows, 128) for d in
              (small, {n: a["m_" + n] for n in SMALL}, {n: a["v_" + n] for n in SMALL})]
    gs = gs.reshape(2, rows, 128)
    sm = adamw("adamw_small_0", packed[0], gs[0], packed[1], packed[2], 0)
    sm = adamw("adamw_small_1", packed[0], gs[1], packed[1], packed[2], 1, sm)
    for dst, buf in zip((out_g, out_d, out_m, out_v), sm):
        dst.update(_unpack_small(buf, small))

    return (loss, gx[None], *[out_g[n] for n in WEIGHTS], *[out_d[n] for n in WEIGHTS],
            *[out_m[n] for n in WEIGHTS], *[out_v[n] for n in WEIGHTS])
```

```python
import math

import jax
import jax.numpy as jnp
from jax import lax
from jax.experimental import pallas as pl
from jax.experimental.pallas import tpu as pltpu

F32 = jnp.float32
BF16 = jnp.bfloat16

D_MODEL = 2048
D_FF = 5504
N_CHIPS = 4
FF_SHARD = D_FF // N_CHIPS
FF_PAD = 1408
FF_P = N_CHIPS * FF_PAD
HEADS = 8
QK_NOPE = 128
QK_ROPE = 64
QK_DIM = 192
HEAD_PAD = 256
V_DIM = 128
Q_RANK = 512
KV_RANK = 256
ATTN_W = 1024
GM_W = 1024
GROUPS = 8
CHUNK = 128
PLE_DIM = 256
IN_P = 3072
IN_SHARD = 720
IN_SHARD_PAD = 736
EPS = 1e-6
ROPE_BASE = 10000.0
ATTN_SCALE = QK_DIM ** -0.5
VMEM_LIMIT_BYTES = 56 * 1024 * 1024


def _params(sem):
    return pltpu.CompilerParams(dimension_semantics=sem, vmem_limit_bytes=VMEM_LIMIT_BYTES)


def _bf(x):
    return x if x.dtype == BF16 else x.astype(BF16)


def _sigmoid(x):
    return 1.0 / (1.0 + jnp.exp(-x))


_GELU_C = math.sqrt(2.0 / math.pi)


def _gelu(x):
    t = jnp.tanh(_GELU_C * (x + 0.044715 * x * x * x))
    return 0.5 * x * (1.0 + t)


def _gelu_grad(x):
    t = jnp.tanh(_GELU_C * (x + 0.044715 * x * x * x))
    return 0.5 * (1.0 + t) + 0.5 * x * (1.0 - t * t) * _GELU_C * (1.0 + 3 * 0.044715 * x * x)


def op_a(a, tm, tk):
    return (a, (tm, tk), lambda i, j, k: (i, k), 1)


def op_at(a, tm, tk):
    return (a, (tk, tm), lambda i, j, k: (k, i), 0)


def op_b(b, tk, tn):
    return (b, (tk, tn), lambda i, j, k: (k, j), 0)


def op_bt(b, tk, tn):
    return (b, (tn, tk), lambda i, j, k: (j, k), 1)


def op_b_cols(g, pre, tk, tn):
    nb = g.shape[-1] // tn
    none = (None,) * (1 + len(pre))
    return (g, none + (tk, tn), lambda i, j, k: (j // nb,) + tuple(pre) + (k, j % nb), 0)


def op_b_rows(g, pre, tk, tn, koff=0):
    nb = g.shape[-2] // tk
    none = (None,) * (1 + len(pre))
    return (g, none + (tk, tn), lambda i, j, k: ((k + koff) // nb,) + tuple(pre) + ((k + koff) % nb, j), 0)


def op_b_rows_t(g, pre, tk, tn):
    nb = g.shape[-2] // tn
    none = (None,) * (1 + len(pre))
    return (g, none + (tn, tk), lambda i, j, k: (j // nb,) + tuple(pre) + (j % nb, k), 1)


def tile_mn(x, tm, tn):
    return (x, (tm, tn), lambda i, j: (i, j))


def out_mn(M, N, tm, tn, dtype):
    return (jax.ShapeDtypeStruct((M, N), dtype), (tm, tn), lambda i, j: (i, j))


def out_cols(M, ns, tm, tn, dtype):
    nb = ns // tn
    return (jax.ShapeDtypeStruct((N_CHIPS, M, ns), dtype), (None, tm, tn), lambda i, j: (j // nb, i, j % nb))


def matmul(name, grid_mnk, a_ops, b_ops, terms, n_acc, extras, outs, epilogue, acc_tile, n_outer=False, after=None):
    gm, gn, gk = grid_mnk
    na, nb, nx, no = len(a_ops), len(b_ops), len(extras), len(outs)
    nd = 0 if after is None else 1

    def body(*refs):
        a_refs, b_refs = refs[:na], refs[na:na + nb]
        x_refs = refs[na + nb:na + nb + nx]
        o_refs = refs[na + nb + nx + nd:na + nb + nx + nd + no]
        acc_refs = refs[na + nb + nx + nd + no:]
        k = pl.program_id(2)

        @pl.when(k == 0)
        def _():
            for acc in acc_refs:
                acc[...] = jnp.zeros_like(acc)

        for ai, bi, ci in terms:
            dims = (((a_ops[ai][3],), (b_ops[bi][3],)), ((), ()))
            acc_refs[ci][...] += lax.dot_general(_bf(a_refs[ai][...]), _bf(b_refs[bi][...]), dims,
                                                 preferred_element_type=F32)

        @pl.when(k == gk - 1)
        def _():
            res = epilogue([acc[...] for acc in acc_refs], [x[...] for x in x_refs])
            for o, v in zip(o_refs, res):
                o[...] = v.astype(o.dtype)

    if n_outer:
        grid = (gn, gm, gk)

        def ix3(f):
            return lambda j, i, k: f(i, j, k)

        def ix2(f):
            return lambda j, i, k: f(i, j)
    else:
        grid = (gm, gn, gk)

        def ix3(f):
            return lambda i, j, k: f(i, j, k)

        def ix2(f):
            return lambda i, j, k: f(i, j)

    in_specs = [pl.BlockSpec(blk, ix3(f)) for (_, blk, f, _) in list(a_ops) + list(b_ops)]
    in_specs += [pl.BlockSpec(blk, ix2(f)) for (_, blk, f) in extras]
    in_specs += [pl.BlockSpec(memory_space=pl.ANY)] * nd
    out_specs = [pl.BlockSpec(blk, ix2(f)) for (_, blk, f) in outs]
    return pl.pallas_call(
        body,
        name=name,
        grid=grid,
        in_specs=in_specs,
        out_specs=out_specs,
        out_shape=[s for (s, _, _) in outs],
        scratch_shapes=[pltpu.VMEM(acc_tile, F32) for _ in range(n_acc)],
        compiler_params=_params(("parallel", "parallel", "arbitrary")),
    )(*[o[0] for o in a_ops], *[o[0] for o in b_ops], *[x[0] for x in extras], *([after] * nd))


def _acc0(accs, xs):
    return (accs[0],)


def mm_simple(name, a, b_op_fn, M, N, K, tm, tn, tk, out_dtype=F32, a_t=False, extras=(), epilogue=_acc0, outs=None,
              after=None):
    a_op = op_at(a, tm, tk) if a_t else op_a(a, tm, tk)
    outs = outs or [out_mn(M, N, tm, tn, out_dtype)]
    return matmul(name, (M // tm, N // tn, K // tk), [a_op], [b_op_fn(tk, tn)], [(0, 0, 0)], 1,
                  list(extras), outs, epilogue, (tm, tn), after=after)


def rms_fwd(name, x, g, width, col_blk=0, tm=512, out_dtype=BF16):
    T = x.shape[0]

    def body(x_ref, g_ref, o_ref):
        xv = x_ref[...].astype(F32)
        r = lax.rsqrt(jnp.mean(xv * xv, axis=-1, keepdims=True) + EPS)
        o_ref[...] = (xv * r * g_ref[...]).astype(o_ref.dtype)

    return pl.pallas_call(
        body, name=name, grid=(T // tm,),
        in_specs=[pl.BlockSpec((tm, width), lambda i: (i, col_blk)), pl.BlockSpec((1, width), lambda i: (0, 0))],
        out_specs=pl.BlockSpec((tm, width), lambda i: (i, 0)),
        out_shape=jax.ShapeDtypeStruct((T, width), out_dtype),
        compiler_params=_params(("parallel",)),
    )(x, g.reshape(1, width))


def rms_bwd(name, x, g, dn, width, col_blk=0, dres=None, tm=512, with_delta=False, bf16_copy=False):
    T = x.shape[0]
    has_res = dres is not None

    def body(*refs):
        x_ref, g_ref, dn_ref = refs[:3]
        pos = 3
        res_ref = None
        if has_res:
            res_ref = refs[pos]
            pos += 1
        dx_ref, dg_ref = refs[pos], refs[pos + 1]
        delta_ref = refs[pos + 2] if with_delta else None
        lo_ref = refs[-1] if bf16_copy else None
        i = pl.program_id(0)
        xv = x_ref[...].astype(F32)
        r = lax.rsqrt(jnp.mean(xv * xv, axis=-1, keepdims=True) + EPS)
        xh = xv * r
        d = dn_ref[...].astype(F32)
        gd = d * g_ref[...]
        dx = r * (gd - xh * jnp.mean(gd * xh, axis=-1, keepdims=True))
        if has_res:
            dx = dx + res_ref[...]
        dx_ref[...] = dx.astype(dx_ref.dtype)
        if bf16_copy:
            lo_ref[...] = dx.astype(BF16)
        part = jnp.sum(d * xh, axis=0, keepdims=True)

        @pl.when(i == 0)
        def _():
            dg_ref[...] = part

        @pl.when(i > 0)
        def _():
            dg_ref[...] += part

        if with_delta:
            for h in range(width // 128):
                sl = slice(h * 128, (h + 1) * 128)
                s = jnp.sum(dx[:, sl] * xv[:, sl], axis=-1, keepdims=True)
                delta_ref[:, sl] = jnp.broadcast_to(s, (tm, 128))

    in_specs = [pl.BlockSpec((tm, width), lambda i: (i, col_blk)), pl.BlockSpec((1, width), lambda i: (0, 0)),
                pl.BlockSpec((tm, width), lambda i: (i, 0))]
    args = [x, g.reshape(1, width), dn]
    if has_res:
        in_specs.append(pl.BlockSpec((tm, width), lambda i: (i, 0)))
        args.append(dres)
    out_specs = [pl.BlockSpec((tm, width), lambda i: (i, 0)), pl.BlockSpec((1, width), lambda i: (0, 0))]
    out_shape = [jax.ShapeDtypeStruct((T, width), F32), jax.ShapeDtypeStruct((1, width), F32)]
    if with_delta:
        out_specs.append(pl.BlockSpec((tm, width), lambda i: (i, 0)))
        out_shape.append(jax.ShapeDtypeStruct((T, width), F32))
    if bf16_copy:
        out_specs.append(pl.BlockSpec((tm, width), lambda i: (i, 0)))
        out_shape.append(jax.ShapeDtypeStruct((T, width), BF16))
    return pl.pallas_call(
        body, name=name, grid=(T // tm,), in_specs=in_specs, out_specs=out_specs, out_shape=out_shape,
        compiler_params=_params(("arbitrary",)),
    )(*args)


def ffn_fwd(tag, h, g, w1g, w3g, w2g, pre, w2_late=None):
    T = h.shape[0]
    n = rms_fwd(f"{tag}_rms", h, g, D_MODEL)
    tm, tn = 512, FF_PAD

    def up_epi(accs, xs):
        a1, a3 = accs
        return a1, a3, a1 * _sigmoid(a1) * a3

    a1, a3, s = matmul(
        f"{tag}_up", (T // tm, FF_P // tn, 1),
        [op_a(n, tm, D_MODEL)], [op_b_rows_t(w1g, pre, D_MODEL, tn), op_b_rows_t(w3g, pre, D_MODEL, tn)],
        [(0, 0, 0), (0, 1, 1)], 2, [],
        [out_mn(T, FF_P, tm, tn, BF16)] * 3, up_epi, (tm, tn), n_outer=True)

    if w2_late is not None:
        w2g = w2_late(s)
    tm2, tn2 = 1024, 1024
    (h_out,) = matmul(
        f"{tag}_down", (T // tm2, D_MODEL // tn2, N_CHIPS),
        [op_a(s, tm2, FF_PAD)], [op_b_rows(w2g, pre, FF_PAD, tn2)],
        [(0, 0, 0)], 1, [tile_mn(h, tm2, tn2)],
        [out_mn(T, D_MODEL, tm2, tn2, F32)], lambda accs, xs: (xs[0] + 0.5 * accs[0],), (tm2, tn2))
    return h_out, (n, a1, a3, s)


def ffn_bwd(tag, dh_out, dh_bf, h, g, res, w1g, w3g, w2g, pre, after=None, before_dw=None):
    n, a1, a3, s = res
    T = h.shape[0]
    tm, tn = 512, FF_PAD

    def act_epi(accs, xs):
        ds = 0.5 * accs[0]
        x1, x3 = xs[0].astype(F32), xs[1].astype(F32)
        sg = _sigmoid(x1)
        silu = x1 * sg
        return ds * x3 * (sg + silu * (1.0 - sg)), ds * silu

    da1, da3 = matmul(
        f"{tag}_dact", (T // tm, FF_P // tn, 1),
        [op_a(dh_bf, tm, D_MODEL)], [op_b_rows_t(w2g, pre, D_MODEL, tn)],
        [(0, 0, 0)], 1, [tile_mn(a1, tm, tn), tile_mn(a3, tm, tn)],
        [out_mn(T, FF_P, tm, tn, BF16)] * 2, act_epi, (tm, tn), n_outer=True, after=after)

    tm2, tn2 = 1024, 1024
    (dn,) = matmul(
        f"{tag}_dn", (T // tm2, D_MODEL // tn2, N_CHIPS),
        [op_a(da1, tm2, FF_PAD), op_a(da3, tm2, FF_PAD)],
        [op_b_rows(w1g, pre, FF_PAD, tn2), op_b_rows(w3g, pre, FF_PAD, tn2)],
        [(0, 0, 0), (1, 1, 0)], 1, [], [out_mn(T, D_MODEL, tm2, tn2, F32)], _acc0, (tm2, tn2))
    dh, dg, dh_lo = rms_bwd(f"{tag}_rms_bwd", h, g, dn, D_MODEL, dres=dh_out, bf16_copy=True)
    if before_dw is not None:
        after = before_dw(dh)

    tk, tn3 = T, 512

    def dw_t(nm, left, right, scale):
        (dw,) = matmul(
            f"{tag}_{nm}", (FF_P // FF_PAD, D_MODEL // tn3, T // tk),
            [op_at(left, FF_PAD, tk)], [op_b(right, tk, tn3)],
            [(0, 0, 0)], 1, [], [out_mn(FF_P, D_MODEL, FF_PAD, tn3, BF16)],
            lambda accs, xs: (scale * accs[0],), (FF_PAD, tn3), after=after)
        return dw

    dw2 = dw_t("dw2", s, dh_bf, 0.5)
    dw1 = dw_t("dw1", da1, n, 1.0)
    dw3 = dw_t("dw3", da3, n, 1.0)
    return dh, dh_lo, dg, dw1, dw3, dw2


def rope_tables(positions):
    inv_freq = ROPE_BASE ** (-jnp.arange(0, QK_ROPE, 2, dtype=F32) / QK_ROPE)
    ang = positions.astype(F32)[:, None] * inv_freq
    cos, sin = jnp.cos(ang), jnp.sin(ang)
    T = positions.shape[0]
    one, zero = jnp.ones((T, QK_NOPE), F32), jnp.zeros((T, 64), F32)
    z32, z128 = jnp.zeros((T, 32), F32), jnp.zeros((T, QK_NOPE), F32)
    c = jnp.concatenate([one, cos, cos, zero], axis=1)
    s1 = jnp.concatenate([z128, -sin, z32, zero], axis=1)
    s2 = jnp.concatenate([z128, z32, sin, zero], axis=1)
    return c, s1, s2


def _rope(y, c, s1, s2):
    return y * c + pltpu.roll(y, HEAD_PAD - 32, 1) * s1 + pltpu.roll(y, 32, 1) * s2


def _rope_t(d, c, s1, s2):
    return d * c + pltpu.roll(d * s1, 32, 1) + pltpu.roll(d * s2, HEAD_PAD - 32, 1)


def _head_norm(x):
    r = lax.rsqrt(jnp.sum(x * x, axis=-1, keepdims=True) * (1.0 / QK_DIM) + EPS)
    return x * r, r


def qk_prep_fwd(tag, q_raw, kk_raw, z_p, gq, gk, tabs, tm=256):
    T = q_raw.shape[0]
    c, s1, s2 = tabs

    def body(q_ref, k_ref, kr_ref, gq_ref, gk_ref, c_ref, s1_ref, s2_ref, qo_ref, ko_ref):
        cv, s1v, s2v = c_ref[...], s1_ref[...], s2_ref[...]
        kr = kr_ref[...]
        for h in range(HEADS):
            sl = slice(h * HEAD_PAD, (h + 1) * HEAD_PAD)
            xh, _ = _head_norm(q_ref[:, sl])
            qo_ref[:, sl] = (_rope(xh * gq_ref[...], cv, s1v, s2v) * ATTN_SCALE).astype(BF16)
            xh, _ = _head_norm(k_ref[:, sl] + kr)
            ko_ref[:, sl] = _rope(xh * gk_ref[...], cv, s1v, s2v).astype(BF16)

    row = lambda i: (i, 0)
    full = pl.BlockSpec((tm, HEADS * HEAD_PAD), row)
    tab = pl.BlockSpec((tm, HEAD_PAD), row)
    vec = pl.BlockSpec((1, HEAD_PAD), lambda i: (0, 0))
    return pl.pallas_call(
        body, name=f"{tag}_qk_prep", grid=(T // tm,),
        in_specs=[full, full, pl.BlockSpec((tm, HEAD_PAD), lambda i: (i, 3)), vec, vec, tab, tab, tab],
        out_specs=[full, full],
        out_shape=[jax.ShapeDtypeStruct((T, HEADS * HEAD_PAD), BF16)] * 2,
        compiler_params=_params(("parallel",)),
    )(q_raw, kk_raw, z_p, gq, gk, c, s1, s2)


def qk_prep_bwd(tag, dq_full, dk_full, q_raw, kk_raw, z_p, gq, gk, tabs, tm=256):
    T = q_raw.shape[0]
    c, s1, s2 = tabs

    def body(dq_ref, dk_ref, q_ref, k_ref, kr_ref, gq_ref, gk_ref, c_ref, s1_ref, s2_ref,
             dqr_ref, dkr_ref, dz_ref, dgq_ref, dgk_ref):
        i = pl.program_id(0)
        cv, s1v, s2v = c_ref[...], s1_ref[...], s2_ref[...]
        kr = kr_ref[...]
        lane = lax.broadcasted_iota(jnp.int32, (tm, HEAD_PAD), 1)
        slot = ((lane >= QK_NOPE) & (lane < QK_DIM)).astype(F32)

        def one(x, g, d):
            xh, r = _head_norm(x)
            dy = _rope_t(d, cv, s1v, s2v)
            gd = dy * g
            dx = r * (gd - xh * (jnp.sum(gd * xh, axis=-1, keepdims=True) * (1.0 / QK_DIM)))
            return dx, jnp.sum(dy * xh, axis=0, keepdims=True)

        dgq = jnp.zeros((1, HEAD_PAD), F32)
        dgk = jnp.zeros((1, HEAD_PAD), F32)
        dz = jnp.zeros((tm, HEAD_PAD), F32)
        for h in range(HEADS):
            sl = slice(h * HEAD_PAD, (h + 1) * HEAD_PAD)
            dx, dg = one(q_ref[:, sl], gq_ref[...], dq_ref[:, sl].astype(F32) * ATTN_SCALE)
            dqr_ref[:, sl] = dx
            dgq = dgq + dg
            dx, dg = one(k_ref[:, sl] + kr, gk_ref[...], dk_ref[:, sl].astype(F32))
            dkr_ref[:, sl] = dx
            dgk = dgk + dg
            dz = dz + dx
        dz_ref[...] = dz * slot

        @pl.when(i == 0)
        def _():
            dgq_ref[...] = dgq
            dgk_ref[...] = dgk

        @pl.when(i > 0)
        def _():
            dgq_ref[...] += dgq
            dgk_ref[...] += dgk

    row = lambda i: (i, 0)
    full = pl.BlockSpec((tm, HEADS * HEAD_PAD), row)
    tab = pl.BlockSpec((tm, HEAD_PAD), row)
    vec = pl.BlockSpec((1, HEAD_PAD), lambda i: (0, 0))
    return pl.pallas_call(
        body, name=f"{tag}_qk_prep_bwd", grid=(T // tm,),
        in_specs=[full, full, full, full, pl.BlockSpec((tm, HEAD_PAD), lambda i: (i, 3)), vec, vec, tab, tab, tab],
        out_specs=[full, full, tab, vec, vec],
        out_shape=[jax.ShapeDtypeStruct((T, HEADS * HEAD_PAD), F32)] * 2
        + [jax.ShapeDtypeStruct((T, HEAD_PAD), F32)] + [jax.ShapeDtypeStruct((1, HEAD_PAD), F32)] * 2,
        compiler_params=_params(("arbitrary",)),
    )(dq_full, dk_full, q_raw, kk_raw, z_p, gq, gk, c, s1, s2)


def attn_fwd(tag, q_full, k_full, vv, blk=512):
    T = q_full.shape[0]
    nb = T // blk
    neg = float(jnp.finfo(jnp.float32).min)

    def body(q_ref, k_ref, v_ref, o_ref, lse_ref, m_ref, l_ref, acc_ref):
        i = pl.program_id(1)
        m_ref[...] = jnp.full_like(m_ref, neg)
        l_ref[...] = jnp.zeros_like(l_ref)
        acc_ref[...] = jnp.zeros_like(acc_ref)
        q = q_ref[...]

        def step(j, masked):
            rows = pl.ds(pl.multiple_of(j * blk, blk), blk)
            s = lax.dot_general(q, k_ref[rows, :], (((1,), (1,)), ((), ())), preferred_element_type=F32)
            if masked:
                row = lax.broadcasted_iota(jnp.int32, (blk, blk), 0)
                col = lax.broadcasted_iota(jnp.int32, (blk, blk), 1)
                s = jnp.where(col <= row, s, neg)
            m_prev = m_ref[...]
            m_new = jnp.maximum(m_prev, jnp.max(s, axis=-1, keepdims=True))
            alpha = jnp.exp(m_prev - m_new)
            p = jnp.exp(s - m_new[:, :1])
            l_ref[...] = alpha * l_ref[...] + jnp.sum(p, axis=-1, keepdims=True)
            acc_ref[...] = alpha * acc_ref[...] + jnp.dot(p.astype(BF16), v_ref[rows, :], preferred_element_type=F32)
            m_ref[...] = m_new

        def off_diagonal(j, carry):
            step(j, False)
            return carry

        lax.fori_loop(0, i, off_diagonal, 0)
        step(i, True)
        o_ref[...] = acc_ref[...] / l_ref[...]
        lse_ref[...] = m_ref[...] + jnp.log(l_ref[...])

    return pl.pallas_call(
        body, name=f"{tag}_attn_fwd", grid=(HEADS, nb),
        in_specs=[pl.BlockSpec((blk, HEAD_PAD), lambda h, i: (i, h)),
                  pl.BlockSpec((T, HEAD_PAD), lambda h, i: (0, h)), pl.BlockSpec((T, V_DIM), lambda h, i: (0, h))],
        out_specs=[pl.BlockSpec((blk, V_DIM), lambda h, i: (i, h))] * 2,
        out_shape=[jax.ShapeDtypeStruct((T, ATTN_W), F32)] * 2,
        scratch_shapes=[pltpu.VMEM((blk, V_DIM), F32)] * 3,
        compiler_params=_params(("parallel", "parallel")),
    )(q_full, k_full, vv)


def attn_bwd(tag, q_full, k_full, vv, do, lse, delta, blk=512):
    T = q_full.shape[0]
    nb = T // blk
    neg = float(jnp.finfo(jnp.float32).min)

    def body(q_ref, k_ref, v_ref, do_ref, lse_ref, dl_ref, dq_ref, dk_ref, dv_ref, dk_acc, dv_acc):
        j = pl.program_id(1)

        @pl.when(j == 0)
        def _():
            dq_ref[...] = jnp.zeros_like(dq_ref)

        dk_acc[...] = jnp.zeros_like(dk_acc)
        dv_acc[...] = jnp.zeros_like(dv_acc)
        k, v = k_ref[...], v_ref[...]

        def step(i, masked):
            rows = pl.ds(pl.multiple_of(i * blk, blk), blk)
            q = q_ref[rows, :]
            s = lax.dot_general(q, k, (((1,), (1,)), ((), ())), preferred_element_type=F32)
            if masked:
                row = lax.broadcasted_iota(jnp.int32, (blk, blk), 0)
                col = lax.broadcasted_iota(jnp.int32, (blk, blk), 1)
                s = jnp.where(col <= row, s, neg)
            p = jnp.exp(s - lse_ref[rows, :1])
            dob = _bf(do_ref[rows, :])
            dv_acc[...] += lax.dot_general(p.astype(BF16), dob, (((0,), (0,)), ((), ())), preferred_element_type=F32)
            dp = lax.dot_general(dob, v, (((1,), (1,)), ((), ())), preferred_element_type=F32)
            ds = (p * (dp - dl_ref[rows, :1])).astype(BF16)
            dk_acc[...] += lax.dot_general(ds, q, (((0,), (0,)), ((), ())), preferred_element_type=F32)
            dq_ref[rows, :] += jnp.dot(ds, k, preferred_element_type=F32)

        def off_diagonal(i, carry):
            step(i, False)
            return carry

        step(j, True)
        lax.fori_loop(j + 1, nb, off_diagonal, 0)
        dk_ref[...] = dk_acc[...]
        dv_ref[...] = dv_acc[...]

    head = lambda h, j: (0, h)
    kv_ix = lambda h, j: (j, h)
    return pl.pallas_call(
        body, name=f"{tag}_attn_bwd", grid=(HEADS, nb),
        in_specs=[pl.BlockSpec((T, HEAD_PAD), head), pl.BlockSpec((blk, HEAD_PAD), kv_ix),
                  pl.BlockSpec((blk, V_DIM), kv_ix), pl.BlockSpec((T, V_DIM), head),
                  pl.BlockSpec((T, V_DIM), head), pl.BlockSpec((T, V_DIM), head)],
        out_specs=[pl.BlockSpec((T, HEAD_PAD), head),
                   pl.BlockSpec((blk, HEAD_PAD), kv_ix), pl.BlockSpec((blk, V_DIM), kv_ix)],
        out_shape=[jax.ShapeDtypeStruct((T, HEADS * HEAD_PAD), F32)] * 2 + [jax.ShapeDtypeStruct((T, ATTN_W), F32)],
        scratch_shapes=[pltpu.VMEM((blk, HEAD_PAD), F32), pltpu.VMEM((blk, V_DIM), F32)],
        compiler_params=_params(("parallel", "arbitrary")),
    )(q_full, k_full, vv, do, lse, delta)


def _gm_forward(u, v, gv, wc_ref, bb_ref, nchunk):
    ug = _gelu(u)
    vg = _gelu(v)
    rv = lax.rsqrt(jnp.mean(vg * vg, axis=-1, keepdims=True) + EPS)
    vhat = vg * rv
    vn = (vhat * gv).astype(BF16)
    gates = []
    for cidx in range(nchunk):
        rows = slice(cidx * CHUNK, (cidx + 1) * CHUNK)
        gates.append(jnp.concatenate(
            [jnp.dot(wc_ref[gidx], vn[rows, gidx * 128:(gidx + 1) * 128], preferred_element_type=F32) + bb_ref[gidx]
             for gidx in range(GROUPS)], axis=1))
    gate = jnp.concatenate(gates, axis=0)
    return ug, vhat, rv, vn, gate


def gmlp_fwd(tag, z_p, gv, gout, wc, bb, tm=256):
    T = z_p.shape[0]
    nchunk = tm // CHUNK

    def body(u_ref, v_ref, gv_ref, go_ref, wc_ref, bb_ref, o_ref):
        ug, _, _, _, gate = _gm_forward(u_ref[...], v_ref[...], gv_ref[...], wc_ref, bb_ref, nchunk)
        go = ug * gate
        ro = lax.rsqrt(jnp.mean(go * go, axis=-1, keepdims=True) + EPS)
        o_ref[...] = (go * ro * go_ref[...]).astype(BF16)

    vec = pl.BlockSpec((1, GM_W), lambda i: (0, 0))
    w3 = pl.BlockSpec((GROUPS, CHUNK, CHUNK), lambda i: (0, 0, 0))
    return pl.pallas_call(
        body, name=f"{tag}_gmlp_fwd", grid=(T // tm,),
        in_specs=[pl.BlockSpec((tm, GM_W), lambda i: (i, 1)), pl.BlockSpec((tm, GM_W), lambda i: (i, 2)), vec, vec, w3, w3],
        out_specs=pl.BlockSpec((tm, GM_W), lambda i: (i, 0)),
        out_shape=jax.ShapeDtypeStruct((T, GM_W), BF16),
        compiler_params=_params(("parallel",)),
    )(z_p, z_p, gv.reshape(1, GM_W), gout.reshape(1, GM_W), wc, bb)


def gmlp_bwd(tag, z_p, dmixed, gv, gout, wc, bb, tm=256):
    T = z_p.shape[0]
    nchunk = tm // CHUNK

    def body(u_ref, v_ref, dm_ref, gv_ref, go_ref, wc_ref, bb_ref, du_ref, dv_ref, dwc_ref, dbb_ref, dgv_ref, dgo_ref):
        i = pl.program_id(0)
        u, v = u_ref[...], v_ref[...]
        ug, vhat, rv, vn, gate = _gm_forward(u, v, gv_ref[...], wc_ref, bb_ref, nchunk)
        go = ug * gate
        ro = lax.rsqrt(jnp.mean(go * go, axis=-1, keepdims=True) + EPS)
        ohat = go * ro
        dm = dm_ref[...].astype(F32)
        dgo_part = jnp.sum(dm * ohat, axis=0, keepdims=True)
        doh = dm * go_ref[...]
        dgo = ro * (doh - ohat * jnp.mean(doh * ohat, axis=-1, keepdims=True))
        du_ref[...] = dgo * gate * _gelu_grad(u)
        dgate = dgo * ug
        dgb = dgate.astype(BF16)
        dvn_rows = []
        dwc_parts = []
        dbb_parts = []
        for gidx in range(GROUPS):
            cols = slice(gidx * 128, (gidx + 1) * 128)
            dw = jnp.zeros((CHUNK, CHUNK), F32)
            db = jnp.zeros((CHUNK, 128), F32)
            for cidx in range(nchunk):
                rows = slice(cidx * CHUNK, (cidx + 1) * CHUNK)
                dw = dw + lax.dot_general(dgb[rows, cols], vn[rows, cols], (((1,), (1,)), ((), ())),
                                          preferred_element_type=F32)
                db = db + dgate[rows, cols]
            dwc_parts.append(dw)
            dbb_parts.append(db)
        for cidx in range(nchunk):
            rows = slice(cidx * CHUNK, (cidx + 1) * CHUNK)
            dvn_rows.append(jnp.concatenate(
                [lax.dot_general(wc_ref[gidx], dgb[rows, gidx * 128:(gidx + 1) * 128], (((0,), (0,)), ((), ())),
                                 preferred_element_type=F32) for gidx in range(GROUPS)], axis=1))
        dvn = jnp.concatenate(dvn_rows, axis=0)
        dgv_part = jnp.sum(dvn * vhat, axis=0, keepdims=True)
        dvh = dvn * gv_ref[...]
        dvg = rv * (dvh - vhat * jnp.mean(dvh * vhat, axis=-1, keepdims=True))
        dv_ref[...] = dvg * _gelu_grad(v)

        @pl.when(i == 0)
        def _():
            for gidx in range(GROUPS):
                dwc_ref[gidx] = dwc_parts[gidx]
                dbb_ref[gidx] = dbb_parts[gidx]
            dgv_ref[...] = dgv_part
            dgo_ref[...] = dgo_part

        @pl.when(i > 0)
        def _():
            for gidx in range(GROUPS):
                dwc_ref[gidx] += dwc_parts[gidx]
                dbb_ref[gidx] += dbb_parts[gidx]
            dgv_ref[...] += dgv_part
            dgo_ref[...] += dgo_part

    vec = pl.BlockSpec((1, GM_W), lambda i: (0, 0))
    w3 = pl.BlockSpec((GROUPS, CHUNK, CHUNK), lambda i: (0, 0, 0))
    blk = pl.BlockSpec((tm, GM_W), lambda i: (i, 0))
    return pl.pallas_call(
        body, name=f"{tag}_gmlp_bwd", grid=(T // tm,),
        in_specs=[pl.BlockSpec((tm, GM_W), lambda i: (i, 1)), pl.BlockSpec((tm, GM_W), lambda i: (i, 2)),
                  pl.BlockSpec((tm, GM_W), lambda i: (i, 1)), vec, vec, w3, w3],
        out_specs=[blk, blk, w3, w3, vec, vec],
        out_shape=[jax.ShapeDtypeStruct((T, GM_W), F32)] * 2 + [jax.ShapeDtypeStruct((GROUPS, CHUNK, CHUNK), F32)] * 2
        + [jax.ShapeDtypeStruct((1, GM_W), F32)] * 2,
        compiler_params=_params(("arbitrary",)),
    )(z_p, z_p, dmixed, gv.reshape(1, GM_W), gout.reshape(1, GM_W), wc, bb)


def mixer_fwd(tag, h, w, tabs, wout_g, pre):
    T = h.shape[0]
    n2 = rms_fwd(f"{tag}_mix_rms", h, w["mix_norm"], D_MODEL)
    (z_p,) = mm_simple(f"{tag}_win", n2, lambda tk, tn: op_bt(w["w_in_pt"], tk, tn), T, IN_P, D_MODEL, 1024, 1024, D_MODEL)
    cqn = rms_fwd(f"{tag}_cq_rms", z_p, w["q_a_norm"], Q_RANK, col_blk=0)
    ckvn = rms_fwd(f"{tag}_ckv_rms", z_p, w["kv_a_norm"], KV_RANK, col_blk=2)
    (q_raw,) = mm_simple(f"{tag}_wq", cqn, lambda tk, tn: op_b(w["wq_p"], tk, tn), T, 2048, Q_RANK, 512, 1024, Q_RANK)
    (kk_raw,) = mm_simple(f"{tag}_wk", ckvn, lambda tk, tn: op_b(w["wk_p"], tk, tn), T, 2048, KV_RANK, 512, 1024, KV_RANK)
    (vv,) = mm_simple(f"{tag}_wv", ckvn, lambda tk, tn: op_b(w["wv"], tk, tn), T, ATTN_W, KV_RANK, 512, 1024, KV_RANK,
                      out_dtype=BF16)
    q_full, k_full = qk_prep_fwd(tag, q_raw, kk_raw, z_p, w["gq_p"], w["gk_p"], tabs)
    a_out, lse = attn_fwd(tag, q_full, k_full, vv)
    mixed_a = rms_fwd(f"{tag}_ao_rms", a_out, w["attn_out_norm"], ATTN_W)
    mixed_g = gmlp_fwd(tag, z_p, w["gm_v_norm"], w["gm_out_norm"], w["wc"], w["bb"])
    tm, tn, tk = 1024, 1024, 512
    (h2,) = matmul(
        f"{tag}_wout", (T // tm, D_MODEL // tn, ATTN_W // tk),
        [op_a(mixed_a, tm, tk), op_a(mixed_g, tm, tk)],
        [op_b_rows(wout_g, pre, tk, tn), op_b_rows(wout_g, pre, tk, tn, koff=ATTN_W // tk)],
        [(0, 0, 0), (1, 1, 0)], 1, [tile_mn(h, tm, tn)], [out_mn(T, D_MODEL, tm, tn, F32)],
        lambda accs, xs: (xs[0] + accs[0],), (tm, tn))
    res = dict(n2=n2, z_p=z_p, cqn=cqn, ckvn=ckvn, q_raw=q_raw, kk_raw=kk_raw, vv=vv, q_full=q_full, k_full=k_full,
               a_out=a_out, lse=lse, mixed_a=mixed_a, mixed_g=mixed_g)
    return h2, res


def mixer_bwd(tag, dh2, dh2_bf, h, w, tabs, wout_g, pre, r, after=None):
    T = h.shape[0]
    g = {}
    (dmixed,) = mm_simple(f"{tag}_dmixed", dh2_bf, lambda tk, tn: op_b_rows_t(wout_g, pre, tk, tn), T, D_MODEL, D_MODEL,
                          1024, 512, D_MODEL, after=after)
    (dwo_a,) = mm_simple(f"{tag}_dwout_a", r["mixed_a"], lambda tk, tn: op_b(dh2_bf, tk, tn), ATTN_W, D_MODEL, T,
                         1024, 1024, T, a_t=True, out_dtype=BF16)
    (dwo_g,) = mm_simple(f"{tag}_dwout_g", r["mixed_g"], lambda tk, tn: op_b(dh2_bf, tk, tn), GM_W, D_MODEL, T,
                         1024, 1024, T, a_t=True, out_dtype=BF16)
    g["w_out"] = jnp.concatenate([dwo_a, dwo_g], axis=0)
    da_out, g["attn_out_norm"], delta = rms_bwd(f"{tag}_ao_rms_bwd", r["a_out"], w["attn_out_norm"], dmixed, ATTN_W,
                                                with_delta=True)
    dq_full, dk_full, dvv = attn_bwd(tag, r["q_full"], r["k_full"], r["vv"], da_out, r["lse"], delta)
    dq_raw, dkk_raw, dzkr, g["gq_p"], g["gk_p"] = qk_prep_bwd(tag, dq_full, dk_full, r["q_raw"], r["kk_raw"], r["z_p"],
                                                            w["gq_p"], w["gk_p"], tabs)
    (g["wq_p"],) = mm_simple(f"{tag}_dwq", r["cqn"], lambda tk, tn: op_b(dq_raw, tk, tn), Q_RANK, 2048, T, Q_RANK, 1024, 2048,
                             a_t=True, out_dtype=BF16)
    (g["wk_p"],) = mm_simple(f"{tag}_dwk", r["ckvn"], lambda tk, tn: op_b(dkk_raw, tk, tn), KV_RANK, 2048, T, KV_RANK, 1024,
                             2048, a_t=True, out_dtype=BF16)
    (g["wv"],) = mm_simple(f"{tag}_dwv", r["ckvn"], lambda tk, tn: op_b(dvv, tk, tn), KV_RANK, ATTN_W, T, KV_RANK, 1024, 2048,
                           a_t=True, out_dtype=BF16)
    (dcqn,) = mm_simple(f"{tag}_dcqn", dq_raw, lambda tk, tn: op_bt(w["wq_p"], tk, tn), T, Q_RANK, 2048, 512, Q_RANK, 2048)
    (dck1,) = mm_simple(f"{tag}_dckvn_k", dkk_raw, lambda tk, tn: op_bt(w["wk_p"], tk, tn), T, KV_RANK, 2048, 512, KV_RANK,
                        2048)
    (dckvn,) = mm_simple(f"{tag}_dckvn_v", dvv, lambda tk, tn: op_bt(w["wv"], tk, tn), T, KV_RANK, ATTN_W, 512, KV_RANK,
                         ATTN_W, extras=[tile_mn(dck1, 512, KV_RANK)], epilogue=lambda accs, xs: (accs[0] + xs[0],))
    dc_q, g["q_a_norm"] = rms_bwd(f"{tag}_cq_rms_bwd", r["z_p"], w["q_a_norm"], dcqn, Q_RANK, col_blk=0)
    dc_kv, g["kv_a_norm"] = rms_bwd(f"{tag}_ckv_rms_bwd", r["z_p"], w["kv_a_norm"], dckvn, KV_RANK, col_blk=2)
    du, dv, g["wc"], g["bb"], g["gm_v_norm"], g["gm_out_norm"] = gmlp_bwd(
        tag, r["z_p"], dmixed, w["gm_v_norm"], w["gm_out_norm"], w["wc"], w["bb"])
    dz_p = jnp.concatenate([dc_q, dc_kv, dzkr, du, dv], axis=1).astype(BF16)
    (g["w_in_pt"],) = mm_simple(f"{tag}_dwin", dz_p, lambda tk, tn: op_b(r["n2"], tk, tn), IN_P, D_MODEL, T, 1024, 1024, T,
                                a_t=True, out_dtype=BF16)
    (dn2,) = mm_simple(f"{tag}_dn2", dz_p, lambda tk, tn: op_b(w["w_in_pt"], tk, tn), T, D_MODEL, IN_P, 1024, 1024, IN_P)
    dh1, g["mix_norm"], dh1_bf = rms_bwd(f"{tag}_mix_rms_bwd", h, w["mix_norm"], dn2, D_MODEL, dres=dh2, bf16_copy=True)
    return dh1, dh1_bf, g


def ple_fwd(tag, h3, p_l, w, wpg_g, wple_g, pre):
    T = h3.shape[0]
    (pw,) = mm_simple(f"{tag}_wple", p_l, lambda tk, tn: op_b_cols(wple_g, pre, tk, tn), T, D_MODEL, PLE_DIM, 512, 512,
                      PLE_DIM)
    e = rms_fwd(f"{tag}_ple_rms", pw, w["ple_norm"], D_MODEL, out_dtype=F32)
    n4 = rms_fwd(f"{tag}_pg_rms", h3, w["ple_gate_norm"], D_MODEL)

    def epi(accs, xs):
        gt = _sigmoid(accs[0])
        return xs[0] + gt * xs[1], gt

    tm, tn, tk = 1024, 1024, 512
    h4, gate = matmul(
        f"{tag}_wpg", (T // tm, D_MODEL // tn, D_MODEL // tk),
        [op_a(n4, tm, tk)], [op_b_rows(wpg_g, pre, tk, tn)], [(0, 0, 0)], 1,
        [tile_mn(h3, tm, tn), tile_mn(e, tm, tn)],
        [out_mn(T, D_MODEL, tm, tn, F32), out_mn(T, D_MODEL, tm, tn, BF16)], epi, (tm, tn))
    return h4, dict(pw=pw, e=e, n4=n4, gate=gate)


def ple_bwd(tag, dh4, h3, p_l, w, wpg_g, wple_g, pre, r, tm=256):
    T = h3.shape[0]

    def act_body(d_ref, g_ref, e_ref, dpre_ref, de_ref):
        d, gt = d_ref[...], g_ref[...].astype(F32)
        dpre_ref[...] = (d * e_ref[...] * gt * (1.0 - gt)).astype(BF16)
        de_ref[...] = d * gt

    blk = pl.BlockSpec((tm, D_MODEL), lambda i: (i, 0))
    dpre, de = pl.pallas_call(
        act_body, name=f"{tag}_ple_act_bwd", grid=(T // tm,), in_specs=[blk, blk, blk], out_specs=[blk, blk],
        out_shape=[jax.ShapeDtypeStruct((T, D_MODEL), BF16), jax.ShapeDtypeStruct((T, D_MODEL), F32)],
        compiler_params=_params(("parallel",)),
    )(dh4, r["gate"], r["e"])
    g = {}
    (g["w_ple_gate"],) = mm_simple(f"{tag}_dwpg", r["n4"], lambda tk, tn: op_b(dpre, tk, tn), D_MODEL, D_MODEL, T,
                                   1024, 1024, T, a_t=True, out_dtype=BF16)
    (dn4,) = mm_simple(f"{tag}_dn4", dpre, lambda tk, tn: op_b_rows_t(wpg_g, pre, tk, tn), T, D_MODEL, D_MODEL, 1024, 512,
                       D_MODEL)
    dh3, g["ple_gate_norm"], dh3_bf = rms_bwd(f"{tag}_pg_rms_bwd", h3, w["ple_gate_norm"], dn4, D_MODEL, dres=dh4,
                                              bf16_copy=True)
    dpw, g["ple_norm"] = rms_bwd(f"{tag}_ple_rms_bwd", r["pw"], w["ple_norm"], de, D_MODEL)
    (g["w_ple"],) = mm_simple(f"{tag}_dwple", p_l, lambda tk, tn: op_b(dpw, tk, tn), PLE_DIM, D_MODEL, T, PLE_DIM, 512, T,
                              a_t=True, outs=[out_cols(PLE_DIM, 512, PLE_DIM, 512, BF16)])
    return dh3, dh3_bf, g


def loss_grad(y, target, tm=256):
    T = y.shape[0]

    def body(y_ref, t_ref, dy_ref, l_ref):
        i = pl.program_id(0)
        d = y_ref[...] - t_ref[...]
        dy_ref[...] = d * (1.0 / D_MODEL)
        part = jnp.sum((d * d).reshape(tm // 8, 8, D_MODEL), axis=0)

        @pl.when(i == 0)
        def _():
            l_ref[...] = part

        @pl.when(i > 0)
        def _():
            l_ref[...] += part

    blk = pl.BlockSpec((tm, D_MODEL), lambda i: (i, 0))
    dy, part = pl.pallas_call(
        body, name="loss_grad", grid=(T // tm,), in_specs=[blk, blk],
        out_specs=[blk, pl.BlockSpec((8, D_MODEL), lambda i: (0, 0))],
        out_shape=[jax.ShapeDtypeStruct((T, D_MODEL), F32), jax.ShapeDtypeStruct((8, D_MODEL), F32)],
        compiler_params=_params(("arbitrary",)),
    )(y, target)
    return dy, 0.5 * jnp.sum(part) / D_MODEL


def _unshard_cols(g_l):
    return g_l.transpose(1, 0, 2).reshape(g_l.shape[1], -1)


def _shard_cols(w):
    return w.reshape(w.shape[0], N_CHIPS, -1).transpose(1, 0, 2)


def layer_weights(l, Gl, small):
    w = {k: small[k][l] for k in ("mix_norm", "q_a_norm", "kv_a_norm", "gm_v_norm", "attn_out_norm", "gm_out_norm",
                                  "ple_gate_norm", "ple_norm")}
    wint = Gl["w_in"][:, :IN_SHARD].reshape(-1, D_MODEL)
    z = lambda n: jnp.zeros((n, D_MODEL), BF16)
    w["w_in_pt"] = jnp.concatenate([wint[:768], z(128), wint[768:832], z(64), wint[832:]], axis=0)
    wuq = _unshard_cols(Gl["w_uq"]).reshape(Q_RANK, HEADS, QK_DIM)
    w["wq_p"] = jnp.pad(wuq, ((0, 0), (0, 0), (0, HEAD_PAD - QK_DIM))).reshape(Q_RANK, HEADS * HEAD_PAD)
    wukv = _unshard_cols(Gl["w_ukv"]).reshape(KV_RANK, HEADS, QK_NOPE + V_DIM)
    w["wk_p"] = jnp.pad(wukv[:, :, :QK_NOPE], ((0, 0), (0, 0), (0, HEAD_PAD - QK_NOPE))).reshape(KV_RANK, HEADS * HEAD_PAD)
    w["wv"] = wukv[:, :, QK_NOPE:].reshape(KV_RANK, ATTN_W)
    w["gq_p"] = jnp.pad(small["q_norm"][l], (0, HEAD_PAD - QK_DIM)).reshape(1, HEAD_PAD)
    w["gk_p"] = jnp.pad(small["k_norm"][l], (0, HEAD_PAD - QK_DIM)).reshape(1, HEAD_PAD)
    tril = jnp.tril(jnp.ones((CHUNK, CHUNK), dtype=bool))
    w["wc"] = jnp.where(tril[None], small["gm_ws"][l], 0.0).astype(BF16)
    w["bb"] = jnp.broadcast_to(small["gm_bs"][l][:, :, None], (GROUPS, CHUNK, 128)).astype(F32)
    return w


def mixer_grads_to_shards(g):
    out = {}
    dwint = g["w_in_pt"]
    dwint = jnp.concatenate([dwint[:768], dwint[896:960], dwint[1024:]], axis=0).reshape(N_CHIPS, IN_SHARD, D_MODEL)
    out["w_in"] = jnp.pad(dwint, ((0, 0), (0, IN_SHARD_PAD - IN_SHARD), (0, 0)))
    dwuq = g["wq_p"].reshape(Q_RANK, HEADS, HEAD_PAD)[:, :, :QK_DIM].reshape(Q_RANK, HEADS * QK_DIM)
    out["w_uq"] = _shard_cols(dwuq)
    dwukv = jnp.concatenate([g["wk_p"].reshape(KV_RANK, HEADS, HEAD_PAD)[:, :, :QK_NOPE],
                             g["wv"].reshape(KV_RANK, HEADS, V_DIM)], axis=-1).reshape(KV_RANK, HEADS * (QK_NOPE + V_DIM))
    out["w_ukv"] = _shard_cols(dwukv)
    out["w_out"] = g["w_out"].reshape(N_CHIPS, D_MODEL // N_CHIPS, D_MODEL)
    out["q_norm"] = g["gq_p"][0, :QK_DIM]
    out["k_norm"] = g["gk_p"][0, :QK_DIM]
    tril = jnp.tril(jnp.ones((CHUNK, CHUNK), dtype=bool))
    out["gm_ws"] = jnp.where(tril[None], g["wc"], 0.0)
    out["gm_bs"] = jnp.sum(g["bb"], axis=-1)
    for k in ("mix_norm", "q_a_norm", "kv_a_norm", "gm_v_norm", "attn_out_norm", "gm_out_norm"):
        out[k] = g[k][0]
    return out


def layer_fwd(l, h, p_l, Gl, small, tabs, before=None):
    before = before or {}
    h1, r_a = ffn_fwd(f"l{l}a", h, small["ffn_a_norm"][l], Gl["ffn_a_w1"], Gl["ffn_a_w3"], Gl.get("ffn_a_w2"), (),
                      before.get("down_a"))
    if "mixer" in before:
        Gl, small = before["mixer"](h1, Gl, small)
    w = layer_weights(l, Gl, small)
    h2, r_m = mixer_fwd(f"l{l}", h1, w, tabs, Gl["w_out"], ())
    if "ffn_b" in before:
        Gl = before["ffn_b"](h2, Gl)
    h3, r_b = ffn_fwd(f"l{l}b", h2, small["ffn_b_norm"][l], Gl["ffn_b_w1"], Gl["ffn_b_w3"], Gl["ffn_b_w2"], ())
    if "ple" in before:
        w = {**w, "ple_norm": w["ple_norm"] + before["ple"](h3)[0, 0]}
    h4, r_p = ple_fwd(f"l{l}", h3, p_l, w, Gl["w_ple_gate"], Gl["w_ple"], ())
    return h4, (w, h, h1, h2, h3, r_a, r_m, r_b, r_p)


def layer_bwd(l, dh, p_l, Gl, small, tabs, saved, before=None):
    w, h0, h1, h2, h3, r_a, r_m, r_b, r_p = saved
    slabs = lambda d: d.reshape(N_CHIPS, FF_PAD, D_MODEL)
    hook = lambda block: before[block](gl, dh) if before and block in before else None
    gl = {}
    dh, dh_bf, g_p = ple_bwd(f"l{l}", dh, h3, p_l, w, Gl["w_ple_gate"], Gl["w_ple"], (), r_p)
    gl["w_ple_gate"] = g_p["w_ple_gate"].reshape(N_CHIPS, D_MODEL // N_CHIPS, D_MODEL)
    gl["w_ple"] = g_p["w_ple"]
    gl["ple_gate_norm"], gl["ple_norm"] = g_p["ple_gate_norm"][0], g_p["ple_norm"][0]
    dh, dh_bf, dg, dw1, dw3, dw2 = ffn_bwd(f"l{l}b", dh, dh_bf, h2, small["ffn_b_norm"][l], r_b,
                                           Gl["ffn_b_w1"], Gl["ffn_b_w3"], Gl["ffn_b_w2"], (), hook("ffn_b"))
    gl["ffn_b_norm"] = dg[0]
    gl["ffn_b_w1"], gl["ffn_b_w3"], gl["ffn_b_w2"] = slabs(dw1), slabs(dw3), slabs(dw2)
    dh, dh_bf, g_m = mixer_bwd(f"l{l}", dh, dh_bf, h1, w, tabs, Gl["w_out"], (), r_m, hook("mixer"))
    gl.update(mixer_grads_to_shards(g_m))
    last_dw = (lambda dh_: before["ffn_a_dw"](gl, dh_)) if before and "ffn_a_dw" in before else None
    dh, _, dg, dw1, dw3, dw2 = ffn_bwd(f"l{l}a", dh, dh_bf, h0, small["ffn_a_norm"][l], r_a,
                                       Gl["ffn_a_w1"], Gl["ffn_a_w3"], Gl["ffn_a_w2"], (), hook("ffn_a"), last_dw)
    gl["ffn_a_norm"] = dg[0]
    gl["ffn_a_w1"], gl["ffn_a_w3"], gl["ffn_a_w2"] = slabs(dw1), slabs(dw3), slabs(dw2)
    return dh, gl


MESH = pl.DeviceIdType.MESH
HBM_SPEC = pl.BlockSpec(memory_space=pltpu.HBM)


def _place():
    x, y, c = lax.axis_index("x"), lax.axis_index("y"), lax.axis_index("c")
    others = [(1 - x, y), (x, 1 - y), (1 - x, 1 - y)]
    return x, y, c, 2 * x + y, others


def prep_shard(name, w, layer, rows_pad, place, after=None):
    _, ks, n = w.shape
    ksp = ks + rows_pad
    tc = 512 if n % 512 == 0 else n
    deps = [] if after is None else [after]

    def body(place_ref, x_ref, *rest):
        o_ref = rest[-1]
        o_ref[:ks] = x_ref[...].astype(BF16)
        if rows_pad:
            o_ref[ks:] = jnp.zeros((rows_pad, tc), BF16)

    return pl.pallas_call(
        body, name=name,
        grid_spec=pltpu.PrefetchScalarGridSpec(
            num_scalar_prefetch=1, grid=(n // tc,),
            in_specs=[pl.BlockSpec((None, ks, tc), lambda i, s: (layer, 0, i))] + [ANY_SPEC] * len(deps),
            out_specs=pl.BlockSpec((None, ksp, tc), lambda i, s: (s[0], 0, i))),
        out_shape=jax.ShapeDtypeStruct((N_CHIPS, ksp, n), BF16),
        compiler_params=_params(("parallel",)),
    )(place, w, *deps)


SEM_SPEC = pl.BlockSpec(memory_space=pltpu.SEMAPHORE)
ANY_SPEC = pl.BlockSpec(memory_space=pl.ANY)
DATAFLOW = pltpu.SideEffectType.DATAFLOW_SIDE_EFFECTING


def _hbm(x):
    return pltpu.with_memory_space_constraint(x, pltpu.HBM)


def _start_call(name, slots, after, issue):
    n = len(slots)
    deps = [] if after is None else [after]
    nd = len(deps)

    def body(*refs):
        issue(refs[n + nd + 2:2 * n + nd + 2], refs[n + nd], refs[n + nd + 1])
        token = refs[2 * n + nd + 2]
        token[...] = jnp.zeros_like(token)

    outs = pl.pallas_call(
        body, name=name,
        in_specs=[HBM_SPEC] * n + [ANY_SPEC] * nd,
        out_specs=(SEM_SPEC, SEM_SPEC, *([HBM_SPEC] * n), pl.BlockSpec(memory_space=pltpu.VMEM)),
        out_shape=(pltpu.SemaphoreType.DMA((n,)), pltpu.SemaphoreType.DMA((n,)),
                   *[pltpu.HBM(s.shape, s.dtype) for s in slots], jax.ShapeDtypeStruct((8, 128), F32)),
        input_output_aliases={w: w + 2 for w in range(n)},
        compiler_params=pltpu.CompilerParams(has_side_effects=DATAFLOW),
    )(*[_hbm(s) for s in slots], *deps)
    return outs[0], outs[1], list(outs[2:2 + n]), outs[2 + n]


def gather_start(name, slots, after):
    def issue(g_refs, send, recv):
        x, y, c, jme, others = _place()
        for w in range(len(slots)):
            kh = slots[w].shape[1] // 2
            mine = g_refs[w].at[jme, pl.ds(c * kh, kh)]
            for (px, py) in others:
                pltpu.make_async_remote_copy(src_ref=mine, dst_ref=mine, send_sem=send.at[w], recv_sem=recv.at[w],
                                             device_id=(px, py, c), device_id_type=MESH).start()

    return _start_call(name, slots, after, issue)


def forward_start(name, slots):
    def issue(g_refs, send, recv):
        x, y, c, _, others = _place()
        for w in range(len(slots)):
            kh = slots[w].shape[1] // 2
            for (px, py) in others:
                blk = g_refs[w].at[2 * px + py, pl.ds(c * kh, kh)]
                pltpu.make_async_remote_copy(src_ref=blk, dst_ref=blk, send_sem=send.at[w], recv_sem=recv.at[w],
                                             device_id=(x, y, 1 - c), device_id_type=MESH).start()

    return _start_call(name, slots, None, issue)


def share_start(name, fulls):
    def issue(o_refs, send, recv):
        x, y, c, _, _ = _place()
        for w in range(len(fulls)):
            kh = fulls[w].shape[0] // 2
            half = o_refs[w].at[pl.ds(c * kh, kh)]
            pltpu.make_async_remote_copy(src_ref=half, dst_ref=half, send_sem=send.at[w], recv_sem=recv.at[w],
                                         device_id=(x, y, 1 - c), device_id_type=MESH).start()

    return _start_call(name, fulls, None, issue)


def share_wait(name, send, recv, flying, after):
    return _wait_call(name, send, recv, flying, after, lambda r: r.at[pl.ds(0, r.shape[0] // 2)])


def gather_wait(name, send, recv, flying, after):
    return _wait_call(name, send, recv, flying, after, lambda r: r.at[pl.ds(0, 3), pl.ds(0, r.shape[1] // 2)])


def _wait_call(name, send, recv, flying, after, landed):
    n = len(flying)

    def body(*refs):
        send_ref, recv_ref = refs[n], refs[n + 1]
        g_refs = refs[n + 3:]
        x, y, c, _, _ = _place()
        for w in range(n):
            cp = pltpu.make_async_remote_copy(src_ref=landed(g_refs[w]), dst_ref=landed(g_refs[w]),
                                              send_sem=send_ref.at[w], recv_sem=recv_ref.at[w],
                                              device_id=(x, y, 1 - c), device_id_type=MESH)
            cp.wait_send()
            cp.wait_recv()

    return pl.pallas_call(
        body, name=name,
        in_specs=[HBM_SPEC] * n + [SEM_SPEC, SEM_SPEC, ANY_SPEC],
        out_specs=[HBM_SPEC] * n,
        out_shape=[pltpu.HBM(s.shape, s.dtype) for s in flying],
        input_output_aliases={w: w for w in range(n)},
        compiler_params=pltpu.CompilerParams(has_side_effects=DATAFLOW),
    )(*flying, send, recv, after)


def _send_start(name, srcs, land_shapes, issue, after):
    n = len(srcs)
    deps = [] if after is None else [after]
    nd = len(deps)

    def body(*refs):
        base = 2 * n + nd
        issue(refs[base + 2:base + 2 + n], refs[base + 2 + n:base + 2 + 2 * n], refs[base], refs[base + 1])
        token = refs[base + 2 + 2 * n]
        token[...] = jnp.zeros_like(token)

    lands = [_hbm(lax.empty(shape, s.dtype)) for shape, s in zip(land_shapes, srcs)]
    outs = pl.pallas_call(
        body, name=name,
        in_specs=[HBM_SPEC] * (2 * n) + [ANY_SPEC] * nd,
        out_specs=(SEM_SPEC, SEM_SPEC, *([HBM_SPEC] * (2 * n)), pl.BlockSpec(memory_space=pltpu.VMEM)),
        out_shape=(pltpu.SemaphoreType.DMA((n,)), pltpu.SemaphoreType.DMA((n,)),
                   *[pltpu.HBM(s.shape, s.dtype) for s in srcs], *[pltpu.HBM(l.shape, l.dtype) for l in lands],
                   jax.ShapeDtypeStruct((8, 128), F32)),
        input_output_aliases={w: w + 2 for w in range(2 * n)},
        compiler_params=pltpu.CompilerParams(has_side_effects=DATAFLOW),
    )(*[_hbm(s) for s in srcs], *lands, *deps)
    return outs[0], outs[1], list(outs[2:2 + n]), list(outs[2 + n:2 + 2 * n]), outs[2 + 2 * n]


def _send_wait(name, send, recv, srcs, lands, after, landed):
    n = len(srcs)

    def body(*refs):
        send_ref, recv_ref = refs[2 * n], refs[2 * n + 1]
        q_refs = refs[3 * n + 3:]
        x, y, c, _, _ = _place()
        for w in range(n):
            cp = pltpu.make_async_remote_copy(src_ref=landed(q_refs[w]), dst_ref=landed(q_refs[w]), send_sem=send_ref.at[w],
                                              recv_sem=recv_ref.at[w], device_id=(x, y, 1 - c), device_id_type=MESH)
            cp.wait_send()
            cp.wait_recv()

    outs = pl.pallas_call(
        body, name=name,
        in_specs=[HBM_SPEC] * (2 * n) + [SEM_SPEC, SEM_SPEC, ANY_SPEC],
        out_specs=[HBM_SPEC] * (2 * n),
        out_shape=[pltpu.HBM(a.shape, a.dtype) for a in list(srcs) + list(lands)],
        input_output_aliases={w: w for w in range(2 * n)},
        compiler_params=pltpu.CompilerParams(has_side_effects=DATAFLOW),
    )(*srcs, *lands, send, recv, after)
    return list(outs[:n]), list(outs[n:])


def exchange_start(name, grads, after):
    def issue(d_refs, r_refs, send, recv):
        x, y, c, _, _ = _place()
        for w in range(len(grads)):
            half = grads[w].shape[1] // 2
            pltpu.make_async_remote_copy(
                src_ref=d_refs[w].at[pl.ds(0, N_CHIPS), pl.ds((1 - c) * half, half)], dst_ref=r_refs[w],
                send_sem=send.at[w], recv_sem=recv.at[w], device_id=(x, y, 1 - c), device_id_type=MESH).start()

    return _send_start(name, grads, [(N_CHIPS, g.shape[1] // 2, g.shape[2]) for g in grads], issue, after)


def exchange_wait(name, send, recv, grads, lands, after):
    return _send_wait(name, send, recv, grads, lands, after, lambda r: r)


def scatter_start(name, parts):
    def issue(p_refs, q_refs, send, recv):
        x, y, c, jme, others = _place()
        for w in range(len(parts)):
            for (px, py) in others:
                pltpu.make_async_remote_copy(
                    src_ref=p_refs[w].at[2 * px + py], dst_ref=q_refs[w].at[jme], send_sem=send.at[w], recv_sem=recv.at[w],
                    device_id=(px, py, c), device_id_type=MESH).start()

    return _send_start(name, parts, [p.shape for p in parts], issue, None)


def scatter_wait(name, send, recv, parts, lands, after):
    return _send_wait(name, send, recv, parts, lands, after, lambda r: r.at[pl.ds(0, 3)])


def allreduce_small(v):
    R = v.shape[0]

    def body(v_ref, o_ref, sib_ref, mine_ref, all_ref, d_send, d_recv, i_send, i_recv):
        x, y, c, jme, others = _place()
        swap = pltpu.make_async_remote_copy(src_ref=v_ref, dst_ref=sib_ref, send_sem=d_send, recv_sem=d_recv,
                                            device_id=(x, y, 1 - c), device_id_type=MESH)
        swap.start()
        swap.wait()
        mine_ref[...] = v_ref[...] + sib_ref[...]
        for (px, py) in others:
            pltpu.make_async_remote_copy(src_ref=mine_ref, dst_ref=all_ref.at[jme], send_sem=i_send, recv_sem=i_recv,
                                         device_id=(px, py, c), device_id_type=MESH).start()
        three = all_ref.at[pl.ds(0, 3)]
        wait3 = pltpu.make_async_remote_copy(src_ref=three, dst_ref=three, send_sem=i_send, recv_sem=i_recv,
                                             device_id=(x, y, c), device_id_type=MESH)
        wait3.wait_recv()
        wait3.wait_send()
        all_ref[jme] = mine_ref[...]
        o_ref[...] = ((all_ref[0] + all_ref[1]) + all_ref[2]) + all_ref[3]

    vm = pl.BlockSpec(memory_space=pltpu.VMEM)
    return pl.pallas_call(
        body, name="allreduce_small", in_specs=[vm], out_specs=vm,
        out_shape=jax.ShapeDtypeStruct(v.shape, F32),
        scratch_shapes=[pltpu.VMEM((R, 128), F32), pltpu.VMEM((R, 128), F32), pltpu.VMEM((N_CHIPS, R, 128), F32),
                        pltpu.SemaphoreType.DMA, pltpu.SemaphoreType.DMA, pltpu.SemaphoreType.DMA, pltpu.SemaphoreType.DMA],
        compiler_params=pltpu.CompilerParams(vmem_limit_bytes=VMEM_LIMIT_BYTES),
    )(v)


def _row_tile(rows, width, mult=16, cap=3 << 20):
    best = rows
    for t in range(mult, rows + 1, mult):
        if rows % t == 0 and t * width * 4 <= cap:
            best = t
    return best


def add_sibling(name, mine, theirs, place):
    _, kh, ns = theirs.shape
    tr = _row_tile(kh, ns)
    nblk = kh // tr

    def body(place_ref, a_ref, b_ref, o_ref):
        o_ref[...] = (a_ref[...].astype(F32) + b_ref[...].astype(F32)).astype(BF16)

    return pl.pallas_call(
        body, name=name,
        grid_spec=pltpu.PrefetchScalarGridSpec(
            num_scalar_prefetch=1, grid=(N_CHIPS, nblk),
            in_specs=[pl.BlockSpec((None, tr, ns), lambda j, i, s: (j, s[1] * nblk + i, 0)),
                      pl.BlockSpec((None, tr, ns), lambda j, i, s: (j, i, 0))],
            out_specs=pl.BlockSpec((None, tr, ns), lambda j, i, s: (j, i, 0))),
        out_shape=jax.ShapeDtypeStruct(theirs.shape, BF16),
        compiler_params=_params(("parallel", "parallel")),
    )(place, mine, theirs)


def add_chips(name, q, p, place):
    _, kh, ns = q.shape
    tr = _row_tile(kh, ns)
    nblk = kh // tr

    def body(place_ref, *refs):
        q_refs, own_ref, o_ref = refs[:N_CHIPS], refs[N_CHIPS], refs[-1]
        jme = place_ref[0]
        tot = None
        for j in range(N_CHIPS):
            v = jnp.where(jme == j, own_ref[...], q_refs[j][...]).astype(F32)
            tot = v if tot is None else tot + v
        o_ref[...] = tot

    def q_ix(j):
        return lambda i, s: (jnp.where(s[0] == j, (j + 1) % N_CHIPS, j), i, 0)

    in_specs = [pl.BlockSpec((None, tr, ns), q_ix(j)) for j in range(N_CHIPS)]
    in_specs.append(pl.BlockSpec((None, tr, ns), lambda i, s: (s[0], i, 0)))
    return pl.pallas_call(
        body, name=name,
        grid_spec=pltpu.PrefetchScalarGridSpec(
            num_scalar_prefetch=1, grid=(nblk,), in_specs=in_specs,
            out_specs=pl.BlockSpec((tr, ns), lambda i, s: (s[1] * nblk + i, 0))),
        out_shape=jax.ShapeDtypeStruct((2 * kh, ns), F32),
        compiler_params=_params(("parallel",)),
    )(place, q, q, q, q, p)


ADAM_LR, ADAM_B1, ADAM_B2, ADAM_EPS, ADAM_WD, ADAM_STEP = 0.001, 0.9, 0.999, 1e-08, 0.01, 10


def adamw(name, w, g, m, v, layer, prev=None, after=None):
    _, k, ns = w.shape
    nsp = g.shape[1]
    tr = _row_tile(k, nsp, mult=8, cap=3 << 20)

    def body(w_ref, g_ref, m_ref, v_ref, *rest):
        go_ref, d_ref, mo_ref, vo_ref = rest[-4:]
        gv = g_ref[:, :ns] if nsp != ns else g_ref[...]
        mn = ADAM_B1 * m_ref[...] + (1.0 - ADAM_B1) * gv
        vn = ADAM_B2 * v_ref[...] + (1.0 - ADAM_B2) * (gv * gv)
        m_hat = mn / (1.0 - ADAM_B1 ** ADAM_STEP)
        v_hat = vn / (1.0 - ADAM_B2 ** ADAM_STEP)
        go_ref[...] = gv
        d_ref[...] = -ADAM_LR * (m_hat / (jnp.sqrt(v_hat) + ADAM_EPS) + ADAM_WD * w_ref[...])
        mo_ref[...] = mn
        vo_ref[...] = vn

    blk = pl.BlockSpec((None, tr, ns), lambda i: (layer, i, 0))
    gblk = pl.BlockSpec((tr, nsp), lambda i: (i, 0))
    args, in_specs, aliases = [w, g, m, v], [blk, gblk, blk, blk], {}
    if prev is not None:
        args += list(prev)
        in_specs += [pl.BlockSpec(memory_space=pl.ANY)] * 4
        aliases = {4 + i: i for i in range(4)}
    if after is not None:
        args.append(after)
        in_specs.append(pl.BlockSpec(memory_space=pl.ANY))
    return pl.pallas_call(
        body, name=name, grid=(k // tr,), in_specs=in_specs, out_specs=[blk] * 4,
        out_shape=[jax.ShapeDtypeStruct(w.shape, F32)] * 4, input_output_aliases=aliases,
        compiler_params=_params(("parallel",)),
    )(*args)


WEIGHTS = ("ffn_a_norm", "ffn_a_w1", "ffn_a_w3", "ffn_a_w2", "mix_norm", "w_in", "q_a_norm", "w_uq", "kv_a_norm", "w_ukv",
           "q_norm", "k_norm", "gm_v_norm", "gm_ws", "gm_bs", "attn_out_norm", "gm_out_norm", "w_out", "ffn_b_norm",
           "ffn_b_w1", "ffn_b_w3", "ffn_b_w2", "ple_gate_norm", "w_ple_gate", "w_ple", "ple_norm")
_FF = FF_PAD - FF_SHARD
BIG = {"ffn_a_w1": _FF, "ffn_a_w3": _FF, "ffn_a_w2": _FF, "ffn_b_w1": _FF, "ffn_b_w3": _FF, "ffn_b_w2": _FF,
       "w_in": IN_SHARD_PAD - IN_SHARD, "w_uq": 0, "w_ukv": 0, "w_ple": 0, "w_out": 0, "w_ple_gate": 0}
TRANSPOSED = ("ffn_a_w1", "ffn_a_w3", "ffn_b_w1", "ffn_b_w3", "w_in")
SMALL = tuple(n for n in WEIGHTS if n not in BIG)
PACK = 1024


def _pack_small(d):
    parts = []
    for n in SMALL:
        flat = d[n].reshape(-1)
        parts.append(jnp.pad(flat, (0, (-flat.shape[0]) % PACK)))
    return jnp.concatenate(parts).reshape(-1, 128)


def _unpack_small(buf, like):
    flat = buf.reshape(-1)
    out, pos = {}, 0
    for n in SMALL:
        size = math.prod(like[n].shape)
        out[n] = flat[pos:pos + size].reshape(like[n].shape)
        pos += size + (-size) % PACK
    return out


def kernel(*args):
    names = (("x", "p", "positions") + WEIGHTS + ("loss_target",) + tuple("m_" + n for n in WEIGHTS)
             + tuple("v_" + n for n in WEIGHTS))
    a = dict(zip(names, args, strict=True))
    x, p, positions, target = a["x"][0], a["p"][:, 0], a["positions"][0], a["loss_target"][0]
    for n in TRANSPOSED:
        for pre in ("", "m_", "v_"):
            a[pre + n] = jnp.swapaxes(a[pre + n], 1, 2)

    place = jnp.stack([2 * lax.axis_index("x") + lax.axis_index("y"), lax.axis_index("c")]).astype(jnp.int32)
    small = {n: a[n] for n in SMALL}
    tabs = rope_tables(positions)
    order = {"l0a": ("ffn_a_w1", "ffn_a_w3"), "l0b": ("ffn_a_w2",), "l0c": ("w_in", "w_uq", "w_ukv", "w_out"),
             "l0d": ("ffn_b_w1", "ffn_b_w3", "ffn_b_w2", "w_ple_gate", "w_ple")}
    prep = lambda n, l, after: prep_shard(f"prep_{n}_{l}", a[n], l, BIG[n], place, after)
    flights, token = {}, None
    for tag, names in order.items():
        flights[tag] = gather_start(f"gather_{tag}_start", [prep(n, 0, token) for n in names], None)
        token = flights[tag][3]
    slots1 = []
    for n in BIG:
        slots1.append(prep(n, 1, slots1[-1] if slots1 else token))

    def arrive(tag, after):
        send, recv, flying, _ = flights[tag]
        arrived = gather_wait(f"gather_{tag}_wait", send, recv, flying, after)
        send, recv, flying, token = forward_start(f"forward_{tag}_start", arrived)
        return dict(zip(order[tag], gather_wait(f"forward_{tag}_wait", send, recv, flying, token)))

    G0 = arrive("l0a", slots1[-1])

    def before_down(s):
        G0.update(arrive("l0b", s))
        return G0["ffn_a_w2"]

    def before_mixer(h1, Gl, small_):
        G0.update(arrive("l0c", h1))
        flights["l1"] = gather_start("gather_l1_start", slots1, G0["w_uq"])
        return G0, {**small_, "mix_norm": small_["mix_norm"] + flights["l1"][3][0, 0]}

    def before_ffn_b(h2, Gl):
        G0.update(arrive("l0d", h2))
        return G0

    def before_ple(h3):
        send, recv, flying, _ = flights["l1"]
        flights["f1"] = forward_start("forward_l1_start", gather_wait("gather_l1_wait", send, recv, flying, h3))
        return flights["f1"][3]

    h, saved0 = layer_fwd(0, x, p[0], G0, small, tabs,
                          {"down_a": before_down, "mixer": before_mixer, "ffn_b": before_ffn_b, "ple": before_ple})
    G1 = dict(zip(BIG, gather_wait("forward_l1_wait", *flights["f1"][:3], h)))
    h, saved1 = layer_fwd(1, h, p[1], G1, small, tabs)
    dh, loss = loss_grad(h, target)
    loss = lax.psum(loss, ("x", "y", "c"))

    groups = {"l1": tuple(BIG),
              "l0a": ("w_ple_gate", "w_ple", "ffn_b_w1", "ffn_b_w3", "ffn_b_w2"),
              "l0b": ("w_in", "w_uq", "w_ukv", "w_out"),
              "l0c": ("ffn_a_w1", "ffn_a_w3", "ffn_a_w2")}
    crossing, started = [], {}

    def begin(tag, gl, after):
        ex = exchange_start(f"exchange_{tag}_start", [gl[n] for n in groups[tag]], after)
        crossing.append((tag, ex))
        return ex[4]

    def advance(after):
        tag, (send, recv, mine, lands, _) = crossing.pop()
        mine, theirs = exchange_wait(f"exchange_{tag}_wait", send, recv, mine, lands, after)
        parts = [add_sibling(f"add_sibling_{n}_{tag}", d, r, place) for n, d, r in zip(groups[tag], mine, theirs)]
        started[tag] = scatter_start(f"scatter_{tag}_start", parts)
        return started[tag][4]

    def sum_chips(tag, after):
        send, recv, parts, lands, _ = started[tag]
        parts, slabs = scatter_wait(f"scatter_{tag}_wait", send, recv, parts, lands, after)
        halves = [add_chips(f"add_chips_{n}_{tag}", q, pt, place) for n, q, pt in zip(groups[tag], slabs, parts)]
        return share_start(f"share_{tag}_start", halves)

    def shared(tag, sharing, after):
        send, recv, flying, _ = sharing
        return dict(zip(groups[tag], share_wait(f"share_{tag}_wait", send, recv, flying, after)))

    def update(names, full, layer, prev, after):
        outs = {}
        for n in names:
            outs[n] = adamw(f"adamw_{n}_{layer}", a[n], full[n], a["m_" + n], a["v_" + n], layer, prev and prev[n], after)
            after = outs[n][1]
        return outs, after

    grads = [None, None]
    dh, grads[1] = layer_bwd(1, dh, p[1], G1, small, tabs, saved1)
    token = begin("l1", grads[1], None)
    w0 = {**saved0[0], "ple_gate_norm": saved0[0]["ple_gate_norm"] + token[0, 0]}
    hooks = {"ffn_b": lambda gl, dh_: advance(dh_),
             "mixer": lambda gl, dh_: begin("l0a", gl, None),
             "ffn_a": lambda gl, dh_: begin("l0b", gl, advance(dh_)),
             "ffn_a_dw": lambda gl, dh_: advance(dh_)}
    gx, grads[0] = layer_bwd(0, dh, p[0], G0, small, tabs, (w0,) + saved0[1:], hooks)
    token = begin("l0c", grads[0], None)
    sharing = sum_chips("l1", token)
    full1 = shared("l1", sharing, advance(sharing[3]))
    outs1, behind = update(BIG, full1, 1, None, None)
    sharing_a = sum_chips("l0a", behind)
    sharing_b = sum_chips("l0b", sharing_a[3])
    full0 = shared("l0a", sharing_a, sharing_b[3])
    outs0, behind = update(groups["l0a"], full0, 0, outs1, None)
    sharing_c = sum_chips("l0c", behind)
    full0.update(shared("l0b", sharing_b, sharing_c[3]))
    outs, behind = update(groups["l0b"], full0, 0, outs1, None)
    outs0.update(outs)
    full0.update(shared("l0c", sharing_c, behind))
    outs0.update(update(groups["l0c"], full0, 0, outs1, None)[0])

    out_g, out_d, out_m, out_v = {}, {}, {}, {}
    for n in BIG:
        outs = [jnp.swapaxes(o, 1, 2) for o in outs0[n]] if n in TRANSPOSED else outs0[n]
        out_g[n], out_d[n], out_m[n], out_v[n] = outs

    gs = allreduce_small(_pack_small({n: jnp.stack([grads[0][n], grads[1][n]]) for n in SMALL}))
    rows = gs.shape[0] // 2
    packed = [_pack_small(d).reshape(2, rows, 128) for d in
              (small, {n: a["m_" + n] for n in SMALL}, {n: a["v_" + n] for n in SMALL})]
    gs = gs.reshape(2, rows, 128)
    sm = adamw("adamw_small_0", packed[0], gs[0], packed[1], packed[2], 0)
    sm = adamw("adamw_small_1", packed[0], gs[1], packed[1], packed[2], 1, sm)
    for dst, buf in zip((out_g, out_d, out_m, out_v), sm):
        dst.update(_unpack_small(buf, small))

    return (loss, gx[None], *[out_g[n] for n in WEIGHTS], *[out_d[n] for n in WEIGHTS],
            *[out_m[n] for n in WEIGHTS], *[out_v[n] for n in WEIGHTS])
```
